```python
import jax
import jax.numpy as jnp
from jax import lax
import numpy as np

D_MODEL = 2048
BATCH = 8
SEQ = 8192
DEPTH = 1

CTX_LEN = 256
GRID_W = 64
WIN_H = 8
WIN_W = 16
NA_HEADS = 16
NA_HEAD_DIM = 64
NA_WIDTH = NA_HEADS * NA_HEAD_DIM
HG_HEADS = 8
HG_KEY_DIM = 128
HG_VAL_DIM = 128
HG_FDIM = HG_HEADS * HG_KEY_DIM
HG_WIDTH = HG_HEADS * HG_VAL_DIM
HG_CHUNK = 64
FFN_HIDDEN = -(-8 * D_MODEL // (3 * 256)) * 256
N_MOD = 6
EPS = 1e-6
IN_SPLIT = (NA_WIDTH, NA_WIDTH, NA_WIDTH, HG_FDIM, HG_FDIM, HG_FDIM, HG_WIDTH, HG_WIDTH, D_MODEL, D_MODEL)
IN_WIDTH = sum(IN_SPLIT)

kernel_name = 'hybrid_natten_hgrn2_dit_layer'


def rmsnorm(x, g):
    xf = x.astype(jnp.float32)
    y = xf * lax.rsqrt(jnp.mean(xf * xf, axis=-1, keepdims=True) + EPS)
    return (y * g.astype(jnp.float32)).astype(x.dtype)


def modulate(h, shift, scale):
    return h * (1 + scale) + shift


def split_columns(u):
    cuts, acc = [], 0
    for w in IN_SPLIT[:-1]:
        acc += w
        cuts.append(acc)
    return jnp.split(u, cuts, axis=-1)


def to_heads(a, n_heads):
    return a.reshape(a.shape[0], a.shape[1], n_heads, -1)


def neighbourhood_attention(q, k, v, k_ctx, v_ctx, rpb):
    b, s, h, dh = q.shape
    rows = s // GRID_W
    kh = min(WIN_H, rows)
    scale = dh ** -0.5
    qg = q.reshape(b, rows, GRID_W, h, dh)
    kg = k.reshape(b, rows, GRID_W, h, dh)
    vg = v.reshape(b, rows, GRID_W, h, dh)
    col = jnp.arange(GRID_W)
    col_start = jnp.clip(col - WIN_W // 2, 0, GRID_W - WIN_W)
    in_win = (col[None, :] >= col_start[:, None]) & (col[None, :] < col_start[:, None] + WIN_W)
    band_mask = jnp.broadcast_to(in_win[:, None, :], (GRID_W, kh, GRID_W)).reshape(GRID_W, kh * GRID_W)
    dc_idx = jnp.clip(col[None, :] - col[:, None], 1 - WIN_W, WIN_W - 1) + WIN_W - 1
    row_start = jnp.clip(jnp.arange(rows) - kh // 2, 0, rows - kh)
    nb = kh * GRID_W

    def one_row(r):
        rs = row_start[r]
        q_r = lax.dynamic_index_in_dim(qg, r, axis=1, keepdims=False)
        k_band = lax.dynamic_slice_in_dim(kg, rs, kh, axis=1).reshape(b, nb, h, dh)
        v_band = lax.dynamic_slice_in_dim(vg, rs, kh, axis=1).reshape(b, nb, h, dh)
        dr_idx = rs + jnp.arange(kh) - r + WIN_H - 1
        bias = rpb[:, dr_idx][:, :, dc_idx]
        bias = bias.transpose(0, 2, 1, 3).reshape(h, GRID_W, nb).astype(jnp.float32)
        s_band = jnp.einsum('bqhd,bnhd->bhqn', q_r, k_band, preferred_element_type=jnp.float32) * scale + bias
        s_band = jnp.where(band_mask, s_band, -jnp.inf)
        s_ctx = jnp.einsum('bqhd,bmhd->bhqm', q_r, k_ctx, preferred_element_type=jnp.float32) * scale
        p = jax.nn.softmax(jnp.concatenate([s_band, s_ctx], axis=-1), axis=-1).astype(v.dtype)
        return (jnp.einsum('bhqn,bnhd->bqhd', p[..., :nb], v_band)
                + jnp.einsum('bhqm,bmhd->bqhd', p[..., nb:], v_ctx))

    out = lax.map(one_row, jnp.arange(rows))
    return out.transpose(1, 0, 2, 3, 4).reshape(b, s, h * dh)


def context_attention(q, k, v):
    b, l, h, dh = q.shape
    s = jnp.einsum('blhd,bmhd->bhlm', q, k, preferred_element_type=jnp.float32) * dh ** -0.5
    p = jax.nn.softmax(s, axis=-1).astype(v.dtype)
    return jnp.einsum('bhlm,bmhd->blhd', p, v).reshape(b, l, h * dh)


def hgrn2_gates(f_logits, lb):
    f = lb + (1 - lb) * jax.nn.sigmoid(f_logits.astype(jnp.float32))
    return jnp.log(f), 1 - f


def hgrn2_chunk_scan(q, k, log_f, v, state0):
    b, t, h, dk = q.shape
    dv = v.shape[-1]
    n = t // HG_CHUNK

    def chunks(a):
        return a.astype(jnp.float32).reshape(b, n, HG_CHUNK, h, a.shape[-1]).transpose(1, 0, 3, 2, 4)

    tri = jnp.tril(jnp.ones((HG_CHUNK, HG_CHUNK), dtype=bool))[:, :, None]

    def step(state, inp):
        q_c, k_c, lf_c, v_c = inp
        cum = jnp.cumsum(lf_c, axis=2)
        rel = jnp.exp(jnp.where(tri, cum[:, :, :, None, :] - cum[:, :, None, :, :], -jnp.inf))
        attn = jnp.einsum('bhtk,bhsk,bhtsk->bhts', q_c, k_c, rel)
        out = (jnp.einsum('bhts,bhsv->bhtv', attn, v_c)
               + jnp.einsum('bhtk,bhkv->bhtv', q_c * jnp.exp(cum), state))
        last = cum[:, :, -1:, :]
        new_state = (jnp.exp(last[:, :, 0, :])[..., None] * state
                     + jnp.einsum('bhsk,bhsv->bhkv', k_c * jnp.exp(last - cum), v_c))
        return new_state, out

    final, out = lax.scan(step, state0, (chunks(q), chunks(k), chunks(log_f), chunks(v)))
    return out.transpose(1, 0, 3, 2, 4).reshape(b, t, h, dv), final


def hgrn2_bidirectional(q, f_fwd, f_bwd, i, lb_fwd, lb_bwd, state_fwd, state_bwd):
    lf1, k1 = hgrn2_gates(f_fwd, lb_fwd)
    lf2, k2 = hgrn2_gates(f_bwd, lb_bwd)
    o1, s1 = hgrn2_chunk_scan(q, k1, lf1, i, state_fwd)
    o2, s2 = hgrn2_chunk_scan(jnp.flip(q, 1), jnp.flip(k2, 1), jnp.flip(lf2, 1), jnp.flip(i, 1), state_bwd)
    return o1 + jnp.flip(o2, 1), s1, s2


def hgrn2_readout(o, out_gate, g):
    on = o * lax.rsqrt(jnp.mean(o * o, axis=-1, keepdims=True) + EPS) * g.astype(jnp.float32)
    on = on.reshape(o.shape[0], o.shape[1], -1)
    return (on * jax.nn.silu(out_gate.astype(jnp.float32))).astype(out_gate.dtype)


def gated_merge(o_a, o_b, gate_a, gate_b, w_pa, w_pb, w_out):
    y = jax.nn.sigmoid(gate_a) * (o_a @ w_pa) + jax.nn.sigmoid(gate_b) * (o_b @ w_pb)
    return y @ w_out


def swiglu(h, w_in, w_out):
    a, u = jnp.split(h @ w_in, 2, axis=-1)
    return (jax.nn.silu(a) * u) @ w_out


def token_mixing(h, hc, w_in, rpb, lb_f, lb_b, hg_g, w_pa, w_pb, w_out, with_ctx_out):
    q, k, v, hq, hf_f, hf_b, hi, hog, ga, gb = split_columns(h @ w_in)
    qc, kc, vc, hqc, hfc_f, hfc_b, hic, hogc, gac, gbc = split_columns(hc @ w_in)
    b = hc.shape[0]
    kc_h, vc_h = to_heads(kc, NA_HEADS), to_heads(vc, NA_HEADS)
    o_na = neighbourhood_attention(to_heads(q, NA_HEADS), to_heads(k, NA_HEADS), to_heads(v, NA_HEADS),
                                   kc_h, vc_h, rpb)
    zero = jnp.zeros((b, HG_HEADS, HG_KEY_DIM, HG_VAL_DIM), jnp.float32)
    oc_hg, s_f, s_b = hgrn2_bidirectional(to_heads(hqc, HG_HEADS), to_heads(hfc_f, HG_HEADS),
                                          to_heads(hfc_b, HG_HEADS), to_heads(hic, HG_HEADS),
                                          lb_f, lb_b, zero, zero)
    o_hg, _, _ = hgrn2_bidirectional(to_heads(hq, HG_HEADS), to_heads(hf_f, HG_HEADS),
                                     to_heads(hf_b, HG_HEADS), to_heads(hi, HG_HEADS),
                                     lb_f, lb_b, s_f, s_b)
    y = gated_merge(o_na, hgrn2_readout(o_hg, hog, hg_g), ga, gb, w_pa, w_pb, w_out)
    yc = None
    if with_ctx_out:
        oc_na = context_attention(to_heads(qc, NA_HEADS), kc_h, vc_h)
        yc = gated_merge(oc_na, hgrn2_readout(oc_hg, hogc, hg_g), gac, gbc, w_pa, w_pb, w_out)
    return y, yc


def _fwd_setup_inputs(seed: int = 0) -> dict:
    key = jax.random.key(seed)
    ks = jax.random.split(key, 20)
    f32 = jnp.float32

    def nrm(k, shape, scale):
        return jax.random.normal(k, shape, f32) * scale

    return {
        'x': nrm(ks[0], (BATCH, SEQ, D_MODEL), 1.0),
        'c': nrm(ks[1], (BATCH, D_MODEL), 1.0),
        'ctx': nrm(ks[2], (BATCH, CTX_LEN, D_MODEL), 1.0),
        'c_ctx': nrm(ks[3], (D_MODEL,), 1.0),
        'w_ada': nrm(ks[4], (DEPTH, D_MODEL, N_MOD * D_MODEL), 0.5 * D_MODEL ** -0.5),
        'b_ada': nrm(ks[5], (DEPTH, N_MOD * D_MODEL), 0.01),
        'norm1_g': 1.0 + nrm(ks[6], (DEPTH, D_MODEL), 0.02),
        'w_in': nrm(ks[7], (DEPTH, D_MODEL, IN_WIDTH), D_MODEL ** -0.5),
        'na_rpb': nrm(ks[8], (DEPTH, NA_HEADS, 2 * WIN_H - 1, 2 * WIN_W - 1), 0.1),
        'hg_lb_logits': nrm(ks[9], (DEPTH + 1, 2, HG_FDIM), 1.0),
        'hg_norm_g': 1.0 + nrm(ks[10], (DEPTH, HG_VAL_DIM), 0.02),
        'w_pa': nrm(ks[11], (DEPTH, NA_WIDTH, D_MODEL), NA_WIDTH ** -0.5),
        'w_pb': nrm(ks[12], (DEPTH, HG_WIDTH, D_MODEL), HG_WIDTH ** -0.5),
        'w_out': nrm(ks[13], (DEPTH, D_MODEL, D_MODEL), D_MODEL ** -0.5),
        'norm2_g': 1.0 + nrm(ks[14], (DEPTH, D_MODEL), 0.02),
        'w_ffn_in': nrm(ks[15], (DEPTH, D_MODEL, 2 * FFN_HIDDEN), D_MODEL ** -0.5),
        'w_ffn_out': nrm(ks[16], (DEPTH, FFN_HIDDEN, D_MODEL), FFN_HIDDEN ** -0.5),
        'final_g': 1.0 + nrm(ks[17], (D_MODEL,), 0.02),
    }


def _fwd_reference(x, c, ctx, c_ctx, w_ada, b_ada, norm1_g, w_in, na_rpb, hg_lb_logits, hg_norm_g,
              w_pa, w_pb, w_out, norm2_g, w_ffn_in, w_ffn_out, final_g):
    b = x.shape[0]
    lb_table = jnp.cumsum(jax.nn.softmax(hg_lb_logits.astype(jnp.float32), axis=0), axis=0)
    silu_c = jax.nn.silu(c)
    silu_cc = jax.nn.silu(c_ctx)
    xc = ctx
    for l in range(DEPTH):
        last = l == DEPTH - 1
        mod = (silu_c @ w_ada[l] + b_ada[l]).reshape(b, N_MOD, 1, D_MODEL)
        mod_c = (silu_cc @ w_ada[l] + b_ada[l]).reshape(N_MOD, 1, D_MODEL)
        sh1, sc1, g1, sh2, sc2, g2 = [mod[:, j] for j in range(N_MOD)]
        sh1c, sc1c, g1c, sh2c, sc2c, g2c = [mod_c[j] for j in range(N_MOD)]
        lb_f = lb_table[l, 0].reshape(HG_HEADS, HG_KEY_DIM)
        lb_b = lb_table[l, 1].reshape(HG_HEADS, HG_KEY_DIM)
        h = modulate(rmsnorm(x, norm1_g[l]), sh1, sc1)
        hc = modulate(rmsnorm(xc, norm1_g[l]), sh1c, sc1c)
        y, yc = token_mixing(h, hc, w_in[l], na_rpb[l], lb_f, lb_b, hg_norm_g[l],
                             w_pa[l], w_pb[l], w_out[l], not last)
        x = x + g1 * y
        x = x + g2 * swiglu(modulate(rmsnorm(x, norm2_g[l]), sh2, sc2), w_ffn_in[l], w_ffn_out[l])
        if not last:
            xc = xc + g1c * yc
            xc = xc + g2c * swiglu(modulate(rmsnorm(xc, norm2_g[l]), sh2c, sc2c), w_ffn_in[l], w_ffn_out[l])
    return rmsnorm(x, final_g)


import jax as _jax
import jax.numpy as _jnp

TWIN_FORMAT = 'train_step'
FWD_PARAMS = ['x', 'c', 'ctx', 'c_ctx', 'w_ada', 'b_ada', 'norm1_g', 'w_in', 'na_rpb', 'hg_lb_logits', 'hg_norm_g', 'w_pa', 'w_pb', 'w_out', 'norm2_g', 'w_ffn_in', 'w_ffn_out', 'final_g']
TWIN_WEIGHTS = ['c_ctx', 'w_ada', 'b_ada', 'norm1_g', 'w_in', 'na_rpb', 'hg_lb_logits', 'hg_norm_g', 'w_pa', 'w_pb', 'w_out', 'norm2_g', 'w_ffn_in', 'w_ffn_out', 'final_g']
TWIN_DIFF_INPUT = 'x'
TWIN_INPUTS = ['x', 'c', 'ctx', 'c_ctx', 'w_ada', 'b_ada', 'norm1_g', 'w_in', 'na_rpb', 'hg_lb_logits', 'hg_norm_g', 'w_pa', 'w_pb', 'w_out', 'norm2_g', 'w_ffn_in', 'w_ffn_out', 'final_g', 'loss_target', 'm_c_ctx', 'm_w_ada', 'm_b_ada', 'm_norm1_g', 'm_w_in', 'm_na_rpb', 'm_hg_lb_logits', 'm_hg_norm_g', 'm_w_pa', 'm_w_pb', 'm_w_out', 'm_norm2_g', 'm_w_ffn_in', 'm_w_ffn_out', 'm_final_g', 'v_c_ctx', 'v_w_ada', 'v_b_ada', 'v_norm1_g', 'v_w_in', 'v_na_rpb', 'v_hg_lb_logits', 'v_hg_norm_g', 'v_w_pa', 'v_w_pb', 'v_w_out', 'v_norm2_g', 'v_w_ffn_in', 'v_w_ffn_out', 'v_final_g']
TWIN_OUTPUTS = ['loss', 'grad_x', 'grad_c_ctx', 'grad_w_ada', 'grad_b_ada', 'grad_norm1_g', 'grad_w_in', 'grad_na_rpb', 'grad_hg_lb_logits', 'grad_hg_norm_g', 'grad_w_pa', 'grad_w_pb', 'grad_w_out', 'grad_norm2_g', 'grad_w_ffn_in', 'grad_w_ffn_out', 'grad_final_g', 'delta_c_ctx', 'delta_w_ada', 'delta_b_ada', 'delta_norm1_g', 'delta_w_in', 'delta_na_rpb', 'delta_hg_lb_logits', 'delta_hg_norm_g', 'delta_w_pa', 'delta_w_pb', 'delta_w_out', 'delta_norm2_g', 'delta_w_ffn_in', 'delta_w_ffn_out', 'delta_final_g', 'new_m_c_ctx', 'new_m_w_ada', 'new_m_b_ada', 'new_m_norm1_g', 'new_m_w_in', 'new_m_na_rpb', 'new_m_hg_lb_logits', 'new_m_hg_norm_g', 'new_m_w_pa', 'new_m_w_pb', 'new_m_w_out', 'new_m_norm2_g', 'new_m_w_ffn_in', 'new_m_w_ffn_out', 'new_m_final_g', 'new_v_c_ctx', 'new_v_w_ada', 'new_v_b_ada', 'new_v_norm1_g', 'new_v_w_in', 'new_v_na_rpb', 'new_v_hg_lb_logits', 'new_v_hg_norm_g', 'new_v_w_pa', 'new_v_w_pb', 'new_v_w_out', 'new_v_norm2_g', 'new_v_w_ffn_in', 'new_v_w_ffn_out', 'new_v_final_g']
TWIN_LEAF_KINDS = {'loss': 'loss', 'grad_x': 'grad_x', 'grad_c_ctx': 'grad_w', 'grad_w_ada': 'grad_w', 'grad_b_ada': 'grad_w', 'grad_norm1_g': 'grad_w', 'grad_w_in': 'grad_w', 'grad_na_rpb': 'grad_w', 'grad_hg_lb_logits': 'grad_w', 'grad_hg_norm_g': 'grad_w', 'grad_w_pa': 'grad_w', 'grad_w_pb': 'grad_w', 'grad_w_out': 'grad_w', 'grad_norm2_g': 'grad_w', 'grad_w_ffn_in': 'grad_w', 'grad_w_ffn_out': 'grad_w', 'grad_final_g': 'grad_w', 'delta_c_ctx': 'delta_w', 'delta_w_ada': 'delta_w', 'delta_b_ada': 'delta_w', 'delta_norm1_g': 'delta_w', 'delta_w_in': 'delta_w', 'delta_na_rpb': 'delta_w', 'delta_hg_lb_logits': 'delta_w', 'delta_hg_norm_g': 'delta_w', 'delta_w_pa': 'delta_w', 'delta_w_pb': 'delta_w', 'delta_w_out': 'delta_w', 'delta_norm2_g': 'delta_w', 'delta_w_ffn_in': 'delta_w', 'delta_w_ffn_out': 'delta_w', 'delta_final_g': 'delta_w', 'new_m_c_ctx': 'new_m', 'new_m_w_ada': 'new_m', 'new_m_b_ada': 'new_m', 'new_m_norm1_g': 'new_m', 'new_m_w_in': 'new_m', 'new_m_na_rpb': 'new_m', 'new_m_hg_lb_logits': 'new_m', 'new_m_hg_norm_g': 'new_m', 'new_m_w_pa': 'new_m', 'new_m_w_pb': 'new_m', 'new_m_w_out': 'new_m', 'new_m_norm2_g': 'new_m', 'new_m_w_ffn_in': 'new_m', 'new_m_w_ffn_out': 'new_m', 'new_m_final_g': 'new_m', 'new_v_c_ctx': 'new_v', 'new_v_w_ada': 'new_v', 'new_v_b_ada': 'new_v', 'new_v_norm1_g': 'new_v', 'new_v_w_in': 'new_v', 'new_v_na_rpb': 'new_v', 'new_v_hg_lb_logits': 'new_v', 'new_v_hg_norm_g': 'new_v', 'new_v_w_pa': 'new_v', 'new_v_w_pb': 'new_v', 'new_v_w_out': 'new_v', 'new_v_norm2_g': 'new_v', 'new_v_w_ffn_in': 'new_v', 'new_v_w_ffn_out': 'new_v', 'new_v_final_g': 'new_v'}


def _forward(args):
    return _fwd_reference(*[args[k] for k in FWD_PARAMS])


def _output_shape():
    def fwd():
        inp = _fwd_setup_inputs(0)
        return _fwd_reference(*[inp[k] for k in FWD_PARAMS])
    out = _jax.eval_shape(fwd)
    return out.shape, out.dtype

N_MICROBATCH = 1
ADAM_LR = 0.001
ADAM_B1 = 0.9
ADAM_B2 = 0.999
ADAM_EPS = 1e-08
ADAM_WD = 0.01
ADAM_STEP = 10
PER_EXAMPLE_BATCH_AXIS = {'x': 0, 'c': 0, 'ctx': 0, 'loss_target': 0}
SHARED_INPUTS = []
_WEIGHT_DTYPES = {'c_ctx': _jnp.float32, 'w_ada': _jnp.float32, 'b_ada': _jnp.float32, 'norm1_g': _jnp.float32, 'w_in': _jnp.float32, 'na_rpb': _jnp.float32, 'hg_lb_logits': _jnp.float32, 'hg_norm_g': _jnp.float32, 'w_pa': _jnp.float32, 'w_pb': _jnp.float32, 'w_out': _jnp.float32, 'norm2_g': _jnp.float32, 'w_ffn_in': _jnp.float32, 'w_ffn_out': _jnp.float32, 'final_g': _jnp.float32}
MOMENT_SCALE = {'c_ctx': 5.077730e-03, 'w_ada': 3.031155e-02, 'b_ada': 5.285734e-02, 'norm1_g': 2.820600e-02, 'w_in': 1.259665e-02, 'na_rpb': 8.034070e-04, 'hg_lb_logits': 5.315767e-03, 'hg_norm_g': 5.773905e-02, 'w_pa': 5.415094e-03, 'w_pb': 1.348265e-02, 'w_out': 1.449605e-02, 'norm2_g': 3.622225e-02, 'w_ffn_in': 1.594963e-02, 'w_ffn_out': 2.592415e-02, 'final_g': 3.197226e+01}


def _to_microbatches(a, axis):
    t = _jnp.moveaxis(a, axis, 0)
    t = t.reshape((N_MICROBATCH, t.shape[0] // N_MICROBATCH) + t.shape[1:])
    return _jnp.moveaxis(t, 1, axis + 1)


def setup_inputs(seed: int = 0) -> dict:
    inp = _fwd_setup_inputs(seed)
    key = _jax.random.fold_in(_jax.random.key(seed), 7919)
    shape, _ = _output_shape()
    out = dict(inp)
    out["loss_target"] = _jax.random.normal(_jax.random.fold_in(key, 0), shape, _jnp.float32)
    for i, name in enumerate(TWIN_WEIGHTS):
        w = inp[name].astype(_jnp.float32)
        if MOMENT_SCALE is None:
            s = _jnp.sqrt(_jnp.mean(_jnp.square(w)) + 1e-30)
        else:
            s = MOMENT_SCALE[name]
        km, kv = _jax.random.split(_jax.random.fold_in(key, i + 1))
        out[name] = w
        out["m_" + name] = s * _jax.random.normal(km, w.shape, _jnp.float32)
        out["v_" + name] = (s * s) * _jax.random.uniform(kv, w.shape, _jnp.float32, 0.5, 1.5)
    if N_MICROBATCH > 1:
        for name, axis in PER_EXAMPLE_BATCH_AXIS.items():
            out[name] = _to_microbatches(out[name], axis)
    return {'x': out['x'], 'c': out['c'], 'ctx': out['ctx'], 'c_ctx': out['c_ctx'], 'w_ada': out['w_ada'], 'b_ada': out['b_ada'], 'norm1_g': out['norm1_g'], 'w_in': out['w_in'], 'na_rpb': out['na_rpb'], 'hg_lb_logits': out['hg_lb_logits'], 'hg_norm_g': out['hg_norm_g'], 'w_pa': out['w_pa'], 'w_pb': out['w_pb'], 'w_out': out['w_out'], 'norm2_g': out['norm2_g'], 'w_ffn_in': out['w_ffn_in'], 'w_ffn_out': out['w_ffn_out'], 'final_g': out['final_g'], 'loss_target': out['loss_target'], 'm_c_ctx': out['m_c_ctx'], 'm_w_ada': out['m_w_ada'], 'm_b_ada': out['m_b_ada'], 'm_norm1_g': out['m_norm1_g'], 'm_w_in': out['m_w_in'], 'm_na_rpb': out['m_na_rpb'], 'm_hg_lb_logits': out['m_hg_lb_logits'], 'm_hg_norm_g': out['m_hg_norm_g'], 'm_w_pa': out['m_w_pa'], 'm_w_pb': out['m_w_pb'], 'm_w_out': out['m_w_out'], 'm_norm2_g': out['m_norm2_g'], 'm_w_ffn_in': out['m_w_ffn_in'], 'm_w_ffn_out': out['m_w_ffn_out'], 'm_final_g': out['m_final_g'], 'v_c_ctx': out['v_c_ctx'], 'v_w_ada': out['v_w_ada'], 'v_b_ada': out['v_b_ada'], 'v_norm1_g': out['v_norm1_g'], 'v_w_in': out['v_w_in'], 'v_na_rpb': out['v_na_rpb'], 'v_hg_lb_logits': out['v_hg_lb_logits'], 'v_hg_norm_g': out['v_hg_norm_g'], 'v_w_pa': out['v_w_pa'], 'v_w_pb': out['v_w_pb'], 'v_w_out': out['v_w_out'], 'v_norm2_g': out['v_norm2_g'], 'v_w_ffn_in': out['v_w_ffn_in'], 'v_w_ffn_out': out['v_w_ffn_out'], 'v_final_g': out['v_final_g']}


def _loss(weights, diff, rest, loss_target):
    with _jax.named_scope("forward"):
        args = {**rest, TWIN_DIFF_INPUT: diff, **{k: w.astype(_WEIGHT_DTYPES[k]) for k, w in weights.items()}}
        y = _forward(args)
    with _jax.named_scope("loss_head"):
        err = _jnp.square(y.astype(_jnp.float32) - loss_target)
        return 0.5 * _jnp.sum(_jnp.mean(err, axis=-1)) if err.ndim else 0.5 * err


def _adamw(w, g, m, v):
    m = ADAM_B1 * m + (1.0 - ADAM_B1) * g
    v = ADAM_B2 * v + (1.0 - ADAM_B2) * _jnp.square(g)
    m_hat = m / (1.0 - ADAM_B1 ** ADAM_STEP)
    v_hat = v / (1.0 - ADAM_B2 ** ADAM_STEP)
    delta = -ADAM_LR * (m_hat / (_jnp.sqrt(v_hat) + ADAM_EPS) + ADAM_WD * w)
    return delta, m, v


def reference(x, c, ctx, c_ctx, w_ada, b_ada, norm1_g, w_in, na_rpb, hg_lb_logits, hg_norm_g, w_pa, w_pb, w_out, norm2_g, w_ffn_in, w_ffn_out, final_g, loss_target, m_c_ctx, m_w_ada, m_b_ada, m_norm1_g, m_w_in, m_na_rpb, m_hg_lb_logits, m_hg_norm_g, m_w_pa, m_w_pb, m_w_out, m_norm2_g, m_w_ffn_in, m_w_ffn_out, m_final_g, v_c_ctx, v_w_ada, v_b_ada, v_norm1_g, v_w_in, v_na_rpb, v_hg_lb_logits, v_hg_norm_g, v_w_pa, v_w_pb, v_w_out, v_norm2_g, v_w_ffn_in, v_w_ffn_out, v_final_g):
    given = dict(x=x, c=c, ctx=ctx, c_ctx=c_ctx, w_ada=w_ada, b_ada=b_ada, norm1_g=norm1_g, w_in=w_in, na_rpb=na_rpb, hg_lb_logits=hg_lb_logits, hg_norm_g=hg_norm_g, w_pa=w_pa, w_pb=w_pb, w_out=w_out, norm2_g=norm2_g, w_ffn_in=w_ffn_in, w_ffn_out=w_ffn_out, final_g=final_g, loss_target=loss_target, m_c_ctx=m_c_ctx, m_w_ada=m_w_ada, m_b_ada=m_b_ada, m_norm1_g=m_norm1_g, m_w_in=m_w_in, m_na_rpb=m_na_rpb, m_hg_lb_logits=m_hg_lb_logits, m_hg_norm_g=m_hg_norm_g, m_w_pa=m_w_pa, m_w_pb=m_w_pb, m_w_out=m_w_out, m_norm2_g=m_norm2_g, m_w_ffn_in=m_w_ffn_in, m_w_ffn_out=m_w_ffn_out, m_final_g=m_final_g, v_c_ctx=v_c_ctx, v_w_ada=v_w_ada, v_b_ada=v_b_ada, v_norm1_g=v_norm1_g, v_w_in=v_w_in, v_na_rpb=v_na_rpb, v_hg_lb_logits=v_hg_lb_logits, v_hg_norm_g=v_hg_norm_g, v_w_pa=v_w_pa, v_w_pb=v_w_pb, v_w_out=v_w_out, v_norm2_g=v_norm2_g, v_w_ffn_in=v_w_ffn_in, v_w_ffn_out=v_w_ffn_out, v_final_g=v_final_g)
    weights = {n: given[n] for n in TWIN_WEIGHTS}
    shared = {n: given[n] for n in SHARED_INPUTS}
    per_example = {n: given[n] for n in ['x', 'c', 'ctx']}
    grad_fn = _jax.value_and_grad(_loss, argnums=(0, 1))

    def one_microbatch(ex, loss_target):
        ex = dict(ex)
        diff = ex.pop(TWIN_DIFF_INPUT)
        return grad_fn(weights, diff, {**shared, **ex}, loss_target)

    if N_MICROBATCH == 1:
        loss, (grad_w, grad_x) = one_microbatch(per_example, given["loss_target"])
    else:
        def body(carry, xs):
            loss_sum, grad_sum = carry
            l_k, (gw_k, gx_k) = one_microbatch(xs[0], xs[1])
            with _jax.named_scope("update"):
                return (loss_sum + l_k, _jax.tree.map(_jnp.add, grad_sum, gw_k)), gx_k

        init = (_jnp.zeros((), _jnp.float32), _jax.tree.map(_jnp.zeros_like, weights))
        (loss, grad_w), grad_x = _jax.lax.scan(body, init, (per_example, given["loss_target"]))
    with _jax.named_scope("update"):
        delta_w, new_m, new_v = {}, {}, {}
        for n in TWIN_WEIGHTS:
            delta_w[n], new_m[n], new_v[n] = _adamw(weights[n], grad_w[n], given["m_" + n], given["v_" + n])
    return (loss, grad_x, *[grad_w[n] for n in TWIN_WEIGHTS], *[delta_w[n] for n in TWIN_WEIGHTS],
            *[new_m[n] for n in TWIN_WEIGHTS], *[new_v[n] for n in TWIN_WEIGHTS])
```

```python
import functools

import jax
import jax.numpy as jnp
from jax import lax
from jax.experimental import pallas as pl
from jax.experimental.pallas import tpu as pltpu

F32 = jnp.float32
BF16 = jnp.bfloat16

GRID_W = 64
WIN_H = 8
WIN_W = 16
NA_HEADS = 16
NA_HEAD_DIM = 64
HG_HEADS = 8
HG_DIM = 128
HG_CHUNK = 64
N_MOD = 6
EPS = 1e-6
ADAM_LR = 0.001
ADAM_B1 = 0.9
ADAM_B2 = 0.999
ADAM_EPS = 1e-08
ADAM_WD = 0.01
ADAM_STEP = 10

LANES = 128
VMEM_LIMIT = 56 * 1024 * 1024
MASK_VALUE = -1e30
EXP_CLAMP = 80.0
MESH_ID = pl.DeviceIdType.MESH
HI = lax.Precision.HIGHEST


def _tile(dim, target, mult=LANES):
    best = None
    t = mult
    while t <= min(dim, target):
        if dim % t == 0:
            best = t
        t += mult
    assert best is not None, (dim, target, mult)
    return best


def _params(sem):
    return pltpu.CompilerParams(dimension_semantics=sem, vmem_limit_bytes=VMEM_LIMIT)


def _dot(a, b, precision=None):
    return jnp.dot(a, b, preferred_element_type=F32, precision=precision)


def _dot_nt(a, b, precision=None):
    return lax.dot_general(a, b, (((1,), (1,)), ((), ())), preferred_element_type=F32, precision=precision)


def _dot_tn(a, b, precision=None):
    return lax.dot_general(a, b, (((0,), (0,)), ((), ())), preferred_element_type=F32, precision=precision)


def _sigmoid(v):
    return 1.0 / (1.0 + jnp.exp(-v))


def _mm_nn(a, b3, out_dtype, name):
    M, K = a.shape
    nsh, _, Ns = b3.shape
    tm, tn, tk = _tile(M, 1024), _tile(Ns, 1408), _tile(K, 2048)
    tps, nk = Ns // tn, K // tk

    def body(a_ref, b_ref, o_ref, acc):
        k = pl.program_id(2)

        @pl.when(k == 0)
        def _():
            acc[...] = jnp.zeros_like(acc)

        acc[...] += _dot(a_ref[...], b_ref[...])

        @pl.when(k == nk - 1)
        def _():
            o_ref[...] = acc[...].astype(o_ref.dtype)

    return pl.pallas_call(
        body, name=name, grid=(M // tm, nsh * tps, nk),
        in_specs=[pl.BlockSpec((tm, tk), lambda m, n, k: (m, k)),
                  pl.BlockSpec((None, tk, tn), lambda m, n, k: (n // tps, k, n % tps))],
        out_specs=pl.BlockSpec((tm, tn), lambda m, n, k: (m, n)),
        out_shape=jax.ShapeDtypeStruct((M, nsh * Ns), out_dtype),
        scratch_shapes=[pltpu.VMEM((tm, tn), F32)],
        compiler_params=_params(("parallel", "parallel", "arbitrary")),
    )(a, b3)


def _mm_nt(a, b3, out_dtype, name):
    M = a.shape[0]
    nsh, Kw, Ns = b3.shape
    tm, tn, tk = _tile(M, 1024), _tile(Kw, 1024), _tile(Ns, 1536)
    kps = Ns // tk
    nk = nsh * kps

    def body(a_ref, b_ref, o_ref, acc):
        k = pl.program_id(2)

        @pl.when(k == 0)
        def _():
            acc[...] = jnp.zeros_like(acc)

        acc[...] += _dot_nt(a_ref[...], b_ref[...])

        @pl.when(k == nk - 1)
        def _():
            o_ref[...] = acc[...].astype(o_ref.dtype)

    return pl.pallas_call(
        body, name=name, grid=(M // tm, Kw // tn, nk),
        in_specs=[pl.BlockSpec((tm, tk), lambda m, n, k: (m, k)),
                  pl.BlockSpec((None, tn, tk), lambda m, n, k: (k // kps, n, k % kps))],
        out_specs=pl.BlockSpec((tm, tn), lambda m, n, k: (m, n)),
        out_shape=jax.ShapeDtypeStruct((M, Kw), out_dtype),
        scratch_shapes=[pltpu.VMEM((tm, tn), F32)],
        compiler_params=_params(("parallel", "parallel", "arbitrary")),
    )(a, b3)


def _mm_tn(a, g, nsh, name):
    Tk, M = a.shape
    Ns = g.shape[1] // nsh
    tm, tn, tk = _tile(M // 2, 1408), _tile(Ns, 1408), _tile(Tk, 1024)
    mh, tps, nk = (M // 2) // tm, Ns // tn, Tk // tk

    def body(a_ref, g_ref, o_ref, acc):
        k = pl.program_id(2)

        @pl.when(k == 0)
        def _():
            acc[...] = jnp.zeros_like(acc)

        acc[...] += _dot_tn(a_ref[...], g_ref[...])

        @pl.when(k == nk - 1)
        def _():
            o_ref[...] = acc[...]

    return pl.pallas_call(
        body, name=name, grid=(M // tm, nsh * tps, nk),
        in_specs=[pl.BlockSpec((tk, tm), lambda m, n, k: (k, m)),
                  pl.BlockSpec((tk, tn), lambda m, n, k: (k, n))],
        out_specs=pl.BlockSpec((None, None, tm, tn), lambda m, n, k: (n // tps, m // mh, m % mh, n % tps)),
        out_shape=jax.ShapeDtypeStruct((nsh, 2, M // 2, Ns), F32),
        scratch_shapes=[pltpu.VMEM((tm, tn), F32)],
        compiler_params=_params(("parallel", "parallel", "arbitrary")),
    )(a, g)


def _rowwise(fn, nblk, tm, rins, vins, routs, accs, name):
    nr, nv, no, na = len(rins), len(vins), len(routs), len(accs)

    def body(*refs):
        i = pl.program_id(0)
        outs, accv = fn(i, [r[...] for r in refs[:nr]], [r[...] for r in refs[nr:nr + nv]])
        for r, v in zip(refs[nr + nv:nr + nv + no], outs):
            r[...] = v.astype(r.dtype)
        arefs = refs[nr + nv + no:]
        if na:
            @pl.when(i == 0)
            def _():
                for a in arefs:
                    a[...] = jnp.zeros_like(a)

            for a, v in zip(arefs, accv):
                a[...] += v

    def row_spec(w, cb, rm):
        if rm is None:
            return pl.BlockSpec((tm, w), lambda i: (i, cb))
        return pl.BlockSpec((tm, w), lambda i: (rm(i), cb))

    in_specs = [row_spec(w, cb, rm) for (_, w, cb, rm) in rins]
    in_specs += [pl.BlockSpec(v.shape, lambda i: (0, 0)) for v in vins]
    out_specs = [pl.BlockSpec((tm, w), lambda i: (i, 0)) for (w, _) in routs]
    out_specs += [pl.BlockSpec((1, w), lambda i: (0, 0)) for w in accs]
    out_shape = [jax.ShapeDtypeStruct((nblk * tm, w), dt) for (w, dt) in routs]
    out_shape += [jax.ShapeDtypeStruct((1, w), F32) for w in accs]
    res = pl.pallas_call(
        body, name=name, grid=(nblk,), in_specs=in_specs, out_specs=out_specs, out_shape=out_shape,
        compiler_params=_params(("arbitrary",)),
    )(*[r[0] for r in rins], *vins)
    return list(res)


def _colsum(v):
    return jnp.sum(v, axis=0, keepdims=True)


def _rms(v):
    return lax.rsqrt(jnp.mean(v * v, axis=-1, keepdims=True) + EPS)


def _all_gather8(xs):
    m_per, n = xs.shape

    def body(x_ref, out_ref, send_sems, recv_sems, local_sem):
        x, y, c = lax.axis_index("x"), lax.axis_index("y"), lax.axis_index("c")
        me, sibling = (x, y, c), (x, y, 1 - c)
        chips = [(1 - x, y), (x, 1 - y), (1 - x, 1 - y)]

        def rows(px, py, pc):
            return out_ref.at[pl.ds((4 * px + 2 * py + pc) * m_per, m_per), :]

        def copy(k, block, to, src=None):
            return pltpu.make_async_remote_copy(
                src_ref=rows(*block) if src is None else src, dst_ref=rows(*block),
                send_sem=send_sems.at[k], recv_sem=recv_sems.at[k], device_id=to, device_id_type=MESH_ID)

        mine = pltpu.make_async_copy(x_ref, rows(*me), local_sem)
        mine.start()
        first = [copy(0, me, sibling, src=x_ref)]
        first += [copy(1 + j, me, (*chip, c), src=x_ref) for j, chip in enumerate(chips)]
        for cp in first:
            cp.start()
        passed = [copy(4 + j, (*chip, c), sibling) for j, chip in enumerate(chips)]
        for j, chip in enumerate(chips):
            copy(1 + j, (*chip, c), me).wait_recv()
            passed[j].start()
        copy(0, sibling, me).wait_recv()
        for j, chip in enumerate(chips):
            copy(4 + j, (*chip, 1 - c), me).wait_recv()
        for cp in first + passed:
            cp.wait_send()
        mine.wait()

    return pl.pallas_call(
        body, name="all_gather8_%dx%d" % (m_per, n),
        out_shape=jax.ShapeDtypeStruct((8 * m_per, n), xs.dtype),
        in_specs=[pl.BlockSpec(memory_space=pltpu.VMEM)],
        out_specs=pl.BlockSpec(memory_space=pltpu.VMEM),
        scratch_shapes=[pltpu.SemaphoreType.DMA((7,)), pltpu.SemaphoreType.DMA((7,)), pltpu.SemaphoreType.DMA],
    )(xs)


def _mesh_pos():
    x, y, c = lax.axis_index("x"), lax.axis_index("y"), lax.axis_index("c")
    chips = [(1 - x, y), (x, 1 - y), (1 - x, 1 - y)]
    return x, y, c, chips


def _gather_weights(ws):
    n = len(ws)

    def body(*refs):
        w, out = refs[:n], refs[n:2 * n]
        send_sems, recv_sems, local_sems = refs[2 * n:]
        x, y, c, chips = _mesh_pos()
        s = 2 * x + y
        sibling = (x, y, 1 - c)

        def copy(i, k, shard, half, to, src=None):
            dst = out[i].at[shard, half]
            return pltpu.make_async_remote_copy(
                src_ref=dst if src is None else src, dst_ref=dst,
                send_sem=send_sems.at[6 * i + k], recv_sem=recv_sems.at[6 * i + k],
                device_id=to, device_id_type=MESH_ID)

        mine = [pltpu.make_async_copy(w[i], out[i].at[s], local_sems.at[i]) for i in range(n)]
        for cp in mine:
            cp.start()
        first = [copy(i, j, s, c, (*chip, c), src=w[i].at[c]) for i in range(n) for j, chip in enumerate(chips)]
        for cp in first:
            cp.start()
        passed = []
        for j, chip in enumerate(chips):
            sj = 2 * chip[0] + chip[1]
            for i in range(n):
                copy(i, j, sj, c, (x, y, c)).wait_recv()
                cp = copy(i, 3 + j, sj, c, sibling)
                cp.start()
                passed.append(cp)
        for j, chip in enumerate(chips):
            sj = 2 * chip[0] + chip[1]
            for i in range(n):
                copy(i, 3 + j, sj, 1 - c, (x, y, c)).wait_recv()
        for cp in first + passed:
            cp.wait_send()
        for cp in mine:
            cp.wait()

    any_spec = pl.BlockSpec(memory_space=pl.ANY)
    return pl.pallas_call(
        body, name="gather_weights",
        out_shape=[jax.ShapeDtypeStruct((4,) + w.shape, w.dtype) for w in ws],
        in_specs=[any_spec] * n, out_specs=[any_spec] * n,
        scratch_shapes=[pltpu.SemaphoreType.DMA((6 * n,)), pltpu.SemaphoreType.DMA((6 * n,)),
                        pltpu.SemaphoreType.DMA((n,))],
    )(*ws)


def _swap_other_half(gs):
    n = len(gs)

    def body(*refs):
        g, land = refs[:n], refs[n:2 * n]
        send_sems, recv_sems = refs[2 * n:]
        x, y, c, _ = _mesh_pos()
        cps = [pltpu.make_async_remote_copy(
            src_ref=g[i].at[:, 1 - c], dst_ref=land[i], send_sem=send_sems.at[i], recv_sem=recv_sems.at[i],
            device_id=(x, y, 1 - c), device_id_type=MESH_ID) for i in range(n)]
        for cp in cps:
            cp.start()
        for cp in cps:
            cp.wait()

    any_spec = pl.BlockSpec(memory_space=pl.ANY)
    return pl.pallas_call(
        body, name="rs_swap_other_half",
        out_shape=[jax.ShapeDtypeStruct((4,) + g.shape[2:], g.dtype) for g in gs],
        in_specs=[any_spec] * n, out_specs=[any_spec] * n,
        scratch_shapes=[pltpu.SemaphoreType.DMA((n,)), pltpu.SemaphoreType.DMA((n,))],
    )(*gs)


def _scatter_pieces(ps):
    n = len(ps)

    def body(*refs):
        p, land = refs[:n], refs[n:2 * n]
        send_sems, recv_sems, local_sems = refs[2 * n:]
        x, y, c, chips = _mesh_pos()
        s = 2 * x + y

        def copy(i, j, chip):
            sj = 2 * chip[0] + chip[1]
            return pltpu.make_async_remote_copy(
                src_ref=p[i].at[sj], dst_ref=land[i].at[s],
                send_sem=send_sems.at[3 * i + j], recv_sem=recv_sems.at[3 * i + j],
                device_id=(*chip, c), device_id_type=MESH_ID)

        def arrival(i, j, chip):
            sj = 2 * chip[0] + chip[1]
            return pltpu.make_async_remote_copy(
                src_ref=land[i].at[sj], dst_ref=land[i].at[sj],
                send_sem=send_sems.at[3 * i + j], recv_sem=recv_sems.at[3 * i + j],
                device_id=(x, y, c), device_id_type=MESH_ID)

        mine = [pltpu.make_async_copy(p[i].at[s], land[i].at[s], local_sems.at[i]) for i in range(n)]
        for cp in mine:
            cp.start()
        sends = [copy(i, j, chip) for i in range(n) for j, chip in enumerate(chips)]
        for cp in sends:
            cp.start()
        for i in range(n):
            for j, chip in enumerate(chips):
                arrival(i, j, chip).wait_recv()
        for cp in sends:
            cp.wait_send()
        for cp in mine:
            cp.wait()

    any_spec = pl.BlockSpec(memory_space=pl.ANY)
    return pl.pallas_call(
        body, name="rs_scatter_pieces",
        out_shape=[jax.ShapeDtypeStruct(p.shape, p.dtype) for p in ps],
        in_specs=[any_spec] * n, out_specs=[any_spec] * n,
        scratch_shapes=[pltpu.SemaphoreType.DMA((3 * n,)), pltpu.SemaphoreType.DMA((3 * n,)),
                        pltpu.SemaphoreType.DMA((n,))],
    )(*ps)


def _swap_result_half(rs):
    n = len(rs)

    def body(*refs):
        r, out = refs[:n], refs[n:2 * n]
        send_sems, recv_sems, local_sems = refs[2 * n:]
        x, y, c, _ = _mesh_pos()
        mine = [pltpu.make_async_copy(r[i], out[i].at[c], local_sems.at[i]) for i in range(n)]
        for cp in mine:
            cp.start()
        cps = [pltpu.make_async_remote_copy(
            src_ref=r[i], dst_ref=out[i].at[c], send_sem=send_sems.at[i], recv_sem=recv_sems.at[i],
            device_id=(x, y, 1 - c), device_id_type=MESH_ID) for i in range(n)]
        for cp in cps:
            cp.start()
        for i in range(n):
            pltpu.make_async_remote_copy(
                src_ref=r[i], dst_ref=out[i].at[1 - c], send_sem=send_sems.at[i], recv_sem=recv_sems.at[i],
                device_id=(x, y, c), device_id_type=MESH_ID).wait_recv()
        for cp in cps:
            cp.wait_send()
        for cp in mine:
            cp.wait()

    any_spec = pl.BlockSpec(memory_space=pl.ANY)
    return pl.pallas_call(
        body, name="rs_swap_result_half",
        out_shape=[jax.ShapeDtypeStruct((2,) + r.shape, r.dtype) for r in rs],
        in_specs=[any_spec] * n, out_specs=[any_spec] * n,
        scratch_shapes=[pltpu.SemaphoreType.DMA((n,)), pltpu.SemaphoreType.DMA((n,)),
                        pltpu.SemaphoreType.DMA((n,))],
    )(*rs)


def _add_own_half(cidx, g, land, name):
    _, _, rh, cw = g.shape
    tm = _tile(rh, max(16, (1 << 20) // (4 * cw)), 16)

    def body(c_ref, g_ref, l_ref, o_ref):
        o_ref[...] = (g_ref[...] + l_ref[...]).astype(o_ref.dtype)

    return pl.pallas_call(
        body, name=name,
        grid_spec=pltpu.PrefetchScalarGridSpec(
            num_scalar_prefetch=1, grid=(4, rh // tm),
            in_specs=[pl.BlockSpec((None, None, tm, cw), lambda s, i, c_ref: (s, c_ref[0], i, 0)),
                      pl.BlockSpec((None, tm, cw), lambda s, i, c_ref: (s, i, 0))],
            out_specs=pl.BlockSpec((None, tm, cw), lambda s, i, c_ref: (s, i, 0))),
        out_shape=jax.ShapeDtypeStruct((4, rh, cw), BF16),
        compiler_params=_params(("parallel", "parallel")),
    )(cidx, g, land)


def _sum_pieces(land, name):
    _, rh, cw = land.shape
    tm = _tile(rh, max(16, (1 << 20) // (4 * cw)), 16) if rh % 16 == 0 else rh

    def body(l_ref, o_ref):
        v = l_ref[...].astype(F32)
        o_ref[...] = (v[0] + v[1]) + (v[2] + v[3])

    return pl.pallas_call(
        body, name=name, grid=(rh // tm,),
        in_specs=[pl.BlockSpec((4, tm, cw), lambda i: (0, i, 0))],
        out_specs=pl.BlockSpec((tm, cw), lambda i: (i, 0)),
        out_shape=jax.ShapeDtypeStruct((rh, cw), F32),
        compiler_params=_params(("parallel",)),
    )(land)


def _reduce_scatter(cidx, gs, names):
    lands = _swap_other_half(gs)
    parts = [_add_own_half(cidx, g, l, "rs_add_" + nm) for g, l, nm in zip(gs, lands, names)]
    pieces = _scatter_pieces(parts)
    halves = [_sum_pieces(p, "rs_sum_" + nm) for p, nm in zip(pieces, names)]
    fulls = _swap_result_half(halves)
    return [f.reshape(2 * f.shape[1], f.shape[2]) for f in fulls]


def _sum8(g, name):
    def body(g_ref, o_ref):
        acc = g_ref[0]
        for k in range(1, 8):
            acc = acc + g_ref[k]
        o_ref[...] = acc

    return pl.pallas_call(body, name=name, out_shape=jax.ShapeDtypeStruct(g.shape[1:], F32))(g)


def _silu(v):
    return v * _sigmoid(v)


def _dsilu(v):
    s = _sigmoid(v)
    return s * (1.0 + v * (1.0 - s))


def _ada_fwd(cin, w, b):
    d, ns = w.shape
    tn = _tile(ns, 512)

    def body(c_ref, w_ref, b_ref, o_ref):
        o_ref[...] = _dot(_silu(c_ref[...]), w_ref[...], HI) + b_ref[...]

    return pl.pallas_call(
        body, name="ada_fwd", grid=(ns // tn,),
        in_specs=[pl.BlockSpec(cin.shape, lambda n: (0, 0)), pl.BlockSpec((d, tn), lambda n: (0, n)),
                  pl.BlockSpec((1, tn), lambda n: (0, n))],
        out_specs=pl.BlockSpec((cin.shape[0], tn), lambda n: (0, n)),
        out_shape=jax.ShapeDtypeStruct((cin.shape[0], ns), F32),
        compiler_params=_params(("parallel",)),
    )(cin, w, b)


def _ada_bwd(cin, w, dm):
    d, ns = w.shape
    tn = _tile(ns, 512)

    def body(c_ref, w_ref, d_ref, dw_ref, dc_ref):
        n = pl.program_id(0)

        @pl.when(n == 0)
        def _():
            dc_ref[...] = jnp.zeros_like(dc_ref)

        dw_ref[...] = _dot_tn(_silu(c_ref[...]), d_ref[...], HI)
        dc_ref[...] += _dot_nt(d_ref[...], w_ref[...], HI)

    return pl.pallas_call(
        body, name="ada_bwd", grid=(ns // tn,),
        in_specs=[pl.BlockSpec(cin.shape, lambda n: (0, 0)), pl.BlockSpec((d, tn), lambda n: (0, n)),
                  pl.BlockSpec((cin.shape[0], tn), lambda n: (0, n))],
        out_specs=[pl.BlockSpec((d, tn), lambda n: (0, n)), pl.BlockSpec(cin.shape, lambda n: (0, 0))],
        out_shape=[jax.ShapeDtypeStruct((d, ns), F32), jax.ShapeDtypeStruct(cin.shape, F32)],
        compiler_params=_params(("arbitrary",)),
    )(cin, w, dm)


def _bias_tables(rpb, rows):
    kh = min(WIN_H, rows)
    col = jnp.arange(GRID_W)
    col_start = jnp.clip(col - WIN_W // 2, 0, GRID_W - WIN_W)
    in_win = (col[None, :] >= col_start[:, None]) & (col[None, :] < col_start[:, None] + WIN_W)
    dc_idx = jnp.clip(col[None, :] - col[:, None], 1 - WIN_W, WIN_W - 1) + WIN_W - 1
    dr_idx = jnp.arange(kh)[None, :] - jnp.arange(kh)[:, None] + WIN_H - 1
    t = rpb[:, dr_idx][:, :, :, dc_idx]
    t = jnp.where(in_win[None, None, None], t, MASK_VALUE)
    return t.transpose(0, 1, 3, 2, 4).reshape(rpb.shape[0], kh, GRID_W, kh * GRID_W).astype(F32)


def _na_geometry(S):
    rows = S // GRID_W
    kh = min(WIN_H, rows)

    def row_start(r):
        return jnp.clip(r - kh // 2, 0, rows - kh)

    return rows, kh, row_start


def _na_scores(q2, kband, kctx, b_ref, hh, sel):
    qh = jnp.where(sel, q2, jnp.zeros_like(q2))
    sb = _dot_nt(qh, kband) + b_ref[hh]
    sc = _dot_nt(qh, kctx)
    return qh, sb, sc


def _na_fwd(qs, kb, vb, bias, S, L):
    T, naw = qs.shape
    rows, kh, row_start = _na_geometry(S)
    nb = kh * GRID_W
    npair = naw // LANES

    def body(q_ref, k_ref, v_ref, b_ref, o_ref, lse_ref):
        r = pl.program_id(1)
        start = pl.multiple_of(row_start(r) * GRID_W, GRID_W)
        q2 = q_ref[...]
        kband, vband = k_ref[pl.ds(start, nb), :], v_ref[pl.ds(start, nb), :]
        kctx, vctx = k_ref[pl.ds(S, L), :], v_ref[pl.ds(S, L), :]
        lane = lax.broadcasted_iota(jnp.int32, (GRID_W, LANES), 1)
        out = jnp.zeros((GRID_W, LANES), F32)
        lse = jnp.zeros((GRID_W, LANES), F32)
        for hh in range(2):
            sel = (lane >= NA_HEAD_DIM) if hh else (lane < NA_HEAD_DIM)
            _, sb, sc = _na_scores(q2, kband, kctx, b_ref, hh, sel)
            m = jnp.maximum(jnp.max(sb, axis=-1, keepdims=True), jnp.max(sc, axis=-1, keepdims=True))
            pb, pc = jnp.exp(sb - m), jnp.exp(sc - m)
            l = jnp.sum(pb, axis=-1, keepdims=True) + jnp.sum(pc, axis=-1, keepdims=True)
            inv = 1.0 / l
            o = _dot((pb * inv).astype(BF16), vband) + _dot((pc * inv).astype(BF16), vctx)
            out = jnp.where(sel, o, out)
            lse = jnp.where(sel, m + jnp.log(l), lse)
        o_ref[...] = out.astype(o_ref.dtype)
        lse_ref[...] = lse

    return pl.pallas_call(
        body, name="na_fwd", grid=(npair, rows),
        in_specs=[pl.BlockSpec((GRID_W, LANES), lambda p, r: (r, p)),
                  pl.BlockSpec((T, LANES), lambda p, r: (0, p)),
                  pl.BlockSpec((T, LANES), lambda p, r: (0, p)),
                  pl.BlockSpec((2, None, GRID_W, nb), lambda p, r: (p, r - row_start(r), 0, 0))],
        out_specs=[pl.BlockSpec((GRID_W, LANES), lambda p, r: (r, p)),
                   pl.BlockSpec((GRID_W, LANES), lambda p, r: (r, p))],
        out_shape=[jax.ShapeDtypeStruct((S, naw), BF16), jax.ShapeDtypeStruct((S, naw), F32)],
        compiler_params=_params(("parallel", "arbitrary")),
    )(qs, kb, vb, bias)


def _na_bwd(qs, kb, vb, bias, do, o, lse, S, L):
    T, naw = qs.shape
    rows, kh, row_start = _na_geometry(S)
    nb = kh * GRID_W
    npair = naw // LANES

    def body(q_ref, k_ref, v_ref, b_ref, do_ref, o_ref, lse_ref, dq_ref, dk_ref, dv_ref, db_ref):
        r = pl.program_id(1)

        @pl.when(r == 0)
        def _():
            dk_ref[...] = jnp.zeros_like(dk_ref)
            dv_ref[...] = jnp.zeros_like(dv_ref)
            db_ref[...] = jnp.zeros_like(db_ref)

        rs = row_start(r)
        shift = r - rs
        start = pl.multiple_of(rs * GRID_W, GRID_W)
        q2 = q_ref[...]
        kband, vband = k_ref[pl.ds(start, nb), :], v_ref[pl.ds(start, nb), :]
        kctx, vctx = k_ref[pl.ds(S, L), :], v_ref[pl.ds(S, L), :]
        do2 = do_ref[...]
        o2 = o_ref[...].astype(F32)
        lse2 = lse_ref[...]
        lane = lax.broadcasted_iota(jnp.int32, (GRID_W, LANES), 1)
        dq = jnp.zeros((GRID_W, LANES), F32)
        for hh in range(2):
            sel = (lane >= NA_HEAD_DIM) if hh else (lane < NA_HEAD_DIM)
            qh, sb, sc = _na_scores(q2, kband, kctx, b_ref.at[:, shift], hh, sel)
            lse_h = lse2[:, hh * NA_HEAD_DIM:hh * NA_HEAD_DIM + 1]
            pb, pc = jnp.exp(sb - lse_h), jnp.exp(sc - lse_h)
            doh = jnp.where(sel, do2, jnp.zeros_like(do2))
            delta = jnp.sum(doh.astype(F32) * o2, axis=-1, keepdims=True)
            dsb = pb * (_dot_nt(doh, vband) - delta)
            dsc = pc * (_dot_nt(doh, vctx) - delta)
            db_ref[hh, shift] += dsb
            dsb16, dsc16 = dsb.astype(BF16), dsc.astype(BF16)
            dq = jnp.where(sel, _dot(dsb16, kband) + _dot(dsc16, kctx), dq)
            dk_ref[pl.ds(start, nb), :] += _dot_tn(dsb16, qh)
            dk_ref[pl.ds(S, L), :] += _dot_tn(dsc16, qh)
            dv_ref[pl.ds(start, nb), :] += _dot_tn(pb.astype(BF16), doh)
            dv_ref[pl.ds(S, L), :] += _dot_tn(pc.astype(BF16), doh)
        dq_ref[...] = dq

    blk = pl.BlockSpec((GRID_W, LANES), lambda p, r: (r, p))
    col = pl.BlockSpec((T, LANES), lambda p, r: (0, p))
    return pl.pallas_call(
        body, name="na_bwd", grid=(npair, rows),
        in_specs=[blk, col, col, pl.BlockSpec((2, kh, GRID_W, nb), lambda p, r: (p, 0, 0, 0)), blk, blk, blk],
        out_specs=[blk, col, col, pl.BlockSpec((2, kh, GRID_W, nb), lambda p, r: (p, 0, 0, 0))],
        out_shape=[jax.ShapeDtypeStruct((S, naw), F32), jax.ShapeDtypeStruct((T, naw), F32),
                   jax.ShapeDtypeStruct((T, naw), F32), jax.ShapeDtypeStruct(bias.shape, F32)],
        compiler_params=_params(("parallel", "arbitrary")),
    )(qs, kb, vb, bias, do, o, lse)


def _hg_cols(naw, hgf, rev):
    qcol = (3 * naw) // HG_DIM
    fcol = (3 * naw + hgf * (2 if rev else 1)) // HG_DIM
    icol = (3 * naw + 3 * hgf) // HG_DIM
    return qcol, fcol, icol


def _hg_chunk_order(S, L, rev):
    ncl, ncc = S // HG_CHUNK, L // HG_CHUNK
    nc = ncl + ncc

    def chunk_of(i):
        if rev:
            return nc - 1 - i
        return jnp.where(i < ncc, ncl + i, i - ncc)

    return nc, ncl, chunk_of


def _hg_gates(q, z, lb, rev):
    row = lax.broadcasted_iota(jnp.int32, (HG_CHUNK, HG_CHUNK), 0)
    colm = lax.broadcasted_iota(jnp.int32, (HG_CHUNK, HG_CHUNK), 1)
    tri = (colm >= row) if rev else (row >= colm)
    trif = tri.astype(F32)
    sig = _sigmoid(z)
    f = lb + (1.0 - lb) * sig
    lf = jnp.log(f)
    k = 1.0 - f
    cum = _dot(trif, lf, HI)
    mid = cum[HG_CHUNK // 2:HG_CHUNK // 2 + 1, :]
    last = cum[0:1, :] if rev else cum[HG_CHUNK - 1:HG_CHUNK, :]
    eq = jnp.exp(jnp.clip(cum - mid, -EXP_CLAMP, EXP_CLAMP))
    ek = jnp.exp(jnp.clip(mid - cum, -EXP_CLAMP, EXP_CLAMP))
    return tri, trif, sig, f, k, cum, last, eq, ek


def _hg_fwd(u, lbb, S, L, naw, hgf, rev):
    T = S + L
    nh = hgf // HG_DIM
    nc, ncl, chunk_of = _hg_chunk_order(S, L, rev)
    qcol, fcol, icol = _hg_cols(naw, hgf, rev)

    def body(q_ref, z_ref, v_ref, lb_ref, o_ref, st_ref, state):
        i = pl.program_id(1)

        @pl.when(i == 0)
        def _():
            state[...] = jnp.zeros_like(state)

        q, z, v = q_ref[...], z_ref[...], v_ref[...]
        lb = lb_ref[0:1, :]
        tri, _, _, _, k, cum, last, eq, ek = _hg_gates(q, z, lb, rev)
        a = jnp.where(tri, _dot_nt(q * eq, k * ek, HI), 0.0)
        s0 = state[...]
        st_ref[...] = s0
        o_ref[...] = _dot(a, v, HI) + _dot_nt(q * jnp.exp(cum), s0, HI)
        state[...] = s0 * jnp.exp(last) + _dot_tn(v, k * jnp.exp(last - cum), HI)

    def blk(colbase):
        return pl.BlockSpec((HG_CHUNK, HG_DIM), lambda h, i: (chunk_of(i), colbase + h))

    return pl.pallas_call(
        body, name="hg_fwd_rev" if rev else "hg_fwd", grid=(nh, nc),
        in_specs=[blk(qcol), blk(fcol), blk(icol), pl.BlockSpec((None, 8, HG_DIM), lambda h, i: (h, 0, 0))],
        out_specs=[pl.BlockSpec((HG_CHUNK, HG_DIM), lambda h, i: (chunk_of(i), h)),
                   pl.BlockSpec((None, None, HG_DIM, HG_DIM), lambda h, i: (chunk_of(i), h, 0, 0))],
        out_shape=[jax.ShapeDtypeStruct((T, hgf), F32), jax.ShapeDtypeStruct((nc, nh, HG_DIM, HG_DIM), F32)],
        scratch_shapes=[pltpu.VMEM((HG_DIM, HG_DIM), F32)],
        compiler_params=_params(("parallel", "arbitrary")),
    )(u, u, u, lbb)


def _hg_bwd(u, lbb, st, do, S, L, naw, hgf, rev):
    T = S + L
    nh = hgf // HG_DIM
    nc, ncl, chunk_fwd = _hg_chunk_order(S, L, rev)
    qcol, fcol, icol = _hg_cols(naw, hgf, rev)

    def chunk_of(j):
        return chunk_fwd(nc - 1 - j)

    def body(q_ref, z_ref, v_ref, lb_ref, st_ref, do_ref, dq_ref, dz_ref, dv_ref, dlb_ref, dstate):
        j = pl.program_id(1)

        @pl.when(j == 0)
        def _():
            dstate[...] = jnp.zeros_like(dstate)
            dlb_ref[...] = jnp.zeros_like(dlb_ref)

        q, z, v = q_ref[...], z_ref[...], v_ref[...]
        lb = lb_ref[0:1, :]
        tri, trif, sig, f, k, cum, last, eq, ek = _hg_gates(q, z, lb, rev)
        qe, ke = q * eq, k * ek
        a = jnp.where(tri, _dot_nt(qe, ke, HI), 0.0)
        s0 = st_ref[...]
        ec, el, ekd = jnp.exp(cum), jnp.exp(last), jnp.exp(last - cum)
        qd, kd = q * ec, k * ekd
        dout = jnp.where(chunk_of(j) < ncl, do_ref[...], 0.0)
        ds1 = dstate[...]
        da = jnp.where(tri, _dot_nt(dout, v, HI), 0.0)
        dv_ref[...] = _dot_tn(a, dout, HI) + _dot_nt(kd, ds1, HI)
        dqe, dke = _dot(da, ke, HI), _dot_tn(da, qe, HI)
        dqd, dkd = _dot(dout, s0, HI), _dot(v, ds1, HI)
        dstate[...] = _dot_tn(dout, qd, HI) + ds1 * el
        dq_ref[...] = dqe * eq + dqd * ec
        dk = dke * ek + dkd * ekd
        dcum = dqe * qe - dke * ke + dqd * qd - dkd * kd
        dlast = _colsum(dkd * kd) + el * _colsum(ds1 * s0)
        dlf = _dot_tn(trif, dcum, HI) + dlast
        df = dlf / f - dk
        dz_ref[...] = df * (1.0 - lb) * sig * (1.0 - sig)
        dlb_ref[...] += _colsum(df * (1.0 - sig))

    def blk(colbase):
        return pl.BlockSpec((HG_CHUNK, HG_DIM), lambda h, j: (chunk_of(j), colbase + h))

    oblk = pl.BlockSpec((HG_CHUNK, HG_DIM), lambda h, j: (chunk_of(j), h))
    return pl.pallas_call(
        body, name="hg_bwd_rev" if rev else "hg_bwd", grid=(nh, nc),
        in_specs=[blk(qcol), blk(fcol), blk(icol), pl.BlockSpec((None, 8, HG_DIM), lambda h, j: (h, 0, 0)),
                  pl.BlockSpec((None, None, HG_DIM, HG_DIM), lambda h, j: (chunk_of(j), h, 0, 0)),
                  pl.BlockSpec((HG_CHUNK, HG_DIM), lambda h, j: (jnp.minimum(chunk_of(j), ncl - 1), h))],
        out_specs=[oblk, oblk, oblk, pl.BlockSpec((None, 1, HG_DIM), lambda h, j: (h, 0, 0))],
        out_shape=[jax.ShapeDtypeStruct((T, hgf), F32)] * 3 + [jax.ShapeDtypeStruct((nh, 1, HG_DIM), F32)],
        scratch_shapes=[pltpu.VMEM((HG_DIM, HG_DIM), F32)],
        compiler_params=_params(("parallel", "arbitrary")),
    )(u, u, u, lbb, st, do)


def _adamw(w, g, m, v, name):
    shape = w.shape
    if w.ndim != 2 or shape[0] % 8 or shape[1] % LANES:
        w, g, m, v = [a.reshape(1, -1) for a in (w, g, m, v)]
    r, cw = w.shape
    tm = _tile(r, max(8, (1 << 19) // cw), 8) if r % 8 == 0 else r
    c1 = 1.0 / (1.0 - ADAM_B1 ** ADAM_STEP)
    c2 = 1.0 / (1.0 - ADAM_B2 ** ADAM_STEP)

    def body(w_ref, g_ref, m_ref, v_ref, d_ref, nm_ref, nv_ref):
        gg = g_ref[...]
        nm = ADAM_B1 * m_ref[...] + (1.0 - ADAM_B1) * gg
        nv = ADAM_B2 * v_ref[...] + (1.0 - ADAM_B2) * (gg * gg)
        d_ref[...] = -ADAM_LR * ((nm * c1) / (jnp.sqrt(nv * c2) + ADAM_EPS) + ADAM_WD * w_ref[...])
        nm_ref[...] = nm
        nv_ref[...] = nv

    spec = pl.BlockSpec((tm, cw), lambda i: (i, 0))
    outs = pl.pallas_call(
        body, name=name, grid=(r // tm,), in_specs=[spec] * 4, out_specs=[spec] * 3,
        out_shape=[jax.ShapeDtypeStruct((r, cw), F32)] * 3,
        compiler_params=_params(("parallel",)),
    )(w, g, m, v)
    return [o.reshape(shape) for o in outs]


def kernel(x, c, ctx, c_ctx, w_ada, b_ada, norm1_g, w_in, na_rpb, hg_lb_logits, hg_norm_g, w_pa, w_pb, w_out, norm2_g, w_ffn_in, w_ffn_out, final_g, loss_target, m_c_ctx, m_w_ada, m_b_ada, m_norm1_g, m_w_in, m_na_rpb, m_hg_lb_logits, m_hg_norm_g, m_w_pa, m_w_pb, m_w_out, m_norm2_g, m_w_ffn_in, m_w_ffn_out, m_final_g, v_c_ctx, v_w_ada, v_b_ada, v_norm1_g, v_w_in, v_na_rpb, v_hg_lb_logits, v_hg_norm_g, v_w_pa, v_w_pb, v_w_out, v_norm2_g, v_w_ffn_in, v_w_ffn_out, v_final_g):
    xi, yi, ci = lax.axis_index("x"), lax.axis_index("y"), lax.axis_index("c")
    sidx = 2 * xi + yi
    eidx = 4 * xi + 2 * yi + ci
    cidx = jnp.reshape(ci, (1,)).astype(jnp.int32)

    S, D = x.shape[1], x.shape[2]
    L = ctx.shape[1]
    T = S + L
    naw = NA_HEADS * NA_HEAD_DIM
    hgf = HG_HEADS * HG_DIM
    inw = 3 * naw + 5 * hgf + 2 * D
    fh = w_ffn_out.shape[1] * 4
    ads = w_ada.shape[2]
    fs = hg_lb_logits.shape[2]
    rows = S // GRID_W
    tr = _tile(L, 256)
    nlat, nall = S // tr, T // tr
    assert naw == hgf and D % naw == 0 and S % tr == 0 and 2 * hgf <= D

    pack0 = jnp.concatenate([c, jnp.pad(hg_lb_logits.reshape(1, -1), ((0, 0), (0, D - 4 * fs))),
                             jnp.zeros((6, D), F32)], axis=0)
    g0 = _all_gather8(pack0).reshape(8, 8, D)
    cs = g0[:, 0]
    lbl = g0[::2, 1, :4 * fs].reshape(4, 2, 2, fs).transpose(1, 2, 0, 3).reshape(2, 2, 4 * fs)
    p_lb = jax.nn.softmax(lbl, axis=0)
    lb = p_lb[0]
    lbb = [jnp.broadcast_to(lb[d].reshape(HG_HEADS, 1, HG_DIM), (HG_HEADS, 8, HG_DIM)) for d in range(2)]

    cin = jnp.concatenate([cs, c_ctx[None], jnp.zeros((7, D), F32)], axis=0)
    b_sh = lax.dynamic_slice(b_ada, (0, sidx * ads), (1, ads))
    modp = _ada_fwd(cin, w_ada[0], b_sh)
    modfull = _all_gather8(modp).reshape(8, 16, ads)[::2].transpose(1, 0, 2).reshape(16, 4 * ads)
    mod_e = jnp.pad(lax.dynamic_index_in_dim(modfull, eidx, 0, keepdims=False).reshape(N_MOD, D), ((0, 2), (0, 0)))
    mod_c = jnp.pad(modfull[8].reshape(N_MOD, D), ((0, 2), (0, 0)))

    def halves(w2):
        return w2.astype(BF16).reshape(2, w2.shape[0] // 2, w2.shape[1])

    gathered = _gather_weights([halves(w[0]) for w in (w_in, w_pa, w_pb, w_out, w_ffn_in, w_ffn_out)])
    win3, wpa3, wpb3, wout3, wi3, wfo3 = [g.reshape(4, 2 * g.shape[2], g.shape[3]) for g in gathered]
    wout1 = wout3.reshape(1, D, D)
    wfo1 = wfo3.reshape(1, fh, D)

    xx = jnp.concatenate([x[0], ctx[0]], axis=0)

    def f_ln1(i, rv, vv):
        xt, = rv
        g, me, mc = vv
        isc = i >= nlat
        sh = jnp.where(isc, mc[0:1], me[0:1])
        sc = jnp.where(isc, mc[1:2], me[1:2])
        return [xt * _rms(xt) * g * (1.0 + sc) + sh], []

    hb, = _rowwise(f_ln1, nall, tr, [(xx, D, 0, None)], [norm1_g, mod_e, mod_c], [(D, BF16)], [], "ln1")
    u = _mm_nn(hb, win3, F32, "mm_in")

    scale = NA_HEAD_DIM ** -0.5

    def f_qkv(i, rv, vv):
        q, k, v = rv
        return [q * scale, k, v], []

    qs, kb, vb = _rowwise(f_qkv, nall, tr, [(u, naw, 0, None), (u, naw, 1, None), (u, naw, 2, None)], [],
                          [(naw, BF16)] * 3, [], "qkv_cast")
    bias_fn = functools.partial(_bias_tables, rows=rows)
    bias, bias_vjp = jax.vjp(bias_fn, na_rpb[0])
    o_na, lse = _na_fwd(qs, kb, vb, bias, S, L)

    o_f, st_f = _hg_fwd(u, lbb[0], S, L, naw, hgf, False)
    o_b, st_b = _hg_fwd(u, lbb[1], S, L, naw, hgf, True)

    hgn = jnp.tile(hg_norm_g, (1, HG_HEADS))
    hog_cb = (3 * naw + 4 * hgf) // hgf
    ga_cb = (3 * naw + 5 * hgf) // D
    gb_cb = ga_cb + 1

    def heads_rms(o):
        return jnp.concatenate([jnp.broadcast_to(_rms(o[:, h * HG_DIM:(h + 1) * HG_DIM]), (o.shape[0], HG_DIM))
                                for h in range(HG_HEADS)], axis=1)

    def f_readout(i, rv, vv):
        of, ob_, hog = rv
        g, = vv
        o = of + ob_
        return [o * heads_rms(o) * g * _silu(hog)], []

    ob, = _rowwise(f_readout, nlat, tr, [(o_f, hgf, 0, None), (o_b, hgf, 0, None), (u, hgf, hog_cb, None)],
                   [hgn], [(hgf, BF16)], [], "hg_readout")

    ya = _mm_nn(o_na, wpa3, F32, "mm_pa")
    yb = _mm_nn(ob, wpb3, F32, "mm_pb")

    def f_merge(i, rv, vv):
        ya_, yb_, ga, gb = rv
        return [_sigmoid(ga) * ya_ + _sigmoid(gb) * yb_], []

    yv, = _rowwise(f_merge, nlat, tr, [(ya, D, 0, None), (yb, D, 0, None), (u, D, ga_cb, None), (u, D, gb_cb, None)],
                   [], [(D, BF16)], [], "merge")
    z = _mm_nn(yv, wout1, F32, "mm_out")

    def f_res1(i, rv, vv):
        xt, zt = rv
        g, me = vv
        x1 = xt + me[2:3] * zt
        return [x1, x1 * _rms(x1) * g * (1.0 + me[4:5]) + me[3:4]], []

    x1, h2 = _rowwise(f_res1, nlat, tr, [(xx, D, 0, None), (z, D, 0, None)], [norm2_g, mod_e],
                      [(D, F32), (D, BF16)], [], "res1_ln2")
    au = _mm_nn(h2, wi3, F32, "mm_ffn_in")
    tf = _tile(L, 128)
    nlf = S // tf

    def f_swiglu(i, rv, vv):
        a, uu = rv
        return [_silu(a) * uu], []

    sw, = _rowwise(f_swiglu, nlf, tf, [(au, fh, 0, None), (au, fh, 1, None)], [], [(fh, BF16)], [], "swiglu")
    ff = _mm_nn(sw, wfo1, F32, "mm_ffn_out")

    fg = final_g.reshape(1, D)

    def f_final(i, rv, vv):
        x1t, ft, tg = rv
        g, me = vv
        x2 = x1t + me[5:6] * ft
        r3 = _rms(x2)
        xn = x2 * r3
        err = xn * g - tg
        dyy = err * (1.0 / D)
        dxn = dyy * g
        dx2 = r3 * (dxn - xn * jnp.mean(dxn * xn, axis=-1, keepdims=True))
        return [dx2, dx2 * me[5:6]], [_colsum(err * err), _colsum(dyy * xn), _colsum(dx2 * ft)]

    dx2, dfb, loss_cols, dfg, dg2 = _rowwise(
        f_final, nlat, tr, [(x1, D, 0, None), (ff, D, 0, None), (loss_target[0], D, 0, None)], [fg, mod_e],
        [(D, F32), (D, BF16)], [D, D, D], "final_loss")

    dsw = _mm_nt(dfb, wfo1, F32, "mm_d_sw")

    def f_dswiglu(i, rv, vv):
        d, a, uu = rv
        return [jnp.concatenate([d * uu * _dsilu(a), d * _silu(a)], axis=1)], []

    dau, = _rowwise(f_dswiglu, nlf, tf, [(dsw, fh, 0, None), (au, fh, 0, None), (au, fh, 1, None)], [],
                    [(2 * fh, BF16)], [], "swiglu_bwd")
    g_wfo = _mm_tn(sw, dfb, 1, "mm_dw_ffn_out").reshape(4, 2, fh // 8, D)
    dh2 = _mm_nt(dau, wi3, F32, "mm_d_h2")
    g_wi = _mm_tn(h2, dau, 4, "mm_dw_ffn_in")

    def f_ln2_bwd(i, rv, vv):
        dh, x1t, dx2t, zt = rv
        g, me = vv
        r2 = _rms(x1t)
        xn = x1t * r2
        dxn = dh * g * (1.0 + me[4:5])
        dx1 = dx2t + r2 * (dxn - xn * jnp.mean(dxn * xn, axis=-1, keepdims=True))
        return ([dx1, dx1 * me[2:3]],
                [_colsum(dh), _colsum(dh * xn * g), _colsum(dh * xn * (1.0 + me[4:5])), _colsum(dx1 * zt)])

    dx1, dzb, dsh2, dsc2, dn2g, dg1 = _rowwise(
        f_ln2_bwd, nlat, tr, [(dh2, D, 0, None), (x1, D, 0, None), (dx2, D, 0, None), (z, D, 0, None)],
        [norm2_g, mod_e], [(D, F32), (D, BF16)], [D, D, D, D], "ln2_bwd")

    dy = _mm_nt(dzb, wout1, F32, "mm_d_y")
    g_wout = _mm_tn(yv, dzb, 1, "mm_dw_out").reshape(4, 2, D // 8, D)

    def f_dmerge(i, rv, vv):
        d, ya_, yb_, ga, gb = rv
        sa, sb_ = _sigmoid(ga), _sigmoid(gb)
        return [d * sa, d * sb_, d * ya_ * sa * (1.0 - sa), d * yb_ * sb_ * (1.0 - sb_)], []

    dya, dyb, dga, dgb = _rowwise(
        f_dmerge, nlat, tr, [(dy, D, 0, None), (ya, D, 0, None), (yb, D, 0, None), (u, D, ga_cb, None),
                             (u, D, gb_cb, None)], [], [(D, BF16)] * 4, [], "merge_bwd")
    d_ona = _mm_nt(dya, wpa3, BF16, "mm_d_ona")
    d_ob = _mm_nt(dyb, wpb3, F32, "mm_d_ob")
    g_wpa = _mm_tn(o_na, dya, 4, "mm_dw_pa")
    g_wpb = _mm_tn(ob, dyb, 4, "mm_dw_pb")

    def f_dreadout(i, rv, vv):
        d, of, ob_, hog = rv
        g, = vv
        o = of + ob_
        on = o * heads_rms(o)
        t = d * _silu(hog) * g
        mt = jnp.concatenate([jnp.broadcast_to(jnp.mean((t * on)[:, h * HG_DIM:(h + 1) * HG_DIM], axis=-1,
                                                        keepdims=True), (o.shape[0], HG_DIM))
                              for h in range(HG_HEADS)], axis=1)
        do_ = heads_rms(o) * (t - on * mt)
        return [do_, d * on * g * _dsilu(hog)], [_colsum(d * _silu(hog) * on)]

    do_hg, dhog, dhgn = _rowwise(
        f_dreadout, nlat, tr, [(d_ob, hgf, 0, None), (o_f, hgf, 0, None), (o_b, hgf, 0, None),
                               (u, hgf, hog_cb, None)], [hgn], [(hgf, F32), (hgf, BF16)], [hgf], "hg_readout_bwd")

    dq_f, dz_f, dv_f, dlb_f = _hg_bwd(u, lbb[0], st_f, do_hg, S, L, naw, hgf, False)
    dq_b, dz_b, dv_b, dlb_b = _hg_bwd(u, lbb[1], st_b, do_hg, S, L, naw, hgf, True)
    dq_na, dk_na, dv_na, dbias = _na_bwd(qs, kb, vb, bias, d_ona, o_na, lse, S, L)

    ta = _tile(L, 128)
    nla, naa = S // ta, T // ta
    lat = lambda i: jnp.minimum(i, nla - 1)

    def f_assemble(i, rv, vv):
        dqn, dk, dv, dqf, dqb, dzf, dzb_, dvf, dvb, dho, dga_, dgb_ = rv
        keep = (i < nla).astype(F32)
        return [jnp.concatenate([dqn * (scale * keep), dk, dv, dqf + dqb, dzf, dzb_, dvf + dvb,
                                 dho.astype(F32) * keep, dga_.astype(F32) * keep, dgb_.astype(F32) * keep],
                                axis=1)], []

    du, = _rowwise(
        f_assemble, naa, ta,
        [(dq_na, naw, 0, lat), (dk_na, naw, 0, None), (dv_na, naw, 0, None), (dq_f, hgf, 0, None),
         (dq_b, hgf, 0, None), (dz_f, hgf, 0, None), (dz_b, hgf, 0, None), (dv_f, hgf, 0, None),
         (dv_b, hgf, 0, None), (dhog, hgf, 0, lat), (dga, D, 0, lat), (dgb, D, 0, lat)],
        [], [(inw, BF16)], [], "assemble_du")

    dh = _mm_nt(du, win3, F32, "mm_d_h")
    g_win = _mm_tn(hb, du, 4, "mm_dw_in")

    def f_ln1_bwd(i, rv, vv):
        dht, xt, dx1t = rv
        g, me = vv
        r1 = _rms(xt)
        xn = xt * r1
        dxn = dht * g * (1.0 + me[1:2])
        dx = dx1t + r1 * (dxn - xn * jnp.mean(dxn * xn, axis=-1, keepdims=True))
        return [dx], [_colsum(dht), _colsum(dht * xn * g), _colsum(dht * xn * (1.0 + me[1:2]))]

    grad_x, dsh1, dsc1, dn1g_l = _rowwise(
        f_ln1_bwd, nlat, tr, [(dh, D, 0, None), (xx, D, 0, None), (dx1, D, 0, None)], [norm1_g, mod_e],
        [(D, F32)], [D, D, D], "ln1_bwd")

    def f_ln1_bwd_ctx(i, rv, vv):
        dht, xt = rv
        g, mc = vv
        xn = xt * _rms(xt)
        return [], [_colsum(dht), _colsum(dht * xn * g), _colsum(dht * xn * (1.0 + mc[1:2]))]

    ctx_rows = lambda i: i + nlat
    dsh1c, dsc1c, dn1g_c = _rowwise(
        f_ln1_bwd_ctx, nall - nlat, tr, [(dh, D, 0, ctx_rows), (xx, D, 0, ctx_rows)], [norm1_g, mod_c],
        [], [D, D, D], "ln1_bwd_ctx")

    drpb = bias_vjp(dbias)[0].reshape(1, -1)
    nrp = -(-drpb.shape[1] // D)
    drpb_rows = jnp.pad(drpb, ((0, 0), (0, nrp * D - drpb.shape[1]))).reshape(nrp, D)
    dlb = jnp.concatenate([dlb_f.reshape(1, hgf), dlb_b.reshape(1, hgf)], axis=1)
    dhg = jnp.sum(dhgn.reshape(HG_HEADS, HG_DIM), axis=0, keepdims=True)

    def wide(v):
        return jnp.pad(v, ((0, 0), (0, D - v.shape[1])))

    pack_rows = [loss_cols, dfg, dn2g, dn1g_l + dn1g_c, dsh1, dsc1, dg1, dsh2, dsc2, dg2, dsh1c, dsc1c,
                 wide(dhg), wide(dlb), drpb_rows]
    pack = jnp.concatenate(pack_rows, axis=0)
    npk = -(-pack.shape[0] // 8) * 8
    pack = jnp.pad(pack, ((0, npk - pack.shape[0]), (0, 0)))
    gp = _all_gather8(pack).reshape(8, npk, D)
    tot = _sum8(gp, "sum_small_grads")

    loss = (0.5 / D) * jnp.sum(tot[0])
    grad_final_g = tot[1]
    grad_norm2_g = tot[2:3]
    grad_norm1_g = tot[3:4]
    grad_hg_norm_g = tot[12:13, :HG_DIM]
    dlb_tot = tot[13, :2 * hgf].reshape(2, hgf)
    grad_na_rpb = tot[14:14 + nrp].reshape(-1)[:drpb.shape[1]].reshape(na_rpb.shape)
    dlog = jnp.stack([dlb_tot * p_lb[0] * (1.0 - p_lb[0]), -dlb_tot * p_lb[0] * p_lb[1]], axis=0)
    grad_hg_lb = lax.dynamic_slice(dlog, (0, 0, sidx * fs), (2, 2, fs))

    dmod_all = gp[:, 4:10].reshape(8, N_MOD * D)
    dmod_ctx = jnp.concatenate([tot[10], tot[11], jnp.zeros((4 * D,), F32)])[None]
    dm16 = jnp.concatenate([dmod_all, dmod_ctx, jnp.zeros((7, N_MOD * D), F32)], axis=0)
    grad_b_ada = jnp.sum(dm16, axis=0, keepdims=True)
    dm_sh = lax.dynamic_slice(dm16, (0, sidx * ads), (16, ads))
    g_wada, dcin = _ada_bwd(cin, w_ada[0], dm_sh)
    gc = _all_gather8(dcin[8:16]).reshape(8, 8, D)
    grad_c_ctx = (gc[0, 0] + gc[2, 0] + gc[4, 0] + gc[6, 0]) * _dsilu(c_ctx)

    names = ["w_in", "w_pa", "w_pb", "w_out", "w_ffn_in", "w_ffn_out"]
    g_win, g_wpa, g_wpb, g_wout, g_wi, g_wfo = _reduce_scatter(
        cidx, [g_win, g_wpa, g_wpb, g_wout, g_wi, g_wfo], names)

    grads = {
        "c_ctx": grad_c_ctx, "w_ada": g_wada[None], "b_ada": grad_b_ada, "norm1_g": grad_norm1_g,
        "w_in": g_win[None], "na_rpb": grad_na_rpb, "hg_lb_logits": grad_hg_lb, "hg_norm_g": grad_hg_norm_g,
        "w_pa": g_wpa[None], "w_pb": g_wpb[None], "w_out": g_wout[None], "norm2_g": grad_norm2_g,
        "w_ffn_in": g_wi[None], "w_ffn_out": g_wfo[None], "final_g": grad_final_g,
    }
    weights = {
        "c_ctx": (c_ctx, m_c_ctx, v_c_ctx), "w_ada": (w_ada, m_w_ada, v_w_ada), "b_ada": (b_ada, m_b_ada, v_b_ada),
        "norm1_g": (norm1_g, m_norm1_g, v_norm1_g), "w_in": (w_in, m_w_in, v_w_in),
        "na_rpb": (na_rpb, m_na_rpb, v_na_rpb), "hg_lb_logits": (hg_lb_logits, m_hg_lb_logits, v_hg_lb_logits),
        "hg_norm_g": (hg_norm_g, m_hg_norm_g, v_hg_norm_g), "w_pa": (w_pa, m_w_pa, v_w_pa),
        "w_pb": (w_pb, m_w_pb, v_w_pb), "w_out": (w_out, m_w_out, v_w_out),
        "norm2_g": (norm2_g, m_norm2_g, v_norm2_g), "w_ffn_in": (w_ffn_in, m_w_ffn_in, v_w_ffn_in),
        "w_ffn_out": (w_ffn_out, m_w_ffn_out, v_w_ffn_out), "final_g": (final_g, m_final_g, v_final_g),
    }
    order = list(weights)
    deltas, new_ms, new_vs = [], [], []
    for nm in order:
        w, m, v = weights[nm]
        g = grads[nm].reshape(w.shape)
        grads[nm] = g
        if w.ndim == 3 and w.shape[0] == 1:
            d_, m_, v_ = _adamw(w[0], g[0], m[0], v[0], "adamw_" + nm)
            d_, m_, v_ = d_[None], m_[None], v_[None]
        else:
            d_, m_, v_ = _adamw(w, g, m, v, "adamw_" + nm)
        deltas.append(d_)
        new_ms.append(m_)
        new_vs.append(v_)

    return (loss, grad_x[None], *[grads[nm] for nm in order], *deltas, *new_ms, *new_vs)
```

```python
import functools

import jax
import jax.numpy as jnp
from jax import lax
from jax.experimental import pallas as pl
from jax.experimental.pallas import tpu as pltpu

F32 = jnp.float32
BF16 = jnp.bfloat16

GRID_W = 64
WIN_H = 8
WIN_W = 16
NA_HEADS = 16
NA_HEAD_DIM = 64
HG_HEADS = 8
HG_DIM = 128
HG_CHUNK = 64
N_MOD = 6
EPS = 1e-6
ADAM_LR = 0.001
ADAM_B1 = 0.9
ADAM_B2 = 0.999
ADAM_EPS = 1e-08
ADAM_WD = 0.01
ADAM_STEP = 10

LANES = 128
NA_ROWS_PER_STEP = 4
VMEM_LIMIT = 56 * 1024 * 1024
MASK_VALUE = -1e30
EXP_CLAMP = 80.0
MESH_ID = pl.DeviceIdType.MESH
HI = lax.Precision.HIGHEST


def _tile(dim, target, mult=LANES):
    best = None
    t = mult
    while t <= min(dim, target):
        if dim % t == 0:
            best = t
        t += mult
    assert best is not None, (dim, target, mult)
    return best


def _params(sem):
    return pltpu.CompilerParams(dimension_semantics=sem, vmem_limit_bytes=VMEM_LIMIT)


def _dot(a, b, precision=None):
    return jnp.dot(a, b, preferred_element_type=F32, precision=precision)


def _dot_nt(a, b, precision=None):
    return lax.dot_general(a, b, (((1,), (1,)), ((), ())), preferred_element_type=F32, precision=precision)


def _dot_tn(a, b, precision=None):
    return lax.dot_general(a, b, (((0,), (0,)), ((), ())), preferred_element_type=F32, precision=precision)


def _sigmoid(v):
    return 1.0 / (1.0 + jnp.exp(-v))


def _mm_nn(a, b3, out_dtype, name):
    M, K = a.shape
    nsh, _, Ns = b3.shape
    tm, tn, tk = _tile(M, 1024), _tile(Ns, 1408), _tile(K, 2048)
    tps, nk = Ns // tn, K // tk

    def body(a_ref, b_ref, o_ref, acc):
        k = pl.program_id(2)

        @pl.when(k == 0)
        def _():
            acc[...] = jnp.zeros_like(acc)

        acc[...] += _dot(a_ref[...], b_ref[...])

        @pl.when(k == nk - 1)
        def _():
            o_ref[...] = acc[...].astype(o_ref.dtype)

    return pl.pallas_call(
        body, name=name, grid=(M // tm, nsh * tps, nk),
        in_specs=[pl.BlockSpec((tm, tk), lambda m, n, k: (m, k)),
                  pl.BlockSpec((None, tk, tn), lambda m, n, k: (n // tps, k, n % tps))],
        out_specs=pl.BlockSpec((tm, tn), lambda m, n, k: (m, n)),
        out_shape=jax.ShapeDtypeStruct((M, nsh * Ns), out_dtype),
        scratch_shapes=[pltpu.VMEM((tm, tn), F32)],
        compiler_params=_params(("parallel", "parallel", "arbitrary")),
    )(a, b3)


def _mm_nt(a, b3, out_dtype, name):
    M = a.shape[0]
    nsh, Kw, Ns = b3.shape
    tm, tn, tk = _tile(M, 1024), _tile(Kw, 1024), _tile(Ns, 1536)
    kps = Ns // tk
    nk = nsh * kps

    def body(a_ref, b_ref, o_ref, acc):
        k = pl.program_id(2)

        @pl.when(k == 0)
        def _():
            acc[...] = jnp.zeros_like(acc)

        acc[...] += _dot_nt(a_ref[...], b_ref[...])

        @pl.when(k == nk - 1)
        def _():
            o_ref[...] = acc[...].astype(o_ref.dtype)

    return pl.pallas_call(
        body, name=name, grid=(M // tm, Kw // tn, nk),
        in_specs=[pl.BlockSpec((tm, tk), lambda m, n, k: (m, k)),
                  pl.BlockSpec((None, tn, tk), lambda m, n, k: (k // kps, n, k % kps))],
        out_specs=pl.BlockSpec((tm, tn), lambda m, n, k: (m, n)),
        out_shape=jax.ShapeDtypeStruct((M, Kw), out_dtype),
        scratch_shapes=[pltpu.VMEM((tm, tn), F32)],
        compiler_params=_params(("parallel", "parallel", "arbitrary")),
    )(a, b3)


def _mm_tn(a, g, nsh, name):
    Tk, M = a.shape
    Ns = g.shape[1] // nsh
    tm, tn, tk = _tile(M // 2, 1408), _tile(Ns, 1408), _tile(Tk, 1024)
    mh, tps, nk = (M // 2) // tm, Ns // tn, Tk // tk

    def body(a_ref, g_ref, o_ref, acc):
        k = pl.program_id(2)

        @pl.when(k == 0)
        def _():
            acc[...] = jnp.zeros_like(acc)

        acc[...] += _dot_tn(a_ref[...], g_ref[...])

        @pl.when(k == nk - 1)
        def _():
            o_ref[...] = acc[...]

    return pl.pallas_call(
        body, name=name, grid=(M // tm, nsh * tps, nk),
        in_specs=[pl.BlockSpec((tk, tm), lambda m, n, k: (k, m)),
                  pl.BlockSpec((tk, tn), lambda m, n, k: (k, n))],
        out_specs=pl.BlockSpec((None, None, tm, tn), lambda m, n, k: (n // tps, m // mh, m % mh, n % tps)),
        out_shape=jax.ShapeDtypeStruct((nsh, 2, M // 2, Ns), F32),
        scratch_shapes=[pltpu.VMEM((tm, tn), F32)],
        compiler_params=_params(("parallel", "parallel", "arbitrary")),
    )(a, g)


def _rowwise(fn, nblk, tm, rins, vins, routs, accs, name):
    nr, nv, no, na = len(rins), len(vins), len(routs), len(accs)

    def body(*refs):
        i = pl.program_id(0)
        outs, accv = fn(i, [r[...] for r in refs[:nr]], [r[...] for r in refs[nr:nr + nv]])
        for r, v in zip(refs[nr + nv:nr + nv + no], outs):
            r[...] = v.astype(r.dtype)
        arefs = refs[nr + nv + no:]
        if na:
            @pl.when(i == 0)
            def _():
                for a in arefs:
                    a[...] = jnp.zeros_like(a)

            for a, v in zip(arefs, accv):
                a[...] += v

    def row_spec(w, cb, rm):
        if rm is None:
            return pl.BlockSpec((tm, w), lambda i: (i, cb))
        return pl.BlockSpec((tm, w), lambda i: (rm(i), cb))

    in_specs = [row_spec(w, cb, rm) for (_, w, cb, rm) in rins]
    in_specs += [pl.BlockSpec(v.shape, lambda i: (0, 0)) for v in vins]
    out_specs = [pl.BlockSpec((tm, w), lambda i: (i, 0)) for (w, _) in routs]
    out_specs += [pl.BlockSpec((1, w), lambda i: (0, 0)) for w in accs]
    out_shape = [jax.ShapeDtypeStruct((nblk * tm, w), dt) for (w, dt) in routs]
    out_shape += [jax.ShapeDtypeStruct((1, w), F32) for w in accs]
    res = pl.pallas_call(
        body, name=name, grid=(nblk,), in_specs=in_specs, out_specs=out_specs, out_shape=out_shape,
        compiler_params=_params(("arbitrary",)),
    )(*[r[0] for r in rins], *vins)
    return list(res)


def _colsum(v):
    return jnp.sum(v, axis=0, keepdims=True)


def _rms(v):
    return lax.rsqrt(jnp.mean(v * v, axis=-1, keepdims=True) + EPS)


def _all_gather8(xs):
    m_per, n = xs.shape

    def body(x_ref, out_ref, send_sems, recv_sems, local_sem):
        x, y, c = lax.axis_index("x"), lax.axis_index("y"), lax.axis_index("c")
        me, sibling = (x, y, c), (x, y, 1 - c)
        chips = [(1 - x, y), (x, 1 - y), (1 - x, 1 - y)]

        def rows(px, py, pc):
            return out_ref.at[pl.ds((4 * px + 2 * py + pc) * m_per, m_per), :]

        def copy(k, block, to, src=None):
            return pltpu.make_async_remote_copy(
                src_ref=rows(*block) if src is None else src, dst_ref=rows(*block),
                send_sem=send_sems.at[k], recv_sem=recv_sems.at[k], device_id=to, device_id_type=MESH_ID)

        mine = pltpu.make_async_copy(x_ref, rows(*me), local_sem)
        mine.start()
        first = [copy(0, me, sibling, src=x_ref)]
        first += [copy(1 + j, me, (*chip, c), src=x_ref) for j, chip in enumerate(chips)]
        for cp in first:
            cp.start()
        passed = [copy(4 + j, (*chip, c), sibling) for j, chip in enumerate(chips)]
        for j, chip in enumerate(chips):
            copy(1 + j, (*chip, c), me).wait_recv()
            passed[j].start()
        copy(0, sibling, me).wait_recv()
        for j, chip in enumerate(chips):
            copy(4 + j, (*chip, 1 - c), me).wait_recv()
        for cp in first + passed:
            cp.wait_send()
        mine.wait()

    return pl.pallas_call(
        body, name="all_gather8_%dx%d" % (m_per, n),
        out_shape=jax.ShapeDtypeStruct((8 * m_per, n), xs.dtype),
        in_specs=[pl.BlockSpec(memory_space=pltpu.VMEM)],
        out_specs=pl.BlockSpec(memory_space=pltpu.VMEM),
        scratch_shapes=[pltpu.SemaphoreType.DMA((7,)), pltpu.SemaphoreType.DMA((7,)), pltpu.SemaphoreType.DMA],
    )(xs)


def _mesh_pos():
    x, y, c = lax.axis_index("x"), lax.axis_index("y"), lax.axis_index("c")
    chips = [(1 - x, y), (x, 1 - y), (1 - x, 1 - y)]
    return x, y, c, chips


def _gather_weights(ws):
    n = len(ws)

    def body(*refs):
        w, out = refs[:n], refs[n:2 * n]
        send_sems, recv_sems, local_sems = refs[2 * n:]
        x, y, c, chips = _mesh_pos()
        s = 2 * x + y
        sibling = (x, y, 1 - c)

        def copy(i, k, shard, half, to, src=None):
            dst = out[i].at[shard, half]
            return pltpu.make_async_remote_copy(
                src_ref=dst if src is None else src, dst_ref=dst,
                send_sem=send_sems.at[6 * i + k], recv_sem=recv_sems.at[6 * i + k],
                device_id=to, device_id_type=MESH_ID)

        mine = [pltpu.make_async_copy(w[i], out[i].at[s], local_sems.at[i]) for i in range(n)]
        for cp in mine:
            cp.start()
        first = [copy(i, j, s, c, (*chip, c), src=w[i].at[c]) for i in range(n) for j, chip in enumerate(chips)]
        for cp in first:
            cp.start()
        passed = []
        for j, chip in enumerate(chips):
            sj = 2 * chip[0] + chip[1]
            for i in range(n):
                copy(i, j, sj, c, (x, y, c)).wait_recv()
                cp = copy(i, 3 + j, sj, c, sibling)
                cp.start()
                passed.append(cp)
        for j, chip in enumerate(chips):
            sj = 2 * chip[0] + chip[1]
            for i in range(n):
                copy(i, 3 + j, sj, 1 - c, (x, y, c)).wait_recv()
        for cp in first + passed:
            cp.wait_send()
        for cp in mine:
            cp.wait()

    any_spec = pl.BlockSpec(memory_space=pl.ANY)
    return pl.pallas_call(
        body, name="gather_weights",
        out_shape=[jax.ShapeDtypeStruct((4,) + w.shape, w.dtype) for w in ws],
        in_specs=[any_spec] * n, out_specs=[any_spec] * n,
        scratch_shapes=[pltpu.SemaphoreType.DMA((6 * n,)), pltpu.SemaphoreType.DMA((6 * n,)),
                        pltpu.SemaphoreType.DMA((n,))],
    )(*ws)


def _swap_other_half(gs):
    n = len(gs)

    def body(*refs):
        g, land = refs[:n], refs[n:2 * n]
        send_sems, recv_sems = refs[2 * n:]
        x, y, c, _ = _mesh_pos()
        cps = [pltpu.make_async_remote_copy(
            src_ref=g[i].at[:, 1 - c], dst_ref=land[i], send_sem=send_sems.at[i], recv_sem=recv_sems.at[i],
            device_id=(x, y, 1 - c), device_id_type=MESH_ID) for i in range(n)]
        for cp in cps:
            cp.start()
        for cp in cps:
            cp.wait()

    any_spec = pl.BlockSpec(memory_space=pl.ANY)
    return pl.pallas_call(
        body, name="rs_swap_other_half",
        out_shape=[jax.ShapeDtypeStruct((4,) + g.shape[2:], g.dtype) for g in gs],
        in_specs=[any_spec] * n, out_specs=[any_spec] * n,
        scratch_shapes=[pltpu.SemaphoreType.DMA((n,)), pltpu.SemaphoreType.DMA((n,))],
    )(*gs)


def _scatter_pieces(ps):
    n = len(ps)

    def body(*refs):
        p, land = refs[:n], refs[n:2 * n]
        send_sems, recv_sems, local_sems = refs[2 * n:]
        x, y, c, chips = _mesh_pos()
        s = 2 * x + y

        def copy(i, j, chip):
            sj = 2 * chip[0] + chip[1]
            return pltpu.make_async_remote_copy(
                src_ref=p[i].at[sj], dst_ref=land[i].at[s],
                send_sem=send_sems.at[3 * i + j], recv_sem=recv_sems.at[3 * i + j],
                device_id=(*chip, c), device_id_type=MESH_ID)

        def arrival(i, j, chip):
            sj = 2 * chip[0] + chip[1]
            return pltpu.make_async_remote_copy(
                src_ref=land[i].at[sj], dst_ref=land[i].at[sj],
                send_sem=send_sems.at[3 * i + j], recv_sem=recv_sems.at[3 * i + j],
                device_id=(x, y, c), device_id_type=MESH_ID)

        mine = [pltpu.make_async_copy(p[i].at[s], land[i].at[s], local_sems.at[i]) for i in range(n)]
        for cp in mine:
            cp.start()
        sends = [copy(i, j, chip) for i in range(n) for j, chip in enumerate(chips)]
        for cp in sends:
            cp.start()
        for i in range(n):
            for j, chip in enumerate(chips):
                arrival(i, j, chip).wait_recv()
        for cp in sends:
            cp.wait_send()
        for cp in mine:
            cp.wait()

    any_spec = pl.BlockSpec(memory_space=pl.ANY)
    return pl.pallas_call(
        body, name="rs_scatter_pieces",
        out_shape=[jax.ShapeDtypeStruct(p.shape, p.dtype) for p in ps],
        in_specs=[any_spec] * n, out_specs=[any_spec] * n,
        scratch_shapes=[pltpu.SemaphoreType.DMA((3 * n,)), pltpu.SemaphoreType.DMA((3 * n,)),
                        pltpu.SemaphoreType.DMA((n,))],
    )(*ps)


def _swap_result_half(rs):
    n = len(rs)

    def body(*refs):
        r, out = refs[:n], refs[n:2 * n]
        send_sems, recv_sems, local_sems = refs[2 * n:]
        x, y, c, _ = _mesh_pos()
        mine = [pltpu.make_async_copy(r[i], out[i].at[c], local_sems.at[i]) for i in range(n)]
        for cp in mine:
            cp.start()
        cps = [pltpu.make_async_remote_copy(
            src_ref=r[i], dst_ref=out[i].at[c], send_sem=send_sems.at[i], recv_sem=recv_sems.at[i],
            device_id=(x, y, 1 - c), device_id_type=MESH_ID) for i in range(n)]
        for cp in cps:
            cp.start()
        for i in range(n):
            pltpu.make_async_remote_copy(
                src_ref=r[i], dst_ref=out[i].at[1 - c], send_sem=send_sems.at[i], recv_sem=recv_sems.at[i],
                device_id=(x, y, c), device_id_type=MESH_ID).wait_recv()
        for cp in cps:
            cp.wait_send()
        for cp in mine:
            cp.wait()

    any_spec = pl.BlockSpec(memory_space=pl.ANY)
    return pl.pallas_call(
        body, name="rs_swap_result_half",
        out_shape=[jax.ShapeDtypeStruct((2,) + r.shape, r.dtype) for r in rs],
        in_specs=[any_spec] * n, out_specs=[any_spec] * n,
        scratch_shapes=[pltpu.SemaphoreType.DMA((n,)), pltpu.SemaphoreType.DMA((n,)),
                        pltpu.SemaphoreType.DMA((n,))],
    )(*rs)


def _add_own_half(cidx, g, land, name):
    _, _, rh, cw = g.shape
    tm = _tile(rh, max(16, (1 << 20) // (4 * cw)), 16)

    def body(c_ref, g_ref, l_ref, o_ref):
        o_ref[...] = (g_ref[...] + l_ref[...]).astype(o_ref.dtype)

    return pl.pallas_call(
        body, name=name,
        grid_spec=pltpu.PrefetchScalarGridSpec(
            num_scalar_prefetch=1, grid=(4, rh // tm),
            in_specs=[pl.BlockSpec((None, None, tm, cw), lambda s, i, c_ref: (s, c_ref[0], i, 0)),
                      pl.BlockSpec((None, tm, cw), lambda s, i, c_ref: (s, i, 0))],
            out_specs=pl.BlockSpec((None, tm, cw), lambda s, i, c_ref: (s, i, 0))),
        out_shape=jax.ShapeDtypeStruct((4, rh, cw), BF16),
        compiler_params=_params(("parallel", "parallel")),
    )(cidx, g, land)


def _sum_pieces(land, name):
    _, rh, cw = land.shape
    tm = _tile(rh, max(16, (1 << 20) // (4 * cw)), 16) if rh % 16 == 0 else rh

    def body(l_ref, o_ref):
        v = l_ref[...].astype(F32)
        o_ref[...] = (v[0] + v[1]) + (v[2] + v[3])

    return pl.pallas_call(
        body, name=name, grid=(rh // tm,),
        in_specs=[pl.BlockSpec((4, tm, cw), lambda i: (0, i, 0))],
        out_specs=pl.BlockSpec((tm, cw), lambda i: (i, 0)),
        out_shape=jax.ShapeDtypeStruct((rh, cw), F32),
        compiler_params=_params(("parallel",)),
    )(land)


def _reduce_scatter(cidx, gs, names):
    lands = _swap_other_half(gs)
    parts = [_add_own_half(cidx, g, l, "rs_add_" + nm) for g, l, nm in zip(gs, lands, names)]
    pieces = _scatter_pieces(parts)
    halves = [_sum_pieces(p, "rs_sum_" + nm) for p, nm in zip(pieces, names)]
    fulls = _swap_result_half(halves)
    return [f.reshape(2 * f.shape[1], f.shape[2]) for f in fulls]


def _sum8(g, name):
    def body(g_ref, o_ref):
        acc = g_ref[0]
        for k in range(1, 8):
            acc = acc + g_ref[k]
        o_ref[...] = acc

    return pl.pallas_call(body, name=name, out_shape=jax.ShapeDtypeStruct(g.shape[1:], F32))(g)


def _silu(v):
    return v * _sigmoid(v)


def _dsilu(v):
    s = _sigmoid(v)
    return s * (1.0 + v * (1.0 - s))


def _ada_fwd(cin, w, b):
    d, ns = w.shape
    tn = _tile(ns, 512)

    def body(c_ref, w_ref, b_ref, o_ref):
        o_ref[...] = _dot(_silu(c_ref[...]), w_ref[...], HI) + b_ref[...]

    return pl.pallas_call(
        body, name="ada_fwd", grid=(ns // tn,),
        in_specs=[pl.BlockSpec(cin.shape, lambda n: (0, 0)), pl.BlockSpec((d, tn), lambda n: (0, n)),
                  pl.BlockSpec((1, tn), lambda n: (0, n))],
        out_specs=pl.BlockSpec((cin.shape[0], tn), lambda n: (0, n)),
        out_shape=jax.ShapeDtypeStruct((cin.shape[0], ns), F32),
        compiler_params=_params(("parallel",)),
    )(cin, w, b)


def _ada_bwd(cin, w, dm):
    d, ns = w.shape
    tn = _tile(ns, 512)

    def body(c_ref, w_ref, d_ref, dw_ref, dc_ref):
        n = pl.program_id(0)

        @pl.when(n == 0)
        def _():
            dc_ref[...] = jnp.zeros_like(dc_ref)

        dw_ref[...] = _dot_tn(_silu(c_ref[...]), d_ref[...], HI)
        dc_ref[...] += _dot_nt(d_ref[...], w_ref[...], HI)

    return pl.pallas_call(
        body, name="ada_bwd", grid=(ns // tn,),
        in_specs=[pl.BlockSpec(cin.shape, lambda n: (0, 0)), pl.BlockSpec((d, tn), lambda n: (0, n)),
                  pl.BlockSpec((cin.shape[0], tn), lambda n: (0, n))],
        out_specs=[pl.BlockSpec((d, tn), lambda n: (0, n)), pl.BlockSpec(cin.shape, lambda n: (0, 0))],
        out_shape=[jax.ShapeDtypeStruct((d, ns), F32), jax.ShapeDtypeStruct(cin.shape, F32)],
        compiler_params=_params(("arbitrary",)),
    )(cin, w, dm)


def _bias_tables(rpb, rows):
    kh = min(WIN_H, rows)
    col = jnp.arange(GRID_W)
    col_start = jnp.clip(col - WIN_W // 2, 0, GRID_W - WIN_W)
    in_win = (col[None, :] >= col_start[:, None]) & (col[None, :] < col_start[:, None] + WIN_W)
    dc_idx = jnp.clip(col[None, :] - col[:, None], 1 - WIN_W, WIN_W - 1) + WIN_W - 1
    dr_idx = jnp.arange(kh)[None, :] - jnp.arange(kh)[:, None] + WIN_H - 1
    t = rpb[:, dr_idx][:, :, :, dc_idx]
    t = jnp.where(in_win[None, None, None], t, MASK_VALUE)
    return t.transpose(0, 1, 3, 2, 4).reshape(rpb.shape[0], kh, GRID_W, kh * GRID_W).astype(F32)


def _na_geometry(S):
    rows = S // GRID_W
    kh = min(WIN_H, rows)

    def row_start(r):
        return jnp.clip(r - kh // 2, 0, rows - kh)

    return rows, kh, row_start


def _na_by_head(ref, rr, lane):
    t = ref[rr * GRID_W:(rr + 1) * GRID_W, :]
    zero = jnp.zeros_like(t)
    return jnp.concatenate([jnp.where(lane < NA_HEAD_DIM, t, zero), jnp.where(lane >= NA_HEAD_DIM, t, zero)], axis=0)


def _na_pick_head(t2, lane):
    return jnp.where(lane < NA_HEAD_DIM, t2[:GRID_W], t2[GRID_W:])


def _na_scores(q_ref, k_ref, b_ref, i, nrs, nb, S, L, row_start, lane):
    qh = jnp.concatenate([_na_by_head(q_ref, rr, lane) for rr in range(nrs)], axis=0)
    sc = _dot_nt(qh, k_ref[pl.ds(S, L), :])
    starts, shifts, sb = [], [], []
    for rr in range(nrs):
        r = i * nrs + rr
        rs = row_start(r)
        starts.append(pl.multiple_of(rs * GRID_W, GRID_W))
        shifts.append(r - rs)
        bias = jnp.concatenate([b_ref[0, r - rs], b_ref[1, r - rs]], axis=0)
        sb.append(_dot_nt(qh[rr * 2 * GRID_W:(rr + 1) * 2 * GRID_W], k_ref[pl.ds(starts[-1], nb), :]) + bias)
    return qh, jnp.concatenate(sb, axis=0), sc, starts, shifts


def _na_fwd(qs, kb, vb, bias, S, L):
    T, naw = qs.shape
    rows, kh, row_start = _na_geometry(S)
    nb = kh * GRID_W
    npair = naw // LANES

    nrs = min(NA_ROWS_PER_STEP, rows)
    assert rows % nrs == 0

    def body(q_ref, k_ref, v_ref, b_ref, o_ref, lse_ref):
        i = pl.program_id(1)
        lane = lax.broadcasted_iota(jnp.int32, (GRID_W, LANES), 1)
        _, sb, sc, starts, _ = _na_scores(q_ref, k_ref, b_ref, i, nrs, nb, S, L, row_start, lane)
        m = jnp.maximum(jnp.max(sb, axis=-1, keepdims=True), jnp.max(sc, axis=-1, keepdims=True))
        pb, pc = jnp.exp(sb - m), jnp.exp(sc - m)
        l = jnp.sum(pb, axis=-1, keepdims=True) + jnp.sum(pc, axis=-1, keepdims=True)
        inv = 1.0 / l
        pb16, pc16 = (pb * inv).astype(BF16), (pc * inv).astype(BF16)
        oc = _dot(pc16, v_ref[pl.ds(S, L), :])
        lse = jnp.broadcast_to(m + jnp.log(l), oc.shape)
        for rr in range(nrs):
            two = slice(rr * 2 * GRID_W, (rr + 1) * 2 * GRID_W)
            rsl = slice(rr * GRID_W, (rr + 1) * GRID_W)
            o2 = oc[two] + _dot(pb16[two], v_ref[pl.ds(starts[rr], nb), :])
            o_ref[rsl, :] = _na_pick_head(o2, lane).astype(o_ref.dtype)
            lse_ref[rsl, :] = _na_pick_head(lse[two], lane)

    blk = pl.BlockSpec((nrs * GRID_W, LANES), lambda p, i: (i, p))
    col = pl.BlockSpec((T, LANES), lambda p, i: (0, p))
    return pl.pallas_call(
        body, name="na_fwd", grid=(npair, rows // nrs),
        in_specs=[blk, col, col, pl.BlockSpec((2, kh, GRID_W, nb), lambda p, i: (p, 0, 0, 0))],
        out_specs=[blk, blk],
        out_shape=[jax.ShapeDtypeStruct((S, naw), BF16), jax.ShapeDtypeStruct((S, naw), F32)],
        compiler_params=_params(("parallel", "arbitrary")),
    )(qs, kb, vb, bias)


def _na_bwd(qs, kb, vb, bias, do, o, lse, S, L):
    T, naw = qs.shape
    rows, kh, row_start = _na_geometry(S)
    nb = kh * GRID_W
    npair = naw // LANES

    nrs = min(NA_ROWS_PER_STEP, rows)
    assert rows % nrs == 0

    def body(q_ref, k_ref, v_ref, b_ref, do_ref, o_ref, lse_ref, dq_ref, dk_ref, dv_ref, db_ref):
        i = pl.program_id(1)

        @pl.when(i == 0)
        def _():
            dk_ref[...] = jnp.zeros_like(dk_ref)
            dv_ref[...] = jnp.zeros_like(dv_ref)
            db_ref[...] = jnp.zeros_like(db_ref)

        lane = lax.broadcasted_iota(jnp.int32, (GRID_W, LANES), 1)
        qh, sb, sc, starts, shifts = _na_scores(q_ref, k_ref, b_ref, i, nrs, nb, S, L, row_start, lane)
        doh = jnp.concatenate([_na_by_head(do_ref, rr, lane) for rr in range(nrs)], axis=0)
        o2 = jnp.concatenate([o_ref[rr * GRID_W:(rr + 1) * GRID_W, :] for rr in range(nrs) for _ in range(2)], axis=0)
        lse = jnp.concatenate([lse_ref[rr * GRID_W:(rr + 1) * GRID_W, :][:, hh * NA_HEAD_DIM:hh * NA_HEAD_DIM + 1]
                               for rr in range(nrs) for hh in range(2)], axis=0)
        pb, pc = jnp.exp(sb - lse), jnp.exp(sc - lse)
        delta = jnp.sum(doh.astype(F32) * o2.astype(F32), axis=-1, keepdims=True)
        dpb = jnp.concatenate([_dot_nt(doh[rr * 2 * GRID_W:(rr + 1) * 2 * GRID_W], v_ref[pl.ds(starts[rr], nb), :])
                               for rr in range(nrs)], axis=0)
        dsb = pb * (dpb - delta)
        dsc = pc * (_dot_nt(doh, v_ref[pl.ds(S, L), :]) - delta)
        dsb16, dsc16, pb16, pc16 = dsb.astype(BF16), dsc.astype(BF16), pb.astype(BF16), pc.astype(BF16)
        dqc = _dot(dsc16, k_ref[pl.ds(S, L), :])
        dk_ref[pl.ds(S, L), :] += _dot_tn(dsc16, qh)
        dv_ref[pl.ds(S, L), :] += _dot_tn(pc16, doh)
        for rr in range(nrs):
            two = slice(rr * 2 * GRID_W, (rr + 1) * 2 * GRID_W)
            band = pl.ds(starts[rr], nb)
            dq2 = dqc[two] + _dot(dsb16[two], k_ref[band, :])
            dq_ref[rr * GRID_W:(rr + 1) * GRID_W, :] = _na_pick_head(dq2, lane)
            dk_ref[band, :] += _dot_tn(dsb16[two], qh[two])
            dv_ref[band, :] += _dot_tn(pb16[two], doh[two])
            for hh in range(2):
                db_ref[hh, shifts[rr]] += dsb[(2 * rr + hh) * GRID_W:(2 * rr + hh + 1) * GRID_W]

    blk = pl.BlockSpec((nrs * GRID_W, LANES), lambda p, r: (r, p))
    col = pl.BlockSpec((T, LANES), lambda p, r: (0, p))
    return pl.pallas_call(
        body, name="na_bwd", grid=(npair, rows // nrs),
        in_specs=[blk, col, col, pl.BlockSpec((2, kh, GRID_W, nb), lambda p, r: (p, 0, 0, 0)), blk, blk, blk],
        out_specs=[blk, col, col, pl.BlockSpec((2, kh, GRID_W, nb), lambda p, r: (p, 0, 0, 0))],
        out_shape=[jax.ShapeDtypeStruct((S, naw), F32), jax.ShapeDtypeStruct((T, naw), F32),
                   jax.ShapeDtypeStruct((T, naw), F32), jax.ShapeDtypeStruct(bias.shape, F32)],
        compiler_params=_params(("parallel", "arbitrary")),
    )(qs, kb, vb, bias, do, o, lse)


def _hg_cols(naw, hgf, rev):
    qcol = (3 * naw) // hgf
    fcol = (3 * naw + hgf * (2 if rev else 1)) // hgf
    icol = (3 * naw + 3 * hgf) // hgf
    return qcol, fcol, icol


def _hg_chunk_order(S, L, rev):
    ncl, ncc = S // HG_CHUNK, L // HG_CHUNK
    nc = ncl + ncc

    def chunk_of(i):
        if rev:
            return nc - 1 - i
        return jnp.where(i < ncc, ncl + i, i - ncc)

    return nc, ncl, chunk_of


def _hg_gates(q, z, lb, rev):
    row = lax.broadcasted_iota(jnp.int32, (HG_CHUNK, HG_CHUNK), 0)
    colm = lax.broadcasted_iota(jnp.int32, (HG_CHUNK, HG_CHUNK), 1)
    tri = (colm >= row) if rev else (row >= colm)
    trif = tri.astype(F32)
    sig = _sigmoid(z)
    f = lb + (1.0 - lb) * sig
    lf = jnp.log(f)
    k = 1.0 - f
    cum = _dot(trif, lf, HI)
    mid = cum[HG_CHUNK // 2:HG_CHUNK // 2 + 1, :]
    last = cum[0:1, :] if rev else cum[HG_CHUNK - 1:HG_CHUNK, :]
    eq = jnp.exp(jnp.clip(cum - mid, -EXP_CLAMP, EXP_CLAMP))
    ek = jnp.exp(jnp.clip(mid - cum, -EXP_CLAMP, EXP_CLAMP))
    return tri, trif, sig, f, k, cum, last, eq, ek


def _hg_fwd(u, lbr, S, L, naw, hgf, rev):
    T = S + L
    nh = hgf // HG_DIM
    nc, ncl, chunk_of = _hg_chunk_order(S, L, rev)
    qcol, fcol, icol = _hg_cols(naw, hgf, rev)

    def body(q_ref, z_ref, v_ref, lb_ref, o_ref, st_ref, state):
        i = pl.program_id(0)

        @pl.when(i == 0)
        def _():
            state[...] = jnp.zeros_like(state)

        q, z, v = q_ref[...], z_ref[...], v_ref[...]
        tri, _, _, _, k, cum, last, eq, ek = _hg_gates(q, z, lb_ref[...], rev)
        qe, ke = (q * eq).astype(BF16), (k * ek).astype(BF16)
        qd, kd = (q * jnp.exp(cum)).astype(BF16), (k * jnp.exp(last - cum)).astype(BF16)
        v16, el = v.astype(BF16), jnp.exp(last)
        for h in range(nh):
            sl = slice(h * HG_DIM, (h + 1) * HG_DIM)
            a = jnp.where(tri, _dot_nt(qe[:, sl], ke[:, sl]), 0.0)
            s0 = state[h]
            st_ref[h] = s0
            o_ref[:, sl] = _dot(a.astype(BF16), v16[:, sl]) + _dot_nt(qd[:, sl], s0.astype(BF16))
            state[h] = s0 * el[:, sl] + _dot_tn(v16[:, sl], kd[:, sl])

    def blk(cb):
        return pl.BlockSpec((HG_CHUNK, hgf), lambda i: (chunk_of(i), cb))

    return pl.pallas_call(
        body, name="hg_fwd_rev" if rev else "hg_fwd", grid=(nc,),
        in_specs=[blk(qcol), blk(fcol), blk(icol), pl.BlockSpec((1, hgf), lambda i: (0, 0))],
        out_specs=[pl.BlockSpec((HG_CHUNK, hgf), lambda i: (chunk_of(i), 0)),
                   pl.BlockSpec((None, nh, HG_DIM, HG_DIM), lambda i: (chunk_of(i), 0, 0, 0))],
        out_shape=[jax.ShapeDtypeStruct((T, hgf), F32), jax.ShapeDtypeStruct((nc, nh, HG_DIM, HG_DIM), F32)],
        scratch_shapes=[pltpu.VMEM((nh, HG_DIM, HG_DIM), F32)],
        compiler_params=_params(("arbitrary",)),
    )(u, u, u, lbr)


def _hg_bwd(u, lbr, st, do, S, L, naw, hgf, rev):
    T = S + L
    nh = hgf // HG_DIM
    nc, ncl, chunk_fwd = _hg_chunk_order(S, L, rev)
    qcol, fcol, icol = _hg_cols(naw, hgf, rev)

    def chunk_of(j):
        return chunk_fwd(nc - 1 - j)

    def body(q_ref, z_ref, v_ref, lb_ref, st_ref, do_ref, dq_ref, dz_ref, dv_ref, dlb_ref, dstate,
             dqe_s, dke_s, dqd_s, dkd_s, dl_s):
        j = pl.program_id(0)

        @pl.when(j == 0)
        def _():
            dstate[...] = jnp.zeros_like(dstate)
            dlb_ref[...] = jnp.zeros_like(dlb_ref)

        q, z, v = q_ref[...], z_ref[...], v_ref[...]
        lb = lb_ref[...]
        tri, trif, sig, f, k, cum, last, eq, ek = _hg_gates(q, z, lb, rev)
        ec, el, ekd = jnp.exp(cum), jnp.exp(last), jnp.exp(last - cum)
        qe, ke, qd, kd = q * eq, k * ek, q * ec, k * ekd
        qe16, ke16, qd16, kd16 = [t.astype(BF16) for t in (qe, ke, qd, kd)]
        dout = jnp.where(chunk_of(j) < ncl, do_ref[...], 0.0)
        dout16 = dout.astype(BF16)
        for h in range(nh):
            sl = slice(h * HG_DIM, (h + 1) * HG_DIM)
            a = jnp.where(tri, _dot_nt(qe16[:, sl], ke16[:, sl]), 0.0).astype(BF16)
            s0 = st_ref[h]
            ds1 = dstate[h]
            dv_ref[:, sl] = _dot_tn(a, dout16[:, sl]) + _dot_nt(kd16[:, sl], ds1.astype(BF16))
            da = jnp.where(tri, _dot_nt(dout[:, sl], v[:, sl], HI), 0.0)
            dqe_s[:, sl] = _dot(da, ke[:, sl], HI)
            dke_s[:, sl] = _dot_tn(da, qe[:, sl], HI)
            dqd_s[:, sl] = _dot(dout[:, sl], s0, HI)
            dkd_s[:, sl] = _dot(v[:, sl], ds1, HI)
            dl_s[:, sl] = _colsum(ds1 * s0)
            dstate[h] = _dot_tn(dout16[:, sl], qd16[:, sl]) + ds1 * el[:, sl]
        dqe, dke, dqd, dkd = dqe_s[...], dke_s[...], dqd_s[...], dkd_s[...]
        dq_ref[...] = dqe * eq + dqd * ec
        dk = dke * ek + dkd * ekd
        dcum = dqe * qe - dke * ke + dqd * qd - dkd * kd
        dlast = _colsum(dkd * kd) + el * dl_s[...]
        dlf = _dot_tn(trif, dcum, HI) + dlast
        df = dlf / f - dk
        dz_ref[...] = df * (1.0 - lb) * sig * (1.0 - sig)
        dlb_ref[...] += _colsum(df * (1.0 - sig))

    def blk(cb):
        return pl.BlockSpec((HG_CHUNK, hgf), lambda j: (chunk_of(j), cb))

    oblk = pl.BlockSpec((HG_CHUNK, hgf), lambda j: (chunk_of(j), 0))
    wide = pltpu.VMEM((HG_CHUNK, hgf), F32)
    return pl.pallas_call(
        body, name="hg_bwd_rev" if rev else "hg_bwd", grid=(nc,),
        in_specs=[blk(qcol), blk(fcol), blk(icol), pl.BlockSpec((1, hgf), lambda j: (0, 0)),
                  pl.BlockSpec((None, nh, HG_DIM, HG_DIM), lambda j: (chunk_of(j), 0, 0, 0)),
                  pl.BlockSpec((HG_CHUNK, hgf), lambda j: (jnp.minimum(chunk_of(j), ncl - 1), 0))],
        out_specs=[oblk, oblk, oblk, pl.BlockSpec((1, hgf), lambda j: (0, 0))],
        out_shape=[jax.ShapeDtypeStruct((T, hgf), F32)] * 3 + [jax.ShapeDtypeStruct((1, hgf), F32)],
        scratch_shapes=[pltpu.VMEM((nh, HG_DIM, HG_DIM), F32), wide, wide, wide, wide,
                        pltpu.VMEM((1, hgf), F32)],
        compiler_params=_params(("arbitrary",)),
    )(u, u, u, lbr, st, do)


def _adamw(w, g, m, v, name):
    shape = w.shape
    if w.ndim != 2 or shape[0] % 8 or shape[1] % LANES:
        w, g, m, v = [a.reshape(1, -1) for a in (w, g, m, v)]
    r, cw = w.shape
    tm = _tile(r, max(8, (1 << 19) // cw), 8) if r % 8 == 0 else r
    c1 = 1.0 / (1.0 - ADAM_B1 ** ADAM_STEP)
    c2 = 1.0 / (1.0 - ADAM_B2 ** ADAM_STEP)

    def body(w_ref, g_ref, m_ref, v_ref, d_ref, nm_ref, nv_ref):
        gg = g_ref[...]
        nm = ADAM_B1 * m_ref[...] + (1.0 - ADAM_B1) * gg
        nv = ADAM_B2 * v_ref[...] + (1.0 - ADAM_B2) * (gg * gg)
        d_ref[...] = -ADAM_LR * ((nm * c1) / (jnp.sqrt(nv * c2) + ADAM_EPS) + ADAM_WD * w_ref[...])
        nm_ref[...] = nm
        nv_ref[...] = nv

    spec = pl.BlockSpec((tm, cw), lambda i: (i, 0))
    outs = pl.pallas_call(
        body, name=name, grid=(r // tm,), in_specs=[spec] * 4, out_specs=[spec] * 3,
        out_shape=[jax.ShapeDtypeStruct((r, cw), F32)] * 3,
        compiler_params=_params(("parallel",)),
    )(w, g, m, v)
    return [o.reshape(shape) for o in outs]


def kernel(x, c, ctx, c_ctx, w_ada, b_ada, norm1_g, w_in, na_rpb, hg_lb_logits, hg_norm_g, w_pa, w_pb, w_out, norm2_g, w_ffn_in, w_ffn_out, final_g, loss_target, m_c_ctx, m_w_ada, m_b_ada, m_norm1_g, m_w_in, m_na_rpb, m_hg_lb_logits, m_hg_norm_g, m_w_pa, m_w_pb, m_w_out, m_norm2_g, m_w_ffn_in, m_w_ffn_out, m_final_g, v_c_ctx, v_w_ada, v_b_ada, v_norm1_g, v_w_in, v_na_rpb, v_hg_lb_logits, v_hg_norm_g, v_w_pa, v_w_pb, v_w_out, v_norm2_g, v_w_ffn_in, v_w_ffn_out, v_final_g):
    xi, yi, ci = lax.axis_index("x"), lax.axis_index("y"), lax.axis_index("c")
    sidx = 2 * xi + yi
    eidx = 4 * xi + 2 * yi + ci
    cidx = jnp.reshape(ci, (1,)).astype(jnp.int32)

    S, D = x.shape[1], x.shape[2]
    L = ctx.shape[1]
    T = S + L
    naw = NA_HEADS * NA_HEAD_DIM
    hgf = HG_HEADS * HG_DIM
    inw = 3 * naw + 5 * hgf + 2 * D
    fh = w_ffn_out.shape[1] * 4
    ads = w_ada.shape[2]
    fs = hg_lb_logits.shape[2]
    rows = S // GRID_W
    tr = _tile(L, 256)
    nlat, nall = S // tr, T // tr
    assert naw == hgf and D % naw == 0 and S % tr == 0 and 2 * hgf <= D

    pack0 = jnp.concatenate([c, jnp.pad(hg_lb_logits.reshape(1, -1), ((0, 0), (0, D - 4 * fs))),
                             jnp.zeros((6, D), F32)], axis=0)
    g0 = _all_gather8(pack0).reshape(8, 8, D)
    cs = g0[:, 0]
    lbl = g0[::2, 1, :4 * fs].reshape(4, 2, 2, fs).transpose(1, 2, 0, 3).reshape(2, 2, 4 * fs)
    p_lb = jax.nn.softmax(lbl, axis=0)
    lb = p_lb[0]
    lbb = [lb[d].reshape(1, hgf) for d in range(2)]

    cin = jnp.concatenate([cs, c_ctx[None], jnp.zeros((7, D), F32)], axis=0)
    b_sh = lax.dynamic_slice(b_ada, (0, sidx * ads), (1, ads))
    modp = _ada_fwd(cin, w_ada[0], b_sh)
    modfull = _all_gather8(modp).reshape(8, 16, ads)[::2].transpose(1, 0, 2).reshape(16, 4 * ads)
    mod_e = jnp.pad(lax.dynamic_index_in_dim(modfull, eidx, 0, keepdims=False).reshape(N_MOD, D), ((0, 2), (0, 0)))
    mod_c = jnp.pad(modfull[8].reshape(N_MOD, D), ((0, 2), (0, 0)))

    def halves(w2):
        return w2.astype(BF16).reshape(2, w2.shape[0] // 2, w2.shape[1])

    gathered = _gather_weights([halves(w[0]) for w in (w_in, w_pa, w_pb, w_out, w_ffn_in, w_ffn_out)])
    win3, wpa3, wpb3, wout3, wi3, wfo3 = [g.reshape(4, 2 * g.shape[2], g.shape[3]) for g in gathered]
    wout1 = wout3.reshape(1, D, D)
    wfo1 = wfo3.reshape(1, fh, D)

    xx = jnp.concatenate([x[0], ctx[0]], axis=0)

    def f_ln1(i, rv, vv):
        xt, = rv
        g, me, mc = vv
        isc = i >= nlat
        sh = jnp.where(isc, mc[0:1], me[0:1])
        sc = jnp.where(isc, mc[1:2], me[1:2])
        return [xt * _rms(xt) * g * (1.0 + sc) + sh], []

    hb, = _rowwise(f_ln1, nall, tr, [(xx, D, 0, None)], [norm1_g, mod_e, mod_c], [(D, BF16)], [], "ln1")
    u = _mm_nn(hb, win3, F32, "mm_in")

    scale = NA_HEAD_DIM ** -0.5

    def f_qkv(i, rv, vv):
        q, k, v = rv
        return [q * scale, k, v], []

    qs, kb, vb = _rowwise(f_qkv, nall, tr, [(u, naw, 0, None), (u, naw, 1, None), (u, naw, 2, None)], [],
                          [(naw, BF16)] * 3, [], "qkv_cast")
    bias_fn = functools.partial(_bias_tables, rows=rows)
    bias, bias_vjp = jax.vjp(bias_fn, na_rpb[0])
    o_na, lse = _na_fwd(qs, kb, vb, bias, S, L)

    o_f, st_f = _hg_fwd(u, lbb[0], S, L, naw, hgf, False)
    o_b, st_b = _hg_fwd(u, lbb[1], S, L, naw, hgf, True)

    hgn = jnp.tile(hg_norm_g, (1, HG_HEADS))
    hog_cb = (3 * naw + 4 * hgf) // hgf
    ga_cb = (3 * naw + 5 * hgf) // D
    gb_cb = ga_cb + 1

    def heads_rms(o):
        return jnp.concatenate([jnp.broadcast_to(_rms(o[:, h * HG_DIM:(h + 1) * HG_DIM]), (o.shape[0], HG_DIM))
                                for h in range(HG_HEADS)], axis=1)

    def f_readout(i, rv, vv):
        of, ob_, hog = rv
        g, = vv
        o = of + ob_
        return [o * heads_rms(o) * g * _silu(hog)], []

    ob, = _rowwise(f_readout, nlat, tr, [(o_f, hgf, 0, None), (o_b, hgf, 0, None), (u, hgf, hog_cb, None)],
                   [hgn], [(hgf, BF16)], [], "hg_readout")

    ya = _mm_nn(o_na, wpa3, F32, "mm_pa")
    yb = _mm_nn(ob, wpb3, F32, "mm_pb")

    def f_merge(i, rv, vv):
        ya_, yb_, ga, gb = rv
        return [_sigmoid(ga) * ya_ + _sigmoid(gb) * yb_], []

    yv, = _rowwise(f_merge, nlat, tr, [(ya, D, 0, None), (yb, D, 0, None), (u, D, ga_cb, None), (u, D, gb_cb, None)],
                   [], [(D, BF16)], [], "merge")
    z = _mm_nn(yv, wout1, F32, "mm_out")

    def f_res1(i, rv, vv):
        xt, zt = rv
        g, me = vv
        x1 = xt + me[2:3] * zt
        return [x1, x1 * _rms(x1) * g * (1.0 + me[4:5]) + me[3:4]], []

    x1, h2 = _rowwise(f_res1, nlat, tr, [(xx, D, 0, None), (z, D, 0, None)], [norm2_g, mod_e],
                      [(D, F32), (D, BF16)], [], "res1_ln2")
    au = _mm_nn(h2, wi3, F32, "mm_ffn_in")
    tf = _tile(L, 128)
    nlf = S // tf

    def f_swiglu(i, rv, vv):
        a, uu = rv
        return [_silu(a) * uu], []

    sw, = _rowwise(f_swiglu, nlf, tf, [(au, fh, 0, None), (au, fh, 1, None)], [], [(fh, BF16)], [], "swiglu")
    ff = _mm_nn(sw, wfo1, F32, "mm_ffn_out")

    fg = final_g.reshape(1, D)

    def f_final(i, rv, vv):
        x1t, ft, tg = rv
        g, me = vv
        x2 = x1t + me[5:6] * ft
        r3 = _rms(x2)
        xn = x2 * r3
        err = xn * g - tg
        dyy = err * (1.0 / D)
        dxn = dyy * g
        dx2 = r3 * (dxn - xn * jnp.mean(dxn * xn, axis=-1, keepdims=True))
        return [dx2, dx2 * me[5:6]], [_colsum(err * err), _colsum(dyy * xn), _colsum(dx2 * ft)]

    dx2, dfb, loss_cols, dfg, dg2 = _rowwise(
        f_final, nlat, tr, [(x1, D, 0, None), (ff, D, 0, None), (loss_target[0], D, 0, None)], [fg, mod_e],
        [(D, F32), (D, BF16)], [D, D, D], "final_loss")

    dsw = _mm_nt(dfb, wfo1, F32, "mm_d_sw")

    def f_dswiglu(i, rv, vv):
        d, a, uu = rv
        return [jnp.concatenate([d * uu * _dsilu(a), d * _silu(a)], axis=1)], []

    dau, = _rowwise(f_dswiglu, nlf, tf, [(dsw, fh, 0, None), (au, fh, 0, None), (au, fh, 1, None)], [],
                    [(2 * fh, BF16)], [], "swiglu_bwd")
    g_wfo = _mm_tn(sw, dfb, 1, "mm_dw_ffn_out").reshape(4, 2, fh // 8, D)
    dh2 = _mm_nt(dau, wi3, F32, "mm_d_h2")
    g_wi = _mm_tn(h2, dau, 4, "mm_dw_ffn_in")

    def f_ln2_bwd(i, rv, vv):
        dh, x1t, dx2t, zt = rv
        g, me = vv
        r2 = _rms(x1t)
        xn = x1t * r2
        dxn = dh * g * (1.0 + me[4:5])
        dx1 = dx2t + r2 * (dxn - xn * jnp.mean(dxn * xn, axis=-1, keepdims=True))
        return ([dx1, dx1 * me[2:3]],
                [_colsum(dh), _colsum(dh * xn * g), _colsum(dh * xn * (1.0 + me[4:5])), _colsum(dx1 * zt)])

    dx1, dzb, dsh2, dsc2, dn2g, dg1 = _rowwise(
        f_ln2_bwd, nlat, tr, [(dh2, D, 0, None), (x1, D, 0, None), (dx2, D, 0, None), (z, D, 0, None)],
        [norm2_g, mod_e], [(D, F32), (D, BF16)], [D, D, D, D], "ln2_bwd")

    dy = _mm_nt(dzb, wout1, F32, "mm_d_y")
    g_wout = _mm_tn(yv, dzb, 1, "mm_dw_out").reshape(4, 2, D // 8, D)

    def f_dmerge(i, rv, vv):
        d, ya_, yb_, ga, gb = rv
        sa, sb_ = _sigmoid(ga), _sigmoid(gb)
        return [d * sa, d * sb_, d * ya_ * sa * (1.0 - sa), d * yb_ * sb_ * (1.0 - sb_)], []

    dya, dyb, dga, dgb = _rowwise(
        f_dmerge, nlat, tr, [(dy, D, 0, None), (ya, D, 0, None), (yb, D, 0, None), (u, D, ga_cb, None),
                             (u, D, gb_cb, None)], [], [(D, BF16)] * 4, [], "merge_bwd")
    d_ona = _mm_nt(dya, wpa3, BF16, "mm_d_ona")
    d_ob = _mm_nt(dyb, wpb3, F32, "mm_d_ob")
    g_wpa = _mm_tn(o_na, dya, 4, "mm_dw_pa")
    g_wpb = _mm_tn(ob, dyb, 4, "mm_dw_pb")

    def f_dreadout(i, rv, vv):
        d, of, ob_, hog = rv
        g, = vv
        o = of + ob_
        on = o * heads_rms(o)
        t = d * _silu(hog) * g
        mt = jnp.concatenate([jnp.broadcast_to(jnp.mean((t * on)[:, h * HG_DIM:(h + 1) * HG_DIM], axis=-1,
                                                        keepdims=True), (o.shape[0], HG_DIM))
                              for h in range(HG_HEADS)], axis=1)
        do_ = heads_rms(o) * (t - on * mt)
        return [do_, d * on * g * _dsilu(hog)], [_colsum(d * _silu(hog) * on)]

    do_hg, dhog, dhgn = _rowwise(
        f_dreadout, nlat, tr, [(d_ob, hgf, 0, None), (o_f, hgf, 0, None), (o_b, hgf, 0, None),
                               (u, hgf, hog_cb, None)], [hgn], [(hgf, F32), (hgf, BF16)], [hgf], "hg_readout_bwd")

    dq_f, dz_f, dv_f, dlb_f = _hg_bwd(u, lbb[0], st_f, do_hg, S, L, naw, hgf, False)
    dq_b, dz_b, dv_b, dlb_b = _hg_bwd(u, lbb[1], st_b, do_hg, S, L, naw, hgf, True)
    dq_na, dk_na, dv_na, dbias = _na_bwd(qs, kb, vb, bias, d_ona, o_na, lse, S, L)

    ta = _tile(L, 128)
    nla, naa = S // ta, T // ta
    lat = lambda i: jnp.minimum(i, nla - 1)

    def f_assemble(i, rv, vv):
        dqn, dk, dv, dqf, dqb, dzf, dzb_, dvf, dvb, dho, dga_, dgb_ = rv
        keep = (i < nla).astype(F32)
        return [jnp.concatenate([dqn * (scale * keep), dk, dv, dqf + dqb, dzf, dzb_, dvf + dvb,
                                 dho.astype(F32) * keep, dga_.astype(F32) * keep, dgb_.astype(F32) * keep],
                                axis=1)], []

    du, = _rowwise(
        f_assemble, naa, ta,
        [(dq_na, naw, 0, lat), (dk_na, naw, 0, None), (dv_na, naw, 0, None), (dq_f, hgf, 0, None),
         (dq_b, hgf, 0, None), (dz_f, hgf, 0, None), (dz_b, hgf, 0, None), (dv_f, hgf, 0, None),
         (dv_b, hgf, 0, None), (dhog, hgf, 0, lat), (dga, D, 0, lat), (dgb, D, 0, lat)],
        [], [(inw, BF16)], [], "assemble_du")

    dh = _mm_nt(du, win3, F32, "mm_d_h")
    g_win = _mm_tn(hb, du, 4, "mm_dw_in")

    def f_ln1_bwd(i, rv, vv):
        dht, xt, dx1t = rv
        g, me = vv
        r1 = _rms(xt)
        xn = xt * r1
        dxn = dht * g * (1.0 + me[1:2])
        dx = dx1t + r1 * (dxn - xn * jnp.mean(dxn * xn, axis=-1, keepdims=True))
        return [dx], [_colsum(dht), _colsum(dht * xn * g), _colsum(dht * xn * (1.0 + me[1:2]))]

    grad_x, dsh1, dsc1, dn1g_l = _rowwise(
        f_ln1_bwd, nlat, tr, [(dh, D, 0, None), (xx, D, 0, None), (dx1, D, 0, None)], [norm1_g, mod_e],
        [(D, F32)], [D, D, D], "ln1_bwd")

    def f_ln1_bwd_ctx(i, rv, vv):
        dht, xt = rv
        g, mc = vv
        xn = xt * _rms(xt)
        return [], [_colsum(dht), _colsum(dht * xn * g), _colsum(dht * xn * (1.0 + mc[1:2]))]

    ctx_rows = lambda i: i + nlat
    dsh1c, dsc1c, dn1g_c = _rowwise(
        f_ln1_bwd_ctx, nall - nlat, tr, [(dh, D, 0, ctx_rows), (xx, D, 0, ctx_rows)], [norm1_g, mod_c],
        [], [D, D, D], "ln1_bwd_ctx")

    drpb = bias_vjp(dbias)[0].reshape(1, -1)
    nrp = -(-drpb.shape[1] // D)
    drpb_rows = jnp.pad(drpb, ((0, 0), (0, nrp * D - drpb.shape[1]))).reshape(nrp, D)
    dlb = jnp.concatenate([dlb_f, dlb_b], axis=1)
    dhg = jnp.sum(dhgn.reshape(HG_HEADS, HG_DIM), axis=0, keepdims=True)

    def wide(v):
        return jnp.pad(v, ((0, 0), (0, D - v.shape[1])))

    pack_rows = [loss_cols, dfg, dn2g, dn1g_l + dn1g_c, dsh1, dsc1, dg1, dsh2, dsc2, dg2, dsh1c, dsc1c,
                 wide(dhg), wide(dlb), drpb_rows]
    pack = jnp.concatenate(pack_rows, axis=0)
    npk = -(-pack.shape[0] // 8) * 8
    pack = jnp.pad(pack, ((0, npk - pack.shape[0]), (0, 0)))
    gp = _all_gather8(pack).reshape(8, npk, D)
    tot = _sum8(gp, "sum_small_grads")

    loss = (0.5 / D) * jnp.sum(tot[0])
    grad_final_g = tot[1]
    grad_norm2_g = tot[2:3]
    grad_norm1_g = tot[3:4]
    grad_hg_norm_g = tot[12:13, :HG_DIM]
    dlb_tot = tot[13, :2 * hgf].reshape(2, hgf)
    grad_na_rpb = tot[14:14 + nrp].reshape(-1)[:drpb.shape[1]].reshape(na_rpb.shape)
    dlog = jnp.stack([dlb_tot * p_lb[0] * (1.0 - p_lb[0]), -dlb_tot * p_lb[0] * p_lb[1]], axis=0)
    grad_hg_lb = lax.dynamic_slice(dlog, (0, 0, sidx * fs), (2, 2, fs))

    dmod_all = gp[:, 4:10].reshape(8, N_MOD * D)
    dmod_ctx = jnp.concatenate([tot[10], tot[11], jnp.zeros((4 * D,), F32)])[None]
    dm16 = jnp.concatenate([dmod_all, dmod_ctx, jnp.zeros((7, N_MOD * D), F32)], axis=0)
    grad_b_ada = jnp.sum(dm16, axis=0, keepdims=True)
    dm_sh = lax.dynamic_slice(dm16, (0, sidx * ads), (16, ads))
    g_wada, dcin = _ada_bwd(cin, w_ada[0], dm_sh)
    gc = _all_gather8(dcin[8:16]).reshape(8, 8, D)
    grad_c_ctx = (gc[0, 0] + gc[2, 0] + gc[4, 0] + gc[6, 0]) * _dsilu(c_ctx)

    names = ["w_in", "w_pa", "w_pb", "w_out", "w_ffn_in", "w_ffn_out"]
    g_win, g_wpa, g_wpb, g_wout, g_wi, g_wfo = _reduce_scatter(
        cidx, [g_win, g_wpa, g_wpb, g_wout, g_wi, g_wfo], names)

    grads = {
        "c_ctx": grad_c_ctx, "w_ada": g_wada[None], "b_ada": grad_b_ada, "norm1_g": grad_norm1_g,
        "w_in": g_win[None], "na_rpb": grad_na_rpb, "hg_lb_logits": grad_hg_lb, "hg_norm_g": grad_hg_norm_g,
        "w_pa": g_wpa[None], "w_pb": g_wpb[None], "w_out": g_wout[None], "norm2_g": grad_norm2_g,
        "w_ffn_in": g_wi[None], "w_ffn_out": g_wfo[None], "final_g": grad_final_g,
    }
    weights = {
        "c_ctx": (c_ctx, m_c_ctx, v_c_ctx), "w_ada": (w_ada, m_w_ada, v_w_ada), "b_ada": (b_ada, m_b_ada, v_b_ada),
        "norm1_g": (norm1_g, m_norm1_g, v_norm1_g), "w_in": (w_in, m_w_in, v_w_in),
        "na_rpb": (na_rpb, m_na_rpb, v_na_rpb), "hg_lb_logits": (hg_lb_logits, m_hg_lb_logits, v_hg_lb_logits),
        "hg_norm_g": (hg_norm_g, m_hg_norm_g, v_hg_norm_g), "w_pa": (w_pa, m_w_pa, v_w_pa),
        "w_pb": (w_pb, m_w_pb, v_w_pb), "w_out": (w_out, m_w_out, v_w_out),
        "norm2_g": (norm2_g, m_norm2_g, v_norm2_g), "w_ffn_in": (w_ffn_in, m_w_ffn_in, v_w_ffn_in),
        "w_ffn_out": (w_ffn_out, m_w_ffn_out, v_w_ffn_out), "final_g": (final_g, m_final_g, v_final_g),
    }
    order = list(weights)
    deltas, new_ms, new_vs = [], [], []
    for nm in order:
        w, m, v = weights[nm]
        g = grads[nm].reshape(w.shape)
        grads[nm] = g
        if w.ndim == 3 and w.shape[0] == 1:
            d_, m_, v_ = _adamw(w[0], g[0], m[0], v[0], "adamw_" + nm)
            d_, m_, v_ = d_[None], m_[None], v_[None]
        else:
            d_, m_, v_ = _adamw(w, g, m, v, "adamw_" + nm)
        deltas.append(d_)
        new_ms.append(m_)
        new_vs.append(v_)

    return (loss, grad_x[None], *[grads[nm] for nm in order], *deltas, *new_ms, *new_vs)
```

```python
import numpy as np

import jax
import jax.numpy as jnp
from jax import lax
from jax.experimental import pallas as pl
from jax.experimental.pallas import tpu as pltpu

F32 = jnp.float32
BF16 = jnp.bfloat16

GRID_W = 64
WIN_H = 8
WIN_W = 16
NA_HEADS = 16
NA_HEAD_DIM = 64
HG_HEADS = 8
HG_DIM = 128
HG_CHUNK = 64
N_MOD = 6
EPS = 1e-6
ADAM_LR = 0.001
ADAM_B1 = 0.9
ADAM_B2 = 0.999
ADAM_EPS = 1e-08
ADAM_WD = 0.01
ADAM_STEP = 10

LANES = 128
NA_ROWS_PER_STEP = 4
VMEM_LIMIT = 56 * 1024 * 1024
MASK_VALUE = -1e30
EXP_CLAMP = 80.0
MESH_ID = pl.DeviceIdType.MESH
HI = lax.Precision.HIGHEST


def _tile(dim, target, mult=LANES):
    best = None
    t = mult
    while t <= min(dim, target):
        if dim % t == 0:
            best = t
        t += mult
    assert best is not None, (dim, target, mult)
    return best


def _params(sem):
    return pltpu.CompilerParams(dimension_semantics=sem, vmem_limit_bytes=VMEM_LIMIT)


def _dot(a, b, precision=None):
    return jnp.dot(a, b, preferred_element_type=F32, precision=precision)


def _dot_nt(a, b, precision=None):
    return lax.dot_general(a, b, (((1,), (1,)), ((), ())), preferred_element_type=F32, precision=precision)


def _dot_tn(a, b, precision=None):
    return lax.dot_general(a, b, (((0,), (0,)), ((), ())), preferred_element_type=F32, precision=precision)


def _sigmoid(v):
    return 1.0 / (1.0 + jnp.exp(-v))


def _mm_nn(a, b3, out_dtype, name):
    M, K = a.shape
    nsh, _, Ns = b3.shape
    tm, tn, tk = _tile(M, 1024), _tile(Ns, 1408), _tile(K, 2048)
    tps, nk = Ns // tn, K // tk

    def body(a_ref, b_ref, o_ref, acc):
        k = pl.program_id(2)

        @pl.when(k == 0)
        def _():
            acc[...] = jnp.zeros_like(acc)

        acc[...] += _dot(a_ref[...], b_ref[...])

        @pl.when(k == nk - 1)
        def _():
            o_ref[...] = acc[...].astype(o_ref.dtype)

    return pl.pallas_call(
        body, name=name, grid=(M // tm, nsh * tps, nk),
        in_specs=[pl.BlockSpec((tm, tk), lambda m, n, k: (m, k)),
                  pl.BlockSpec((None, tk, tn), lambda m, n, k: (n // tps, k, n % tps))],
        out_specs=pl.BlockSpec((tm, tn), lambda m, n, k: (m, n)),
        out_shape=jax.ShapeDtypeStruct((M, nsh * Ns), out_dtype),
        scratch_shapes=[pltpu.VMEM((tm, tn), F32)],
        compiler_params=_params(("parallel", "parallel", "arbitrary")),
    )(a, b3)


def _mm_nt(a, b3, out_dtype, name):
    M = a.shape[0]
    nsh, Kw, Ns = b3.shape
    tm, tn, tk = _tile(M, 1024), _tile(Kw, 1024), _tile(Ns, 1536)
    kps = Ns // tk
    nk = nsh * kps

    def body(a_ref, b_ref, o_ref, acc):
        k = pl.program_id(2)

        @pl.when(k == 0)
        def _():
            acc[...] = jnp.zeros_like(acc)

        acc[...] += _dot_nt(a_ref[...], b_ref[...])

        @pl.when(k == nk - 1)
        def _():
            o_ref[...] = acc[...].astype(o_ref.dtype)

    return pl.pallas_call(
        body, name=name, grid=(M // tm, Kw // tn, nk),
        in_specs=[pl.BlockSpec((tm, tk), lambda m, n, k: (m, k)),
                  pl.BlockSpec((None, tn, tk), lambda m, n, k: (k // kps, n, k % kps))],
        out_specs=pl.BlockSpec((tm, tn), lambda m, n, k: (m, n)),
        out_shape=jax.ShapeDtypeStruct((M, Kw), out_dtype),
        scratch_shapes=[pltpu.VMEM((tm, tn), F32)],
        compiler_params=_params(("parallel", "parallel", "arbitrary")),
    )(a, b3)


def _mm_tn(a, g, nsh, name):
    Tk, M = a.shape
    Ns = g.shape[1] // nsh
    tm, tn, tk = _tile(M // 2, 1408), _tile(Ns, 1408), _tile(Tk, 1024)
    mh, tps, nk = (M // 2) // tm, Ns // tn, Tk // tk

    def body(a_ref, g_ref, o_ref, acc):
        k = pl.program_id(2)

        @pl.when(k == 0)
        def _():
            acc[...] = jnp.zeros_like(acc)

        acc[...] += _dot_tn(a_ref[...], g_ref[...])

        @pl.when(k == nk - 1)
        def _():
            o_ref[...] = acc[...]

    return pl.pallas_call(
        body, name=name, grid=(M // tm, nsh * tps, nk),
        in_specs=[pl.BlockSpec((tk, tm), lambda m, n, k: (k, m)),
                  pl.BlockSpec((tk, tn), lambda m, n, k: (k, n))],
        out_specs=pl.BlockSpec((None, None, tm, tn), lambda m, n, k: (n // tps, m // mh, m % mh, n % tps)),
        out_shape=jax.ShapeDtypeStruct((nsh, 2, M // 2, Ns), F32),
        scratch_shapes=[pltpu.VMEM((tm, tn), F32)],
        compiler_params=_params(("parallel", "parallel", "arbitrary")),
    )(a, g)


def _rowwise(fn, nblk, tm, rins, vins, routs, accs, name):
    nr, nv, no, na = len(rins), len(vins), len(routs), len(accs)

    def body(*refs):
        i = pl.program_id(0)
        outs, accv = fn(i, [r[...] for r in refs[:nr]], [r[...] for r in refs[nr:nr + nv]])
        for r, v in zip(refs[nr + nv:nr + nv + no], outs):
            r[...] = v.astype(r.dtype)
        arefs = refs[nr + nv + no:]
        if na:
            @pl.when(i == 0)
            def _():
                for a in arefs:
                    a[...] = jnp.zeros_like(a)

            for a, v in zip(arefs, accv):
                a[...] += v

    def row_spec(w, cb, rm):
        if rm is None:
            return pl.BlockSpec((tm, w), lambda i: (i, cb))
        return pl.BlockSpec((tm, w), lambda i: (rm(i), cb))

    in_specs = [row_spec(w, cb, rm) for (_, w, cb, rm) in rins]
    in_specs += [pl.BlockSpec(v.shape, lambda i: (0, 0)) for v in vins]
    out_specs = [pl.BlockSpec((tm, w), lambda i: (i, 0)) for (w, _) in routs]
    out_specs += [pl.BlockSpec((1, w), lambda i: (0, 0)) for w in accs]
    out_shape = [jax.ShapeDtypeStruct((nblk * tm, w), dt) for (w, dt) in routs]
    out_shape += [jax.ShapeDtypeStruct((1, w), F32) for w in accs]
    res = pl.pallas_call(
        body, name=name, grid=(nblk,), in_specs=in_specs, out_specs=out_specs, out_shape=out_shape,
        compiler_params=_params(("arbitrary",)),
    )(*[r[0] for r in rins], *vins)
    return list(res)


def _colsum(v):
    return jnp.sum(v, axis=0, keepdims=True)


def _rms(v):
    return lax.rsqrt(jnp.mean(v * v, axis=-1, keepdims=True) + EPS)


def _all_gather8(xs):
    m_per, n = xs.shape

    def body(x_ref, out_ref, send_sems, recv_sems, local_sem):
        x, y, c = lax.axis_index("x"), lax.axis_index("y"), lax.axis_index("c")
        me, sibling = (x, y, c), (x, y, 1 - c)
        chips = [(1 - x, y), (x, 1 - y), (1 - x, 1 - y)]

        def rows(px, py, pc):
            return out_ref.at[pl.ds((4 * px + 2 * py + pc) * m_per, m_per), :]

        def copy(k, block, to, src=None):
            return pltpu.make_async_remote_copy(
                src_ref=rows(*block) if src is None else src, dst_ref=rows(*block),
                send_sem=send_sems.at[k], recv_sem=recv_sems.at[k], device_id=to, device_id_type=MESH_ID)

        mine = pltpu.make_async_copy(x_ref, rows(*me), local_sem)
        mine.start()
        first = [copy(0, me, sibling, src=x_ref)]
        first += [copy(1 + j, me, (*chip, c), src=x_ref) for j, chip in enumerate(chips)]
        for cp in first:
            cp.start()
        passed = [copy(4 + j, (*chip, c), sibling) for j, chip in enumerate(chips)]
        for j, chip in enumerate(chips):
            copy(1 + j, (*chip, c), me).wait_recv()
            passed[j].start()
        copy(0, sibling, me).wait_recv()
        for j, chip in enumerate(chips):
            copy(4 + j, (*chip, 1 - c), me).wait_recv()
        for cp in first + passed:
            cp.wait_send()
        mine.wait()

    return pl.pallas_call(
        body, name="all_gather8_%dx%d" % (m_per, n),
        out_shape=jax.ShapeDtypeStruct((8 * m_per, n), xs.dtype),
        in_specs=[pl.BlockSpec(memory_space=pltpu.VMEM)],
        out_specs=pl.BlockSpec(memory_space=pltpu.VMEM),
        scratch_shapes=[pltpu.SemaphoreType.DMA((7,)), pltpu.SemaphoreType.DMA((7,)), pltpu.SemaphoreType.DMA],
    )(xs)


def _mesh_pos():
    x, y, c = lax.axis_index("x"), lax.axis_index("y"), lax.axis_index("c")
    chips = [(1 - x, y), (x, 1 - y), (1 - x, 1 - y)]
    return x, y, c, chips


def _cast_place(sidx, w, name):
    r, cw = w.shape
    rh = r // 2
    tm = _tile(rh, max(16, (1 << 20) // (4 * cw)), 16)
    nt = rh // tm

    def body(s_ref, w_ref, o_ref):
        o_ref[...] = w_ref[...].astype(o_ref.dtype)

    return pl.pallas_call(
        body, name=name,
        grid_spec=pltpu.PrefetchScalarGridSpec(
            num_scalar_prefetch=1, grid=(2, nt),
            in_specs=[pl.BlockSpec((tm, cw), lambda h, i, s_ref: (h * nt + i, 0))],
            out_specs=pl.BlockSpec((None, None, tm, cw), lambda h, i, s_ref: (s_ref[0], h, i, 0))),
        out_shape=jax.ShapeDtypeStruct((4, 2, rh, cw), BF16),
        compiler_params=_params(("parallel", "parallel")),
    )(sidx, w)


def _gather_weights(bufs):
    n = len(bufs)

    def body(*refs):
        out = refs[n:2 * n]
        send_sems, recv_sems = refs[2 * n:]
        x, y, c, chips = _mesh_pos()
        s = 2 * x + y
        sibling = (x, y, 1 - c)

        def copy(i, k, shard, half, to):
            dst = out[i].at[shard, half]
            return pltpu.make_async_remote_copy(
                src_ref=dst, dst_ref=dst, send_sem=send_sems.at[6 * i + k], recv_sem=recv_sems.at[6 * i + k],
                device_id=to, device_id_type=MESH_ID)

        first = [copy(i, j, s, c, (*chip, c)) for i in range(n) for j, chip in enumerate(chips)]
        for cp in first:
            cp.start()
        passed = []
        for j, chip in enumerate(chips):
            sj = 2 * chip[0] + chip[1]
            for i in range(n):
                copy(i, j, sj, c, (x, y, c)).wait_recv()
                cp = copy(i, 3 + j, sj, c, sibling)
                cp.start()
                passed.append(cp)
        for j, chip in enumerate(chips):
            sj = 2 * chip[0] + chip[1]
            for i in range(n):
                copy(i, 3 + j, sj, 1 - c, (x, y, c)).wait_recv()
        for cp in first + passed:
            cp.wait_send()

    any_spec = pl.BlockSpec(memory_space=pl.ANY)
    return pl.pallas_call(
        body, name="gather_weights",
        out_shape=[jax.ShapeDtypeStruct(b.shape, b.dtype) for b in bufs],
        in_specs=[any_spec] * n, out_specs=[any_spec] * n,
        input_output_aliases={i: i for i in range(n)},
        scratch_shapes=[pltpu.SemaphoreType.DMA((6 * n,)), pltpu.SemaphoreType.DMA((6 * n,))],
    )(*bufs)


def _swap_other_half(gs):
    n = len(gs)

    def body(*refs):
        g, land = refs[:n], refs[n:2 * n]
        send_sems, recv_sems = refs[2 * n:]
        x, y, c, _ = _mesh_pos()
        cps = [pltpu.make_async_remote_copy(
            src_ref=g[i].at[:, 1 - c], dst_ref=land[i], send_sem=send_sems.at[i], recv_sem=recv_sems.at[i],
            device_id=(x, y, 1 - c), device_id_type=MESH_ID) for i in range(n)]
        for cp in cps:
            cp.start()
        for cp in cps:
            cp.wait()

    any_spec = pl.BlockSpec(memory_space=pl.ANY)
    return pl.pallas_call(
        body, name="rs_swap_other_half",
        out_shape=[jax.ShapeDtypeStruct((4,) + g.shape[2:], g.dtype) for g in gs],
        in_specs=[any_spec] * n, out_specs=[any_spec] * n,
        scratch_shapes=[pltpu.SemaphoreType.DMA((n,)), pltpu.SemaphoreType.DMA((n,))],
    )(*gs)


def _scatter_pieces(ps):
    n = len(ps)

    def body(*refs):
        p, land = refs[:n], refs[n:2 * n]
        send_sems, recv_sems = refs[2 * n:]
        x, y, c, chips = _mesh_pos()

        def copy(i, j, chip):
            return pltpu.make_async_remote_copy(
                src_ref=p[i].at[2 * chip[0] + chip[1]], dst_ref=land[i].at[j],
                send_sem=send_sems.at[3 * i + j], recv_sem=recv_sems.at[3 * i + j],
                device_id=(*chip, c), device_id_type=MESH_ID)

        sends = [copy(i, j, chip) for i in range(n) for j, chip in enumerate(chips)]
        for cp in sends:
            cp.start()
        for cp in sends:
            cp.wait()

    any_spec = pl.BlockSpec(memory_space=pl.ANY)
    return pl.pallas_call(
        body, name="rs_scatter_pieces",
        out_shape=[jax.ShapeDtypeStruct((3,) + p.shape[1:], p.dtype) for p in ps],
        in_specs=[any_spec] * n, out_specs=[any_spec] * n,
        scratch_shapes=[pltpu.SemaphoreType.DMA((3 * n,)), pltpu.SemaphoreType.DMA((3 * n,))],
    )(*ps)


def _swap_result_half(bufs):
    n = len(bufs)

    def body(*refs):
        out = refs[n:2 * n]
        send_sems, recv_sems = refs[2 * n:]
        x, y, c, _ = _mesh_pos()
        cps = [pltpu.make_async_remote_copy(
            src_ref=out[i].at[c], dst_ref=out[i].at[c], send_sem=send_sems.at[i], recv_sem=recv_sems.at[i],
            device_id=(x, y, 1 - c), device_id_type=MESH_ID) for i in range(n)]
        for cp in cps:
            cp.start()
        for i in range(n):
            pltpu.make_async_remote_copy(
                src_ref=out[i].at[1 - c], dst_ref=out[i].at[1 - c], send_sem=send_sems.at[i], recv_sem=recv_sems.at[i],
                device_id=(x, y, c), device_id_type=MESH_ID).wait_recv()
        for cp in cps:
            cp.wait_send()

    any_spec = pl.BlockSpec(memory_space=pl.ANY)
    return pl.pallas_call(
        body, name="rs_swap_result_half",
        out_shape=[jax.ShapeDtypeStruct(b.shape, b.dtype) for b in bufs],
        in_specs=[any_spec] * n, out_specs=[any_spec] * n,
        input_output_aliases={i: i for i in range(n)},
        scratch_shapes=[pltpu.SemaphoreType.DMA((n,)), pltpu.SemaphoreType.DMA((n,))],
    )(*bufs)


def _add_own_half(cidx, g, land, name):
    _, _, rh, cw = g.shape
    tm = _tile(rh, max(16, (1 << 20) // (4 * cw)), 16)

    def body(c_ref, g_ref, l_ref, o_ref):
        o_ref[...] = (g_ref[...] + l_ref[...]).astype(o_ref.dtype)

    return pl.pallas_call(
        body, name=name,
        grid_spec=pltpu.PrefetchScalarGridSpec(
            num_scalar_prefetch=1, grid=(4, rh // tm),
            in_specs=[pl.BlockSpec((None, None, tm, cw), lambda s, i, c_ref: (s, c_ref[0], i, 0)),
                      pl.BlockSpec((None, tm, cw), lambda s, i, c_ref: (s, i, 0))],
            out_specs=pl.BlockSpec((None, tm, cw), lambda s, i, c_ref: (s, i, 0))),
        out_shape=jax.ShapeDtypeStruct((4, rh, cw), BF16),
        compiler_params=_params(("parallel", "parallel")),
    )(cidx, g, land)


def _sum_pieces(scidx, part, land, name):
    _, rh, cw = land.shape
    tm = _tile(rh, max(16, (1 << 20) // (4 * cw)), 16)

    def body(sc_ref, p_ref, l_ref, o_ref):
        v = l_ref[...].astype(F32)
        o_ref[...] = (p_ref[...].astype(F32) + v[0]) + (v[1] + v[2])

    return pl.pallas_call(
        body, name=name,
        grid_spec=pltpu.PrefetchScalarGridSpec(
            num_scalar_prefetch=1, grid=(rh // tm,),
            in_specs=[pl.BlockSpec((None, tm, cw), lambda i, sc_ref: (sc_ref[0], i, 0)),
                      pl.BlockSpec((3, tm, cw), lambda i, sc_ref: (0, i, 0))],
            out_specs=pl.BlockSpec((None, tm, cw), lambda i, sc_ref: (sc_ref[1], i, 0))),
        out_shape=jax.ShapeDtypeStruct((2, rh, cw), F32),
        compiler_params=_params(("parallel",)),
    )(scidx, part, land)


def _reduce_scatter(scidx, gs, names):
    lands = _swap_other_half(gs)
    parts = [_add_own_half(scidx[1:], g, l, "rs_add_" + nm) for g, l, nm in zip(gs, lands, names)]
    pieces = _scatter_pieces(parts)
    halves = [_sum_pieces(scidx, p, l, "rs_sum_" + nm) for p, l, nm in zip(parts, pieces, names)]
    fulls = _swap_result_half(halves)
    return [f.reshape(2 * f.shape[1], f.shape[2]) for f in fulls]


def _sum8(g, name):
    def body(g_ref, o_ref):
        acc = g_ref[0]
        for k in range(1, 8):
            acc = acc + g_ref[k]
        o_ref[...] = acc

    return pl.pallas_call(body, name=name, out_shape=jax.ShapeDtypeStruct(g.shape[1:], F32))(g)


def _silu(v):
    return v * _sigmoid(v)


def _dsilu(v):
    s = _sigmoid(v)
    return s * (1.0 + v * (1.0 - s))


def _ada_fwd(cin, w, b):
    d, ns = w.shape
    tn = _tile(ns, 512)

    def body(c_ref, w_ref, b_ref, o_ref):
        o_ref[...] = _dot(_silu(c_ref[...]), w_ref[...], HI) + b_ref[...]

    return pl.pallas_call(
        body, name="ada_fwd", grid=(ns // tn,),
        in_specs=[pl.BlockSpec(cin.shape, lambda n: (0, 0)), pl.BlockSpec((d, tn), lambda n: (0, n)),
                  pl.BlockSpec((1, tn), lambda n: (0, n))],
        out_specs=pl.BlockSpec((cin.shape[0], tn), lambda n: (0, n)),
        out_shape=jax.ShapeDtypeStruct((cin.shape[0], ns), F32),
        compiler_params=_params(("parallel",)),
    )(cin, w, b)


def _ada_bwd(cin, w, dm):
    d, ns = w.shape
    tn = _tile(ns, 512)

    def body(c_ref, w_ref, d_ref, dw_ref, dc_ref):
        n = pl.program_id(0)

        @pl.when(n == 0)
        def _():
            dc_ref[...] = jnp.zeros_like(dc_ref)

        dw_ref[...] = _dot_tn(_silu(c_ref[...]), d_ref[...], HI)
        dc_ref[...] += _dot_nt(d_ref[...], w_ref[...], HI)

    return pl.pallas_call(
        body, name="ada_bwd", grid=(ns // tn,),
        in_specs=[pl.BlockSpec(cin.shape, lambda n: (0, 0)), pl.BlockSpec((d, tn), lambda n: (0, n)),
                  pl.BlockSpec((cin.shape[0], tn), lambda n: (0, n))],
        out_specs=[pl.BlockSpec((d, tn), lambda n: (0, n)), pl.BlockSpec(cin.shape, lambda n: (0, 0))],
        out_shape=[jax.ShapeDtypeStruct((d, ns), F32), jax.ShapeDtypeStruct(cin.shape, F32)],
        compiler_params=_params(("arbitrary",)),
    )(cin, w, dm)


def _bias_tables(rpb, rows):
    kh = min(WIN_H, rows)
    col = jnp.arange(GRID_W)
    col_start = jnp.clip(col - WIN_W // 2, 0, GRID_W - WIN_W)
    in_win = (col[None, :] >= col_start[:, None]) & (col[None, :] < col_start[:, None] + WIN_W)
    dc_idx = jnp.clip(col[None, :] - col[:, None], 1 - WIN_W, WIN_W - 1) + WIN_W - 1
    dr_idx = jnp.arange(kh)[None, :] - jnp.arange(kh)[:, None] + WIN_H - 1
    t = rpb[:, dr_idx][:, :, :, dc_idx]
    t = jnp.where(in_win[None, None, None], t, MASK_VALUE)
    return t.transpose(0, 1, 3, 2, 4).reshape(rpb.shape[0], kh, GRID_W, kh * GRID_W).astype(F32)


def _mm_f32(a, b, name):
    M, K = a.shape
    N = b.shape[1]
    tm = _tile(M, 256, 8)

    def body(a_ref, b_ref, o_ref):
        o_ref[...] = _dot(a_ref[...], b_ref[...], HI)

    return pl.pallas_call(
        body, name=name, grid=(M // tm,),
        in_specs=[pl.BlockSpec((tm, K), lambda i: (i, 0)), pl.BlockSpec((K, N), lambda i: (0, 0))],
        out_specs=pl.BlockSpec((tm, N), lambda i: (i, 0)),
        out_shape=jax.ShapeDtypeStruct((M, N), F32),
        compiler_params=_params(("parallel",)),
    )(a, b)


def _bias_tables_transpose(dbias, rows):
    kh = min(WIN_H, rows)
    nh = dbias.shape[0]
    col = np.arange(GRID_W)
    col_start = np.clip(col - WIN_W // 2, 0, GRID_W - WIN_W)
    in_win = (col[None, :] >= col_start[:, None]) & (col[None, :] < col_start[:, None] + WIN_W)
    dc_idx = np.clip(col[None, :] - col[:, None], 1 - WIN_W, WIN_W - 1) + WIN_W - 1
    ndc = 2 * WIN_W - 1
    onehot = np.zeros((GRID_W * GRID_W, LANES), np.float32)
    qq, kk = np.nonzero(in_win)
    onehot[qq * GRID_W + kk, dc_idx[qq, kk]] = 1.0
    x = dbias.reshape(nh, kh, GRID_W, kh, GRID_W).transpose(0, 1, 3, 2, 4).reshape(nh * kh * kh, GRID_W * GRID_W)
    z = _mm_f32(x, jnp.asarray(onehot), "rpb_fold")[:, :ndc].reshape(nh, kh, kh, ndc)
    fold = np.zeros((2 * WIN_H - 1, kh, kh), np.float32)
    for sh in range(kh):
        for j in range(kh):
            fold[j - sh + WIN_H - 1, sh, j] = 1.0
    return jnp.einsum("dsj,hsjc->hdc", jnp.asarray(fold), z, precision=HI)


def _na_geometry(S):
    rows = S // GRID_W
    kh = min(WIN_H, rows)

    def row_start(r):
        return jnp.clip(r - kh // 2, 0, rows - kh)

    return rows, kh, row_start


def _na_by_head(ref, rr, lane):
    t = ref[rr * GRID_W:(rr + 1) * GRID_W, :]
    zero = jnp.zeros_like(t)
    return jnp.concatenate([jnp.where(lane < NA_HEAD_DIM, t, zero), jnp.where(lane >= NA_HEAD_DIM, t, zero)], axis=0)


def _na_pick_head(t2, lane):
    return jnp.where(lane < NA_HEAD_DIM, t2[:GRID_W], t2[GRID_W:])


def _na_scores(q_ref, k_ref, b_ref, i, nrs, nb, S, L, row_start, lane):
    qh = jnp.concatenate([_na_by_head(q_ref, rr, lane) for rr in range(nrs)], axis=0)
    sc = _dot_nt(qh, k_ref[pl.ds(S, L), :])
    starts, shifts, sb = [], [], []
    for rr in range(nrs):
        r = i * nrs + rr
        rs = row_start(r)
        starts.append(pl.multiple_of(rs * GRID_W, GRID_W))
        shifts.append(r - rs)
        bias = jnp.concatenate([b_ref[0, r - rs], b_ref[1, r - rs]], axis=0)
        sb.append(_dot_nt(qh[rr * 2 * GRID_W:(rr + 1) * 2 * GRID_W], k_ref[pl.ds(starts[-1], nb), :]) + bias)
    return qh, jnp.concatenate(sb, axis=0), sc, starts, shifts


def _na_fwd(qs, kb, vb, bias, S, L):
    T, naw = qs.shape
    rows, kh, row_start = _na_geometry(S)
    nb = kh * GRID_W
    npair = naw // LANES

    nrs = min(NA_ROWS_PER_STEP, rows)
    assert rows % nrs == 0

    def body(q_ref, k_ref, v_ref, b_ref, o_ref, lse_ref):
        i = pl.program_id(1)
        lane = lax.broadcasted_iota(jnp.int32, (GRID_W, LANES), 1)
        _, sb, sc, starts, _ = _na_scores(q_ref, k_ref, b_ref, i, nrs, nb, S, L, row_start, lane)
        m = jnp.maximum(jnp.max(sb, axis=-1, keepdims=True), jnp.max(sc, axis=-1, keepdims=True))
        pb, pc = jnp.exp(sb - m), jnp.exp(sc - m)
        l = jnp.sum(pb, axis=-1, keepdims=True) + jnp.sum(pc, axis=-1, keepdims=True)
        inv = 1.0 / l
        pb16, pc16 = (pb * inv).astype(BF16), (pc * inv).astype(BF16)
        oc = _dot(pc16, v_ref[pl.ds(S, L), :])
        lse = jnp.broadcast_to(m + jnp.log(l), oc.shape)
        for rr in range(nrs):
            two = slice(rr * 2 * GRID_W, (rr + 1) * 2 * GRID_W)
            rsl = slice(rr * GRID_W, (rr + 1) * GRID_W)
            o2 = oc[two] + _dot(pb16[two], v_ref[pl.ds(starts[rr], nb), :])
            o_ref[rsl, :] = _na_pick_head(o2, lane).astype(o_ref.dtype)
            lse_ref[rsl, :] = _na_pick_head(lse[two], lane)

    blk = pl.BlockSpec((nrs * GRID_W, LANES), lambda p, i: (i, p))
    col = pl.BlockSpec((T, LANES), lambda p, i: (0, p))
    return pl.pallas_call(
        body, name="na_fwd", grid=(npair, rows // nrs),
        in_specs=[blk, col, col, pl.BlockSpec((2, kh, GRID_W, nb), lambda p, i: (p, 0, 0, 0))],
        out_specs=[blk, blk],
        out_shape=[jax.ShapeDtypeStruct((S, naw), BF16), jax.ShapeDtypeStruct((S, naw), F32)],
        compiler_params=_params(("parallel", "arbitrary")),
    )(qs, kb, vb, bias)


def _na_bwd(qs, kb, vb, bias, do, o, lse, S, L):
    T, naw = qs.shape
    rows, kh, row_start = _na_geometry(S)
    nb = kh * GRID_W
    npair = naw // LANES

    nrs = min(NA_ROWS_PER_STEP, rows)
    assert rows % nrs == 0

    def body(q_ref, k_ref, v_ref, b_ref, do_ref, o_ref, lse_ref, dq_ref, dk_ref, dv_ref, db_ref):
        i = pl.program_id(1)

        @pl.when(i == 0)
        def _():
            dk_ref[...] = jnp.zeros_like(dk_ref)
            dv_ref[...] = jnp.zeros_like(dv_ref)
            db_ref[...] = jnp.zeros_like(db_ref)

        lane = lax.broadcasted_iota(jnp.int32, (GRID_W, LANES), 1)
        qh, sb, sc, starts, shifts = _na_scores(q_ref, k_ref, b_ref, i, nrs, nb, S, L, row_start, lane)
        doh = jnp.concatenate([_na_by_head(do_ref, rr, lane) for rr in range(nrs)], axis=0)
        o2 = jnp.concatenate([o_ref[rr * GRID_W:(rr + 1) * GRID_W, :] for rr in range(nrs) for _ in range(2)], axis=0)
        lse = jnp.concatenate([lse_ref[rr * GRID_W:(rr + 1) * GRID_W, :][:, hh * NA_HEAD_DIM:hh * NA_HEAD_DIM + 1]
                               for rr in range(nrs) for hh in range(2)], axis=0)
        pb, pc = jnp.exp(sb - lse), jnp.exp(sc - lse)
        delta = jnp.sum(doh.astype(F32) * o2.astype(F32), axis=-1, keepdims=True)
        dpb = jnp.concatenate([_dot_nt(doh[rr * 2 * GRID_W:(rr + 1) * 2 * GRID_W], v_ref[pl.ds(starts[rr], nb), :])
                               for rr in range(nrs)], axis=0)
        dsb = pb * (dpb - delta)
        dsc = pc * (_dot_nt(doh, v_ref[pl.ds(S, L), :]) - delta)
        dsb16, dsc16, pb16, pc16 = dsb.astype(BF16), dsc.astype(BF16), pb.astype(BF16), pc.astype(BF16)
        dqc = _dot(dsc16, k_ref[pl.ds(S, L), :])
        dk_ref[pl.ds(S, L), :] += _dot_tn(dsc16, qh)
        dv_ref[pl.ds(S, L), :] += _dot_tn(pc16, doh)
        for rr in range(nrs):
            two = slice(rr * 2 * GRID_W, (rr + 1) * 2 * GRID_W)
            band = pl.ds(starts[rr], nb)
            dq2 = dqc[two] + _dot(dsb16[two], k_ref[band, :])
            dq_ref[rr * GRID_W:(rr + 1) * GRID_W, :] = _na_pick_head(dq2, lane)
            dk_ref[band, :] += _dot_tn(dsb16[two], qh[two])
            dv_ref[band, :] += _dot_tn(pb16[two], doh[two])
            for hh in range(2):
                db_ref[hh, shifts[rr]] += dsb[(2 * rr + hh) * GRID_W:(2 * rr + hh + 1) * GRID_W]

    blk = pl.BlockSpec((nrs * GRID_W, LANES), lambda p, r: (r, p))
    col = pl.BlockSpec((T, LANES), lambda p, r: (0, p))
    return pl.pallas_call(
        body, name="na_bwd", grid=(npair, rows // nrs),
        in_specs=[blk, col, col, pl.BlockSpec((2, kh, GRID_W, nb), lambda p, r: (p, 0, 0, 0)), blk, blk, blk],
        out_specs=[blk, col, col, pl.BlockSpec((2, kh, GRID_W, nb), lambda p, r: (p, 0, 0, 0))],
        out_shape=[jax.ShapeDtypeStruct((S, naw), F32), jax.ShapeDtypeStruct((T, naw), F32),
                   jax.ShapeDtypeStruct((T, naw), F32), jax.ShapeDtypeStruct(bias.shape, F32)],
        compiler_params=_params(("parallel", "arbitrary")),
    )(qs, kb, vb, bias, do, o, lse)


def _hg_cols(naw, hgf, rev):
    qcol = (3 * naw) // hgf
    fcol = (3 * naw + hgf * (2 if rev else 1)) // hgf
    icol = (3 * naw + 3 * hgf) // hgf
    return qcol, fcol, icol


def _hg_chunk_order(S, L, rev):
    ncl, ncc = S // HG_CHUNK, L // HG_CHUNK
    nc = ncl + ncc

    def chunk_of(i):
        if rev:
            return nc - 1 - i
        return jnp.where(i < ncc, ncl + i, i - ncc)

    return nc, ncl, chunk_of


def _hg_gates(q, z, lb, rev):
    row = lax.broadcasted_iota(jnp.int32, (HG_CHUNK, HG_CHUNK), 0)
    colm = lax.broadcasted_iota(jnp.int32, (HG_CHUNK, HG_CHUNK), 1)
    tri = (colm >= row) if rev else (row >= colm)
    trif = tri.astype(F32)
    sig = _sigmoid(z)
    f = lb + (1.0 - lb) * sig
    lf = jnp.log(f)
    k = 1.0 - f
    cum = _dot(trif, lf, HI)
    mid = cum[HG_CHUNK // 2:HG_CHUNK // 2 + 1, :]
    last = cum[0:1, :] if rev else cum[HG_CHUNK - 1:HG_CHUNK, :]
    eq = jnp.exp(jnp.clip(cum - mid, -EXP_CLAMP, EXP_CLAMP))
    ek = jnp.exp(jnp.clip(mid - cum, -EXP_CLAMP, EXP_CLAMP))
    return tri, trif, sig, f, k, cum, last, eq, ek


def _hg_fwd(u, lbr, S, L, naw, hgf, rev):
    T = S + L
    nh = hgf // HG_DIM
    nc, ncl, chunk_of = _hg_chunk_order(S, L, rev)
    qcol, fcol, icol = _hg_cols(naw, hgf, rev)

    def body(q_ref, z_ref, v_ref, lb_ref, o_ref, st_ref, state):
        i = pl.program_id(0)

        @pl.when(i == 0)
        def _():
            state[...] = jnp.zeros_like(state)

        q, z, v = q_ref[...], z_ref[...], v_ref[...]
        tri, _, _, _, k, cum, last, eq, ek = _hg_gates(q, z, lb_ref[...], rev)
        qe, ke = (q * eq).astype(BF16), (k * ek).astype(BF16)
        qd, kd = (q * jnp.exp(cum)).astype(BF16), (k * jnp.exp(last - cum)).astype(BF16)
        v16, el = v.astype(BF16), jnp.exp(last)
        for h in range(nh):
            sl = slice(h * HG_DIM, (h + 1) * HG_DIM)
            a = jnp.where(tri, _dot_nt(qe[:, sl], ke[:, sl]), 0.0)
            s0 = state[h]
            st_ref[h] = s0
            o_ref[:, sl] = _dot(a.astype(BF16), v16[:, sl]) + _dot_nt(qd[:, sl], s0.astype(BF16))
            state[h] = s0 * el[:, sl] + _dot_tn(v16[:, sl], kd[:, sl])

    def blk(cb):
        return pl.BlockSpec((HG_CHUNK, hgf), lambda i: (chunk_of(i), cb))

    return pl.pallas_call(
        body, name="hg_fwd_rev" if rev else "hg_fwd", grid=(nc,),
        in_specs=[blk(qcol), blk(fcol), blk(icol), pl.BlockSpec((1, hgf), lambda i: (0, 0))],
        out_specs=[pl.BlockSpec((HG_CHUNK, hgf), lambda i: (chunk_of(i), 0)),
                   pl.BlockSpec((None, nh, HG_DIM, HG_DIM), lambda i: (chunk_of(i), 0, 0, 0))],
        out_shape=[jax.ShapeDtypeStruct((T, hgf), F32), jax.ShapeDtypeStruct((nc, nh, HG_DIM, HG_DIM), F32)],
        scratch_shapes=[pltpu.VMEM((nh, HG_DIM, HG_DIM), F32)],
        compiler_params=_params(("arbitrary",)),
    )(u, u, u, lbr)


def _hg_bwd(u, lbr, st, do, S, L, naw, hgf, rev):
    T = S + L
    nh = hgf // HG_DIM
    nc, ncl, chunk_fwd = _hg_chunk_order(S, L, rev)
    qcol, fcol, icol = _hg_cols(naw, hgf, rev)

    def chunk_of(j):
        return chunk_fwd(nc - 1 - j)

    def body(q_ref, z_ref, v_ref, lb_ref, st_ref, do_ref, dq_ref, dz_ref, dv_ref, dlb_ref, dstate,
             dqe_s, dke_s, dqd_s, dkd_s, dl_s):
        j = pl.program_id(0)

        @pl.when(j == 0)
        def _():
            dstate[...] = jnp.zeros_like(dstate)
            dlb_ref[...] = jnp.zeros_like(dlb_ref)

        q, z, v = q_ref[...], z_ref[...], v_ref[...]
        lb = lb_ref[...]
        tri, trif, sig, f, k, cum, last, eq, ek = _hg_gates(q, z, lb, rev)
        ec, el, ekd = jnp.exp(cum), jnp.exp(last), jnp.exp(last - cum)
        qe, ke, qd, kd = q * eq, k * ek, q * ec, k * ekd
        qe16, ke16, qd16, kd16 = [t.astype(BF16) for t in (qe, ke, qd, kd)]
        dout = jnp.where(chunk_of(j) < ncl, do_ref[...], 0.0)
        dout16 = dout.astype(BF16)
        for h in range(nh):
            sl = slice(h * HG_DIM, (h + 1) * HG_DIM)
            a = jnp.where(tri, _dot_nt(qe16[:, sl], ke16[:, sl]), 0.0).astype(BF16)
            s0 = st_ref[h]
            ds1 = dstate[h]
            dv_ref[:, sl] = _dot_tn(a, dout16[:, sl]) + _dot_nt(kd16[:, sl], ds1.astype(BF16))
            da = jnp.where(tri, _dot_nt(dout[:, sl], v[:, sl], HI), 0.0)
            dqe_s[:, sl] = _dot(da, ke[:, sl], HI)
            dke_s[:, sl] = _dot_tn(da, qe[:, sl], HI)
            dqd_s[:, sl] = _dot(dout[:, sl], s0, HI)
            dkd_s[:, sl] = _dot(v[:, sl], ds1, HI)
            dl_s[:, sl] = _colsum(ds1 * s0)
            dstate[h] = _dot_tn(dout16[:, sl], qd16[:, sl]) + ds1 * el[:, sl]
        dqe, dke, dqd, dkd = dqe_s[...], dke_s[...], dqd_s[...], dkd_s[...]
        dq_ref[...] = dqe * eq + dqd * ec
        dk = dke * ek + dkd * ekd
        dcum = dqe * qe - dke * ke + dqd * qd - dkd * kd
        dlast = _colsum(dkd * kd) + el * dl_s[...]
        dlf = _dot_tn(trif, dcum, HI) + dlast
        df = dlf / f - dk
        dz_ref[...] = df * (1.0 - lb) * sig * (1.0 - sig)
        dlb_ref[...] += _colsum(df * (1.0 - sig))

    def blk(cb):
        return pl.BlockSpec((HG_CHUNK, hgf), lambda j: (chunk_of(j), cb))

    oblk = pl.BlockSpec((HG_CHUNK, hgf), lambda j: (chunk_of(j), 0))
    wide = pltpu.VMEM((HG_CHUNK, hgf), F32)
    return pl.pallas_call(
        body, name="hg_bwd_rev" if rev else "hg_bwd", grid=(nc,),
        in_specs=[blk(qcol), blk(fcol), blk(icol), pl.BlockSpec((1, hgf), lambda j: (0, 0)),
                  pl.BlockSpec((None, nh, HG_DIM, HG_DIM), lambda j: (chunk_of(j), 0, 0, 0)),
                  pl.BlockSpec((HG_CHUNK, hgf), lambda j: (jnp.minimum(chunk_of(j), ncl - 1), 0))],
        out_specs=[oblk, oblk, oblk, pl.BlockSpec((1, hgf), lambda j: (0, 0))],
        out_shape=[jax.ShapeDtypeStruct((T, hgf), F32)] * 3 + [jax.ShapeDtypeStruct((1, hgf), F32)],
        scratch_shapes=[pltpu.VMEM((nh, HG_DIM, HG_DIM), F32), wide, wide, wide, wide,
                        pltpu.VMEM((1, hgf), F32)],
        compiler_params=_params(("arbitrary",)),
    )(u, u, u, lbr, st, do)


def _adamw(w, g, m, v, name):
    shape = w.shape
    if w.ndim != 2 or shape[0] % 8 or shape[1] % LANES:
        w, g, m, v = [a.reshape(1, -1) for a in (w, g, m, v)]
    r, cw = w.shape
    tm = _tile(r, max(8, (1 << 19) // cw), 8) if r % 8 == 0 else r
    c1 = 1.0 / (1.0 - ADAM_B1 ** ADAM_STEP)
    c2 = 1.0 / (1.0 - ADAM_B2 ** ADAM_STEP)

    def body(w_ref, g_ref, m_ref, v_ref, d_ref, nm_ref, nv_ref):
        gg = g_ref[...]
        nm = ADAM_B1 * m_ref[...] + (1.0 - ADAM_B1) * gg
        nv = ADAM_B2 * v_ref[...] + (1.0 - ADAM_B2) * (gg * gg)
        d_ref[...] = -ADAM_LR * ((nm * c1) / (jnp.sqrt(nv * c2) + ADAM_EPS) + ADAM_WD * w_ref[...])
        nm_ref[...] = nm
        nv_ref[...] = nv

    spec = pl.BlockSpec((tm, cw), lambda i: (i, 0))
    outs = pl.pallas_call(
        body, name=name, grid=(r // tm,), in_specs=[spec] * 4, out_specs=[spec] * 3,
        out_shape=[jax.ShapeDtypeStruct((r, cw), F32)] * 3,
        compiler_params=_params(("parallel",)),
    )(w, g, m, v)
    return [o.reshape(shape) for o in outs]


def kernel(x, c, ctx, c_ctx, w_ada, b_ada, norm1_g, w_in, na_rpb, hg_lb_logits, hg_norm_g, w_pa, w_pb, w_out, norm2_g, w_ffn_in, w_ffn_out, final_g, loss_target, m_c_ctx, m_w_ada, m_b_ada, m_norm1_g, m_w_in, m_na_rpb, m_hg_lb_logits, m_hg_norm_g, m_w_pa, m_w_pb, m_w_out, m_norm2_g, m_w_ffn_in, m_w_ffn_out, m_final_g, v_c_ctx, v_w_ada, v_b_ada, v_norm1_g, v_w_in, v_na_rpb, v_hg_lb_logits, v_hg_norm_g, v_w_pa, v_w_pb, v_w_out, v_norm2_g, v_w_ffn_in, v_w_ffn_out, v_final_g):
    xi, yi, ci = lax.axis_index("x"), lax.axis_index("y"), lax.axis_index("c")
    sidx = 2 * xi + yi
    eidx = 4 * xi + 2 * yi + ci
    scidx = jnp.stack([sidx, ci]).astype(jnp.int32)

    S, D = x.shape[1], x.shape[2]
    L = ctx.shape[1]
    T = S + L
    naw = NA_HEADS * NA_HEAD_DIM
    hgf = HG_HEADS * HG_DIM
    inw = 3 * naw + 5 * hgf + 2 * D
    fh = w_ffn_out.shape[1] * 4
    ads = w_ada.shape[2]
    fs = hg_lb_logits.shape[2]
    rows = S // GRID_W
    tr = _tile(L, 256)
    nlat, nall = S // tr, T // tr
    assert naw == hgf and D % naw == 0 and S % tr == 0 and 2 * hgf <= D

    pack0 = jnp.concatenate([c, jnp.pad(hg_lb_logits.reshape(1, -1), ((0, 0), (0, D - 4 * fs))),
                             jnp.zeros((6, D), F32)], axis=0)
    g0 = _all_gather8(pack0).reshape(8, 8, D)
    cs = g0[:, 0]
    lbl = g0[::2, 1, :4 * fs].reshape(4, 2, 2, fs).transpose(1, 2, 0, 3).reshape(2, 2, 4 * fs)
    p_lb = jax.nn.softmax(lbl, axis=0)
    lb = p_lb[0]
    lbb = [lb[d].reshape(1, hgf) for d in range(2)]

    cin = jnp.concatenate([cs, c_ctx[None], jnp.zeros((7, D), F32)], axis=0)
    b_sh = lax.dynamic_slice(b_ada, (0, sidx * ads), (1, ads))
    modp = _ada_fwd(cin, w_ada[0], b_sh)
    modfull = _all_gather8(modp).reshape(8, 16, ads)[::2].transpose(1, 0, 2).reshape(16, 4 * ads)
    mod_e = jnp.pad(lax.dynamic_index_in_dim(modfull, eidx, 0, keepdims=False).reshape(N_MOD, D), ((0, 2), (0, 0)))
    mod_c = jnp.pad(modfull[8].reshape(N_MOD, D), ((0, 2), (0, 0)))

    names = ["w_in", "w_pa", "w_pb", "w_out", "w_ffn_in", "w_ffn_out"]
    placed = [_cast_place(scidx[:1], w[0], "cast_" + nm)
              for w, nm in zip((w_in, w_pa, w_pb, w_out, w_ffn_in, w_ffn_out), names)]
    gathered = _gather_weights(placed)
    win3, wpa3, wpb3, wout3, wi3, wfo3 = [g.reshape(4, 2 * g.shape[2], g.shape[3]) for g in gathered]
    wout1 = wout3.reshape(1, D, D)
    wfo1 = wfo3.reshape(1, fh, D)

    xx = jnp.concatenate([x[0], ctx[0]], axis=0)

    def f_ln1(i, rv, vv):
        xt, = rv
        g, me, mc = vv
        isc = i >= nlat
        sh = jnp.where(isc, mc[0:1], me[0:1])
        sc = jnp.where(isc, mc[1:2], me[1:2])
        return [xt * _rms(xt) * g * (1.0 + sc) + sh], []

    hb, = _rowwise(f_ln1, nall, tr, [(xx, D, 0, None)], [norm1_g, mod_e, mod_c], [(D, BF16)], [], "ln1")
    u = _mm_nn(hb, win3, F32, "mm_in")

    scale = NA_HEAD_DIM ** -0.5

    def f_qkv(i, rv, vv):
        q, k, v = rv
        return [q * scale, k, v], []

    qs, kb, vb = _rowwise(f_qkv, nall, tr, [(u, naw, 0, None), (u, naw, 1, None), (u, naw, 2, None)], [],
                          [(naw, BF16)] * 3, [], "qkv_cast")
    bias = _bias_tables(na_rpb[0], rows)
    o_na, lse = _na_fwd(qs, kb, vb, bias, S, L)

    o_f, st_f = _hg_fwd(u, lbb[0], S, L, naw, hgf, False)
    o_b, st_b = _hg_fwd(u, lbb[1], S, L, naw, hgf, True)

    hgn = jnp.tile(hg_norm_g, (1, HG_HEADS))
    hog_cb = (3 * naw + 4 * hgf) // hgf
    ga_cb = (3 * naw + 5 * hgf) // D
    gb_cb = ga_cb + 1

    def heads_rms(o):
        return jnp.concatenate([jnp.broadcast_to(_rms(o[:, h * HG_DIM:(h + 1) * HG_DIM]), (o.shape[0], HG_DIM))
                                for h in range(HG_HEADS)], axis=1)

    def f_readout(i, rv, vv):
        of, ob_, hog = rv
        g, = vv
        o = of + ob_
        return [o * heads_rms(o) * g * _silu(hog)], []

    ob, = _rowwise(f_readout, nlat, tr, [(o_f, hgf, 0, None), (o_b, hgf, 0, None), (u, hgf, hog_cb, None)],
                   [hgn], [(hgf, BF16)], [], "hg_readout")

    ya = _mm_nn(o_na, wpa3, F32, "mm_pa")
    yb = _mm_nn(ob, wpb3, F32, "mm_pb")

    def f_merge(i, rv, vv):
        ya_, yb_, ga, gb = rv
        return [_sigmoid(ga) * ya_ + _sigmoid(gb) * yb_], []

    yv, = _rowwise(f_merge, nlat, tr, [(ya, D, 0, None), (yb, D, 0, None), (u, D, ga_cb, None), (u, D, gb_cb, None)],
                   [], [(D, BF16)], [], "merge")
    z = _mm_nn(yv, wout1, F32, "mm_out")

    def f_res1(i, rv, vv):
        xt, zt = rv
        g, me = vv
        x1 = xt + me[2:3] * zt
        return [x1, x1 * _rms(x1) * g * (1.0 + me[4:5]) + me[3:4]], []

    x1, h2 = _rowwise(f_res1, nlat, tr, [(xx, D, 0, None), (z, D, 0, None)], [norm2_g, mod_e],
                      [(D, F32), (D, BF16)], [], "res1_ln2")
    au = _mm_nn(h2, wi3, F32, "mm_ffn_in")
    tf = _tile(L, 128)
    nlf = S // tf

    def f_swiglu(i, rv, vv):
        a, uu = rv
        return [_silu(a) * uu], []

    sw, = _rowwise(f_swiglu, nlf, tf, [(au, fh, 0, None), (au, fh, 1, None)], [], [(fh, BF16)], [], "swiglu")
    ff = _mm_nn(sw, wfo1, F32, "mm_ffn_out")

    fg = final_g.reshape(1, D)

    def f_final(i, rv, vv):
        x1t, ft, tg = rv
        g, me = vv
        x2 = x1t + me[5:6] * ft
        r3 = _rms(x2)
        xn = x2 * r3
        err = xn * g - tg
        dyy = err * (1.0 / D)
        dxn = dyy * g
        dx2 = r3 * (dxn - xn * jnp.mean(dxn * xn, axis=-1, keepdims=True))
        return [dx2, dx2 * me[5:6]], [_colsum(err * err), _colsum(dyy * xn), _colsum(dx2 * ft)]

    dx2, dfb, loss_cols, dfg, dg2 = _rowwise(
        f_final, nlat, tr, [(x1, D, 0, None), (ff, D, 0, None), (loss_target[0], D, 0, None)], [fg, mod_e],
        [(D, F32), (D, BF16)], [D, D, D], "final_loss")

    dsw = _mm_nt(dfb, wfo1, F32, "mm_d_sw")

    def f_dswiglu(i, rv, vv):
        d, a, uu = rv
        return [jnp.concatenate([d * uu * _dsilu(a), d * _silu(a)], axis=1)], []

    dau, = _rowwise(f_dswiglu, nlf, tf, [(dsw, fh, 0, None), (au, fh, 0, None), (au, fh, 1, None)], [],
                    [(2 * fh, BF16)], [], "swiglu_bwd")
    g_wfo = _mm_tn(sw, dfb, 1, "mm_dw_ffn_out").reshape(4, 2, fh // 8, D)
    dh2 = _mm_nt(dau, wi3, F32, "mm_d_h2")
    g_wi = _mm_tn(h2, dau, 4, "mm_dw_ffn_in")

    def f_ln2_bwd(i, rv, vv):
        dh, x1t, dx2t, zt = rv
        g, me = vv
        r2 = _rms(x1t)
        xn = x1t * r2
        dxn = dh * g * (1.0 + me[4:5])
        dx1 = dx2t + r2 * (dxn - xn * jnp.mean(dxn * xn, axis=-1, keepdims=True))
        return ([dx1, dx1 * me[2:3]],
                [_colsum(dh), _colsum(dh * xn * g), _colsum(dh * xn * (1.0 + me[4:5])), _colsum(dx1 * zt)])

    dx1, dzb, dsh2, dsc2, dn2g, dg1 = _rowwise(
        f_ln2_bwd, nlat, tr, [(dh2, D, 0, None), (x1, D, 0, None), (dx2, D, 0, None), (z, D, 0, None)],
        [norm2_g, mod_e], [(D, F32), (D, BF16)], [D, D, D, D], "ln2_bwd")

    dy = _mm_nt(dzb, wout1, F32, "mm_d_y")
    g_wout = _mm_tn(yv, dzb, 1, "mm_dw_out").reshape(4, 2, D // 8, D)

    def f_dmerge(i, rv, vv):
        d, ya_, yb_, ga, gb = rv
        sa, sb_ = _sigmoid(ga), _sigmoid(gb)
        return [d * sa, d * sb_, d * ya_ * sa * (1.0 - sa), d * yb_ * sb_ * (1.0 - sb_)], []

    dya, dyb, dga, dgb = _rowwise(
        f_dmerge, nlat, tr, [(dy, D, 0, None), (ya, D, 0, None), (yb, D, 0, None), (u, D, ga_cb, None),
                             (u, D, gb_cb, None)], [], [(D, BF16)] * 4, [], "merge_bwd")
    d_ona = _mm_nt(dya, wpa3, BF16, "mm_d_ona")
    d_ob = _mm_nt(dyb, wpb3, F32, "mm_d_ob")
    g_wpa = _mm_tn(o_na, dya, 4, "mm_dw_pa")
    g_wpb = _mm_tn(ob, dyb, 4, "mm_dw_pb")

    def f_dreadout(i, rv, vv):
        d, of, ob_, hog = rv
        g, = vv
        o = of + ob_
        on = o * heads_rms(o)
        t = d * _silu(hog) * g
        mt = jnp.concatenate([jnp.broadcast_to(jnp.mean((t * on)[:, h * HG_DIM:(h + 1) * HG_DIM], axis=-1,
                                                        keepdims=True), (o.shape[0], HG_DIM))
                              for h in range(HG_HEADS)], axis=1)
        do_ = heads_rms(o) * (t - on * mt)
        return [do_, d * on * g * _dsilu(hog)], [_colsum(d * _silu(hog) * on)]

    do_hg, dhog, dhgn = _rowwise(
        f_dreadout, nlat, tr, [(d_ob, hgf, 0, None), (o_f, hgf, 0, None), (o_b, hgf, 0, None),
                               (u, hgf, hog_cb, None)], [hgn], [(hgf, F32), (hgf, BF16)], [hgf], "hg_readout_bwd")

    dq_f, dz_f, dv_f, dlb_f = _hg_bwd(u, lbb[0], st_f, do_hg, S, L, naw, hgf, False)
    dq_b, dz_b, dv_b, dlb_b = _hg_bwd(u, lbb[1], st_b, do_hg, S, L, naw, hgf, True)
    dq_na, dk_na, dv_na, dbias = _na_bwd(qs, kb, vb, bias, d_ona, o_na, lse, S, L)

    ta = _tile(L, 128)
    nla, naa = S // ta, T // ta
    lat = lambda i: jnp.minimum(i, nla - 1)

    def f_assemble(i, rv, vv):
        dqn, dk, dv, dqf, dqb, dzf, dzb_, dvf, dvb, dho, dga_, dgb_ = rv
        keep = (i < nla).astype(F32)
        return [jnp.concatenate([dqn * (scale * keep), dk, dv, dqf + dqb, dzf, dzb_, dvf + dvb,
                                 dho.astype(F32) * keep, dga_.astype(F32) * keep, dgb_.astype(F32) * keep],
                                axis=1)], []

    du, = _rowwise(
        f_assemble, naa, ta,
        [(dq_na, naw, 0, lat), (dk_na, naw, 0, None), (dv_na, naw, 0, None), (dq_f, hgf, 0, None),
         (dq_b, hgf, 0, None), (dz_f, hgf, 0, None), (dz_b, hgf, 0, None), (dv_f, hgf, 0, None),
         (dv_b, hgf, 0, None), (dhog, hgf, 0, lat), (dga, D, 0, lat), (dgb, D, 0, lat)],
        [], [(inw, BF16)], [], "assemble_du")

    dh = _mm_nt(du, win3, F32, "mm_d_h")
    g_win = _mm_tn(hb, du, 4, "mm_dw_in")

    def f_ln1_bwd(i, rv, vv):
        dht, xt, dx1t = rv
        g, me = vv
        r1 = _rms(xt)
        xn = xt * r1
        dxn = dht * g * (1.0 + me[1:2])
        dx = dx1t + r1 * (dxn - xn * jnp.mean(dxn * xn, axis=-1, keepdims=True))
        return [dx], [_colsum(dht), _colsum(dht * xn * g), _colsum(dht * xn * (1.0 + me[1:2]))]

    grad_x, dsh1, dsc1, dn1g_l = _rowwise(
        f_ln1_bwd, nlat, tr, [(dh, D, 0, None), (xx, D, 0, None), (dx1, D, 0, None)], [norm1_g, mod_e],
        [(D, F32)], [D, D, D], "ln1_bwd")

    def f_ln1_bwd_ctx(i, rv, vv):
        dht, xt = rv
        g, mc = vv
        xn = xt * _rms(xt)
        return [], [_colsum(dht), _colsum(dht * xn * g), _colsum(dht * xn * (1.0 + mc[1:2]))]

    ctx_rows = lambda i: i + nlat
    dsh1c, dsc1c, dn1g_c = _rowwise(
        f_ln1_bwd_ctx, nall - nlat, tr, [(dh, D, 0, ctx_rows), (xx, D, 0, ctx_rows)], [norm1_g, mod_c],
        [], [D, D, D], "ln1_bwd_ctx")

    drpb = _bias_tables_transpose(dbias, rows).reshape(1, -1)
    nrp = -(-drpb.shape[1] // D)
    drpb_rows = jnp.pad(drpb, ((0, 0), (0, nrp * D - drpb.shape[1]))).reshape(nrp, D)
    dlb = jnp.concatenate([dlb_f, dlb_b], axis=1)
    dhg = jnp.sum(dhgn.reshape(HG_HEADS, HG_DIM), axis=0, keepdims=True)

    def wide(v):
        return jnp.pad(v, ((0, 0), (0, D - v.shape[1])))

    pack_rows = [loss_cols, dfg, dn2g, dn1g_l + dn1g_c, dsh1, dsc1, dg1, dsh2, dsc2, dg2, dsh1c, dsc1c,
                 wide(dhg), wide(dlb), drpb_rows]
    pack = jnp.concatenate(pack_rows, axis=0)
    npk = -(-pack.shape[0] // 8) * 8
    pack = jnp.pad(pack, ((0, npk - pack.shape[0]), (0, 0)))
    gp = _all_gather8(pack).reshape(8, npk, D)
    tot = _sum8(gp, "sum_small_grads")

    loss = (0.5 / D) * jnp.sum(tot[0])
    grad_final_g = tot[1]
    grad_norm2_g = tot[2:3]
    grad_norm1_g = tot[3:4]
    grad_hg_norm_g = tot[12:13, :HG_DIM]
    dlb_tot = tot[13, :2 * hgf].reshape(2, hgf)
    grad_na_rpb = tot[14:14 + nrp].reshape(-1)[:drpb.shape[1]].reshape(na_rpb.shape)
    dlog = jnp.stack([dlb_tot * p_lb[0] * (1.0 - p_lb[0]), -dlb_tot * p_lb[0] * p_lb[1]], axis=0)
    grad_hg_lb = lax.dynamic_slice(dlog, (0, 0, sidx * fs), (2, 2, fs))

    dmod_all = gp[:, 4:10].reshape(8, N_MOD * D)
    dmod_ctx = jnp.concatenate([tot[10], tot[11], jnp.zeros((4 * D,), F32)])[None]
    dm16 = jnp.concatenate([dmod_all, dmod_ctx, jnp.zeros((7, N_MOD * D), F32)], axis=0)
    grad_b_ada = jnp.sum(dm16, axis=0, keepdims=True)
    dm_sh = lax.dynamic_slice(dm16, (0, sidx * ads), (16, ads))
    g_wada, dcin = _ada_bwd(cin, w_ada[0], dm_sh)
    gc = _all_gather8(dcin[8:16]).reshape(8, 8, D)
    grad_c_ctx = (gc[0, 0] + gc[2, 0] + gc[4, 0] + gc[6, 0]) * _dsilu(c_ctx)

    g_win, g_wpa, g_wpb, g_wout, g_wi, g_wfo = _reduce_scatter(
        scidx, [g_win, g_wpa, g_wpb, g_wout, g_wi, g_wfo], names)

    grads = {
        "c_ctx": grad_c_ctx, "w_ada": g_wada[None], "b_ada": grad_b_ada, "norm1_g": grad_norm1_g,
        "w_in": g_win[None], "na_rpb": grad_na_rpb, "hg_lb_logits": grad_hg_lb, "hg_norm_g": grad_hg_norm_g,
        "w_pa": g_wpa[None], "w_pb": g_wpb[None], "w_out": g_wout[None], "norm2_g": grad_norm2_g,
        "w_ffn_in": g_wi[None], "w_ffn_out": g_wfo[None], "final_g": grad_final_g,
    }
    weights = {
        "c_ctx": (c_ctx, m_c_ctx, v_c_ctx), "w_ada": (w_ada, m_w_ada, v_w_ada), "b_ada": (b_ada, m_b_ada, v_b_ada),
        "norm1_g": (norm1_g, m_norm1_g, v_norm1_g), "w_in": (w_in, m_w_in, v_w_in),
        "na_rpb": (na_rpb, m_na_rpb, v_na_rpb), "hg_lb_logits": (hg_lb_logits, m_hg_lb_logits, v_hg_lb_logits),
        "hg_norm_g": (hg_norm_g, m_hg_norm_g, v_hg_norm_g), "w_pa": (w_pa, m_w_pa, v_w_pa),
        "w_pb": (w_pb, m_w_pb, v_w_pb), "w_out": (w_out, m_w_out, v_w_out),
        "norm2_g": (norm2_g, m_norm2_g, v_norm2_g), "w_ffn_in": (w_ffn_in, m_w_ffn_in, v_w_ffn_in),
        "w_ffn_out": (w_ffn_out, m_w_ffn_out, v_w_ffn_out), "final_g": (final_g, m_final_g, v_final_g),
    }
    order = list(weights)
    deltas, new_ms, new_vs = [], [], []
    for nm in order:
        w, m, v = weights[nm]
        g = grads[nm].reshape(w.shape)
        grads[nm] = g
        if w.ndim == 3 and w.shape[0] == 1:
            d_, m_, v_ = _adamw(w[0], g[0], m[0], v[0], "adamw_" + nm)
            d_, m_, v_ = d_[None], m_[None], v_[None]
        else:
            d_, m_, v_ = _adamw(w, g, m, v, "adamw_" + nm)
        deltas.append(d_)
        new_ms.append(m_)
        new_vs.append(v_)

    return (loss, grad_x[None], *[grads[nm] for nm in order], *deltas, *new_ms, *new_vs)
```

```python
import numpy as np

import jax
import jax.numpy as jnp
from jax import lax
from jax.experimental import pallas as pl
from jax.experimental.pallas import tpu as pltpu

F32 = jnp.float32
BF16 = jnp.bfloat16

GRID_W = 64
WIN_H = 8
WIN_W = 16
NA_HEADS = 16
NA_HEAD_DIM = 64
HG_HEADS = 8
HG_DIM = 128
HG_CHUNK = 64
N_MOD = 6
EPS = 1e-6
ADAM_LR = 0.001
ADAM_B1 = 0.9
ADAM_B2 = 0.999
ADAM_EPS = 1e-08
ADAM_WD = 0.01
ADAM_STEP = 10

LANES = 128
NA_ROWS_PER_STEP = 8
VMEM_LIMIT = 56 * 1024 * 1024
MASK_VALUE = -1e30
EXP_CLAMP = 80.0
MESH_ID = pl.DeviceIdType.MESH
HI = lax.Precision.HIGHEST


def _tile(dim, target, mult=LANES):
    best = None
    t = mult
    while t <= min(dim, target):
        if dim % t == 0:
            best = t
        t += mult
    assert best is not None, (dim, target, mult)
    return best


def _params(sem):
    return pltpu.CompilerParams(dimension_semantics=sem, vmem_limit_bytes=VMEM_LIMIT)


def _dot(a, b, precision=None):
    return jnp.dot(a, b, preferred_element_type=F32, precision=precision)


def _dot_nt(a, b, precision=None):
    return lax.dot_general(a, b, (((1,), (1,)), ((), ())), preferred_element_type=F32, precision=precision)


def _dot_tn(a, b, precision=None):
    return lax.dot_general(a, b, (((0,), (0,)), ((), ())), preferred_element_type=F32, precision=precision)


def _split2(v):
    hi = v.astype(BF16)
    return hi, (v - hi.astype(F32)).astype(BF16)


def _dot_x3(dot, a2, b2):
    return dot(a2[0], b2[0]) + (dot(a2[0], b2[1]) + dot(a2[1], b2[0]))


def _sigmoid(v):
    return 1.0 / (1.0 + jnp.exp(-v))


def _mm_nn(a, b3, out_dtype, name):
    M, K = a.shape
    nsh, _, Ns = b3.shape
    tm, tn, tk = _tile(M, 1024), _tile(Ns, 1408), _tile(K, 2048)
    tps, nk = Ns // tn, K // tk

    def body(a_ref, b_ref, o_ref, acc):
        k = pl.program_id(2)

        @pl.when(k == 0)
        def _():
            acc[...] = jnp.zeros_like(acc)

        acc[...] += _dot(a_ref[...], b_ref[...])

        @pl.when(k == nk - 1)
        def _():
            o_ref[...] = acc[...].astype(o_ref.dtype)

    return pl.pallas_call(
        body, name=name, grid=(M // tm, nsh * tps, nk),
        in_specs=[pl.BlockSpec((tm, tk), lambda m, n, k: (m, k)),
                  pl.BlockSpec((None, tk, tn), lambda m, n, k: (n // tps, k, n % tps))],
        out_specs=pl.BlockSpec((tm, tn), lambda m, n, k: (m, n)),
        out_shape=jax.ShapeDtypeStruct((M, nsh * Ns), out_dtype),
        scratch_shapes=[pltpu.VMEM((tm, tn), F32)],
        compiler_params=_params(("parallel", "parallel", "arbitrary")),
    )(a, b3)


def _mm_nt(a, b3, out_dtype, name):
    M = a.shape[0]
    nsh, Kw, Ns = b3.shape
    tm, tn, tk = _tile(M, 1024), _tile(Kw, 1408), _tile(Ns, 2048)
    kps = Ns // tk
    nk = nsh * kps

    def body(a_ref, b_ref, o_ref, acc):
        k = pl.program_id(2)

        @pl.when(k == 0)
        def _():
            acc[...] = jnp.zeros_like(acc)

        acc[...] += _dot_nt(a_ref[...], b_ref[...])

        @pl.when(k == nk - 1)
        def _():
            o_ref[...] = acc[...].astype(o_ref.dtype)

    return pl.pallas_call(
        body, name=name, grid=(M // tm, Kw // tn, nk),
        in_specs=[pl.BlockSpec((tm, tk), lambda m, n, k: (m, k)),
                  pl.BlockSpec((None, tn, tk), lambda m, n, k: (k // kps, n, k % kps))],
        out_specs=pl.BlockSpec((tm, tn), lambda m, n, k: (m, n)),
        out_shape=jax.ShapeDtypeStruct((M, Kw), out_dtype),
        scratch_shapes=[pltpu.VMEM((tm, tn), F32)],
        compiler_params=_params(("parallel", "parallel", "arbitrary")),
    )(a, b3)


def _mm_tn(a, g, nsh, name):
    Tk, M = a.shape
    Ns = g.shape[1] // nsh
    tm, tn, tk = _tile(M // 2, 1408), _tile(Ns, 1408), _tile(Tk, 1024)
    mh, tps, nk = (M // 2) // tm, Ns // tn, Tk // tk

    def body(a_ref, g_ref, o_ref, acc):
        k = pl.program_id(2)

        @pl.when(k == 0)
        def _():
            acc[...] = jnp.zeros_like(acc)

        acc[...] += _dot_tn(a_ref[...], g_ref[...])

        @pl.when(k == nk - 1)
        def _():
            o_ref[...] = acc[...]

    return pl.pallas_call(
        body, name=name, grid=(M // tm, nsh * tps, nk),
        in_specs=[pl.BlockSpec((tk, tm), lambda m, n, k: (k, m)),
                  pl.BlockSpec((tk, tn), lambda m, n, k: (k, n))],
        out_specs=pl.BlockSpec((None, None, tm, tn), lambda m, n, k: (n // tps, m // mh, m % mh, n % tps)),
        out_shape=jax.ShapeDtypeStruct((nsh, 2, M // 2, Ns), F32),
        scratch_shapes=[pltpu.VMEM((tm, tn), F32)],
        compiler_params=_params(("parallel", "parallel", "arbitrary")),
    )(a, g)


def _rowwise(fn, nblk, tm, rins, vins, routs, accs, name):
    nr, nv, no, na = len(rins), len(vins), len(routs), len(accs)

    def body(*refs):
        i = pl.program_id(0)
        outs, accv = fn(i, [r[...] for r in refs[:nr]], [r[...] for r in refs[nr:nr + nv]])
        for r, v in zip(refs[nr + nv:nr + nv + no], outs):
            r[...] = v.astype(r.dtype)
        arefs = refs[nr + nv + no:]
        if na:
            @pl.when(i == 0)
            def _():
                for a in arefs:
                    a[...] = jnp.zeros_like(a)

            for a, v in zip(arefs, accv):
                a[...] += v

    def row_spec(w, cb, rm):
        if rm is None:
            return pl.BlockSpec((tm, w), lambda i: (i, cb))
        return pl.BlockSpec((tm, w), lambda i: (rm(i), cb))

    in_specs = [row_spec(w, cb, rm) for (_, w, cb, rm) in rins]
    in_specs += [pl.BlockSpec(v.shape, lambda i: (0, 0)) for v in vins]
    out_specs = [pl.BlockSpec((tm, w), lambda i: (i, 0)) for (w, _) in routs]
    out_specs += [pl.BlockSpec((1, w), lambda i: (0, 0)) for w in accs]
    out_shape = [jax.ShapeDtypeStruct((nblk * tm, w), dt) for (w, dt) in routs]
    out_shape += [jax.ShapeDtypeStruct((1, w), F32) for w in accs]
    res = pl.pallas_call(
        body, name=name, grid=(nblk,), in_specs=in_specs, out_specs=out_specs, out_shape=out_shape,
        compiler_params=_params(("arbitrary",)),
    )(*[r[0] for r in rins], *vins)
    return list(res)


def _colsum(v):
    return jnp.sum(v, axis=0, keepdims=True)


def _rms(v):
    return lax.rsqrt(jnp.mean(v * v, axis=-1, keepdims=True) + EPS)


def _all_gather8(xs):
    m_per, n = xs.shape

    def body(x_ref, out_ref, send_sems, recv_sems, local_sem):
        x, y, c = lax.axis_index("x"), lax.axis_index("y"), lax.axis_index("c")
        me, sibling = (x, y, c), (x, y, 1 - c)
        chips = [(1 - x, y), (x, 1 - y), (1 - x, 1 - y)]

        def rows(px, py, pc):
            return out_ref.at[pl.ds((4 * px + 2 * py + pc) * m_per, m_per), :]

        def copy(k, block, to, src=None):
            return pltpu.make_async_remote_copy(
                src_ref=rows(*block) if src is None else src, dst_ref=rows(*block),
                send_sem=send_sems.at[k], recv_sem=recv_sems.at[k], device_id=to, device_id_type=MESH_ID)

        mine = pltpu.make_async_copy(x_ref, rows(*me), local_sem)
        mine.start()
        first = [copy(0, me, sibling, src=x_ref)]
        first += [copy(1 + j, me, (*chip, c), src=x_ref) for j, chip in enumerate(chips)]
        for cp in first:
            cp.start()
        passed = [copy(4 + j, (*chip, c), sibling) for j, chip in enumerate(chips)]
        for j, chip in enumerate(chips):
            copy(1 + j, (*chip, c), me).wait_recv()
            passed[j].start()
        copy(0, sibling, me).wait_recv()
        for j, chip in enumerate(chips):
            copy(4 + j, (*chip, 1 - c), me).wait_recv()
        for cp in first + passed:
            cp.wait_send()
        mine.wait()

    return pl.pallas_call(
        body, name="all_gather8_%dx%d" % (m_per, n),
        out_shape=jax.ShapeDtypeStruct((8 * m_per, n), xs.dtype),
        in_specs=[pl.BlockSpec(memory_space=pltpu.VMEM)],
        out_specs=pl.BlockSpec(memory_space=pltpu.VMEM),
        scratch_shapes=[pltpu.SemaphoreType.DMA((7,)), pltpu.SemaphoreType.DMA((7,)), pltpu.SemaphoreType.DMA],
    )(xs)


def _mesh_pos():
    x, y, c = lax.axis_index("x"), lax.axis_index("y"), lax.axis_index("c")
    chips = [(1 - x, y), (x, 1 - y), (1 - x, 1 - y)]
    return x, y, c, chips


def _cast_place(sidx, w, name):
    r, cw = w.shape
    rh = r // 2
    tm = _tile(rh, max(16, (1 << 20) // (4 * cw)), 16)
    nt = rh // tm

    def body(s_ref, w_ref, o_ref):
        o_ref[...] = w_ref[...].astype(o_ref.dtype)

    return pl.pallas_call(
        body, name=name,
        grid_spec=pltpu.PrefetchScalarGridSpec(
            num_scalar_prefetch=1, grid=(2, nt),
            in_specs=[pl.BlockSpec((tm, cw), lambda h, i, s_ref: (h * nt + i, 0))],
            out_specs=pl.BlockSpec((None, None, tm, cw), lambda h, i, s_ref: (s_ref[0], h, i, 0))),
        out_shape=jax.ShapeDtypeStruct((4, 2, rh, cw), BF16),
        compiler_params=_params(("parallel", "parallel")),
    )(sidx, w)


def _gather_weights(bufs):
    n = len(bufs)

    def body(*refs):
        out = refs[n:2 * n]
        send_sems, recv_sems = refs[2 * n:]
        x, y, c, chips = _mesh_pos()
        s = 2 * x + y
        sibling = (x, y, 1 - c)

        def copy(i, k, shard, half, to):
            dst = out[i].at[shard, half]
            return pltpu.make_async_remote_copy(
                src_ref=dst, dst_ref=dst, send_sem=send_sems.at[6 * i + k], recv_sem=recv_sems.at[6 * i + k],
                device_id=to, device_id_type=MESH_ID)

        first = [copy(i, j, s, c, (*chip, c)) for i in range(n) for j, chip in enumerate(chips)]
        for cp in first:
            cp.start()
        passed = []
        for j, chip in enumerate(chips):
            sj = 2 * chip[0] + chip[1]
            for i in range(n):
                copy(i, j, sj, c, (x, y, c)).wait_recv()
                cp = copy(i, 3 + j, sj, c, sibling)
                cp.start()
                passed.append(cp)
        for j, chip in enumerate(chips):
            sj = 2 * chip[0] + chip[1]
            for i in range(n):
                copy(i, 3 + j, sj, 1 - c, (x, y, c)).wait_recv()
        for cp in first + passed:
            cp.wait_send()

    any_spec = pl.BlockSpec(memory_space=pl.ANY)
    return pl.pallas_call(
        body, name="gather_weights",
        out_shape=[jax.ShapeDtypeStruct(b.shape, b.dtype) for b in bufs],
        in_specs=[any_spec] * n, out_specs=[any_spec] * n,
        input_output_aliases={i: i for i in range(n)},
        scratch_shapes=[pltpu.SemaphoreType.DMA((6 * n,)), pltpu.SemaphoreType.DMA((6 * n,))],
    )(*bufs)


def _swap_other_half(gs):
    n = len(gs)

    def body(*refs):
        g, land = refs[:n], refs[n:2 * n]
        send_sems, recv_sems = refs[2 * n:]
        x, y, c, _ = _mesh_pos()
        cps = [pltpu.make_async_remote_copy(
            src_ref=g[i].at[:, 1 - c], dst_ref=land[i], send_sem=send_sems.at[i], recv_sem=recv_sems.at[i],
            device_id=(x, y, 1 - c), device_id_type=MESH_ID) for i in range(n)]
        for cp in cps:
            cp.start()
        for cp in cps:
            cp.wait()

    any_spec = pl.BlockSpec(memory_space=pl.ANY)
    return pl.pallas_call(
        body, name="rs_swap_other_half",
        out_shape=[jax.ShapeDtypeStruct((4,) + g.shape[2:], g.dtype) for g in gs],
        in_specs=[any_spec] * n, out_specs=[any_spec] * n,
        scratch_shapes=[pltpu.SemaphoreType.DMA((n,)), pltpu.SemaphoreType.DMA((n,))],
    )(*gs)


def _scatter_pieces(ps):
    n = len(ps)

    def body(*refs):
        p, land = refs[:n], refs[n:2 * n]
        send_sems, recv_sems = refs[2 * n:]
        x, y, c, chips = _mesh_pos()

        def copy(i, j, chip):
            return pltpu.make_async_remote_copy(
                src_ref=p[i].at[2 * chip[0] + chip[1]], dst_ref=land[i].at[j],
                send_sem=send_sems.at[3 * i + j], recv_sem=recv_sems.at[3 * i + j],
                device_id=(*chip, c), device_id_type=MESH_ID)

        sends = [copy(i, j, chip) for i in range(n) for j, chip in enumerate(chips)]
        for cp in sends:
            cp.start()
        for cp in sends:
            cp.wait()

    any_spec = pl.BlockSpec(memory_space=pl.ANY)
    return pl.pallas_call(
        body, name="rs_scatter_pieces",
        out_shape=[jax.ShapeDtypeStruct((3,) + p.shape[1:], p.dtype) for p in ps],
        in_specs=[any_spec] * n, out_specs=[any_spec] * n,
        scratch_shapes=[pltpu.SemaphoreType.DMA((3 * n,)), pltpu.SemaphoreType.DMA((3 * n,))],
    )(*ps)


def _swap_result_half(bufs):
    n = len(bufs)

    def body(*refs):
        out = refs[n:2 * n]
        send_sems, recv_sems = refs[2 * n:]
        x, y, c, _ = _mesh_pos()
        cps = [pltpu.make_async_remote_copy(
            src_ref=out[i].at[c], dst_ref=out[i].at[c], send_sem=send_sems.at[i], recv_sem=recv_sems.at[i],
            device_id=(x, y, 1 - c), device_id_type=MESH_ID) for i in range(n)]
        for cp in cps:
            cp.start()
        for i in range(n):
            pltpu.make_async_remote_copy(
                src_ref=out[i].at[1 - c], dst_ref=out[i].at[1 - c], send_sem=send_sems.at[i], recv_sem=recv_sems.at[i],
                device_id=(x, y, c), device_id_type=MESH_ID).wait_recv()
        for cp in cps:
            cp.wait_send()

    any_spec = pl.BlockSpec(memory_space=pl.ANY)
    return pl.pallas_call(
        body, name="rs_swap_result_half",
        out_shape=[jax.ShapeDtypeStruct(b.shape, b.dtype) for b in bufs],
        in_specs=[any_spec] * n, out_specs=[any_spec] * n,
        input_output_aliases={i: i for i in range(n)},
        scratch_shapes=[pltpu.SemaphoreType.DMA((n,)), pltpu.SemaphoreType.DMA((n,))],
    )(*bufs)


def _add_own_half(cidx, g, land, name):
    _, _, rh, cw = g.shape
    tm = _tile(rh, max(16, (1 << 20) // (4 * cw)), 16)

    def body(c_ref, g_ref, l_ref, o_ref):
        o_ref[...] = (g_ref[...] + l_ref[...]).astype(o_ref.dtype)

    return pl.pallas_call(
        body, name=name,
        grid_spec=pltpu.PrefetchScalarGridSpec(
            num_scalar_prefetch=1, grid=(4, rh // tm),
            in_specs=[pl.BlockSpec((None, None, tm, cw), lambda s, i, c_ref: (s, c_ref[0], i, 0)),
                      pl.BlockSpec((None, tm, cw), lambda s, i, c_ref: (s, i, 0))],
            out_specs=pl.BlockSpec((None, tm, cw), lambda s, i, c_ref: (s, i, 0))),
        out_shape=jax.ShapeDtypeStruct((4, rh, cw), BF16),
        compiler_params=_params(("parallel", "parallel")),
    )(cidx, g, land)


def _sum_pieces(scidx, part, land, name):
    _, rh, cw = land.shape
    tm = _tile(rh, max(16, (1 << 20) // (4 * cw)), 16)

    def body(sc_ref, p_ref, l_ref, o_ref):
        v = l_ref[...].astype(F32)
        o_ref[...] = (p_ref[...].astype(F32) + v[0]) + (v[1] + v[2])

    return pl.pallas_call(
        body, name=name,
        grid_spec=pltpu.PrefetchScalarGridSpec(
            num_scalar_prefetch=1, grid=(rh // tm,),
            in_specs=[pl.BlockSpec((None, tm, cw), lambda i, sc_ref: (sc_ref[0], i, 0)),
                      pl.BlockSpec((3, tm, cw), lambda i, sc_ref: (0, i, 0))],
            out_specs=pl.BlockSpec((None, tm, cw), lambda i, sc_ref: (sc_ref[1], i, 0))),
        out_shape=jax.ShapeDtypeStruct((2, rh, cw), F32),
        compiler_params=_params(("parallel",)),
    )(scidx, part, land)


def _reduce_scatter(scidx, gs, names):
    lands = _swap_other_half(gs)
    parts = [_add_own_half(scidx[1:], g, l, "rs_add_" + nm) for g, l, nm in zip(gs, lands, names)]
    pieces = _scatter_pieces(parts)
    halves = [_sum_pieces(scidx, p, l, "rs_sum_" + nm) for p, l, nm in zip(parts, pieces, names)]
    fulls = _swap_result_half(halves)
    return [f.reshape(2 * f.shape[1], f.shape[2]) for f in fulls]


def _sum8(g, name):
    def body(g_ref, o_ref):
        acc = g_ref[0]
        for k in range(1, 8):
            acc = acc + g_ref[k]
        o_ref[...] = acc

    return pl.pallas_call(body, name=name, out_shape=jax.ShapeDtypeStruct(g.shape[1:], F32))(g)


def _silu(v):
    return v * _sigmoid(v)


def _dsilu(v):
    s = _sigmoid(v)
    return s * (1.0 + v * (1.0 - s))


def _ada_fwd(cin, w, b):
    d, ns = w.shape
    tn = _tile(ns, 512)

    def body(c_ref, w_ref, b_ref, o_ref):
        o_ref[...] = _dot(_silu(c_ref[...]), w_ref[...], HI) + b_ref[...]

    return pl.pallas_call(
        body, name="ada_fwd", grid=(ns // tn,),
        in_specs=[pl.BlockSpec(cin.shape, lambda n: (0, 0)), pl.BlockSpec((d, tn), lambda n: (0, n)),
                  pl.BlockSpec((1, tn), lambda n: (0, n))],
        out_specs=pl.BlockSpec((cin.shape[0], tn), lambda n: (0, n)),
        out_shape=jax.ShapeDtypeStruct((cin.shape[0], ns), F32),
        compiler_params=_params(("parallel",)),
    )(cin, w, b)


def _ada_bwd(cin, w, dm):
    d, ns = w.shape
    tn = _tile(ns, 512)

    def body(c_ref, w_ref, d_ref, dw_ref, dc_ref):
        n = pl.program_id(0)

        @pl.when(n == 0)
        def _():
            dc_ref[...] = jnp.zeros_like(dc_ref)

        dw_ref[...] = _dot_tn(_silu(c_ref[...]), d_ref[...], HI)
        dc_ref[...] += _dot_nt(d_ref[...], w_ref[...], HI)

    return pl.pallas_call(
        body, name="ada_bwd", grid=(ns // tn,),
        in_specs=[pl.BlockSpec(cin.shape, lambda n: (0, 0)), pl.BlockSpec((d, tn), lambda n: (0, n)),
                  pl.BlockSpec((cin.shape[0], tn), lambda n: (0, n))],
        out_specs=[pl.BlockSpec((d, tn), lambda n: (0, n)), pl.BlockSpec(cin.shape, lambda n: (0, 0))],
        out_shape=[jax.ShapeDtypeStruct((d, ns), F32), jax.ShapeDtypeStruct(cin.shape, F32)],
        compiler_params=_params(("arbitrary",)),
    )(cin, w, dm)


def _bias_tables(rpb, rows):
    kh = min(WIN_H, rows)
    col = jnp.arange(GRID_W)
    col_start = jnp.clip(col - WIN_W // 2, 0, GRID_W - WIN_W)
    in_win = (col[None, :] >= col_start[:, None]) & (col[None, :] < col_start[:, None] + WIN_W)
    dc_idx = jnp.clip(col[None, :] - col[:, None], 1 - WIN_W, WIN_W - 1) + WIN_W - 1
    dr_idx = jnp.arange(kh)[None, :] - jnp.arange(kh)[:, None] + WIN_H - 1
    t = rpb[:, dr_idx][:, :, :, dc_idx]
    t = jnp.where(in_win[None, None, None], t, MASK_VALUE)
    return t.transpose(0, 1, 3, 2, 4).reshape(rpb.shape[0], kh, GRID_W, kh * GRID_W).astype(F32)


def _mm_f32(a, b, name):
    M, K = a.shape
    N = b.shape[1]
    tm = _tile(M, 256, 8)

    def body(a_ref, b_ref, o_ref):
        o_ref[...] = _dot(a_ref[...], b_ref[...], HI)

    return pl.pallas_call(
        body, name=name, grid=(M // tm,),
        in_specs=[pl.BlockSpec((tm, K), lambda i: (i, 0)), pl.BlockSpec((K, N), lambda i: (0, 0))],
        out_specs=pl.BlockSpec((tm, N), lambda i: (i, 0)),
        out_shape=jax.ShapeDtypeStruct((M, N), F32),
        compiler_params=_params(("parallel",)),
    )(a, b)


def _bias_tables_transpose(dbias, rows):
    kh = min(WIN_H, rows)
    nh = dbias.shape[0]
    col = np.arange(GRID_W)
    col_start = np.clip(col - WIN_W // 2, 0, GRID_W - WIN_W)
    in_win = (col[None, :] >= col_start[:, None]) & (col[None, :] < col_start[:, None] + WIN_W)
    dc_idx = np.clip(col[None, :] - col[:, None], 1 - WIN_W, WIN_W - 1) + WIN_W - 1
    ndc = 2 * WIN_W - 1
    onehot = np.zeros((GRID_W * GRID_W, LANES), np.float32)
    qq, kk = np.nonzero(in_win)
    onehot[qq * GRID_W + kk, dc_idx[qq, kk]] = 1.0
    x = dbias.reshape(nh, kh, GRID_W, kh, GRID_W).transpose(0, 1, 3, 2, 4).reshape(nh * kh * kh, GRID_W * GRID_W)
    z = _mm_f32(x, jnp.asarray(onehot), "rpb_fold")[:, :ndc].reshape(nh, kh, kh, ndc)
    fold = np.zeros((2 * WIN_H - 1, kh, kh), np.float32)
    for sh in range(kh):
        for j in range(kh):
            fold[j - sh + WIN_H - 1, sh, j] = 1.0
    return jnp.einsum("dsj,hsjc->hdc", jnp.asarray(fold), z, precision=HI)


def _na_geometry(S):
    rows = S // GRID_W
    kh = min(WIN_H, rows)

    def row_start(r):
        return jnp.clip(r - kh // 2, 0, rows - kh)

    return rows, kh, row_start


def _na_by_head(ref, rr, lane):
    t = ref[rr * GRID_W:(rr + 1) * GRID_W, :]
    zero = jnp.zeros_like(t)
    return jnp.concatenate([jnp.where(lane < NA_HEAD_DIM, t, zero), jnp.where(lane >= NA_HEAD_DIM, t, zero)], axis=0)


def _na_pick_head(t2, lane):
    return jnp.where(lane < NA_HEAD_DIM, t2[:GRID_W], t2[GRID_W:])


def _na_scores(q_ref, k_ref, b_ref, i, nrs, nb, S, L, row_start, lane):
    qh = jnp.concatenate([_na_by_head(q_ref, rr, lane) for rr in range(nrs)], axis=0)
    sc = _dot_nt(qh, k_ref[pl.ds(S, L), :])
    starts, shifts, sb = [], [], []
    for rr in range(nrs):
        r = i * nrs + rr
        rs = row_start(r)
        starts.append(pl.multiple_of(rs * GRID_W, GRID_W))
        shifts.append(r - rs)
        bias = jnp.concatenate([b_ref[0, r - rs], b_ref[1, r - rs]], axis=0)
        sb.append(_dot_nt(qh[rr * 2 * GRID_W:(rr + 1) * 2 * GRID_W], k_ref[pl.ds(starts[-1], nb), :]) + bias)
    return qh, jnp.concatenate(sb, axis=0), sc, starts, shifts


def _na_fwd(qs, kb, vb, bias, S, L):
    T, naw = qs.shape
    rows, kh, row_start = _na_geometry(S)
    nb = kh * GRID_W
    npair = naw // LANES

    nrs = min(NA_ROWS_PER_STEP, rows)
    assert rows % nrs == 0

    def body(q_ref, k_ref, v_ref, b_ref, o_ref, lse_ref):
        i = pl.program_id(1)
        lane = lax.broadcasted_iota(jnp.int32, (GRID_W, LANES), 1)
        _, sb, sc, starts, _ = _na_scores(q_ref, k_ref, b_ref, i, nrs, nb, S, L, row_start, lane)
        m = jnp.maximum(jnp.max(sb, axis=-1, keepdims=True), jnp.max(sc, axis=-1, keepdims=True))
        pb, pc = jnp.exp(sb - m), jnp.exp(sc - m)
        l = jnp.sum(pb, axis=-1, keepdims=True) + jnp.sum(pc, axis=-1, keepdims=True)
        inv = 1.0 / l
        pb16, pc16 = (pb * inv).astype(BF16), (pc * inv).astype(BF16)
        oc = _dot(pc16, v_ref[pl.ds(S, L), :])
        lse = jnp.broadcast_to(m + jnp.log(l), oc.shape)
        for rr in range(nrs):
            two = slice(rr * 2 * GRID_W, (rr + 1) * 2 * GRID_W)
            rsl = slice(rr * GRID_W, (rr + 1) * GRID_W)
            o2 = oc[two] + _dot(pb16[two], v_ref[pl.ds(starts[rr], nb), :])
            o_ref[rsl, :] = _na_pick_head(o2, lane).astype(o_ref.dtype)
            lse_ref[rsl, :] = _na_pick_head(lse[two], lane)

    blk = pl.BlockSpec((nrs * GRID_W, LANES), lambda p, i: (i, p))
    col = pl.BlockSpec((T, LANES), lambda p, i: (0, p))
    return pl.pallas_call(
        body, name="na_fwd", grid=(npair, rows // nrs),
        in_specs=[blk, col, col, pl.BlockSpec((2, kh, GRID_W, nb), lambda p, i: (p, 0, 0, 0))],
        out_specs=[blk, blk],
        out_shape=[jax.ShapeDtypeStruct((S, naw), BF16), jax.ShapeDtypeStruct((S, naw), F32)],
        compiler_params=_params(("parallel", "arbitrary")),
    )(qs, kb, vb, bias)


def _na_bwd(qs, kb, vb, bias, do, o, lse, S, L):
    T, naw = qs.shape
    rows, kh, row_start = _na_geometry(S)
    nb = kh * GRID_W
    npair = naw // LANES

    nrs = min(NA_ROWS_PER_STEP, rows)
    assert rows % nrs == 0

    def body(q_ref, k_ref, v_ref, b_ref, do_ref, o_ref, lse_ref, dq_ref, dk_ref, dv_ref, db_ref):
        i = pl.program_id(1)

        @pl.when(i == 0)
        def _():
            dk_ref[...] = jnp.zeros_like(dk_ref)
            dv_ref[...] = jnp.zeros_like(dv_ref)
            db_ref[...] = jnp.zeros_like(db_ref)

        lane = lax.broadcasted_iota(jnp.int32, (GRID_W, LANES), 1)
        qh, sb, sc, starts, shifts = _na_scores(q_ref, k_ref, b_ref, i, nrs, nb, S, L, row_start, lane)
        doh = jnp.concatenate([_na_by_head(do_ref, rr, lane) for rr in range(nrs)], axis=0)
        o2 = jnp.concatenate([o_ref[rr * GRID_W:(rr + 1) * GRID_W, :] for rr in range(nrs) for _ in range(2)], axis=0)
        lse = jnp.concatenate([lse_ref[rr * GRID_W:(rr + 1) * GRID_W, :][:, hh * NA_HEAD_DIM:hh * NA_HEAD_DIM + 1]
                               for rr in range(nrs) for hh in range(2)], axis=0)
        pb, pc = jnp.exp(sb - lse), jnp.exp(sc - lse)
        delta = jnp.sum(doh.astype(F32) * o2.astype(F32), axis=-1, keepdims=True)
        dpb = jnp.concatenate([_dot_nt(doh[rr * 2 * GRID_W:(rr + 1) * 2 * GRID_W], v_ref[pl.ds(starts[rr], nb), :])
                               for rr in range(nrs)], axis=0)
        dsb = pb * (dpb - delta)
        dsc = pc * (_dot_nt(doh, v_ref[pl.ds(S, L), :]) - delta)
        dsb16, dsc16, pb16, pc16 = dsb.astype(BF16), dsc.astype(BF16), pb.astype(BF16), pc.astype(BF16)
        dqc = _dot(dsc16, k_ref[pl.ds(S, L), :])
        dk_ref[pl.ds(S, L), :] += _dot_tn(dsc16, qh)
        dv_ref[pl.ds(S, L), :] += _dot_tn(pc16, doh)
        for rr in range(nrs):
            two = slice(rr * 2 * GRID_W, (rr + 1) * 2 * GRID_W)
            band = pl.ds(starts[rr], nb)
            dq2 = dqc[two] + _dot(dsb16[two], k_ref[band, :])
            dq_ref[rr * GRID_W:(rr + 1) * GRID_W, :] = _na_pick_head(dq2, lane)
            dk_ref[band, :] += _dot_tn(dsb16[two], qh[two])
            dv_ref[band, :] += _dot_tn(pb16[two], doh[two])
            for hh in range(2):
                db_ref[hh, shifts[rr]] += dsb[(2 * rr + hh) * GRID_W:(2 * rr + hh + 1) * GRID_W]

    blk = pl.BlockSpec((nrs * GRID_W, LANES), lambda p, r: (r, p))
    col = pl.BlockSpec((T, LANES), lambda p, r: (0, p))
    return pl.pallas_call(
        body, name="na_bwd", grid=(npair, rows // nrs),
        in_specs=[blk, col, col, pl.BlockSpec((2, kh, GRID_W, nb), lambda p, r: (p, 0, 0, 0)), blk, blk, blk],
        out_specs=[blk, col, col, pl.BlockSpec((2, kh, GRID_W, nb), lambda p, r: (p, 0, 0, 0))],
        out_shape=[jax.ShapeDtypeStruct((S, naw), F32), jax.ShapeDtypeStruct((T, naw), F32),
                   jax.ShapeDtypeStruct((T, naw), F32), jax.ShapeDtypeStruct(bias.shape, F32)],
        compiler_params=_params(("parallel", "arbitrary")),
    )(qs, kb, vb, bias, do, o, lse)


def _hg_cols(naw, hgf, rev):
    qcol = (3 * naw) // hgf
    fcol = (3 * naw + hgf * (2 if rev else 1)) // hgf
    icol = (3 * naw + 3 * hgf) // hgf
    return qcol, fcol, icol


def _hg_chunk_order(S, L, rev):
    ncl, ncc = S // HG_CHUNK, L // HG_CHUNK
    nc = ncl + ncc

    def chunk_of(i):
        if rev:
            return nc - 1 - i
        return jnp.where(i < ncc, ncl + i, i - ncc)

    return nc, ncl, chunk_of


def _hg_gates(q, z, lb, rev):
    row = lax.broadcasted_iota(jnp.int32, (HG_CHUNK, HG_CHUNK), 0)
    colm = lax.broadcasted_iota(jnp.int32, (HG_CHUNK, HG_CHUNK), 1)
    tri = (colm >= row) if rev else (row >= colm)
    trif = tri.astype(F32)
    sig = _sigmoid(z)
    f = lb + (1.0 - lb) * sig
    lf = jnp.log(f)
    k = 1.0 - f
    cum = _dot(trif, lf, HI)
    mid = cum[HG_CHUNK // 2:HG_CHUNK // 2 + 1, :]
    last = cum[0:1, :] if rev else cum[HG_CHUNK - 1:HG_CHUNK, :]
    eq = jnp.exp(jnp.clip(cum - mid, -EXP_CLAMP, EXP_CLAMP))
    ek = jnp.exp(jnp.clip(mid - cum, -EXP_CLAMP, EXP_CLAMP))
    return tri, trif, sig, f, k, cum, last, eq, ek


def _hg_fwd(u, lbr, S, L, naw, hgf, rev):
    T = S + L
    nh = hgf // HG_DIM
    nc, ncl, chunk_of = _hg_chunk_order(S, L, rev)
    qcol, fcol, icol = _hg_cols(naw, hgf, rev)

    def body(q_ref, z_ref, v_ref, lb_ref, o_ref, st_ref, state):
        i = pl.program_id(0)

        @pl.when(i == 0)
        def _():
            state[...] = jnp.zeros_like(state)

        q, z, v = q_ref[...], z_ref[...], v_ref[...]
        tri, _, _, _, k, cum, last, eq, ek = _hg_gates(q, z, lb_ref[...], rev)
        qe, ke = (q * eq).astype(BF16), (k * ek).astype(BF16)
        qd, kd = (q * jnp.exp(cum)).astype(BF16), (k * jnp.exp(last - cum)).astype(BF16)
        v16, el = v.astype(BF16), jnp.exp(last)
        for h in range(nh):
            sl = slice(h * HG_DIM, (h + 1) * HG_DIM)
            a = jnp.where(tri, _dot_nt(qe[:, sl], ke[:, sl]), 0.0)
            s0 = state[h]
            st_ref[h] = s0
            o_ref[:, sl] = _dot(a.astype(BF16), v16[:, sl]) + _dot_nt(qd[:, sl], s0.astype(BF16))
            state[h] = s0 * el[:, sl] + _dot_tn(v16[:, sl], kd[:, sl])

    def blk(cb):
        return pl.BlockSpec((HG_CHUNK, hgf), lambda i: (chunk_of(i), cb))

    return pl.pallas_call(
        body, name="hg_fwd_rev" if rev else "hg_fwd", grid=(nc,),
        in_specs=[blk(qcol), blk(fcol), blk(icol), pl.BlockSpec((1, hgf), lambda i: (0, 0))],
        out_specs=[pl.BlockSpec((HG_CHUNK, hgf), lambda i: (chunk_of(i), 0)),
                   pl.BlockSpec((None, nh, HG_DIM, HG_DIM), lambda i: (chunk_of(i), 0, 0, 0))],
        out_shape=[jax.ShapeDtypeStruct((T, hgf), F32), jax.ShapeDtypeStruct((nc, nh, HG_DIM, HG_DIM), F32)],
        scratch_shapes=[pltpu.VMEM((nh, HG_DIM, HG_DIM), F32)],
        compiler_params=_params(("arbitrary",)),
    )(u, u, u, lbr)


def _hg_bwd(u, lbr, st, do, S, L, naw, hgf, rev):
    T = S + L
    nh = hgf // HG_DIM
    nc, ncl, chunk_fwd = _hg_chunk_order(S, L, rev)
    qcol, fcol, icol = _hg_cols(naw, hgf, rev)

    def chunk_of(j):
        return chunk_fwd(nc - 1 - j)

    def body(q_ref, z_ref, v_ref, lb_ref, st_ref, do_ref, dq_ref, dz_ref, dv_ref, dlb_ref, dstate,
             dqe_s, dke_s, dqd_s, dkd_s, dl_s):
        j = pl.program_id(0)

        @pl.when(j == 0)
        def _():
            dstate[...] = jnp.zeros_like(dstate)
            dlb_ref[...] = jnp.zeros_like(dlb_ref)

        q, z, v = q_ref[...], z_ref[...], v_ref[...]
        lb = lb_ref[...]
        tri, trif, sig, f, k, cum, last, eq, ek = _hg_gates(q, z, lb, rev)
        ec, el, ekd = jnp.exp(cum), jnp.exp(last), jnp.exp(last - cum)
        qe, ke, qd, kd = q * eq, k * ek, q * ec, k * ekd
        qd16, kd16 = qd.astype(BF16), kd.astype(BF16)
        dout = jnp.where(chunk_of(j) < ncl, do_ref[...], 0.0)
        qe2, ke2, v2, dout2 = _split2(qe), _split2(ke), _split2(v), _split2(dout)
        for h in range(nh):
            sl = slice(h * HG_DIM, (h + 1) * HG_DIM)
            qeh, keh, vh, douth = [(t[0][:, sl], t[1][:, sl]) for t in (qe2, ke2, v2, dout2)]
            a = jnp.where(tri, _dot_nt(qeh[0], keh[0]), 0.0).astype(BF16)
            s0 = st_ref[h]
            ds1 = dstate[h]
            s02, ds12 = _split2(s0), _split2(ds1)
            dv_ref[:, sl] = _dot_tn(a, douth[0]) + _dot_nt(kd16[:, sl], ds12[0])
            da2 = _split2(jnp.where(tri, _dot_x3(_dot_nt, douth, vh), 0.0))
            dqe_s[:, sl] = _dot_x3(_dot, da2, keh)
            dke_s[:, sl] = _dot_x3(_dot_tn, da2, qeh)
            dqd_s[:, sl] = _dot_x3(_dot, douth, s02)
            dkd_s[:, sl] = _dot_x3(_dot, vh, ds12)
            dl_s[:, sl] = _colsum(ds1 * s0)
            dstate[h] = _dot_tn(douth[0], qd16[:, sl]) + ds1 * el[:, sl]
        dqe, dke, dqd, dkd = dqe_s[...], dke_s[...], dqd_s[...], dkd_s[...]
        dq_ref[...] = dqe * eq + dqd * ec
        dk = dke * ek + dkd * ekd
        dcum = dqe * qe - dke * ke + dqd * qd - dkd * kd
        dlast = _colsum(dkd * kd) + el * dl_s[...]
        dlf = _dot_tn(trif, dcum, HI) + dlast
        df = dlf / f - dk
        dz_ref[...] = df * (1.0 - lb) * sig * (1.0 - sig)
        dlb_ref[...] += _colsum(df * (1.0 - sig))

    def blk(cb):
        return pl.BlockSpec((HG_CHUNK, hgf), lambda j: (chunk_of(j), cb))

    oblk = pl.BlockSpec((HG_CHUNK, hgf), lambda j: (chunk_of(j), 0))
    wide = pltpu.VMEM((HG_CHUNK, hgf), F32)
    return pl.pallas_call(
        body, name="hg_bwd_rev" if rev else "hg_bwd", grid=(nc,),
        in_specs=[blk(qcol), blk(fcol), blk(icol), pl.BlockSpec((1, hgf), lambda j: (0, 0)),
                  pl.BlockSpec((None, nh, HG_DIM, HG_DIM), lambda j: (chunk_of(j), 0, 0, 0)),
                  pl.BlockSpec((HG_CHUNK, hgf), lambda j: (jnp.minimum(chunk_of(j), ncl - 1), 0))],
        out_specs=[oblk, oblk, oblk, pl.BlockSpec((1, hgf), lambda j: (0, 0))],
        out_shape=[jax.ShapeDtypeStruct((T, hgf), F32)] * 3 + [jax.ShapeDtypeStruct((1, hgf), F32)],
        scratch_shapes=[pltpu.VMEM((nh, HG_DIM, HG_DIM), F32), wide, wide, wide, wide,
                        pltpu.VMEM((1, hgf), F32)],
        compiler_params=_params(("arbitrary",)),
    )(u, u, u, lbr, st, do)


def _adamw(w, g, m, v, name):
    shape = w.shape
    if w.ndim != 2 or shape[0] % 8 or shape[1] % LANES:
        w, g, m, v = [a.reshape(1, -1) for a in (w, g, m, v)]
    r, cw = w.shape
    tm = _tile(r, max(8, (1 << 19) // cw), 8) if r % 8 == 0 else r
    c1 = 1.0 / (1.0 - ADAM_B1 ** ADAM_STEP)
    c2 = 1.0 / (1.0 - ADAM_B2 ** ADAM_STEP)

    def body(w_ref, g_ref, m_ref, v_ref, d_ref, nm_ref, nv_ref):
        gg = g_ref[...]
        nm = ADAM_B1 * m_ref[...] + (1.0 - ADAM_B1) * gg
        nv = ADAM_B2 * v_ref[...] + (1.0 - ADAM_B2) * (gg * gg)
        d_ref[...] = -ADAM_LR * ((nm * c1) / (jnp.sqrt(nv * c2) + ADAM_EPS) + ADAM_WD * w_ref[...])
        nm_ref[...] = nm
        nv_ref[...] = nv

    spec = pl.BlockSpec((tm, cw), lambda i: (i, 0))
    outs = pl.pallas_call(
        body, name=name, grid=(r // tm,), in_specs=[spec] * 4, out_specs=[spec] * 3,
        out_shape=[jax.ShapeDtypeStruct((r, cw), F32)] * 3,
        compiler_params=_params(("parallel",)),
    )(w, g, m, v)
    return [o.reshape(shape) for o in outs]


def kernel(x, c, ctx, c_ctx, w_ada, b_ada, norm1_g, w_in, na_rpb, hg_lb_logits, hg_norm_g, w_pa, w_pb, w_out, norm2_g, w_ffn_in, w_ffn_out, final_g, loss_target, m_c_ctx, m_w_ada, m_b_ada, m_norm1_g, m_w_in, m_na_rpb, m_hg_lb_logits, m_hg_norm_g, m_w_pa, m_w_pb, m_w_out, m_norm2_g, m_w_ffn_in, m_w_ffn_out, m_final_g, v_c_ctx, v_w_ada, v_b_ada, v_norm1_g, v_w_in, v_na_rpb, v_hg_lb_logits, v_hg_norm_g, v_w_pa, v_w_pb, v_w_out, v_norm2_g, v_w_ffn_in, v_w_ffn_out, v_final_g):
    xi, yi, ci = lax.axis_index("x"), lax.axis_index("y"), lax.axis_index("c")
    sidx = 2 * xi + yi
    eidx = 4 * xi + 2 * yi + ci
    scidx = jnp.stack([sidx, ci]).astype(jnp.int32)

    S, D = x.shape[1], x.shape[2]
    L = ctx.shape[1]
    T = S + L
    naw = NA_HEADS * NA_HEAD_DIM
    hgf = HG_HEADS * HG_DIM
    inw = 3 * naw + 5 * hgf + 2 * D
    fh = w_ffn_out.shape[1] * 4
    ads = w_ada.shape[2]
    fs = hg_lb_logits.shape[2]
    rows = S // GRID_W
    tr = _tile(L, 256)
    nlat, nall = S // tr, T // tr
    assert naw == hgf and D % naw == 0 and S % tr == 0 and 2 * hgf <= D

    pack0 = jnp.concatenate([c, jnp.pad(hg_lb_logits.reshape(1, -1), ((0, 0), (0, D - 4 * fs))),
                             jnp.zeros((6, D), F32)], axis=0)
    g0 = _all_gather8(pack0).reshape(8, 8, D)
    cs = g0[:, 0]
    lbl = g0[::2, 1, :4 * fs].reshape(4, 2, 2, fs).transpose(1, 2, 0, 3).reshape(2, 2, 4 * fs)
    p_lb = jax.nn.softmax(lbl, axis=0)
    lb = p_lb[0]
    lbb = [lb[d].reshape(1, hgf) for d in range(2)]

    cin = jnp.concatenate([cs, c_ctx[None], jnp.zeros((7, D), F32)], axis=0)
    b_sh = lax.dynamic_slice(b_ada, (0, sidx * ads), (1, ads))
    modp = _ada_fwd(cin, w_ada[0], b_sh)
    modfull = _all_gather8(modp).reshape(8, 16, ads)[::2].transpose(1, 0, 2).reshape(16, 4 * ads)
    mod_e = jnp.pad(lax.dynamic_index_in_dim(modfull, eidx, 0, keepdims=False).reshape(N_MOD, D), ((0, 2), (0, 0)))
    mod_c = jnp.pad(modfull[8].reshape(N_MOD, D), ((0, 2), (0, 0)))

    names = ["w_in", "w_pa", "w_pb", "w_out", "w_ffn_in", "w_ffn_out"]
    placed = [_cast_place(scidx[:1], w[0], "cast_" + nm)
              for w, nm in zip((w_in, w_pa, w_pb, w_out, w_ffn_in, w_ffn_out), names)]
    gathered = _gather_weights(placed)
    win3, wpa3, wpb3, wout3, wi3, wfo3 = [g.reshape(4, 2 * g.shape[2], g.shape[3]) for g in gathered]
    wout1 = wout3.reshape(1, D, D)
    wfo1 = wfo3.reshape(1, fh, D)

    xx = jnp.concatenate([x[0], ctx[0]], axis=0)

    def f_ln1(i, rv, vv):
        xt, = rv
        g, me, mc = vv
        isc = i >= nlat
        sh = jnp.where(isc, mc[0:1], me[0:1])
        sc = jnp.where(isc, mc[1:2], me[1:2])
        return [xt * _rms(xt) * g * (1.0 + sc) + sh], []

    hb, = _rowwise(f_ln1, nall, tr, [(xx, D, 0, None)], [norm1_g, mod_e, mod_c], [(D, BF16)], [], "ln1")
    u = _mm_nn(hb, win3, F32, "mm_in")

    scale = NA_HEAD_DIM ** -0.5

    def f_qkv(i, rv, vv):
        q, k, v = rv
        return [q * scale, k, v], []

    qs, kb, vb = _rowwise(f_qkv, nall, tr, [(u, naw, 0, None), (u, naw, 1, None), (u, naw, 2, None)], [],
                          [(naw, BF16)] * 3, [], "qkv_cast")
    bias = _bias_tables(na_rpb[0], rows)
    o_na, lse = _na_fwd(qs, kb, vb, bias, S, L)

    o_f, st_f = _hg_fwd(u, lbb[0], S, L, naw, hgf, False)
    o_b, st_b = _hg_fwd(u, lbb[1], S, L, naw, hgf, True)

    hgn = jnp.tile(hg_norm_g, (1, HG_HEADS))
    hog_cb = (3 * naw + 4 * hgf) // hgf
    ga_cb = (3 * naw + 5 * hgf) // D
    gb_cb = ga_cb + 1

    def heads_rms(o):
        return jnp.concatenate([jnp.broadcast_to(_rms(o[:, h * HG_DIM:(h + 1) * HG_DIM]), (o.shape[0], HG_DIM))
                                for h in range(HG_HEADS)], axis=1)

    def f_readout(i, rv, vv):
        of, ob_, hog = rv
        g, = vv
        o = of + ob_
        return [o * heads_rms(o) * g * _silu(hog)], []

    ob, = _rowwise(f_readout, nlat, tr, [(o_f, hgf, 0, None), (o_b, hgf, 0, None), (u, hgf, hog_cb, None)],
                   [hgn], [(hgf, BF16)], [], "hg_readout")

    ya = _mm_nn(o_na, wpa3, F32, "mm_pa")
    yb = _mm_nn(ob, wpb3, F32, "mm_pb")

    def f_merge(i, rv, vv):
        ya_, yb_, ga, gb = rv
        return [_sigmoid(ga) * ya_ + _sigmoid(gb) * yb_], []

    yv, = _rowwise(f_merge, nlat, tr, [(ya, D, 0, None), (yb, D, 0, None), (u, D, ga_cb, None), (u, D, gb_cb, None)],
                   [], [(D, BF16)], [], "merge")
    z = _mm_nn(yv, wout1, F32, "mm_out")

    def f_res1(i, rv, vv):
        xt, zt = rv
        g, me = vv
        x1 = xt + me[2:3] * zt
        return [x1, x1 * _rms(x1) * g * (1.0 + me[4:5]) + me[3:4]], []

    x1, h2 = _rowwise(f_res1, nlat, tr, [(xx, D, 0, None), (z, D, 0, None)], [norm2_g, mod_e],
                      [(D, F32), (D, BF16)], [], "res1_ln2")
    au = _mm_nn(h2, wi3, F32, "mm_ffn_in")
    tf = _tile(L, 128)
    nlf = S // tf

    def f_swiglu(i, rv, vv):
        a, uu = rv
        return [_silu(a) * uu], []

    sw, = _rowwise(f_swiglu, nlf, tf, [(au, fh, 0, None), (au, fh, 1, None)], [], [(fh, BF16)], [], "swiglu")
    ff = _mm_nn(sw, wfo1, F32, "mm_ffn_out")

    fg = final_g.reshape(1, D)

    def f_final(i, rv, vv):
        x1t, ft, tg = rv
        g, me = vv
        x2 = x1t + me[5:6] * ft
        r3 = _rms(x2)
        xn = x2 * r3
        err = xn * g - tg
        dyy = err * (1.0 / D)
        dxn = dyy * g
        dx2 = r3 * (dxn - xn * jnp.mean(dxn * xn, axis=-1, keepdims=True))
        return [dx2, dx2 * me[5:6]], [_colsum(err * err), _colsum(dyy * xn), _colsum(dx2 * ft)]

    dx2, dfb, loss_cols, dfg, dg2 = _rowwise(
        f_final, nlat, tr, [(x1, D, 0, None), (ff, D, 0, None), (loss_target[0], D, 0, None)], [fg, mod_e],
        [(D, F32), (D, BF16)], [D, D, D], "final_loss")

    dsw = _mm_nt(dfb, wfo1, BF16, "mm_d_sw")

    def f_dswiglu(i, rv, vv):
        d, a, uu = rv
        return [jnp.concatenate([d * uu * _dsilu(a), d * _silu(a)], axis=1)], []

    dau, = _rowwise(f_dswiglu, nlf, tf, [(dsw, fh, 0, None), (au, fh, 0, None), (au, fh, 1, None)], [],
                    [(2 * fh, BF16)], [], "swiglu_bwd")
    g_wfo = _mm_tn(sw, dfb, 1, "mm_dw_ffn_out").reshape(4, 2, fh // 8, D)
    dh2 = _mm_nt(dau, wi3, F32, "mm_d_h2")
    g_wi = _mm_tn(h2, dau, 4, "mm_dw_ffn_in")

    def f_ln2_bwd(i, rv, vv):
        dh, x1t, dx2t, zt = rv
        g, me = vv
        r2 = _rms(x1t)
        xn = x1t * r2
        dxn = dh * g * (1.0 + me[4:5])
        dx1 = dx2t + r2 * (dxn - xn * jnp.mean(dxn * xn, axis=-1, keepdims=True))
        return ([dx1, dx1 * me[2:3]],
                [_colsum(dh), _colsum(dh * xn * g), _colsum(dh * xn * (1.0 + me[4:5])), _colsum(dx1 * zt)])

    dx1, dzb, dsh2, dsc2, dn2g, dg1 = _rowwise(
        f_ln2_bwd, nlat, tr, [(dh2, D, 0, None), (x1, D, 0, None), (dx2, D, 0, None), (z, D, 0, None)],
        [norm2_g, mod_e], [(D, F32), (D, BF16)], [D, D, D, D], "ln2_bwd")

    dy = _mm_nt(dzb, wout1, F32, "mm_d_y")
    g_wout = _mm_tn(yv, dzb, 1, "mm_dw_out").reshape(4, 2, D // 8, D)

    def f_dmerge(i, rv, vv):
        d, ya_, yb_, ga, gb = rv
        sa, sb_ = _sigmoid(ga), _sigmoid(gb)
        return [d * sa, d * sb_, d * ya_ * sa * (1.0 - sa), d * yb_ * sb_ * (1.0 - sb_)], []

    dya, dyb, dga, dgb = _rowwise(
        f_dmerge, nlat, tr, [(dy, D, 0, None), (ya, D, 0, None), (yb, D, 0, None), (u, D, ga_cb, None),
                             (u, D, gb_cb, None)], [], [(D, BF16)] * 4, [], "merge_bwd")
    d_ona = _mm_nt(dya, wpa3, BF16, "mm_d_ona")
    d_ob = _mm_nt(dyb, wpb3, F32, "mm_d_ob")
    g_wpa = _mm_tn(o_na, dya, 4, "mm_dw_pa")
    g_wpb = _mm_tn(ob, dyb, 4, "mm_dw_pb")

    def f_dreadout(i, rv, vv):
        d, of, ob_, hog = rv
        g, = vv
        o = of + ob_
        on = o * heads_rms(o)
        t = d * _silu(hog) * g
        mt = jnp.concatenate([jnp.broadcast_to(jnp.mean((t * on)[:, h * HG_DIM:(h + 1) * HG_DIM], axis=-1,
                                                        keepdims=True), (o.shape[0], HG_DIM))
                              for h in range(HG_HEADS)], axis=1)
        do_ = heads_rms(o) * (t - on * mt)
        return [do_, d * on * g * _dsilu(hog)], [_colsum(d * _silu(hog) * on)]

    do_hg, dhog, dhgn = _rowwise(
        f_dreadout, nlat, tr, [(d_ob, hgf, 0, None), (o_f, hgf, 0, None), (o_b, hgf, 0, None),
                               (u, hgf, hog_cb, None)], [hgn], [(hgf, F32), (hgf, BF16)], [hgf], "hg_readout_bwd")

    dq_f, dz_f, dv_f, dlb_f = _hg_bwd(u, lbb[0], st_f, do_hg, S, L, naw, hgf, False)
    dq_b, dz_b, dv_b, dlb_b = _hg_bwd(u, lbb[1], st_b, do_hg, S, L, naw, hgf, True)
    dq_na, dk_na, dv_na, dbias = _na_bwd(qs, kb, vb, bias, d_ona, o_na, lse, S, L)

    ta = _tile(L, 128)
    nla, naa = S // ta, T // ta
    lat = lambda i: jnp.minimum(i, nla - 1)

    def f_assemble(i, rv, vv):
        dqn, dk, dv, dqf, dqb, dzf, dzb_, dvf, dvb, dho, dga_, dgb_ = rv
        keep = (i < nla).astype(F32)
        return [jnp.concatenate([dqn * (scale * keep), dk, dv, dqf + dqb, dzf, dzb_, dvf + dvb,
                                 dho.astype(F32) * keep, dga_.astype(F32) * keep, dgb_.astype(F32) * keep],
                                axis=1)], []

    du, = _rowwise(
        f_assemble, naa, ta,
        [(dq_na, naw, 0, lat), (dk_na, naw, 0, None), (dv_na, naw, 0, None), (dq_f, hgf, 0, None),
         (dq_b, hgf, 0, None), (dz_f, hgf, 0, None), (dz_b, hgf, 0, None), (dv_f, hgf, 0, None),
         (dv_b, hgf, 0, None), (dhog, hgf, 0, lat), (dga, D, 0, lat), (dgb, D, 0, lat)],
        [], [(inw, BF16)], [], "assemble_du")

    dh = _mm_nt(du, win3, F32, "mm_d_h")
    g_win = _mm_tn(hb, du, 4, "mm_dw_in")

    def f_ln1_bwd(i, rv, vv):
        dht, xt, dx1t = rv
        g, me = vv
        r1 = _rms(xt)
        xn = xt * r1
        dxn = dht * g * (1.0 + me[1:2])
        dx = dx1t + r1 * (dxn - xn * jnp.mean(dxn * xn, axis=-1, keepdims=True))
        return [dx], [_colsum(dht), _colsum(dht * xn * g), _colsum(dht * xn * (1.0 + me[1:2]))]

    grad_x, dsh1, dsc1, dn1g_l = _rowwise(
        f_ln1_bwd, nlat, tr, [(dh, D, 0, None), (xx, D, 0, None), (dx1, D, 0, None)], [norm1_g, mod_e],
        [(D, F32)], [D, D, D], "ln1_bwd")

    def f_ln1_bwd_ctx(i, rv, vv):
        dht, xt = rv
        g, mc = vv
        xn = xt * _rms(xt)
        return [], [_colsum(dht), _colsum(dht * xn * g), _colsum(dht * xn * (1.0 + mc[1:2]))]

    ctx_rows = lambda i: i + nlat
    dsh1c, dsc1c, dn1g_c = _rowwise(
        f_ln1_bwd_ctx, nall - nlat, tr, [(dh, D, 0, ctx_rows), (xx, D, 0, ctx_rows)], [norm1_g, mod_c],
        [], [D, D, D], "ln1_bwd_ctx")

    drpb = _bias_tables_transpose(dbias, rows).reshape(1, -1)
    nrp = -(-drpb.shape[1] // D)
    drpb_rows = jnp.pad(drpb, ((0, 0), (0, nrp * D - drpb.shape[1]))).reshape(nrp, D)
    dlb = jnp.concatenate([dlb_f, dlb_b], axis=1)
    dhg = jnp.sum(dhgn.reshape(HG_HEADS, HG_DIM), axis=0, keepdims=True)

    def wide(v):
        return jnp.pad(v, ((0, 0), (0, D - v.shape[1])))

    pack_rows = [loss_cols, dfg, dn2g, dn1g_l + dn1g_c, dsh1, dsc1, dg1, dsh2, dsc2, dg2, dsh1c, dsc1c,
                 wide(dhg), wide(dlb), drpb_rows]
    pack = jnp.concatenate(pack_rows, axis=0)
    npk = -(-pack.shape[0] // 8) * 8
    pack = jnp.pad(pack, ((0, npk - pack.shape[0]), (0, 0)))
    gp = _all_gather8(pack).reshape(8, npk, D)
    tot = _sum8(gp, "sum_small_grads")

    loss = (0.5 / D) * jnp.sum(tot[0])
    grad_final_g = tot[1]
    grad_norm2_g = tot[2:3]
    grad_norm1_g = tot[3:4]
    grad_hg_norm_g = tot[12:13, :HG_DIM]
    dlb_tot = tot[13, :2 * hgf].reshape(2, hgf)
    grad_na_rpb = tot[14:14 + nrp].reshape(-1)[:drpb.shape[1]].reshape(na_rpb.shape)
    dlog = jnp.stack([dlb_tot * p_lb[0] * (1.0 - p_lb[0]), -dlb_tot * p_lb[0] * p_lb[1]], axis=0)
    grad_hg_lb = lax.dynamic_slice(dlog, (0, 0, sidx * fs), (2, 2, fs))

    dmod_all = gp[:, 4:10].reshape(8, N_MOD * D)
    dmod_ctx = jnp.concatenate([tot[10], tot[11], jnp.zeros((4 * D,), F32)])[None]
    dm16 = jnp.concatenate([dmod_all, dmod_ctx, jnp.zeros((7, N_MOD * D), F32)], axis=0)
    grad_b_ada = jnp.sum(dm16, axis=0, keepdims=True)
    dm_sh = lax.dynamic_slice(dm16, (0, sidx * ads), (16, ads))
    g_wada, dcin = _ada_bwd(cin, w_ada[0], dm_sh)
    gc = _all_gather8(dcin[8:16]).reshape(8, 8, D)
    grad_c_ctx = (gc[0, 0] + gc[2, 0] + gc[4, 0] + gc[6, 0]) * _dsilu(c_ctx)

    g_win, g_wpa, g_wpb, g_wout, g_wi, g_wfo = _reduce_scatter(
        scidx, [g_win, g_wpa, g_wpb, g_wout, g_wi, g_wfo], names)

    grads = {
        "c_ctx": grad_c_ctx, "w_ada": g_wada[None], "b_ada": grad_b_ada, "norm1_g": grad_norm1_g,
        "w_in": g_win[None], "na_rpb": grad_na_rpb, "hg_lb_logits": grad_hg_lb, "hg_norm_g": grad_hg_norm_g,
        "w_pa": g_wpa[None], "w_pb": g_wpb[None], "w_out": g_wout[None], "norm2_g": grad_norm2_g,
        "w_ffn_in": g_wi[None], "w_ffn_out": g_wfo[None], "final_g": grad_final_g,
    }
    weights = {
        "c_ctx": (c_ctx, m_c_ctx, v_c_ctx), "w_ada": (w_ada, m_w_ada, v_w_ada), "b_ada": (b_ada, m_b_ada, v_b_ada),
        "norm1_g": (norm1_g, m_norm1_g, v_norm1_g), "w_in": (w_in, m_w_in, v_w_in),
        "na_rpb": (na_rpb, m_na_rpb, v_na_rpb), "hg_lb_logits": (hg_lb_logits, m_hg_lb_logits, v_hg_lb_logits),
        "hg_norm_g": (hg_norm_g, m_hg_norm_g, v_hg_norm_g), "w_pa": (w_pa, m_w_pa, v_w_pa),
        "w_pb": (w_pb, m_w_pb, v_w_pb), "w_out": (w_out, m_w_out, v_w_out),
        "norm2_g": (norm2_g, m_norm2_g, v_norm2_g), "w_ffn_in": (w_ffn_in, m_w_ffn_in, v_w_ffn_in),
        "w_ffn_out": (w_ffn_out, m_w_ffn_out, v_w_ffn_out), "final_g": (final_g, m_final_g, v_final_g),
    }
    order = list(weights)
    deltas, new_ms, new_vs = [], [], []
    for nm in order:
        w, m, v = weights[nm]
        g = grads[nm].reshape(w.shape)
        grads[nm] = g
        if w.ndim == 3 and w.shape[0] == 1:
            d_, m_, v_ = _adamw(w[0], g[0], m[0], v[0], "adamw_" + nm)
            d_, m_, v_ = d_[None], m_[None], v_[None]
        else:
            d_, m_, v_ = _adamw(w, g, m, v, "adamw_" + nm)
        deltas.append(d_)
        new_ms.append(m_)
        new_vs.append(v_)

    return (loss, grad_x[None], *[grads[nm] for nm in order], *deltas, *new_ms, *new_vs)
```

```python
import numpy as np

import jax
import jax.numpy as jnp
from jax import lax
from jax.experimental import pallas as pl
from jax.experimental.pallas import tpu as pltpu

F32 = jnp.float32
BF16 = jnp.bfloat16

GRID_W = 64
WIN_H = 8
WIN_W = 16
NA_HEADS = 16
NA_HEAD_DIM = 64
HG_HEADS = 8
HG_DIM = 128
HG_CHUNK = 64
N_MOD = 6
EPS = 1e-6
ADAM_LR = 0.001
ADAM_B1 = 0.9
ADAM_B2 = 0.999
ADAM_EPS = 1e-08
ADAM_WD = 0.01
ADAM_STEP = 10

LANES = 128
NA_ROWS_PER_STEP = 8
VMEM_LIMIT = 56 * 1024 * 1024
MASK_VALUE = -1e30
EXP_CLAMP = 80.0
MESH_ID = pl.DeviceIdType.MESH
HI = lax.Precision.HIGHEST


def _tile(dim, target, mult=LANES):
    best = None
    t = mult
    while t <= min(dim, target):
        if dim % t == 0:
            best = t
        t += mult
    assert best is not None, (dim, target, mult)
    return best


def _params(sem):
    return pltpu.CompilerParams(dimension_semantics=sem, vmem_limit_bytes=VMEM_LIMIT)


def _dot(a, b, precision=None):
    return jnp.dot(a, b, preferred_element_type=F32, precision=precision)


def _dot_nt(a, b, precision=None):
    return lax.dot_general(a, b, (((1,), (1,)), ((), ())), preferred_element_type=F32, precision=precision)


def _dot_tn(a, b, precision=None):
    return lax.dot_general(a, b, (((0,), (0,)), ((), ())), preferred_element_type=F32, precision=precision)


def _split2(v):
    hi = v.astype(BF16)
    return hi, (v - hi.astype(F32)).astype(BF16)


def _dot_x3(dot, a2, b2):
    return dot(a2[0], b2[0]) + (dot(a2[0], b2[1]) + dot(a2[1], b2[0]))


def _sigmoid(v):
    return 1.0 / (1.0 + jnp.exp(-v))


def _mm_call(dot, operands, grid, in_specs, out_spec, out_shape, acc_shape, name, carry):
    nk = grid[2]
    nci = 0 if carry is None else len(carry["ins"])
    nco = 0 if carry is None else len(carry["outs"])

    def body(*refs):
        a_ref, b_ref = refs[:2]
        cin = refs[2:2 + nci]
        o_ref = refs[2 + nci]
        cout = refs[3 + nci:3 + nci + nco]
        acc = refs[3 + nci + nco]
        sems = refs[4 + nci + nco:]
        m, n, k = pl.program_id(0), pl.program_id(1), pl.program_id(2)

        if carry is not None:
            @pl.when((m == 0) & (n == 0) & (k == 0))
            def _():
                carry["start"](cin, cout, *sems)

        @pl.when(k == 0)
        def _():
            acc[...] = jnp.zeros_like(acc)

        acc[...] += dot(a_ref[...], b_ref[...])

        @pl.when(k == nk - 1)
        def _():
            o_ref[...] = acc[...].astype(o_ref.dtype)

        if carry is not None:
            @pl.when((m == grid[0] - 1) & (n == grid[1] - 1) & (k == nk - 1))
            def _():
                carry["finish"](cin, cout, *sems)

    any_spec = pl.BlockSpec(memory_space=pl.ANY)
    scratch = [pltpu.VMEM(acc_shape, F32)]
    extra = {}
    if carry is not None:
        scratch += [pltpu.SemaphoreType.DMA((carry["nsem"],)), pltpu.SemaphoreType.DMA((carry["nsem"],))]
        extra["input_output_aliases"] = {2 + i: 1 + j for i, j in carry["alias"].items()}
    sem = ("arbitrary",) * 3 if carry is not None else ("parallel", "parallel", "arbitrary")
    res = pl.pallas_call(
        body, name=name, grid=grid,
        in_specs=list(in_specs) + [any_spec] * nci,
        out_specs=[out_spec] + [any_spec] * nco,
        out_shape=[out_shape] + ([] if carry is None else list(carry["outs"])),
        scratch_shapes=scratch, compiler_params=_params(sem), **extra,
    )(*operands, *([] if carry is None else carry["ins"]))
    return res[0] if carry is None else (res[0], list(res[1:]))


def _mm_nn(a, b3, out_dtype, name, carry=None):
    M, K = a.shape
    nsh, _, Ns = b3.shape
    tm, tn, tk = _tile(M, 1024), _tile(Ns, 1408), _tile(K, 2048)
    tps, nk = Ns // tn, K // tk
    return _mm_call(
        _dot, (a, b3), (M // tm, nsh * tps, nk),
        [pl.BlockSpec((tm, tk), lambda m, n, k: (m, k)),
         pl.BlockSpec((None, tk, tn), lambda m, n, k: (n // tps, k, n % tps))],
        pl.BlockSpec((tm, tn), lambda m, n, k: (m, n)),
        jax.ShapeDtypeStruct((M, nsh * Ns), out_dtype), (tm, tn), name, carry)


def _mm_nt(a, b3, out_dtype, name, carry=None):
    M = a.shape[0]
    nsh, Kw, Ns = b3.shape
    tm, tn, tk = _tile(M, 1024), _tile(Kw, 1408), _tile(Ns, 2048)
    kps = Ns // tk
    return _mm_call(
        _dot_nt, (a, b3), (M // tm, Kw // tn, nsh * kps),
        [pl.BlockSpec((tm, tk), lambda m, n, k: (m, k)),
         pl.BlockSpec((None, tn, tk), lambda m, n, k: (k // kps, n, k % kps))],
        pl.BlockSpec((tm, tn), lambda m, n, k: (m, n)),
        jax.ShapeDtypeStruct((M, Kw), out_dtype), (tm, tn), name, carry)


def _mm_tn(a, g, nsh, name, carry=None):
    Tk, M = a.shape
    Ns = g.shape[1] // nsh
    tm, tn, tk = _tile(M // 2, 1408), _tile(Ns, 1408), _tile(Tk, 1024)
    mh, tps, nk = (M // 2) // tm, Ns // tn, Tk // tk
    return _mm_call(
        _dot_tn, (a, g), (M // tm, nsh * tps, nk),
        [pl.BlockSpec((tk, tm), lambda m, n, k: (k, m)),
         pl.BlockSpec((tk, tn), lambda m, n, k: (k, n))],
        pl.BlockSpec((None, None, tm, tn), lambda m, n, k: (n // tps, m // mh, m % mh, n % tps)),
        jax.ShapeDtypeStruct((nsh, 2, M // 2, Ns), F32), (tm, tn), name, carry)


def _rowwise(fn, nblk, tm, rins, vins, routs, accs, name):
    nr, nv, no, na = len(rins), len(vins), len(routs), len(accs)

    def body(*refs):
        i = pl.program_id(0)
        outs, accv = fn(i, [r[...] for r in refs[:nr]], [r[...] for r in refs[nr:nr + nv]])
        for r, v in zip(refs[nr + nv:nr + nv + no], outs):
            r[...] = v.astype(r.dtype)
        arefs = refs[nr + nv + no:]
        if na:
            @pl.when(i == 0)
            def _():
                for a in arefs:
                    a[...] = jnp.zeros_like(a)

            for a, v in zip(arefs, accv):
                a[...] += v

    def row_spec(w, cb, rm):
        if rm is None:
            return pl.BlockSpec((tm, w), lambda i: (i, cb))
        return pl.BlockSpec((tm, w), lambda i: (rm(i), cb))

    in_specs = [row_spec(w, cb, rm) for (_, w, cb, rm) in rins]
    in_specs += [pl.BlockSpec(v.shape, lambda i: (0, 0)) for v in vins]
    out_specs = [pl.BlockSpec((tm, w), lambda i: (i, 0)) for (w, _) in routs]
    out_specs += [pl.BlockSpec((1, w), lambda i: (0, 0)) for w in accs]
    out_shape = [jax.ShapeDtypeStruct((nblk * tm, w), dt) for (w, dt) in routs]
    out_shape += [jax.ShapeDtypeStruct((1, w), F32) for w in accs]
    res = pl.pallas_call(
        body, name=name, grid=(nblk,), in_specs=in_specs, out_specs=out_specs, out_shape=out_shape,
        compiler_params=_params(("arbitrary",)),
    )(*[r[0] for r in rins], *vins)
    return list(res)


def _colsum(v):
    return jnp.sum(v, axis=0, keepdims=True)


def _rms(v):
    return lax.rsqrt(jnp.mean(v * v, axis=-1, keepdims=True) + EPS)


def _all_gather8(xs):
    m_per, n = xs.shape

    def body(x_ref, out_ref, send_sems, recv_sems, local_sem):
        x, y, c = lax.axis_index("x"), lax.axis_index("y"), lax.axis_index("c")
        me, sibling = (x, y, c), (x, y, 1 - c)
        chips = [(1 - x, y), (x, 1 - y), (1 - x, 1 - y)]

        def rows(px, py, pc):
            return out_ref.at[pl.ds((4 * px + 2 * py + pc) * m_per, m_per), :]

        def copy(k, block, to, src=None):
            return pltpu.make_async_remote_copy(
                src_ref=rows(*block) if src is None else src, dst_ref=rows(*block),
                send_sem=send_sems.at[k], recv_sem=recv_sems.at[k], device_id=to, device_id_type=MESH_ID)

        mine = pltpu.make_async_copy(x_ref, rows(*me), local_sem)
        mine.start()
        first = [copy(0, me, sibling, src=x_ref)]
        first += [copy(1 + j, me, (*chip, c), src=x_ref) for j, chip in enumerate(chips)]
        for cp in first:
            cp.start()
        passed = [copy(4 + j, (*chip, c), sibling) for j, chip in enumerate(chips)]
        for j, chip in enumerate(chips):
            copy(1 + j, (*chip, c), me).wait_recv()
            passed[j].start()
        copy(0, sibling, me).wait_recv()
        for j, chip in enumerate(chips):
            copy(4 + j, (*chip, 1 - c), me).wait_recv()
        for cp in first + passed:
            cp.wait_send()
        mine.wait()

    return pl.pallas_call(
        body, name="all_gather8_%dx%d" % (m_per, n),
        out_shape=jax.ShapeDtypeStruct((8 * m_per, n), xs.dtype),
        in_specs=[pl.BlockSpec(memory_space=pltpu.VMEM)],
        out_specs=pl.BlockSpec(memory_space=pltpu.VMEM),
        scratch_shapes=[pltpu.SemaphoreType.DMA((7,)), pltpu.SemaphoreType.DMA((7,)), pltpu.SemaphoreType.DMA],
    )(xs)


def _mesh_pos():
    x, y, c = lax.axis_index("x"), lax.axis_index("y"), lax.axis_index("c")
    chips = [(1 - x, y), (x, 1 - y), (1 - x, 1 - y)]
    return x, y, c, chips


def _cast_place(sidx, w, name):
    r, cw = w.shape
    rh = r // 2
    tm = _tile(rh, max(16, (1 << 20) // (4 * cw)), 16)
    nt = rh // tm

    def body(s_ref, w_ref, o_ref):
        o_ref[...] = w_ref[...].astype(o_ref.dtype)

    return pl.pallas_call(
        body, name=name,
        grid_spec=pltpu.PrefetchScalarGridSpec(
            num_scalar_prefetch=1, grid=(2, nt),
            in_specs=[pl.BlockSpec((tm, cw), lambda h, i, s_ref: (h * nt + i, 0))],
            out_specs=pl.BlockSpec((None, None, tm, cw), lambda h, i, s_ref: (s_ref[0], h, i, 0))),
        out_shape=jax.ShapeDtypeStruct((4, 2, rh, cw), BF16),
        compiler_params=_params(("parallel", "parallel")),
    )(sidx, w)


def _exchange_gather(bufs):
    n = len(bufs)

    def copies(out, send_sems, recv_sems, base):
        def copy(i, k, shard, half, to):
            dst = out[i].at[shard, half]
            return pltpu.make_async_remote_copy(
                src_ref=dst, dst_ref=dst, send_sem=send_sems.at[base + 6 * i + k],
                recv_sem=recv_sems.at[base + 6 * i + k], device_id=to, device_id_type=MESH_ID)
        return copy

    def first(copy):
        x, y, c, chips = _mesh_pos()
        return [copy(i, j, 2 * x + y, c, (*chip, c)) for i in range(n) for j, chip in enumerate(chips)]

    def start(cin, out, send_sems, recv_sems, base=0):
        for cp in first(copies(out, send_sems, recv_sems, base)):
            cp.start()

    def finish(cin, out, send_sems, recv_sems, base=0):
        copy = copies(out, send_sems, recv_sems, base)
        x, y, c, chips = _mesh_pos()
        passed = []
        for j, chip in enumerate(chips):
            sj = 2 * chip[0] + chip[1]
            for i in range(n):
                copy(i, j, sj, c, (x, y, c)).wait_recv()
                cp = copy(i, 3 + j, sj, c, (x, y, 1 - c))
                cp.start()
                passed.append(cp)
        for j, chip in enumerate(chips):
            sj = 2 * chip[0] + chip[1]
            for i in range(n):
                copy(i, 3 + j, sj, 1 - c, (x, y, c)).wait_recv()
        for cp in first(copy) + passed:
            cp.wait_send()

    return dict(ins=list(bufs), outs=[jax.ShapeDtypeStruct(b.shape, b.dtype) for b in bufs],
                alias={i: i for i in range(n)}, nsem=6 * n, start=start, finish=finish)


def _exchange_join(a, b):
    nai, nao = len(a["ins"]), len(a["outs"])

    def start(cin, cout, send_sems, recv_sems, base=0):
        a["start"](cin[:nai], cout[:nao], send_sems, recv_sems, base)
        b["start"](cin[nai:], cout[nao:], send_sems, recv_sems, base + a["nsem"])

    def finish(cin, cout, send_sems, recv_sems, base=0):
        a["finish"](cin[:nai], cout[:nao], send_sems, recv_sems, base)
        b["finish"](cin[nai:], cout[nao:], send_sems, recv_sems, base + a["nsem"])

    alias = dict(a["alias"])
    alias.update({nai + i: nao + j for i, j in b["alias"].items()})
    return dict(ins=a["ins"] + b["ins"], outs=a["outs"] + b["outs"], alias=alias, nsem=a["nsem"] + b["nsem"],
                start=start, finish=finish)


def _exchange_call(ex, name):
    nci, nco = len(ex["ins"]), len(ex["outs"])

    def body(*refs):
        cin, cout, sems = refs[:nci], refs[nci:nci + nco], refs[nci + nco:]
        ex["start"](cin, cout, *sems)
        ex["finish"](cin, cout, *sems)

    any_spec = pl.BlockSpec(memory_space=pl.ANY)
    return list(pl.pallas_call(
        body, name=name, out_shape=list(ex["outs"]), in_specs=[any_spec] * nci, out_specs=[any_spec] * nco,
        input_output_aliases=dict(ex["alias"]),
        scratch_shapes=[pltpu.SemaphoreType.DMA((ex["nsem"],)), pltpu.SemaphoreType.DMA((ex["nsem"],))],
    )(*ex["ins"]))


def _exchange_swap_other_half(gs):
    n = len(gs)

    def copies(g, land, send_sems, recv_sems, base):
        x, y, c, _ = _mesh_pos()
        return [pltpu.make_async_remote_copy(
            src_ref=g[i].at[:, 1 - c], dst_ref=land[i], send_sem=send_sems.at[base + i],
            recv_sem=recv_sems.at[base + i], device_id=(x, y, 1 - c), device_id_type=MESH_ID) for i in range(n)]

    def start(g, land, send_sems, recv_sems, base=0):
        for cp in copies(g, land, send_sems, recv_sems, base):
            cp.start()

    def finish(g, land, send_sems, recv_sems, base=0):
        for cp in copies(g, land, send_sems, recv_sems, base):
            cp.wait()

    return dict(ins=list(gs), outs=[jax.ShapeDtypeStruct((4,) + g.shape[2:], g.dtype) for g in gs],
                alias={}, nsem=n, start=start, finish=finish)


def _exchange_scatter(ps):
    n = len(ps)

    def copies(p, land, send_sems, recv_sems, base):
        x, y, c, chips = _mesh_pos()
        return [pltpu.make_async_remote_copy(
            src_ref=p[i].at[2 * chip[0] + chip[1]], dst_ref=land[i].at[j],
            send_sem=send_sems.at[base + 3 * i + j], recv_sem=recv_sems.at[base + 3 * i + j],
            device_id=(*chip, c), device_id_type=MESH_ID) for i in range(n) for j, chip in enumerate(chips)]

    def start(p, land, send_sems, recv_sems, base=0):
        for cp in copies(p, land, send_sems, recv_sems, base):
            cp.start()

    def finish(p, land, send_sems, recv_sems, base=0):
        for cp in copies(p, land, send_sems, recv_sems, base):
            cp.wait()

    return dict(ins=list(ps), outs=[jax.ShapeDtypeStruct((3,) + p.shape[1:], p.dtype) for p in ps],
                alias={}, nsem=3 * n, start=start, finish=finish)


def _exchange_swap_result(bufs):
    n = len(bufs)

    def copies(out, send_sems, recv_sems, base, half):
        x, y, c, _ = _mesh_pos()
        h = c if half == "mine" else 1 - c
        return [pltpu.make_async_remote_copy(
            src_ref=out[i].at[h], dst_ref=out[i].at[h], send_sem=send_sems.at[base + i],
            recv_sem=recv_sems.at[base + i], device_id=(x, y, 1 - c), device_id_type=MESH_ID) for i in range(n)]

    def start(cin, out, send_sems, recv_sems, base=0):
        for cp in copies(out, send_sems, recv_sems, base, "mine"):
            cp.start()

    def finish(cin, out, send_sems, recv_sems, base=0):
        for cp in copies(out, send_sems, recv_sems, base, "theirs"):
            cp.wait_recv()
        for cp in copies(out, send_sems, recv_sems, base, "mine"):
            cp.wait_send()

    return dict(ins=list(bufs), outs=[jax.ShapeDtypeStruct(b.shape, b.dtype) for b in bufs],
                alias={i: i for i in range(n)}, nsem=n, start=start, finish=finish)


def _add_own_half(cidx, g, land, name):
    _, _, rh, cw = g.shape
    tm = _tile(rh, max(16, (1 << 20) // (4 * cw)), 16)

    def body(c_ref, g_ref, l_ref, o_ref):
        o_ref[...] = (g_ref[...] + l_ref[...]).astype(o_ref.dtype)

    return pl.pallas_call(
        body, name=name,
        grid_spec=pltpu.PrefetchScalarGridSpec(
            num_scalar_prefetch=1, grid=(4, rh // tm),
            in_specs=[pl.BlockSpec((None, None, tm, cw), lambda s, i, c_ref: (s, c_ref[0], i, 0)),
                      pl.BlockSpec((None, tm, cw), lambda s, i, c_ref: (s, i, 0))],
            out_specs=pl.BlockSpec((None, tm, cw), lambda s, i, c_ref: (s, i, 0))),
        out_shape=jax.ShapeDtypeStruct((4, rh, cw), BF16),
        compiler_params=_params(("parallel", "parallel")),
    )(cidx, g, land)


def _sum_pieces(scidx, part, land, name):
    _, rh, cw = land.shape
    tm = _tile(rh, max(16, (1 << 20) // (4 * cw)), 16)

    def body(sc_ref, p_ref, l_ref, o_ref):
        v = l_ref[...].astype(F32)
        o_ref[...] = (p_ref[...].astype(F32) + v[0]) + (v[1] + v[2])

    return pl.pallas_call(
        body, name=name,
        grid_spec=pltpu.PrefetchScalarGridSpec(
            num_scalar_prefetch=1, grid=(rh // tm,),
            in_specs=[pl.BlockSpec((None, tm, cw), lambda i, sc_ref: (sc_ref[0], i, 0)),
                      pl.BlockSpec((3, tm, cw), lambda i, sc_ref: (0, i, 0))],
            out_specs=pl.BlockSpec((None, tm, cw), lambda i, sc_ref: (sc_ref[1], i, 0))),
        out_shape=jax.ShapeDtypeStruct((2, rh, cw), F32),
        compiler_params=_params(("parallel",)),
    )(scidx, part, land)


def _sum8(g, name):
    def body(g_ref, o_ref):
        acc = g_ref[0]
        for k in range(1, 8):
            acc = acc + g_ref[k]
        o_ref[...] = acc

    return pl.pallas_call(body, name=name, out_shape=jax.ShapeDtypeStruct(g.shape[1:], F32))(g)


def _silu(v):
    return v * _sigmoid(v)


def _dsilu(v):
    s = _sigmoid(v)
    return s * (1.0 + v * (1.0 - s))


def _ada_fwd(cin, w, b):
    d, ns = w.shape
    tn = _tile(ns, 512)

    def body(c_ref, w_ref, b_ref, o_ref):
        o_ref[...] = _dot(_silu(c_ref[...]), w_ref[...], HI) + b_ref[...]

    return pl.pallas_call(
        body, name="ada_fwd", grid=(ns // tn,),
        in_specs=[pl.BlockSpec(cin.shape, lambda n: (0, 0)), pl.BlockSpec((d, tn), lambda n: (0, n)),
                  pl.BlockSpec((1, tn), lambda n: (0, n))],
        out_specs=pl.BlockSpec((cin.shape[0], tn), lambda n: (0, n)),
        out_shape=jax.ShapeDtypeStruct((cin.shape[0], ns), F32),
        compiler_params=_params(("parallel",)),
    )(cin, w, b)


def _ada_bwd(cin, w, dm):
    d, ns = w.shape
    tn = _tile(ns, 512)

    def body(c_ref, w_ref, d_ref, dw_ref, dc_ref):
        n = pl.program_id(0)

        @pl.when(n == 0)
        def _():
            dc_ref[...] = jnp.zeros_like(dc_ref)

        dw_ref[...] = _dot_tn(_silu(c_ref[...]), d_ref[...], HI)
        dc_ref[...] += _dot_nt(d_ref[...], w_ref[...], HI)

    return pl.pallas_call(
        body, name="ada_bwd", grid=(ns // tn,),
        in_specs=[pl.BlockSpec(cin.shape, lambda n: (0, 0)), pl.BlockSpec((d, tn), lambda n: (0, n)),
                  pl.BlockSpec((cin.shape[0], tn), lambda n: (0, n))],
        out_specs=[pl.BlockSpec((d, tn), lambda n: (0, n)), pl.BlockSpec(cin.shape, lambda n: (0, 0))],
        out_shape=[jax.ShapeDtypeStruct((d, ns), F32), jax.ShapeDtypeStruct(cin.shape, F32)],
        compiler_params=_params(("arbitrary",)),
    )(cin, w, dm)


def _bias_tables(rpb, rows):
    kh = min(WIN_H, rows)
    col = jnp.arange(GRID_W)
    col_start = jnp.clip(col - WIN_W // 2, 0, GRID_W - WIN_W)
    in_win = (col[None, :] >= col_start[:, None]) & (col[None, :] < col_start[:, None] + WIN_W)
    dc_idx = jnp.clip(col[None, :] - col[:, None], 1 - WIN_W, WIN_W - 1) + WIN_W - 1
    dr_idx = jnp.arange(kh)[None, :] - jnp.arange(kh)[:, None] + WIN_H - 1
    t = rpb[:, dr_idx][:, :, :, dc_idx]
    t = jnp.where(in_win[None, None, None], t, MASK_VALUE)
    return t.transpose(0, 1, 3, 2, 4).reshape(rpb.shape[0], kh, GRID_W, kh * GRID_W).astype(F32)


def _mm_f32(a, b, name):
    M, K = a.shape
    N = b.shape[1]
    tm = _tile(M, 256, 8)

    def body(a_ref, b_ref, o_ref):
        o_ref[...] = _dot(a_ref[...], b_ref[...], HI)

    return pl.pallas_call(
        body, name=name, grid=(M // tm,),
        in_specs=[pl.BlockSpec((tm, K), lambda i: (i, 0)), pl.BlockSpec((K, N), lambda i: (0, 0))],
        out_specs=pl.BlockSpec((tm, N), lambda i: (i, 0)),
        out_shape=jax.ShapeDtypeStruct((M, N), F32),
        compiler_params=_params(("parallel",)),
    )(a, b)


def _bias_tables_transpose(dbias, rows):
    kh = min(WIN_H, rows)
    nh = dbias.shape[0]
    col = np.arange(GRID_W)
    col_start = np.clip(col - WIN_W // 2, 0, GRID_W - WIN_W)
    in_win = (col[None, :] >= col_start[:, None]) & (col[None, :] < col_start[:, None] + WIN_W)
    dc_idx = np.clip(col[None, :] - col[:, None], 1 - WIN_W, WIN_W - 1) + WIN_W - 1
    ndc = 2 * WIN_W - 1
    onehot = np.zeros((GRID_W * GRID_W, LANES), np.float32)
    qq, kk = np.nonzero(in_win)
    onehot[qq * GRID_W + kk, dc_idx[qq, kk]] = 1.0
    x = dbias.reshape(nh, kh, GRID_W, kh, GRID_W).transpose(0, 1, 3, 2, 4).reshape(nh * kh * kh, GRID_W * GRID_W)
    z = _mm_f32(x, jnp.asarray(onehot), "rpb_fold")[:, :ndc].reshape(nh, kh, kh, ndc)
    fold = np.zeros((2 * WIN_H - 1, kh, kh), np.float32)
    for sh in range(kh):
        for j in range(kh):
            fold[j - sh + WIN_H - 1, sh, j] = 1.0
    return jnp.einsum("dsj,hsjc->hdc", jnp.asarray(fold), z, precision=HI)


def _na_geometry(S):
    rows = S // GRID_W
    kh = min(WIN_H, rows)

    def row_start(r):
        return jnp.clip(r - kh // 2, 0, rows - kh)

    return rows, kh, row_start


def _na_by_head(ref, rr, lane):
    t = ref[rr * GRID_W:(rr + 1) * GRID_W, :]
    zero = jnp.zeros_like(t)
    return jnp.concatenate([jnp.where(lane < NA_HEAD_DIM, t, zero), jnp.where(lane >= NA_HEAD_DIM, t, zero)], axis=0)


def _na_pick_head(t2, lane):
    return jnp.where(lane < NA_HEAD_DIM, t2[:GRID_W], t2[GRID_W:])


def _na_scores(q_ref, k_ref, b_ref, i, nrs, nb, S, L, row_start, lane):
    qh = jnp.concatenate([_na_by_head(q_ref, rr, lane) for rr in range(nrs)], axis=0)
    sc = _dot_nt(qh, k_ref[pl.ds(S, L), :])
    starts, shifts, sb = [], [], []
    for rr in range(nrs):
        r = i * nrs + rr
        rs = row_start(r)
        starts.append(pl.multiple_of(rs * GRID_W, GRID_W))
        shifts.append(r - rs)
        bias = jnp.concatenate([b_ref[0, r - rs], b_ref[1, r - rs]], axis=0)
        sb.append(_dot_nt(qh[rr * 2 * GRID_W:(rr + 1) * 2 * GRID_W], k_ref[pl.ds(starts[-1], nb), :]) + bias)
    return qh, jnp.concatenate(sb, axis=0), sc, starts, shifts


def _na_fwd(qs, kb, vb, bias, S, L):
    T, naw = qs.shape
    rows, kh, row_start = _na_geometry(S)
    nb = kh * GRID_W
    npair = naw // LANES

    nrs = min(NA_ROWS_PER_STEP, rows)
    assert rows % nrs == 0

    def body(q_ref, k_ref, v_ref, b_ref, o_ref, lse_ref):
        i = pl.program_id(1)
        lane = lax.broadcasted_iota(jnp.int32, (GRID_W, LANES), 1)
        _, sb, sc, starts, _ = _na_scores(q_ref, k_ref, b_ref, i, nrs, nb, S, L, row_start, lane)
        m = jnp.maximum(jnp.max(sb, axis=-1, keepdims=True), jnp.max(sc, axis=-1, keepdims=True))
        pb, pc = jnp.exp(sb - m), jnp.exp(sc - m)
        l = jnp.sum(pb, axis=-1, keepdims=True) + jnp.sum(pc, axis=-1, keepdims=True)
        inv = 1.0 / l
        pb16, pc16 = (pb * inv).astype(BF16), (pc * inv).astype(BF16)
        oc = _dot(pc16, v_ref[pl.ds(S, L), :])
        lse = jnp.broadcast_to(m + jnp.log(l), oc.shape)
        for rr in range(nrs):
            two = slice(rr * 2 * GRID_W, (rr + 1) * 2 * GRID_W)
            rsl = slice(rr * GRID_W, (rr + 1) * GRID_W)
            o2 = oc[two] + _dot(pb16[two], v_ref[pl.ds(starts[rr], nb), :])
            o_ref[rsl, :] = _na_pick_head(o2, lane).astype(o_ref.dtype)
            lse_ref[rsl, :] = _na_pick_head(lse[two], lane)

    blk = pl.BlockSpec((nrs * GRID_W, LANES), lambda p, i: (i, p))
    col = pl.BlockSpec((T, LANES), lambda p, i: (0, p))
    return pl.pallas_call(
        body, name="na_fwd", grid=(npair, rows // nrs),
        in_specs=[blk, col, col, pl.BlockSpec((2, kh, GRID_W, nb), lambda p, i: (p, 0, 0, 0))],
        out_specs=[blk, blk],
        out_shape=[jax.ShapeDtypeStruct((S, naw), BF16), jax.ShapeDtypeStruct((S, naw), F32)],
        compiler_params=_params(("parallel", "arbitrary")),
    )(qs, kb, vb, bias)


def _na_bwd(qs, kb, vb, bias, do, o, lse, S, L):
    T, naw = qs.shape
    rows, kh, row_start = _na_geometry(S)
    nb = kh * GRID_W
    npair = naw // LANES

    nrs = min(NA_ROWS_PER_STEP, rows)
    assert rows % nrs == 0

    def body(q_ref, k_ref, v_ref, b_ref, do_ref, o_ref, lse_ref, dq_ref, dk_ref, dv_ref, db_ref):
        i = pl.program_id(1)

        @pl.when(i == 0)
        def _():
            dk_ref[...] = jnp.zeros_like(dk_ref)
            dv_ref[...] = jnp.zeros_like(dv_ref)
            db_ref[...] = jnp.zeros_like(db_ref)

        lane = lax.broadcasted_iota(jnp.int32, (GRID_W, LANES), 1)
        qh, sb, sc, starts, shifts = _na_scores(q_ref, k_ref, b_ref, i, nrs, nb, S, L, row_start, lane)
        doh = jnp.concatenate([_na_by_head(do_ref, rr, lane) for rr in range(nrs)], axis=0)
        o2 = jnp.concatenate([o_ref[rr * GRID_W:(rr + 1) * GRID_W, :] for rr in range(nrs) for _ in range(2)], axis=0)
        lse = jnp.concatenate([lse_ref[rr * GRID_W:(rr + 1) * GRID_W, :][:, hh * NA_HEAD_DIM:hh * NA_HEAD_DIM + 1]
                               for rr in range(nrs) for hh in range(2)], axis=0)
        pb, pc = jnp.exp(sb - lse), jnp.exp(sc - lse)
        delta = jnp.sum(doh.astype(F32) * o2.astype(F32), axis=-1, keepdims=True)
        dpb = jnp.concatenate([_dot_nt(doh[rr * 2 * GRID_W:(rr + 1) * 2 * GRID_W], v_ref[pl.ds(starts[rr], nb), :])
                               for rr in range(nrs)], axis=0)
        dsb = pb * (dpb - delta)
        dsc = pc * (_dot_nt(doh, v_ref[pl.ds(S, L), :]) - delta)
        dsb16, dsc16, pb16, pc16 = dsb.astype(BF16), dsc.astype(BF16), pb.astype(BF16), pc.astype(BF16)
        dqc = _dot(dsc16, k_ref[pl.ds(S, L), :])
        dk_ref[pl.ds(S, L), :] += _dot_tn(dsc16, qh)
        dv_ref[pl.ds(S, L), :] += _dot_tn(pc16, doh)
        for rr in range(nrs):
            two = slice(rr * 2 * GRID_W, (rr + 1) * 2 * GRID_W)
            band = pl.ds(starts[rr], nb)
            dq2 = dqc[two] + _dot(dsb16[two], k_ref[band, :])
            dq_ref[rr * GRID_W:(rr + 1) * GRID_W, :] = _na_pick_head(dq2, lane)
            dk_ref[band, :] += _dot_tn(dsb16[two], qh[two])
            dv_ref[band, :] += _dot_tn(pb16[two], doh[two])
            for hh in range(2):
                db_ref[hh, shifts[rr]] += dsb[(2 * rr + hh) * GRID_W:(2 * rr + hh + 1) * GRID_W]

    blk = pl.BlockSpec((nrs * GRID_W, LANES), lambda p, r: (r, p))
    col = pl.BlockSpec((T, LANES), lambda p, r: (0, p))
    return pl.pallas_call(
        body, name="na_bwd", grid=(npair, rows // nrs),
        in_specs=[blk, col, col, pl.BlockSpec((2, kh, GRID_W, nb), lambda p, r: (p, 0, 0, 0)), blk, blk, blk],
        out_specs=[blk, col, col, pl.BlockSpec((2, kh, GRID_W, nb), lambda p, r: (p, 0, 0, 0))],
        out_shape=[jax.ShapeDtypeStruct((S, naw), F32), jax.ShapeDtypeStruct((T, naw), F32),
                   jax.ShapeDtypeStruct((T, naw), F32), jax.ShapeDtypeStruct(bias.shape, F32)],
        compiler_params=_params(("parallel", "arbitrary")),
    )(qs, kb, vb, bias, do, o, lse)


def _hg_cols(naw, hgf, rev):
    qcol = (3 * naw) // hgf
    fcol = (3 * naw + hgf * (2 if rev else 1)) // hgf
    icol = (3 * naw + 3 * hgf) // hgf
    return qcol, fcol, icol


def _hg_chunk_order(S, L, rev):
    ncl, ncc = S // HG_CHUNK, L // HG_CHUNK
    nc = ncl + ncc

    def chunk_of(i):
        if rev:
            return nc - 1 - i
        return jnp.where(i < ncc, ncl + i, i - ncc)

    return nc, ncl, chunk_of


def _hg_gates(q, z, lb, rev):
    row = lax.broadcasted_iota(jnp.int32, (HG_CHUNK, HG_CHUNK), 0)
    colm = lax.broadcasted_iota(jnp.int32, (HG_CHUNK, HG_CHUNK), 1)
    tri = (colm >= row) if rev else (row >= colm)
    trif = tri.astype(F32)
    sig = _sigmoid(z)
    f = lb + (1.0 - lb) * sig
    lf = jnp.log(f)
    k = 1.0 - f
    cum = _dot(trif, lf, HI)
    mid = cum[HG_CHUNK // 2:HG_CHUNK // 2 + 1, :]
    last = cum[0:1, :] if rev else cum[HG_CHUNK - 1:HG_CHUNK, :]
    eq = jnp.exp(jnp.clip(cum - mid, -EXP_CLAMP, EXP_CLAMP))
    ek = jnp.exp(jnp.clip(mid - cum, -EXP_CLAMP, EXP_CLAMP))
    return tri, trif, sig, f, k, cum, last, eq, ek


def _hg_fwd(u, lbr, S, L, naw, hgf, rev):
    T = S + L
    nh = hgf // HG_DIM
    nc, ncl, chunk_of = _hg_chunk_order(S, L, rev)
    qcol, fcol, icol = _hg_cols(naw, hgf, rev)

    def body(q_ref, z_ref, v_ref, lb_ref, o_ref, st_ref, state):
        i = pl.program_id(0)

        @pl.when(i == 0)
        def _():
            state[...] = jnp.zeros_like(state)

        q, z, v = q_ref[...], z_ref[...], v_ref[...]
        tri, _, _, _, k, cum, last, eq, ek = _hg_gates(q, z, lb_ref[...], rev)
        qe, ke = (q * eq).astype(BF16), (k * ek).astype(BF16)
        qd, kd = (q * jnp.exp(cum)).astype(BF16), (k * jnp.exp(last - cum)).astype(BF16)
        v16, el = v.astype(BF16), jnp.exp(last)
        for h in range(nh):
            sl = slice(h * HG_DIM, (h + 1) * HG_DIM)
            a = jnp.where(tri, _dot_nt(qe[:, sl], ke[:, sl]), 0.0)
            s0 = state[h]
            st_ref[h] = s0
            o_ref[:, sl] = _dot(a.astype(BF16), v16[:, sl]) + _dot_nt(qd[:, sl], s0.astype(BF16))
            state[h] = s0 * el[:, sl] + _dot_tn(v16[:, sl], kd[:, sl])

    def blk(cb):
        return pl.BlockSpec((HG_CHUNK, hgf), lambda i: (chunk_of(i), cb))

    return pl.pallas_call(
        body, name="hg_fwd_rev" if rev else "hg_fwd", grid=(nc,),
        in_specs=[blk(qcol), blk(fcol), blk(icol), pl.BlockSpec((1, hgf), lambda i: (0, 0))],
        out_specs=[pl.BlockSpec((HG_CHUNK, hgf), lambda i: (chunk_of(i), 0)),
                   pl.BlockSpec((None, nh, HG_DIM, HG_DIM), lambda i: (chunk_of(i), 0, 0, 0))],
        out_shape=[jax.ShapeDtypeStruct((T, hgf), F32), jax.ShapeDtypeStruct((nc, nh, HG_DIM, HG_DIM), F32)],
        scratch_shapes=[pltpu.VMEM((nh, HG_DIM, HG_DIM), F32)],
        compiler_params=_params(("arbitrary",)),
    )(u, u, u, lbr)


def _hg_bwd(u, lbr, st, do, S, L, naw, hgf, rev):
    T = S + L
    nh = hgf // HG_DIM
    nc, ncl, chunk_fwd = _hg_chunk_order(S, L, rev)
    qcol, fcol, icol = _hg_cols(naw, hgf, rev)

    def chunk_of(j):
        return chunk_fwd(nc - 1 - j)

    def body(q_ref, z_ref, v_ref, lb_ref, st_ref, do_ref, dq_ref, dz_ref, dv_ref, dlb_ref, dstate,
             dqe_s, dke_s, dqd_s, dkd_s, dl_s):
        j = pl.program_id(0)

        @pl.when(j == 0)
        def _():
            dstate[...] = jnp.zeros_like(dstate)
            dlb_ref[...] = jnp.zeros_like(dlb_ref)

        q, z, v = q_ref[...], z_ref[...], v_ref[...]
        lb = lb_ref[...]
        tri, trif, sig, f, k, cum, last, eq, ek = _hg_gates(q, z, lb, rev)
        ec, el, ekd = jnp.exp(cum), jnp.exp(last), jnp.exp(last - cum)
        qe, ke, qd, kd = q * eq, k * ek, q * ec, k * ekd
        qd16, kd16 = qd.astype(BF16), kd.astype(BF16)
        dout = jnp.where(chunk_of(j) < ncl, do_ref[...], 0.0)
        qe2, ke2, v2, dout2 = _split2(qe), _split2(ke), _split2(v), _split2(dout)
        for h in range(nh):
            sl = slice(h * HG_DIM, (h + 1) * HG_DIM)
            qeh, keh, vh, douth = [(t[0][:, sl], t[1][:, sl]) for t in (qe2, ke2, v2, dout2)]
            a = jnp.where(tri, _dot_nt(qeh[0], keh[0]), 0.0).astype(BF16)
            s0 = st_ref[h]
            ds1 = dstate[h]
            s02, ds12 = _split2(s0), _split2(ds1)
            dv_ref[:, sl] = _dot_tn(a, douth[0]) + _dot_nt(kd16[:, sl], ds12[0])
            da2 = _split2(jnp.where(tri, _dot_x3(_dot_nt, douth, vh), 0.0))
            dqe_s[:, sl] = _dot_x3(_dot, da2, keh)
            dke_s[:, sl] = _dot_x3(_dot_tn, da2, qeh)
            dqd_s[:, sl] = _dot_x3(_dot, douth, s02)
            dkd_s[:, sl] = _dot_x3(_dot, vh, ds12)
            dl_s[:, sl] = _colsum(ds1 * s0)
            dstate[h] = _dot_tn(douth[0], qd16[:, sl]) + ds1 * el[:, sl]
        dqe, dke, dqd, dkd = dqe_s[...], dke_s[...], dqd_s[...], dkd_s[...]
        dq_ref[...] = dqe * eq + dqd * ec
        dk = dke * ek + dkd * ekd
        dcum = dqe * qe - dke * ke + dqd * qd - dkd * kd
        dlast = _colsum(dkd * kd) + el * dl_s[...]
        dlf = _dot_tn(trif, dcum, HI) + dlast
        df = dlf / f - dk
        dz_ref[...] = df * (1.0 - lb) * sig * (1.0 - sig)
        dlb_ref[...] += _colsum(df * (1.0 - sig))

    def blk(cb):
        return pl.BlockSpec((HG_CHUNK, hgf), lambda j: (chunk_of(j), cb))

    oblk = pl.BlockSpec((HG_CHUNK, hgf), lambda j: (chunk_of(j), 0))
    wide = pltpu.VMEM((HG_CHUNK, hgf), F32)
    return pl.pallas_call(
        body, name="hg_bwd_rev" if rev else "hg_bwd", grid=(nc,),
        in_specs=[blk(qcol), blk(fcol), blk(icol), pl.BlockSpec((1, hgf), lambda j: (0, 0)),
                  pl.BlockSpec((None, nh, HG_DIM, HG_DIM), lambda j: (chunk_of(j), 0, 0, 0)),
                  pl.BlockSpec((HG_CHUNK, hgf), lambda j: (jnp.minimum(chunk_of(j), ncl - 1), 0))],
        out_specs=[oblk, oblk, oblk, pl.BlockSpec((1, hgf), lambda j: (0, 0))],
        out_shape=[jax.ShapeDtypeStruct((T, hgf), F32)] * 3 + [jax.ShapeDtypeStruct((1, hgf), F32)],
        scratch_shapes=[pltpu.VMEM((nh, HG_DIM, HG_DIM), F32), wide, wide, wide, wide,
                        pltpu.VMEM((1, hgf), F32)],
        compiler_params=_params(("arbitrary",)),
    )(u, u, u, lbr, st, do)


def _adamw(w, g, m, v, name):
    shape = w.shape
    if w.ndim != 2 or shape[0] % 8 or shape[1] % LANES:
        w, g, m, v = [a.reshape(1, -1) for a in (w, g, m, v)]
    r, cw = w.shape
    tm = _tile(r, max(8, (1 << 19) // cw), 8) if r % 8 == 0 else r
    c1 = 1.0 / (1.0 - ADAM_B1 ** ADAM_STEP)
    c2 = 1.0 / (1.0 - ADAM_B2 ** ADAM_STEP)

    def body(w_ref, g_ref, m_ref, v_ref, d_ref, nm_ref, nv_ref):
        gg = g_ref[...]
        nm = ADAM_B1 * m_ref[...] + (1.0 - ADAM_B1) * gg
        nv = ADAM_B2 * v_ref[...] + (1.0 - ADAM_B2) * (gg * gg)
        d_ref[...] = -ADAM_LR * ((nm * c1) / (jnp.sqrt(nv * c2) + ADAM_EPS) + ADAM_WD * w_ref[...])
        nm_ref[...] = nm
        nv_ref[...] = nv

    spec = pl.BlockSpec((tm, cw), lambda i: (i, 0))
    outs = pl.pallas_call(
        body, name=name, grid=(r // tm,), in_specs=[spec] * 4, out_specs=[spec] * 3,
        out_shape=[jax.ShapeDtypeStruct((r, cw), F32)] * 3,
        compiler_params=_params(("parallel",)),
    )(w, g, m, v)
    return [o.reshape(shape) for o in outs]


def kernel(x, c, ctx, c_ctx, w_ada, b_ada, norm1_g, w_in, na_rpb, hg_lb_logits, hg_norm_g, w_pa, w_pb, w_out, norm2_g, w_ffn_in, w_ffn_out, final_g, loss_target, m_c_ctx, m_w_ada, m_b_ada, m_norm1_g, m_w_in, m_na_rpb, m_hg_lb_logits, m_hg_norm_g, m_w_pa, m_w_pb, m_w_out, m_norm2_g, m_w_ffn_in, m_w_ffn_out, m_final_g, v_c_ctx, v_w_ada, v_b_ada, v_norm1_g, v_w_in, v_na_rpb, v_hg_lb_logits, v_hg_norm_g, v_w_pa, v_w_pb, v_w_out, v_norm2_g, v_w_ffn_in, v_w_ffn_out, v_final_g):
    xi, yi, ci = lax.axis_index("x"), lax.axis_index("y"), lax.axis_index("c")
    sidx = 2 * xi + yi
    eidx = 4 * xi + 2 * yi + ci
    scidx = jnp.stack([sidx, ci]).astype(jnp.int32)

    S, D = x.shape[1], x.shape[2]
    L = ctx.shape[1]
    T = S + L
    naw = NA_HEADS * NA_HEAD_DIM
    hgf = HG_HEADS * HG_DIM
    inw = 3 * naw + 5 * hgf + 2 * D
    fh = w_ffn_out.shape[1] * 4
    ads = w_ada.shape[2]
    fs = hg_lb_logits.shape[2]
    rows = S // GRID_W
    tr = _tile(L, 256)
    nlat, nall = S // tr, T // tr
    assert naw == hgf and D % naw == 0 and S % tr == 0 and 2 * hgf <= D

    pack0 = jnp.concatenate([c, jnp.pad(hg_lb_logits.reshape(1, -1), ((0, 0), (0, D - 4 * fs))),
                             jnp.zeros((6, D), F32)], axis=0)
    g0 = _all_gather8(pack0).reshape(8, 8, D)
    cs = g0[:, 0]
    lbl = g0[::2, 1, :4 * fs].reshape(4, 2, 2, fs).transpose(1, 2, 0, 3).reshape(2, 2, 4 * fs)
    p_lb = jax.nn.softmax(lbl, axis=0)
    lb = p_lb[0]
    lbb = [lb[d].reshape(1, hgf) for d in range(2)]

    cin = jnp.concatenate([cs, c_ctx[None], jnp.zeros((7, D), F32)], axis=0)
    b_sh = lax.dynamic_slice(b_ada, (0, sidx * ads), (1, ads))
    modp = _ada_fwd(cin, w_ada[0], b_sh)
    modfull = _all_gather8(modp).reshape(8, 16, ads)[::2].transpose(1, 0, 2).reshape(16, 4 * ads)
    mod_e = jnp.pad(lax.dynamic_index_in_dim(modfull, eidx, 0, keepdims=False).reshape(N_MOD, D), ((0, 2), (0, 0)))
    mod_c = jnp.pad(modfull[8].reshape(N_MOD, D), ((0, 2), (0, 0)))

    names = ["w_in", "w_pa", "w_pb", "w_out", "w_ffn_in", "w_ffn_out"]
    placed = [_cast_place(scidx[:1], w[0], "cast_" + nm)
              for w, nm in zip((w_in, w_pa, w_pb, w_out, w_ffn_in, w_ffn_out), names)]
    def shards(g):
        return g.reshape(4, 2 * g.shape[2], g.shape[3])

    win3 = shards(_exchange_call(_exchange_gather(placed[:1]), "gather_w_in")[0])

    xx =jnp.concatenate([x[0], ctx[0]], axis=0)

    def f_ln1(i, rv, vv):
        xt, = rv
        g, me, mc = vv
        isc = i >= nlat
        sh = jnp.where(isc, mc[0:1], me[0:1])
        sc = jnp.where(isc, mc[1:2], me[1:2])
        return [xt * _rms(xt) * g * (1.0 + sc) + sh], []

    hb, = _rowwise(f_ln1, nall, tr, [(xx, D, 0, None)], [norm1_g, mod_e, mod_c], [(D, BF16)], [], "ln1")
    u, gathered = _mm_nn(hb, win3, F32, "mm_in", carry=_exchange_gather(placed[1:]))
    wpa3, wpb3, wout3, wi3, wfo3 = [shards(g) for g in gathered]
    wout1 = wout3.reshape(1, D, D)
    wfo1 = wfo3.reshape(1, fh, D)

    scale = NA_HEAD_DIM ** -0.5

    def f_qkv(i, rv, vv):
        q, k, v = rv
        return [q * scale, k, v], []

    qs, kb, vb = _rowwise(f_qkv, nall, tr, [(u, naw, 0, None), (u, naw, 1, None), (u, naw, 2, None)], [],
                          [(naw, BF16)] * 3, [], "qkv_cast")
    bias = _bias_tables(na_rpb[0], rows)
    o_na, lse = _na_fwd(qs, kb, vb, bias, S, L)

    o_f, st_f = _hg_fwd(u, lbb[0], S, L, naw, hgf, False)
    o_b, st_b = _hg_fwd(u, lbb[1], S, L, naw, hgf, True)

    hgn = jnp.tile(hg_norm_g, (1, HG_HEADS))
    hog_cb = (3 * naw + 4 * hgf) // hgf
    ga_cb = (3 * naw + 5 * hgf) // D
    gb_cb = ga_cb + 1

    def heads_rms(o):
        return jnp.concatenate([jnp.broadcast_to(_rms(o[:, h * HG_DIM:(h + 1) * HG_DIM]), (o.shape[0], HG_DIM))
                                for h in range(HG_HEADS)], axis=1)

    def f_readout(i, rv, vv):
        of, ob_, hog = rv
        g, = vv
        o = of + ob_
        return [o * heads_rms(o) * g * _silu(hog)], []

    ob, = _rowwise(f_readout, nlat, tr, [(o_f, hgf, 0, None), (o_b, hgf, 0, None), (u, hgf, hog_cb, None)],
                   [hgn], [(hgf, BF16)], [], "hg_readout")

    ya = _mm_nn(o_na, wpa3, F32, "mm_pa")
    yb = _mm_nn(ob, wpb3, F32, "mm_pb")

    def f_merge(i, rv, vv):
        ya_, yb_, ga, gb = rv
        return [_sigmoid(ga) * ya_ + _sigmoid(gb) * yb_], []

    yv, = _rowwise(f_merge, nlat, tr, [(ya, D, 0, None), (yb, D, 0, None), (u, D, ga_cb, None), (u, D, gb_cb, None)],
                   [], [(D, BF16)], [], "merge")
    z = _mm_nn(yv, wout1, F32, "mm_out")

    def f_res1(i, rv, vv):
        xt, zt = rv
        g, me = vv
        x1 = xt + me[2:3] * zt
        return [x1, x1 * _rms(x1) * g * (1.0 + me[4:5]) + me[3:4]], []

    x1, h2 = _rowwise(f_res1, nlat, tr, [(xx, D, 0, None), (z, D, 0, None)], [norm2_g, mod_e],
                      [(D, F32), (D, BF16)], [], "res1_ln2")
    au = _mm_nn(h2, wi3, F32, "mm_ffn_in")
    tf = _tile(L, 128)
    nlf = S // tf

    def f_swiglu(i, rv, vv):
        a, uu = rv
        return [_silu(a) * uu], []

    sw, = _rowwise(f_swiglu, nlf, tf, [(au, fh, 0, None), (au, fh, 1, None)], [], [(fh, BF16)], [], "swiglu")
    ff = _mm_nn(sw, wfo1, F32, "mm_ffn_out")

    fg = final_g.reshape(1, D)

    def f_final(i, rv, vv):
        x1t, ft, tg = rv
        g, me = vv
        x2 = x1t + me[5:6] * ft
        r3 = _rms(x2)
        xn = x2 * r3
        err = xn * g - tg
        dyy = err * (1.0 / D)
        dxn = dyy * g
        dx2 = r3 * (dxn - xn * jnp.mean(dxn * xn, axis=-1, keepdims=True))
        return [dx2, dx2 * me[5:6]], [_colsum(err * err), _colsum(dyy * xn), _colsum(dx2 * ft)]

    dx2, dfb, loss_cols, dfg, dg2 = _rowwise(
        f_final, nlat, tr, [(x1, D, 0, None), (ff, D, 0, None), (loss_target[0], D, 0, None)], [fg, mod_e],
        [(D, F32), (D, BF16)], [D, D, D], "final_loss")

    dsw = _mm_nt(dfb, wfo1, BF16, "mm_d_sw")

    def f_dswiglu(i, rv, vv):
        d, a, uu = rv
        return [jnp.concatenate([d * uu * _dsilu(a), d * _silu(a)], axis=1)], []

    dau, = _rowwise(f_dswiglu, nlf, tf, [(dsw, fh, 0, None), (au, fh, 0, None), (au, fh, 1, None)], [],
                    [(2 * fh, BF16)], [], "swiglu_bwd")
    g_wfo = _mm_tn(sw, dfb, 1, "mm_dw_ffn_out").reshape(4, 2, fh // 8, D)
    dh2 = _mm_nt(dau, wi3, F32, "mm_d_h2")
    g_wi = _mm_tn(h2, dau, 4, "mm_dw_ffn_in")

    def f_ln2_bwd(i, rv, vv):
        dh, x1t, dx2t, zt = rv
        g, me = vv
        r2 = _rms(x1t)
        xn = x1t * r2
        dxn = dh * g * (1.0 + me[4:5])
        dx1 = dx2t + r2 * (dxn - xn * jnp.mean(dxn * xn, axis=-1, keepdims=True))
        return ([dx1, dx1 * me[2:3]],
                [_colsum(dh), _colsum(dh * xn * g), _colsum(dh * xn * (1.0 + me[4:5])), _colsum(dx1 * zt)])

    dx1, dzb, dsh2, dsc2, dn2g, dg1 = _rowwise(
        f_ln2_bwd, nlat, tr, [(dh2, D, 0, None), (x1, D, 0, None), (dx2, D, 0, None), (z, D, 0, None)],
        [norm2_g, mod_e], [(D, F32), (D, BF16)], [D, D, D, D], "ln2_bwd")

    dy = _mm_nt(dzb, wout1, F32, "mm_d_y")
    g_wout = _mm_tn(yv, dzb, 1, "mm_dw_out").reshape(4, 2, D // 8, D)

    def f_dmerge(i, rv, vv):
        d, ya_, yb_, ga, gb = rv
        sa, sb_ = _sigmoid(ga), _sigmoid(gb)
        return [d * sa, d * sb_, d * ya_ * sa * (1.0 - sa), d * yb_ * sb_ * (1.0 - sb_)], []

    dya, dyb, dga, dgb = _rowwise(
        f_dmerge, nlat, tr, [(dy, D, 0, None), (ya, D, 0, None), (yb, D, 0, None), (u, D, ga_cb, None),
                             (u, D, gb_cb, None)], [], [(D, BF16)] * 4, [], "merge_bwd")
    d_ona = _mm_nt(dya, wpa3, BF16, "mm_d_ona")
    d_ob = _mm_nt(dyb, wpb3, F32, "mm_d_ob")
    g_wpa = _mm_tn(o_na, dya, 4, "mm_dw_pa")
    g_wpb = _mm_tn(ob, dyb, 4, "mm_dw_pb")

    def f_dreadout(i, rv, vv):
        d, of, ob_, hog = rv
        g, = vv
        o = of + ob_
        on = o * heads_rms(o)
        t = d * _silu(hog) * g
        mt = jnp.concatenate([jnp.broadcast_to(jnp.mean((t * on)[:, h * HG_DIM:(h + 1) * HG_DIM], axis=-1,
                                                        keepdims=True), (o.shape[0], HG_DIM))
                              for h in range(HG_HEADS)], axis=1)
        do_ = heads_rms(o) * (t - on * mt)
        return [do_, d * on * g * _dsilu(hog)], [_colsum(d * _silu(hog) * on)]

    do_hg, dhog, dhgn = _rowwise(
        f_dreadout, nlat, tr, [(d_ob, hgf, 0, None), (o_f, hgf, 0, None), (o_b, hgf, 0, None),
                               (u, hgf, hog_cb, None)], [hgn], [(hgf, F32), (hgf, BF16)], [hgf], "hg_readout_bwd")

    dq_f, dz_f, dv_f, dlb_f = _hg_bwd(u, lbb[0], st_f, do_hg, S, L, naw, hgf, False)
    dq_b, dz_b, dv_b, dlb_b = _hg_bwd(u, lbb[1], st_b, do_hg, S, L, naw, hgf, True)
    dq_na, dk_na, dv_na, dbias = _na_bwd(qs, kb, vb, bias, d_ona, o_na, lse, S, L)

    ta = _tile(L, 128)
    nla, naa = S // ta, T // ta
    lat = lambda i: jnp.minimum(i, nla - 1)

    def f_assemble(i, rv, vv):
        dqn, dk, dv, dqf, dqb, dzf, dzb_, dvf, dvb, dho, dga_, dgb_ = rv
        keep = (i < nla).astype(F32)
        return [jnp.concatenate([dqn * (scale * keep), dk, dv, dqf + dqb, dzf, dzb_, dvf + dvb,
                                 dho.astype(F32) * keep, dga_.astype(F32) * keep, dgb_.astype(F32) * keep],
                                axis=1)], []

    du, = _rowwise(
        f_assemble, naa, ta,
        [(dq_na, naw, 0, lat), (dk_na, naw, 0, None), (dv_na, naw, 0, None), (dq_f, hgf, 0, None),
         (dq_b, hgf, 0, None), (dz_f, hgf, 0, None), (dz_b, hgf, 0, None), (dv_f, hgf, 0, None),
         (dv_b, hgf, 0, None), (dhog, hgf, 0, lat), (dga, D, 0, lat), (dgb, D, 0, lat)],
        [], [(inw, BF16)], [], "assemble_du")

    def add_half(g, land, nm):
        return _add_own_half(scidx[1:], g, land, "rs_add_" + nm)

    early = [g_wpa, g_wpb, g_wout, g_wi, g_wfo]
    g_win, lands = _mm_tn(hb, du, 4, "mm_dw_in", carry=_exchange_swap_other_half(early))
    parts = [add_half(g, l, nm) for g, l, nm in zip(early, lands, names[1:])]
    dh, landed = _mm_nt(du, win3, F32, "mm_d_h", carry=_exchange_join(
        _exchange_scatter(parts), _exchange_swap_other_half([g_win])))
    parts = [add_half(g_win, landed[5], names[0])] + parts
    pieces = _exchange_call(_exchange_scatter(parts[:1]), "rs_scatter_w_in") + landed[:5]
    halves = [_sum_pieces(scidx, p, l, "rs_sum_" + nm) for p, l, nm in zip(parts, pieces, names)]
    g_win, g_wpa, g_wpb, g_wout, g_wi, g_wfo = [
        f.reshape(2 * f.shape[1], f.shape[2])
        for f in _exchange_call(_exchange_swap_result(halves), "rs_swap_result_half")]

    def f_ln1_bwd(i, rv, vv):
        dht, xt, dx1t = rv
        g, me = vv
        r1 = _rms(xt)
        xn = xt * r1
        dxn = dht * g * (1.0 + me[1:2])
        dx = dx1t + r1 * (dxn - xn * jnp.mean(dxn * xn, axis=-1, keepdims=True))
        return [dx], [_colsum(dht), _colsum(dht * xn * g), _colsum(dht * xn * (1.0 + me[1:2]))]

    grad_x, dsh1, dsc1, dn1g_l = _rowwise(
        f_ln1_bwd, nlat, tr, [(dh, D, 0, None), (xx, D, 0, None), (dx1, D, 0, None)], [norm1_g, mod_e],
        [(D, F32)], [D, D, D], "ln1_bwd")

    def f_ln1_bwd_ctx(i, rv, vv):
        dht, xt = rv
        g, mc = vv
        xn = xt * _rms(xt)
        return [], [_colsum(dht), _colsum(dht * xn * g), _colsum(dht * xn * (1.0 + mc[1:2]))]

    ctx_rows = lambda i: i + nlat
    dsh1c, dsc1c, dn1g_c = _rowwise(
        f_ln1_bwd_ctx, nall - nlat, tr, [(dh, D, 0, ctx_rows), (xx, D, 0, ctx_rows)], [norm1_g, mod_c],
        [], [D, D, D], "ln1_bwd_ctx")

    drpb = _bias_tables_transpose(dbias, rows).reshape(1, -1)
    nrp = -(-drpb.shape[1] // D)
    drpb_rows = jnp.pad(drpb, ((0, 0), (0, nrp * D - drpb.shape[1]))).reshape(nrp, D)
    dlb = jnp.concatenate([dlb_f, dlb_b], axis=1)
    dhg = jnp.sum(dhgn.reshape(HG_HEADS, HG_DIM), axis=0, keepdims=True)

    def wide(v):
        return jnp.pad(v, ((0, 0), (0, D - v.shape[1])))

    pack_rows = [loss_cols, dfg, dn2g, dn1g_l + dn1g_c, dsh1, dsc1, dg1, dsh2, dsc2, dg2, dsh1c, dsc1c,
                 wide(dhg), wide(dlb), drpb_rows]
    pack = jnp.concatenate(pack_rows, axis=0)
    npk = -(-pack.shape[0] // 8) * 8
    pack = jnp.pad(pack, ((0, npk - pack.shape[0]), (0, 0)))
    gp = _all_gather8(pack).reshape(8, npk, D)
    tot = _sum8(gp, "sum_small_grads")

    loss = (0.5 / D) * jnp.sum(tot[0])
    grad_final_g = tot[1]
    grad_norm2_g = tot[2:3]
    grad_norm1_g = tot[3:4]
    grad_hg_norm_g = tot[12:13, :HG_DIM]
    dlb_tot = tot[13, :2 * hgf].reshape(2, hgf)
    grad_na_rpb = tot[14:14 + nrp].reshape(-1)[:drpb.shape[1]].reshape(na_rpb.shape)
    dlog = jnp.stack([dlb_tot * p_lb[0] * (1.0 - p_lb[0]), -dlb_tot * p_lb[0] * p_lb[1]], axis=0)
    grad_hg_lb = lax.dynamic_slice(dlog, (0, 0, sidx * fs), (2, 2, fs))

    dmod_all = gp[:, 4:10].reshape(8, N_MOD * D)
    dmod_ctx = jnp.concatenate([tot[10], tot[11], jnp.zeros((4 * D,), F32)])[None]
    dm16 = jnp.concatenate([dmod_all, dmod_ctx, jnp.zeros((7, N_MOD * D), F32)], axis=0)
    grad_b_ada = jnp.sum(dm16, axis=0, keepdims=True)
    dm_sh = lax.dynamic_slice(dm16, (0, sidx * ads), (16, ads))
    g_wada, dcin = _ada_bwd(cin, w_ada[0], dm_sh)
    gc = _all_gather8(dcin[8:16]).reshape(8, 8, D)
    grad_c_ctx = (gc[0, 0] + gc[2, 0] + gc[4, 0] + gc[6, 0]) * _dsilu(c_ctx)

    grads = {
        "c_ctx": grad_c_ctx, "w_ada": g_wada[None], "b_ada": grad_b_ada, "norm1_g": grad_norm1_g,
        "w_in": g_win[None], "na_rpb": grad_na_rpb, "hg_lb_logits": grad_hg_lb, "hg_norm_g": grad_hg_norm_g,
        "w_pa": g_wpa[None], "w_pb": g_wpb[None], "w_out": g_wout[None], "norm2_g": grad_norm2_g,
        "w_ffn_in": g_wi[None], "w_ffn_out": g_wfo[None], "final_g": grad_final_g,
    }
    weights = {
        "c_ctx": (c_ctx, m_c_ctx, v_c_ctx), "w_ada": (w_ada, m_w_ada, v_w_ada), "b_ada": (b_ada, m_b_ada, v_b_ada),
        "norm1_g": (norm1_g, m_norm1_g, v_norm1_g), "w_in": (w_in, m_w_in, v_w_in),
        "na_rpb": (na_rpb, m_na_rpb, v_na_rpb), "hg_lb_logits": (hg_lb_logits, m_hg_lb_logits, v_hg_lb_logits),
        "hg_norm_g": (hg_norm_g, m_hg_norm_g, v_hg_norm_g), "w_pa": (w_pa, m_w_pa, v_w_pa),
        "w_pb": (w_pb, m_w_pb, v_w_pb), "w_out": (w_out, m_w_out, v_w_out),
        "norm2_g": (norm2_g, m_norm2_g, v_norm2_g), "w_ffn_in": (w_ffn_in, m_w_ffn_in, v_w_ffn_in),
        "w_ffn_out": (w_ffn_out, m_w_ffn_out, v_w_ffn_out), "final_g": (final_g, m_final_g, v_final_g),
    }
    order = list(weights)
    deltas, new_ms, new_vs = [], [], []
    for nm in order:
        w, m, v = weights[nm]
        g = grads[nm].reshape(w.shape)
        grads[nm] = g
        if w.ndim == 3 and w.shape[0] == 1:
            d_, m_, v_ = _adamw(w[0], g[0], m[0], v[0], "adamw_" + nm)
            d_, m_, v_ = d_[None], m_[None], v_[None]
        else:
            d_, m_, v_ = _adamw(w, g, m, v, "adamw_" + nm)
        deltas.append(d_)
        new_ms.append(m_)
        new_vs.append(v_)

    return (loss, grad_x[None], *[grads[nm] for nm in order], *deltas, *new_ms, *new_vs)
```

```python
import numpy as np

import jax
import jax.numpy as jnp
from jax import lax
from jax.experimental import pallas as pl
from jax.experimental.pallas import tpu as pltpu

F32 = jnp.float32
BF16 = jnp.bfloat16

GRID_W = 64
WIN_H = 8
WIN_W = 16
NA_HEADS = 16
NA_HEAD_DIM = 64
HG_HEADS = 8
HG_DIM = 128
HG_CHUNK = 64
N_MOD = 6
EPS = 1e-6
ADAM_LR = 0.001
ADAM_B1 = 0.9
ADAM_B2 = 0.999
ADAM_EPS = 1e-08
ADAM_WD = 0.01
ADAM_STEP = 10

LANES = 128
NA_ROWS_PER_STEP = 8
VMEM_LIMIT = 56 * 1024 * 1024
MASK_VALUE = -1e30
EXP_CLAMP = 80.0
MESH_ID = pl.DeviceIdType.MESH
HI = lax.Precision.HIGHEST


def _tile(dim, target, mult=LANES):
    best = None
    t = mult
    while t <= min(dim, target):
        if dim % t == 0:
            best = t
        t += mult
    assert best is not None, (dim, target, mult)
    return best


def _params(sem):
    return pltpu.CompilerParams(dimension_semantics=sem, vmem_limit_bytes=VMEM_LIMIT)


def _dot(a, b, precision=None):
    return jnp.dot(a, b, preferred_element_type=F32, precision=precision)


def _dot_nt(a, b, precision=None):
    return lax.dot_general(a, b, (((1,), (1,)), ((), ())), preferred_element_type=F32, precision=precision)


def _dot_tn(a, b, precision=None):
    return lax.dot_general(a, b, (((0,), (0,)), ((), ())), preferred_element_type=F32, precision=precision)


def _split2(v):
    hi = v.astype(BF16)
    return hi, (v - hi.astype(F32)).astype(BF16)


def _dot_x3(dot, a2, b2):
    return dot(a2[0], b2[0]) + (dot(a2[0], b2[1]) + dot(a2[1], b2[0]))


def _sigmoid(v):
    return 1.0 / (1.0 + jnp.exp(-v))


def _mm_call(dot, operands, grid, in_specs, out_spec, out_shape, acc_shape, name, carry=None, epi=None):
    nk = grid[2]
    nci = 0 if carry is None else len(carry["ins"])
    nco = 0 if carry is None else len(carry["outs"])
    nei = 0 if epi is None else len(epi["ins"])
    neo = 1 if epi is None else len(epi["outs"])

    def body(*refs):
        a_ref, b_ref = refs[:2]
        ein = refs[2:2 + nei]
        cin = refs[2 + nei:2 + nei + nci]
        outs = refs[2 + nei + nci:2 + nei + nci + neo]
        cout = refs[2 + nei + nci + neo:2 + nei + nci + neo + nco]
        acc = refs[2 + nei + nci + neo + nco]
        sems = refs[3 + nei + nci + neo + nco:]
        m, n, k = pl.program_id(0), pl.program_id(1), pl.program_id(2)

        if carry is not None:
            @pl.when((m == 0) & (n == 0) & (k == 0))
            def _():
                carry["start"](cin, cout, *sems)

        @pl.when(k == 0)
        def _():
            acc[...] = jnp.zeros_like(acc)

        acc[...] += dot(a_ref[...], b_ref[...])

        @pl.when(k == nk - 1)
        def _():
            vals = [acc[...]] if epi is None else epi["fn"](acc[...], [r[...] for r in ein])
            for r, v in zip(outs, vals):
                if isinstance(v, tuple):
                    for i, vi in enumerate(v):
                        r[i] = vi.astype(r.dtype)
                else:
                    r[...] = v.astype(r.dtype)

        if carry is not None:
            @pl.when((m == grid[0] - 1) & (n == grid[1] - 1) & (k == nk - 1))
            def _():
                carry["finish"](cin, cout, *sems)

    any_spec = pl.BlockSpec(memory_space=pl.ANY)
    scratch = [pltpu.VMEM(acc_shape, F32)]
    extra = {}
    if carry is not None:
        scratch += [pltpu.SemaphoreType.DMA((carry["nsem"],)), pltpu.SemaphoreType.DMA((carry["nsem"],))]
        extra["input_output_aliases"] = {2 + nei + i: neo + j for i, j in carry["alias"].items()}
    sem = ("arbitrary",) * 3 if carry is not None else ("parallel", "parallel", "arbitrary")
    main_outs = [(out_shape, out_spec)] if epi is None else list(epi["outs"])
    res = pl.pallas_call(
        body, name=name, grid=grid,
        in_specs=list(in_specs) + ([] if epi is None else [sp for _, sp in epi["ins"]]) + [any_spec] * nci,
        out_specs=[sp for _, sp in main_outs] + [any_spec] * nco,
        out_shape=[sh for sh, _ in main_outs] + ([] if carry is None else list(carry["outs"])),
        scratch_shapes=scratch, compiler_params=_params(sem), **extra,
    )(*operands, *([] if epi is None else [ar for ar, _ in epi["ins"]]), *([] if carry is None else carry["ins"]))
    res = list(res)
    main = res[0] if epi is None else res[:neo]
    return main if carry is None else (main, res[neo:])


def _mm_nn(a, b3, out_dtype, name, carry=None):
    M, K = a.shape
    nsh, _, Ns = b3.shape
    tm, tn, tk = _tile(M, 1024), _tile(Ns, 1408), _tile(K, 2048)
    tps, nk = Ns // tn, K // tk
    return _mm_call(
        _dot, (a, b3), (M // tm, nsh * tps, nk),
        [pl.BlockSpec((tm, tk), lambda m, n, k: (m, k)),
         pl.BlockSpec((None, tk, tn), lambda m, n, k: (n // tps, k, n % tps))],
        pl.BlockSpec((tm, tn), lambda m, n, k: (m, n)),
        jax.ShapeDtypeStruct((M, nsh * Ns), out_dtype), (tm, tn), name, carry)


def _mm_nt(a, b3, out_dtype, name, carry=None, epi=None, tn_target=1408):
    a3 = a if a.ndim == 3 else a[None]
    na, M, Ka = a3.shape
    nsh, Kw, Ns = b3.shape
    assert na * Ka == nsh * Ns
    tm, tn, tk = _tile(M, 1024), _tile(Kw, tn_target), _tile(int(np.gcd(Ka, Ns)), 2048)
    kpa, kps = Ka // tk, Ns // tk
    return _mm_call(
        _dot_nt, (a3, b3), (M // tm, Kw // tn, nsh * kps),
        [pl.BlockSpec((None, tm, tk), lambda m, n, k: (k // kpa, m, k % kpa)),
         pl.BlockSpec((None, tn, tk), lambda m, n, k: (k // kps, n, k % kps))],
        pl.BlockSpec((tm, tn), lambda m, n, k: (m, n)),
        jax.ShapeDtypeStruct((M, Kw), out_dtype), (tm, tn), name, carry,
        None if epi is None else epi(tm, tn))


def _mm_tn(a, g, nsh, name, carry=None):
    Tk, M = a.shape
    g3 = g if g.ndim == 3 else g[None]
    ng, _, Ng = g3.shape
    Ns = ng * Ng // nsh
    tm, tn, tk = _tile(M // 2, 1408), _tile(int(np.gcd(Ng, Ns)), 1408), _tile(Tk, 1024)
    mh, tps, tpg, nk = (M // 2) // tm, Ns // tn, Ng // tn, Tk // tk
    return _mm_call(
        _dot_tn, (a, g3), (M // tm, nsh * tps, nk),
        [pl.BlockSpec((tk, tm), lambda m, n, k: (k, m)),
         pl.BlockSpec((None, tk, tn), lambda m, n, k: (n // tpg, k, n % tpg))],
        pl.BlockSpec((None, None, tm, tn), lambda m, n, k: (n // tps, m // mh, m % mh, n % tps)),
        jax.ShapeDtypeStruct((nsh, 2, M // 2, Ns), F32), (tm, tn), name, carry)


def _ffn_in_fused(h2, wi3, name):
    M, K = h2.shape
    _, _, Ns = wi3.shape
    fh = 2 * Ns
    tm, tn = _tile(M, 1024), _tile(Ns, 512)
    tps = Ns // tn

    def body(h_ref, ba_ref, bu_ref, au_ref, sw_ref):
        h = h_ref[...]
        a, u = _dot(h, ba_ref[...]), _dot(h, bu_ref[...])
        au_ref[0] = a
        au_ref[1] = u
        sw_ref[...] = (_silu(a) * u).astype(sw_ref.dtype)

    return pl.pallas_call(
        body, name=name, grid=(M // tm, fh // tn),
        in_specs=[pl.BlockSpec((tm, K), lambda m, n: (m, 0)),
                  pl.BlockSpec((None, K, tn), lambda m, n: (n // tps, 0, n % tps)),
                  pl.BlockSpec((None, K, tn), lambda m, n: (2 + n // tps, 0, n % tps))],
        out_specs=[pl.BlockSpec((2, tm, tn), lambda m, n: (0, m, n)), pl.BlockSpec((tm, tn), lambda m, n: (m, n))],
        out_shape=[jax.ShapeDtypeStruct((2, M, fh), F32), jax.ShapeDtypeStruct((M, fh), BF16)],
        compiler_params=_params(("parallel", "parallel")),
    )(h2, wi3, wi3)


def _rowwise(fn, nblk, tm, rins, vins, routs, accs, name):
    nr, nv, no, na = len(rins), len(vins), len(routs), len(accs)

    def body(*refs):
        i = pl.program_id(0)
        outs, accv = fn(i, [r[...] for r in refs[:nr]], [r[...] for r in refs[nr:nr + nv]])
        for r, v in zip(refs[nr + nv:nr + nv + no], outs):
            r[...] = v.astype(r.dtype)
        arefs = refs[nr + nv + no:]
        if na:
            @pl.when(i == 0)
            def _():
                for a in arefs:
                    a[...] = jnp.zeros_like(a)

            for a, v in zip(arefs, accv):
                a[...] += v

    def row_spec(w, cb, rm):
        if rm is None:
            return pl.BlockSpec((tm, w), lambda i: (i, cb))
        return pl.BlockSpec((tm, w), lambda i: (rm(i), cb))

    in_specs = [row_spec(w, cb, rm) for (_, w, cb, rm) in rins]
    in_specs += [pl.BlockSpec(v.shape, lambda i: (0, 0)) for v in vins]
    out_specs = [pl.BlockSpec((tm, w), lambda i: (i, 0)) for (w, _) in routs]
    out_specs += [pl.BlockSpec((1, w), lambda i: (0, 0)) for w in accs]
    out_shape = [jax.ShapeDtypeStruct((nblk * tm, w), dt) for (w, dt) in routs]
    out_shape += [jax.ShapeDtypeStruct((1, w), F32) for w in accs]
    res = pl.pallas_call(
        body, name=name, grid=(nblk,), in_specs=in_specs, out_specs=out_specs, out_shape=out_shape,
        compiler_params=_params(("arbitrary",)),
    )(*[r[0] for r in rins], *vins)
    return list(res)


def _colsum(v):
    return jnp.sum(v, axis=0, keepdims=True)


def _rms(v):
    return lax.rsqrt(jnp.mean(v * v, axis=-1, keepdims=True) + EPS)


def _all_gather8(xs):
    m_per, n = xs.shape

    def body(x_ref, out_ref, send_sems, recv_sems, local_sem):
        x, y, c = lax.axis_index("x"), lax.axis_index("y"), lax.axis_index("c")
        me, sibling = (x, y, c), (x, y, 1 - c)
        chips = [(1 - x, y), (x, 1 - y), (1 - x, 1 - y)]

        def rows(px, py, pc):
            return out_ref.at[pl.ds((4 * px + 2 * py + pc) * m_per, m_per), :]

        def copy(k, block, to, src=None):
            return pltpu.make_async_remote_copy(
                src_ref=rows(*block) if src is None else src, dst_ref=rows(*block),
                send_sem=send_sems.at[k], recv_sem=recv_sems.at[k], device_id=to, device_id_type=MESH_ID)

        mine = pltpu.make_async_copy(x_ref, rows(*me), local_sem)
        mine.start()
        first = [copy(0, me, sibling, src=x_ref)]
        first += [copy(1 + j, me, (*chip, c), src=x_ref) for j, chip in enumerate(chips)]
        for cp in first:
            cp.start()
        passed = [copy(4 + j, (*chip, c), sibling) for j, chip in enumerate(chips)]
        for j, chip in enumerate(chips):
            copy(1 + j, (*chip, c), me).wait_recv()
            passed[j].start()
        copy(0, sibling, me).wait_recv()
        for j, chip in enumerate(chips):
            copy(4 + j, (*chip, 1 - c), me).wait_recv()
        for cp in first + passed:
            cp.wait_send()
        mine.wait()

    return pl.pallas_call(
        body, name="all_gather8_%dx%d" % (m_per, n),
        out_shape=jax.ShapeDtypeStruct((8 * m_per, n), xs.dtype),
        in_specs=[pl.BlockSpec(memory_space=pltpu.VMEM)],
        out_specs=pl.BlockSpec(memory_space=pltpu.VMEM),
        scratch_shapes=[pltpu.SemaphoreType.DMA((7,)), pltpu.SemaphoreType.DMA((7,)), pltpu.SemaphoreType.DMA],
    )(xs)


def _mesh_pos():
    x, y, c = lax.axis_index("x"), lax.axis_index("y"), lax.axis_index("c")
    chips = [(1 - x, y), (x, 1 - y), (1 - x, 1 - y)]
    return x, y, c, chips


def _my_shard():
    return 2 * lax.axis_index("x") + lax.axis_index("y")


def _cast_place(w, name):
    r, cw = w.shape
    rh = r // 2
    tm = _tile(rh, max(16, (1 << 20) // (4 * cw)), 16)
    nt = rh // tm

    def body(w_ref, o_ref):
        o_ref[...] = w_ref[...].astype(o_ref.dtype)

    return pl.pallas_call(
        body, name=name, grid=(2, nt),
        in_specs=[pl.BlockSpec((tm, cw), lambda h, i: (h * nt + i, 0))],
        out_specs=pl.BlockSpec((None, None, tm, cw), lambda h, i: (_my_shard(), h, i, 0)),
        out_shape=jax.ShapeDtypeStruct((4, 2, rh, cw), BF16),
        compiler_params=_params(("parallel", "parallel")),
    )(w)


def _exchange_gather(bufs):
    n = len(bufs)

    def copies(out, send_sems, recv_sems, base):
        def copy(i, k, shard, half, to):
            dst = out[i].at[shard, half]
            return pltpu.make_async_remote_copy(
                src_ref=dst, dst_ref=dst, send_sem=send_sems.at[base + 6 * i + k],
                recv_sem=recv_sems.at[base + 6 * i + k], device_id=to, device_id_type=MESH_ID)
        return copy

    def first(copy):
        x, y, c, chips = _mesh_pos()
        return [copy(i, j, 2 * x + y, c, (*chip, c)) for i in range(n) for j, chip in enumerate(chips)]

    def start(cin, out, send_sems, recv_sems, base=0):
        for cp in first(copies(out, send_sems, recv_sems, base)):
            cp.start()

    def finish(cin, out, send_sems, recv_sems, base=0):
        copy = copies(out, send_sems, recv_sems, base)
        x, y, c, chips = _mesh_pos()
        passed = []
        for j, chip in enumerate(chips):
            sj = 2 * chip[0] + chip[1]
            for i in range(n):
                copy(i, j, sj, c, (x, y, c)).wait_recv()
                cp = copy(i, 3 + j, sj, c, (x, y, 1 - c))
                cp.start()
                passed.append(cp)
        for j, chip in enumerate(chips):
            sj = 2 * chip[0] + chip[1]
            for i in range(n):
                copy(i, 3 + j, sj, 1 - c, (x, y, c)).wait_recv()
        for cp in first(copy) + passed:
            cp.wait_send()

    return dict(ins=list(bufs), outs=[jax.ShapeDtypeStruct(b.shape, b.dtype) for b in bufs],
                alias={i: i for i in range(n)}, nsem=6 * n, start=start, finish=finish)


def _exchange_join(a, b):
    nai, nao = len(a["ins"]), len(a["outs"])

    def start(cin, cout, send_sems, recv_sems, base=0):
        a["start"](cin[:nai], cout[:nao], send_sems, recv_sems, base)
        b["start"](cin[nai:], cout[nao:], send_sems, recv_sems, base + a["nsem"])

    def finish(cin, cout, send_sems, recv_sems, base=0):
        a["finish"](cin[:nai], cout[:nao], send_sems, recv_sems, base)
        b["finish"](cin[nai:], cout[nao:], send_sems, recv_sems, base + a["nsem"])

    alias = dict(a["alias"])
    alias.update({nai + i: nao + j for i, j in b["alias"].items()})
    return dict(ins=a["ins"] + b["ins"], outs=a["outs"] + b["outs"], alias=alias, nsem=a["nsem"] + b["nsem"],
                start=start, finish=finish)


def _exchange_call(ex, name):
    nci, nco = len(ex["ins"]), len(ex["outs"])

    def body(*refs):
        cin, cout, sems = refs[:nci], refs[nci:nci + nco], refs[nci + nco:]
        ex["start"](cin, cout, *sems)
        ex["finish"](cin, cout, *sems)

    any_spec = pl.BlockSpec(memory_space=pl.ANY)
    return list(pl.pallas_call(
        body, name=name, out_shape=list(ex["outs"]), in_specs=[any_spec] * nci, out_specs=[any_spec] * nco,
        input_output_aliases=dict(ex["alias"]),
        scratch_shapes=[pltpu.SemaphoreType.DMA((ex["nsem"],)), pltpu.SemaphoreType.DMA((ex["nsem"],))],
    )(*ex["ins"]))


def _exchange_swap_other_half(gs):
    n = len(gs)

    def copies(g, land, send_sems, recv_sems, base):
        x, y, c, _ = _mesh_pos()
        return [pltpu.make_async_remote_copy(
            src_ref=g[i].at[:, 1 - c], dst_ref=land[i], send_sem=send_sems.at[base + i],
            recv_sem=recv_sems.at[base + i], device_id=(x, y, 1 - c), device_id_type=MESH_ID) for i in range(n)]

    def start(g, land, send_sems, recv_sems, base=0):
        for cp in copies(g, land, send_sems, recv_sems, base):
            cp.start()

    def finish(g, land, send_sems, recv_sems, base=0):
        for cp in copies(g, land, send_sems, recv_sems, base):
            cp.wait()

    return dict(ins=list(gs), outs=[jax.ShapeDtypeStruct((4,) + g.shape[2:], g.dtype) for g in gs],
                alias={}, nsem=n, start=start, finish=finish)


def _exchange_scatter(ps):
    n = len(ps)

    def copies(p, land, send_sems, recv_sems, base):
        x, y, c, chips = _mesh_pos()
        return [pltpu.make_async_remote_copy(
            src_ref=p[i].at[2 * chip[0] + chip[1]], dst_ref=land[i].at[j],
            send_sem=send_sems.at[base + 3 * i + j], recv_sem=recv_sems.at[base + 3 * i + j],
            device_id=(*chip, c), device_id_type=MESH_ID) for i in range(n) for j, chip in enumerate(chips)]

    def start(p, land, send_sems, recv_sems, base=0):
        for cp in copies(p, land, send_sems, recv_sems, base):
            cp.start()

    def finish(p, land, send_sems, recv_sems, base=0):
        for cp in copies(p, land, send_sems, recv_sems, base):
            cp.wait()

    return dict(ins=list(ps), outs=[jax.ShapeDtypeStruct((3,) + p.shape[1:], p.dtype) for p in ps],
                alias={}, nsem=3 * n, start=start, finish=finish)


def _exchange_swap_result(bufs):
    n = len(bufs)

    def copies(out, send_sems, recv_sems, base, half):
        x, y, c, _ = _mesh_pos()
        h = c if half == "mine" else 1 - c
        return [pltpu.make_async_remote_copy(
            src_ref=out[i].at[h], dst_ref=out[i].at[h], send_sem=send_sems.at[base + i],
            recv_sem=recv_sems.at[base + i], device_id=(x, y, 1 - c), device_id_type=MESH_ID) for i in range(n)]

    def start(cin, out, send_sems, recv_sems, base=0):
        for cp in copies(out, send_sems, recv_sems, base, "mine"):
            cp.start()

    def finish(cin, out, send_sems, recv_sems, base=0):
        for cp in copies(out, send_sems, recv_sems, base, "theirs"):
            cp.wait_recv()
        for cp in copies(out, send_sems, recv_sems, base, "mine"):
            cp.wait_send()

    return dict(ins=list(bufs), outs=[jax.ShapeDtypeStruct(b.shape, b.dtype) for b in bufs],
                alias={i: i for i in range(n)}, nsem=n, start=start, finish=finish)


def _add_own_half(g, land, name):
    _, _, rh, cw = g.shape
    tm = _tile(rh, max(16, (1 << 20) // (4 * cw)), 16)

    def body(g_ref, l_ref, o_ref):
        o_ref[...] = (g_ref[...] + l_ref[...]).astype(o_ref.dtype)

    return pl.pallas_call(
        body, name=name, grid=(4, rh // tm),
        in_specs=[pl.BlockSpec((None, None, tm, cw), lambda s, i: (s, lax.axis_index("c"), i, 0)),
                  pl.BlockSpec((None, tm, cw), lambda s, i: (s, i, 0))],
        out_specs=pl.BlockSpec((None, tm, cw), lambda s, i: (s, i, 0)),
        out_shape=jax.ShapeDtypeStruct((4, rh, cw), BF16),
        compiler_params=_params(("parallel", "parallel")),
    )(g, land)


def _sum_pieces(part, land, name):
    _, rh, cw = land.shape
    tm = _tile(rh, max(16, (1 << 20) // (4 * cw)), 16)

    def body(p_ref, l_ref, o_ref):
        v = l_ref[...].astype(F32)
        o_ref[...] = (p_ref[...].astype(F32) + v[0]) + (v[1] + v[2])

    return pl.pallas_call(
        body, name=name, grid=(rh // tm,),
        in_specs=[pl.BlockSpec((None, tm, cw), lambda i: (_my_shard(), i, 0)),
                  pl.BlockSpec((3, tm, cw), lambda i: (0, i, 0))],
        out_specs=pl.BlockSpec((None, tm, cw), lambda i: (lax.axis_index("c"), i, 0)),
        out_shape=jax.ShapeDtypeStruct((2, rh, cw), F32),
        compiler_params=_params(("parallel",)),
    )(part, land)


def _sum8(g, name):
    def body(g_ref, o_ref):
        acc = g_ref[0]
        for k in range(1, 8):
            acc = acc + g_ref[k]
        o_ref[...] = acc

    return pl.pallas_call(body, name=name, out_shape=jax.ShapeDtypeStruct(g.shape[1:], F32))(g)


def _silu(v):
    return v * _sigmoid(v)


def _dsilu(v):
    s = _sigmoid(v)
    return s * (1.0 + v * (1.0 - s))


def _ada_fwd(cin, w, b):
    d, ns = w.shape
    tn = _tile(ns, 512)

    def body(c_ref, w_ref, b_ref, o_ref):
        o_ref[...] = _dot(_silu(c_ref[...]), w_ref[...], HI) + b_ref[...]

    return pl.pallas_call(
        body, name="ada_fwd", grid=(ns // tn,),
        in_specs=[pl.BlockSpec(cin.shape, lambda n: (0, 0)), pl.BlockSpec((d, tn), lambda n: (0, n)),
                  pl.BlockSpec((1, tn), lambda n: (0, n))],
        out_specs=pl.BlockSpec((cin.shape[0], tn), lambda n: (0, n)),
        out_shape=jax.ShapeDtypeStruct((cin.shape[0], ns), F32),
        compiler_params=_params(("parallel",)),
    )(cin, w, b)


def _ada_bwd(cin, w, dm):
    d, ns = w.shape
    tn = _tile(ns, 512)

    def body(c_ref, w_ref, d_ref, dw_ref, dc_ref):
        n = pl.program_id(0)

        @pl.when(n == 0)
        def _():
            dc_ref[...] = jnp.zeros_like(dc_ref)

        dw_ref[...] = _dot_tn(_silu(c_ref[...]), d_ref[...], HI)
        dc_ref[...] += _dot_nt(d_ref[...], w_ref[...], HI)

    return pl.pallas_call(
        body, name="ada_bwd", grid=(ns // tn,),
        in_specs=[pl.BlockSpec(cin.shape, lambda n: (0, 0)), pl.BlockSpec((d, tn), lambda n: (0, n)),
                  pl.BlockSpec((cin.shape[0], tn), lambda n: (0, n))],
        out_specs=[pl.BlockSpec((d, tn), lambda n: (0, n)), pl.BlockSpec(cin.shape, lambda n: (0, 0))],
        out_shape=[jax.ShapeDtypeStruct((d, ns), F32), jax.ShapeDtypeStruct(cin.shape, F32)],
        compiler_params=_params(("arbitrary",)),
    )(cin, w, dm)


def _bias_tables(rpb, rows):
    kh = min(WIN_H, rows)
    fold, onehot, in_win = _bias_selectors(kh)
    t = jnp.einsum("hdc,dsj->hsjc", rpb, jnp.asarray(fold), precision=HI)
    t = jnp.einsum("hsjc,cqk->hsqjk", t, jnp.asarray(onehot), precision=HI)
    t = jnp.where(jnp.asarray(in_win)[None, None, :, None, :], t, MASK_VALUE)
    return t.reshape(rpb.shape[0], kh, GRID_W, kh * GRID_W).astype(F32)


def _bias_selectors(kh):
    col = np.arange(GRID_W)
    col_start = np.clip(col - WIN_W // 2, 0, GRID_W - WIN_W)
    in_win = (col[None, :] >= col_start[:, None]) & (col[None, :] < col_start[:, None] + WIN_W)
    dc_idx = np.clip(col[None, :] - col[:, None], 1 - WIN_W, WIN_W - 1) + WIN_W - 1
    onehot = np.zeros((2 * WIN_W - 1, GRID_W, GRID_W), np.float32)
    qq, kk = np.nonzero(in_win)
    onehot[dc_idx[qq, kk], qq, kk] = 1.0
    fold = np.zeros((2 * WIN_H - 1, kh, kh), np.float32)
    for sh in range(kh):
        for j in range(kh):
            fold[j - sh + WIN_H - 1, sh, j] = 1.0
    return fold, onehot, in_win


def _mm_f32(a, b, name):
    M, K = a.shape
    N = b.shape[1]
    tm = _tile(M, 256, 8)

    def body(a_ref, b_ref, o_ref):
        o_ref[...] = _dot(a_ref[...], b_ref[...], HI)

    return pl.pallas_call(
        body, name=name, grid=(M // tm,),
        in_specs=[pl.BlockSpec((tm, K), lambda i: (i, 0)), pl.BlockSpec((K, N), lambda i: (0, 0))],
        out_specs=pl.BlockSpec((tm, N), lambda i: (i, 0)),
        out_shape=jax.ShapeDtypeStruct((M, N), F32),
        compiler_params=_params(("parallel",)),
    )(a, b)


def _bias_tables_transpose(dbias, rows):
    kh = min(WIN_H, rows)
    nh = dbias.shape[0]
    fold, onehot, _ = _bias_selectors(kh)
    ndc = onehot.shape[0]
    sel = np.zeros((GRID_W * GRID_W, LANES), np.float32)
    sel[:, :ndc] = onehot.reshape(ndc, -1).T
    x = dbias.reshape(nh, kh, GRID_W, kh, GRID_W).transpose(0, 1, 3, 2, 4).reshape(nh * kh * kh, GRID_W * GRID_W)
    z = _mm_f32(x, jnp.asarray(sel), "rpb_fold")[:, :ndc].reshape(nh, kh, kh, ndc)
    return jnp.einsum("dsj,hsjc->hdc", jnp.asarray(fold), z, precision=HI)


def _na_geometry(S):
    rows = S // GRID_W
    kh = min(WIN_H, rows)

    def row_start(r):
        return jnp.clip(r - kh // 2, 0, rows - kh)

    return rows, kh, row_start


def _na_by_head(ref, rr, lane):
    t = ref[rr * GRID_W:(rr + 1) * GRID_W, :]
    zero = jnp.zeros_like(t)
    return jnp.concatenate([jnp.where(lane < NA_HEAD_DIM, t, zero), jnp.where(lane >= NA_HEAD_DIM, t, zero)], axis=0)


def _na_pick_head(t2, lane):
    return jnp.where(lane < NA_HEAD_DIM, t2[:GRID_W], t2[GRID_W:])


def _na_scores(q_ref, k_ref, b_ref, i, nrs, nb, S, L, row_start, lane):
    qh = jnp.concatenate([_na_by_head(q_ref, rr, lane) for rr in range(nrs)], axis=0)
    sc = _dot_nt(qh, k_ref[pl.ds(S, L), :])
    starts, shifts, sb = [], [], []
    for rr in range(nrs):
        r = i * nrs + rr
        rs = row_start(r)
        starts.append(pl.multiple_of(rs * GRID_W, GRID_W))
        shifts.append(r - rs)
        bias = jnp.concatenate([b_ref[0, r - rs], b_ref[1, r - rs]], axis=0)
        sb.append(_dot_nt(qh[rr * 2 * GRID_W:(rr + 1) * 2 * GRID_W], k_ref[pl.ds(starts[-1], nb), :]) + bias)
    return qh, jnp.concatenate(sb, axis=0), sc, starts, shifts


def _na_fwd(qs, kb, vb, bias, S, L):
    T, naw = qs.shape
    rows, kh, row_start = _na_geometry(S)
    nb = kh * GRID_W
    npair = naw // LANES

    nrs = min(NA_ROWS_PER_STEP, rows)
    assert rows % nrs == 0

    def body(q_ref, k_ref, v_ref, b_ref, o_ref, lse_ref):
        i = pl.program_id(1)
        lane = lax.broadcasted_iota(jnp.int32, (GRID_W, LANES), 1)
        _, sb, sc, starts, _ = _na_scores(q_ref, k_ref, b_ref, i, nrs, nb, S, L, row_start, lane)
        m = jnp.maximum(jnp.max(sb, axis=-1, keepdims=True), jnp.max(sc, axis=-1, keepdims=True))
        pb, pc = jnp.exp(sb - m), jnp.exp(sc - m)
        l = jnp.sum(pb, axis=-1, keepdims=True) + jnp.sum(pc, axis=-1, keepdims=True)
        inv = 1.0 / l
        pb16, pc16 = (pb * inv).astype(BF16), (pc * inv).astype(BF16)
        oc = _dot(pc16, v_ref[pl.ds(S, L), :])
        lse = jnp.broadcast_to(m + jnp.log(l), oc.shape)
        for rr in range(nrs):
            two = slice(rr * 2 * GRID_W, (rr + 1) * 2 * GRID_W)
            rsl = slice(rr * GRID_W, (rr + 1) * GRID_W)
            o2 = oc[two] + _dot(pb16[two], v_ref[pl.ds(starts[rr], nb), :])
            o_ref[rsl, :] = _na_pick_head(o2, lane).astype(o_ref.dtype)
            lse_ref[rsl, :] = _na_pick_head(lse[two], lane)

    blk = pl.BlockSpec((nrs * GRID_W, LANES), lambda p, i: (i, p))
    col = pl.BlockSpec((T, LANES), lambda p, i: (0, p))
    return pl.pallas_call(
        body, name="na_fwd", grid=(npair, rows // nrs),
        in_specs=[blk, col, col, pl.BlockSpec((2, kh, GRID_W, nb), lambda p, i: (p, 0, 0, 0))],
        out_specs=[blk, blk],
        out_shape=[jax.ShapeDtypeStruct((S, naw), BF16), jax.ShapeDtypeStruct((S, naw), F32)],
        compiler_params=_params(("parallel", "arbitrary")),
    )(qs, kb, vb, bias)


def _na_bwd(qs, kb, vb, bias, do, o, lse, S, L):
    T, naw = qs.shape
    rows, kh, row_start = _na_geometry(S)
    nb = kh * GRID_W
    npair = naw // LANES

    nrs = min(NA_ROWS_PER_STEP, rows)
    assert rows % nrs == 0

    def body(q_ref, k_ref, v_ref, b_ref, do_ref, o_ref, lse_ref, dq_ref, dk_ref, dv_ref, db_ref):
        i = pl.program_id(1)

        @pl.when(i == 0)
        def _():
            dk_ref[...] = jnp.zeros_like(dk_ref)
            dv_ref[...] = jnp.zeros_like(dv_ref)
            db_ref[...] = jnp.zeros_like(db_ref)

        lane = lax.broadcasted_iota(jnp.int32, (GRID_W, LANES), 1)
        qh, sb, sc, starts, shifts = _na_scores(q_ref, k_ref, b_ref, i, nrs, nb, S, L, row_start, lane)
        doh = jnp.concatenate([_na_by_head(do_ref, rr, lane) for rr in range(nrs)], axis=0)
        o2 = jnp.concatenate([o_ref[rr * GRID_W:(rr + 1) * GRID_W, :] for rr in range(nrs) for _ in range(2)], axis=0)
        lse = jnp.concatenate([lse_ref[rr * GRID_W:(rr + 1) * GRID_W, :][:, hh * NA_HEAD_DIM:hh * NA_HEAD_DIM + 1]
                               for rr in range(nrs) for hh in range(2)], axis=0)
        pb, pc = jnp.exp(sb - lse), jnp.exp(sc - lse)
        delta = jnp.sum(doh.astype(F32) * o2.astype(F32), axis=-1, keepdims=True)
        dpb = jnp.concatenate([_dot_nt(doh[rr * 2 * GRID_W:(rr + 1) * 2 * GRID_W], v_ref[pl.ds(starts[rr], nb), :])
                               for rr in range(nrs)], axis=0)
        dsb = pb * (dpb - delta)
        dsc = pc * (_dot_nt(doh, v_ref[pl.ds(S, L), :]) - delta)
        dsb16, dsc16, pb16, pc16 = dsb.astype(BF16), dsc.astype(BF16), pb.astype(BF16), pc.astype(BF16)
        dqc = _dot(dsc16, k_ref[pl.ds(S, L), :])
        dk_ref[pl.ds(S, L), :] += _dot_tn(dsc16, qh)
        dv_ref[pl.ds(S, L), :] += _dot_tn(pc16, doh)
        for rr in range(nrs):
            two = slice(rr * 2 * GRID_W, (rr + 1) * 2 * GRID_W)
            band = pl.ds(starts[rr], nb)
            dq2 = dqc[two] + _dot(dsb16[two], k_ref[band, :])
            dq_ref[rr * GRID_W:(rr + 1) * GRID_W, :] = _na_pick_head(dq2, lane)
            dk_ref[band, :] += _dot_tn(dsb16[two], qh[two])
            dv_ref[band, :] += _dot_tn(pb16[two], doh[two])
            for hh in range(2):
                db_ref[hh, shifts[rr]] += dsb[(2 * rr + hh) * GRID_W:(2 * rr + hh + 1) * GRID_W]

    blk = pl.BlockSpec((nrs * GRID_W, LANES), lambda p, r: (r, p))
    col = pl.BlockSpec((T, LANES), lambda p, r: (0, p))
    return pl.pallas_call(
        body, name="na_bwd", grid=(npair, rows // nrs),
        in_specs=[blk, col, col, pl.BlockSpec((2, kh, GRID_W, nb), lambda p, r: (p, 0, 0, 0)), blk, blk, blk],
        out_specs=[blk, col, col, pl.BlockSpec((2, kh, GRID_W, nb), lambda p, r: (p, 0, 0, 0))],
        out_shape=[jax.ShapeDtypeStruct((S, naw), F32), jax.ShapeDtypeStruct((T, naw), F32),
                   jax.ShapeDtypeStruct((T, naw), F32), jax.ShapeDtypeStruct(bias.shape, F32)],
        compiler_params=_params(("parallel", "arbitrary")),
    )(qs, kb, vb, bias, do, o, lse)


def _hg_cols(naw, hgf, rev):
    qcol = (3 * naw) // hgf
    fcol = (3 * naw + hgf * (2 if rev else 1)) // hgf
    icol = (3 * naw + 3 * hgf) // hgf
    return qcol, fcol, icol


def _hg_chunk_order(S, L, rev):
    ncl, ncc = S // HG_CHUNK, L // HG_CHUNK
    nc = ncl + ncc

    def chunk_of(i):
        if rev:
            return nc - 1 - i
        return jnp.where(i < ncc, ncl + i, i - ncc)

    return nc, ncl, chunk_of


def _hg_gates(q, z, lb, rev):
    row = lax.broadcasted_iota(jnp.int32, (HG_CHUNK, HG_CHUNK), 0)
    colm = lax.broadcasted_iota(jnp.int32, (HG_CHUNK, HG_CHUNK), 1)
    tri = (colm >= row) if rev else (row >= colm)
    trif = tri.astype(F32)
    sig = _sigmoid(z)
    f = lb + (1.0 - lb) * sig
    lf = jnp.log(f)
    k = 1.0 - f
    cum = _dot(trif, lf, HI)
    mid = cum[HG_CHUNK // 2:HG_CHUNK // 2 + 1, :]
    last = cum[0:1, :] if rev else cum[HG_CHUNK - 1:HG_CHUNK, :]
    eq = jnp.exp(jnp.clip(cum - mid, -EXP_CLAMP, EXP_CLAMP))
    ek = jnp.exp(jnp.clip(mid - cum, -EXP_CLAMP, EXP_CLAMP))
    return tri, trif, sig, f, k, cum, last, eq, ek


def _hg_fwd(u, lbr, S, L, naw, hgf, rev):
    T = S + L
    nh = hgf // HG_DIM
    nc, ncl, chunk_of = _hg_chunk_order(S, L, rev)
    qcol, fcol, icol = _hg_cols(naw, hgf, rev)

    def body(q_ref, z_ref, v_ref, lb_ref, o_ref, st_ref, state):
        i = pl.program_id(0)

        @pl.when(i == 0)
        def _():
            state[...] = jnp.zeros_like(state)

        q, z, v = q_ref[...], z_ref[...], v_ref[...]
        tri, _, _, _, k, cum, last, eq, ek = _hg_gates(q, z, lb_ref[...], rev)
        qe, ke = (q * eq).astype(BF16), (k * ek).astype(BF16)
        qd, kd = (q * jnp.exp(cum)).astype(BF16), (k * jnp.exp(last - cum)).astype(BF16)
        v16, el = v.astype(BF16), jnp.exp(last)
        for h in range(nh):
            sl = slice(h * HG_DIM, (h + 1) * HG_DIM)
            a = jnp.where(tri, _dot_nt(qe[:, sl], ke[:, sl]), 0.0)
            s0 = state[h]
            st_ref[h] = s0
            o_ref[:, sl] = _dot(a.astype(BF16), v16[:, sl]) + _dot_nt(qd[:, sl], s0.astype(BF16))
            state[h] = s0 * el[:, sl] + _dot_tn(v16[:, sl], kd[:, sl])

    def blk(cb):
        return pl.BlockSpec((HG_CHUNK, hgf), lambda i: (chunk_of(i), cb))

    return pl.pallas_call(
        body, name="hg_fwd_rev" if rev else "hg_fwd", grid=(nc,),
        in_specs=[blk(qcol), blk(fcol), blk(icol), pl.BlockSpec((1, hgf), lambda i: (0, 0))],
        out_specs=[pl.BlockSpec((HG_CHUNK, hgf), lambda i: (chunk_of(i), 0)),
                   pl.BlockSpec((None, nh, HG_DIM, HG_DIM), lambda i: (chunk_of(i), 0, 0, 0))],
        out_shape=[jax.ShapeDtypeStruct((T, hgf), F32), jax.ShapeDtypeStruct((nc, nh, HG_DIM, HG_DIM), F32)],
        scratch_shapes=[pltpu.VMEM((nh, HG_DIM, HG_DIM), F32)],
        compiler_params=_params(("arbitrary",)),
    )(u, u, u, lbr)


def _hg_bwd(u, lbr, st, do, S, L, naw, hgf, rev):
    T = S + L
    nh = hgf // HG_DIM
    nc, ncl, chunk_fwd = _hg_chunk_order(S, L, rev)
    qcol, fcol, icol = _hg_cols(naw, hgf, rev)

    def chunk_of(j):
        return chunk_fwd(nc - 1 - j)

    def body(q_ref, z_ref, v_ref, lb_ref, st_ref, do_ref, dq_ref, dz_ref, dv_ref, dlb_ref, dstate,
             dqe_s, dke_s, dqd_s, dkd_s, dl_s):
        j = pl.program_id(0)

        @pl.when(j == 0)
        def _():
            dstate[...] = jnp.zeros_like(dstate)
            dlb_ref[...] = jnp.zeros_like(dlb_ref)

        q, z, v = q_ref[...], z_ref[...], v_ref[...]
        lb = lb_ref[...]
        tri, trif, sig, f, k, cum, last, eq, ek = _hg_gates(q, z, lb, rev)
        ec, el, ekd = jnp.exp(cum), jnp.exp(last), jnp.exp(last - cum)
        qe, ke, qd, kd = q * eq, k * ek, q * ec, k * ekd
        qd16, kd16 = qd.astype(BF16), kd.astype(BF16)
        dout = jnp.where(chunk_of(j) < ncl, do_ref[...], 0.0)
        qe2, ke2, v2, dout2 = _split2(qe), _split2(ke), _split2(v), _split2(dout)
        for h in range(nh):
            sl = slice(h * HG_DIM, (h + 1) * HG_DIM)
            qeh, keh, vh, douth = [(t[0][:, sl], t[1][:, sl]) for t in (qe2, ke2, v2, dout2)]
            a = jnp.where(tri, _dot_nt(qeh[0], keh[0]), 0.0).astype(BF16)
            s0 = st_ref[h]
            ds1 = dstate[h]
            s02, ds12 = _split2(s0), _split2(ds1)
            dv_ref[:, sl] = _dot_tn(a, douth[0]) + _dot_nt(kd16[:, sl], ds12[0])
            da2 = _split2(jnp.where(tri, _dot_x3(_dot_nt, douth, vh), 0.0))
            dqe_s[:, sl] = _dot_x3(_dot, da2, keh)
            dke_s[:, sl] = _dot_x3(_dot_tn, da2, qeh)
            dqd_s[:, sl] = _dot_x3(_dot, douth, s02)
            dkd_s[:, sl] = _dot_x3(_dot, vh, ds12)
            dl_s[:, sl] = _colsum(ds1 * s0)
            dstate[h] = _dot_tn(douth[0], qd16[:, sl]) + ds1 * el[:, sl]
        dqe, dke, dqd, dkd = dqe_s[...], dke_s[...], dqd_s[...], dkd_s[...]
        dq_ref[...] = dqe * eq + dqd * ec
        dk = dke * ek + dkd * ekd
        dcum = dqe * qe - dke * ke + dqd * qd - dkd * kd
        dlast = _colsum(dkd * kd) + el * dl_s[...]
        dlf = _dot_tn(trif, dcum, HI) + dlast
        df = dlf / f - dk
        dz_ref[...] = df * (1.0 - lb) * sig * (1.0 - sig)
        dlb_ref[...] += _colsum(df * (1.0 - sig))

    def blk(cb):
        return pl.BlockSpec((HG_CHUNK, hgf), lambda j: (chunk_of(j), cb))

    oblk = pl.BlockSpec((HG_CHUNK, hgf), lambda j: (chunk_of(j), 0))
    wide = pltpu.VMEM((HG_CHUNK, hgf), F32)
    return pl.pallas_call(
        body, name="hg_bwd_rev" if rev else "hg_bwd", grid=(nc,),
        in_specs=[blk(qcol), blk(fcol), blk(icol), pl.BlockSpec((1, hgf), lambda j: (0, 0)),
                  pl.BlockSpec((None, nh, HG_DIM, HG_DIM), lambda j: (chunk_of(j), 0, 0, 0)),
                  pl.BlockSpec((HG_CHUNK, hgf), lambda j: (jnp.minimum(chunk_of(j), ncl - 1), 0))],
        out_specs=[oblk, oblk, oblk, pl.BlockSpec((1, hgf), lambda j: (0, 0))],
        out_shape=[jax.ShapeDtypeStruct((T, hgf), F32)] * 3 + [jax.ShapeDtypeStruct((1, hgf), F32)],
        scratch_shapes=[pltpu.VMEM((nh, HG_DIM, HG_DIM), F32), wide, wide, wide, wide,
                        pltpu.VMEM((1, hgf), F32)],
        compiler_params=_params(("arbitrary",)),
    )(u, u, u, lbr, st, do)


def _adamw(w, g, m, v, name):
    shape = w.shape
    if w.ndim != 2 or shape[0] % 8 or shape[1] % LANES:
        w, g, m, v = [a.reshape(1, -1) for a in (w, g, m, v)]
    r, cw = w.shape
    tm = _tile(r, max(8, (1 << 19) // cw), 8) if r % 8 == 0 else r
    c1 = 1.0 / (1.0 - ADAM_B1 ** ADAM_STEP)
    c2 = 1.0 / (1.0 - ADAM_B2 ** ADAM_STEP)

    def body(w_ref, g_ref, m_ref, v_ref, d_ref, nm_ref, nv_ref):
        gg = g_ref[...]
        nm = ADAM_B1 * m_ref[...] + (1.0 - ADAM_B1) * gg
        nv = ADAM_B2 * v_ref[...] + (1.0 - ADAM_B2) * (gg * gg)
        d_ref[...] = -ADAM_LR * ((nm * c1) / (jnp.sqrt(nv * c2) + ADAM_EPS) + ADAM_WD * w_ref[...])
        nm_ref[...] = nm
        nv_ref[...] = nv

    spec = pl.BlockSpec((tm, cw), lambda i: (i, 0))
    outs = pl.pallas_call(
        body, name=name, grid=(r // tm,), in_specs=[spec] * 4, out_specs=[spec] * 3,
        out_shape=[jax.ShapeDtypeStruct((r, cw), F32)] * 3,
        compiler_params=_params(("parallel",)),
    )(w, g, m, v)
    return [o.reshape(shape) for o in outs]


def kernel(x, c, ctx, c_ctx, w_ada, b_ada, norm1_g, w_in, na_rpb, hg_lb_logits, hg_norm_g, w_pa, w_pb, w_out, norm2_g, w_ffn_in, w_ffn_out, final_g, loss_target, m_c_ctx, m_w_ada, m_b_ada, m_norm1_g, m_w_in, m_na_rpb, m_hg_lb_logits, m_hg_norm_g, m_w_pa, m_w_pb, m_w_out, m_norm2_g, m_w_ffn_in, m_w_ffn_out, m_final_g, v_c_ctx, v_w_ada, v_b_ada, v_norm1_g, v_w_in, v_na_rpb, v_hg_lb_logits, v_hg_norm_g, v_w_pa, v_w_pb, v_w_out, v_norm2_g, v_w_ffn_in, v_w_ffn_out, v_final_g):
    xi, yi, ci = lax.axis_index("x"), lax.axis_index("y"), lax.axis_index("c")
    sidx = 2 * xi + yi
    eidx = 4 * xi + 2 * yi + ci

    S, D = x.shape[1], x.shape[2]
    L = ctx.shape[1]
    T = S + L
    naw = NA_HEADS * NA_HEAD_DIM
    hgf = HG_HEADS * HG_DIM
    inw = 3 * naw + 5 * hgf + 2 * D
    fh = w_ffn_out.shape[1] * 4
    ads = w_ada.shape[2]
    fs = hg_lb_logits.shape[2]
    rows = S // GRID_W
    tr = _tile(L, 256)
    nlat, nall = S // tr, T // tr
    assert naw == hgf and D % naw == 0 and S % tr == 0 and 2 * hgf <= D

    pack0 = jnp.concatenate([c, jnp.pad(hg_lb_logits.reshape(1, -1), ((0, 0), (0, D - 4 * fs))),
                             jnp.zeros((6, D), F32)], axis=0)
    g0 = _all_gather8(pack0).reshape(8, 8, D)
    cs = g0[:, 0]
    lbl = g0[::2, 1, :4 * fs].reshape(4, 2, 2, fs).transpose(1, 2, 0, 3).reshape(2, 2, 4 * fs)
    p_lb = jax.nn.softmax(lbl, axis=0)
    lb = p_lb[0]
    lbb = [lb[d].reshape(1, hgf) for d in range(2)]

    cin = jnp.concatenate([cs, c_ctx[None], jnp.zeros((7, D), F32)], axis=0)
    b_sh = lax.dynamic_slice(b_ada, (0, sidx * ads), (1, ads))
    modp = _ada_fwd(cin, w_ada[0], b_sh)
    modfull = _all_gather8(modp).reshape(8, 16, ads)[::2].transpose(1, 0, 2).reshape(16, 4 * ads)
    mod_e = jnp.pad(lax.dynamic_index_in_dim(modfull, eidx, 0, keepdims=False).reshape(N_MOD, D), ((0, 2), (0, 0)))
    mod_c = jnp.pad(modfull[8].reshape(N_MOD, D), ((0, 2), (0, 0)))

    names = ["w_in", "w_pa", "w_pb", "w_out", "w_ffn_in", "w_ffn_out"]
    placed = [_cast_place(w[0], "cast_" + nm)
              for w, nm in zip((w_in, w_pa, w_pb, w_out, w_ffn_in, w_ffn_out), names)]
    def shards(g):
        return g.reshape(4, 2 * g.shape[2], g.shape[3])

    win3 = shards(_exchange_call(_exchange_gather(placed[:1]), "gather_w_in")[0])

    xx =jnp.concatenate([x[0], ctx[0]], axis=0)

    def f_ln1(i, rv, vv):
        xt, = rv
        g, me, mc = vv
        isc = i >= nlat
        sh = jnp.where(isc, mc[0:1], me[0:1])
        sc = jnp.where(isc, mc[1:2], me[1:2])
        return [xt * _rms(xt) * g * (1.0 + sc) + sh], []

    hb, = _rowwise(f_ln1, nall, tr, [(xx, D, 0, None)], [norm1_g, mod_e, mod_c], [(D, BF16)], [], "ln1")
    u, gathered = _mm_nn(hb, win3, F32, "mm_in", carry=_exchange_gather(placed[1:]))
    wpa3, wpb3, wout3, wi3, wfo3 = [shards(g) for g in gathered]
    wout1 = wout3.reshape(1, D, D)
    wfo1 = wfo3.reshape(1, fh, D)

    scale = NA_HEAD_DIM ** -0.5

    def f_qkv(i, rv, vv):
        q, k, v = rv
        return [q * scale, k, v], []

    qs, kb, vb = _rowwise(f_qkv, nall, tr, [(u, naw, 0, None), (u, naw, 1, None), (u, naw, 2, None)], [],
                          [(naw, BF16)] * 3, [], "qkv_cast")
    bias = _bias_tables(na_rpb[0], rows)
    o_na, lse = _na_fwd(qs, kb, vb, bias, S, L)

    o_f, st_f = _hg_fwd(u, lbb[0], S, L, naw, hgf, False)
    o_b, st_b = _hg_fwd(u, lbb[1], S, L, naw, hgf, True)

    hgn = jnp.tile(hg_norm_g, (1, HG_HEADS))
    hog_cb = (3 * naw + 4 * hgf) // hgf
    ga_cb = (3 * naw + 5 * hgf) // D
    gb_cb = ga_cb + 1

    def heads_rms(o):
        return jnp.concatenate([jnp.broadcast_to(_rms(o[:, h * HG_DIM:(h + 1) * HG_DIM]), (o.shape[0], HG_DIM))
                                for h in range(HG_HEADS)], axis=1)

    def f_readout(i, rv, vv):
        of, ob_, hog = rv
        g, = vv
        o = of + ob_
        return [o * heads_rms(o) * g * _silu(hog)], []

    ob, = _rowwise(f_readout, nlat, tr, [(o_f, hgf, 0, None), (o_b, hgf, 0, None), (u, hgf, hog_cb, None)],
                   [hgn], [(hgf, BF16)], [], "hg_readout")

    ya = _mm_nn(o_na, wpa3, F32, "mm_pa")
    yb = _mm_nn(ob, wpb3, F32, "mm_pb")

    def f_merge(i, rv, vv):
        ya_, yb_, ga, gb = rv
        return [_sigmoid(ga) * ya_ + _sigmoid(gb) * yb_], []

    yv, = _rowwise(f_merge, nlat, tr, [(ya, D, 0, None), (yb, D, 0, None), (u, D, ga_cb, None), (u, D, gb_cb, None)],
                   [], [(D, BF16)], [], "merge")
    z = _mm_nn(yv, wout1, F32, "mm_out")

    def f_res1(i, rv, vv):
        xt, zt = rv
        g, me = vv
        x1 = xt + me[2:3] * zt
        return [x1, x1 * _rms(x1) * g * (1.0 + me[4:5]) + me[3:4]], []

    x1, h2 = _rowwise(f_res1, nlat, tr, [(xx, D, 0, None), (z, D, 0, None)], [norm2_g, mod_e],
                      [(D, F32), (D, BF16)], [], "res1_ln2")
    au3, sw = _ffn_in_fused(h2, wi3, "mm_ffn_in")
    ff =_mm_nn(sw, wfo1, F32, "mm_ffn_out")

    fg = final_g.reshape(1, D)

    def f_final(i, rv, vv):
        x1t, ft, tg = rv
        g, me = vv
        x2 = x1t + me[5:6] * ft
        r3 = _rms(x2)
        xn = x2 * r3
        err = xn * g - tg
        dyy = err * (1.0 / D)
        dxn = dyy * g
        dx2 = r3 * (dxn - xn * jnp.mean(dxn * xn, axis=-1, keepdims=True))
        return [dx2, dx2 * me[5:6]], [_colsum(err * err), _colsum(dyy * xn), _colsum(dx2 * ft)]

    dx2, dfb, loss_cols, dfg, dg2 = _rowwise(
        f_final, nlat, tr, [(x1, D, 0, None), (ff, D, 0, None), (loss_target[0], D, 0, None)], [fg, mod_e],
        [(D, F32), (D, BF16)], [D, D, D], "final_loss")

    def dswiglu_epi(tm, tn):
        blk = pl.BlockSpec((2, tm, tn), lambda m, n, k: (0, m, n))

        def fn(d, ins):
            a, uu = ins[0][0], ins[0][1]
            return [(d * uu * _dsilu(a), d * _silu(a))]

        return dict(ins=[(au3, blk)], outs=[(jax.ShapeDtypeStruct((2, S, fh), BF16), blk)], fn=fn)

    dau3, = _mm_nt(dfb, wfo1, BF16, "mm_d_sw", epi=dswiglu_epi, tn_target=512)
    g_wfo = _mm_tn(sw, dfb, 1, "mm_dw_ffn_out").reshape(4, 2, fh // 8, D)
    dh2 = _mm_nt(dau3, wi3, F32, "mm_d_h2")
    g_wi = _mm_tn(h2, dau3, 4, "mm_dw_ffn_in")

    def f_ln2_bwd(i, rv, vv):
        dh, x1t, dx2t, zt = rv
        g, me = vv
        r2 = _rms(x1t)
        xn = x1t * r2
        dxn = dh * g * (1.0 + me[4:5])
        dx1 = dx2t + r2 * (dxn - xn * jnp.mean(dxn * xn, axis=-1, keepdims=True))
        return ([dx1, dx1 * me[2:3]],
                [_colsum(dh), _colsum(dh * xn * g), _colsum(dh * xn * (1.0 + me[4:5])), _colsum(dx1 * zt)])

    dx1, dzb, dsh2, dsc2, dn2g, dg1 = _rowwise(
        f_ln2_bwd, nlat, tr, [(dh2, D, 0, None), (x1, D, 0, None), (dx2, D, 0, None), (z, D, 0, None)],
        [norm2_g, mod_e], [(D, F32), (D, BF16)], [D, D, D, D], "ln2_bwd")

    g_wout = _mm_tn(yv, dzb, 1, "mm_dw_out").reshape(4, 2, D // 8, D)

    def dmerge_epi(tm, tn):
        blk = pl.BlockSpec((tm, tn), lambda m, n, k: (m, n))

        def gate(cb):
            return pl.BlockSpec((tm, tn), lambda m, n, k: (m, cb * (D // tn) + n))

        def fn(d, ins):
            ya_, yb_, ga, gb = ins
            sa, sb_ = _sigmoid(ga), _sigmoid(gb)
            return [d * sa, d * sb_, d * ya_ * sa * (1.0 - sa), d * yb_ * sb_ * (1.0 - sb_)]

        return dict(ins=[(ya, blk), (yb, blk), (u, gate(ga_cb)), (u, gate(gb_cb))],
                    outs=[(jax.ShapeDtypeStruct((S, D), BF16), blk)] * 4, fn=fn)

    dya, dyb, dga, dgb = _mm_nt(dzb, wout1, BF16, "mm_d_y", epi=dmerge_epi, tn_target=512)
    d_ona = _mm_nt(dya, wpa3, BF16, "mm_d_ona")
    d_ob = _mm_nt(dyb, wpb3, F32, "mm_d_ob")
    g_wpa = _mm_tn(o_na, dya, 4, "mm_dw_pa")
    g_wpb = _mm_tn(ob, dyb, 4, "mm_dw_pb")

    def f_dreadout(i, rv, vv):
        d, of, ob_, hog = rv
        g, = vv
        o = of + ob_
        on = o * heads_rms(o)
        t = d * _silu(hog) * g
        mt = jnp.concatenate([jnp.broadcast_to(jnp.mean((t * on)[:, h * HG_DIM:(h + 1) * HG_DIM], axis=-1,
                                                        keepdims=True), (o.shape[0], HG_DIM))
                              for h in range(HG_HEADS)], axis=1)
        do_ = heads_rms(o) * (t - on * mt)
        return [do_, d * on * g * _dsilu(hog)], [_colsum(d * _silu(hog) * on)]

    do_hg, dhog, dhgn = _rowwise(
        f_dreadout, nlat, tr, [(d_ob, hgf, 0, None), (o_f, hgf, 0, None), (o_b, hgf, 0, None),
                               (u, hgf, hog_cb, None)], [hgn], [(hgf, F32), (hgf, BF16)], [hgf], "hg_readout_bwd")

    dq_f, dz_f, dv_f, dlb_f = _hg_bwd(u, lbb[0], st_f, do_hg, S, L, naw, hgf, False)
    dq_b, dz_b, dv_b, dlb_b = _hg_bwd(u, lbb[1], st_b, do_hg, S, L, naw, hgf, True)
    dq_na, dk_na, dv_na, dbias = _na_bwd(qs, kb, vb, bias, d_ona, o_na, lse, S, L)

    ta = _tile(L, 128)
    nla, naa = S // ta, T // ta
    lat = lambda i: jnp.minimum(i, nla - 1)

    def f_assemble(i, rv, vv):
        dqn, dk, dv, dqf, dqb, dzf, dzb_, dvf, dvb, dho, dga_, dgb_ = rv
        keep = (i < nla).astype(F32)
        return [jnp.concatenate([dqn * (scale * keep), dk, dv, dqf + dqb, dzf, dzb_, dvf + dvb,
                                 dho.astype(F32) * keep, dga_.astype(F32) * keep, dgb_.astype(F32) * keep],
                                axis=1)], []

    du, = _rowwise(
        f_assemble, naa, ta,
        [(dq_na, naw, 0, lat), (dk_na, naw, 0, None), (dv_na, naw, 0, None), (dq_f, hgf, 0, None),
         (dq_b, hgf, 0, None), (dz_f, hgf, 0, None), (dz_b, hgf, 0, None), (dv_f, hgf, 0, None),
         (dv_b, hgf, 0, None), (dhog, hgf, 0, lat), (dga, D, 0, lat), (dgb, D, 0, lat)],
        [], [(inw, BF16)], [], "assemble_du")

    def add_half(g, land, nm):
        return _add_own_half(g, land, "rs_add_" + nm)

    early = [g_wpa, g_wpb, g_wout, g_wi, g_wfo]
    g_win, lands = _mm_tn(hb, du, 4, "mm_dw_in", carry=_exchange_swap_other_half(early))
    parts = [add_half(g, l, nm) for g, l, nm in zip(early, lands, names[1:])]
    dh, landed = _mm_nt(du, win3, F32, "mm_d_h", carry=_exchange_join(
        _exchange_scatter(parts), _exchange_swap_other_half([g_win])))
    parts = [add_half(g_win, landed[5], names[0])] + parts
    pieces = _exchange_call(_exchange_scatter(parts[:1]), "rs_scatter_w_in") + landed[:5]
    halves = [_sum_pieces(p, l, "rs_sum_" + nm) for p, l, nm in zip(parts, pieces, names)]
    g_win, g_wpa, g_wpb, g_wout, g_wi, g_wfo = [
        f.reshape(2 * f.shape[1], f.shape[2])
        for f in _exchange_call(_exchange_swap_result(halves), "rs_swap_result_half")]

    def f_ln1_bwd(i, rv, vv):
        dht, xt, dx1t = rv
        g, me = vv
        r1 = _rms(xt)
        xn = xt * r1
        dxn = dht * g * (1.0 + me[1:2])
        dx = dx1t + r1 * (dxn - xn * jnp.mean(dxn * xn, axis=-1, keepdims=True))
        return [dx], [_colsum(dht), _colsum(dht * xn * g), _colsum(dht * xn * (1.0 + me[1:2]))]

    grad_x, dsh1, dsc1, dn1g_l = _rowwise(
        f_ln1_bwd, nlat, tr, [(dh, D, 0, None), (xx, D, 0, None), (dx1, D, 0, None)], [norm1_g, mod_e],
        [(D, F32)], [D, D, D], "ln1_bwd")

    def f_ln1_bwd_ctx(i, rv, vv):
        dht, xt = rv
        g, mc = vv
        xn = xt * _rms(xt)
        return [], [_colsum(dht), _colsum(dht * xn * g), _colsum(dht * xn * (1.0 + mc[1:2]))]

    ctx_rows = lambda i: i + nlat
    dsh1c, dsc1c, dn1g_c = _rowwise(
        f_ln1_bwd_ctx, nall - nlat, tr, [(dh, D, 0, ctx_rows), (xx, D, 0, ctx_rows)], [norm1_g, mod_c],
        [], [D, D, D], "ln1_bwd_ctx")

    drpb = _bias_tables_transpose(dbias, rows).reshape(1, -1)
    nrp = -(-drpb.shape[1] // D)
    drpb_rows = jnp.pad(drpb, ((0, 0), (0, nrp * D - drpb.shape[1]))).reshape(nrp, D)
    dlb = jnp.concatenate([dlb_f, dlb_b], axis=1)
    dhg = jnp.sum(dhgn.reshape(HG_HEADS, HG_DIM), axis=0, keepdims=True)

    def wide(v):
        return jnp.pad(v, ((0, 0), (0, D - v.shape[1])))

    pack_rows = [loss_cols, dfg, dn2g, dn1g_l + dn1g_c, dsh1, dsc1, dg1, dsh2, dsc2, dg2, dsh1c, dsc1c,
                 wide(dhg), wide(dlb), drpb_rows]
    pack = jnp.concatenate(pack_rows, axis=0)
    npk = -(-pack.shape[0] // 8) * 8
    pack = jnp.pad(pack, ((0, npk - pack.shape[0]), (0, 0)))
    gp = _all_gather8(pack).reshape(8, npk, D)
    tot = _sum8(gp, "sum_small_grads")

    loss = (0.5 / D) * jnp.sum(tot[0])
    grad_final_g = tot[1]
    grad_norm2_g = tot[2:3]
    grad_norm1_g = tot[3:4]
    grad_hg_norm_g = tot[12:13, :HG_DIM]
    dlb_tot = tot[13, :2 * hgf].reshape(2, hgf)
    grad_na_rpb = tot[14:14 + nrp].reshape(-1)[:drpb.shape[1]].reshape(na_rpb.shape)
    dlog = jnp.stack([dlb_tot * p_lb[0] * (1.0 - p_lb[0]), -dlb_tot * p_lb[0] * p_lb[1]], axis=0)
    grad_hg_lb = lax.dynamic_slice(dlog, (0, 0, sidx * fs), (2, 2, fs))

    dmod_all = gp[:, 4:10].reshape(8, N_MOD * D)
    dmod_ctx = jnp.concatenate([tot[10], tot[11], jnp.zeros((4 * D,), F32)])[None]
    dm16 = jnp.concatenate([dmod_all, dmod_ctx, jnp.zeros((7, N_MOD * D), F32)], axis=0)
    grad_b_ada = jnp.sum(dm16, axis=0, keepdims=True)
    dm_sh = lax.dynamic_slice(dm16, (0, sidx * ads), (16, ads))
    g_wada, dcin = _ada_bwd(cin, w_ada[0], dm_sh)
    gc = _all_gather8(dcin[8:16]).reshape(8, 8, D)
    grad_c_ctx = (gc[0, 0] + gc[2, 0] + gc[4, 0] + gc[6, 0]) * _dsilu(c_ctx)

    grads = {
        "c_ctx": grad_c_ctx, "w_ada": g_wada[None], "b_ada": grad_b_ada, "norm1_g": grad_norm1_g,
        "w_in": g_win[None], "na_rpb": grad_na_rpb, "hg_lb_logits": grad_hg_lb, "hg_norm_g": grad_hg_norm_g,
        "w_pa": g_wpa[None], "w_pb": g_wpb[None], "w_out": g_wout[None], "norm2_g": grad_norm2_g,
        "w_ffn_in": g_wi[None], "w_ffn_out": g_wfo[None], "final_g": grad_final_g,
    }
    weights = {
        "c_ctx": (c_ctx, m_c_ctx, v_c_ctx), "w_ada": (w_ada, m_w_ada, v_w_ada), "b_ada": (b_ada, m_b_ada, v_b_ada),
        "norm1_g": (norm1_g, m_norm1_g, v_norm1_g), "w_in": (w_in, m_w_in, v_w_in),
        "na_rpb": (na_rpb, m_na_rpb, v_na_rpb), "hg_lb_logits": (hg_lb_logits, m_hg_lb_logits, v_hg_lb_logits),
        "hg_norm_g": (hg_norm_g, m_hg_norm_g, v_hg_norm_g), "w_pa": (w_pa, m_w_pa, v_w_pa),
        "w_pb": (w_pb, m_w_pb, v_w_pb), "w_out": (w_out, m_w_out, v_w_out),
        "norm2_g": (norm2_g, m_norm2_g, v_norm2_g), "w_ffn_in": (w_ffn_in, m_w_ffn_in, v_w_ffn_in),
        "w_ffn_out": (w_ffn_out, m_w_ffn_out, v_w_ffn_out), "final_g": (final_g, m_final_g, v_final_g),
    }
    order = list(weights)
    deltas, new_ms, new_vs = [], [], []
    for nm in order:
        w, m, v = weights[nm]
        g = grads[nm].reshape(w.shape)
        grads[nm] = g
        if w.ndim == 3 and w.shape[0] == 1:
            d_, m_, v_ = _adamw(w[0], g[0], m[0], v[0], "adamw_" + nm)
            d_, m_, v_ = d_[None], m_[None], v_[None]
        else:
            d_, m_, v_ = _adamw(w, g, m, v, "adamw_" + nm)
        deltas.append(d_)
        new_ms.append(m_)
        new_vs.append(v_)

    return (loss, grad_x[None], *[grads[nm] for nm in order], *deltas, *new_ms, *new_vs)
```

```python
import numpy as np

import jax
import jax.numpy as jnp
from jax import lax
from jax.experimental import pallas as pl
from jax.experimental.pallas import tpu as pltpu

F32 = jnp.float32
BF16 = jnp.bfloat16

GRID_W = 64
WIN_H = 8
WIN_W = 16
NA_HEADS = 16
NA_HEAD_DIM = 64
HG_HEADS = 8
HG_DIM = 128
HG_CHUNK = 64
N_MOD = 6
EPS = 1e-6
ADAM_LR = 0.001
ADAM_B1 = 0.9
ADAM_B2 = 0.999
ADAM_EPS = 1e-08
ADAM_WD = 0.01
ADAM_STEP = 10

LANES = 128
NA_ROWS_PER_STEP = 8
VMEM_LIMIT = 56 * 1024 * 1024
MASK_VALUE = -1e30
EXP_CLAMP = 80.0
MESH_ID = pl.DeviceIdType.MESH
HI = lax.Precision.HIGHEST


def _tile(dim, target, mult=LANES):
    best = None
    t = mult
    while t <= min(dim, target):
        if dim % t == 0:
            best = t
        t += mult
    assert best is not None, (dim, target, mult)
    return best


def _params(sem):
    return pltpu.CompilerParams(dimension_semantics=sem, vmem_limit_bytes=VMEM_LIMIT)


def _dot(a, b, precision=None):
    return jnp.dot(a, b, preferred_element_type=F32, precision=precision)


def _dot_nt(a, b, precision=None):
    return lax.dot_general(a, b, (((1,), (1,)), ((), ())), preferred_element_type=F32, precision=precision)


def _dot_tn(a, b, precision=None):
    return lax.dot_general(a, b, (((0,), (0,)), ((), ())), preferred_element_type=F32, precision=precision)


def _split2(v):
    hi = v.astype(BF16)
    return hi, (v - hi.astype(F32)).astype(BF16)


def _dot_x3(dot, a2, b2):
    return dot(a2[0], b2[0]) + (dot(a2[0], b2[1]) + dot(a2[1], b2[0]))


def _sigmoid(v):
    return 1.0 / (1.0 + jnp.exp(-v))


def _mm_call(dot, operands, grid, in_specs, out_spec, out_shape, acc_shape, name, carry=None, epi=None):
    nk = grid[2]
    nci = 0 if carry is None else len(carry["ins"])
    nco = 0 if carry is None else len(carry["outs"])
    nei = 0 if epi is None else len(epi["ins"])
    neo = 1 if epi is None else len(epi["outs"])

    def body(*refs):
        a_ref, b_ref = refs[:2]
        ein = refs[2:2 + nei]
        cin = refs[2 + nei:2 + nei + nci]
        outs = refs[2 + nei + nci:2 + nei + nci + neo]
        cout = refs[2 + nei + nci + neo:2 + nei + nci + neo + nco]
        acc = refs[2 + nei + nci + neo + nco]
        sems = refs[3 + nei + nci + neo + nco:]
        m, n, k = pl.program_id(0), pl.program_id(1), pl.program_id(2)

        if carry is not None:
            @pl.when((m == 0) & (n == 0) & (k == 0))
            def _():
                carry["start"](cin, cout, *sems)

        @pl.when(k == 0)
        def _():
            acc[...] = jnp.zeros_like(acc)

        acc[...] += dot(a_ref[...], b_ref[...])

        @pl.when(k == nk - 1)
        def _():
            vals = [acc[...]] if epi is None else epi["fn"](acc[...], [r[...] for r in ein])
            for r, v in zip(outs, vals):
                if isinstance(v, tuple):
                    for i, vi in enumerate(v):
                        r[i] = vi.astype(r.dtype)
                else:
                    r[...] = v.astype(r.dtype)

        if carry is not None:
            @pl.when((m == grid[0] - 1) & (n == grid[1] - 1) & (k == nk - 1))
            def _():
                carry["finish"](cin, cout, *sems)

    any_spec = pl.BlockSpec(memory_space=pl.ANY)
    scratch = [pltpu.VMEM(acc_shape, F32)]
    extra = {}
    if carry is not None:
        scratch += [pltpu.SemaphoreType.DMA((carry["nsem"],)), pltpu.SemaphoreType.DMA((carry["nsem"],))]
        extra["input_output_aliases"] = {2 + nei + i: neo + j for i, j in carry["alias"].items()}
    sem = ("arbitrary",) * 3 if carry is not None else ("parallel", "parallel", "arbitrary")
    main_outs = [(out_shape, out_spec)] if epi is None else list(epi["outs"])
    res = pl.pallas_call(
        body, name=name, grid=grid,
        in_specs=list(in_specs) + ([] if epi is None else [sp for _, sp in epi["ins"]]) + [any_spec] * nci,
        out_specs=[sp for _, sp in main_outs] + [any_spec] * nco,
        out_shape=[sh for sh, _ in main_outs] + ([] if carry is None else list(carry["outs"])),
        scratch_shapes=scratch, compiler_params=_params(sem), **extra,
    )(*operands, *([] if epi is None else [ar for ar, _ in epi["ins"]]), *([] if carry is None else carry["ins"]))
    res = list(res)
    main = res[0] if epi is None else res[:neo]
    return main if carry is None else (main, res[neo:])


def _mm_nn(a, b3, out_dtype, name, carry=None):
    M, K = a.shape
    nsh, _, Ns = b3.shape
    tm, tn, tk = _tile(M, 1024), _tile(Ns, 1408), _tile(K, 2048)
    tps, nk = Ns // tn, K // tk
    return _mm_call(
        _dot, (a, b3), (M // tm, nsh * tps, nk),
        [pl.BlockSpec((tm, tk), lambda m, n, k: (m, k)),
         pl.BlockSpec((None, tk, tn), lambda m, n, k: (n // tps, k, n % tps))],
        pl.BlockSpec((tm, tn), lambda m, n, k: (m, n)),
        jax.ShapeDtypeStruct((M, nsh * Ns), out_dtype), (tm, tn), name, carry)


def _mm_nt(a, b3, out_dtype, name, carry=None, epi=None, tn_target=1408):
    a3 = a if a.ndim == 3 else a[None]
    na, M, Ka = a3.shape
    nsh, Kw, Ns = b3.shape
    assert na * Ka == nsh * Ns
    tm, tn, tk = _tile(M, 1024), _tile(Kw, tn_target), _tile(int(np.gcd(Ka, Ns)), 2048)
    kpa, kps = Ka // tk, Ns // tk
    return _mm_call(
        _dot_nt, (a3, b3), (M // tm, Kw // tn, nsh * kps),
        [pl.BlockSpec((None, tm, tk), lambda m, n, k: (k // kpa, m, k % kpa)),
         pl.BlockSpec((None, tn, tk), lambda m, n, k: (k // kps, n, k % kps))],
        pl.BlockSpec((tm, tn), lambda m, n, k: (m, n)),
        jax.ShapeDtypeStruct((M, Kw), out_dtype), (tm, tn), name, carry,
        None if epi is None else epi(tm, tn))


def _mm_tn(a, g, nsh, name, carry=None, a_is_t=False):
    Tk, M = a.shape[::-1] if a_is_t else a.shape
    g3 = g if g.ndim == 3 else g[None]
    ng, _, Ng = g3.shape
    Ns = ng * Ng // nsh
    tm, tn, tk = _tile(M // 2, 1408), _tile(int(np.gcd(Ng, Ns)), 1408), _tile(Tk, 1024)
    mh, tps, tpg, nk = (M // 2) // tm, Ns // tn, Ng // tn, Tk // tk
    return _mm_call(
        _dot if a_is_t else _dot_tn, (a, g3), (M // tm, nsh * tps, nk),
        [pl.BlockSpec((tm, tk), lambda m, n, k: (m, k)) if a_is_t else pl.BlockSpec((tk, tm), lambda m, n, k: (k, m)),
         pl.BlockSpec((None, tk, tn), lambda m, n, k: (n // tpg, k, n % tpg))],
        pl.BlockSpec((None, None, tm, tn), lambda m, n, k: (n // tps, m // mh, m % mh, n % tps)),
        jax.ShapeDtypeStruct((nsh, 2, M // 2, Ns), F32), (tm, tn), name, carry)


def _mm_tn_half(at, g, nsh, other, name, carry=None):
    M, Tk = at.shape
    Ns = g.shape[1] // nsh
    tm, tn, tk = _tile(M // 2, 1408), _tile(Ns, 1408), _tile(Tk, 1024)
    mh, tps, nk = (M // 2) // tm, Ns // tn, Tk // tk

    def half():
        c = lax.axis_index("c")
        return 1 - c if other else c

    return _mm_call(
        _dot, (at, g), (mh, nsh * tps, nk),
        [pl.BlockSpec((tm, tk), lambda m, n, k: (half() * mh + m, k)),
         pl.BlockSpec((tk, tn), lambda m, n, k: (k, n))],
        pl.BlockSpec((None, tm, tn), lambda m, n, k: (n // tps, m, n % tps)),
        jax.ShapeDtypeStruct((nsh, M // 2, Ns), F32), (tm, tn), name, carry)


def _ffn_in_fused(h2, wi3, name):
    M, K = h2.shape
    _, _, Ns = wi3.shape
    fh = 2 * Ns
    tm, tn = _tile(M, 1024), _tile(Ns, 512)
    tps = Ns // tn

    def body(h_ref, ba_ref, bu_ref, au_ref, sw_ref, swt_ref):
        h = h_ref[...]
        a, u = _dot(h, ba_ref[...]), _dot(h, bu_ref[...])
        au_ref[0] = a
        au_ref[1] = u
        sw = (_silu(a) * u).astype(sw_ref.dtype)
        sw_ref[...] = sw
        swt_ref[...] = sw.T

    return pl.pallas_call(
        body, name=name, grid=(M // tm, fh // tn),
        in_specs=[pl.BlockSpec((tm, K), lambda m, n: (m, 0)),
                  pl.BlockSpec((None, K, tn), lambda m, n: (n // tps, 0, n % tps)),
                  pl.BlockSpec((None, K, tn), lambda m, n: (2 + n // tps, 0, n % tps))],
        out_specs=[pl.BlockSpec((2, tm, tn), lambda m, n: (0, m, n)), pl.BlockSpec((tm, tn), lambda m, n: (m, n)),
                   pl.BlockSpec((tn, tm), lambda m, n: (n, m))],
        out_shape=[jax.ShapeDtypeStruct((2, M, fh), F32), jax.ShapeDtypeStruct((M, fh), BF16),
                   jax.ShapeDtypeStruct((fh, M), BF16)],
        compiler_params=_params(("parallel", "parallel")),
    )(h2, wi3, wi3)


def _rowwise(fn, nblk, tm, rins, vins, routs, accs, name):
    nr, nv, no, na = len(rins), len(vins), len(routs), len(accs)

    def body(*refs):
        i = pl.program_id(0)
        outs, accv = fn(i, [r[...] for r in refs[:nr]], [r[...] for r in refs[nr:nr + nv]])
        for r, v, spec in zip(refs[nr + nv:nr + nv + no], outs, routs):
            v = v.astype(r.dtype)
            r[...] = v.T if len(spec) == 3 else v
        arefs = refs[nr + nv + no:]
        if na:
            @pl.when(i == 0)
            def _():
                for a in arefs:
                    a[...] = jnp.zeros_like(a)

            for a, v in zip(arefs, accv):
                a[...] += v

    def row_spec(w, cb, rm):
        if rm is None:
            return pl.BlockSpec((tm, w), lambda i: (i, cb))
        return pl.BlockSpec((tm, w), lambda i: (rm(i), cb))

    in_specs = [row_spec(w, cb, rm) for (_, w, cb, rm) in rins]
    in_specs += [pl.BlockSpec(v.shape, lambda i: (0, 0)) for v in vins]
    def out_of(spec):
        w, dt = spec[:2]
        if len(spec) == 3:
            return pl.BlockSpec((w, tm), lambda i: (0, i)), jax.ShapeDtypeStruct((w, nblk * tm), dt)
        return pl.BlockSpec((tm, w), lambda i: (i, 0)), jax.ShapeDtypeStruct((nblk * tm, w), dt)

    out_specs = [out_of(sp)[0] for sp in routs] + [pl.BlockSpec((1, w), lambda i: (0, 0)) for w in accs]
    out_shape = [out_of(sp)[1] for sp in routs] + [jax.ShapeDtypeStruct((1, w), F32) for w in accs]
    res = pl.pallas_call(
        body, name=name, grid=(nblk,), in_specs=in_specs, out_specs=out_specs, out_shape=out_shape,
        compiler_params=_params(("arbitrary",)),
    )(*[r[0] for r in rins], *vins)
    return list(res)


def _colsum(v):
    return jnp.sum(v, axis=0, keepdims=True)


def _rms(v):
    return lax.rsqrt(jnp.mean(v * v, axis=-1, keepdims=True) + EPS)


def _all_gather8(xs):
    m_per, n = xs.shape

    def body(x_ref, out_ref, send_sems, recv_sems, local_sem):
        x, y, c = lax.axis_index("x"), lax.axis_index("y"), lax.axis_index("c")
        me, sibling = (x, y, c), (x, y, 1 - c)
        chips = [(1 - x, y), (x, 1 - y), (1 - x, 1 - y)]

        def rows(px, py, pc):
            return out_ref.at[pl.ds((4 * px + 2 * py + pc) * m_per, m_per), :]

        def copy(k, block, to, src=None):
            return pltpu.make_async_remote_copy(
                src_ref=rows(*block) if src is None else src, dst_ref=rows(*block),
                send_sem=send_sems.at[k], recv_sem=recv_sems.at[k], device_id=to, device_id_type=MESH_ID)

        mine = pltpu.make_async_copy(x_ref, rows(*me), local_sem)
        mine.start()
        first = [copy(0, me, sibling, src=x_ref)]
        first += [copy(1 + j, me, (*chip, c), src=x_ref) for j, chip in enumerate(chips)]
        for cp in first:
            cp.start()
        passed = [copy(4 + j, (*chip, c), sibling) for j, chip in enumerate(chips)]
        for j, chip in enumerate(chips):
            copy(1 + j, (*chip, c), me).wait_recv()
            passed[j].start()
        copy(0, sibling, me).wait_recv()
        for j, chip in enumerate(chips):
            copy(4 + j, (*chip, 1 - c), me).wait_recv()
        for cp in first + passed:
            cp.wait_send()
        mine.wait()

    return pl.pallas_call(
        body, name="all_gather8_%dx%d" % (m_per, n),
        out_shape=jax.ShapeDtypeStruct((8 * m_per, n), xs.dtype),
        in_specs=[pl.BlockSpec(memory_space=pltpu.VMEM)],
        out_specs=pl.BlockSpec(memory_space=pltpu.VMEM),
        scratch_shapes=[pltpu.SemaphoreType.DMA((7,)), pltpu.SemaphoreType.DMA((7,)), pltpu.SemaphoreType.DMA],
    )(xs)


def _mesh_pos():
    x, y, c = lax.axis_index("x"), lax.axis_index("y"), lax.axis_index("c")
    chips = [(1 - x, y), (x, 1 - y), (1 - x, 1 - y)]
    return x, y, c, chips


def _my_shard():
    return 2 * lax.axis_index("x") + lax.axis_index("y")


def _cast_place(w, name):
    r, cw = w.shape
    rh = r // 2
    tm = _tile(rh, max(16, (1 << 20) // (4 * cw)), 16)
    nt = rh // tm

    def body(w_ref, o_ref):
        o_ref[...] = w_ref[...].astype(o_ref.dtype)

    return pl.pallas_call(
        body, name=name, grid=(2, nt),
        in_specs=[pl.BlockSpec((tm, cw), lambda h, i: (h * nt + i, 0))],
        out_specs=pl.BlockSpec((None, None, tm, cw), lambda h, i: (_my_shard(), h, i, 0)),
        out_shape=jax.ShapeDtypeStruct((4, 2, rh, cw), BF16),
        compiler_params=_params(("parallel", "parallel")),
    )(w)


def _exchange_gather(bufs):
    n = len(bufs)

    def copies(out, send_sems, recv_sems, base):
        def copy(i, k, shard, half, to):
            dst = out[i].at[shard, half]
            return pltpu.make_async_remote_copy(
                src_ref=dst, dst_ref=dst, send_sem=send_sems.at[base + 6 * i + k],
                recv_sem=recv_sems.at[base + 6 * i + k], device_id=to, device_id_type=MESH_ID)
        return copy

    def first(copy):
        x, y, c, chips = _mesh_pos()
        return [copy(i, j, 2 * x + y, c, (*chip, c)) for i in range(n) for j, chip in enumerate(chips)]

    def start(cin, out, send_sems, recv_sems, base=0):
        for cp in first(copies(out, send_sems, recv_sems, base)):
            cp.start()

    def finish(cin, out, send_sems, recv_sems, base=0):
        copy = copies(out, send_sems, recv_sems, base)
        x, y, c, chips = _mesh_pos()
        passed = []
        for j, chip in enumerate(chips):
            sj = 2 * chip[0] + chip[1]
            for i in range(n):
                copy(i, j, sj, c, (x, y, c)).wait_recv()
                cp = copy(i, 3 + j, sj, c, (x, y, 1 - c))
                cp.start()
                passed.append(cp)
        for j, chip in enumerate(chips):
            sj = 2 * chip[0] + chip[1]
            for i in range(n):
                copy(i, 3 + j, sj, 1 - c, (x, y, c)).wait_recv()
        for cp in first(copy) + passed:
            cp.wait_send()

    return dict(ins=list(bufs), outs=[jax.ShapeDtypeStruct(b.shape, b.dtype) for b in bufs],
                alias={i: i for i in range(n)}, nsem=6 * n, start=start, finish=finish)


def _exchange_join(a, b):
    nai, nao = len(a["ins"]), len(a["outs"])

    def start(cin, cout, send_sems, recv_sems, base=0):
        a["start"](cin[:nai], cout[:nao], send_sems, recv_sems, base)
        b["start"](cin[nai:], cout[nao:], send_sems, recv_sems, base + a["nsem"])

    def finish(cin, cout, send_sems, recv_sems, base=0):
        a["finish"](cin[:nai], cout[:nao], send_sems, recv_sems, base)
        b["finish"](cin[nai:], cout[nao:], send_sems, recv_sems, base + a["nsem"])

    alias = dict(a["alias"])
    alias.update({nai + i: nao + j for i, j in b["alias"].items()})
    return dict(ins=a["ins"] + b["ins"], outs=a["outs"] + b["outs"], alias=alias, nsem=a["nsem"] + b["nsem"],
                start=start, finish=finish)


def _exchange_call(ex, name):
    nci, nco = len(ex["ins"]), len(ex["outs"])

    def body(*refs):
        cin, cout, sems = refs[:nci], refs[nci:nci + nco], refs[nci + nco:]
        ex["start"](cin, cout, *sems)
        ex["finish"](cin, cout, *sems)

    any_spec = pl.BlockSpec(memory_space=pl.ANY)
    return list(pl.pallas_call(
        body, name=name, out_shape=list(ex["outs"]), in_specs=[any_spec] * nci, out_specs=[any_spec] * nco,
        input_output_aliases=dict(ex["alias"]),
        scratch_shapes=[pltpu.SemaphoreType.DMA((ex["nsem"],)), pltpu.SemaphoreType.DMA((ex["nsem"],))],
    )(*ex["ins"]))


def _exchange_swap_other_half(gs):
    n = len(gs)

    def copies(g, land, send_sems, recv_sems, base):
        x, y, c, _ = _mesh_pos()
        return [pltpu.make_async_remote_copy(
            src_ref=g[i].at[:, 1 - c] if len(gs[i].shape) == 4 else g[i], dst_ref=land[i],
            send_sem=send_sems.at[base + i],
            recv_sem=recv_sems.at[base + i], device_id=(x, y, 1 - c), device_id_type=MESH_ID) for i in range(n)]

    def start(g, land, send_sems, recv_sems, base=0):
        for cp in copies(g, land, send_sems, recv_sems, base):
            cp.start()

    def finish(g, land, send_sems, recv_sems, base=0):
        for cp in copies(g, land, send_sems, recv_sems, base):
            cp.wait()

    return dict(ins=list(gs), outs=[jax.ShapeDtypeStruct((4,) + g.shape[-2:], g.dtype) for g in gs],
                alias={}, nsem=n, start=start, finish=finish)


def _exchange_scatter(ps):
    n = len(ps)

    def copies(p, land, send_sems, recv_sems, base):
        x, y, c, chips = _mesh_pos()
        return [pltpu.make_async_remote_copy(
            src_ref=p[i].at[2 * chip[0] + chip[1]], dst_ref=land[i].at[j],
            send_sem=send_sems.at[base + 3 * i + j], recv_sem=recv_sems.at[base + 3 * i + j],
            device_id=(*chip, c), device_id_type=MESH_ID) for i in range(n) for j, chip in enumerate(chips)]

    def start(p, land, send_sems, recv_sems, base=0):
        for cp in copies(p, land, send_sems, recv_sems, base):
            cp.start()

    def finish(p, land, send_sems, recv_sems, base=0):
        for cp in copies(p, land, send_sems, recv_sems, base):
            cp.wait()

    return dict(ins=list(ps), outs=[jax.ShapeDtypeStruct((3,) + p.shape[1:], p.dtype) for p in ps],
                alias={}, nsem=3 * n, start=start, finish=finish)


def _exchange_swap_result(bufs):
    n = len(bufs)

    def copies(out, send_sems, recv_sems, base, half):
        x, y, c, _ = _mesh_pos()
        h = c if half == "mine" else 1 - c
        return [pltpu.make_async_remote_copy(
            src_ref=out[i].at[h], dst_ref=out[i].at[h], send_sem=send_sems.at[base + i],
            recv_sem=recv_sems.at[base + i], device_id=(x, y, 1 - c), device_id_type=MESH_ID) for i in range(n)]

    def start(cin, out, send_sems, recv_sems, base=0):
        for cp in copies(out, send_sems, recv_sems, base, "mine"):
            cp.start()

    def finish(cin, out, send_sems, recv_sems, base=0):
        for cp in copies(out, send_sems, recv_sems, base, "theirs"):
            cp.wait_recv()
        for cp in copies(out, send_sems, recv_sems, base, "mine"):
            cp.wait_send()

    return dict(ins=list(bufs), outs=[jax.ShapeDtypeStruct(b.shape, b.dtype) for b in bufs],
                alias={i: i for i in range(n)}, nsem=n, start=start, finish=finish)


def _add_own_half(g, land, name):
    rh, cw = g.shape[-2:]
    tm = _tile(rh, max(16, (1 << 20) // (4 * cw)), 16)

    def body(g_ref, l_ref, o_ref):
        o_ref[...] = (g_ref[...] + l_ref[...]).astype(o_ref.dtype)

    mine = (pl.BlockSpec((None, None, tm, cw), lambda s, i: (s, lax.axis_index("c"), i, 0)) if g.ndim == 4
            else pl.BlockSpec((None, tm, cw), lambda s, i: (s, i, 0)))
    return pl.pallas_call(
        body, name=name, grid=(4, rh // tm),
        in_specs=[mine,
                  pl.BlockSpec((None, tm, cw), lambda s, i: (s, i, 0))],
        out_specs=pl.BlockSpec((None, tm, cw), lambda s, i: (s, i, 0)),
        out_shape=jax.ShapeDtypeStruct((4, rh, cw), BF16),
        compiler_params=_params(("parallel", "parallel")),
    )(g, land)


def _sum_pieces(part, land, name):
    _, rh, cw = land.shape
    tm = _tile(rh, max(16, (1 << 20) // (4 * cw)), 16)

    def body(p_ref, l_ref, o_ref):
        v = l_ref[...].astype(F32)
        o_ref[...] = (p_ref[...].astype(F32) + v[0]) + (v[1] + v[2])

    return pl.pallas_call(
        body, name=name, grid=(rh // tm,),
        in_specs=[pl.BlockSpec((None, tm, cw), lambda i: (_my_shard(), i, 0)),
                  pl.BlockSpec((3, tm, cw), lambda i: (0, i, 0))],
        out_specs=pl.BlockSpec((None, tm, cw), lambda i: (lax.axis_index("c"), i, 0)),
        out_shape=jax.ShapeDtypeStruct((2, rh, cw), F32),
        compiler_params=_params(("parallel",)),
    )(part, land)


def _sum8(g, name):
    def body(g_ref, o_ref):
        acc = g_ref[0]
        for k in range(1, 8):
            acc = acc + g_ref[k]
        o_ref[...] = acc

    return pl.pallas_call(body, name=name, out_shape=jax.ShapeDtypeStruct(g.shape[1:], F32))(g)


def _silu(v):
    return v * _sigmoid(v)


def _dsilu(v):
    s = _sigmoid(v)
    return s * (1.0 + v * (1.0 - s))


def _ada_fwd(cin, w, b):
    d, ns = w.shape
    tn = _tile(ns, 512)

    def body(c_ref, w_ref, b_ref, o_ref):
        o_ref[...] = _dot(_silu(c_ref[...]), w_ref[...], HI) + b_ref[...]

    return pl.pallas_call(
        body, name="ada_fwd", grid=(ns // tn,),
        in_specs=[pl.BlockSpec(cin.shape, lambda n: (0, 0)), pl.BlockSpec((d, tn), lambda n: (0, n)),
                  pl.BlockSpec((1, tn), lambda n: (0, n))],
        out_specs=pl.BlockSpec((cin.shape[0], tn), lambda n: (0, n)),
        out_shape=jax.ShapeDtypeStruct((cin.shape[0], ns), F32),
        compiler_params=_params(("parallel",)),
    )(cin, w, b)


def _ada_bwd(cin, w, dm):
    d, ns = w.shape
    tn = _tile(ns, 512)

    def body(c_ref, w_ref, d_ref, dw_ref, dc_ref):
        n = pl.program_id(0)

        @pl.when(n == 0)
        def _():
            dc_ref[...] = jnp.zeros_like(dc_ref)

        dw_ref[...] = _dot_tn(_silu(c_ref[...]), d_ref[...], HI)
        dc_ref[...] += _dot_nt(d_ref[...], w_ref[...], HI)

    return pl.pallas_call(
        body, name="ada_bwd", grid=(ns // tn,),
        in_specs=[pl.BlockSpec(cin.shape, lambda n: (0, 0)), pl.BlockSpec((d, tn), lambda n: (0, n)),
                  pl.BlockSpec((cin.shape[0], tn), lambda n: (0, n))],
        out_specs=[pl.BlockSpec((d, tn), lambda n: (0, n)), pl.BlockSpec(cin.shape, lambda n: (0, 0))],
        out_shape=[jax.ShapeDtypeStruct((d, ns), F32), jax.ShapeDtypeStruct(cin.shape, F32)],
        compiler_params=_params(("arbitrary",)),
    )(cin, w, dm)


def _bias_tables(rpb, rows):
    kh = min(WIN_H, rows)
    fold, onehot, in_win = _bias_selectors(kh)
    t = jnp.einsum("hdc,dsj->hsjc", rpb, jnp.asarray(fold), precision=HI)
    t = jnp.einsum("hsjc,cqk->hsqjk", t, jnp.asarray(onehot), precision=HI)
    t = jnp.where(jnp.asarray(in_win)[None, None, :, None, :], t, MASK_VALUE)
    return t.reshape(rpb.shape[0], kh, GRID_W, kh * GRID_W).astype(F32)


def _bias_selectors(kh):
    col = np.arange(GRID_W)
    col_start = np.clip(col - WIN_W // 2, 0, GRID_W - WIN_W)
    in_win = (col[None, :] >= col_start[:, None]) & (col[None, :] < col_start[:, None] + WIN_W)
    dc_idx = np.clip(col[None, :] - col[:, None], 1 - WIN_W, WIN_W - 1) + WIN_W - 1
    onehot = np.zeros((2 * WIN_W - 1, GRID_W, GRID_W), np.float32)
    qq, kk = np.nonzero(in_win)
    onehot[dc_idx[qq, kk], qq, kk] = 1.0
    fold = np.zeros((2 * WIN_H - 1, kh, kh), np.float32)
    for sh in range(kh):
        for j in range(kh):
            fold[j - sh + WIN_H - 1, sh, j] = 1.0
    return fold, onehot, in_win


def _mm_f32(a, b, name):
    M, K = a.shape
    N = b.shape[1]
    tm = _tile(M, 256, 8)

    def body(a_ref, b_ref, o_ref):
        o_ref[...] = _dot(a_ref[...], b_ref[...], HI)

    return pl.pallas_call(
        body, name=name, grid=(M // tm,),
        in_specs=[pl.BlockSpec((tm, K), lambda i: (i, 0)), pl.BlockSpec((K, N), lambda i: (0, 0))],
        out_specs=pl.BlockSpec((tm, N), lambda i: (i, 0)),
        out_shape=jax.ShapeDtypeStruct((M, N), F32),
        compiler_params=_params(("parallel",)),
    )(a, b)


def _bias_tables_transpose(dbias, rows):
    kh = min(WIN_H, rows)
    nh = dbias.shape[0]
    fold, onehot, _ = _bias_selectors(kh)
    ndc = onehot.shape[0]
    sel = np.zeros((GRID_W * GRID_W, LANES), np.float32)
    sel[:, :ndc] = onehot.reshape(ndc, -1).T
    x = dbias.reshape(nh, kh, GRID_W, kh, GRID_W).transpose(0, 1, 3, 2, 4).reshape(nh * kh * kh, GRID_W * GRID_W)
    z = _mm_f32(x, jnp.asarray(sel), "rpb_fold")[:, :ndc].reshape(nh, kh, kh, ndc)
    return jnp.einsum("dsj,hsjc->hdc", jnp.asarray(fold), z, precision=HI)


def _na_geometry(S):
    rows = S // GRID_W
    kh = min(WIN_H, rows)

    def row_start(r):
        return jnp.clip(r - kh // 2, 0, rows - kh)

    return rows, kh, row_start


def _na_by_head(ref, rr, lane):
    t = ref[rr * GRID_W:(rr + 1) * GRID_W, :]
    zero = jnp.zeros_like(t)
    return jnp.concatenate([jnp.where(lane < NA_HEAD_DIM, t, zero), jnp.where(lane >= NA_HEAD_DIM, t, zero)], axis=0)


def _na_pick_head(t2, lane):
    return jnp.where(lane < NA_HEAD_DIM, t2[:GRID_W], t2[GRID_W:])


def _na_scores(q_ref, k_ref, b_ref, i, nrs, nb, S, L, row_start, lane):
    qh = jnp.concatenate([_na_by_head(q_ref, rr, lane) for rr in range(nrs)], axis=0)
    sc = _dot_nt(qh, k_ref[pl.ds(S, L), :])
    starts, shifts, sb = [], [], []
    for rr in range(nrs):
        r = i * nrs + rr
        rs = row_start(r)
        starts.append(pl.multiple_of(rs * GRID_W, GRID_W))
        shifts.append(r - rs)
        bias = jnp.concatenate([b_ref[0, r - rs], b_ref[1, r - rs]], axis=0)
        sb.append(_dot_nt(qh[rr * 2 * GRID_W:(rr + 1) * 2 * GRID_W], k_ref[pl.ds(starts[-1], nb), :]) + bias)
    return qh, jnp.concatenate(sb, axis=0), sc, starts, shifts


def _na_fwd(qs, kb, vb, bias, S, L):
    T, naw = qs.shape
    rows, kh, row_start = _na_geometry(S)
    nb = kh * GRID_W
    npair = naw // LANES

    nrs = min(NA_ROWS_PER_STEP, rows)
    assert rows % nrs == 0

    def body(q_ref, k_ref, v_ref, b_ref, o_ref, lse_ref):
        i = pl.program_id(1)
        lane = lax.broadcasted_iota(jnp.int32, (GRID_W, LANES), 1)
        _, sb, sc, starts, _ = _na_scores(q_ref, k_ref, b_ref, i, nrs, nb, S, L, row_start, lane)
        m = jnp.maximum(jnp.max(sb, axis=-1, keepdims=True), jnp.max(sc, axis=-1, keepdims=True))
        pb, pc = jnp.exp(sb - m), jnp.exp(sc - m)
        l = jnp.sum(pb, axis=-1, keepdims=True) + jnp.sum(pc, axis=-1, keepdims=True)
        inv = 1.0 / l
        pb16, pc16 = (pb * inv).astype(BF16), (pc * inv).astype(BF16)
        oc = _dot(pc16, v_ref[pl.ds(S, L), :])
        lse = jnp.broadcast_to(m + jnp.log(l), oc.shape)
        for rr in range(nrs):
            two = slice(rr * 2 * GRID_W, (rr + 1) * 2 * GRID_W)
            rsl = slice(rr * GRID_W, (rr + 1) * GRID_W)
            o2 = oc[two] + _dot(pb16[two], v_ref[pl.ds(starts[rr], nb), :])
            o_ref[rsl, :] = _na_pick_head(o2, lane).astype(o_ref.dtype)
            lse_ref[rsl, :] = _na_pick_head(lse[two], lane)

    blk = pl.BlockSpec((nrs * GRID_W, LANES), lambda p, i: (i, p))
    col = pl.BlockSpec((T, LANES), lambda p, i: (0, p))
    return pl.pallas_call(
        body, name="na_fwd", grid=(npair, rows // nrs),
        in_specs=[blk, col, col, pl.BlockSpec((2, kh, GRID_W, nb), lambda p, i: (p, 0, 0, 0))],
        out_specs=[blk, blk],
        out_shape=[jax.ShapeDtypeStruct((S, naw), BF16), jax.ShapeDtypeStruct((S, naw), F32)],
        compiler_params=_params(("parallel", "arbitrary")),
    )(qs, kb, vb, bias)


def _na_bwd(qs, kb, vb, bias, do, o, lse, S, L):
    T, naw = qs.shape
    rows, kh, row_start = _na_geometry(S)
    nb = kh * GRID_W
    npair = naw // LANES

    nrs = min(NA_ROWS_PER_STEP, rows)
    assert rows % nrs == 0

    def body(q_ref, k_ref, v_ref, b_ref, do_ref, o_ref, lse_ref, dq_ref, dk_ref, dv_ref, db_ref):
        i = pl.program_id(1)

        @pl.when(i == 0)
        def _():
            dk_ref[...] = jnp.zeros_like(dk_ref)
            dv_ref[...] = jnp.zeros_like(dv_ref)
            db_ref[...] = jnp.zeros_like(db_ref)

        lane = lax.broadcasted_iota(jnp.int32, (GRID_W, LANES), 1)
        qh, sb, sc, starts, shifts = _na_scores(q_ref, k_ref, b_ref, i, nrs, nb, S, L, row_start, lane)
        doh = jnp.concatenate([_na_by_head(do_ref, rr, lane) for rr in range(nrs)], axis=0)
        o2 = jnp.concatenate([o_ref[rr * GRID_W:(rr + 1) * GRID_W, :] for rr in range(nrs) for _ in range(2)], axis=0)
        lse = jnp.concatenate([lse_ref[rr * GRID_W:(rr + 1) * GRID_W, :][:, hh * NA_HEAD_DIM:hh * NA_HEAD_DIM + 1]
                               for rr in range(nrs) for hh in range(2)], axis=0)
        pb, pc = jnp.exp(sb - lse), jnp.exp(sc - lse)
        delta = jnp.sum(doh.astype(F32) * o2.astype(F32), axis=-1, keepdims=True)
        dpb = jnp.concatenate([_dot_nt(doh[rr * 2 * GRID_W:(rr + 1) * 2 * GRID_W], v_ref[pl.ds(starts[rr], nb), :])
                               for rr in range(nrs)], axis=0)
        dsb = pb * (dpb - delta)
        dsc = pc * (_dot_nt(doh, v_ref[pl.ds(S, L), :]) - delta)
        dsb16, dsc16, pb16, pc16 = dsb.astype(BF16), dsc.astype(BF16), pb.astype(BF16), pc.astype(BF16)
        dqc = _dot(dsc16, k_ref[pl.ds(S, L), :])
        dk_ref[pl.ds(S, L), :] += _dot_tn(dsc16, qh)
        dv_ref[pl.ds(S, L), :] += _dot_tn(pc16, doh)
        for rr in range(nrs):
            two = slice(rr * 2 * GRID_W, (rr + 1) * 2 * GRID_W)
            band = pl.ds(starts[rr], nb)
            dq2 = dqc[two] + _dot(dsb16[two], k_ref[band, :])
            dq_ref[rr * GRID_W:(rr + 1) * GRID_W, :] = _na_pick_head(dq2, lane)
            dk_ref[band, :] += _dot_tn(dsb16[two], qh[two])
            dv_ref[band, :] += _dot_tn(pb16[two], doh[two])
            for hh in range(2):
                db_ref[hh, shifts[rr]] += dsb[(2 * rr + hh) * GRID_W:(2 * rr + hh + 1) * GRID_W]

    blk = pl.BlockSpec((nrs * GRID_W, LANES), lambda p, r: (r, p))
    col = pl.BlockSpec((T, LANES), lambda p, r: (0, p))
    return pl.pallas_call(
        body, name="na_bwd", grid=(npair, rows // nrs),
        in_specs=[blk, col, col, pl.BlockSpec((2, kh, GRID_W, nb), lambda p, r: (p, 0, 0, 0)), blk, blk, blk],
        out_specs=[blk, col, col, pl.BlockSpec((2, kh, GRID_W, nb), lambda p, r: (p, 0, 0, 0))],
        out_shape=[jax.ShapeDtypeStruct((S, naw), F32), jax.ShapeDtypeStruct((T, naw), F32),
                   jax.ShapeDtypeStruct((T, naw), F32), jax.ShapeDtypeStruct(bias.shape, F32)],
        compiler_params=_params(("parallel", "arbitrary")),
    )(qs, kb, vb, bias, do, o, lse)


def _hg_cols(naw, hgf, rev):
    qcol = (3 * naw) // hgf
    fcol = (3 * naw + hgf * (2 if rev else 1)) // hgf
    icol = (3 * naw + 3 * hgf) // hgf
    return qcol, fcol, icol


def _hg_chunk_order(S, L, rev):
    ncl, ncc = S // HG_CHUNK, L // HG_CHUNK
    nc = ncl + ncc

    def chunk_of(i):
        if rev:
            return nc - 1 - i
        return jnp.where(i < ncc, ncl + i, i - ncc)

    return nc, ncl, chunk_of


def _hg_gates(q, z, lb, rev):
    row = lax.broadcasted_iota(jnp.int32, (HG_CHUNK, HG_CHUNK), 0)
    colm = lax.broadcasted_iota(jnp.int32, (HG_CHUNK, HG_CHUNK), 1)
    tri = (colm >= row) if rev else (row >= colm)
    trif = tri.astype(F32)
    sig = _sigmoid(z)
    f = lb + (1.0 - lb) * sig
    lf = jnp.log(f)
    k = 1.0 - f
    cum = _dot(trif, lf, HI)
    mid = cum[HG_CHUNK // 2:HG_CHUNK // 2 + 1, :]
    last = cum[0:1, :] if rev else cum[HG_CHUNK - 1:HG_CHUNK, :]
    eq = jnp.exp(jnp.clip(cum - mid, -EXP_CLAMP, EXP_CLAMP))
    ek = jnp.exp(jnp.clip(mid - cum, -EXP_CLAMP, EXP_CLAMP))
    return tri, trif, sig, f, k, cum, last, eq, ek


def _hg_fwd(u, lbr, S, L, naw, hgf, rev):
    T = S + L
    nh = hgf // HG_DIM
    nc, ncl, chunk_of = _hg_chunk_order(S, L, rev)
    qcol, fcol, icol = _hg_cols(naw, hgf, rev)

    def body(q_ref, z_ref, v_ref, lb_ref, o_ref, st_ref, state):
        i = pl.program_id(0)

        @pl.when(i == 0)
        def _():
            state[...] = jnp.zeros_like(state)

        q, z, v = q_ref[...], z_ref[...], v_ref[...]
        tri, _, _, _, k, cum, last, eq, ek = _hg_gates(q, z, lb_ref[...], rev)
        qe, ke = (q * eq).astype(BF16), (k * ek).astype(BF16)
        qd, kd = (q * jnp.exp(cum)).astype(BF16), (k * jnp.exp(last - cum)).astype(BF16)
        v16, el = v.astype(BF16), jnp.exp(last)
        for h in range(nh):
            sl = slice(h * HG_DIM, (h + 1) * HG_DIM)
            a = jnp.where(tri, _dot_nt(qe[:, sl], ke[:, sl]), 0.0)
            s0 = state[h]
            st_ref[h] = s0
            o_ref[:, sl] = _dot(a.astype(BF16), v16[:, sl]) + _dot_nt(qd[:, sl], s0.astype(BF16))
            state[h] = s0 * el[:, sl] + _dot_tn(v16[:, sl], kd[:, sl])

    def blk(cb):
        return pl.BlockSpec((HG_CHUNK, hgf), lambda i: (chunk_of(i), cb))

    return pl.pallas_call(
        body, name="hg_fwd_rev" if rev else "hg_fwd", grid=(nc,),
        in_specs=[blk(qcol), blk(fcol), blk(icol), pl.BlockSpec((1, hgf), lambda i: (0, 0))],
        out_specs=[pl.BlockSpec((HG_CHUNK, hgf), lambda i: (chunk_of(i), 0)),
                   pl.BlockSpec((None, nh, HG_DIM, HG_DIM), lambda i: (chunk_of(i), 0, 0, 0))],
        out_shape=[jax.ShapeDtypeStruct((T, hgf), F32), jax.ShapeDtypeStruct((nc, nh, HG_DIM, HG_DIM), F32)],
        scratch_shapes=[pltpu.VMEM((nh, HG_DIM, HG_DIM), F32)],
        compiler_params=_params(("arbitrary",)),
    )(u, u, u, lbr)


def _hg_bwd(u, lbr, st, do, S, L, naw, hgf, rev):
    T = S + L
    nh = hgf // HG_DIM
    nc, ncl, chunk_fwd = _hg_chunk_order(S, L, rev)
    qcol, fcol, icol = _hg_cols(naw, hgf, rev)

    def chunk_of(j):
        return chunk_fwd(nc - 1 - j)

    def body(q_ref, z_ref, v_ref, lb_ref, st_ref, do_ref, dq_ref, dz_ref, dv_ref, dlb_ref, dstate,
             dqe_s, dke_s, dqd_s, dkd_s, dl_s):
        j = pl.program_id(0)

        @pl.when(j == 0)
        def _():
            dstate[...] = jnp.zeros_like(dstate)
            dlb_ref[...] = jnp.zeros_like(dlb_ref)

        q, z, v = q_ref[...], z_ref[...], v_ref[...]
        lb = lb_ref[...]
        tri, trif, sig, f, k, cum, last, eq, ek = _hg_gates(q, z, lb, rev)
        ec, el, ekd = jnp.exp(cum), jnp.exp(last), jnp.exp(last - cum)
        qe, ke, qd, kd = q * eq, k * ek, q * ec, k * ekd
        qd16, kd16 = qd.astype(BF16), kd.astype(BF16)
        dout = jnp.where(chunk_of(j) < ncl, do_ref[...], 0.0)
        qe2, ke2, v2, dout2 = _split2(qe), _split2(ke), _split2(v), _split2(dout)
        for h in range(nh):
            sl = slice(h * HG_DIM, (h + 1) * HG_DIM)
            qeh, keh, vh, douth = [(t[0][:, sl], t[1][:, sl]) for t in (qe2, ke2, v2, dout2)]
            a = jnp.where(tri, _dot_nt(qeh[0], keh[0]), 0.0).astype(BF16)
            s0 = st_ref[h]
            ds1 = dstate[h]
            s02, ds12 = _split2(s0), _split2(ds1)
            dv_ref[:, sl] = _dot_tn(a, douth[0]) + _dot_nt(kd16[:, sl], ds12[0])
            da2 = _split2(jnp.where(tri, _dot_x3(_dot_nt, douth, vh), 0.0))
            dqe_s[:, sl] = _dot_x3(_dot, da2, keh)
            dke_s[:, sl] = _dot_x3(_dot_tn, da2, qeh)
            dqd_s[:, sl] = _dot_x3(_dot, douth, s02)
            dkd_s[:, sl] = _dot_x3(_dot, vh, ds12)
            dl_s[:, sl] = _colsum(ds1 * s0)
            dstate[h] = _dot_tn(douth[0], qd16[:, sl]) + ds1 * el[:, sl]
        dqe, dke, dqd, dkd = dqe_s[...], dke_s[...], dqd_s[...], dkd_s[...]
        dq_ref[...] = dqe * eq + dqd * ec
        dk = dke * ek + dkd * ekd
        dcum = dqe * qe - dke * ke + dqd * qd - dkd * kd
        dlast = _colsum(dkd * kd) + el * dl_s[...]
        dlf = _dot_tn(trif, dcum, HI) + dlast
        df = dlf / f - dk
        dz_ref[...] = df * (1.0 - lb) * sig * (1.0 - sig)
        dlb_ref[...] += _colsum(df * (1.0 - sig))

    def blk(cb):
        return pl.BlockSpec((HG_CHUNK, hgf), lambda j: (chunk_of(j), cb))

    oblk = pl.BlockSpec((HG_CHUNK, hgf), lambda j: (chunk_of(j), 0))
    wide = pltpu.VMEM((HG_CHUNK, hgf), F32)
    return pl.pallas_call(
        body, name="hg_bwd_rev" if rev else "hg_bwd", grid=(nc,),
        in_specs=[blk(qcol), blk(fcol), blk(icol), pl.BlockSpec((1, hgf), lambda j: (0, 0)),
                  pl.BlockSpec((None, nh, HG_DIM, HG_DIM), lambda j: (chunk_of(j), 0, 0, 0)),
                  pl.BlockSpec((HG_CHUNK, hgf), lambda j: (jnp.minimum(chunk_of(j), ncl - 1), 0))],
        out_specs=[oblk, oblk, oblk, pl.BlockSpec((1, hgf), lambda j: (0, 0))],
        out_shape=[jax.ShapeDtypeStruct((T, hgf), F32)] * 3 + [jax.ShapeDtypeStruct((1, hgf), F32)],
        scratch_shapes=[pltpu.VMEM((nh, HG_DIM, HG_DIM), F32), wide, wide, wide, wide,
                        pltpu.VMEM((1, hgf), F32)],
        compiler_params=_params(("arbitrary",)),
    )(u, u, u, lbr, st, do)


def _adamw(w, g, m, v, name):
    shape = w.shape
    if w.ndim != 2 or shape[0] % 8 or shape[1] % LANES:
        w, g, m, v = [a.reshape(1, -1) for a in (w, g, m, v)]
    r, cw = w.shape
    tm = _tile(r, max(8, (1 << 19) // cw), 8) if r % 8 == 0 else r
    c1 = 1.0 / (1.0 - ADAM_B1 ** ADAM_STEP)
    c2 = 1.0 / (1.0 - ADAM_B2 ** ADAM_STEP)

    def body(w_ref, g_ref, m_ref, v_ref, d_ref, nm_ref, nv_ref):
        gg = g_ref[...]
        nm = ADAM_B1 * m_ref[...] + (1.0 - ADAM_B1) * gg
        nv = ADAM_B2 * v_ref[...] + (1.0 - ADAM_B2) * (gg * gg)
        d_ref[...] = -ADAM_LR * ((nm * c1) / (jnp.sqrt(nv * c2) + ADAM_EPS) + ADAM_WD * w_ref[...])
        nm_ref[...] = nm
        nv_ref[...] = nv

    spec = pl.BlockSpec((tm, cw), lambda i: (i, 0))
    outs = pl.pallas_call(
        body, name=name, grid=(r // tm,), in_specs=[spec] * 4, out_specs=[spec] * 3,
        out_shape=[jax.ShapeDtypeStruct((r, cw), F32)] * 3,
        compiler_params=_params(("parallel",)),
    )(w, g, m, v)
    return [o.reshape(shape) for o in outs]


def kernel(x, c, ctx, c_ctx, w_ada, b_ada, norm1_g, w_in, na_rpb, hg_lb_logits, hg_norm_g, w_pa, w_pb, w_out, norm2_g, w_ffn_in, w_ffn_out, final_g, loss_target, m_c_ctx, m_w_ada, m_b_ada, m_norm1_g, m_w_in, m_na_rpb, m_hg_lb_logits, m_hg_norm_g, m_w_pa, m_w_pb, m_w_out, m_norm2_g, m_w_ffn_in, m_w_ffn_out, m_final_g, v_c_ctx, v_w_ada, v_b_ada, v_norm1_g, v_w_in, v_na_rpb, v_hg_lb_logits, v_hg_norm_g, v_w_pa, v_w_pb, v_w_out, v_norm2_g, v_w_ffn_in, v_w_ffn_out, v_final_g):
    xi, yi, ci = lax.axis_index("x"), lax.axis_index("y"), lax.axis_index("c")
    sidx = 2 * xi + yi
    eidx = 4 * xi + 2 * yi + ci

    S, D = x.shape[1], x.shape[2]
    L = ctx.shape[1]
    T = S + L
    naw = NA_HEADS * NA_HEAD_DIM
    hgf = HG_HEADS * HG_DIM
    inw = 3 * naw + 5 * hgf + 2 * D
    fh = w_ffn_out.shape[1] * 4
    ads = w_ada.shape[2]
    fs = hg_lb_logits.shape[2]
    rows = S // GRID_W
    tr = _tile(L, 256)
    nlat, nall = S // tr, T // tr
    assert naw == hgf and D % naw == 0 and S % tr == 0 and 2 * hgf <= D

    pack0 = jnp.concatenate([c, jnp.pad(hg_lb_logits.reshape(1, -1), ((0, 0), (0, D - 4 * fs))),
                             jnp.zeros((6, D), F32)], axis=0)
    g0 = _all_gather8(pack0).reshape(8, 8, D)
    cs = g0[:, 0]
    lbl = g0[::2, 1, :4 * fs].reshape(4, 2, 2, fs).transpose(1, 2, 0, 3).reshape(2, 2, 4 * fs)
    p_lb = jax.nn.softmax(lbl, axis=0)
    lb = p_lb[0]
    lbb = [lb[d].reshape(1, hgf) for d in range(2)]

    cin = jnp.concatenate([cs, c_ctx[None], jnp.zeros((7, D), F32)], axis=0)
    b_sh = lax.dynamic_slice(b_ada, (0, sidx * ads), (1, ads))
    modp = _ada_fwd(cin, w_ada[0], b_sh)
    modfull = _all_gather8(modp).reshape(8, 16, ads)[::2].transpose(1, 0, 2).reshape(16, 4 * ads)
    mod_e = jnp.pad(lax.dynamic_index_in_dim(modfull, eidx, 0, keepdims=False).reshape(N_MOD, D), ((0, 2), (0, 0)))
    mod_c = jnp.pad(modfull[8].reshape(N_MOD, D), ((0, 2), (0, 0)))

    names = ["w_in", "w_pa", "w_pb", "w_out", "w_ffn_in", "w_ffn_out"]
    placed = [_cast_place(w[0], "cast_" + nm)
              for w, nm in zip((w_in, w_pa, w_pb, w_out, w_ffn_in, w_ffn_out), names)]
    def shards(g):
        return g.reshape(4, 2 * g.shape[2], g.shape[3])

    win3 = shards(_exchange_call(_exchange_gather(placed[:1]), "gather_w_in")[0])

    xx =jnp.concatenate([x[0], ctx[0]], axis=0)

    def f_ln1(i, rv, vv):
        xt, = rv
        g, me, mc = vv
        isc = i >= nlat
        sh = jnp.where(isc, mc[0:1], me[0:1])
        sc = jnp.where(isc, mc[1:2], me[1:2])
        h = xt * _rms(xt) * g * (1.0 + sc) + sh
        return [h, h], []

    hb, hbt = _rowwise(f_ln1, nall, tr, [(xx, D, 0, None)], [norm1_g, mod_e, mod_c],
                       [(D, BF16), (D, BF16, "T")], [], "ln1")
    u, gathered = _mm_nn(hb, win3, F32, "mm_in", carry=_exchange_gather(placed[1:]))
    wpa3, wpb3, wout3, wi3, wfo3 = [shards(g) for g in gathered]
    wout1 = wout3.reshape(1, D, D)
    wfo1 = wfo3.reshape(1, fh, D)

    scale = NA_HEAD_DIM ** -0.5

    def f_qkv(i, rv, vv):
        q, k, v = rv
        return [q * scale, k, v], []

    qs, kb, vb = _rowwise(f_qkv, nall, tr, [(u, naw, 0, None), (u, naw, 1, None), (u, naw, 2, None)], [],
                          [(naw, BF16)] * 3, [], "qkv_cast")
    bias = _bias_tables(na_rpb[0], rows)
    o_na, lse = _na_fwd(qs, kb, vb, bias, S, L)

    o_f, st_f = _hg_fwd(u, lbb[0], S, L, naw, hgf, False)
    o_b, st_b = _hg_fwd(u, lbb[1], S, L, naw, hgf, True)

    hgn = jnp.tile(hg_norm_g, (1, HG_HEADS))
    hog_cb = (3 * naw + 4 * hgf) // hgf
    ga_cb = (3 * naw + 5 * hgf) // D
    gb_cb = ga_cb + 1

    def heads_rms(o):
        return jnp.concatenate([jnp.broadcast_to(_rms(o[:, h * HG_DIM:(h + 1) * HG_DIM]), (o.shape[0], HG_DIM))
                                for h in range(HG_HEADS)], axis=1)

    def f_readout(i, rv, vv):
        of, ob_, hog = rv
        g, = vv
        o = of + ob_
        return [o * heads_rms(o) * g * _silu(hog)], []

    ob, = _rowwise(f_readout, nlat, tr, [(o_f, hgf, 0, None), (o_b, hgf, 0, None), (u, hgf, hog_cb, None)],
                   [hgn], [(hgf, BF16)], [], "hg_readout")

    ya = _mm_nn(o_na, wpa3, F32, "mm_pa")
    yb = _mm_nn(ob, wpb3, F32, "mm_pb")

    def f_merge(i, rv, vv):
        ya_, yb_, ga, gb = rv
        return [_sigmoid(ga) * ya_ + _sigmoid(gb) * yb_], []

    yv, = _rowwise(f_merge, nlat, tr, [(ya, D, 0, None), (yb, D, 0, None), (u, D, ga_cb, None), (u, D, gb_cb, None)],
                   [], [(D, BF16)], [], "merge")
    z = _mm_nn(yv, wout1, F32, "mm_out")

    def f_res1(i, rv, vv):
        xt, zt = rv
        g, me = vv
        x1 = xt + me[2:3] * zt
        h = x1 * _rms(x1) * g * (1.0 + me[4:5]) + me[3:4]
        return [x1, h, h], []

    x1, h2, h2t = _rowwise(f_res1, nlat, tr, [(xx, D, 0, None), (z, D, 0, None)], [norm2_g, mod_e],
                           [(D, F32), (D, BF16), (D, BF16, "T")], [], "res1_ln2")
    au3, sw, swt = _ffn_in_fused(h2, wi3, "mm_ffn_in")
    ff =_mm_nn(sw, wfo1, F32, "mm_ffn_out")

    fg = final_g.reshape(1, D)

    def f_final(i, rv, vv):
        x1t, ft, tg = rv
        g, me = vv
        x2 = x1t + me[5:6] * ft
        r3 = _rms(x2)
        xn = x2 * r3
        err = xn * g - tg
        dyy = err * (1.0 / D)
        dxn = dyy * g
        dx2 = r3 * (dxn - xn * jnp.mean(dxn * xn, axis=-1, keepdims=True))
        return [dx2, dx2 * me[5:6]], [_colsum(err * err), _colsum(dyy * xn), _colsum(dx2 * ft)]

    dx2, dfb, loss_cols, dfg, dg2 = _rowwise(
        f_final, nlat, tr, [(x1, D, 0, None), (ff, D, 0, None), (loss_target[0], D, 0, None)], [fg, mod_e],
        [(D, F32), (D, BF16)], [D, D, D], "final_loss")

    def dswiglu_epi(tm, tn):
        blk = pl.BlockSpec((2, tm, tn), lambda m, n, k: (0, m, n))

        def fn(d, ins):
            a, uu = ins[0][0], ins[0][1]
            return [(d * uu * _dsilu(a), d * _silu(a))]

        return dict(ins=[(au3, blk)], outs=[(jax.ShapeDtypeStruct((2, S, fh), BF16), blk)], fn=fn)

    dau3, = _mm_nt(dfb, wfo1, BF16, "mm_d_sw", epi=dswiglu_epi, tn_target=512)
    g_wfo = _mm_tn(swt, dfb, 1, "mm_dw_ffn_out", a_is_t=True).reshape(4, 2, fh // 8, D)
    dh2 = _mm_nt(dau3, wi3, F32, "mm_d_h2")
    g_wi = _mm_tn(h2t, dau3, 4, "mm_dw_ffn_in", a_is_t=True)

    def f_ln2_bwd(i, rv, vv):
        dh, x1t, dx2t, zt = rv
        g, me = vv
        r2 = _rms(x1t)
        xn = x1t * r2
        dxn = dh * g * (1.0 + me[4:5])
        dx1 = dx2t + r2 * (dxn - xn * jnp.mean(dxn * xn, axis=-1, keepdims=True))
        return ([dx1, dx1 * me[2:3]],
                [_colsum(dh), _colsum(dh * xn * g), _colsum(dh * xn * (1.0 + me[4:5])), _colsum(dx1 * zt)])

    dx1, dzb, dsh2, dsc2, dn2g, dg1 = _rowwise(
        f_ln2_bwd, nlat, tr, [(dh2, D, 0, None), (x1, D, 0, None), (dx2, D, 0, None), (z, D, 0, None)],
        [norm2_g, mod_e], [(D, F32), (D, BF16)], [D, D, D, D], "ln2_bwd")

    g_wout = _mm_tn(yv, dzb, 1, "mm_dw_out").reshape(4, 2, D // 8, D)

    def dmerge_epi(tm, tn):
        blk = pl.BlockSpec((tm, tn), lambda m, n, k: (m, n))

        def gate(cb):
            return pl.BlockSpec((tm, tn), lambda m, n, k: (m, cb * (D // tn) + n))

        def fn(d, ins):
            ya_, yb_, ga, gb = ins
            sa, sb_ = _sigmoid(ga), _sigmoid(gb)
            return [d * sa, d * sb_, d * ya_ * sa * (1.0 - sa), d * yb_ * sb_ * (1.0 - sb_)]

        return dict(ins=[(ya, blk), (yb, blk), (u, gate(ga_cb)), (u, gate(gb_cb))],
                    outs=[(jax.ShapeDtypeStruct((S, D), BF16), blk)] * 4, fn=fn)

    dya, dyb, dga, dgb = _mm_nt(dzb, wout1, BF16, "mm_d_y", epi=dmerge_epi, tn_target=512)
    d_ona = _mm_nt(dya, wpa3, BF16, "mm_d_ona")
    d_ob = _mm_nt(dyb, wpb3, F32, "mm_d_ob")
    g_wpa = _mm_tn(o_na, dya, 4, "mm_dw_pa")
    g_wpb = _mm_tn(ob, dyb, 4, "mm_dw_pb")

    def f_dreadout(i, rv, vv):
        d, of, ob_, hog = rv
        g, = vv
        o = of + ob_
        on = o * heads_rms(o)
        t = d * _silu(hog) * g
        mt = jnp.concatenate([jnp.broadcast_to(jnp.mean((t * on)[:, h * HG_DIM:(h + 1) * HG_DIM], axis=-1,
                                                        keepdims=True), (o.shape[0], HG_DIM))
                              for h in range(HG_HEADS)], axis=1)
        do_ = heads_rms(o) * (t - on * mt)
        return [do_, d * on * g * _dsilu(hog)], [_colsum(d * _silu(hog) * on)]

    do_hg, dhog, dhgn = _rowwise(
        f_dreadout, nlat, tr, [(d_ob, hgf, 0, None), (o_f, hgf, 0, None), (o_b, hgf, 0, None),
                               (u, hgf, hog_cb, None)], [hgn], [(hgf, F32), (hgf, BF16)], [hgf], "hg_readout_bwd")

    dq_f, dz_f, dv_f, dlb_f = _hg_bwd(u, lbb[0], st_f, do_hg, S, L, naw, hgf, False)
    dq_b, dz_b, dv_b, dlb_b = _hg_bwd(u, lbb[1], st_b, do_hg, S, L, naw, hgf, True)
    dq_na, dk_na, dv_na, dbias = _na_bwd(qs, kb, vb, bias, d_ona, o_na, lse, S, L)

    ta = _tile(L, 128)
    nla, naa = S // ta, T // ta
    lat = lambda i: jnp.minimum(i, nla - 1)

    def f_assemble(i, rv, vv):
        dqn, dk, dv, dqf, dqb, dzf, dzb_, dvf, dvb, dho, dga_, dgb_ = rv
        keep = (i < nla).astype(F32)
        return [jnp.concatenate([dqn * (scale * keep), dk, dv, dqf + dqb, dzf, dzb_, dvf + dvb,
                                 dho.astype(F32) * keep, dga_.astype(F32) * keep, dgb_.astype(F32) * keep],
                                axis=1)], []

    du, = _rowwise(
        f_assemble, naa, ta,
        [(dq_na, naw, 0, lat), (dk_na, naw, 0, None), (dv_na, naw, 0, None), (dq_f, hgf, 0, None),
         (dq_b, hgf, 0, None), (dz_f, hgf, 0, None), (dz_b, hgf, 0, None), (dv_f, hgf, 0, None),
         (dv_b, hgf, 0, None), (dhog, hgf, 0, lat), (dga, D, 0, lat), (dgb, D, 0, lat)],
        [], [(inw, BF16)], [], "assemble_du")

    def add_half(g, land, nm):
        return _add_own_half(g, land, "rs_add_" + nm)

    early = [g_wpa, g_wpb, g_wout, g_wi, g_wfo]
    g_win_other, lands = _mm_tn_half(hbt, du, 4, True, "mm_dw_in_other", carry=_exchange_swap_other_half(early))
    parts = [add_half(g, l, nm) for g, l, nm in zip(early, lands, names[1:])]
    g_win_mine, landed = _mm_tn_half(hbt, du, 4, False, "mm_dw_in_mine", carry=_exchange_join(
        _exchange_scatter(parts[3:4]), _exchange_swap_other_half([g_win_other])))
    piece_wi = landed[0]
    parts = [add_half(g_win_mine, landed[1], names[0])] + parts
    dh, landed = _mm_nt(du, win3, F32, "mm_d_h", carry=_exchange_scatter(parts[:4] + parts[5:]))
    pieces = landed[:4] + [piece_wi] + landed[4:]
    halves = [_sum_pieces(p, l, "rs_sum_" + nm) for p, l, nm in zip(parts, pieces, names)]
    g_win, g_wpa, g_wpb, g_wout, g_wi, g_wfo = [
        f.reshape(2 * f.shape[1], f.shape[2])
        for f in _exchange_call(_exchange_swap_result(halves), "rs_swap_result_half")]

    def f_ln1_bwd(i, rv, vv):
        dht, xt, dx1t = rv
        g, me = vv
        r1 = _rms(xt)
        xn = xt * r1
        dxn = dht * g * (1.0 + me[1:2])
        dx = dx1t + r1 * (dxn - xn * jnp.mean(dxn * xn, axis=-1, keepdims=True))
        return [dx], [_colsum(dht), _colsum(dht * xn * g), _colsum(dht * xn * (1.0 + me[1:2]))]

    grad_x, dsh1, dsc1, dn1g_l = _rowwise(
        f_ln1_bwd, nlat, tr, [(dh, D, 0, None), (xx, D, 0, None), (dx1, D, 0, None)], [norm1_g, mod_e],
        [(D, F32)], [D, D, D], "ln1_bwd")

    def f_ln1_bwd_ctx(i, rv, vv):
        dht, xt = rv
        g, mc = vv
        xn = xt * _rms(xt)
        return [], [_colsum(dht), _colsum(dht * xn * g), _colsum(dht * xn * (1.0 + mc[1:2]))]

    ctx_rows = lambda i: i + nlat
    dsh1c, dsc1c, dn1g_c = _rowwise(
        f_ln1_bwd_ctx, nall - nlat, tr, [(dh, D, 0, ctx_rows), (xx, D, 0, ctx_rows)], [norm1_g, mod_c],
        [], [D, D, D], "ln1_bwd_ctx")

    drpb = _bias_tables_transpose(dbias, rows).reshape(1, -1)
    nrp = -(-drpb.shape[1] // D)
    drpb_rows = jnp.pad(drpb, ((0, 0), (0, nrp * D - drpb.shape[1]))).reshape(nrp, D)
    dlb = jnp.concatenate([dlb_f, dlb_b], axis=1)
    dhg = jnp.sum(dhgn.reshape(HG_HEADS, HG_DIM), axis=0, keepdims=True)

    def wide(v):
        return jnp.pad(v, ((0, 0), (0, D - v.shape[1])))

    pack_rows = [loss_cols, dfg, dn2g, dn1g_l + dn1g_c, dsh1, dsc1, dg1, dsh2, dsc2, dg2, dsh1c, dsc1c,
                 wide(dhg), wide(dlb), drpb_rows]
    pack = jnp.concatenate(pack_rows, axis=0)
    npk = -(-pack.shape[0] // 8) * 8
    pack = jnp.pad(pack, ((0, npk - pack.shape[0]), (0, 0)))
    gp = _all_gather8(pack).reshape(8, npk, D)
    tot = _sum8(gp, "sum_small_grads")

    loss = (0.5 / D) * jnp.sum(tot[0])
    grad_final_g = tot[1]
    grad_norm2_g = tot[2:3]
    grad_norm1_g = tot[3:4]
    grad_hg_norm_g = tot[12:13, :HG_DIM]
    dlb_tot = tot[13, :2 * hgf].reshape(2, hgf)
    grad_na_rpb = tot[14:14 + nrp].reshape(-1)[:drpb.shape[1]].reshape(na_rpb.shape)
    dlog = jnp.stack([dlb_tot * p_lb[0] * (1.0 - p_lb[0]), -dlb_tot * p_lb[0] * p_lb[1]], axis=0)
    grad_hg_lb = lax.dynamic_slice(dlog, (0, 0, sidx * fs), (2, 2, fs))

    dmod_all = gp[:, 4:10].reshape(8, N_MOD * D)
    dmod_ctx = jnp.concatenate([tot[10], tot[11], jnp.zeros((4 * D,), F32)])[None]
    dm16 = jnp.concatenate([dmod_all, dmod_ctx, jnp.zeros((7, N_MOD * D), F32)], axis=0)
    grad_b_ada = jnp.sum(dm16, axis=0, keepdims=True)
    dm_sh = lax.dynamic_slice(dm16, (0, sidx * ads), (16, ads))
    g_wada, dcin = _ada_bwd(cin, w_ada[0], dm_sh)
    gc = _all_gather8(dcin[8:16]).reshape(8, 8, D)
    grad_c_ctx = (gc[0, 0] + gc[2, 0] + gc[4, 0] + gc[6, 0]) * _dsilu(c_ctx)

    grads = {
        "c_ctx": grad_c_ctx, "w_ada": g_wada[None], "b_ada": grad_b_ada, "norm1_g": grad_norm1_g,
        "w_in": g_win[None], "na_rpb": grad_na_rpb, "hg_lb_logits": grad_hg_lb, "hg_norm_g": grad_hg_norm_g,
        "w_pa": g_wpa[None], "w_pb": g_wpb[None], "w_out": g_wout[None], "norm2_g": grad_norm2_g,
        "w_ffn_in": g_wi[None], "w_ffn_out": g_wfo[None], "final_g": grad_final_g,
    }
    weights = {
        "c_ctx": (c_ctx, m_c_ctx, v_c_ctx), "w_ada": (w_ada, m_w_ada, v_w_ada), "b_ada": (b_ada, m_b_ada, v_b_ada),
        "norm1_g": (norm1_g, m_norm1_g, v_norm1_g), "w_in": (w_in, m_w_in, v_w_in),
        "na_rpb": (na_rpb, m_na_rpb, v_na_rpb), "hg_lb_logits": (hg_lb_logits, m_hg_lb_logits, v_hg_lb_logits),
        "hg_norm_g": (hg_norm_g, m_hg_norm_g, v_hg_norm_g), "w_pa": (w_pa, m_w_pa, v_w_pa),
        "w_pb": (w_pb, m_w_pb, v_w_pb), "w_out": (w_out, m_w_out, v_w_out),
        "norm2_g": (norm2_g, m_norm2_g, v_norm2_g), "w_ffn_in": (w_ffn_in, m_w_ffn_in, v_w_ffn_in),
        "w_ffn_out": (w_ffn_out, m_w_ffn_out, v_w_ffn_out), "final_g": (final_g, m_final_g, v_final_g),
    }
    order = list(weights)
    deltas, new_ms, new_vs = [], [], []
    for nm in order:
        w, m, v = weights[nm]
        g = grads[nm].reshape(w.shape)
        grads[nm] = g
        if w.ndim == 3 and w.shape[0] == 1:
            d_, m_, v_ = _adamw(w[0], g[0], m[0], v[0], "adamw_" + nm)
            d_, m_, v_ = d_[None], m_[None], v_[None]
        else:
            d_, m_, v_ = _adamw(w, g, m, v, "adamw_" + nm)
        deltas.append(d_)
        new_ms.append(m_)
        new_vs.append(v_)

    return (loss, grad_x[None], *[grads[nm] for nm in order], *deltas, *new_ms, *new_vs)
```

```python
import numpy as np

import jax
import jax.numpy as jnp
from jax import lax
from jax.experimental import pallas as pl
from jax.experimental.pallas import tpu as pltpu

F32 = jnp.float32
BF16 = jnp.bfloat16

GRID_W = 64
WIN_H = 8
WIN_W = 16
NA_HEADS = 16
NA_HEAD_DIM = 64
HG_HEADS = 8
HG_DIM = 128
HG_CHUNK = 64
N_MOD = 6
EPS = 1e-6
ADAM_LR = 0.001
ADAM_B1 = 0.9
ADAM_B2 = 0.999
ADAM_EPS = 1e-08
ADAM_WD = 0.01
ADAM_STEP = 10

LANES = 128
NA_ROWS_PER_STEP = 8
VMEM_LIMIT = 56 * 1024 * 1024
MASK_VALUE = -1e30
EXP_CLAMP = 80.0
MESH_ID = pl.DeviceIdType.MESH
HI = lax.Precision.HIGHEST


def _tile(dim, target, mult=LANES):
    best = None
    t = mult
    while t <= min(dim, target):
        if dim % t == 0:
            best = t
        t += mult
    assert best is not None, (dim, target, mult)
    return best


def _params(sem):
    return pltpu.CompilerParams(dimension_semantics=sem, vmem_limit_bytes=VMEM_LIMIT)


def _dot(a, b, precision=None):
    return jnp.dot(a, b, preferred_element_type=F32, precision=precision)


def _dot_nt(a, b, precision=None):
    return lax.dot_general(a, b, (((1,), (1,)), ((), ())), preferred_element_type=F32, precision=precision)


def _dot_tn(a, b, precision=None):
    return lax.dot_general(a, b, (((0,), (0,)), ((), ())), preferred_element_type=F32, precision=precision)


def _split2(v):
    hi = v.astype(BF16)
    return hi, (v - hi.astype(F32)).astype(BF16)


def _dot_x3(dot, a2, b2):
    return dot(a2[0], b2[0]) + (dot(a2[0], b2[1]) + dot(a2[1], b2[0]))


def _sigmoid(v):
    return 1.0 / (1.0 + jnp.exp(-v))


def _mm_call(dot, operands, grid, in_specs, out_spec, out_shape, acc_shape, name, carry=None, epi=None):
    nk = grid[2]
    nci = 0 if carry is None else len(carry["ins"])
    nco = 0 if carry is None else len(carry["outs"])
    nei = 0 if epi is None else len(epi["ins"])
    neo = 1 if epi is None else len(epi["outs"])

    def body(*refs):
        a_ref, b_ref = refs[:2]
        ein = refs[2:2 + nei]
        cin = refs[2 + nei:2 + nei + nci]
        outs = refs[2 + nei + nci:2 + nei + nci + neo]
        cout = refs[2 + nei + nci + neo:2 + nei + nci + neo + nco]
        acc = refs[2 + nei + nci + neo + nco]
        sems = refs[3 + nei + nci + neo + nco:]
        m, n, k = pl.program_id(0), pl.program_id(1), pl.program_id(2)

        if carry is not None:
            @pl.when((m == 0) & (n == 0) & (k == 0))
            def _():
                carry["start"](cin, cout, *sems)

        @pl.when(k == 0)
        def _():
            acc[...] = jnp.zeros_like(acc)

        acc[...] += dot(a_ref[...], b_ref[...])

        @pl.when(k == nk - 1)
        def _():
            vals = [acc[...]] if epi is None else epi["fn"](acc[...], [r[...] for r in ein])
            for r, v in zip(outs, vals):
                if isinstance(v, tuple):
                    for i, vi in enumerate(v):
                        r[i] = vi.astype(r.dtype)
                else:
                    r[...] = v.astype(r.dtype)

        if carry is not None:
            @pl.when((m == grid[0] - 1) & (n == grid[1] - 1) & (k == nk - 1))
            def _():
                carry["finish"](cin, cout, *sems)

    any_spec = pl.BlockSpec(memory_space=pl.ANY)
    scratch = [pltpu.VMEM(acc_shape, F32)]
    extra = {}
    if carry is not None:
        scratch += [pltpu.SemaphoreType.DMA((carry["nsem"],)), pltpu.SemaphoreType.DMA((carry["nsem"],))]
        extra["input_output_aliases"] = {2 + nei + i: neo + j for i, j in carry["alias"].items()}
    sem = ("arbitrary",) * 3 if carry is not None else ("parallel", "parallel", "arbitrary")
    main_outs = [(out_shape, out_spec)] if epi is None else list(epi["outs"])
    res = pl.pallas_call(
        body, name=name, grid=grid,
        in_specs=list(in_specs) + ([] if epi is None else [sp for _, sp in epi["ins"]]) + [any_spec] * nci,
        out_specs=[sp for _, sp in main_outs] + [any_spec] * nco,
        out_shape=[sh for sh, _ in main_outs] + ([] if carry is None else list(carry["outs"])),
        scratch_shapes=scratch, compiler_params=_params(sem), **extra,
    )(*operands, *([] if epi is None else [ar for ar, _ in epi["ins"]]), *([] if carry is None else carry["ins"]))
    res = list(res)
    main = res[0] if epi is None else res[:neo]
    return main if carry is None else (main, res[neo:])


def _mm_nn(a, b3, out_dtype, name, carry=None):
    M, K = a.shape
    nsh, _, Ns = b3.shape
    tm, tn, tk = _tile(M, 1024), _tile(Ns, 1408), _tile(K, 2048)
    tps, nk = Ns // tn, K // tk
    return _mm_call(
        _dot, (a, b3), (M // tm, nsh * tps, nk),
        [pl.BlockSpec((tm, tk), lambda m, n, k: (m, k)),
         pl.BlockSpec((None, tk, tn), lambda m, n, k: (n // tps, k, n % tps))],
        pl.BlockSpec((tm, tn), lambda m, n, k: (m, n)),
        jax.ShapeDtypeStruct((M, nsh * Ns), out_dtype), (tm, tn), name, carry)


def _mm_nt(a, b3, out_dtype, name, carry=None, epi=None, tn_target=1408):
    a3 = a if a.ndim == 3 else a[None]
    na, M, Ka = a3.shape
    nsh, Kw, Ns = b3.shape
    assert na * Ka == nsh * Ns
    tm, tn, tk = _tile(M, 1024), _tile(Kw, tn_target), _tile(int(np.gcd(Ka, Ns)), 3072)
    kpa, kps = Ka // tk, Ns // tk
    return _mm_call(
        _dot_nt, (a3, b3), (M // tm, Kw // tn, nsh * kps),
        [pl.BlockSpec((None, tm, tk), lambda m, n, k: (k // kpa, m, k % kpa)),
         pl.BlockSpec((None, tn, tk), lambda m, n, k: (k // kps, n, k % kps))],
        pl.BlockSpec((tm, tn), lambda m, n, k: (m, n)),
        jax.ShapeDtypeStruct((M, Kw), out_dtype), (tm, tn), name, carry,
        None if epi is None else epi(tm, tn))


def _mm_tn(a, g, nsh, name, carry=None, a_is_t=False):
    Tk, M = a.shape[::-1] if a_is_t else a.shape
    g3 = g if g.ndim == 3 else g[None]
    ng, _, Ng = g3.shape
    Ns = ng * Ng // nsh
    tm, tn, tk = _tile(M // 2, 1408), _tile(int(np.gcd(Ng, Ns)), 1408), _tile(Tk, 2048)
    mh, tps, tpg, nk = (M // 2) // tm, Ns // tn, Ng // tn, Tk // tk
    return _mm_call(
        _dot if a_is_t else _dot_tn, (a, g3), (M // tm, nsh * tps, nk),
        [pl.BlockSpec((tm, tk), lambda m, n, k: (m, k)) if a_is_t else pl.BlockSpec((tk, tm), lambda m, n, k: (k, m)),
         pl.BlockSpec((None, tk, tn), lambda m, n, k: (n // tpg, k, n % tpg))],
        pl.BlockSpec((None, None, tm, tn), lambda m, n, k: (n // tps, m // mh, m % mh, n % tps)),
        jax.ShapeDtypeStruct((nsh, 2, M // 2, Ns), F32), (tm, tn), name, carry)


def _mm_tn_half(at, g, nsh, other, name, carry=None):
    M, Tk = at.shape
    Ns = g.shape[1] // nsh
    tm, tn, tk = _tile(M // 2, 1408), _tile(Ns, 1408), _tile(Tk, 2048)
    mh, tps, nk = (M // 2) // tm, Ns // tn, Tk // tk

    def half():
        c = lax.axis_index("c")
        return 1 - c if other else c

    return _mm_call(
        _dot, (at, g), (mh, nsh * tps, nk),
        [pl.BlockSpec((tm, tk), lambda m, n, k: (half() * mh + m, k)),
         pl.BlockSpec((tk, tn), lambda m, n, k: (k, n))],
        pl.BlockSpec((None, tm, tn), lambda m, n, k: (n // tps, m, n % tps)),
        jax.ShapeDtypeStruct((nsh, M // 2, Ns), F32), (tm, tn), name, carry)


def _ffn_in_fused(h2, wi3, name):
    M, K = h2.shape
    _, _, Ns = wi3.shape
    fh = 2 * Ns
    tm, tn = _tile(M, 1024), _tile(Ns, 512)
    tps = Ns // tn

    def body(h_ref, ba_ref, bu_ref, au_ref, sw_ref, swt_ref):
        h = h_ref[...]
        a, u = _dot(h, ba_ref[...]), _dot(h, bu_ref[...])
        au_ref[0] = a
        au_ref[1] = u
        sw = (_silu(a) * u).astype(sw_ref.dtype)
        sw_ref[...] = sw
        swt_ref[...] = sw.T

    return pl.pallas_call(
        body, name=name, grid=(M // tm, fh // tn),
        in_specs=[pl.BlockSpec((tm, K), lambda m, n: (m, 0)),
                  pl.BlockSpec((None, K, tn), lambda m, n: (n // tps, 0, n % tps)),
                  pl.BlockSpec((None, K, tn), lambda m, n: (2 + n // tps, 0, n % tps))],
        out_specs=[pl.BlockSpec((2, tm, tn), lambda m, n: (0, m, n)), pl.BlockSpec((tm, tn), lambda m, n: (m, n)),
                   pl.BlockSpec((tn, tm), lambda m, n: (n, m))],
        out_shape=[jax.ShapeDtypeStruct((2, M, fh), F32), jax.ShapeDtypeStruct((M, fh), BF16),
                   jax.ShapeDtypeStruct((fh, M), BF16)],
        compiler_params=_params(("parallel", "parallel")),
    )(h2, wi3, wi3)


def _rowwise(fn, nblk, tm, rins, vins, routs, accs, name):
    nr, nv, no, na = len(rins), len(vins), len(routs), len(accs)

    def body(*refs):
        i = pl.program_id(0)
        outs, accv = fn(i, [r[...] for r in refs[:nr]], [r[...] for r in refs[nr:nr + nv]])
        for r, v, spec in zip(refs[nr + nv:nr + nv + no], outs, routs):
            v = v.astype(r.dtype)
            r[...] = v.T if len(spec) == 3 else v
        arefs = refs[nr + nv + no:]
        if na:
            @pl.when(i == 0)
            def _():
                for a in arefs:
                    a[...] = jnp.zeros_like(a)

            for a, v in zip(arefs, accv):
                a[...] += v

    def row_spec(w, cb, rm):
        if rm is None:
            return pl.BlockSpec((tm, w), lambda i: (i, cb))
        return pl.BlockSpec((tm, w), lambda i: (rm(i), cb))

    in_specs = [row_spec(w, cb, rm) for (_, w, cb, rm) in rins]
    in_specs += [pl.BlockSpec(v.shape, lambda i: (0, 0)) for v in vins]
    def out_of(spec):
        w, dt = spec[:2]
        if len(spec) == 3:
            return pl.BlockSpec((w, tm), lambda i: (0, i)), jax.ShapeDtypeStruct((w, nblk * tm), dt)
        return pl.BlockSpec((tm, w), lambda i: (i, 0)), jax.ShapeDtypeStruct((nblk * tm, w), dt)

    out_specs = [out_of(sp)[0] for sp in routs] + [pl.BlockSpec((1, w), lambda i: (0, 0)) for w in accs]
    out_shape = [out_of(sp)[1] for sp in routs] + [jax.ShapeDtypeStruct((1, w), F32) for w in accs]
    res = pl.pallas_call(
        body, name=name, grid=(nblk,), in_specs=in_specs, out_specs=out_specs, out_shape=out_shape,
        compiler_params=_params(("arbitrary",)),
    )(*[r[0] for r in rins], *vins)
    return list(res)


def _colsum(v):
    return jnp.sum(v, axis=0, keepdims=True)


def _rms(v):
    return lax.rsqrt(jnp.mean(v * v, axis=-1, keepdims=True) + EPS)


def _all_gather8(xs):
    m_per, n = xs.shape

    def body(x_ref, out_ref, send_sems, recv_sems, local_sem):
        x, y, c = lax.axis_index("x"), lax.axis_index("y"), lax.axis_index("c")
        me, sibling = (x, y, c), (x, y, 1 - c)
        chips = [(1 - x, y), (x, 1 - y), (1 - x, 1 - y)]

        def rows(px, py, pc):
            return out_ref.at[pl.ds((4 * px + 2 * py + pc) * m_per, m_per), :]

        def copy(k, block, to, src=None):
            return pltpu.make_async_remote_copy(
                src_ref=rows(*block) if src is None else src, dst_ref=rows(*block),
                send_sem=send_sems.at[k], recv_sem=recv_sems.at[k], device_id=to, device_id_type=MESH_ID)

        mine = pltpu.make_async_copy(x_ref, rows(*me), local_sem)
        mine.start()
        first = [copy(0, me, sibling, src=x_ref)]
        first += [copy(1 + j, me, (*chip, c), src=x_ref) for j, chip in enumerate(chips)]
        for cp in first:
            cp.start()
        passed = [copy(4 + j, (*chip, c), sibling) for j, chip in enumerate(chips)]
        for j, chip in enumerate(chips):
            copy(1 + j, (*chip, c), me).wait_recv()
            passed[j].start()
        copy(0, sibling, me).wait_recv()
        for j, chip in enumerate(chips):
            copy(4 + j, (*chip, 1 - c), me).wait_recv()
        for cp in first + passed:
            cp.wait_send()
        mine.wait()

    return pl.pallas_call(
        body, name="all_gather8_%dx%d" % (m_per, n),
        out_shape=jax.ShapeDtypeStruct((8 * m_per, n), xs.dtype),
        in_specs=[pl.BlockSpec(memory_space=pltpu.VMEM)],
        out_specs=pl.BlockSpec(memory_space=pltpu.VMEM),
        scratch_shapes=[pltpu.SemaphoreType.DMA((7,)), pltpu.SemaphoreType.DMA((7,)), pltpu.SemaphoreType.DMA],
    )(xs)


def _mesh_pos():
    x, y, c = lax.axis_index("x"), lax.axis_index("y"), lax.axis_index("c")
    chips = [(1 - x, y), (x, 1 - y), (1 - x, 1 - y)]
    return x, y, c, chips


def _my_shard():
    return 2 * lax.axis_index("x") + lax.axis_index("y")


def _cast_place(w, name):
    r, cw = w.shape
    rh = r // 2
    tm = _tile(rh, max(16, (1 << 20) // (4 * cw)), 16)
    nt = rh // tm

    def body(w_ref, o_ref):
        o_ref[...] = w_ref[...].astype(o_ref.dtype)

    return pl.pallas_call(
        body, name=name, grid=(2, nt),
        in_specs=[pl.BlockSpec((tm, cw), lambda h, i: (h * nt + i, 0))],
        out_specs=pl.BlockSpec((None, None, tm, cw), lambda h, i: (_my_shard(), h, i, 0)),
        out_shape=jax.ShapeDtypeStruct((4, 2, rh, cw), BF16),
        compiler_params=_params(("parallel", "parallel")),
    )(w)


def _exchange_gather(bufs):
    n = len(bufs)

    def copies(out, send_sems, recv_sems, base):
        def copy(i, k, shard, half, to):
            dst = out[i].at[shard, half]
            return pltpu.make_async_remote_copy(
                src_ref=dst, dst_ref=dst, send_sem=send_sems.at[base + 6 * i + k],
                recv_sem=recv_sems.at[base + 6 * i + k], device_id=to, device_id_type=MESH_ID)
        return copy

    def first(copy):
        x, y, c, chips = _mesh_pos()
        return [copy(i, j, 2 * x + y, c, (*chip, c)) for i in range(n) for j, chip in enumerate(chips)]

    def start(cin, out, send_sems, recv_sems, base=0):
        for cp in first(copies(out, send_sems, recv_sems, base)):
            cp.start()

    def finish(cin, out, send_sems, recv_sems, base=0):
        copy = copies(out, send_sems, recv_sems, base)
        x, y, c, chips = _mesh_pos()
        passed = []
        for j, chip in enumerate(chips):
            sj = 2 * chip[0] + chip[1]
            for i in range(n):
                copy(i, j, sj, c, (x, y, c)).wait_recv()
                cp = copy(i, 3 + j, sj, c, (x, y, 1 - c))
                cp.start()
                passed.append(cp)
        for j, chip in enumerate(chips):
            sj = 2 * chip[0] + chip[1]
            for i in range(n):
                copy(i, 3 + j, sj, 1 - c, (x, y, c)).wait_recv()
        for cp in first(copy) + passed:
            cp.wait_send()

    return dict(ins=list(bufs), outs=[jax.ShapeDtypeStruct(b.shape, b.dtype) for b in bufs],
                alias={i: i for i in range(n)}, nsem=6 * n, start=start, finish=finish)


def _exchange_join(a, b):
    nai, nao = len(a["ins"]), len(a["outs"])

    def start(cin, cout, send_sems, recv_sems, base=0):
        a["start"](cin[:nai], cout[:nao], send_sems, recv_sems, base)
        b["start"](cin[nai:], cout[nao:], send_sems, recv_sems, base + a["nsem"])

    def finish(cin, cout, send_sems, recv_sems, base=0):
        a["finish"](cin[:nai], cout[:nao], send_sems, recv_sems, base)
        b["finish"](cin[nai:], cout[nao:], send_sems, recv_sems, base + a["nsem"])

    alias = dict(a["alias"])
    alias.update({nai + i: nao + j for i, j in b["alias"].items()})
    return dict(ins=a["ins"] + b["ins"], outs=a["outs"] + b["outs"], alias=alias, nsem=a["nsem"] + b["nsem"],
                start=start, finish=finish)


def _exchange_call(ex, name):
    nci, nco = len(ex["ins"]), len(ex["outs"])

    def body(*refs):
        cin, cout, sems = refs[:nci], refs[nci:nci + nco], refs[nci + nco:]
        ex["start"](cin, cout, *sems)
        ex["finish"](cin, cout, *sems)

    any_spec = pl.BlockSpec(memory_space=pl.ANY)
    return list(pl.pallas_call(
        body, name=name, out_shape=list(ex["outs"]), in_specs=[any_spec] * nci, out_specs=[any_spec] * nco,
        input_output_aliases=dict(ex["alias"]),
        scratch_shapes=[pltpu.SemaphoreType.DMA((ex["nsem"],)), pltpu.SemaphoreType.DMA((ex["nsem"],))],
    )(*ex["ins"]))


def _exchange_swap_other_half(gs):
    n = len(gs)

    def copies(g, land, send_sems, recv_sems, base):
        x, y, c, _ = _mesh_pos()
        return [pltpu.make_async_remote_copy(
            src_ref=g[i].at[:, 1 - c] if len(gs[i].shape) == 4 else g[i], dst_ref=land[i],
            send_sem=send_sems.at[base + i],
            recv_sem=recv_sems.at[base + i], device_id=(x, y, 1 - c), device_id_type=MESH_ID) for i in range(n)]

    def start(g, land, send_sems, recv_sems, base=0):
        for cp in copies(g, land, send_sems, recv_sems, base):
            cp.start()

    def finish(g, land, send_sems, recv_sems, base=0):
        for cp in copies(g, land, send_sems, recv_sems, base):
            cp.wait()

    return dict(ins=list(gs), outs=[jax.ShapeDtypeStruct((4,) + g.shape[-2:], g.dtype) for g in gs],
                alias={}, nsem=n, start=start, finish=finish)


def _exchange_scatter(ps):
    n = len(ps)

    def copies(p, land, send_sems, recv_sems, base):
        x, y, c, chips = _mesh_pos()
        return [pltpu.make_async_remote_copy(
            src_ref=p[i].at[2 * chip[0] + chip[1]], dst_ref=land[i].at[j],
            send_sem=send_sems.at[base + 3 * i + j], recv_sem=recv_sems.at[base + 3 * i + j],
            device_id=(*chip, c), device_id_type=MESH_ID) for i in range(n) for j, chip in enumerate(chips)]

    def start(p, land, send_sems, recv_sems, base=0):
        for cp in copies(p, land, send_sems, recv_sems, base):
            cp.start()

    def finish(p, land, send_sems, recv_sems, base=0):
        for cp in copies(p, land, send_sems, recv_sems, base):
            cp.wait()

    return dict(ins=list(ps), outs=[jax.ShapeDtypeStruct((3,) + p.shape[1:], p.dtype) for p in ps],
                alias={}, nsem=3 * n, start=start, finish=finish)


def _exchange_swap_result(bufs):
    n = len(bufs)

    def copies(out, send_sems, recv_sems, base, half):
        x, y, c, _ = _mesh_pos()
        h = c if half == "mine" else 1 - c
        return [pltpu.make_async_remote_copy(
            src_ref=out[i].at[h], dst_ref=out[i].at[h], send_sem=send_sems.at[base + i],
            recv_sem=recv_sems.at[base + i], device_id=(x, y, 1 - c), device_id_type=MESH_ID) for i in range(n)]

    def start(cin, out, send_sems, recv_sems, base=0):
        for cp in copies(out, send_sems, recv_sems, base, "mine"):
            cp.start()

    def finish(cin, out, send_sems, recv_sems, base=0):
        for cp in copies(out, send_sems, recv_sems, base, "theirs"):
            cp.wait_recv()
        for cp in copies(out, send_sems, recv_sems, base, "mine"):
            cp.wait_send()

    return dict(ins=list(bufs), outs=[jax.ShapeDtypeStruct(b.shape, b.dtype) for b in bufs],
                alias={i: i for i in range(n)}, nsem=n, start=start, finish=finish)


def _add_own_half(g, land, name):
    rh, cw = g.shape[-2:]
    tm = _tile(rh, max(16, (1 << 20) // (4 * cw)), 16)

    def body(g_ref, l_ref, o_ref):
        o_ref[...] = (g_ref[...] + l_ref[...]).astype(o_ref.dtype)

    mine = (pl.BlockSpec((None, None, tm, cw), lambda s, i: (s, lax.axis_index("c"), i, 0)) if g.ndim == 4
            else pl.BlockSpec((None, tm, cw), lambda s, i: (s, i, 0)))
    return pl.pallas_call(
        body, name=name, grid=(4, rh // tm),
        in_specs=[mine,
                  pl.BlockSpec((None, tm, cw), lambda s, i: (s, i, 0))],
        out_specs=pl.BlockSpec((None, tm, cw), lambda s, i: (s, i, 0)),
        out_shape=jax.ShapeDtypeStruct((4, rh, cw), BF16),
        compiler_params=_params(("parallel", "parallel")),
    )(g, land)


def _sum_pieces(part, land, name):
    _, rh, cw = land.shape
    tm = _tile(rh, max(16, (1 << 20) // (4 * cw)), 16)

    def body(p_ref, l_ref, o_ref):
        v = l_ref[...].astype(F32)
        o_ref[...] = (p_ref[...].astype(F32) + v[0]) + (v[1] + v[2])

    return pl.pallas_call(
        body, name=name, grid=(rh // tm,),
        in_specs=[pl.BlockSpec((None, tm, cw), lambda i: (_my_shard(), i, 0)),
                  pl.BlockSpec((3, tm, cw), lambda i: (0, i, 0))],
        out_specs=pl.BlockSpec((None, tm, cw), lambda i: (lax.axis_index("c"), i, 0)),
        out_shape=jax.ShapeDtypeStruct((2, rh, cw), F32),
        compiler_params=_params(("parallel",)),
    )(part, land)


def _sum8(g, name):
    def body(g_ref, o_ref):
        acc = g_ref[0]
        for k in range(1, 8):
            acc = acc + g_ref[k]
        o_ref[...] = acc

    return pl.pallas_call(body, name=name, out_shape=jax.ShapeDtypeStruct(g.shape[1:], F32))(g)


def _silu(v):
    return v * _sigmoid(v)


def _dsilu(v):
    s = _sigmoid(v)
    return s * (1.0 + v * (1.0 - s))


def _ada_fwd(cin, w, b):
    d, ns = w.shape
    tn = _tile(ns, 512)

    def body(c_ref, w_ref, b_ref, o_ref):
        o_ref[...] = _dot(_silu(c_ref[...]), w_ref[...], HI) + b_ref[...]

    return pl.pallas_call(
        body, name="ada_fwd", grid=(ns // tn,),
        in_specs=[pl.BlockSpec(cin.shape, lambda n: (0, 0)), pl.BlockSpec((d, tn), lambda n: (0, n)),
                  pl.BlockSpec((1, tn), lambda n: (0, n))],
        out_specs=pl.BlockSpec((cin.shape[0], tn), lambda n: (0, n)),
        out_shape=jax.ShapeDtypeStruct((cin.shape[0], ns), F32),
        compiler_params=_params(("parallel",)),
    )(cin, w, b)


def _ada_bwd(cin, w, dm):
    d, ns = w.shape
    tn = _tile(ns, 512)

    def body(c_ref, w_ref, d_ref, dw_ref, dc_ref):
        n = pl.program_id(0)

        @pl.when(n == 0)
        def _():
            dc_ref[...] = jnp.zeros_like(dc_ref)

        dw_ref[...] = _dot_tn(_silu(c_ref[...]), d_ref[...], HI)
        dc_ref[...] += _dot_nt(d_ref[...], w_ref[...], HI)

    return pl.pallas_call(
        body, name="ada_bwd", grid=(ns // tn,),
        in_specs=[pl.BlockSpec(cin.shape, lambda n: (0, 0)), pl.BlockSpec((d, tn), lambda n: (0, n)),
                  pl.BlockSpec((cin.shape[0], tn), lambda n: (0, n))],
        out_specs=[pl.BlockSpec((d, tn), lambda n: (0, n)), pl.BlockSpec(cin.shape, lambda n: (0, 0))],
        out_shape=[jax.ShapeDtypeStruct((d, ns), F32), jax.ShapeDtypeStruct(cin.shape, F32)],
        compiler_params=_params(("arbitrary",)),
    )(cin, w, dm)


def _bias_tables(rpb, rows):
    kh = min(WIN_H, rows)
    fold, onehot, in_win = _bias_selectors(kh)
    t = jnp.einsum("hdc,dsj->hsjc", rpb, jnp.asarray(fold), precision=HI)
    t = jnp.einsum("hsjc,cqk->hsqjk", t, jnp.asarray(onehot), precision=HI)
    t = jnp.where(jnp.asarray(in_win)[None, None, :, None, :], t, MASK_VALUE)
    return t.reshape(rpb.shape[0], kh, GRID_W, kh * GRID_W).astype(F32)


def _bias_selectors(kh):
    col = np.arange(GRID_W)
    col_start = np.clip(col - WIN_W // 2, 0, GRID_W - WIN_W)
    in_win = (col[None, :] >= col_start[:, None]) & (col[None, :] < col_start[:, None] + WIN_W)
    dc_idx = np.clip(col[None, :] - col[:, None], 1 - WIN_W, WIN_W - 1) + WIN_W - 1
    onehot = np.zeros((2 * WIN_W - 1, GRID_W, GRID_W), np.float32)
    qq, kk = np.nonzero(in_win)
    onehot[dc_idx[qq, kk], qq, kk] = 1.0
    fold = np.zeros((2 * WIN_H - 1, kh, kh), np.float32)
    for sh in range(kh):
        for j in range(kh):
            fold[j - sh + WIN_H - 1, sh, j] = 1.0
    return fold, onehot, in_win


def _mm_f32(a, b, name):
    M, K = a.shape
    N = b.shape[1]
    tm = _tile(M, 256, 8)

    def body(a_ref, b_ref, o_ref):
        o_ref[...] = _dot(a_ref[...], b_ref[...], HI)

    return pl.pallas_call(
        body, name=name, grid=(M // tm,),
        in_specs=[pl.BlockSpec((tm, K), lambda i: (i, 0)), pl.BlockSpec((K, N), lambda i: (0, 0))],
        out_specs=pl.BlockSpec((tm, N), lambda i: (i, 0)),
        out_shape=jax.ShapeDtypeStruct((M, N), F32),
        compiler_params=_params(("parallel",)),
    )(a, b)


def _bias_tables_transpose(dbias, rows):
    kh = min(WIN_H, rows)
    nh = dbias.shape[0]
    fold, onehot, _ = _bias_selectors(kh)
    ndc = onehot.shape[0]
    sel = np.zeros((GRID_W * GRID_W, LANES), np.float32)
    sel[:, :ndc] = onehot.reshape(ndc, -1).T
    x = dbias.reshape(nh, kh, GRID_W, kh, GRID_W).transpose(0, 1, 3, 2, 4).reshape(nh * kh * kh, GRID_W * GRID_W)
    z = _mm_f32(x, jnp.asarray(sel), "rpb_fold")[:, :ndc].reshape(nh, kh, kh, ndc)
    return jnp.einsum("dsj,hsjc->hdc", jnp.asarray(fold), z, precision=HI)


def _na_geometry(S):
    rows = S // GRID_W
    kh = min(WIN_H, rows)

    def row_start(r):
        return jnp.clip(r - kh // 2, 0, rows - kh)

    return rows, kh, row_start


def _na_by_head(ref, rr, lane):
    t = ref[rr * GRID_W:(rr + 1) * GRID_W, :]
    zero = jnp.zeros_like(t)
    return jnp.concatenate([jnp.where(lane < NA_HEAD_DIM, t, zero), jnp.where(lane >= NA_HEAD_DIM, t, zero)], axis=0)


def _na_pick_head(t2, lane):
    return jnp.where(lane < NA_HEAD_DIM, t2[:GRID_W], t2[GRID_W:])


def _na_scores(q_ref, k_ref, b_ref, i, nrs, nb, S, L, row_start, lane):
    qh = jnp.concatenate([_na_by_head(q_ref, rr, lane) for rr in range(nrs)], axis=0)
    sc = _dot_nt(qh, k_ref[pl.ds(S, L), :])
    starts, shifts, sb = [], [], []
    for rr in range(nrs):
        r = i * nrs + rr
        rs = row_start(r)
        starts.append(pl.multiple_of(rs * GRID_W, GRID_W))
        shifts.append(r - rs)
        bias = jnp.concatenate([b_ref[0, r - rs], b_ref[1, r - rs]], axis=0)
        sb.append(_dot_nt(qh[rr * 2 * GRID_W:(rr + 1) * 2 * GRID_W], k_ref[pl.ds(starts[-1], nb), :]) + bias)
    return qh, jnp.concatenate(sb, axis=0), sc, starts, shifts


def _na_fwd(qs, kb, vb, bias, S, L):
    T, naw = qs.shape
    rows, kh, row_start = _na_geometry(S)
    nb = kh * GRID_W
    npair = naw // LANES

    nrs = min(NA_ROWS_PER_STEP, rows)
    assert rows % nrs == 0

    def body(q_ref, k_ref, v_ref, b_ref, o_ref, lse_ref):
        i = pl.program_id(1)
        lane = lax.broadcasted_iota(jnp.int32, (GRID_W, LANES), 1)
        _, sb, sc, starts, _ = _na_scores(q_ref, k_ref, b_ref, i, nrs, nb, S, L, row_start, lane)
        m = jnp.maximum(jnp.max(sb, axis=-1, keepdims=True), jnp.max(sc, axis=-1, keepdims=True))
        pb, pc = jnp.exp(sb - m), jnp.exp(sc - m)
        l = jnp.sum(pb, axis=-1, keepdims=True) + jnp.sum(pc, axis=-1, keepdims=True)
        inv = 1.0 / l
        pb16, pc16 = (pb * inv).astype(BF16), (pc * inv).astype(BF16)
        oc = _dot(pc16, v_ref[pl.ds(S, L), :])
        lse = jnp.broadcast_to(m + jnp.log(l), oc.shape)
        for rr in range(nrs):
            two = slice(rr * 2 * GRID_W, (rr + 1) * 2 * GRID_W)
            rsl = slice(rr * GRID_W, (rr + 1) * GRID_W)
            o2 = oc[two] + _dot(pb16[two], v_ref[pl.ds(starts[rr], nb), :])
            o_ref[rsl, :] = _na_pick_head(o2, lane).astype(o_ref.dtype)
            lse_ref[rsl, :] = _na_pick_head(lse[two], lane)

    blk = pl.BlockSpec((nrs * GRID_W, LANES), lambda p, i: (i, p))
    col = pl.BlockSpec((T, LANES), lambda p, i: (0, p))
    return pl.pallas_call(
        body, name="na_fwd", grid=(npair, rows // nrs),
        in_specs=[blk, col, col, pl.BlockSpec((2, kh, GRID_W, nb), lambda p, i: (p, 0, 0, 0))],
        out_specs=[blk, blk],
        out_shape=[jax.ShapeDtypeStruct((S, naw), BF16), jax.ShapeDtypeStruct((S, naw), F32)],
        compiler_params=_params(("parallel", "arbitrary")),
    )(qs, kb, vb, bias)


def _na_bwd(qs, kb, vb, bias, do, o, lse, S, L):
    T, naw = qs.shape
    rows, kh, row_start = _na_geometry(S)
    nb = kh * GRID_W
    npair = naw // LANES

    nrs = min(NA_ROWS_PER_STEP, rows)
    assert rows % nrs == 0

    def body(q_ref, k_ref, v_ref, b_ref, do_ref, o_ref, lse_ref, dq_ref, dk_ref, dv_ref, db_ref):
        i = pl.program_id(1)

        @pl.when(i == 0)
        def _():
            dk_ref[...] = jnp.zeros_like(dk_ref)
            dv_ref[...] = jnp.zeros_like(dv_ref)
            db_ref[...] = jnp.zeros_like(db_ref)

        lane = lax.broadcasted_iota(jnp.int32, (GRID_W, LANES), 1)
        qh, sb, sc, starts, shifts = _na_scores(q_ref, k_ref, b_ref, i, nrs, nb, S, L, row_start, lane)
        doh = jnp.concatenate([_na_by_head(do_ref, rr, lane) for rr in range(nrs)], axis=0)
        o2 = jnp.concatenate([o_ref[rr * GRID_W:(rr + 1) * GRID_W, :] for rr in range(nrs) for _ in range(2)], axis=0)
        lse = jnp.concatenate([lse_ref[rr * GRID_W:(rr + 1) * GRID_W, :][:, hh * NA_HEAD_DIM:hh * NA_HEAD_DIM + 1]
                               for rr in range(nrs) for hh in range(2)], axis=0)
        pb, pc = jnp.exp(sb - lse), jnp.exp(sc - lse)
        delta = jnp.sum(doh.astype(F32) * o2.astype(F32), axis=-1, keepdims=True)
        dpb = jnp.concatenate([_dot_nt(doh[rr * 2 * GRID_W:(rr + 1) * 2 * GRID_W], v_ref[pl.ds(starts[rr], nb), :])
                               for rr in range(nrs)], axis=0)
        dsb = pb * (dpb - delta)
        dsc = pc * (_dot_nt(doh, v_ref[pl.ds(S, L), :]) - delta)
        dsb16, dsc16, pb16, pc16 = dsb.astype(BF16), dsc.astype(BF16), pb.astype(BF16), pc.astype(BF16)
        dqc = _dot(dsc16, k_ref[pl.ds(S, L), :])
        dk_ref[pl.ds(S, L), :] += _dot_tn(dsc16, qh)
        dv_ref[pl.ds(S, L), :] += _dot_tn(pc16, doh)
        for rr in range(nrs):
            two = slice(rr * 2 * GRID_W, (rr + 1) * 2 * GRID_W)
            band = pl.ds(starts[rr], nb)
            dq2 = dqc[two] + _dot(dsb16[two], k_ref[band, :])
            dq_ref[rr * GRID_W:(rr + 1) * GRID_W, :] = _na_pick_head(dq2, lane)
            dk_ref[band, :] += _dot_tn(dsb16[two], qh[two])
            dv_ref[band, :] += _dot_tn(pb16[two], doh[two])
            for hh in range(2):
                db_ref[hh, shifts[rr]] += dsb[(2 * rr + hh) * GRID_W:(2 * rr + hh + 1) * GRID_W]

    blk = pl.BlockSpec((nrs * GRID_W, LANES), lambda p, r: (r, p))
    col = pl.BlockSpec((T, LANES), lambda p, r: (0, p))
    return pl.pallas_call(
        body, name="na_bwd", grid=(npair, rows // nrs),
        in_specs=[blk, col, col, pl.BlockSpec((2, kh, GRID_W, nb), lambda p, r: (p, 0, 0, 0)), blk, blk, blk],
        out_specs=[blk, col, col, pl.BlockSpec((2, kh, GRID_W, nb), lambda p, r: (p, 0, 0, 0))],
        out_shape=[jax.ShapeDtypeStruct((S, naw), F32), jax.ShapeDtypeStruct((T, naw), F32),
                   jax.ShapeDtypeStruct((T, naw), F32), jax.ShapeDtypeStruct(bias.shape, F32)],
        compiler_params=_params(("parallel", "arbitrary")),
    )(qs, kb, vb, bias, do, o, lse)


def _hg_cols(naw, hgf, rev):
    qcol = (3 * naw) // hgf
    fcol = (3 * naw + hgf * (2 if rev else 1)) // hgf
    icol = (3 * naw + 3 * hgf) // hgf
    return qcol, fcol, icol


def _hg_chunk_order(S, L, rev):
    ncl, ncc = S // HG_CHUNK, L // HG_CHUNK
    nc = ncl + ncc

    def chunk_of(i):
        if rev:
            return nc - 1 - i
        return jnp.where(i < ncc, ncl + i, i - ncc)

    return nc, ncl, chunk_of


def _hg_gates(q, z, lb, rev):
    row = lax.broadcasted_iota(jnp.int32, (HG_CHUNK, HG_CHUNK), 0)
    colm = lax.broadcasted_iota(jnp.int32, (HG_CHUNK, HG_CHUNK), 1)
    tri = (colm >= row) if rev else (row >= colm)
    trif = tri.astype(F32)
    sig = _sigmoid(z)
    f = lb + (1.0 - lb) * sig
    lf = jnp.log(f)
    k = 1.0 - f
    cum = _dot(trif, lf, HI)
    mid = cum[HG_CHUNK // 2:HG_CHUNK // 2 + 1, :]
    last = cum[0:1, :] if rev else cum[HG_CHUNK - 1:HG_CHUNK, :]
    eq = jnp.exp(jnp.clip(cum - mid, -EXP_CLAMP, EXP_CLAMP))
    ek = jnp.exp(jnp.clip(mid - cum, -EXP_CLAMP, EXP_CLAMP))
    return tri, trif, sig, f, k, cum, last, eq, ek


def _hg_fwd(u, lbr, S, L, naw, hgf, rev):
    T = S + L
    nh = hgf // HG_DIM
    nc, ncl, chunk_of = _hg_chunk_order(S, L, rev)
    qcol, fcol, icol = _hg_cols(naw, hgf, rev)

    def step(i, q_ref, z_ref, v_ref, lb_ref, o_ref, st_ref, state):
        @pl.when(i == 0)
        def _():
            state[...] = jnp.zeros_like(state)

        q, z, v = q_ref[...], z_ref[...], v_ref[...]
        tri, _, _, _, k, cum, last, eq, ek = _hg_gates(q, z, lb_ref[...], rev)
        qe, ke = (q * eq).astype(BF16), (k * ek).astype(BF16)
        qd, kd = (q * jnp.exp(cum)).astype(BF16), (k * jnp.exp(last - cum)).astype(BF16)
        v16, el = v.astype(BF16), jnp.exp(last)
        for h in range(nh):
            sl = slice(h * HG_DIM, (h + 1) * HG_DIM)
            a = jnp.where(tri, _dot_nt(qe[:, sl], ke[:, sl]), 0.0)
            s0 = state[h]
            st_ref[h] = s0
            o_ref[:, sl] = _dot(a.astype(BF16), v16[:, sl]) + _dot_nt(qd[:, sl], s0.astype(BF16))
            state[h] = s0 * el[:, sl] + _dot_tn(v16[:, sl], kd[:, sl])

    def blk(cb):
        return pl.BlockSpec((HG_CHUNK, hgf), lambda i: (chunk_of(i), cb))

    return dict(
        step=step, nc=nc, operands=[u, u, u, lbr],
        in_specs=[blk(qcol), blk(fcol), blk(icol), pl.BlockSpec((1, hgf), lambda i: (0, 0))],
        out_specs=[pl.BlockSpec((HG_CHUNK, hgf), lambda i: (chunk_of(i), 0)),
                   pl.BlockSpec((None, nh, HG_DIM, HG_DIM), lambda i: (chunk_of(i), 0, 0, 0))],
        out_shape=[jax.ShapeDtypeStruct((T, hgf), F32), jax.ShapeDtypeStruct((nc, nh, HG_DIM, HG_DIM), F32)],
        scratch=[pltpu.VMEM((nh, HG_DIM, HG_DIM), F32)])


def _hg_both(parts, name):
    nin = [len(p["in_specs"]) for p in parts]
    nout = [len(p["out_specs"]) for p in parts]
    nscr = [len(p["scratch"]) for p in parts]

    def body(*refs):
        i = pl.program_id(0)
        ins, outs, scr = refs[:sum(nin)], refs[sum(nin):sum(nin) + sum(nout)], refs[sum(nin) + sum(nout):]
        for d, p in enumerate(parts):
            p["step"](i, *ins[sum(nin[:d]):sum(nin[:d + 1])], *outs[sum(nout[:d]):sum(nout[:d + 1])],
                      *scr[sum(nscr[:d]):sum(nscr[:d + 1])])

    res = pl.pallas_call(
        body, name=name, grid=(parts[0]["nc"],),
        in_specs=[sp for p in parts for sp in p["in_specs"]],
        out_specs=[sp for p in parts for sp in p["out_specs"]],
        out_shape=[sh for p in parts for sh in p["out_shape"]],
        scratch_shapes=[sc for p in parts for sc in p["scratch"]],
        compiler_params=_params(("arbitrary",)),
    )(*[op for p in parts for op in p["operands"]])
    return [list(res[sum(nout[:d]):sum(nout[:d + 1])]) for d in range(len(parts))]


def _hg_bwd(u, lbr, st, do, S, L, naw, hgf, rev):
    T = S + L
    nh = hgf // HG_DIM
    nc, ncl, chunk_fwd = _hg_chunk_order(S, L, rev)
    qcol, fcol, icol = _hg_cols(naw, hgf, rev)

    def chunk_of(j):
        return chunk_fwd(nc - 1 - j)

    def step(j, q_ref, z_ref, v_ref, lb_ref, st_ref, do_ref, dq_ref, dz_ref, dv_ref, dlb_ref, dstate,
             dqe_s, dke_s, dqd_s, dkd_s, dl_s):
        @pl.when(j == 0)
        def _():
            dstate[...] = jnp.zeros_like(dstate)
            dlb_ref[...] = jnp.zeros_like(dlb_ref)

        q, z, v = q_ref[...], z_ref[...], v_ref[...]
        lb = lb_ref[...]
        tri, trif, sig, f, k, cum, last, eq, ek = _hg_gates(q, z, lb, rev)
        ec, el, ekd = jnp.exp(cum), jnp.exp(last), jnp.exp(last - cum)
        qe, ke, qd, kd = q * eq, k * ek, q * ec, k * ekd
        qd16, kd16 = qd.astype(BF16), kd.astype(BF16)
        dout = jnp.where(chunk_of(j) < ncl, do_ref[...], 0.0)
        qe2, ke2, v2, dout2 = _split2(qe), _split2(ke), _split2(v), _split2(dout)
        for h in range(nh):
            sl = slice(h * HG_DIM, (h + 1) * HG_DIM)
            qeh, keh, vh, douth = [(t[0][:, sl], t[1][:, sl]) for t in (qe2, ke2, v2, dout2)]
            a = jnp.where(tri, _dot_nt(qeh[0], keh[0]), 0.0).astype(BF16)
            s0 = st_ref[h]
            ds1 = dstate[h]
            s02, ds12 = _split2(s0), _split2(ds1)
            dv_ref[:, sl] = _dot_tn(a, douth[0]) + _dot_nt(kd16[:, sl], ds12[0])
            da2 = _split2(jnp.where(tri, _dot_x3(_dot_nt, douth, vh), 0.0))
            dqe_s[:, sl] = _dot_x3(_dot, da2, keh)
            dke_s[:, sl] = _dot_x3(_dot_tn, da2, qeh)
            dqd_s[:, sl] = _dot_x3(_dot, douth, s02)
            dkd_s[:, sl] = _dot_x3(_dot, vh, ds12)
            dl_s[:, sl] = _colsum(ds1 * s0)
            dstate[h] = _dot_tn(douth[0], qd16[:, sl]) + ds1 * el[:, sl]
        dqe, dke, dqd, dkd = dqe_s[...], dke_s[...], dqd_s[...], dkd_s[...]
        dq_ref[...] = dqe * eq + dqd * ec
        dk = dke * ek + dkd * ekd
        dcum = dqe * qe - dke * ke + dqd * qd - dkd * kd
        dlast = _colsum(dkd * kd) + el * dl_s[...]
        dlf = _dot_tn(trif, dcum, HI) + dlast
        df = dlf / f - dk
        dz_ref[...] = df * (1.0 - lb) * sig * (1.0 - sig)
        dlb_ref[...] += _colsum(df * (1.0 - sig))

    def blk(cb):
        return pl.BlockSpec((HG_CHUNK, hgf), lambda j: (chunk_of(j), cb))

    oblk = pl.BlockSpec((HG_CHUNK, hgf), lambda j: (chunk_of(j), 0))
    wide = pltpu.VMEM((HG_CHUNK, hgf), F32)
    return dict(
        step=step, nc=nc, operands=[u, u, u, lbr, st, do],
        in_specs=[blk(qcol), blk(fcol), blk(icol), pl.BlockSpec((1, hgf), lambda j: (0, 0)),
                  pl.BlockSpec((None, nh, HG_DIM, HG_DIM), lambda j: (chunk_of(j), 0, 0, 0)),
                  pl.BlockSpec((HG_CHUNK, hgf), lambda j: (jnp.minimum(chunk_of(j), ncl - 1), 0))],
        out_specs=[oblk, oblk, oblk, pl.BlockSpec((1, hgf), lambda j: (0, 0))],
        out_shape=[jax.ShapeDtypeStruct((T, hgf), F32)] * 3 + [jax.ShapeDtypeStruct((1, hgf), F32)],
        scratch=[pltpu.VMEM((nh, HG_DIM, HG_DIM), F32), wide, wide, wide, wide, pltpu.VMEM((1, hgf), F32)])


def _adamw(w, g, m, v, name):
    shape = w.shape
    if w.ndim != 2 or shape[0] % 8 or shape[1] % LANES:
        w, g, m, v = [a.reshape(1, -1) for a in (w, g, m, v)]
    r, cw = w.shape
    tm = _tile(r, max(8, (1 << 19) // cw), 8) if r % 8 == 0 else r
    c1 = 1.0 / (1.0 - ADAM_B1 ** ADAM_STEP)
    c2 = 1.0 / (1.0 - ADAM_B2 ** ADAM_STEP)

    def body(w_ref, g_ref, m_ref, v_ref, d_ref, nm_ref, nv_ref):
        gg = g_ref[...]
        nm = ADAM_B1 * m_ref[...] + (1.0 - ADAM_B1) * gg
        nv = ADAM_B2 * v_ref[...] + (1.0 - ADAM_B2) * (gg * gg)
        d_ref[...] = -ADAM_LR * ((nm * c1) / (jnp.sqrt(nv * c2) + ADAM_EPS) + ADAM_WD * w_ref[...])
        nm_ref[...] = nm
        nv_ref[...] = nv

    spec = pl.BlockSpec((tm, cw), lambda i: (i, 0))
    outs = pl.pallas_call(
        body, name=name, grid=(r // tm,), in_specs=[spec] * 4, out_specs=[spec] * 3,
        out_shape=[jax.ShapeDtypeStruct((r, cw), F32)] * 3,
        compiler_params=_params(("parallel",)),
    )(w, g, m, v)
    return [o.reshape(shape) for o in outs]


def kernel(x, c, ctx, c_ctx, w_ada, b_ada, norm1_g, w_in, na_rpb, hg_lb_logits, hg_norm_g, w_pa, w_pb, w_out, norm2_g, w_ffn_in, w_ffn_out, final_g, loss_target, m_c_ctx, m_w_ada, m_b_ada, m_norm1_g, m_w_in, m_na_rpb, m_hg_lb_logits, m_hg_norm_g, m_w_pa, m_w_pb, m_w_out, m_norm2_g, m_w_ffn_in, m_w_ffn_out, m_final_g, v_c_ctx, v_w_ada, v_b_ada, v_norm1_g, v_w_in, v_na_rpb, v_hg_lb_logits, v_hg_norm_g, v_w_pa, v_w_pb, v_w_out, v_norm2_g, v_w_ffn_in, v_w_ffn_out, v_final_g):
    xi, yi, ci = lax.axis_index("x"), lax.axis_index("y"), lax.axis_index("c")
    sidx = 2 * xi + yi
    eidx = 4 * xi + 2 * yi + ci

    S, D = x.shape[1], x.shape[2]
    L = ctx.shape[1]
    T = S + L
    naw = NA_HEADS * NA_HEAD_DIM
    hgf = HG_HEADS * HG_DIM
    inw = 3 * naw + 5 * hgf + 2 * D
    fh = w_ffn_out.shape[1] * 4
    ads = w_ada.shape[2]
    fs = hg_lb_logits.shape[2]
    rows = S // GRID_W
    tr = _tile(L, 256)
    nlat, nall = S // tr, T // tr
    assert naw == hgf and D % naw == 0 and S % tr == 0 and 2 * hgf <= D

    pack0 = jnp.concatenate([c, jnp.pad(hg_lb_logits.reshape(1, -1), ((0, 0), (0, D - 4 * fs))),
                             jnp.zeros((6, D), F32)], axis=0)
    g0 = _all_gather8(pack0).reshape(8, 8, D)
    cs = g0[:, 0]
    lbl = g0[::2, 1, :4 * fs].reshape(4, 2, 2, fs).transpose(1, 2, 0, 3).reshape(2, 2, 4 * fs)
    p_lb = jax.nn.softmax(lbl, axis=0)
    lb = p_lb[0]
    lbb = [lb[d].reshape(1, hgf) for d in range(2)]

    cin = jnp.concatenate([cs, c_ctx[None], jnp.zeros((7, D), F32)], axis=0)
    b_sh = lax.dynamic_slice(b_ada, (0, sidx * ads), (1, ads))
    modp = _ada_fwd(cin, w_ada[0], b_sh)
    modfull = _all_gather8(modp).reshape(8, 16, ads)[::2].transpose(1, 0, 2).reshape(16, 4 * ads)
    mod_e = jnp.pad(lax.dynamic_index_in_dim(modfull, eidx, 0, keepdims=False).reshape(N_MOD, D), ((0, 2), (0, 0)))
    mod_c = jnp.pad(modfull[8].reshape(N_MOD, D), ((0, 2), (0, 0)))

    names = ["w_in", "w_pa", "w_pb", "w_out", "w_ffn_in", "w_ffn_out"]
    placed = [_cast_place(w[0], "cast_" + nm)
              for w, nm in zip((w_in, w_pa, w_pb, w_out, w_ffn_in, w_ffn_out), names)]
    def shards(g):
        return g.reshape(4, 2 * g.shape[2], g.shape[3])

    win3 = shards(_exchange_call(_exchange_gather(placed[:1]), "gather_w_in")[0])

    x2d, ctx2d = x[0], ctx[0]

    def f_ln1(i, rv, vv):
        xl, xc = rv
        g, me, mc = vv
        isc = i >= nlat
        xt = jnp.where(isc, xc, xl)
        sh = jnp.where(isc, mc[0:1], me[0:1])
        sc = jnp.where(isc, mc[1:2], me[1:2])
        h = xt * _rms(xt) * g * (1.0 + sc) + sh
        return [h, h], []

    hb, hbt = _rowwise(f_ln1, nall, tr, [(x2d, D, 0, lambda i: jnp.minimum(i, nlat - 1)),
                                        (ctx2d, D, 0, lambda i: jnp.maximum(i - nlat, 0))],
                       [norm1_g, mod_e, mod_c], [(D, BF16), (D, BF16, "T")], [], "ln1")
    u, gathered = _mm_nn(hb, win3, F32, "mm_in", carry=_exchange_gather(placed[1:]))
    wpa3, wpb3, wout3, wi3, wfo3 = [shards(g) for g in gathered]
    wout1 = wout3.reshape(1, D, D)
    wfo1 = wfo3.reshape(1, fh, D)

    scale = NA_HEAD_DIM ** -0.5

    def f_qkv(i, rv, vv):
        q, k, v = rv
        return [q * scale, k, v], []

    qs, kb, vb = _rowwise(f_qkv, nall, tr, [(u, naw, 0, None), (u, naw, 1, None), (u, naw, 2, None)], [],
                          [(naw, BF16)] * 3, [], "qkv_cast")
    bias = _bias_tables(na_rpb[0], rows)
    o_na, lse = _na_fwd(qs, kb, vb, bias, S, L)

    (o_f, st_f), (o_b, st_b) = _hg_both(
        [_hg_fwd(u, lbb[0], S, L, naw, hgf, False), _hg_fwd(u, lbb[1], S, L, naw, hgf, True)], "hg_fwd")

    hgn = jnp.tile(hg_norm_g, (1, HG_HEADS))
    hog_cb = (3 * naw + 4 * hgf) // hgf
    ga_cb = (3 * naw + 5 * hgf) // D
    gb_cb = ga_cb + 1

    def heads_rms(o):
        return jnp.concatenate([jnp.broadcast_to(_rms(o[:, h * HG_DIM:(h + 1) * HG_DIM]), (o.shape[0], HG_DIM))
                                for h in range(HG_HEADS)], axis=1)

    def f_readout(i, rv, vv):
        of, ob_, hog = rv
        g, = vv
        o = of + ob_
        return [o * heads_rms(o) * g * _silu(hog)], []

    ob, = _rowwise(f_readout, nlat, tr, [(o_f, hgf, 0, None), (o_b, hgf, 0, None), (u, hgf, hog_cb, None)],
                   [hgn], [(hgf, BF16)], [], "hg_readout")

    ya = _mm_nn(o_na, wpa3, F32, "mm_pa")
    yb = _mm_nn(ob, wpb3, F32, "mm_pb")

    def f_merge(i, rv, vv):
        ya_, yb_, ga, gb = rv
        return [_sigmoid(ga) * ya_ + _sigmoid(gb) * yb_], []

    yv, = _rowwise(f_merge, nlat, tr, [(ya, D, 0, None), (yb, D, 0, None), (u, D, ga_cb, None), (u, D, gb_cb, None)],
                   [], [(D, BF16)], [], "merge")
    z = _mm_nn(yv, wout1, F32, "mm_out")

    def f_res1(i, rv, vv):
        xt, zt = rv
        g, me = vv
        x1 = xt + me[2:3] * zt
        h = x1 * _rms(x1) * g * (1.0 + me[4:5]) + me[3:4]
        return [x1, h, h], []

    x1, h2, h2t = _rowwise(f_res1, nlat, tr, [(x2d, D, 0, None), (z, D, 0, None)], [norm2_g, mod_e],
                           [(D, F32), (D, BF16), (D, BF16, "T")], [], "res1_ln2")
    au3, sw, swt = _ffn_in_fused(h2, wi3, "mm_ffn_in")
    ff =_mm_nn(sw, wfo1, F32, "mm_ffn_out")

    fg = final_g.reshape(1, D)

    def f_final(i, rv, vv):
        x1t, ft, tg = rv
        g, me = vv
        x2 = x1t + me[5:6] * ft
        r3 = _rms(x2)
        xn = x2 * r3
        err = xn * g - tg
        dyy = err * (1.0 / D)
        dxn = dyy * g
        dx2 = r3 * (dxn - xn * jnp.mean(dxn * xn, axis=-1, keepdims=True))
        return [dx2, dx2 * me[5:6]], [_colsum(err * err), _colsum(dyy * xn), _colsum(dx2 * ft)]

    dx2, dfb, loss_cols, dfg, dg2 = _rowwise(
        f_final, nlat, tr, [(x1, D, 0, None), (ff, D, 0, None), (loss_target[0], D, 0, None)], [fg, mod_e],
        [(D, F32), (D, BF16)], [D, D, D], "final_loss")

    def dswiglu_epi(tm, tn):
        blk = pl.BlockSpec((2, tm, tn), lambda m, n, k: (0, m, n))

        def fn(d, ins):
            a, uu = ins[0][0], ins[0][1]
            return [(d * uu * _dsilu(a), d * _silu(a))]

        return dict(ins=[(au3, blk)], outs=[(jax.ShapeDtypeStruct((2, S, fh), BF16), blk)], fn=fn)

    dau3, = _mm_nt(dfb, wfo1, BF16, "mm_d_sw", epi=dswiglu_epi, tn_target=512)
    g_wfo = _mm_tn(swt, dfb, 1, "mm_dw_ffn_out", a_is_t=True).reshape(4, 2, fh // 8, D)
    dh2 = _mm_nt(dau3, wi3, F32, "mm_d_h2")
    g_wi = _mm_tn(h2t, dau3, 4, "mm_dw_ffn_in", a_is_t=True)

    def f_ln2_bwd(i, rv, vv):
        dh, x1t, dx2t, zt = rv
        g, me = vv
        r2 = _rms(x1t)
        xn = x1t * r2
        dxn = dh * g * (1.0 + me[4:5])
        dx1 = dx2t + r2 * (dxn - xn * jnp.mean(dxn * xn, axis=-1, keepdims=True))
        return ([dx1, dx1 * me[2:3]],
                [_colsum(dh), _colsum(dh * xn * g), _colsum(dh * xn * (1.0 + me[4:5])), _colsum(dx1 * zt)])

    dx1, dzb, dsh2, dsc2, dn2g, dg1 = _rowwise(
        f_ln2_bwd, nlat, tr, [(dh2, D, 0, None), (x1, D, 0, None), (dx2, D, 0, None), (z, D, 0, None)],
        [norm2_g, mod_e], [(D, F32), (D, BF16)], [D, D, D, D], "ln2_bwd")

    g_wout = _mm_tn(yv, dzb, 1, "mm_dw_out").reshape(4, 2, D // 8, D)

    def dmerge_epi(tm, tn):
        blk = pl.BlockSpec((tm, tn), lambda m, n, k: (m, n))

        def gate(cb):
            return pl.BlockSpec((tm, tn), lambda m, n, k: (m, cb * (D // tn) + n))

        def fn(d, ins):
            ya_, yb_, ga, gb = ins
            sa, sb_ = _sigmoid(ga), _sigmoid(gb)
            return [d * sa, d * sb_, d * ya_ * sa * (1.0 - sa), d * yb_ * sb_ * (1.0 - sb_)]

        return dict(ins=[(ya, blk), (yb, blk), (u, gate(ga_cb)), (u, gate(gb_cb))],
                    outs=[(jax.ShapeDtypeStruct((S, D), BF16), blk)] * 4, fn=fn)

    dya, dyb, dga, dgb = _mm_nt(dzb, wout1, BF16, "mm_d_y", epi=dmerge_epi, tn_target=512)
    d_ona = _mm_nt(dya, wpa3, BF16, "mm_d_ona")
    d_ob = _mm_nt(dyb, wpb3, F32, "mm_d_ob")
    g_wpa = _mm_tn(o_na, dya, 4, "mm_dw_pa")
    g_wpb = _mm_tn(ob, dyb, 4, "mm_dw_pb")

    def f_dreadout(i, rv, vv):
        d, of, ob_, hog = rv
        g, = vv
        o = of + ob_
        on = o * heads_rms(o)
        t = d * _silu(hog) * g
        mt = jnp.concatenate([jnp.broadcast_to(jnp.mean((t * on)[:, h * HG_DIM:(h + 1) * HG_DIM], axis=-1,
                                                        keepdims=True), (o.shape[0], HG_DIM))
                              for h in range(HG_HEADS)], axis=1)
        do_ = heads_rms(o) * (t - on * mt)
        return [do_, d * on * g * _dsilu(hog)], [_colsum(d * _silu(hog) * on)]

    do_hg, dhog, dhgn = _rowwise(
        f_dreadout, nlat, tr, [(d_ob, hgf, 0, None), (o_f, hgf, 0, None), (o_b, hgf, 0, None),
                               (u, hgf, hog_cb, None)], [hgn], [(hgf, F32), (hgf, BF16)], [hgf], "hg_readout_bwd")

    (dq_f, dz_f, dv_f, dlb_f), (dq_b, dz_b, dv_b, dlb_b) = _hg_both(
        [_hg_bwd(u, lbb[0], st_f, do_hg, S, L, naw, hgf, False), _hg_bwd(u, lbb[1], st_b, do_hg, S, L, naw, hgf, True)],
        "hg_bwd")
    dq_na, dk_na, dv_na, dbias = _na_bwd(qs, kb, vb, bias, d_ona, o_na, lse, S, L)

    ta = _tile(L, 128)
    nla, naa = S // ta, T // ta
    lat = lambda i: jnp.minimum(i, nla - 1)

    def f_assemble(i, rv, vv):
        dqn, dk, dv, dqf, dqb, dzf, dzb_, dvf, dvb, dho, dga_, dgb_ = rv
        keep = (i < nla).astype(F32)
        return [jnp.concatenate([dqn * (scale * keep), dk, dv, dqf + dqb, dzf, dzb_, dvf + dvb,
                                 dho.astype(F32) * keep, dga_.astype(F32) * keep, dgb_.astype(F32) * keep],
                                axis=1)], []

    du, = _rowwise(
        f_assemble, naa, ta,
        [(dq_na, naw, 0, lat), (dk_na, naw, 0, None), (dv_na, naw, 0, None), (dq_f, hgf, 0, None),
         (dq_b, hgf, 0, None), (dz_f, hgf, 0, None), (dz_b, hgf, 0, None), (dv_f, hgf, 0, None),
         (dv_b, hgf, 0, None), (dhog, hgf, 0, lat), (dga, D, 0, lat), (dgb, D, 0, lat)],
        [], [(inw, BF16)], [], "assemble_du")

    def add_half(g, land, nm):
        return _add_own_half(g, land, "rs_add_" + nm)

    early = [g_wpa, g_wpb, g_wout, g_wi, g_wfo]
    g_win_other, lands = _mm_tn_half(hbt, du, 4, True, "mm_dw_in_other", carry=_exchange_swap_other_half(early))
    parts = [add_half(g, l, nm) for g, l, nm in zip(early, lands, names[1:])]
    g_win_mine, landed = _mm_tn_half(hbt, du, 4, False, "mm_dw_in_mine", carry=_exchange_join(
        _exchange_scatter(parts[3:4]), _exchange_swap_other_half([g_win_other])))
    piece_wi = landed[0]
    parts = [add_half(g_win_mine, landed[1], names[0])] + parts
    dh, landed = _mm_nt(du, win3, F32, "mm_d_h", carry=_exchange_scatter(parts[:4] + parts[5:]))
    pieces = landed[:4] + [piece_wi] + landed[4:]
    halves = [_sum_pieces(p, l, "rs_sum_" + nm) for p, l, nm in zip(parts, pieces, names)]
    g_win, g_wpa, g_wpb, g_wout, g_wi, g_wfo = [
        f.reshape(2 * f.shape[1], f.shape[2])
        for f in _exchange_call(_exchange_swap_result(halves), "rs_swap_result_half")]

    def f_ln1_bwd(i, rv, vv):
        dht, xt, dx1t = rv
        g, me = vv
        r1 = _rms(xt)
        xn = xt * r1
        dxn = dht * g * (1.0 + me[1:2])
        dx = dx1t + r1 * (dxn - xn * jnp.mean(dxn * xn, axis=-1, keepdims=True))
        return [dx], [_colsum(dht), _colsum(dht * xn * g), _colsum(dht * xn * (1.0 + me[1:2]))]

    grad_x, dsh1, dsc1, dn1g_l = _rowwise(
        f_ln1_bwd, nlat, tr, [(dh, D, 0, None), (x2d, D, 0, None), (dx1, D, 0, None)], [norm1_g, mod_e],
        [(D, F32)], [D, D, D], "ln1_bwd")

    def f_ln1_bwd_ctx(i, rv, vv):
        dht, xt = rv
        g, mc = vv
        xn = xt * _rms(xt)
        return [], [_colsum(dht), _colsum(dht * xn * g), _colsum(dht * xn * (1.0 + mc[1:2]))]

    ctx_rows = lambda i: i + nlat
    dsh1c, dsc1c, dn1g_c = _rowwise(
        f_ln1_bwd_ctx, nall - nlat, tr, [(dh, D, 0, ctx_rows), (ctx2d, D, 0, None)], [norm1_g, mod_c],
        [], [D, D, D], "ln1_bwd_ctx")

    drpb = _bias_tables_transpose(dbias, rows).reshape(1, -1)
    nrp = -(-drpb.shape[1] // D)
    drpb_rows = jnp.pad(drpb, ((0, 0), (0, nrp * D - drpb.shape[1]))).reshape(nrp, D)
    dlb = jnp.concatenate([dlb_f, dlb_b], axis=1)
    dhg = jnp.sum(dhgn.reshape(HG_HEADS, HG_DIM), axis=0, keepdims=True)

    def wide(v):
        return jnp.pad(v, ((0, 0), (0, D - v.shape[1])))

    pack_rows = [loss_cols, dfg, dn2g, dn1g_l + dn1g_c, dsh1, dsc1, dg1, dsh2, dsc2, dg2, dsh1c, dsc1c,
                 wide(dhg), wide(dlb), drpb_rows]
    pack = jnp.concatenate(pack_rows, axis=0)
    npk = -(-pack.shape[0] // 8) * 8
    pack = jnp.pad(pack, ((0, npk - pack.shape[0]), (0, 0)))
    gp = _all_gather8(pack).reshape(8, npk, D)
    tot = _sum8(gp, "sum_small_grads")

    loss = (0.5 / D) * jnp.sum(tot[0])
    grad_final_g = tot[1]
    grad_norm2_g = tot[2:3]
    grad_norm1_g = tot[3:4]
    grad_hg_norm_g = tot[12:13, :HG_DIM]
    dlb_tot = tot[13, :2 * hgf].reshape(2, hgf)
    grad_na_rpb = tot[14:14 + nrp].reshape(-1)[:drpb.shape[1]].reshape(na_rpb.shape)
    dlog = jnp.stack([dlb_tot * p_lb[0] * (1.0 - p_lb[0]), -dlb_tot * p_lb[0] * p_lb[1]], axis=0)
    grad_hg_lb = lax.dynamic_slice(dlog, (0, 0, sidx * fs), (2, 2, fs))

    dmod_all = gp[:, 4:10].reshape(8, N_MOD * D)
    dmod_ctx = jnp.concatenate([tot[10], tot[11], jnp.zeros((4 * D,), F32)])[None]
    dm16 = jnp.concatenate([dmod_all, dmod_ctx, jnp.zeros((7, N_MOD * D), F32)], axis=0)
    grad_b_ada = jnp.sum(dm16, axis=0, keepdims=True)
    dm_sh = lax.dynamic_slice(dm16, (0, sidx * ads), (16, ads))
    g_wada, dcin = _ada_bwd(cin, w_ada[0], dm_sh)
    gc = _all_gather8(dcin[8:16]).reshape(8, 8, D)
    grad_c_ctx = (gc[0, 0] + gc[2, 0] + gc[4, 0] + gc[6, 0]) * _dsilu(c_ctx)

    grads = {
        "c_ctx": grad_c_ctx, "w_ada": g_wada[None], "b_ada": grad_b_ada, "norm1_g": grad_norm1_g,
        "w_in": g_win[None], "na_rpb": grad_na_rpb, "hg_lb_logits": grad_hg_lb, "hg_norm_g": grad_hg_norm_g,
        "w_pa": g_wpa[None], "w_pb": g_wpb[None], "w_out": g_wout[None], "norm2_g": grad_norm2_g,
        "w_ffn_in": g_wi[None], "w_ffn_out": g_wfo[None], "final_g": grad_final_g,
    }
    weights = {
        "c_ctx": (c_ctx, m_c_ctx, v_c_ctx), "w_ada": (w_ada, m_w_ada, v_w_ada), "b_ada": (b_ada, m_b_ada, v_b_ada),
        "norm1_g": (norm1_g, m_norm1_g, v_norm1_g), "w_in": (w_in, m_w_in, v_w_in),
        "na_rpb": (na_rpb, m_na_rpb, v_na_rpb), "hg_lb_logits": (hg_lb_logits, m_hg_lb_logits, v_hg_lb_logits),
        "hg_norm_g": (hg_norm_g, m_hg_norm_g, v_hg_norm_g), "w_pa": (w_pa, m_w_pa, v_w_pa),
        "w_pb": (w_pb, m_w_pb, v_w_pb), "w_out": (w_out, m_w_out, v_w_out),
        "norm2_g": (norm2_g, m_norm2_g, v_norm2_g), "w_ffn_in": (w_ffn_in, m_w_ffn_in, v_w_ffn_in),
        "w_ffn_out": (w_ffn_out, m_w_ffn_out, v_w_ffn_out), "final_g": (final_g, m_final_g, v_final_g),
    }
    order = list(weights)
    deltas, new_ms, new_vs = [], [], []
    for nm in order:
        w, m, v = weights[nm]
        g = grads[nm].reshape(w.shape)
        grads[nm] = g
        if w.ndim == 3 and w.shape[0] == 1:
            d_, m_, v_ = _adamw(w[0], g[0], m[0], v[0], "adamw_" + nm)
            d_, m_, v_ = d_[None], m_[None], v_[None]
        else:
            d_, m_, v_ = _adamw(w, g, m, v, "adamw_" + nm)
        deltas.append(d_)
        new_ms.append(m_)
        new_vs.append(v_)

    return (loss, grad_x[None], *[grads[nm] for nm in order], *deltas, *new_ms, *new_vs)
```

```python
import numpy as np

import jax
import jax.numpy as jnp
from jax import lax
from jax.experimental import pallas as pl
from jax.experimental.pallas import tpu as pltpu

F32 = jnp.float32
BF16 = jnp.bfloat16

GRID_W = 64
WIN_H = 8
WIN_W = 16
NA_HEADS = 16
NA_HEAD_DIM = 64
HG_HEADS = 8
HG_DIM = 128
HG_CHUNK = 64
N_MOD = 6
EPS = 1e-6
ADAM_LR = 0.001
ADAM_B1 = 0.9
ADAM_B2 = 0.999
ADAM_EPS = 1e-08
ADAM_WD = 0.01
ADAM_STEP = 10

LANES = 128
NA_ROWS_PER_STEP = 16
VMEM_LIMIT = 56 * 1024 * 1024
MASK_VALUE = -1e30
EXP_CLAMP = 80.0
MESH_ID = pl.DeviceIdType.MESH
HI = lax.Precision.HIGHEST


def _tile(dim, target, mult=LANES):
    best = None
    t = mult
    while t <= min(dim, target):
        if dim % t == 0:
            best = t
        t += mult
    assert best is not None, (dim, target, mult)
    return best


def _params(sem):
    return pltpu.CompilerParams(dimension_semantics=sem, vmem_limit_bytes=VMEM_LIMIT)


def _dot(a, b, precision=None):
    return jnp.dot(a, b, preferred_element_type=F32, precision=precision)


def _dot_nt(a, b, precision=None):
    return lax.dot_general(a, b, (((1,), (1,)), ((), ())), preferred_element_type=F32, precision=precision)


def _dot_tn(a, b, precision=None):
    return lax.dot_general(a, b, (((0,), (0,)), ((), ())), preferred_element_type=F32, precision=precision)


def _split2(v):
    hi = v.astype(BF16)
    return hi, (v - hi.astype(F32)).astype(BF16)


def _dot_x3(dot, a2, b2):
    return dot(a2[0], b2[0]) + (dot(a2[0], b2[1]) + dot(a2[1], b2[0]))


def _sigmoid(v):
    return 1.0 / (1.0 + jnp.exp(-v))


def _mm_call(dot, operands, grid, in_specs, out_spec, out_shape, acc_shape, name, carry=None, epi=None):
    nk = grid[2]
    nci = 0 if carry is None else len(carry["ins"])
    nco = 0 if carry is None else len(carry["outs"])
    nei = 0 if epi is None else len(epi["ins"])
    neo = 1 if epi is None else len(epi["outs"])

    def body(*refs):
        a_ref, b_ref = refs[:2]
        ein = refs[2:2 + nei]
        cin = refs[2 + nei:2 + nei + nci]
        outs = refs[2 + nei + nci:2 + nei + nci + neo]
        cout = refs[2 + nei + nci + neo:2 + nei + nci + neo + nco]
        acc = refs[2 + nei + nci + neo + nco]
        sems = refs[3 + nei + nci + neo + nco:]
        m, n, k = pl.program_id(0), pl.program_id(1), pl.program_id(2)

        if carry is not None:
            @pl.when((m == 0) & (n == 0) & (k == 0))
            def _():
                carry["start"](cin, cout, *sems)

        @pl.when(k == 0)
        def _():
            acc[...] = jnp.zeros_like(acc)

        if carry is not None and "mid" in carry:
            @pl.when((m == grid[0] - 1) & (n == 0) & (k == 0))
            def _():
                carry["mid"](cin, cout, *sems)

        acc[...] += dot(a_ref[...], b_ref[...])

        @pl.when(k == nk - 1)
        def _():
            vals = [acc[...]] if epi is None else epi["fn"](acc[...], [r[...] for r in ein])
            for r, v in zip(outs, vals):
                if isinstance(v, tuple):
                    for i, vi in enumerate(v):
                        r[i] = vi.astype(r.dtype)
                else:
                    r[...] = v.astype(r.dtype)

        if carry is not None:
            @pl.when((m == grid[0] - 1) & (n == grid[1] - 1) & (k == nk - 1))
            def _():
                carry["finish"](cin, cout, *sems)

    any_spec = pl.BlockSpec(memory_space=pl.ANY)
    scratch = [pltpu.VMEM(acc_shape, F32)]
    extra = {}
    if carry is not None:
        scratch += [pltpu.SemaphoreType.DMA((carry["nsem"],)), pltpu.SemaphoreType.DMA((carry["nsem"],))]
        extra["input_output_aliases"] = {2 + nei + i: neo + j for i, j in carry["alias"].items()}
    sem = ("arbitrary",) * 3 if carry is not None else ("parallel", "parallel", "arbitrary")
    main_outs = [(out_shape, out_spec)] if epi is None else list(epi["outs"])
    res = pl.pallas_call(
        body, name=name, grid=grid,
        in_specs=list(in_specs) + ([] if epi is None else [sp for _, sp in epi["ins"]]) + [any_spec] * nci,
        out_specs=[sp for _, sp in main_outs] + [any_spec] * nco,
        out_shape=[sh for sh, _ in main_outs] + ([] if carry is None else list(carry["outs"])),
        scratch_shapes=scratch, compiler_params=_params(sem), **extra,
    )(*operands, *([] if epi is None else [ar for ar, _ in epi["ins"]]), *([] if carry is None else carry["ins"]))
    res = list(res)
    main = res[0] if epi is None else res[:neo]
    return main if carry is None else (main, res[neo:])


def _mm_nn(a, b3, out_dtype, name, carry=None):
    M, K = a.shape
    nsh, _, Ns = b3.shape
    tm, tn, tk = _tile(M, 1024), _tile(Ns, 1408), _tile(K, 2816)
    tps, nk = Ns // tn, K // tk
    return _mm_call(
        _dot, (a, b3), (M // tm, nsh * tps, nk),
        [pl.BlockSpec((tm, tk), lambda m, n, k: (m, k)),
         pl.BlockSpec((None, tk, tn), lambda m, n, k: (n // tps, k, n % tps))],
        pl.BlockSpec((tm, tn), lambda m, n, k: (m, n)),
        jax.ShapeDtypeStruct((M, nsh * Ns), out_dtype), (tm, tn), name, carry)


def _mm_nt(a, b3, out_dtype, name, carry=None, epi=None, tn_target=1408):
    a3 = a if a.ndim == 3 else a[None]
    na, M, Ka = a3.shape
    nsh, Kw, Ns = b3.shape
    assert na * Ka == nsh * Ns
    tm, tn, tk = _tile(M, 1024), _tile(Kw, tn_target), _tile(int(np.gcd(Ka, Ns)), 3072)
    kpa, kps = Ka // tk, Ns // tk
    return _mm_call(
        _dot_nt, (a3, b3), (M // tm, Kw // tn, nsh * kps),
        [pl.BlockSpec((None, tm, tk), lambda m, n, k: (k // kpa, m, k % kpa)),
         pl.BlockSpec((None, tn, tk), lambda m, n, k: (k // kps, n, k % kps))],
        pl.BlockSpec((tm, tn), lambda m, n, k: (m, n)),
        jax.ShapeDtypeStruct((M, Kw), out_dtype), (tm, tn), name, carry,
        None if epi is None else epi(tm, tn))


def _mm_tn(a, g, nsh, name, carry=None, a_is_t=False):
    Tk, M = a.shape[::-1] if a_is_t else a.shape
    g3 = g if g.ndim == 3 else g[None]
    ng, _, Ng = g3.shape
    Ns = ng * Ng // nsh
    tm, tn, tk = _tile(M // 2, 1408), _tile(int(np.gcd(Ng, Ns)), 1408), _tile(Tk, 2048)
    mh, tps, tpg, nk = (M // 2) // tm, Ns // tn, Ng // tn, Tk // tk
    return _mm_call(
        _dot if a_is_t else _dot_tn, (a, g3), (M // tm, nsh * tps, nk),
        [pl.BlockSpec((tm, tk), lambda m, n, k: (m, k)) if a_is_t else pl.BlockSpec((tk, tm), lambda m, n, k: (k, m)),
         pl.BlockSpec((None, tk, tn), lambda m, n, k: (n // tpg, k, n % tpg))],
        pl.BlockSpec((None, None, tm, tn), lambda m, n, k: (n // tps, m // mh, m % mh, n % tps)),
        jax.ShapeDtypeStruct((nsh, 2, M // 2, Ns), F32), (tm, tn), name, carry)


def _mm_tn_half(at, g, nsh, other, name, carry=None):
    M, Tk = at.shape
    Ns = g.shape[1] // nsh
    tm, tn, tk = _tile(M // 2, 1408), _tile(Ns, 1408), _tile(Tk, 2048)
    mh, tps, nk = (M // 2) // tm, Ns // tn, Tk // tk

    def half():
        c = lax.axis_index("c")
        return 1 - c if other else c

    return _mm_call(
        _dot, (at, g), (mh, nsh * tps, nk),
        [pl.BlockSpec((tm, tk), lambda m, n, k: (half() * mh + m, k)),
         pl.BlockSpec((tk, tn), lambda m, n, k: (k, n))],
        pl.BlockSpec((None, tm, tn), lambda m, n, k: (n // tps, m, n % tps)),
        jax.ShapeDtypeStruct((nsh, M // 2, Ns), F32), (tm, tn), name, carry)


def _ffn_in_fused(h2, wi3, name):
    M, K = h2.shape
    _, _, Ns = wi3.shape
    fh = 2 * Ns
    tm, tn = _tile(M, 1024), _tile(Ns, 512)
    tps = Ns // tn

    def body(h_ref, ba_ref, bu_ref, au_ref, sw_ref, swt_ref):
        h = h_ref[...]
        a, u = _dot(h, ba_ref[...]), _dot(h, bu_ref[...])
        au_ref[0] = a
        au_ref[1] = u
        sw = (_silu(a) * u).astype(sw_ref.dtype)
        sw_ref[...] = sw
        swt_ref[...] = sw.T

    return pl.pallas_call(
        body, name=name, grid=(M // tm, fh // tn),
        in_specs=[pl.BlockSpec((tm, K), lambda m, n: (m, 0)),
                  pl.BlockSpec((None, K, tn), lambda m, n: (n // tps, 0, n % tps)),
                  pl.BlockSpec((None, K, tn), lambda m, n: (2 + n // tps, 0, n % tps))],
        out_specs=[pl.BlockSpec((2, tm, tn), lambda m, n: (0, m, n)), pl.BlockSpec((tm, tn), lambda m, n: (m, n)),
                   pl.BlockSpec((tn, tm), lambda m, n: (n, m))],
        out_shape=[jax.ShapeDtypeStruct((2, M, fh), F32), jax.ShapeDtypeStruct((M, fh), BF16),
                   jax.ShapeDtypeStruct((fh, M), BF16)],
        compiler_params=_params(("parallel", "parallel")),
    )(h2, wi3, wi3)


def _rowwise(fn, nblk, tm, rins, vins, routs, accs, name):
    nr, nv, no, na = len(rins), len(vins), len(routs), len(accs)

    def body(*refs):
        i = pl.program_id(0)
        outs, accv = fn(i, [r[...] for r in refs[:nr]], [r[...] for r in refs[nr:nr + nv]])
        for r, v, spec in zip(refs[nr + nv:nr + nv + no], outs, routs):
            v = v.astype(r.dtype)
            r[...] = v.T if len(spec) == 3 else v
        arefs = refs[nr + nv + no:]
        if na:
            @pl.when(i == 0)
            def _():
                for a in arefs:
                    a[...] = jnp.zeros_like(a)

            for a, v in zip(arefs, accv):
                a[...] += v

    def row_spec(w, cb, rm):
        if rm is None:
            return pl.BlockSpec((tm, w), lambda i: (i, cb))
        return pl.BlockSpec((tm, w), lambda i: (rm(i), cb))

    in_specs = [row_spec(w, cb, rm) for (_, w, cb, rm) in rins]
    in_specs += [pl.BlockSpec(v.shape, lambda i: (0, 0)) for v in vins]
    def out_of(spec):
        w, dt = spec[:2]
        if len(spec) == 3:
            return pl.BlockSpec((w, tm), lambda i: (0, i)), jax.ShapeDtypeStruct((w, nblk * tm), dt)
        return pl.BlockSpec((tm, w), lambda i: (i, 0)), jax.ShapeDtypeStruct((nblk * tm, w), dt)

    out_specs = [out_of(sp)[0] for sp in routs] + [pl.BlockSpec((1, w), lambda i: (0, 0)) for w in accs]
    out_shape = [out_of(sp)[1] for sp in routs] + [jax.ShapeDtypeStruct((1, w), F32) for w in accs]
    res = pl.pallas_call(
        body, name=name, grid=(nblk,), in_specs=in_specs, out_specs=out_specs, out_shape=out_shape,
        compiler_params=_params(("arbitrary",)),
    )(*[r[0] for r in rins], *vins)
    return list(res)


def _colsum(v):
    return jnp.sum(v, axis=0, keepdims=True)


def _rms(v):
    return lax.rsqrt(jnp.mean(v * v, axis=-1, keepdims=True) + EPS)


def _all_gather8(xs):
    m_per, n = xs.shape

    def body(x_ref, out_ref, send_sems, recv_sems, local_sem):
        x, y, c = lax.axis_index("x"), lax.axis_index("y"), lax.axis_index("c")
        me, sibling = (x, y, c), (x, y, 1 - c)
        chips = [(1 - x, y), (x, 1 - y), (1 - x, 1 - y)]

        def rows(px, py, pc):
            return out_ref.at[pl.ds((4 * px + 2 * py + pc) * m_per, m_per), :]

        def copy(k, block, to, src=None):
            return pltpu.make_async_remote_copy(
                src_ref=rows(*block) if src is None else src, dst_ref=rows(*block),
                send_sem=send_sems.at[k], recv_sem=recv_sems.at[k], device_id=to, device_id_type=MESH_ID)

        mine = pltpu.make_async_copy(x_ref, rows(*me), local_sem)
        mine.start()
        first = [copy(0, me, sibling, src=x_ref)]
        first += [copy(1 + j, me, (*chip, c), src=x_ref) for j, chip in enumerate(chips)]
        for cp in first:
            cp.start()
        passed = [copy(4 + j, (*chip, c), sibling) for j, chip in enumerate(chips)]
        for j, chip in enumerate(chips):
            copy(1 + j, (*chip, c), me).wait_recv()
            passed[j].start()
        copy(0, sibling, me).wait_recv()
        for j, chip in enumerate(chips):
            copy(4 + j, (*chip, 1 - c), me).wait_recv()
        for cp in first + passed:
            cp.wait_send()
        mine.wait()

    return pl.pallas_call(
        body, name="all_gather8_%dx%d" % (m_per, n),
        out_shape=jax.ShapeDtypeStruct((8 * m_per, n), xs.dtype),
        in_specs=[pl.BlockSpec(memory_space=pltpu.VMEM)],
        out_specs=pl.BlockSpec(memory_space=pltpu.VMEM),
        scratch_shapes=[pltpu.SemaphoreType.DMA((7,)), pltpu.SemaphoreType.DMA((7,)), pltpu.SemaphoreType.DMA],
    )(xs)


def _mesh_pos():
    x, y, c = lax.axis_index("x"), lax.axis_index("y"), lax.axis_index("c")
    chips = [(1 - x, y), (x, 1 - y), (1 - x, 1 - y)]
    return x, y, c, chips


def _my_shard():
    return 2 * lax.axis_index("x") + lax.axis_index("y")


def _cast_place(w, name):
    r, cw = w.shape
    rh = r // 2
    tm = _tile(rh, max(16, (1 << 20) // (4 * cw)), 16)
    nt = rh // tm

    def body(w_ref, o_ref):
        o_ref[...] = w_ref[...].astype(o_ref.dtype)

    return pl.pallas_call(
        body, name=name, grid=(2, nt),
        in_specs=[pl.BlockSpec((tm, cw), lambda h, i: (h * nt + i, 0))],
        out_specs=pl.BlockSpec((None, None, tm, cw), lambda h, i: (_my_shard(), h, i, 0)),
        out_shape=jax.ShapeDtypeStruct((4, 2, rh, cw), BF16),
        compiler_params=_params(("parallel", "parallel")),
    )(w)


def _exchange_gather(bufs):
    n = len(bufs)

    def copies(out, send_sems, recv_sems, base):
        def copy(i, k, shard, half, to):
            dst = out[i].at[shard, half]
            return pltpu.make_async_remote_copy(
                src_ref=dst, dst_ref=dst, send_sem=send_sems.at[base + 6 * i + k],
                recv_sem=recv_sems.at[base + 6 * i + k], device_id=to, device_id_type=MESH_ID)
        return copy

    def first(copy):
        x, y, c, chips = _mesh_pos()
        return [copy(i, j, 2 * x + y, c, (*chip, c)) for i in range(n) for j, chip in enumerate(chips)]

    def start(cin, out, send_sems, recv_sems, base=0):
        for cp in first(copies(out, send_sems, recv_sems, base)):
            cp.start()

    def passed(copy):
        x, y, c, chips = _mesh_pos()
        return [copy(i, 3 + j, 2 * chip[0] + chip[1], c, (x, y, 1 - c)) for j, chip in enumerate(chips) for i in range(n)]

    def mid(cin, out, send_sems, recv_sems, base=0):
        copy = copies(out, send_sems, recv_sems, base)
        x, y, c, chips = _mesh_pos()
        for j, chip in enumerate(chips):
            for i in range(n):
                copy(i, j, 2 * chip[0] + chip[1], c, (x, y, c)).wait_recv()
        for cp in passed(copy):
            cp.start()

    def finish(cin, out, send_sems, recv_sems, base=0):
        copy = copies(out, send_sems, recv_sems, base)
        x, y, c, chips = _mesh_pos()
        for j, chip in enumerate(chips):
            for i in range(n):
                copy(i, 3 + j, 2 * chip[0] + chip[1], 1 - c, (x, y, c)).wait_recv()
        for cp in first(copy) + passed(copy):
            cp.wait_send()

    return dict(ins=list(bufs), outs=[jax.ShapeDtypeStruct(b.shape, b.dtype) for b in bufs],
                alias={i: i for i in range(n)}, nsem=6 * n, start=start, mid=mid, finish=finish)


def _exchange_join(a, b):
    nai, nao = len(a["ins"]), len(a["outs"])

    def start(cin, cout, send_sems, recv_sems, base=0):
        a["start"](cin[:nai], cout[:nao], send_sems, recv_sems, base)
        b["start"](cin[nai:], cout[nao:], send_sems, recv_sems, base + a["nsem"])

    def mid(cin, cout, send_sems, recv_sems, base=0):
        if "mid" in a:
            a["mid"](cin[:nai], cout[:nao], send_sems, recv_sems, base)
        if "mid" in b:
            b["mid"](cin[nai:], cout[nao:], send_sems, recv_sems, base + a["nsem"])

    def finish(cin, cout, send_sems, recv_sems, base=0):
        a["finish"](cin[:nai], cout[:nao], send_sems, recv_sems, base)
        b["finish"](cin[nai:], cout[nao:], send_sems, recv_sems, base + a["nsem"])

    alias = dict(a["alias"])
    alias.update({nai + i: nao + j for i, j in b["alias"].items()})
    return dict(ins=a["ins"] + b["ins"], outs=a["outs"] + b["outs"], alias=alias, nsem=a["nsem"] + b["nsem"],
                start=start, mid=mid, finish=finish)


def _exchange_call(ex, name):
    nci, nco = len(ex["ins"]), len(ex["outs"])

    def body(*refs):
        cin, cout, sems = refs[:nci], refs[nci:nci + nco], refs[nci + nco:]
        ex["start"](cin, cout, *sems)
        if "mid" in ex:
            ex["mid"](cin, cout, *sems)
        ex["finish"](cin, cout, *sems)

    any_spec = pl.BlockSpec(memory_space=pl.ANY)
    return list(pl.pallas_call(
        body, name=name, out_shape=list(ex["outs"]), in_specs=[any_spec] * nci, out_specs=[any_spec] * nco,
        input_output_aliases=dict(ex["alias"]),
        scratch_shapes=[pltpu.SemaphoreType.DMA((ex["nsem"],)), pltpu.SemaphoreType.DMA((ex["nsem"],))],
    )(*ex["ins"]))


def _exchange_swap_other_half(gs):
    n = len(gs)

    def copies(g, land, send_sems, recv_sems, base):
        x, y, c, _ = _mesh_pos()
        return [pltpu.make_async_remote_copy(
            src_ref=g[i].at[:, 1 - c] if len(gs[i].shape) == 4 else g[i], dst_ref=land[i],
            send_sem=send_sems.at[base + i],
            recv_sem=recv_sems.at[base + i], device_id=(x, y, 1 - c), device_id_type=MESH_ID) for i in range(n)]

    def start(g, land, send_sems, recv_sems, base=0):
        for cp in copies(g, land, send_sems, recv_sems, base):
            cp.start()

    def finish(g, land, send_sems, recv_sems, base=0):
        for cp in copies(g, land, send_sems, recv_sems, base):
            cp.wait()

    return dict(ins=list(gs), outs=[jax.ShapeDtypeStruct((4,) + g.shape[-2:], g.dtype) for g in gs],
                alias={}, nsem=n, start=start, finish=finish)


def _exchange_scatter(ps):
    n = len(ps)

    def copies(p, land, send_sems, recv_sems, base):
        x, y, c, chips = _mesh_pos()
        return [pltpu.make_async_remote_copy(
            src_ref=p[i].at[2 * chip[0] + chip[1]], dst_ref=land[i].at[j],
            send_sem=send_sems.at[base + 3 * i + j], recv_sem=recv_sems.at[base + 3 * i + j],
            device_id=(*chip, c), device_id_type=MESH_ID) for i in range(n) for j, chip in enumerate(chips)]

    def start(p, land, send_sems, recv_sems, base=0):
        for cp in copies(p, land, send_sems, recv_sems, base):
            cp.start()

    def finish(p, land, send_sems, recv_sems, base=0):
        for cp in copies(p, land, send_sems, recv_sems, base):
            cp.wait()

    return dict(ins=list(ps), outs=[jax.ShapeDtypeStruct((3,) + p.shape[1:], p.dtype) for p in ps],
                alias={}, nsem=3 * n, start=start, finish=finish)


def _exchange_swap_result(bufs):
    n = len(bufs)

    def copies(out, send_sems, recv_sems, base, half):
        x, y, c, _ = _mesh_pos()
        h = c if half == "mine" else 1 - c
        return [pltpu.make_async_remote_copy(
            src_ref=out[i].at[h], dst_ref=out[i].at[h], send_sem=send_sems.at[base + i],
            recv_sem=recv_sems.at[base + i], device_id=(x, y, 1 - c), device_id_type=MESH_ID) for i in range(n)]

    def start(cin, out, send_sems, recv_sems, base=0):
        for cp in copies(out, send_sems, recv_sems, base, "mine"):
            cp.start()

    def finish(cin, out, send_sems, recv_sems, base=0):
        for cp in copies(out, send_sems, recv_sems, base, "theirs"):
            cp.wait_recv()
        for cp in copies(out, send_sems, recv_sems, base, "mine"):
            cp.wait_send()

    return dict(ins=list(bufs), outs=[jax.ShapeDtypeStruct(b.shape, b.dtype) for b in bufs],
                alias={i: i for i in range(n)}, nsem=n, start=start, finish=finish)


def _add_own_half(g, land, name):
    rh, cw = g.shape[-2:]
    tm = _tile(rh, max(16, (1 << 20) // (4 * cw)), 16)

    def body(g_ref, l_ref, o_ref):
        o_ref[...] = (g_ref[...] + l_ref[...]).astype(o_ref.dtype)

    mine = (pl.BlockSpec((None, None, tm, cw), lambda s, i: (s, lax.axis_index("c"), i, 0)) if g.ndim == 4
            else pl.BlockSpec((None, tm, cw), lambda s, i: (s, i, 0)))
    return pl.pallas_call(
        body, name=name, grid=(4, rh // tm),
        in_specs=[mine,
                  pl.BlockSpec((None, tm, cw), lambda s, i: (s, i, 0))],
        out_specs=pl.BlockSpec((None, tm, cw), lambda s, i: (s, i, 0)),
        out_shape=jax.ShapeDtypeStruct((4, rh, cw), BF16),
        compiler_params=_params(("parallel", "parallel")),
    )(g, land)


def _sum_pieces(part, land, name):
    _, rh, cw = land.shape
    tm = _tile(rh, max(16, (1 << 20) // (4 * cw)), 16)

    def body(p_ref, l_ref, o_ref):
        v = l_ref[...].astype(F32)
        o_ref[...] = (p_ref[...].astype(F32) + v[0]) + (v[1] + v[2])

    return pl.pallas_call(
        body, name=name, grid=(rh // tm,),
        in_specs=[pl.BlockSpec((None, tm, cw), lambda i: (_my_shard(), i, 0)),
                  pl.BlockSpec((3, tm, cw), lambda i: (0, i, 0))],
        out_specs=pl.BlockSpec((None, tm, cw), lambda i: (lax.axis_index("c"), i, 0)),
        out_shape=jax.ShapeDtypeStruct((2, rh, cw), F32),
        compiler_params=_params(("parallel",)),
    )(part, land)


def _sum8(g, name):
    def body(g_ref, o_ref):
        acc = g_ref[0]
        for k in range(1, 8):
            acc = acc + g_ref[k]
        o_ref[...] = acc

    return pl.pallas_call(body, name=name, out_shape=jax.ShapeDtypeStruct(g.shape[1:], F32))(g)


def _silu(v):
    return v * _sigmoid(v)


def _dsilu(v):
    s = _sigmoid(v)
    return s * (1.0 + v * (1.0 - s))


def _ada_fwd(cin, w, b):
    d, ns = w.shape
    tn = _tile(ns, 512)

    def body(c_ref, w_ref, b_ref, o_ref):
        o_ref[...] = _dot(_silu(c_ref[...]), w_ref[...], HI) + b_ref[...]

    return pl.pallas_call(
        body, name="ada_fwd", grid=(ns // tn,),
        in_specs=[pl.BlockSpec(cin.shape, lambda n: (0, 0)), pl.BlockSpec((d, tn), lambda n: (0, n)),
                  pl.BlockSpec((1, tn), lambda n: (0, n))],
        out_specs=pl.BlockSpec((cin.shape[0], tn), lambda n: (0, n)),
        out_shape=jax.ShapeDtypeStruct((cin.shape[0], ns), F32),
        compiler_params=_params(("parallel",)),
    )(cin, w, b)


def _ada_bwd(cin, w, dm):
    d, ns = w.shape
    tn = _tile(ns, 512)

    def body(c_ref, w_ref, d_ref, dw_ref, dc_ref):
        n = pl.program_id(0)

        @pl.when(n == 0)
        def _():
            dc_ref[...] = jnp.zeros_like(dc_ref)

        dw_ref[...] = _dot_tn(_silu(c_ref[...]), d_ref[...], HI)
        dc_ref[...] += _dot_nt(d_ref[...], w_ref[...], HI)

    return pl.pallas_call(
        body, name="ada_bwd", grid=(ns // tn,),
        in_specs=[pl.BlockSpec(cin.shape, lambda n: (0, 0)), pl.BlockSpec((d, tn), lambda n: (0, n)),
                  pl.BlockSpec((cin.shape[0], tn), lambda n: (0, n))],
        out_specs=[pl.BlockSpec((d, tn), lambda n: (0, n)), pl.BlockSpec(cin.shape, lambda n: (0, 0))],
        out_shape=[jax.ShapeDtypeStruct((d, ns), F32), jax.ShapeDtypeStruct(cin.shape, F32)],
        compiler_params=_params(("arbitrary",)),
    )(cin, w, dm)


def _bias_tables(rpb, rows):
    kh = min(WIN_H, rows)
    fold, onehot, in_win = _bias_selectors(kh)
    t = jnp.einsum("hdc,dsj->hsjc", rpb, jnp.asarray(fold), precision=HI)
    t = jnp.einsum("hsjc,cqk->hsqjk", t, jnp.asarray(onehot), precision=HI)
    t = jnp.where(jnp.asarray(in_win)[None, None, :, None, :], t, MASK_VALUE)
    return t.reshape(rpb.shape[0], kh, GRID_W, kh * GRID_W).astype(F32)


def _bias_selectors(kh):
    col = np.arange(GRID_W)
    col_start = np.clip(col - WIN_W // 2, 0, GRID_W - WIN_W)
    in_win = (col[None, :] >= col_start[:, None]) & (col[None, :] < col_start[:, None] + WIN_W)
    dc_idx = np.clip(col[None, :] - col[:, None], 1 - WIN_W, WIN_W - 1) + WIN_W - 1
    onehot = np.zeros((2 * WIN_W - 1, GRID_W, GRID_W), np.float32)
    qq, kk = np.nonzero(in_win)
    onehot[dc_idx[qq, kk], qq, kk] = 1.0
    fold = np.zeros((2 * WIN_H - 1, kh, kh), np.float32)
    for sh in range(kh):
        for j in range(kh):
            fold[j - sh + WIN_H - 1, sh, j] = 1.0
    return fold, onehot, in_win


def _mm_f32(a, b, name):
    M, K = a.shape
    N = b.shape[1]
    tm = _tile(M, 256, 8)

    def body(a_ref, b_ref, o_ref):
        o_ref[...] = _dot(a_ref[...], b_ref[...], HI)

    return pl.pallas_call(
        body, name=name, grid=(M // tm,),
        in_specs=[pl.BlockSpec((tm, K), lambda i: (i, 0)), pl.BlockSpec((K, N), lambda i: (0, 0))],
        out_specs=pl.BlockSpec((tm, N), lambda i: (i, 0)),
        out_shape=jax.ShapeDtypeStruct((M, N), F32),
        compiler_params=_params(("parallel",)),
    )(a, b)


def _bias_tables_transpose(dbias, rows):
    kh = min(WIN_H, rows)
    nh = dbias.shape[0]
    fold, onehot, _ = _bias_selectors(kh)
    ndc = onehot.shape[0]
    sel = np.zeros((GRID_W * GRID_W, LANES), np.float32)
    sel[:, :ndc] = onehot.reshape(ndc, -1).T
    x = dbias.reshape(nh, kh, GRID_W, kh, GRID_W).transpose(0, 1, 3, 2, 4).reshape(nh * kh * kh, GRID_W * GRID_W)
    z = _mm_f32(x, jnp.asarray(sel), "rpb_fold")[:, :ndc].reshape(nh, kh, kh, ndc)
    return jnp.einsum("dsj,hsjc->hdc", jnp.asarray(fold), z, precision=HI)


def _na_geometry(S):
    rows = S // GRID_W
    kh = min(WIN_H, rows)

    def row_start(r):
        return jnp.clip(r - kh // 2, 0, rows - kh)

    return rows, kh, row_start


def _na_by_head(ref, rr, lane):
    t = ref[rr * GRID_W:(rr + 1) * GRID_W, :]
    zero = jnp.zeros_like(t)
    return jnp.concatenate([jnp.where(lane < NA_HEAD_DIM, t, zero), jnp.where(lane >= NA_HEAD_DIM, t, zero)], axis=0)


def _na_pick_head(t2, lane):
    return jnp.where(lane < NA_HEAD_DIM, t2[:GRID_W], t2[GRID_W:])


def _na_scores(q_ref, k_ref, b_ref, i, nrs, nb, S, L, row_start, lane):
    qh = jnp.concatenate([_na_by_head(q_ref, rr, lane) for rr in range(nrs)], axis=0)
    sc = _dot_nt(qh, k_ref[pl.ds(S, L), :])
    starts, shifts, sb = [], [], []
    for rr in range(nrs):
        r = i * nrs + rr
        rs = row_start(r)
        starts.append(pl.multiple_of(rs * GRID_W, GRID_W))
        shifts.append(r - rs)
        bias = jnp.concatenate([b_ref[0, r - rs], b_ref[1, r - rs]], axis=0)
        sb.append(_dot_nt(qh[rr * 2 * GRID_W:(rr + 1) * 2 * GRID_W], k_ref[pl.ds(starts[-1], nb), :]) + bias)
    return qh, jnp.concatenate(sb, axis=0), sc, starts, shifts


def _na_fwd(qs, kb, vb, bias, S, L):
    T, naw = qs.shape
    rows, kh, row_start = _na_geometry(S)
    nb = kh * GRID_W
    npair = naw // LANES

    nrs = min(NA_ROWS_PER_STEP, rows)
    assert rows % nrs == 0

    def body(q_ref, k_ref, v_ref, b_ref, o_ref, lse_ref):
        i = pl.program_id(1)
        lane = lax.broadcasted_iota(jnp.int32, (GRID_W, LANES), 1)
        _, sb, sc, starts, _ = _na_scores(q_ref, k_ref, b_ref, i, nrs, nb, S, L, row_start, lane)
        m = jnp.maximum(jnp.max(sb, axis=-1, keepdims=True), jnp.max(sc, axis=-1, keepdims=True))
        pb, pc = jnp.exp(sb - m), jnp.exp(sc - m)
        l = jnp.sum(pb, axis=-1, keepdims=True) + jnp.sum(pc, axis=-1, keepdims=True)
        inv = 1.0 / l
        pb16, pc16 = (pb * inv).astype(BF16), (pc * inv).astype(BF16)
        oc = _dot(pc16, v_ref[pl.ds(S, L), :])
        lse = jnp.broadcast_to(m + jnp.log(l), oc.shape)
        for rr in range(nrs):
            two = slice(rr * 2 * GRID_W, (rr + 1) * 2 * GRID_W)
            rsl = slice(rr * GRID_W, (rr + 1) * GRID_W)
            o2 = oc[two] + _dot(pb16[two], v_ref[pl.ds(starts[rr], nb), :])
            o_ref[rsl, :] = _na_pick_head(o2, lane).astype(o_ref.dtype)
            lse_ref[rsl, :] = _na_pick_head(lse[two], lane)

    blk = pl.BlockSpec((nrs * GRID_W, LANES), lambda p, i: (i, p))
    col = pl.BlockSpec((T, LANES), lambda p, i: (0, p))
    return pl.pallas_call(
        body, name="na_fwd", grid=(npair, rows // nrs),
        in_specs=[blk, col, col, pl.BlockSpec((2, kh, GRID_W, nb), lambda p, i: (p, 0, 0, 0))],
        out_specs=[blk, blk],
        out_shape=[jax.ShapeDtypeStruct((S, naw), BF16), jax.ShapeDtypeStruct((S, naw), F32)],
        compiler_params=_params(("parallel", "arbitrary")),
    )(qs, kb, vb, bias)


def _na_bwd(qs, kb, vb, bias, do, o, lse, S, L):
    T, naw = qs.shape
    rows, kh, row_start = _na_geometry(S)
    nb = kh * GRID_W
    npair = naw // LANES

    nrs = min(NA_ROWS_PER_STEP, rows)
    assert rows % nrs == 0

    def body(q_ref, k_ref, v_ref, b_ref, do_ref, o_ref, lse_ref, dq_ref, dk_ref, dv_ref, db_ref):
        i = pl.program_id(1)

        @pl.when(i == 0)
        def _():
            dk_ref[...] = jnp.zeros_like(dk_ref)
            dv_ref[...] = jnp.zeros_like(dv_ref)
            db_ref[...] = jnp.zeros_like(db_ref)

        lane = lax.broadcasted_iota(jnp.int32, (GRID_W, LANES), 1)
        qh, sb, sc, starts, shifts = _na_scores(q_ref, k_ref, b_ref, i, nrs, nb, S, L, row_start, lane)
        doh = jnp.concatenate([_na_by_head(do_ref, rr, lane) for rr in range(nrs)], axis=0)
        o2 = jnp.concatenate([o_ref[rr * GRID_W:(rr + 1) * GRID_W, :] for rr in range(nrs) for _ in range(2)], axis=0)
        lse = jnp.concatenate([lse_ref[rr * GRID_W:(rr + 1) * GRID_W, :][:, hh * NA_HEAD_DIM:hh * NA_HEAD_DIM + 1]
                               for rr in range(nrs) for hh in range(2)], axis=0)
        pb, pc = jnp.exp(sb - lse), jnp.exp(sc - lse)
        delta = jnp.sum(doh.astype(F32) * o2.astype(F32), axis=-1, keepdims=True)
        dpb = jnp.concatenate([_dot_nt(doh[rr * 2 * GRID_W:(rr + 1) * 2 * GRID_W], v_ref[pl.ds(starts[rr], nb), :])
                               for rr in range(nrs)], axis=0)
        dsb = pb * (dpb - delta)
        dsc = pc * (_dot_nt(doh, v_ref[pl.ds(S, L), :]) - delta)
        dsb16, dsc16, pb16, pc16 = dsb.astype(BF16), dsc.astype(BF16), pb.astype(BF16), pc.astype(BF16)
        dqc = _dot(dsc16, k_ref[pl.ds(S, L), :])
        dk_ref[pl.ds(S, L), :] += _dot_tn(dsc16, qh)
        dv_ref[pl.ds(S, L), :] += _dot_tn(pc16, doh)
        for rr in range(nrs):
            two = slice(rr * 2 * GRID_W, (rr + 1) * 2 * GRID_W)
            band = pl.ds(starts[rr], nb)
            dq2 = dqc[two] + _dot(dsb16[two], k_ref[band, :])
            dq_ref[rr * GRID_W:(rr + 1) * GRID_W, :] = _na_pick_head(dq2, lane)
            dk_ref[band, :] += _dot_tn(dsb16[two], qh[two])
            dv_ref[band, :] += _dot_tn(pb16[two], doh[two])
            for hh in range(2):
                db_ref[hh, shifts[rr]] += dsb[(2 * rr + hh) * GRID_W:(2 * rr + hh + 1) * GRID_W]

    blk = pl.BlockSpec((nrs * GRID_W, LANES), lambda p, r: (r, p))
    col = pl.BlockSpec((T, LANES), lambda p, r: (0, p))
    return pl.pallas_call(
        body, name="na_bwd", grid=(npair, rows // nrs),
        in_specs=[blk, col, col, pl.BlockSpec((2, kh, GRID_W, nb), lambda p, r: (p, 0, 0, 0)), blk, blk, blk],
        out_specs=[blk, col, col, pl.BlockSpec((2, kh, GRID_W, nb), lambda p, r: (p, 0, 0, 0))],
        out_shape=[jax.ShapeDtypeStruct((S, naw), F32), jax.ShapeDtypeStruct((T, naw), F32),
                   jax.ShapeDtypeStruct((T, naw), F32), jax.ShapeDtypeStruct(bias.shape, F32)],
        compiler_params=_params(("parallel", "arbitrary")),
    )(qs, kb, vb, bias, do, o, lse)


def _hg_cols(naw, hgf, rev):
    qcol = (3 * naw) // hgf
    fcol = (3 * naw + hgf * (2 if rev else 1)) // hgf
    icol = (3 * naw + 3 * hgf) // hgf
    return qcol, fcol, icol


def _hg_chunk_order(S, L, rev):
    ncl, ncc = S // HG_CHUNK, L // HG_CHUNK
    nc = ncl + ncc

    def chunk_of(i):
        if rev:
            return nc - 1 - i
        return jnp.where(i < ncc, ncl + i, i - ncc)

    return nc, ncl, chunk_of


def _hg_gates(q, z, lb, rev):
    row = lax.broadcasted_iota(jnp.int32, (HG_CHUNK, HG_CHUNK), 0)
    colm = lax.broadcasted_iota(jnp.int32, (HG_CHUNK, HG_CHUNK), 1)
    tri = (colm >= row) if rev else (row >= colm)
    trif = tri.astype(F32)
    sig = _sigmoid(z)
    f = lb + (1.0 - lb) * sig
    lf = jnp.log(f)
    k = 1.0 - f
    cum = _dot(trif, lf, HI)
    mid = cum[HG_CHUNK // 2:HG_CHUNK // 2 + 1, :]
    last = cum[0:1, :] if rev else cum[HG_CHUNK - 1:HG_CHUNK, :]
    eq = jnp.exp(jnp.clip(cum - mid, -EXP_CLAMP, EXP_CLAMP))
    ek = jnp.exp(jnp.clip(mid - cum, -EXP_CLAMP, EXP_CLAMP))
    return tri, trif, sig, f, k, cum, last, eq, ek


def _hg_fwd(u, lbr, S, L, naw, hgf, rev):
    T = S + L
    nh = hgf // HG_DIM
    nc, ncl, chunk_of = _hg_chunk_order(S, L, rev)
    qcol, fcol, icol = _hg_cols(naw, hgf, rev)

    def step(i, q_ref, z_ref, v_ref, lb_ref, o_ref, st_ref, state):
        @pl.when(i == 0)
        def _():
            state[...] = jnp.zeros_like(state)

        q, z, v = q_ref[...], z_ref[...], v_ref[...]
        tri, _, _, _, k, cum, last, eq, ek = _hg_gates(q, z, lb_ref[...], rev)
        qe, ke = (q * eq).astype(BF16), (k * ek).astype(BF16)
        qd, kd = (q * jnp.exp(cum)).astype(BF16), (k * jnp.exp(last - cum)).astype(BF16)
        v16, el = v.astype(BF16), jnp.exp(last)
        for h in range(nh):
            sl = slice(h * HG_DIM, (h + 1) * HG_DIM)
            a = jnp.where(tri, _dot_nt(qe[:, sl], ke[:, sl]), 0.0)
            s0 = state[h]
            st_ref[h] = s0
            o_ref[:, sl] = _dot(a.astype(BF16), v16[:, sl]) + _dot_nt(qd[:, sl], s0.astype(BF16))
            state[h] = s0 * el[:, sl] + _dot_tn(v16[:, sl], kd[:, sl])

    def blk(cb):
        return pl.BlockSpec((HG_CHUNK, hgf), lambda i: (chunk_of(i), cb))

    return dict(
        step=step, nc=nc, operands=[u, u, u, lbr],
        in_specs=[blk(qcol), blk(fcol), blk(icol), pl.BlockSpec((1, hgf), lambda i: (0, 0))],
        out_specs=[pl.BlockSpec((HG_CHUNK, hgf), lambda i: (chunk_of(i), 0)),
                   pl.BlockSpec((None, nh, HG_DIM, HG_DIM), lambda i: (chunk_of(i), 0, 0, 0))],
        out_shape=[jax.ShapeDtypeStruct((T, hgf), F32), jax.ShapeDtypeStruct((nc, nh, HG_DIM, HG_DIM), F32)],
        scratch=[pltpu.VMEM((nh, HG_DIM, HG_DIM), F32)])


def _hg_both(parts, name):
    nin = [len(p["in_specs"]) for p in parts]
    nout = [len(p["out_specs"]) for p in parts]
    nscr = [len(p["scratch"]) for p in parts]

    def body(*refs):
        i = pl.program_id(0)
        ins, outs, scr = refs[:sum(nin)], refs[sum(nin):sum(nin) + sum(nout)], refs[sum(nin) + sum(nout):]
        for d, p in enumerate(parts):
            p["step"](i, *ins[sum(nin[:d]):sum(nin[:d + 1])], *outs[sum(nout[:d]):sum(nout[:d + 1])],
                      *scr[sum(nscr[:d]):sum(nscr[:d + 1])])

    res = pl.pallas_call(
        body, name=name, grid=(parts[0]["nc"],),
        in_specs=[sp for p in parts for sp in p["in_specs"]],
        out_specs=[sp for p in parts for sp in p["out_specs"]],
        out_shape=[sh for p in parts for sh in p["out_shape"]],
        scratch_shapes=[sc for p in parts for sc in p["scratch"]],
        compiler_params=_params(("arbitrary",)),
    )(*[op for p in parts for op in p["operands"]])
    return [list(res[sum(nout[:d]):sum(nout[:d + 1])]) for d in range(len(parts))]


def _hg_bwd(u, lbr, st, do, S, L, naw, hgf, rev):
    T = S + L
    nh = hgf // HG_DIM
    nc, ncl, chunk_fwd = _hg_chunk_order(S, L, rev)
    qcol, fcol, icol = _hg_cols(naw, hgf, rev)

    def chunk_of(j):
        return chunk_fwd(nc - 1 - j)

    def step(j, q_ref, z_ref, v_ref, lb_ref, st_ref, do_ref, dq_ref, dz_ref, dv_ref, dlb_ref, dstate,
             dqe_s, dke_s, dqd_s, dkd_s, dl_s):
        @pl.when(j == 0)
        def _():
            dstate[...] = jnp.zeros_like(dstate)
            dlb_ref[...] = jnp.zeros_like(dlb_ref)

        q, z, v = q_ref[...], z_ref[...], v_ref[...]
        lb = lb_ref[...]
        tri, trif, sig, f, k, cum, last, eq, ek = _hg_gates(q, z, lb, rev)
        ec, el, ekd = jnp.exp(cum), jnp.exp(last), jnp.exp(last - cum)
        qe, ke, qd, kd = q * eq, k * ek, q * ec, k * ekd
        qd16, kd16 = qd.astype(BF16), kd.astype(BF16)
        dout = jnp.where(chunk_of(j) < ncl, do_ref[...], 0.0)
        qe2, ke2, v2, dout2 = _split2(qe), _split2(ke), _split2(v), _split2(dout)
        for h in range(nh):
            sl = slice(h * HG_DIM, (h + 1) * HG_DIM)
            qeh, keh, vh, douth = [(t[0][:, sl], t[1][:, sl]) for t in (qe2, ke2, v2, dout2)]
            a = jnp.where(tri, _dot_nt(qeh[0], keh[0]), 0.0).astype(BF16)
            s0 = st_ref[h]
            ds1 = dstate[h]
            s02, ds12 = _split2(s0), _split2(ds1)
            dv_ref[:, sl] = (_dot_tn(a, douth[0]) + _dot_nt(kd16[:, sl], ds12[0])).astype(dv_ref.dtype)
            da2 = _split2(jnp.where(tri, _dot_x3(_dot_nt, douth, vh), 0.0))
            dqe_s[:, sl] = _dot_x3(_dot, da2, keh)
            dke_s[:, sl] = _dot_x3(_dot_tn, da2, qeh)
            dqd_s[:, sl] = _dot_x3(_dot, douth, s02)
            dkd_s[:, sl] = _dot_x3(_dot, vh, ds12)
            dl_s[:, sl] = _colsum(ds1 * s0)
            dstate[h] = _dot_tn(douth[0], qd16[:, sl]) + ds1 * el[:, sl]
        dqe, dke, dqd, dkd = dqe_s[...], dke_s[...], dqd_s[...], dkd_s[...]
        dq_ref[...] = (dqe * eq + dqd * ec).astype(dq_ref.dtype)
        dk = dke * ek + dkd * ekd
        dcum = dqe * qe - dke * ke + dqd * qd - dkd * kd
        dlast = _colsum(dkd * kd) + el * dl_s[...]
        dlf = _dot_tn(trif, dcum, HI) + dlast
        df = dlf / f - dk
        dz_ref[...] = (df * (1.0 - lb) * sig * (1.0 - sig)).astype(dz_ref.dtype)
        dlb_ref[...] += _colsum(df * (1.0 - sig))

    def blk(cb):
        return pl.BlockSpec((HG_CHUNK, hgf), lambda j: (chunk_of(j), cb))

    oblk = pl.BlockSpec((HG_CHUNK, hgf), lambda j: (chunk_of(j), 0))
    wide = pltpu.VMEM((HG_CHUNK, hgf), F32)
    return dict(
        step=step, nc=nc, operands=[u, u, u, lbr, st, do],
        in_specs=[blk(qcol), blk(fcol), blk(icol), pl.BlockSpec((1, hgf), lambda j: (0, 0)),
                  pl.BlockSpec((None, nh, HG_DIM, HG_DIM), lambda j: (chunk_of(j), 0, 0, 0)),
                  pl.BlockSpec((HG_CHUNK, hgf), lambda j: (jnp.minimum(chunk_of(j), ncl - 1), 0))],
        out_specs=[oblk, oblk, oblk, pl.BlockSpec((1, hgf), lambda j: (0, 0))],
        out_shape=[jax.ShapeDtypeStruct((T, hgf), BF16)] * 3 + [jax.ShapeDtypeStruct((1, hgf), F32)],
        scratch=[pltpu.VMEM((nh, HG_DIM, HG_DIM), F32), wide, wide, wide, wide, pltpu.VMEM((1, hgf), F32)])


def _adamw(w, g, m, v, name):
    shape = w.shape
    if w.ndim != 2 or shape[0] % 8 or shape[1] % LANES:
        w, g, m, v = [a.reshape(1, -1) for a in (w, g, m, v)]
    r, cw = w.shape
    tm = _tile(r, max(8, (1 << 19) // cw), 8) if r % 8 == 0 else r
    c1 = 1.0 / (1.0 - ADAM_B1 ** ADAM_STEP)
    c2 = 1.0 / (1.0 - ADAM_B2 ** ADAM_STEP)

    def body(w_ref, g_ref, m_ref, v_ref, d_ref, nm_ref, nv_ref):
        gg = g_ref[...]
        nm = ADAM_B1 * m_ref[...] + (1.0 - ADAM_B1) * gg
        nv = ADAM_B2 * v_ref[...] + (1.0 - ADAM_B2) * (gg * gg)
        d_ref[...] = -ADAM_LR * ((nm * c1) / (jnp.sqrt(nv * c2) + ADAM_EPS) + ADAM_WD * w_ref[...])
        nm_ref[...] = nm
        nv_ref[...] = nv

    spec = pl.BlockSpec((tm, cw), lambda i: (i, 0))
    outs = pl.pallas_call(
        body, name=name, grid=(r // tm,), in_specs=[spec] * 4, out_specs=[spec] * 3,
        out_shape=[jax.ShapeDtypeStruct((r, cw), F32)] * 3,
        compiler_params=_params(("parallel",)),
    )(w, g, m, v)
    return [o.reshape(shape) for o in outs]


def kernel(x, c, ctx, c_ctx, w_ada, b_ada, norm1_g, w_in, na_rpb, hg_lb_logits, hg_norm_g, w_pa, w_pb, w_out, norm2_g, w_ffn_in, w_ffn_out, final_g, loss_target, m_c_ctx, m_w_ada, m_b_ada, m_norm1_g, m_w_in, m_na_rpb, m_hg_lb_logits, m_hg_norm_g, m_w_pa, m_w_pb, m_w_out, m_norm2_g, m_w_ffn_in, m_w_ffn_out, m_final_g, v_c_ctx, v_w_ada, v_b_ada, v_norm1_g, v_w_in, v_na_rpb, v_hg_lb_logits, v_hg_norm_g, v_w_pa, v_w_pb, v_w_out, v_norm2_g, v_w_ffn_in, v_w_ffn_out, v_final_g):
    xi, yi, ci = lax.axis_index("x"), lax.axis_index("y"), lax.axis_index("c")
    sidx = 2 * xi + yi
    eidx = 4 * xi + 2 * yi + ci

    S, D = x.shape[1], x.shape[2]
    L = ctx.shape[1]
    T = S + L
    naw = NA_HEADS * NA_HEAD_DIM
    hgf = HG_HEADS * HG_DIM
    inw = 3 * naw + 5 * hgf + 2 * D
    fh = w_ffn_out.shape[1] * 4
    ads = w_ada.shape[2]
    fs = hg_lb_logits.shape[2]
    rows = S // GRID_W
    tr = _tile(L, 256)
    nlat, nall = S // tr, T // tr
    assert naw == hgf and D % naw == 0 and S % tr == 0 and 2 * hgf <= D

    pack0 = jnp.concatenate([c, jnp.pad(hg_lb_logits.reshape(1, -1), ((0, 0), (0, D - 4 * fs))),
                             jnp.zeros((6, D), F32)], axis=0)
    g0 = _all_gather8(pack0).reshape(8, 8, D)
    cs = g0[:, 0]
    lbl = g0[::2, 1, :4 * fs].reshape(4, 2, 2, fs).transpose(1, 2, 0, 3).reshape(2, 2, 4 * fs)
    p_lb = jax.nn.softmax(lbl, axis=0)
    lb = p_lb[0]
    lbb = [lb[d].reshape(1, hgf) for d in range(2)]

    cin = jnp.concatenate([cs, c_ctx[None], jnp.zeros((7, D), F32)], axis=0)
    b_sh = lax.dynamic_slice(b_ada, (0, sidx * ads), (1, ads))
    modp = _ada_fwd(cin, w_ada[0], b_sh)
    modfull = _all_gather8(modp).reshape(8, 16, ads)[::2].transpose(1, 0, 2).reshape(16, 4 * ads)
    mod_e = jnp.pad(lax.dynamic_index_in_dim(modfull, eidx, 0, keepdims=False).reshape(N_MOD, D), ((0, 2), (0, 0)))
    mod_c = jnp.pad(modfull[8].reshape(N_MOD, D), ((0, 2), (0, 0)))

    names = ["w_in", "w_pa", "w_pb", "w_out", "w_ffn_in", "w_ffn_out"]
    placed = [_cast_place(w[0], "cast_" + nm)
              for w, nm in zip((w_in, w_pa, w_pb, w_out, w_ffn_in, w_ffn_out), names)]
    def shards(g):
        return g.reshape(4, 2 * g.shape[2], g.shape[3])

    win3 = shards(_exchange_call(_exchange_gather(placed[:1]), "gather_w_in")[0])

    x2d, ctx2d = x[0], ctx[0]

    def f_ln1(i, rv, vv):
        xl, xc = rv
        g, me, mc = vv
        isc = i >= nlat
        xt = jnp.where(isc, xc, xl)
        sh = jnp.where(isc, mc[0:1], me[0:1])
        sc = jnp.where(isc, mc[1:2], me[1:2])
        h = xt * _rms(xt) * g * (1.0 + sc) + sh
        return [h, h], []

    hb, hbt = _rowwise(f_ln1, nall, tr, [(x2d, D, 0, lambda i: jnp.minimum(i, nlat - 1)),
                                        (ctx2d, D, 0, lambda i: jnp.maximum(i - nlat, 0))],
                       [norm1_g, mod_e, mod_c], [(D, BF16), (D, BF16, "T")], [], "ln1")
    u, gathered = _mm_nn(hb, win3, F32, "mm_in", carry=_exchange_gather(placed[1:]))
    wpa3, wpb3, wout3, wi3, wfo3 = [shards(g) for g in gathered]
    wout1 = wout3.reshape(1, D, D)
    wfo1 = wfo3.reshape(1, fh, D)

    scale = NA_HEAD_DIM ** -0.5

    def f_qkv(i, rv, vv):
        q, k, v = rv
        return [q * scale, k, v], []

    qs, kb, vb = _rowwise(f_qkv, nall, tr, [(u, naw, 0, None), (u, naw, 1, None), (u, naw, 2, None)], [],
                          [(naw, BF16)] * 3, [], "qkv_cast")
    bias = _bias_tables(na_rpb[0], rows)
    o_na, lse = _na_fwd(qs, kb, vb, bias, S, L)

    (o_f, st_f), (o_b, st_b) = _hg_both(
        [_hg_fwd(u, lbb[0], S, L, naw, hgf, False), _hg_fwd(u, lbb[1], S, L, naw, hgf, True)], "hg_fwd")

    hgn = jnp.tile(hg_norm_g, (1, HG_HEADS))
    hog_cb = (3 * naw + 4 * hgf) // hgf
    ga_cb = (3 * naw + 5 * hgf) // D
    gb_cb = ga_cb + 1

    def heads_rms(o):
        return jnp.concatenate([jnp.broadcast_to(_rms(o[:, h * HG_DIM:(h + 1) * HG_DIM]), (o.shape[0], HG_DIM))
                                for h in range(HG_HEADS)], axis=1)

    def f_readout(i, rv, vv):
        of, ob_, hog = rv
        g, = vv
        o = of + ob_
        return [o * heads_rms(o) * g * _silu(hog)], []

    ob, = _rowwise(f_readout, nlat, tr, [(o_f, hgf, 0, None), (o_b, hgf, 0, None), (u, hgf, hog_cb, None)],
                   [hgn], [(hgf, BF16)], [], "hg_readout")

    ya = _mm_nn(o_na, wpa3, F32, "mm_pa")
    yb = _mm_nn(ob, wpb3, F32, "mm_pb")

    def f_merge(i, rv, vv):
        ya_, yb_, ga, gb = rv
        return [_sigmoid(ga) * ya_ + _sigmoid(gb) * yb_], []

    yv, = _rowwise(f_merge, nlat, tr, [(ya, D, 0, None), (yb, D, 0, None), (u, D, ga_cb, None), (u, D, gb_cb, None)],
                   [], [(D, BF16)], [], "merge")
    z = _mm_nn(yv, wout1, F32, "mm_out")

    def f_res1(i, rv, vv):
        xt, zt = rv
        g, me = vv
        x1 = xt + me[2:3] * zt
        h = x1 * _rms(x1) * g * (1.0 + me[4:5]) + me[3:4]
        return [x1, h, h], []

    x1, h2, h2t = _rowwise(f_res1, nlat, tr, [(x2d, D, 0, None), (z, D, 0, None)], [norm2_g, mod_e],
                           [(D, F32), (D, BF16), (D, BF16, "T")], [], "res1_ln2")
    au3, sw, swt = _ffn_in_fused(h2, wi3, "mm_ffn_in")
    ff =_mm_nn(sw, wfo1, F32, "mm_ffn_out")

    fg = final_g.reshape(1, D)

    def f_final(i, rv, vv):
        x1t, ft, tg = rv
        g, me = vv
        x2 = x1t + me[5:6] * ft
        r3 = _rms(x2)
        xn = x2 * r3
        err = xn * g - tg
        dyy = err * (1.0 / D)
        dxn = dyy * g
        dx2 = r3 * (dxn - xn * jnp.mean(dxn * xn, axis=-1, keepdims=True))
        return [dx2, dx2 * me[5:6]], [_colsum(err * err), _colsum(dyy * xn), _colsum(dx2 * ft)]

    dx2, dfb, loss_cols, dfg, dg2 = _rowwise(
        f_final, nlat, tr, [(x1, D, 0, None), (ff, D, 0, None), (loss_target[0], D, 0, None)], [fg, mod_e],
        [(D, F32), (D, BF16)], [D, D, D], "final_loss")

    def dswiglu_epi(tm, tn):
        blk = pl.BlockSpec((2, tm, tn), lambda m, n, k: (0, m, n))

        def fn(d, ins):
            a, uu = ins[0][0], ins[0][1]
            return [(d * uu * _dsilu(a), d * _silu(a))]

        return dict(ins=[(au3, blk)], outs=[(jax.ShapeDtypeStruct((2, S, fh), BF16), blk)], fn=fn)

    dau3, = _mm_nt(dfb, wfo1, BF16, "mm_d_sw", epi=dswiglu_epi, tn_target=512)
    g_wfo = _mm_tn(swt, dfb, 1, "mm_dw_ffn_out", a_is_t=True).reshape(4, 2, fh // 8, D)
    dh2 = _mm_nt(dau3, wi3, F32, "mm_d_h2")
    g_wi = _mm_tn(h2t, dau3, 4, "mm_dw_ffn_in", a_is_t=True)

    def f_ln2_bwd(i, rv, vv):
        dh, x1t, dx2t, zt = rv
        g, me = vv
        r2 = _rms(x1t)
        xn = x1t * r2
        dxn = dh * g * (1.0 + me[4:5])
        dx1 = dx2t + r2 * (dxn - xn * jnp.mean(dxn * xn, axis=-1, keepdims=True))
        return ([dx1, dx1 * me[2:3]],
                [_colsum(dh), _colsum(dh * xn * g), _colsum(dh * xn * (1.0 + me[4:5])), _colsum(dx1 * zt)])

    dx1, dzb, dsh2, dsc2, dn2g, dg1 = _rowwise(
        f_ln2_bwd, nlat, tr, [(dh2, D, 0, None), (x1, D, 0, None), (dx2, D, 0, None), (z, D, 0, None)],
        [norm2_g, mod_e], [(D, F32), (D, BF16)], [D, D, D, D], "ln2_bwd")

    g_wout = _mm_tn(yv, dzb, 1, "mm_dw_out").reshape(4, 2, D // 8, D)

    def dmerge_epi(tm, tn):
        blk = pl.BlockSpec((tm, tn), lambda m, n, k: (m, n))

        def gate(cb):
            return pl.BlockSpec((tm, tn), lambda m, n, k: (m, cb * (D // tn) + n))

        def fn(d, ins):
            ya_, yb_, ga, gb = ins
            sa, sb_ = _sigmoid(ga), _sigmoid(gb)
            return [d * sa, d * sb_, d * ya_ * sa * (1.0 - sa), d * yb_ * sb_ * (1.0 - sb_)]

        return dict(ins=[(ya, blk), (yb, blk), (u, gate(ga_cb)), (u, gate(gb_cb))],
                    outs=[(jax.ShapeDtypeStruct((S, D), BF16), blk)] * 4, fn=fn)

    dya, dyb, dga, dgb = _mm_nt(dzb, wout1, BF16, "mm_d_y", epi=dmerge_epi, tn_target=512)
    d_ona = _mm_nt(dya, wpa3, BF16, "mm_d_ona")
    d_ob = _mm_nt(dyb, wpb3, F32, "mm_d_ob")
    g_wpa = _mm_tn(o_na, dya, 4, "mm_dw_pa")
    g_wpb = _mm_tn(ob, dyb, 4, "mm_dw_pb")

    def f_dreadout(i, rv, vv):
        d, of, ob_, hog = rv
        g, = vv
        o = of + ob_
        on = o * heads_rms(o)
        t = d * _silu(hog) * g
        mt = jnp.concatenate([jnp.broadcast_to(jnp.mean((t * on)[:, h * HG_DIM:(h + 1) * HG_DIM], axis=-1,
                                                        keepdims=True), (o.shape[0], HG_DIM))
                              for h in range(HG_HEADS)], axis=1)
        do_ = heads_rms(o) * (t - on * mt)
        return [do_, d * on * g * _dsilu(hog)], [_colsum(d * _silu(hog) * on)]

    do_hg, dhog, dhgn = _rowwise(
        f_dreadout, nlat, tr, [(d_ob, hgf, 0, None), (o_f, hgf, 0, None), (o_b, hgf, 0, None),
                               (u, hgf, hog_cb, None)], [hgn], [(hgf, F32), (hgf, BF16)], [hgf], "hg_readout_bwd")

    (dq_f, dz_f, dv_f, dlb_f), (dq_b, dz_b, dv_b, dlb_b) = _hg_both(
        [_hg_bwd(u, lbb[0], st_f, do_hg, S, L, naw, hgf, False), _hg_bwd(u, lbb[1], st_b, do_hg, S, L, naw, hgf, True)],
        "hg_bwd")
    dq_na, dk_na, dv_na, dbias = _na_bwd(qs, kb, vb, bias, d_ona, o_na, lse, S, L)

    ta = _tile(L, 128)
    nla, naa = S // ta, T // ta
    lat = lambda i: jnp.minimum(i, nla - 1)

    def f_assemble(i, rv, vv):
        dqn, dk, dv, dqf, dqb, dzf, dzb_, dvf, dvb, dho, dga_, dgb_ = rv
        keep = (i < nla).astype(F32)
        f32 = lambda t: t.astype(F32)
        return [jnp.concatenate([dqn * (scale * keep), dk, dv, f32(dqf) + f32(dqb), f32(dzf), f32(dzb_),
                                 f32(dvf) + f32(dvb),
                                 dho.astype(F32) * keep, dga_.astype(F32) * keep, dgb_.astype(F32) * keep],
                                axis=1)], []

    du, = _rowwise(
        f_assemble, naa, ta,
        [(dq_na, naw, 0, lat), (dk_na, naw, 0, None), (dv_na, naw, 0, None), (dq_f, hgf, 0, None),
         (dq_b, hgf, 0, None), (dz_f, hgf, 0, None), (dz_b, hgf, 0, None), (dv_f, hgf, 0, None),
         (dv_b, hgf, 0, None), (dhog, hgf, 0, lat), (dga, D, 0, lat), (dgb, D, 0, lat)],
        [], [(inw, BF16)], [], "assemble_du")

    def add_half(g, land, nm):
        return _add_own_half(g, land, "rs_add_" + nm)

    early = [g_wpa, g_wpb, g_wout, g_wi, g_wfo]
    g_win_other, lands = _mm_tn_half(hbt, du, 4, True, "mm_dw_in_other", carry=_exchange_swap_other_half(early))
    parts = [add_half(g, l, nm) for g, l, nm in zip(early, lands, names[1:])]
    g_win_mine, landed = _mm_tn_half(hbt, du, 4, False, "mm_dw_in_mine", carry=_exchange_join(
        _exchange_scatter(parts[3:4]), _exchange_swap_other_half([g_win_other])))
    piece_wi = landed[0]
    parts = [add_half(g_win_mine, landed[1], names[0])] + parts
    dh, landed = _mm_nt(du, win3, F32, "mm_d_h", carry=_exchange_scatter(parts[:4] + parts[5:]))
    pieces = landed[:4] + [piece_wi] + landed[4:]
    halves = [_sum_pieces(p, l, "rs_sum_" + nm) for p, l, nm in zip(parts, pieces, names)]
    g_win, g_wpa, g_wpb, g_wout, g_wi, g_wfo = [
        f.reshape(2 * f.shape[1], f.shape[2])
        for f in _exchange_call(_exchange_swap_result(halves), "rs_swap_result_half")]

    def f_ln1_bwd(i, rv, vv):
        dht, xt, dx1t = rv
        g, me = vv
        r1 = _rms(xt)
        xn = xt * r1
        dxn = dht * g * (1.0 + me[1:2])
        dx = dx1t + r1 * (dxn - xn * jnp.mean(dxn * xn, axis=-1, keepdims=True))
        return [dx], [_colsum(dht), _colsum(dht * xn * g), _colsum(dht * xn * (1.0 + me[1:2]))]

    grad_x, dsh1, dsc1, dn1g_l = _rowwise(
        f_ln1_bwd, nlat, tr, [(dh, D, 0, None), (x2d, D, 0, None), (dx1, D, 0, None)], [norm1_g, mod_e],
        [(D, F32)], [D, D, D], "ln1_bwd")

    def f_ln1_bwd_ctx(i, rv, vv):
        dht, xt = rv
        g, mc = vv
        xn = xt * _rms(xt)
        return [], [_colsum(dht), _colsum(dht * xn * g), _colsum(dht * xn * (1.0 + mc[1:2]))]

    ctx_rows = lambda i: i + nlat
    dsh1c, dsc1c, dn1g_c = _rowwise(
        f_ln1_bwd_ctx, nall - nlat, tr, [(dh, D, 0, ctx_rows), (ctx2d, D, 0, None)], [norm1_g, mod_c],
        [], [D, D, D], "ln1_bwd_ctx")

    drpb = _bias_tables_transpose(dbias, rows).reshape(1, -1)
    nrp = -(-drpb.shape[1] // D)
    drpb_rows = jnp.pad(drpb, ((0, 0), (0, nrp * D - drpb.shape[1]))).reshape(nrp, D)
    dlb = jnp.concatenate([dlb_f, dlb_b], axis=1)
    dhg = jnp.sum(dhgn.reshape(HG_HEADS, HG_DIM), axis=0, keepdims=True)

    def wide(v):
        return jnp.pad(v, ((0, 0), (0, D - v.shape[1])))

    pack_rows = [loss_cols, dfg, dn2g, dn1g_l + dn1g_c, dsh1, dsc1, dg1, dsh2, dsc2, dg2, dsh1c, dsc1c,
                 wide(dhg), wide(dlb), drpb_rows]
    pack = jnp.concatenate(pack_rows, axis=0)
    npk = -(-pack.shape[0] // 8) * 8
    pack = jnp.pad(pack, ((0, npk - pack.shape[0]), (0, 0)))
    gp = _all_gather8(pack).reshape(8, npk, D)
    tot = _sum8(gp, "sum_small_grads")

    loss = (0.5 / D) * jnp.sum(tot[0])
    grad_final_g = tot[1]
    grad_norm2_g = tot[2:3]
    grad_norm1_g = tot[3:4]
    grad_hg_norm_g = tot[12:13, :HG_DIM]
    dlb_tot = tot[13, :2 * hgf].reshape(2, hgf)
    grad_na_rpb = tot[14:14 + nrp].reshape(-1)[:drpb.shape[1]].reshape(na_rpb.shape)
    dlog = jnp.stack([dlb_tot * p_lb[0] * (1.0 - p_lb[0]), -dlb_tot * p_lb[0] * p_lb[1]], axis=0)
    grad_hg_lb = lax.dynamic_slice(dlog, (0, 0, sidx * fs), (2, 2, fs))

    dmod_all = gp[:, 4:10].reshape(8, N_MOD * D)
    dmod_ctx = jnp.concatenate([tot[10], tot[11], jnp.zeros((4 * D,), F32)])[None]
    dm16 = jnp.concatenate([dmod_all, dmod_ctx, jnp.zeros((7, N_MOD * D), F32)], axis=0)
    grad_b_ada = jnp.sum(dm16, axis=0, keepdims=True)
    dm_sh = lax.dynamic_slice(dm16, (0, sidx * ads), (16, ads))
    g_wada, dcin = _ada_bwd(cin, w_ada[0], dm_sh)
    gc = _all_gather8(dcin[8:16]).reshape(8, 8, D)
    grad_c_ctx = (gc[0, 0] + gc[2, 0] + gc[4, 0] + gc[6, 0]) * _dsilu(c_ctx)

    grads = {
        "c_ctx": grad_c_ctx, "w_ada": g_wada[None], "b_ada": grad_b_ada, "norm1_g": grad_norm1_g,
        "w_in": g_win[None], "na_rpb": grad_na_rpb, "hg_lb_logits": grad_hg_lb, "hg_norm_g": grad_hg_norm_g,
        "w_pa": g_wpa[None], "w_pb": g_wpb[None], "w_out": g_wout[None], "norm2_g": grad_norm2_g,
        "w_ffn_in": g_wi[None], "w_ffn_out": g_wfo[None], "final_g": grad_final_g,
    }
    weights = {
        "c_ctx": (c_ctx, m_c_ctx, v_c_ctx), "w_ada": (w_ada, m_w_ada, v_w_ada), "b_ada": (b_ada, m_b_ada, v_b_ada),
        "norm1_g": (norm1_g, m_norm1_g, v_norm1_g), "w_in": (w_in, m_w_in, v_w_in),
        "na_rpb": (na_rpb, m_na_rpb, v_na_rpb), "hg_lb_logits": (hg_lb_logits, m_hg_lb_logits, v_hg_lb_logits),
        "hg_norm_g": (hg_norm_g, m_hg_norm_g, v_hg_norm_g), "w_pa": (w_pa, m_w_pa, v_w_pa),
        "w_pb": (w_pb, m_w_pb, v_w_pb), "w_out": (w_out, m_w_out, v_w_out),
        "norm2_g": (norm2_g, m_norm2_g, v_norm2_g), "w_ffn_in": (w_ffn_in, m_w_ffn_in, v_w_ffn_in),
        "w_ffn_out": (w_ffn_out, m_w_ffn_out, v_w_ffn_out), "final_g": (final_g, m_final_g, v_final_g),
    }
    order = list(weights)
    deltas, new_ms, new_vs = [], [], []
    for nm in order:
        w, m, v = weights[nm]
        g = grads[nm].reshape(w.shape)
        grads[nm] = g
        if w.ndim == 3 and w.shape[0] == 1:
            d_, m_, v_ = _adamw(w[0], g[0], m[0], v[0], "adamw_" + nm)
            d_, m_, v_ = d_[None], m_[None], v_[None]
        else:
            d_, m_, v_ = _adamw(w, g, m, v, "adamw_" + nm)
        deltas.append(d_)
        new_ms.append(m_)
        new_vs.append(v_)

    return (loss, grad_x[None], *[grads[nm] for nm in order], *deltas, *new_ms, *new_vs)
```

```python
import numpy as np

import jax
import jax.numpy as jnp
from jax import lax
from jax.experimental import pallas as pl
from jax.experimental.pallas import tpu as pltpu

F32 = jnp.float32
BF16 = jnp.bfloat16

GRID_W = 64
WIN_H = 8
WIN_W = 16
NA_HEADS = 16
NA_HEAD_DIM = 64
HG_HEADS = 8
HG_DIM = 128
HG_CHUNK = 64
N_MOD = 6
EPS = 1e-6
ADAM_LR = 0.001
ADAM_B1 = 0.9
ADAM_B2 = 0.999
ADAM_EPS = 1e-08
ADAM_WD = 0.01
ADAM_STEP = 10

LANES = 128
NA_ROWS_PER_STEP = 16
VMEM_LIMIT = 56 * 1024 * 1024
MASK_VALUE = -1e30
EXP_CLAMP = 80.0
MESH_ID = pl.DeviceIdType.MESH
HI = lax.Precision.HIGHEST


def _tile(dim, target, mult=LANES):
    best = None
    t = mult
    while t <= min(dim, target):
        if dim % t == 0:
            best = t
        t += mult
    assert best is not None, (dim, target, mult)
    return best


def _params(sem):
    return pltpu.CompilerParams(dimension_semantics=sem, vmem_limit_bytes=VMEM_LIMIT)


def _dot(a, b, precision=None):
    return jnp.dot(a, b, preferred_element_type=F32, precision=precision)


def _dot_nt(a, b, precision=None):
    return lax.dot_general(a, b, (((1,), (1,)), ((), ())), preferred_element_type=F32, precision=precision)


def _dot_tn(a, b, precision=None):
    return lax.dot_general(a, b, (((0,), (0,)), ((), ())), preferred_element_type=F32, precision=precision)


def _split2(v):
    hi = v.astype(BF16)
    return hi, (v - hi.astype(F32)).astype(BF16)


def _dot_x3(dot, a2, b2):
    return dot(a2[0], b2[0]) + (dot(a2[0], b2[1]) + dot(a2[1], b2[0]))


def _sigmoid(v):
    return 1.0 / (1.0 + jnp.exp(-v))


def _mm_call(dot, operands, grid, in_specs, out_spec, out_shape, acc_shape, name, carry=None, epi=None):
    nk = grid[2]
    nci = 0 if carry is None else len(carry["ins"])
    nco = 0 if carry is None else len(carry["outs"])
    nei = 0 if epi is None else len(epi["ins"])
    neo = 1 if epi is None else len(epi["outs"])

    def body(*refs):
        a_ref, b_ref = refs[:2]
        ein = refs[2:2 + nei]
        cin = refs[2 + nei:2 + nei + nci]
        outs = refs[2 + nei + nci:2 + nei + nci + neo]
        cout = refs[2 + nei + nci + neo:2 + nei + nci + neo + nco]
        acc = refs[2 + nei + nci + neo + nco]
        sems = refs[3 + nei + nci + neo + nco:]
        m, n, k = pl.program_id(0), pl.program_id(1), pl.program_id(2)

        if carry is not None:
            @pl.when((m == 0) & (n == 0) & (k == 0))
            def _():
                carry["start"](cin, cout, *sems)

        @pl.when(k == 0)
        def _():
            acc[...] = jnp.zeros_like(acc)

        if carry is not None and "mid" in carry:
            @pl.when((m == grid[0] - 1) & (n == 0) & (k == 0))
            def _():
                carry["mid"](cin, cout, *sems)

        acc[...] += dot(a_ref[...], b_ref[...])

        @pl.when(k == nk - 1)
        def _():
            vals = [acc[...]] if epi is None else epi["fn"](acc[...], [r[...] for r in ein])
            for r, v in zip(outs, vals):
                if isinstance(v, tuple):
                    for i, vi in enumerate(v):
                        r[i] = vi.astype(r.dtype)
                else:
                    r[...] = v.astype(r.dtype)

        if carry is not None:
            @pl.when((m == grid[0] - 1) & (n == grid[1] - 1) & (k == nk - 1))
            def _():
                carry["finish"](cin, cout, *sems)

    any_spec = pl.BlockSpec(memory_space=pl.ANY)
    scratch = [pltpu.VMEM(acc_shape, F32)]
    extra = {}
    if carry is not None:
        scratch += [pltpu.SemaphoreType.DMA((carry["nsem"],)), pltpu.SemaphoreType.DMA((carry["nsem"],))]
        extra["input_output_aliases"] = {2 + nei + i: neo + j for i, j in carry["alias"].items()}
    sem = ("arbitrary",) * 3 if carry is not None else ("parallel", "parallel", "arbitrary")
    main_outs = [(out_shape, out_spec)] if epi is None else list(epi["outs"])
    res = pl.pallas_call(
        body, name=name, grid=grid,
        in_specs=list(in_specs) + ([] if epi is None else [sp for _, sp in epi["ins"]]) + [any_spec] * nci,
        out_specs=[sp for _, sp in main_outs] + [any_spec] * nco,
        out_shape=[sh for sh, _ in main_outs] + ([] if carry is None else list(carry["outs"])),
        scratch_shapes=scratch, compiler_params=_params(sem), **extra,
    )(*operands, *([] if epi is None else [ar for ar, _ in epi["ins"]]), *([] if carry is None else carry["ins"]))
    res = list(res)
    main = res[0] if epi is None else res[:neo]
    return main if carry is None else (main, res[neo:])


def _mm_nn(a, b3, out_dtype, name, carry=None):
    M, K = a.shape
    nsh, _, Ns = b3.shape
    tm, tn, tk = _tile(M, 1024), _tile(Ns, 1408), _tile(K, 2816)
    tps, nk = Ns // tn, K // tk
    return _mm_call(
        _dot, (a, b3), (M // tm, nsh * tps, nk),
        [pl.BlockSpec((tm, tk), lambda m, n, k: (m, k)),
         pl.BlockSpec((None, tk, tn), lambda m, n, k: (n // tps, k, n % tps))],
        pl.BlockSpec((tm, tn), lambda m, n, k: (m, n)),
        jax.ShapeDtypeStruct((M, nsh * Ns), out_dtype), (tm, tn), name, carry)


def _mm_nt(a, b3, out_dtype, name, carry=None, epi=None, tn_target=1408):
    a3 = a if a.ndim == 3 else a[None]
    na, M, Ka = a3.shape
    nsh, Kw, Ns = b3.shape
    assert na * Ka == nsh * Ns
    tm, tn, tk = _tile(M, 1024), _tile(Kw, tn_target), _tile(int(np.gcd(Ka, Ns)), 3072)
    kpa, kps = Ka // tk, Ns // tk
    return _mm_call(
        _dot_nt, (a3, b3), (M // tm, Kw // tn, nsh * kps),
        [pl.BlockSpec((None, tm, tk), lambda m, n, k: (k // kpa, m, k % kpa)),
         pl.BlockSpec((None, tn, tk), lambda m, n, k: (k // kps, n, k % kps))],
        pl.BlockSpec((tm, tn), lambda m, n, k: (m, n)),
        jax.ShapeDtypeStruct((M, Kw), out_dtype), (tm, tn), name, carry,
        None if epi is None else epi(tm, tn))


def _mm_tn(a, g, nsh, name, carry=None, a_is_t=False):
    Tk, M = a.shape[::-1] if a_is_t else a.shape
    g3 = g if g.ndim == 3 else g[None]
    ng, _, Ng = g3.shape
    Ns = ng * Ng // nsh
    tm, tn, tk = _tile(M // 2, 1408), _tile(int(np.gcd(Ng, Ns)), 1408), _tile(Tk, 2048)
    mh, tps, tpg, nk = (M // 2) // tm, Ns // tn, Ng // tn, Tk // tk
    return _mm_call(
        _dot if a_is_t else _dot_tn, (a, g3), (M // tm, nsh * tps, nk),
        [pl.BlockSpec((tm, tk), lambda m, n, k: (m, k)) if a_is_t else pl.BlockSpec((tk, tm), lambda m, n, k: (k, m)),
         pl.BlockSpec((None, tk, tn), lambda m, n, k: (n // tpg, k, n % tpg))],
        pl.BlockSpec((None, None, tm, tn), lambda m, n, k: (n // tps, m // mh, m % mh, n % tps)),
        jax.ShapeDtypeStruct((nsh, 2, M // 2, Ns), F32), (tm, tn), name, carry)


def _mm_tn_half(at, g, nsh, other, name, carry=None):
    M, Tk = at.shape
    Ns = g.shape[1] // nsh
    tm, tn, tk = _tile(M // 2, 1408), _tile(Ns, 1408), _tile(Tk, 2048)
    mh, tps, nk = (M // 2) // tm, Ns // tn, Tk // tk

    def half():
        c = lax.axis_index("c")
        return 1 - c if other else c

    return _mm_call(
        _dot, (at, g), (mh, nsh * tps, nk),
        [pl.BlockSpec((tm, tk), lambda m, n, k: (half() * mh + m, k)),
         pl.BlockSpec((tk, tn), lambda m, n, k: (k, n))],
        pl.BlockSpec((None, tm, tn), lambda m, n, k: (n // tps, m, n % tps)),
        jax.ShapeDtypeStruct((nsh, M // 2, Ns), F32), (tm, tn), name, carry)


def _ffn_in_fused(h2, wi3, name):
    M, K = h2.shape
    _, _, Ns = wi3.shape
    fh = 2 * Ns
    tm, tn = _tile(M, 1024), _tile(Ns, 512)
    tps = Ns // tn

    def body(h_ref, ba_ref, bu_ref, au_ref, sw_ref, swt_ref):
        h = h_ref[...]
        a, u = _dot(h, ba_ref[...]), _dot(h, bu_ref[...])
        au_ref[0] = a.astype(au_ref.dtype)
        au_ref[1] = u.astype(au_ref.dtype)
        sw = (_silu(a) * u).astype(sw_ref.dtype)
        sw_ref[...] = sw
        swt_ref[...] = sw.T

    return pl.pallas_call(
        body, name=name, grid=(M // tm, fh // tn),
        in_specs=[pl.BlockSpec((tm, K), lambda m, n: (m, 0)),
                  pl.BlockSpec((None, K, tn), lambda m, n: (n // tps, 0, n % tps)),
                  pl.BlockSpec((None, K, tn), lambda m, n: (2 + n // tps, 0, n % tps))],
        out_specs=[pl.BlockSpec((2, tm, tn), lambda m, n: (0, m, n)), pl.BlockSpec((tm, tn), lambda m, n: (m, n)),
                   pl.BlockSpec((tn, tm), lambda m, n: (n, m))],
        out_shape=[jax.ShapeDtypeStruct((2, M, fh), BF16), jax.ShapeDtypeStruct((M, fh), BF16),
                   jax.ShapeDtypeStruct((fh, M), BF16)],
        compiler_params=_params(("parallel", "parallel")),
    )(h2, wi3, wi3)


def _rowwise(fn, nblk, tm, rins, vins, routs, accs, name):
    nr, nv, no, na = len(rins), len(vins), len(routs), len(accs)

    def body(*refs):
        i = pl.program_id(0)
        outs, accv = fn(i, [r[...] for r in refs[:nr]], [r[...] for r in refs[nr:nr + nv]])
        for r, v, spec in zip(refs[nr + nv:nr + nv + no], outs, routs):
            v = v.astype(r.dtype)
            r[...] = v.T if len(spec) == 3 else v
        arefs = refs[nr + nv + no:]
        if na:
            @pl.when(i == 0)
            def _():
                for a in arefs:
                    a[...] = jnp.zeros_like(a)

            for a, v in zip(arefs, accv):
                a[...] += v

    def row_spec(w, cb, rm):
        if rm is None:
            return pl.BlockSpec((tm, w), lambda i: (i, cb))
        return pl.BlockSpec((tm, w), lambda i: (rm(i), cb))

    in_specs = [row_spec(w, cb, rm) for (_, w, cb, rm) in rins]
    in_specs += [pl.BlockSpec(v.shape, lambda i: (0, 0)) for v in vins]
    def out_of(spec):
        w, dt = spec[:2]
        if len(spec) == 3:
            return pl.BlockSpec((w, tm), lambda i: (0, i)), jax.ShapeDtypeStruct((w, nblk * tm), dt)
        return pl.BlockSpec((tm, w), lambda i: (i, 0)), jax.ShapeDtypeStruct((nblk * tm, w), dt)

    out_specs = [out_of(sp)[0] for sp in routs] + [pl.BlockSpec((1, w), lambda i: (0, 0)) for w in accs]
    out_shape = [out_of(sp)[1] for sp in routs] + [jax.ShapeDtypeStruct((1, w), F32) for w in accs]
    res = pl.pallas_call(
        body, name=name, grid=(nblk,), in_specs=in_specs, out_specs=out_specs, out_shape=out_shape,
        compiler_params=_params(("arbitrary",)),
    )(*[r[0] for r in rins], *vins)
    return list(res)


def _colsum(v):
    return jnp.sum(v, axis=0, keepdims=True)


def _rms(v):
    return lax.rsqrt(jnp.mean(v * v, axis=-1, keepdims=True) + EPS)


def _all_gather8(xs):
    m_per, n = xs.shape

    def body(x_ref, out_ref, send_sems, recv_sems, local_sem):
        x, y, c = lax.axis_index("x"), lax.axis_index("y"), lax.axis_index("c")
        me, sibling = (x, y, c), (x, y, 1 - c)
        chips = [(1 - x, y), (x, 1 - y), (1 - x, 1 - y)]

        def rows(px, py, pc):
            return out_ref.at[pl.ds((4 * px + 2 * py + pc) * m_per, m_per), :]

        def copy(k, block, to, src=None):
            return pltpu.make_async_remote_copy(
                src_ref=rows(*block) if src is None else src, dst_ref=rows(*block),
                send_sem=send_sems.at[k], recv_sem=recv_sems.at[k], device_id=to, device_id_type=MESH_ID)

        mine = pltpu.make_async_copy(x_ref, rows(*me), local_sem)
        mine.start()
        first = [copy(0, me, sibling, src=x_ref)]
        first += [copy(1 + j, me, (*chip, c), src=x_ref) for j, chip in enumerate(chips)]
        for cp in first:
            cp.start()
        passed = [copy(4 + j, (*chip, c), sibling) for j, chip in enumerate(chips)]
        for j, chip in enumerate(chips):
            copy(1 + j, (*chip, c), me).wait_recv()
            passed[j].start()
        copy(0, sibling, me).wait_recv()
        for j, chip in enumerate(chips):
            copy(4 + j, (*chip, 1 - c), me).wait_recv()
        for cp in first + passed:
            cp.wait_send()
        mine.wait()

    return pl.pallas_call(
        body, name="all_gather8_%dx%d" % (m_per, n),
        out_shape=jax.ShapeDtypeStruct((8 * m_per, n), xs.dtype),
        in_specs=[pl.BlockSpec(memory_space=pltpu.VMEM)],
        out_specs=pl.BlockSpec(memory_space=pltpu.VMEM),
        scratch_shapes=[pltpu.SemaphoreType.DMA((7,)), pltpu.SemaphoreType.DMA((7,)), pltpu.SemaphoreType.DMA],
    )(xs)


def _mesh_pos():
    x, y, c = lax.axis_index("x"), lax.axis_index("y"), lax.axis_index("c")
    chips = [(1 - x, y), (x, 1 - y), (1 - x, 1 - y)]
    return x, y, c, chips


def _my_shard():
    return 2 * lax.axis_index("x") + lax.axis_index("y")


def _cast_place(w, name):
    r, cw = w.shape
    rh = r // 2
    tm = _tile(rh, max(16, (1 << 20) // (4 * cw)), 16)
    nt = rh // tm

    def body(w_ref, o_ref):
        o_ref[...] = w_ref[...].astype(o_ref.dtype)

    return pl.pallas_call(
        body, name=name, grid=(2, nt),
        in_specs=[pl.BlockSpec((tm, cw), lambda h, i: (h * nt + i, 0))],
        out_specs=pl.BlockSpec((None, None, tm, cw), lambda h, i: (_my_shard(), h, i, 0)),
        out_shape=jax.ShapeDtypeStruct((4, 2, rh, cw), BF16),
        compiler_params=_params(("parallel", "parallel")),
    )(w)


def _exchange_gather(bufs):
    n = len(bufs)

    def copies(out, send_sems, recv_sems, base):
        def copy(i, k, shard, half, to):
            dst = out[i].at[shard, half]
            return pltpu.make_async_remote_copy(
                src_ref=dst, dst_ref=dst, send_sem=send_sems.at[base + 6 * i + k],
                recv_sem=recv_sems.at[base + 6 * i + k], device_id=to, device_id_type=MESH_ID)
        return copy

    def first(copy):
        x, y, c, chips = _mesh_pos()
        return [copy(i, j, 2 * x + y, c, (*chip, c)) for i in range(n) for j, chip in enumerate(chips)]

    def start(cin, out, send_sems, recv_sems, base=0):
        for cp in first(copies(out, send_sems, recv_sems, base)):
            cp.start()

    def passed(copy):
        x, y, c, chips = _mesh_pos()
        return [copy(i, 3 + j, 2 * chip[0] + chip[1], c, (x, y, 1 - c)) for j, chip in enumerate(chips) for i in range(n)]

    def mid(cin, out, send_sems, recv_sems, base=0):
        copy = copies(out, send_sems, recv_sems, base)
        x, y, c, chips = _mesh_pos()
        for j, chip in enumerate(chips):
            for i in range(n):
                copy(i, j, 2 * chip[0] + chip[1], c, (x, y, c)).wait_recv()
        for cp in passed(copy):
            cp.start()

    def finish(cin, out, send_sems, recv_sems, base=0):
        copy = copies(out, send_sems, recv_sems, base)
        x, y, c, chips = _mesh_pos()
        for j, chip in enumerate(chips):
            for i in range(n):
                copy(i, 3 + j, 2 * chip[0] + chip[1], 1 - c, (x, y, c)).wait_recv()
        for cp in first(copy) + passed(copy):
            cp.wait_send()

    return dict(ins=list(bufs), outs=[jax.ShapeDtypeStruct(b.shape, b.dtype) for b in bufs],
                alias={i: i for i in range(n)}, nsem=6 * n, start=start, mid=mid, finish=finish)


def _exchange_join(a, b):
    nai, nao = len(a["ins"]), len(a["outs"])

    def start(cin, cout, send_sems, recv_sems, base=0):
        a["start"](cin[:nai], cout[:nao], send_sems, recv_sems, base)
        b["start"](cin[nai:], cout[nao:], send_sems, recv_sems, base + a["nsem"])

    def mid(cin, cout, send_sems, recv_sems, base=0):
        if "mid" in a:
            a["mid"](cin[:nai], cout[:nao], send_sems, recv_sems, base)
        if "mid" in b:
            b["mid"](cin[nai:], cout[nao:], send_sems, recv_sems, base + a["nsem"])

    def finish(cin, cout, send_sems, recv_sems, base=0):
        a["finish"](cin[:nai], cout[:nao], send_sems, recv_sems, base)
        b["finish"](cin[nai:], cout[nao:], send_sems, recv_sems, base + a["nsem"])

    alias = dict(a["alias"])
    alias.update({nai + i: nao + j for i, j in b["alias"].items()})
    return dict(ins=a["ins"] + b["ins"], outs=a["outs"] + b["outs"], alias=alias, nsem=a["nsem"] + b["nsem"],
                start=start, mid=mid, finish=finish)


def _exchange_call(ex, name):
    nci, nco = len(ex["ins"]), len(ex["outs"])

    def body(*refs):
        cin, cout, sems = refs[:nci], refs[nci:nci + nco], refs[nci + nco:]
        ex["start"](cin, cout, *sems)
        if "mid" in ex:
            ex["mid"](cin, cout, *sems)
        ex["finish"](cin, cout, *sems)

    any_spec = pl.BlockSpec(memory_space=pl.ANY)
    return list(pl.pallas_call(
        body, name=name, out_shape=list(ex["outs"]), in_specs=[any_spec] * nci, out_specs=[any_spec] * nco,
        input_output_aliases=dict(ex["alias"]),
        scratch_shapes=[pltpu.SemaphoreType.DMA((ex["nsem"],)), pltpu.SemaphoreType.DMA((ex["nsem"],))],
    )(*ex["ins"]))


def _exchange_swap_other_half(gs):
    n = len(gs)

    def copies(g, land, send_sems, recv_sems, base):
        x, y, c, _ = _mesh_pos()
        return [pltpu.make_async_remote_copy(
            src_ref=g[i].at[:, 1 - c] if len(gs[i].shape) == 4 else g[i], dst_ref=land[i],
            send_sem=send_sems.at[base + i],
            recv_sem=recv_sems.at[base + i], device_id=(x, y, 1 - c), device_id_type=MESH_ID) for i in range(n)]

    def start(g, land, send_sems, recv_sems, base=0):
        for cp in copies(g, land, send_sems, recv_sems, base):
            cp.start()

    def finish(g, land, send_sems, recv_sems, base=0):
        for cp in copies(g, land, send_sems, recv_sems, base):
            cp.wait()

    return dict(ins=list(gs), outs=[jax.ShapeDtypeStruct((4,) + g.shape[-2:], g.dtype) for g in gs],
                alias={}, nsem=n, start=start, finish=finish)


def _exchange_scatter(ps):
    n = len(ps)

    def copies(p, land, send_sems, recv_sems, base):
        x, y, c, chips = _mesh_pos()
        return [pltpu.make_async_remote_copy(
            src_ref=p[i].at[2 * chip[0] + chip[1]], dst_ref=land[i].at[j],
            send_sem=send_sems.at[base + 3 * i + j], recv_sem=recv_sems.at[base + 3 * i + j],
            device_id=(*chip, c), device_id_type=MESH_ID) for i in range(n) for j, chip in enumerate(chips)]

    def start(p, land, send_sems, recv_sems, base=0):
        for cp in copies(p, land, send_sems, recv_sems, base):
            cp.start()

    def finish(p, land, send_sems, recv_sems, base=0):
        for cp in copies(p, land, send_sems, recv_sems, base):
            cp.wait()

    return dict(ins=list(ps), outs=[jax.ShapeDtypeStruct((3,) + p.shape[1:], p.dtype) for p in ps],
                alias={}, nsem=3 * n, start=start, finish=finish)


def _exchange_swap_result(bufs):
    n = len(bufs)

    def copies(out, send_sems, recv_sems, base, half):
        x, y, c, _ = _mesh_pos()
        h = c if half == "mine" else 1 - c
        return [pltpu.make_async_remote_copy(
            src_ref=out[i].at[h], dst_ref=out[i].at[h], send_sem=send_sems.at[base + i],
            recv_sem=recv_sems.at[base + i], device_id=(x, y, 1 - c), device_id_type=MESH_ID) for i in range(n)]

    def start(cin, out, send_sems, recv_sems, base=0):
        for cp in copies(out, send_sems, recv_sems, base, "mine"):
            cp.start()

    def finish(cin, out, send_sems, recv_sems, base=0):
        for cp in copies(out, send_sems, recv_sems, base, "theirs"):
            cp.wait_recv()
        for cp in copies(out, send_sems, recv_sems, base, "mine"):
            cp.wait_send()

    return dict(ins=list(bufs), outs=[jax.ShapeDtypeStruct(b.shape, b.dtype) for b in bufs],
                alias={i: i for i in range(n)}, nsem=n, start=start, finish=finish)


def _add_own_half(g, land, name):
    rh, cw = g.shape[-2:]
    tm = _tile(rh, max(16, (1 << 20) // (4 * cw)), 16)

    def body(g_ref, l_ref, o_ref):
        o_ref[...] = (g_ref[...] + l_ref[...]).astype(o_ref.dtype)

    mine = (pl.BlockSpec((None, None, tm, cw), lambda s, i: (s, lax.axis_index("c"), i, 0)) if g.ndim == 4
            else pl.BlockSpec((None, tm, cw), lambda s, i: (s, i, 0)))
    return pl.pallas_call(
        body, name=name, grid=(4, rh // tm),
        in_specs=[mine,
                  pl.BlockSpec((None, tm, cw), lambda s, i: (s, i, 0))],
        out_specs=pl.BlockSpec((None, tm, cw), lambda s, i: (s, i, 0)),
        out_shape=jax.ShapeDtypeStruct((4, rh, cw), BF16),
        compiler_params=_params(("parallel", "parallel")),
    )(g, land)


def _sum_pieces(part, land, name):
    _, rh, cw = land.shape
    tm = _tile(rh, max(16, (1 << 20) // (4 * cw)), 16)

    def body(p_ref, l_ref, o_ref):
        v = l_ref[...].astype(F32)
        o_ref[...] = (p_ref[...].astype(F32) + v[0]) + (v[1] + v[2])

    return pl.pallas_call(
        body, name=name, grid=(rh // tm,),
        in_specs=[pl.BlockSpec((None, tm, cw), lambda i: (_my_shard(), i, 0)),
                  pl.BlockSpec((3, tm, cw), lambda i: (0, i, 0))],
        out_specs=pl.BlockSpec((None, tm, cw), lambda i: (lax.axis_index("c"), i, 0)),
        out_shape=jax.ShapeDtypeStruct((2, rh, cw), F32),
        compiler_params=_params(("parallel",)),
    )(part, land)


def _sum8(g, name):
    def body(g_ref, o_ref):
        acc = g_ref[0]
        for k in range(1, 8):
            acc = acc + g_ref[k]
        o_ref[...] = acc

    return pl.pallas_call(body, name=name, out_shape=jax.ShapeDtypeStruct(g.shape[1:], F32))(g)


def _silu(v):
    return v * _sigmoid(v)


def _dsilu(v):
    s = _sigmoid(v)
    return s * (1.0 + v * (1.0 - s))


def _ada_fwd(cin, w, b):
    d, ns = w.shape
    tn = _tile(ns, 512)

    def body(c_ref, w_ref, b_ref, o_ref):
        o_ref[...] = _dot(_silu(c_ref[...]), w_ref[...], HI) + b_ref[...]

    return pl.pallas_call(
        body, name="ada_fwd", grid=(ns // tn,),
        in_specs=[pl.BlockSpec(cin.shape, lambda n: (0, 0)), pl.BlockSpec((d, tn), lambda n: (0, n)),
                  pl.BlockSpec((1, tn), lambda n: (0, n))],
        out_specs=pl.BlockSpec((cin.shape[0], tn), lambda n: (0, n)),
        out_shape=jax.ShapeDtypeStruct((cin.shape[0], ns), F32),
        compiler_params=_params(("parallel",)),
    )(cin, w, b)


def _ada_bwd(cin, w, dm):
    d, ns = w.shape
    tn = _tile(ns, 512)

    def body(c_ref, w_ref, d_ref, dw_ref, dc_ref):
        n = pl.program_id(0)

        @pl.when(n == 0)
        def _():
            dc_ref[...] = jnp.zeros_like(dc_ref)

        dw_ref[...] = _dot_tn(_silu(c_ref[...]), d_ref[...], HI)
        dc_ref[...] += _dot_nt(d_ref[...], w_ref[...], HI)

    return pl.pallas_call(
        body, name="ada_bwd", grid=(ns // tn,),
        in_specs=[pl.BlockSpec(cin.shape, lambda n: (0, 0)), pl.BlockSpec((d, tn), lambda n: (0, n)),
                  pl.BlockSpec((cin.shape[0], tn), lambda n: (0, n))],
        out_specs=[pl.BlockSpec((d, tn), lambda n: (0, n)), pl.BlockSpec(cin.shape, lambda n: (0, 0))],
        out_shape=[jax.ShapeDtypeStruct((d, ns), F32), jax.ShapeDtypeStruct(cin.shape, F32)],
        compiler_params=_params(("arbitrary",)),
    )(cin, w, dm)


def _bias_tables(rpb, rows):
    kh = min(WIN_H, rows)
    fold, onehot, in_win = _bias_selectors(kh)
    t = jnp.einsum("hdc,dsj->hsjc", rpb, jnp.asarray(fold), precision=HI)
    t = jnp.einsum("hsjc,cqk->hsqjk", t, jnp.asarray(onehot), precision=HI)
    t = jnp.where(jnp.asarray(in_win)[None, None, :, None, :], t, MASK_VALUE)
    return t.reshape(rpb.shape[0], kh, GRID_W, kh * GRID_W).astype(F32)


def _bias_selectors(kh):
    col = np.arange(GRID_W)
    col_start = np.clip(col - WIN_W // 2, 0, GRID_W - WIN_W)
    in_win = (col[None, :] >= col_start[:, None]) & (col[None, :] < col_start[:, None] + WIN_W)
    dc_idx = np.clip(col[None, :] - col[:, None], 1 - WIN_W, WIN_W - 1) + WIN_W - 1
    onehot = np.zeros((2 * WIN_W - 1, GRID_W, GRID_W), np.float32)
    qq, kk = np.nonzero(in_win)
    onehot[dc_idx[qq, kk], qq, kk] = 1.0
    fold = np.zeros((2 * WIN_H - 1, kh, kh), np.float32)
    for sh in range(kh):
        for j in range(kh):
            fold[j - sh + WIN_H - 1, sh, j] = 1.0
    return fold, onehot, in_win


def _mm_f32(a, b, name):
    M, K = a.shape
    N = b.shape[1]
    tm = _tile(M, 256, 8)

    def body(a_ref, b_ref, o_ref):
        o_ref[...] = _dot(a_ref[...], b_ref[...], HI)

    return pl.pallas_call(
        body, name=name, grid=(M // tm,),
        in_specs=[pl.BlockSpec((tm, K), lambda i: (i, 0)), pl.BlockSpec((K, N), lambda i: (0, 0))],
        out_specs=pl.BlockSpec((tm, N), lambda i: (i, 0)),
        out_shape=jax.ShapeDtypeStruct((M, N), F32),
        compiler_params=_params(("parallel",)),
    )(a, b)


def _bias_tables_transpose(dbias, rows):
    kh = min(WIN_H, rows)
    nh = dbias.shape[0]
    fold, onehot, _ = _bias_selectors(kh)
    ndc = onehot.shape[0]
    sel = np.zeros((GRID_W * GRID_W, LANES), np.float32)
    sel[:, :ndc] = onehot.reshape(ndc, -1).T
    x = dbias.reshape(nh, kh, GRID_W, kh, GRID_W).transpose(0, 1, 3, 2, 4).reshape(nh * kh * kh, GRID_W * GRID_W)
    z = _mm_f32(x, jnp.asarray(sel), "rpb_fold")[:, :ndc].reshape(nh, kh, kh, ndc)
    return jnp.einsum("dsj,hsjc->hdc", jnp.asarray(fold), z, precision=HI)


def _na_geometry(S):
    rows = S // GRID_W
    kh = min(WIN_H, rows)

    def row_start(r):
        return jnp.clip(r - kh // 2, 0, rows - kh)

    return rows, kh, row_start


def _na_by_head(ref, rr, lane):
    t = ref[rr * GRID_W:(rr + 1) * GRID_W, :]
    zero = jnp.zeros_like(t)
    return jnp.concatenate([jnp.where(lane < NA_HEAD_DIM, t, zero), jnp.where(lane >= NA_HEAD_DIM, t, zero)], axis=0)


def _na_pick_head(t2, lane):
    return jnp.where(lane < NA_HEAD_DIM, t2[:GRID_W], t2[GRID_W:])


def _na_scores(q_ref, k_ref, b_ref, i, nrs, nb, S, L, row_start, lane):
    qh = jnp.concatenate([_na_by_head(q_ref, rr, lane) for rr in range(nrs)], axis=0)
    sc = _dot_nt(qh, k_ref[pl.ds(S, L), :])
    starts, shifts, sb = [], [], []
    for rr in range(nrs):
        r = i * nrs + rr
        rs = row_start(r)
        starts.append(pl.multiple_of(rs * GRID_W, GRID_W))
        shifts.append(r - rs)
        bias = jnp.concatenate([b_ref[0, r - rs], b_ref[1, r - rs]], axis=0)
        sb.append(_dot_nt(qh[rr * 2 * GRID_W:(rr + 1) * 2 * GRID_W], k_ref[pl.ds(starts[-1], nb), :]) + bias)
    return qh, jnp.concatenate(sb, axis=0), sc, starts, shifts


def _na_fwd(qs, kb, vb, bias, S, L):
    T, naw = qs.shape
    rows, kh, row_start = _na_geometry(S)
    nb = kh * GRID_W
    npair = naw // LANES

    nrs = min(NA_ROWS_PER_STEP, rows)
    assert rows % nrs == 0

    def body(q_ref, k_ref, v_ref, b_ref, o_ref, lse_ref):
        i = pl.program_id(1)
        lane = lax.broadcasted_iota(jnp.int32, (GRID_W, LANES), 1)
        _, sb, sc, starts, _ = _na_scores(q_ref, k_ref, b_ref, i, nrs, nb, S, L, row_start, lane)
        m = jnp.maximum(jnp.max(sb, axis=-1, keepdims=True), jnp.max(sc, axis=-1, keepdims=True))
        pb, pc = jnp.exp(sb - m), jnp.exp(sc - m)
        l = jnp.sum(pb, axis=-1, keepdims=True) + jnp.sum(pc, axis=-1, keepdims=True)
        inv = 1.0 / l
        pb16, pc16 = (pb * inv).astype(BF16), (pc * inv).astype(BF16)
        oc = _dot(pc16, v_ref[pl.ds(S, L), :])
        lse = jnp.broadcast_to(m + jnp.log(l), oc.shape)
        for rr in range(nrs):
            two = slice(rr * 2 * GRID_W, (rr + 1) * 2 * GRID_W)
            rsl = slice(rr * GRID_W, (rr + 1) * GRID_W)
            o2 = oc[two] + _dot(pb16[two], v_ref[pl.ds(starts[rr], nb), :])
            o_ref[rsl, :] = _na_pick_head(o2, lane).astype(o_ref.dtype)
            lse_ref[rsl, :] = _na_pick_head(lse[two], lane)

    blk = pl.BlockSpec((nrs * GRID_W, LANES), lambda p, i: (i, p))
    col = pl.BlockSpec((T, LANES), lambda p, i: (0, p))
    return pl.pallas_call(
        body, name="na_fwd", grid=(npair, rows // nrs),
        in_specs=[blk, col, col, pl.BlockSpec((2, kh, GRID_W, nb), lambda p, i: (p, 0, 0, 0))],
        out_specs=[blk, blk],
        out_shape=[jax.ShapeDtypeStruct((S, naw), BF16), jax.ShapeDtypeStruct((S, naw), F32)],
        compiler_params=_params(("parallel", "arbitrary")),
    )(qs, kb, vb, bias)


def _na_bwd(qs, kb, vb, bias, do, o, lse, S, L):
    T, naw = qs.shape
    rows, kh, row_start = _na_geometry(S)
    nb = kh * GRID_W
    npair = naw // LANES

    nrs = min(NA_ROWS_PER_STEP, rows)
    assert rows % nrs == 0

    def body(q_ref, k_ref, v_ref, b_ref, do_ref, o_ref, lse_ref, dq_ref, dk_ref, dv_ref, db_ref):
        i = pl.program_id(1)

        @pl.when(i == 0)
        def _():
            dk_ref[...] = jnp.zeros_like(dk_ref)
            dv_ref[...] = jnp.zeros_like(dv_ref)
            db_ref[...] = jnp.zeros_like(db_ref)

        lane = lax.broadcasted_iota(jnp.int32, (GRID_W, LANES), 1)
        qh, sb, sc, starts, shifts = _na_scores(q_ref, k_ref, b_ref, i, nrs, nb, S, L, row_start, lane)
        doh = jnp.concatenate([_na_by_head(do_ref, rr, lane) for rr in range(nrs)], axis=0)
        o2 = jnp.concatenate([o_ref[rr * GRID_W:(rr + 1) * GRID_W, :] for rr in range(nrs) for _ in range(2)], axis=0)
        lse = jnp.concatenate([lse_ref[rr * GRID_W:(rr + 1) * GRID_W, :][:, hh * NA_HEAD_DIM:hh * NA_HEAD_DIM + 1]
                               for rr in range(nrs) for hh in range(2)], axis=0)
        pb, pc = jnp.exp(sb - lse), jnp.exp(sc - lse)
        delta = jnp.sum(doh.astype(F32) * o2.astype(F32), axis=-1, keepdims=True)
        dpb = jnp.concatenate([_dot_nt(doh[rr * 2 * GRID_W:(rr + 1) * 2 * GRID_W], v_ref[pl.ds(starts[rr], nb), :])
                               for rr in range(nrs)], axis=0)
        dsb = pb * (dpb - delta)
        dsc = pc * (_dot_nt(doh, v_ref[pl.ds(S, L), :]) - delta)
        dsb16, dsc16, pb16, pc16 = dsb.astype(BF16), dsc.astype(BF16), pb.astype(BF16), pc.astype(BF16)
        dqc = _dot(dsc16, k_ref[pl.ds(S, L), :])
        dk_ref[pl.ds(S, L), :] += _dot_tn(dsc16, qh)
        dv_ref[pl.ds(S, L), :] += _dot_tn(pc16, doh)
        for rr in range(nrs):
            two = slice(rr * 2 * GRID_W, (rr + 1) * 2 * GRID_W)
            band = pl.ds(starts[rr], nb)
            dq2 = dqc[two] + _dot(dsb16[two], k_ref[band, :])
            dq_ref[rr * GRID_W:(rr + 1) * GRID_W, :] = _na_pick_head(dq2, lane)
            dk_ref[band, :] += _dot_tn(dsb16[two], qh[two])
            dv_ref[band, :] += _dot_tn(pb16[two], doh[two])
            for hh in range(2):
                db_ref[hh, shifts[rr]] += dsb[(2 * rr + hh) * GRID_W:(2 * rr + hh + 1) * GRID_W]

    blk = pl.BlockSpec((nrs * GRID_W, LANES), lambda p, r: (r, p))
    col = pl.BlockSpec((T, LANES), lambda p, r: (0, p))
    return pl.pallas_call(
        body, name="na_bwd", grid=(npair, rows // nrs),
        in_specs=[blk, col, col, pl.BlockSpec((2, kh, GRID_W, nb), lambda p, r: (p, 0, 0, 0)), blk, blk, blk],
        out_specs=[blk, col, col, pl.BlockSpec((2, kh, GRID_W, nb), lambda p, r: (p, 0, 0, 0))],
        out_shape=[jax.ShapeDtypeStruct((S, naw), F32), jax.ShapeDtypeStruct((T, naw), F32),
                   jax.ShapeDtypeStruct((T, naw), F32), jax.ShapeDtypeStruct(bias.shape, F32)],
        compiler_params=_params(("parallel", "arbitrary")),
    )(qs, kb, vb, bias, do, o, lse)


def _hg_cols(naw, hgf, rev):
    qcol = (3 * naw) // hgf
    fcol = (3 * naw + hgf * (2 if rev else 1)) // hgf
    icol = (3 * naw + 3 * hgf) // hgf
    return qcol, fcol, icol


def _hg_chunk_order(S, L, rev):
    ncl, ncc = S // HG_CHUNK, L // HG_CHUNK
    nc = ncl + ncc

    def chunk_of(i):
        if rev:
            return nc - 1 - i
        return jnp.where(i < ncc, ncl + i, i - ncc)

    return nc, ncl, chunk_of


def _hg_gates(q, z, lb, rev):
    row = lax.broadcasted_iota(jnp.int32, (HG_CHUNK, HG_CHUNK), 0)
    colm = lax.broadcasted_iota(jnp.int32, (HG_CHUNK, HG_CHUNK), 1)
    tri = (colm >= row) if rev else (row >= colm)
    trif = tri.astype(F32)
    sig = _sigmoid(z)
    f = lb + (1.0 - lb) * sig
    lf = jnp.log(f)
    k = 1.0 - f
    cum = _dot(trif, lf, HI)
    mid = cum[HG_CHUNK // 2:HG_CHUNK // 2 + 1, :]
    last = cum[0:1, :] if rev else cum[HG_CHUNK - 1:HG_CHUNK, :]
    eq = jnp.exp(jnp.clip(cum - mid, -EXP_CLAMP, EXP_CLAMP))
    ek = jnp.exp(jnp.clip(mid - cum, -EXP_CLAMP, EXP_CLAMP))
    return tri, trif, sig, f, k, cum, last, eq, ek


def _hg_fwd(u, lbr, S, L, naw, hgf, rev):
    T = S + L
    nh = hgf // HG_DIM
    nc, ncl, chunk_of = _hg_chunk_order(S, L, rev)
    qcol, fcol, icol = _hg_cols(naw, hgf, rev)

    def step(i, q_ref, z_ref, v_ref, lb_ref, o_ref, st_ref, state):
        @pl.when(i == 0)
        def _():
            state[...] = jnp.zeros_like(state)

        q, z, v = q_ref[...], z_ref[...], v_ref[...]
        tri, _, _, _, k, cum, last, eq, ek = _hg_gates(q, z, lb_ref[...], rev)
        qe, ke = (q * eq).astype(BF16), (k * ek).astype(BF16)
        qd, kd = (q * jnp.exp(cum)).astype(BF16), (k * jnp.exp(last - cum)).astype(BF16)
        v16, el = v.astype(BF16), jnp.exp(last)
        for h in range(nh):
            sl = slice(h * HG_DIM, (h + 1) * HG_DIM)
            a = jnp.where(tri, _dot_nt(qe[:, sl], ke[:, sl]), 0.0)
            s0 = state[h]
            st_ref[h] = s0
            o_ref[:, sl] = _dot(a.astype(BF16), v16[:, sl]) + _dot_nt(qd[:, sl], s0.astype(BF16))
            state[h] = s0 * el[:, sl] + _dot_tn(v16[:, sl], kd[:, sl])

    def blk(cb):
        return pl.BlockSpec((HG_CHUNK, hgf), lambda i: (chunk_of(i), cb))

    return dict(
        step=step, nc=nc, operands=[u, u, u, lbr],
        in_specs=[blk(qcol), blk(fcol), blk(icol), pl.BlockSpec((1, hgf), lambda i: (0, 0))],
        out_specs=[pl.BlockSpec((HG_CHUNK, hgf), lambda i: (chunk_of(i), 0)),
                   pl.BlockSpec((None, nh, HG_DIM, HG_DIM), lambda i: (chunk_of(i), 0, 0, 0))],
        out_shape=[jax.ShapeDtypeStruct((T, hgf), F32), jax.ShapeDtypeStruct((nc, nh, HG_DIM, HG_DIM), F32)],
        scratch=[pltpu.VMEM((nh, HG_DIM, HG_DIM), F32)])


def _hg_both(parts, name):
    nin = [len(p["in_specs"]) for p in parts]
    nout = [len(p["out_specs"]) for p in parts]
    nscr = [len(p["scratch"]) for p in parts]

    def body(*refs):
        i = pl.program_id(0)
        ins, outs, scr = refs[:sum(nin)], refs[sum(nin):sum(nin) + sum(nout)], refs[sum(nin) + sum(nout):]
        for d, p in enumerate(parts):
            p["step"](i, *ins[sum(nin[:d]):sum(nin[:d + 1])], *outs[sum(nout[:d]):sum(nout[:d + 1])],
                      *scr[sum(nscr[:d]):sum(nscr[:d + 1])])

    res = pl.pallas_call(
        body, name=name, grid=(parts[0]["nc"],),
        in_specs=[sp for p in parts for sp in p["in_specs"]],
        out_specs=[sp for p in parts for sp in p["out_specs"]],
        out_shape=[sh for p in parts for sh in p["out_shape"]],
        scratch_shapes=[sc for p in parts for sc in p["scratch"]],
        compiler_params=_params(("arbitrary",)),
    )(*[op for p in parts for op in p["operands"]])
    return [list(res[sum(nout[:d]):sum(nout[:d + 1])]) for d in range(len(parts))]


def _hg_bwd(u, lbr, st, do, S, L, naw, hgf, rev):
    T = S + L
    nh = hgf // HG_DIM
    nc, ncl, chunk_fwd = _hg_chunk_order(S, L, rev)
    qcol, fcol, icol = _hg_cols(naw, hgf, rev)

    def chunk_of(j):
        return chunk_fwd(nc - 1 - j)

    def step(j, q_ref, z_ref, v_ref, lb_ref, st_ref, do_ref, dq_ref, dz_ref, dv_ref, dlb_ref, dstate,
             dqe_s, dke_s, dqd_s, dkd_s, dl_s):
        @pl.when(j == 0)
        def _():
            dstate[...] = jnp.zeros_like(dstate)
            dlb_ref[...] = jnp.zeros_like(dlb_ref)

        q, z, v = q_ref[...], z_ref[...], v_ref[...]
        lb = lb_ref[...]
        tri, trif, sig, f, k, cum, last, eq, ek = _hg_gates(q, z, lb, rev)
        ec, el, ekd = jnp.exp(cum), jnp.exp(last), jnp.exp(last - cum)
        qe, ke, qd, kd = q * eq, k * ek, q * ec, k * ekd
        qd16, kd16 = qd.astype(BF16), kd.astype(BF16)
        dout = jnp.where(chunk_of(j) < ncl, do_ref[...], 0.0)
        qe2, ke2, v2, dout2 = _split2(qe), _split2(ke), _split2(v), _split2(dout)
        for h in range(nh):
            sl = slice(h * HG_DIM, (h + 1) * HG_DIM)
            qeh, keh, vh, douth = [(t[0][:, sl], t[1][:, sl]) for t in (qe2, ke2, v2, dout2)]
            a = jnp.where(tri, _dot_nt(qeh[0], keh[0]), 0.0).astype(BF16)
            s0 = st_ref[h]
            ds1 = dstate[h]
            s02, ds12 = _split2(s0), _split2(ds1)
            dv_ref[:, sl] = (_dot_tn(a, douth[0]) + _dot_nt(kd16[:, sl], ds12[0])).astype(dv_ref.dtype)
            da2 = _split2(jnp.where(tri, _dot_x3(_dot_nt, douth, vh), 0.0))
            dqe_s[:, sl] = _dot_x3(_dot, da2, keh)
            dke_s[:, sl] = _dot_x3(_dot_tn, da2, qeh)
            dqd_s[:, sl] = _dot_x3(_dot, douth, s02)
            dkd_s[:, sl] = _dot_x3(_dot, vh, ds12)
            dl_s[:, sl] = _colsum(ds1 * s0)
            dstate[h] = _dot_tn(douth[0], qd16[:, sl]) + ds1 * el[:, sl]
        dqe, dke, dqd, dkd = dqe_s[...], dke_s[...], dqd_s[...], dkd_s[...]
        dq_ref[...] = (dqe * eq + dqd * ec).astype(dq_ref.dtype)
        dk = dke * ek + dkd * ekd
        dcum = dqe * qe - dke * ke + dqd * qd - dkd * kd
        dlast = _colsum(dkd * kd) + el * dl_s[...]
        dlf = _dot_tn(trif, dcum, HI) + dlast
        df = dlf / f - dk
        dz_ref[...] = (df * (1.0 - lb) * sig * (1.0 - sig)).astype(dz_ref.dtype)
        dlb_ref[...] += _colsum(df * (1.0 - sig))

    def blk(cb):
        return pl.BlockSpec((HG_CHUNK, hgf), lambda j: (chunk_of(j), cb))

    oblk = pl.BlockSpec((HG_CHUNK, hgf), lambda j: (chunk_of(j), 0))
    wide = pltpu.VMEM((HG_CHUNK, hgf), F32)
    return dict(
        step=step, nc=nc, operands=[u, u, u, lbr, st, do],
        in_specs=[blk(qcol), blk(fcol), blk(icol), pl.BlockSpec((1, hgf), lambda j: (0, 0)),
                  pl.BlockSpec((None, nh, HG_DIM, HG_DIM), lambda j: (chunk_of(j), 0, 0, 0)),
                  pl.BlockSpec((HG_CHUNK, hgf), lambda j: (jnp.minimum(chunk_of(j), ncl - 1), 0))],
        out_specs=[oblk, oblk, oblk, pl.BlockSpec((1, hgf), lambda j: (0, 0))],
        out_shape=[jax.ShapeDtypeStruct((T, hgf), BF16)] * 3 + [jax.ShapeDtypeStruct((1, hgf), F32)],
        scratch=[pltpu.VMEM((nh, HG_DIM, HG_DIM), F32), wide, wide, wide, wide, pltpu.VMEM((1, hgf), F32)])


def _adamw(w, g, m, v, name):
    shape = w.shape
    if w.ndim != 2 or shape[0] % 8 or shape[1] % LANES:
        w, g, m, v = [a.reshape(1, -1) for a in (w, g, m, v)]
    r, cw = w.shape
    tm = _tile(r, max(8, (1 << 19) // cw), 8) if r % 8 == 0 else r
    c1 = 1.0 / (1.0 - ADAM_B1 ** ADAM_STEP)
    c2 = 1.0 / (1.0 - ADAM_B2 ** ADAM_STEP)

    def body(w_ref, g_ref, m_ref, v_ref, d_ref, nm_ref, nv_ref):
        gg = g_ref[...]
        nm = ADAM_B1 * m_ref[...] + (1.0 - ADAM_B1) * gg
        nv = ADAM_B2 * v_ref[...] + (1.0 - ADAM_B2) * (gg * gg)
        d_ref[...] = -ADAM_LR * ((nm * c1) / (jnp.sqrt(nv * c2) + ADAM_EPS) + ADAM_WD * w_ref[...])
        nm_ref[...] = nm
        nv_ref[...] = nv

    spec = pl.BlockSpec((tm, cw), lambda i: (i, 0))
    outs = pl.pallas_call(
        body, name=name, grid=(r // tm,), in_specs=[spec] * 4, out_specs=[spec] * 3,
        out_shape=[jax.ShapeDtypeStruct((r, cw), F32)] * 3,
        compiler_params=_params(("parallel",)),
    )(w, g, m, v)
    return [o.reshape(shape) for o in outs]


def kernel(x, c, ctx, c_ctx, w_ada, b_ada, norm1_g, w_in, na_rpb, hg_lb_logits, hg_norm_g, w_pa, w_pb, w_out, norm2_g, w_ffn_in, w_ffn_out, final_g, loss_target, m_c_ctx, m_w_ada, m_b_ada, m_norm1_g, m_w_in, m_na_rpb, m_hg_lb_logits, m_hg_norm_g, m_w_pa, m_w_pb, m_w_out, m_norm2_g, m_w_ffn_in, m_w_ffn_out, m_final_g, v_c_ctx, v_w_ada, v_b_ada, v_norm1_g, v_w_in, v_na_rpb, v_hg_lb_logits, v_hg_norm_g, v_w_pa, v_w_pb, v_w_out, v_norm2_g, v_w_ffn_in, v_w_ffn_out, v_final_g):
    xi, yi, ci = lax.axis_index("x"), lax.axis_index("y"), lax.axis_index("c")
    sidx = 2 * xi + yi
    eidx = 4 * xi + 2 * yi + ci

    S, D = x.shape[1], x.shape[2]
    L = ctx.shape[1]
    T = S + L
    naw = NA_HEADS * NA_HEAD_DIM
    hgf = HG_HEADS * HG_DIM
    inw = 3 * naw + 5 * hgf + 2 * D
    fh = w_ffn_out.shape[1] * 4
    ads = w_ada.shape[2]
    fs = hg_lb_logits.shape[2]
    rows = S // GRID_W
    tr = _tile(L, 256)
    nlat, nall = S // tr, T // tr
    assert naw == hgf and D % naw == 0 and S % tr == 0 and 2 * hgf <= D

    pack0 = jnp.concatenate([c, jnp.pad(hg_lb_logits.reshape(1, -1), ((0, 0), (0, D - 4 * fs))),
                             jnp.zeros((6, D), F32)], axis=0)
    g0 = _all_gather8(pack0).reshape(8, 8, D)
    cs = g0[:, 0]
    lbl = g0[::2, 1, :4 * fs].reshape(4, 2, 2, fs).transpose(1, 2, 0, 3).reshape(2, 2, 4 * fs)
    p_lb = jax.nn.softmax(lbl, axis=0)
    lb = p_lb[0]
    lbb = [lb[d].reshape(1, hgf) for d in range(2)]

    cin = jnp.concatenate([cs, c_ctx[None], jnp.zeros((7, D), F32)], axis=0)
    b_sh = lax.dynamic_slice(b_ada, (0, sidx * ads), (1, ads))
    modp = _ada_fwd(cin, w_ada[0], b_sh)
    modfull = _all_gather8(modp).reshape(8, 16, ads)[::2].transpose(1, 0, 2).reshape(16, 4 * ads)
    mod_e = jnp.pad(lax.dynamic_index_in_dim(modfull, eidx, 0, keepdims=False).reshape(N_MOD, D), ((0, 2), (0, 0)))
    mod_c = jnp.pad(modfull[8].reshape(N_MOD, D), ((0, 2), (0, 0)))

    names = ["w_in", "w_pa", "w_pb", "w_out", "w_ffn_in", "w_ffn_out"]
    placed = [_cast_place(w[0], "cast_" + nm)
              for w, nm in zip((w_in, w_pa, w_pb, w_out, w_ffn_in, w_ffn_out), names)]
    def shards(g):
        return g.reshape(4, 2 * g.shape[2], g.shape[3])

    win3 = shards(_exchange_call(_exchange_gather(placed[:1]), "gather_w_in")[0])

    x2d, ctx2d = x[0], ctx[0]

    def f_ln1(i, rv, vv):
        xl, xc = rv
        g, me, mc = vv
        isc = i >= nlat
        xt = jnp.where(isc, xc, xl)
        sh = jnp.where(isc, mc[0:1], me[0:1])
        sc = jnp.where(isc, mc[1:2], me[1:2])
        h = xt * _rms(xt) * g * (1.0 + sc) + sh
        return [h, h], []

    hb, hbt = _rowwise(f_ln1, nall, tr, [(x2d, D, 0, lambda i: jnp.minimum(i, nlat - 1)),
                                        (ctx2d, D, 0, lambda i: jnp.maximum(i - nlat, 0))],
                       [norm1_g, mod_e, mod_c], [(D, BF16), (D, BF16, "T")], [], "ln1")
    u, gathered = _mm_nn(hb, win3, F32, "mm_in", carry=_exchange_gather(placed[1:]))
    wpa3, wpb3, wout3, wi3, wfo3 = [shards(g) for g in gathered]
    wout1 = wout3.reshape(1, D, D)
    wfo1 = wfo3.reshape(1, fh, D)

    scale = NA_HEAD_DIM ** -0.5

    def f_qkv(i, rv, vv):
        q, k, v = rv
        return [q * scale, k, v], []

    qs, kb, vb = _rowwise(f_qkv, nall, tr, [(u, naw, 0, None), (u, naw, 1, None), (u, naw, 2, None)], [],
                          [(naw, BF16)] * 3, [], "qkv_cast")
    bias = _bias_tables(na_rpb[0], rows)
    o_na, lse = _na_fwd(qs, kb, vb, bias, S, L)

    (o_f, st_f), (o_b, st_b) = _hg_both(
        [_hg_fwd(u, lbb[0], S, L, naw, hgf, False), _hg_fwd(u, lbb[1], S, L, naw, hgf, True)], "hg_fwd")

    hgn = jnp.tile(hg_norm_g, (1, HG_HEADS))
    hog_cb = (3 * naw + 4 * hgf) // hgf
    ga_cb = (3 * naw + 5 * hgf) // D
    gb_cb = ga_cb + 1

    def heads_rms(o):
        return jnp.concatenate([jnp.broadcast_to(_rms(o[:, h * HG_DIM:(h + 1) * HG_DIM]), (o.shape[0], HG_DIM))
                                for h in range(HG_HEADS)], axis=1)

    def f_readout(i, rv, vv):
        of, ob_, hog = rv
        g, = vv
        o = of + ob_
        return [o * heads_rms(o) * g * _silu(hog)], []

    ob, = _rowwise(f_readout, nlat, tr, [(o_f, hgf, 0, None), (o_b, hgf, 0, None), (u, hgf, hog_cb, None)],
                   [hgn], [(hgf, BF16)], [], "hg_readout")

    ya = _mm_nn(o_na, wpa3, BF16, "mm_pa")
    yb = _mm_nn(ob, wpb3, BF16, "mm_pb")

    def f_merge(i, rv, vv):
        ya_, yb_, ga, gb = rv
        return [_sigmoid(ga) * ya_ + _sigmoid(gb) * yb_], []

    yv, = _rowwise(f_merge, nlat, tr, [(ya, D, 0, None), (yb, D, 0, None), (u, D, ga_cb, None), (u, D, gb_cb, None)],
                   [], [(D, BF16)], [], "merge")
    z = _mm_nn(yv, wout1, F32, "mm_out")

    def f_res1(i, rv, vv):
        xt, zt = rv
        g, me = vv
        x1 = xt + me[2:3] * zt
        h = x1 * _rms(x1) * g * (1.0 + me[4:5]) + me[3:4]
        return [x1, h, h], []

    x1, h2, h2t = _rowwise(f_res1, nlat, tr, [(x2d, D, 0, None), (z, D, 0, None)], [norm2_g, mod_e],
                           [(D, F32), (D, BF16), (D, BF16, "T")], [], "res1_ln2")
    au3, sw, swt = _ffn_in_fused(h2, wi3, "mm_ffn_in")
    ff =_mm_nn(sw, wfo1, F32, "mm_ffn_out")

    fg = final_g.reshape(1, D)

    def f_final(i, rv, vv):
        x1t, ft, tg = rv
        g, me = vv
        x2 = x1t + me[5:6] * ft
        r3 = _rms(x2)
        xn = x2 * r3
        err = xn * g - tg
        dyy = err * (1.0 / D)
        dxn = dyy * g
        dx2 = r3 * (dxn - xn * jnp.mean(dxn * xn, axis=-1, keepdims=True))
        return [dx2, dx2 * me[5:6]], [_colsum(err * err), _colsum(dyy * xn), _colsum(dx2 * ft)]

    dx2, dfb, loss_cols, dfg, dg2 = _rowwise(
        f_final, nlat, tr, [(x1, D, 0, None), (ff, D, 0, None), (loss_target[0], D, 0, None)], [fg, mod_e],
        [(D, F32), (D, BF16)], [D, D, D], "final_loss")

    def dswiglu_epi(tm, tn):
        blk = pl.BlockSpec((2, tm, tn), lambda m, n, k: (0, m, n))

        def fn(d, ins):
            a, uu = ins[0][0].astype(F32), ins[0][1].astype(F32)
            return [(d * uu * _dsilu(a), d * _silu(a))]

        return dict(ins=[(au3, blk)], outs=[(jax.ShapeDtypeStruct((2, S, fh), BF16), blk)], fn=fn)

    dau3, = _mm_nt(dfb, wfo1, BF16, "mm_d_sw", epi=dswiglu_epi, tn_target=512)
    g_wfo = _mm_tn(swt, dfb, 1, "mm_dw_ffn_out", a_is_t=True).reshape(4, 2, fh // 8, D)
    dh2 = _mm_nt(dau3, wi3, F32, "mm_d_h2")
    g_wi = _mm_tn(h2t, dau3, 4, "mm_dw_ffn_in", a_is_t=True)

    def f_ln2_bwd(i, rv, vv):
        dh, x1t, dx2t, zt = rv
        g, me = vv
        r2 = _rms(x1t)
        xn = x1t * r2
        dxn = dh * g * (1.0 + me[4:5])
        dx1 = dx2t + r2 * (dxn - xn * jnp.mean(dxn * xn, axis=-1, keepdims=True))
        return ([dx1, dx1 * me[2:3]],
                [_colsum(dh), _colsum(dh * xn * g), _colsum(dh * xn * (1.0 + me[4:5])), _colsum(dx1 * zt)])

    dx1, dzb, dsh2, dsc2, dn2g, dg1 = _rowwise(
        f_ln2_bwd, nlat, tr, [(dh2, D, 0, None), (x1, D, 0, None), (dx2, D, 0, None), (z, D, 0, None)],
        [norm2_g, mod_e], [(D, F32), (D, BF16)], [D, D, D, D], "ln2_bwd")

    g_wout = _mm_tn(yv, dzb, 1, "mm_dw_out").reshape(4, 2, D // 8, D)

    def dmerge_epi(tm, tn):
        blk = pl.BlockSpec((tm, tn), lambda m, n, k: (m, n))

        def gate(cb):
            return pl.BlockSpec((tm, tn), lambda m, n, k: (m, cb * (D // tn) + n))

        def fn(d, ins):
            ya_, yb_, ga, gb = ins
            sa, sb_ = _sigmoid(ga), _sigmoid(gb)
            return [d * sa, d * sb_, d * ya_ * sa * (1.0 - sa), d * yb_ * sb_ * (1.0 - sb_)]

        return dict(ins=[(ya, blk), (yb, blk), (u, gate(ga_cb)), (u, gate(gb_cb))],
                    outs=[(jax.ShapeDtypeStruct((S, D), BF16), blk)] * 4, fn=fn)

    dya, dyb, dga, dgb = _mm_nt(dzb, wout1, BF16, "mm_d_y", epi=dmerge_epi, tn_target=512)
    d_ona = _mm_nt(dya, wpa3, BF16, "mm_d_ona")
    d_ob = _mm_nt(dyb, wpb3, F32, "mm_d_ob")
    g_wpa = _mm_tn(o_na, dya, 4, "mm_dw_pa")
    g_wpb = _mm_tn(ob, dyb, 4, "mm_dw_pb")

    def f_dreadout(i, rv, vv):
        d, of, ob_, hog = rv
        g, = vv
        o = of + ob_
        on = o * heads_rms(o)
        t = d * _silu(hog) * g
        mt = jnp.concatenate([jnp.broadcast_to(jnp.mean((t * on)[:, h * HG_DIM:(h + 1) * HG_DIM], axis=-1,
                                                        keepdims=True), (o.shape[0], HG_DIM))
                              for h in range(HG_HEADS)], axis=1)
        do_ = heads_rms(o) * (t - on * mt)
        return [do_, d * on * g * _dsilu(hog)], [_colsum(d * _silu(hog) * on)]

    do_hg, dhog, dhgn = _rowwise(
        f_dreadout, nlat, tr, [(d_ob, hgf, 0, None), (o_f, hgf, 0, None), (o_b, hgf, 0, None),
                               (u, hgf, hog_cb, None)], [hgn], [(hgf, F32), (hgf, BF16)], [hgf], "hg_readout_bwd")

    (dq_f, dz_f, dv_f, dlb_f), (dq_b, dz_b, dv_b, dlb_b) = _hg_both(
        [_hg_bwd(u, lbb[0], st_f, do_hg, S, L, naw, hgf, False), _hg_bwd(u, lbb[1], st_b, do_hg, S, L, naw, hgf, True)],
        "hg_bwd")
    dq_na, dk_na, dv_na, dbias = _na_bwd(qs, kb, vb, bias, d_ona, o_na, lse, S, L)

    ta = _tile(L, 128)
    nla, naa = S // ta, T // ta
    lat = lambda i: jnp.minimum(i, nla - 1)

    def f_assemble(i, rv, vv):
        dqn, dk, dv, dqf, dqb, dzf, dzb_, dvf, dvb, dho, dga_, dgb_ = rv
        keep = (i < nla).astype(F32)
        f32 = lambda t: t.astype(F32)
        return [jnp.concatenate([dqn * (scale * keep), dk, dv, f32(dqf) + f32(dqb), f32(dzf), f32(dzb_),
                                 f32(dvf) + f32(dvb),
                                 dho.astype(F32) * keep, dga_.astype(F32) * keep, dgb_.astype(F32) * keep],
                                axis=1)], []

    du, = _rowwise(
        f_assemble, naa, ta,
        [(dq_na, naw, 0, lat), (dk_na, naw, 0, None), (dv_na, naw, 0, None), (dq_f, hgf, 0, None),
         (dq_b, hgf, 0, None), (dz_f, hgf, 0, None), (dz_b, hgf, 0, None), (dv_f, hgf, 0, None),
         (dv_b, hgf, 0, None), (dhog, hgf, 0, lat), (dga, D, 0, lat), (dgb, D, 0, lat)],
        [], [(inw, BF16)], [], "assemble_du")

    def add_half(g, land, nm):
        return _add_own_half(g, land, "rs_add_" + nm)

    early = [g_wpa, g_wpb, g_wout, g_wi, g_wfo]
    g_win_other, lands = _mm_tn_half(hbt, du, 4, True, "mm_dw_in_other", carry=_exchange_swap_other_half(early))
    parts = [add_half(g, l, nm) for g, l, nm in zip(early, lands, names[1:])]
    g_win_mine, landed = _mm_tn_half(hbt, du, 4, False, "mm_dw_in_mine", carry=_exchange_join(
        _exchange_scatter(parts[3:4]), _exchange_swap_other_half([g_win_other])))
    piece_wi = landed[0]
    parts = [add_half(g_win_mine, landed[1], names[0])] + parts
    dh, landed = _mm_nt(du, win3, F32, "mm_d_h", carry=_exchange_scatter(parts[:4] + parts[5:]))
    pieces = landed[:4] + [piece_wi] + landed[4:]
    halves = [_sum_pieces(p, l, "rs_sum_" + nm) for p, l, nm in zip(parts, pieces, names)]
    g_win, g_wpa, g_wpb, g_wout, g_wi, g_wfo = [
        f.reshape(2 * f.shape[1], f.shape[2])
        for f in _exchange_call(_exchange_swap_result(halves), "rs_swap_result_half")]

    def f_ln1_bwd(i, rv, vv):
        dht, xt, dx1t = rv
        g, me = vv
        r1 = _rms(xt)
        xn = xt * r1
        dxn = dht * g * (1.0 + me[1:2])
        dx = dx1t + r1 * (dxn - xn * jnp.mean(dxn * xn, axis=-1, keepdims=True))
        return [dx], [_colsum(dht), _colsum(dht * xn * g), _colsum(dht * xn * (1.0 + me[1:2]))]

    grad_x, dsh1, dsc1, dn1g_l = _rowwise(
        f_ln1_bwd, nlat, tr, [(dh, D, 0, None), (x2d, D, 0, None), (dx1, D, 0, None)], [norm1_g, mod_e],
        [(D, F32)], [D, D, D], "ln1_bwd")

    def f_ln1_bwd_ctx(i, rv, vv):
        dht, xt = rv
        g, mc = vv
        xn = xt * _rms(xt)
        return [], [_colsum(dht), _colsum(dht * xn * g), _colsum(dht * xn * (1.0 + mc[1:2]))]

    ctx_rows = lambda i: i + nlat
    dsh1c, dsc1c, dn1g_c = _rowwise(
        f_ln1_bwd_ctx, nall - nlat, tr, [(dh, D, 0, ctx_rows), (ctx2d, D, 0, None)], [norm1_g, mod_c],
        [], [D, D, D], "ln1_bwd_ctx")

    drpb = _bias_tables_transpose(dbias, rows).reshape(1, -1)
    nrp = -(-drpb.shape[1] // D)
    drpb_rows = jnp.pad(drpb, ((0, 0), (0, nrp * D - drpb.shape[1]))).reshape(nrp, D)
    dlb = jnp.concatenate([dlb_f, dlb_b], axis=1)
    dhg = jnp.sum(dhgn.reshape(HG_HEADS, HG_DIM), axis=0, keepdims=True)

    def wide(v):
        return jnp.pad(v, ((0, 0), (0, D - v.shape[1])))

    pack_rows = [loss_cols, dfg, dn2g, dn1g_l + dn1g_c, dsh1, dsc1, dg1, dsh2, dsc2, dg2, dsh1c, dsc1c,
                 wide(dhg), wide(dlb), drpb_rows]
    pack = jnp.concatenate(pack_rows, axis=0)
    npk = -(-pack.shape[0] // 8) * 8
    pack = jnp.pad(pack, ((0, npk - pack.shape[0]), (0, 0)))
    gp = _all_gather8(pack).reshape(8, npk, D)
    tot = _sum8(gp, "sum_small_grads")

    loss = (0.5 / D) * jnp.sum(tot[0])
    grad_final_g = tot[1]
    grad_norm2_g = tot[2:3]
    grad_norm1_g = tot[3:4]
    grad_hg_norm_g = tot[12:13, :HG_DIM]
    dlb_tot = tot[13, :2 * hgf].reshape(2, hgf)
    grad_na_rpb = tot[14:14 + nrp].reshape(-1)[:drpb.shape[1]].reshape(na_rpb.shape)
    dlog = jnp.stack([dlb_tot * p_lb[0] * (1.0 - p_lb[0]), -dlb_tot * p_lb[0] * p_lb[1]], axis=0)
    grad_hg_lb = lax.dynamic_slice(dlog, (0, 0, sidx * fs), (2, 2, fs))

    dmod_all = gp[:, 4:10].reshape(8, N_MOD * D)
    dmod_ctx = jnp.concatenate([tot[10], tot[11], jnp.zeros((4 * D,), F32)])[None]
    dm16 = jnp.concatenate([dmod_all, dmod_ctx, jnp.zeros((7, N_MOD * D), F32)], axis=0)
    grad_b_ada = jnp.sum(dm16, axis=0, keepdims=True)
    dm_sh = lax.dynamic_slice(dm16, (0, sidx * ads), (16, ads))
    g_wada, dcin = _ada_bwd(cin, w_ada[0], dm_sh)
    gc = _all_gather8(dcin[8:16]).reshape(8, 8, D)
    grad_c_ctx = (gc[0, 0] + gc[2, 0] + gc[4, 0] + gc[6, 0]) * _dsilu(c_ctx)

    grads = {
        "c_ctx": grad_c_ctx, "w_ada": g_wada[None], "b_ada": grad_b_ada, "norm1_g": grad_norm1_g,
        "w_in": g_win[None], "na_rpb": grad_na_rpb, "hg_lb_logits": grad_hg_lb, "hg_norm_g": grad_hg_norm_g,
        "w_pa": g_wpa[None], "w_pb": g_wpb[None], "w_out": g_wout[None], "norm2_g": grad_norm2_g,
        "w_ffn_in": g_wi[None], "w_ffn_out": g_wfo[None], "final_g": grad_final_g,
    }
    weights = {
        "c_ctx": (c_ctx, m_c_ctx, v_c_ctx), "w_ada": (w_ada, m_w_ada, v_w_ada), "b_ada": (b_ada, m_b_ada, v_b_ada),
        "norm1_g": (norm1_g, m_norm1_g, v_norm1_g), "w_in": (w_in, m_w_in, v_w_in),
        "na_rpb": (na_rpb, m_na_rpb, v_na_rpb), "hg_lb_logits": (hg_lb_logits, m_hg_lb_logits, v_hg_lb_logits),
        "hg_norm_g": (hg_norm_g, m_hg_norm_g, v_hg_norm_g), "w_pa": (w_pa, m_w_pa, v_w_pa),
        "w_pb": (w_pb, m_w_pb, v_w_pb), "w_out": (w_out, m_w_out, v_w_out),
        "norm2_g": (norm2_g, m_norm2_g, v_norm2_g), "w_ffn_in": (w_ffn_in, m_w_ffn_in, v_w_ffn_in),
        "w_ffn_out": (w_ffn_out, m_w_ffn_out, v_w_ffn_out), "final_g": (final_g, m_final_g, v_final_g),
    }
    order = list(weights)
    deltas, new_ms, new_vs = [], [], []
    for nm in order:
        w, m, v = weights[nm]
        g = grads[nm].reshape(w.shape)
        grads[nm] = g
        if w.ndim == 3 and w.shape[0] == 1:
            d_, m_, v_ = _adamw(w[0], g[0], m[0], v[0], "adamw_" + nm)
            d_, m_, v_ = d_[None], m_[None], v_[None]
        else:
            d_, m_, v_ = _adamw(w, g, m, v, "adamw_" + nm)
        deltas.append(d_)
        new_ms.append(m_)
        new_vs.append(v_)

    return (loss, grad_x[None], *[grads[nm] for nm in order], *deltas, *new_ms, *new_vs)
```

```python
import numpy as np

import jax
import jax.numpy as jnp
from jax import lax
from jax.experimental import pallas as pl
from jax.experimental.pallas import tpu as pltpu

F32 = jnp.float32
BF16 = jnp.bfloat16

GRID_W = 64
WIN_H = 8
WIN_W = 16
NA_HEADS = 16
NA_HEAD_DIM = 64
HG_HEADS = 8
HG_DIM = 128
HG_CHUNK = 64
N_MOD = 6
EPS = 1e-6
ADAM_LR = 0.001
ADAM_B1 = 0.9
ADAM_B2 = 0.999
ADAM_EPS = 1e-08
ADAM_WD = 0.01
ADAM_STEP = 10

LANES = 128
NA_ROWS_PER_STEP = 16
VMEM_LIMIT = 56 * 1024 * 1024
MASK_VALUE = -1e30
EXP_CLAMP = 80.0
MESH_ID = pl.DeviceIdType.MESH
HI = lax.Precision.HIGHEST


def _tile(dim, target, mult=LANES):
    best = None
    t = mult
    while t <= min(dim, target):
        if dim % t == 0:
            best = t
        t += mult
    assert best is not None, (dim, target, mult)
    return best


def _params(sem):
    return pltpu.CompilerParams(dimension_semantics=sem, vmem_limit_bytes=VMEM_LIMIT)


def _dot(a, b, precision=None):
    return jnp.dot(a, b, preferred_element_type=F32, precision=precision)


def _dot_nt(a, b, precision=None):
    return lax.dot_general(a, b, (((1,), (1,)), ((), ())), preferred_element_type=F32, precision=precision)


def _dot_tn(a, b, precision=None):
    return lax.dot_general(a, b, (((0,), (0,)), ((), ())), preferred_element_type=F32, precision=precision)


def _split2(v):
    hi = v.astype(BF16)
    return hi, (v - hi.astype(F32)).astype(BF16)


def _dot_x3(dot, a2, b2):
    return dot(a2[0], b2[0]) + (dot(a2[0], b2[1]) + dot(a2[1], b2[0]))


def _sigmoid(v):
    return 1.0 / (1.0 + jnp.exp(-v))


def _mm_call(dot, operands, grid, in_specs, out_spec, out_shape, acc_shape, name, carry=None, epi=None):
    nk = grid[2]
    nci = 0 if carry is None else len(carry["ins"])
    nco = 0 if carry is None else len(carry["outs"])
    nei = 0 if epi is None else len(epi["ins"])
    neo = 1 if epi is None else len(epi["outs"])

    def body(*refs):
        a_ref, b_ref = refs[:2]
        ein = refs[2:2 + nei]
        cin = refs[2 + nei:2 + nei + nci]
        outs = refs[2 + nei + nci:2 + nei + nci + neo]
        cout = refs[2 + nei + nci + neo:2 + nei + nci + neo + nco]
        acc = refs[2 + nei + nci + neo + nco]
        sems = refs[3 + nei + nci + neo + nco:]
        m, n, k = pl.program_id(0), pl.program_id(1), pl.program_id(2)

        if carry is not None:
            @pl.when((m == 0) & (n == 0) & (k == 0))
            def _():
                carry["start"](cin, cout, *sems)

        @pl.when(k == 0)
        def _():
            acc[...] = jnp.zeros_like(acc)

        if carry is not None and "mid" in carry:
            @pl.when((m == grid[0] - 1) & (n == 0) & (k == 0))
            def _():
                carry["mid"](cin, cout, *sems)

        acc[...] += dot(a_ref[...], b_ref[...])

        @pl.when(k == nk - 1)
        def _():
            vals = [acc[...]] if epi is None else epi["fn"](acc[...], [r[...] for r in ein])
            for r, v in zip(outs, vals):
                if isinstance(v, tuple):
                    for i, vi in enumerate(v):
                        r[i] = vi.astype(r.dtype)
                else:
                    r[...] = v.astype(r.dtype)

        if carry is not None:
            @pl.when((m == grid[0] - 1) & (n == grid[1] - 1) & (k == nk - 1))
            def _():
                carry["finish"](cin, cout, *sems)

    any_spec = pl.BlockSpec(memory_space=pl.ANY)
    scratch = [pltpu.VMEM(acc_shape, F32)]
    extra = {}
    if carry is not None:
        scratch += [pltpu.SemaphoreType.DMA((carry["nsem"],)), pltpu.SemaphoreType.DMA((carry["nsem"],))]
        extra["input_output_aliases"] = {2 + nei + i: neo + j for i, j in carry["alias"].items()}
    sem = ("arbitrary",) * 3 if carry is not None else ("parallel", "parallel", "arbitrary")
    main_outs = [(out_shape, out_spec)] if epi is None else list(epi["outs"])
    res = pl.pallas_call(
        body, name=name, grid=grid,
        in_specs=list(in_specs) + ([] if epi is None else [sp for _, sp in epi["ins"]]) + [any_spec] * nci,
        out_specs=[sp for _, sp in main_outs] + [any_spec] * nco,
        out_shape=[sh for sh, _ in main_outs] + ([] if carry is None else list(carry["outs"])),
        scratch_shapes=scratch, compiler_params=_params(sem), **extra,
    )(*operands, *([] if epi is None else [ar for ar, _ in epi["ins"]]), *([] if carry is None else carry["ins"]))
    res = list(res)
    main = res[0] if epi is None else res[:neo]
    return main if carry is None else (main, res[neo:])


def _mm_nn(a, b3, out_dtype, name, carry=None):
    M, K = a.shape
    nsh, _, Ns = b3.shape
    tm, tn, tk = _tile(M, 1024), _tile(Ns, 1408), _tile(K, 2816)
    tps, nk = Ns // tn, K // tk
    return _mm_call(
        _dot, (a, b3), (M // tm, nsh * tps, nk),
        [pl.BlockSpec((tm, tk), lambda m, n, k: (m, k)),
         pl.BlockSpec((None, tk, tn), lambda m, n, k: (n // tps, k, n % tps))],
        pl.BlockSpec((tm, tn), lambda m, n, k: (m, n)),
        jax.ShapeDtypeStruct((M, nsh * Ns), out_dtype), (tm, tn), name, carry)


def _mm_nt(a, b3, out_dtype, name, carry=None, epi=None, tn_target=1408):
    a3 = a if a.ndim == 3 else a[None]
    na, M, Ka = a3.shape
    nsh, Kw, Ns = b3.shape
    assert na * Ka == nsh * Ns
    tm, tn, tk = _tile(M, 1024), _tile(Kw, tn_target), _tile(int(np.gcd(Ka, Ns)), 3072)
    kpa, kps = Ka // tk, Ns // tk
    return _mm_call(
        _dot_nt, (a3, b3), (M // tm, Kw // tn, nsh * kps),
        [pl.BlockSpec((None, tm, tk), lambda m, n, k: (k // kpa, m, k % kpa)),
         pl.BlockSpec((None, tn, tk), lambda m, n, k: (k // kps, n, k % kps))],
        pl.BlockSpec((tm, tn), lambda m, n, k: (m, n)),
        jax.ShapeDtypeStruct((M, Kw), out_dtype), (tm, tn), name, carry,
        None if epi is None else epi(tm, tn))


def _mm_tn(a, g, nsh, name, carry=None, a_is_t=False):
    Tk, M = a.shape[::-1] if a_is_t else a.shape
    g3 = g if g.ndim == 3 else g[None]
    ng, _, Ng = g3.shape
    Ns = ng * Ng // nsh
    tm, tn, tk = _tile(M // 2, 1408), _tile(int(np.gcd(Ng, Ns)), 1408), _tile(Tk, 2048)
    mh, tps, tpg, nk = (M // 2) // tm, Ns // tn, Ng // tn, Tk // tk
    return _mm_call(
        _dot if a_is_t else _dot_tn, (a, g3), (M // tm, nsh * tps, nk),
        [pl.BlockSpec((tm, tk), lambda m, n, k: (m, k)) if a_is_t else pl.BlockSpec((tk, tm), lambda m, n, k: (k, m)),
         pl.BlockSpec((None, tk, tn), lambda m, n, k: (n // tpg, k, n % tpg))],
        pl.BlockSpec((None, None, tm, tn), lambda m, n, k: (n // tps, m // mh, m % mh, n % tps)),
        jax.ShapeDtypeStruct((nsh, 2, M // 2, Ns), F32), (tm, tn), name, carry)


def _mm_tn_half(at, g, nsh, other, name, carry=None):
    M, Tk = at.shape
    Ns = g.shape[1] // nsh
    tm, tn, tk = _tile(M // 2, 1408), _tile(Ns, 1408), _tile(Tk, 2048)
    mh, tps, nk = (M // 2) // tm, Ns // tn, Tk // tk

    def half():
        c = lax.axis_index("c")
        return 1 - c if other else c

    return _mm_call(
        _dot, (at, g), (mh, nsh * tps, nk),
        [pl.BlockSpec((tm, tk), lambda m, n, k: (half() * mh + m, k)),
         pl.BlockSpec((tk, tn), lambda m, n, k: (k, n))],
        pl.BlockSpec((None, tm, tn), lambda m, n, k: (n // tps, m, n % tps)),
        jax.ShapeDtypeStruct((nsh, M // 2, Ns), F32), (tm, tn), name, carry)


def _ffn_in_fused(h2, wi3, name):
    M, K = h2.shape
    _, _, Ns = wi3.shape
    fh = 2 * Ns
    tm, tn = _tile(M, 512), _tile(Ns, 1408)
    tps = Ns // tn

    def body(h_ref, ba_ref, bu_ref, au_ref, sw_ref, swt_ref):
        h = h_ref[...]
        a, u = _dot(h, ba_ref[...]), _dot(h, bu_ref[...])
        au_ref[0] = a.astype(au_ref.dtype)
        au_ref[1] = u.astype(au_ref.dtype)
        sw = (_silu(a) * u).astype(sw_ref.dtype)
        sw_ref[...] = sw
        swt_ref[...] = sw.T

    return pl.pallas_call(
        body, name=name, grid=(M // tm, fh // tn),
        in_specs=[pl.BlockSpec((tm, K), lambda m, n: (m, 0)),
                  pl.BlockSpec((None, K, tn), lambda m, n: (n // tps, 0, n % tps)),
                  pl.BlockSpec((None, K, tn), lambda m, n: (2 + n // tps, 0, n % tps))],
        out_specs=[pl.BlockSpec((2, tm, tn), lambda m, n: (0, m, n)), pl.BlockSpec((tm, tn), lambda m, n: (m, n)),
                   pl.BlockSpec((tn, tm), lambda m, n: (n, m))],
        out_shape=[jax.ShapeDtypeStruct((2, M, fh), BF16), jax.ShapeDtypeStruct((M, fh), BF16),
                   jax.ShapeDtypeStruct((fh, M), BF16)],
        compiler_params=_params(("parallel", "parallel")),
    )(h2, wi3, wi3)


def _rowwise(fn, nblk, tm, rins, vins, routs, accs, name):
    nr, nv, no, na = len(rins), len(vins), len(routs), len(accs)

    def body(*refs):
        i = pl.program_id(0)
        outs, accv = fn(i, [r[...] for r in refs[:nr]], [r[...] for r in refs[nr:nr + nv]])
        for r, v, spec in zip(refs[nr + nv:nr + nv + no], outs, routs):
            v = v.astype(r.dtype)
            r[...] = v.T if len(spec) == 3 else v
        arefs = refs[nr + nv + no:]
        if na:
            @pl.when(i == 0)
            def _():
                for a in arefs:
                    a[...] = jnp.zeros_like(a)

            for a, v in zip(arefs, accv):
                a[...] += v

    def row_spec(w, cb, rm):
        if rm is None:
            return pl.BlockSpec((tm, w), lambda i: (i, cb))
        return pl.BlockSpec((tm, w), lambda i: (rm(i), cb))

    in_specs = [row_spec(w, cb, rm) for (_, w, cb, rm) in rins]
    in_specs += [pl.BlockSpec(v.shape, lambda i: (0, 0)) for v in vins]
    def out_of(spec):
        w, dt = spec[:2]
        if len(spec) == 3:
            return pl.BlockSpec((w, tm), lambda i: (0, i)), jax.ShapeDtypeStruct((w, nblk * tm), dt)
        return pl.BlockSpec((tm, w), lambda i: (i, 0)), jax.ShapeDtypeStruct((nblk * tm, w), dt)

    out_specs = [out_of(sp)[0] for sp in routs] + [pl.BlockSpec((1, w), lambda i: (0, 0)) for w in accs]
    out_shape = [out_of(sp)[1] for sp in routs] + [jax.ShapeDtypeStruct((1, w), F32) for w in accs]
    res = pl.pallas_call(
        body, name=name, grid=(nblk,), in_specs=in_specs, out_specs=out_specs, out_shape=out_shape,
        compiler_params=_params(("arbitrary",)),
    )(*[r[0] for r in rins], *vins)
    return list(res)


def _colsum(v):
    return jnp.sum(v, axis=0, keepdims=True)


def _rms(v):
    return lax.rsqrt(jnp.mean(v * v, axis=-1, keepdims=True) + EPS)


def _all_gather8(xs):
    m_per, n = xs.shape

    def body(x_ref, out_ref, send_sems, recv_sems, local_sem):
        x, y, c = lax.axis_index("x"), lax.axis_index("y"), lax.axis_index("c")
        me, sibling = (x, y, c), (x, y, 1 - c)
        chips = [(1 - x, y), (x, 1 - y), (1 - x, 1 - y)]

        def rows(px, py, pc):
            return out_ref.at[pl.ds((4 * px + 2 * py + pc) * m_per, m_per), :]

        def copy(k, block, to, src=None):
            return pltpu.make_async_remote_copy(
                src_ref=rows(*block) if src is None else src, dst_ref=rows(*block),
                send_sem=send_sems.at[k], recv_sem=recv_sems.at[k], device_id=to, device_id_type=MESH_ID)

        mine = pltpu.make_async_copy(x_ref, rows(*me), local_sem)
        mine.start()
        first = [copy(0, me, sibling, src=x_ref)]
        first += [copy(1 + j, me, (*chip, c), src=x_ref) for j, chip in enumerate(chips)]
        for cp in first:
            cp.start()
        passed = [copy(4 + j, (*chip, c), sibling) for j, chip in enumerate(chips)]
        for j, chip in enumerate(chips):
            copy(1 + j, (*chip, c), me).wait_recv()
            passed[j].start()
        copy(0, sibling, me).wait_recv()
        for j, chip in enumerate(chips):
            copy(4 + j, (*chip, 1 - c), me).wait_recv()
        for cp in first + passed:
            cp.wait_send()
        mine.wait()

    return pl.pallas_call(
        body, name="all_gather8_%dx%d" % (m_per, n),
        out_shape=jax.ShapeDtypeStruct((8 * m_per, n), xs.dtype),
        in_specs=[pl.BlockSpec(memory_space=pltpu.VMEM)],
        out_specs=pl.BlockSpec(memory_space=pltpu.VMEM),
        scratch_shapes=[pltpu.SemaphoreType.DMA((7,)), pltpu.SemaphoreType.DMA((7,)), pltpu.SemaphoreType.DMA],
    )(xs)


def _mesh_pos():
    x, y, c = lax.axis_index("x"), lax.axis_index("y"), lax.axis_index("c")
    chips = [(1 - x, y), (x, 1 - y), (1 - x, 1 - y)]
    return x, y, c, chips


def _my_shard():
    return 2 * lax.axis_index("x") + lax.axis_index("y")


def _cast_place(w, name):
    r, cw = w.shape
    rh = r // 2
    tm = _tile(rh, max(16, (1 << 20) // (4 * cw)), 16)
    nt = rh // tm

    def body(w_ref, o_ref):
        o_ref[...] = w_ref[...].astype(o_ref.dtype)

    return pl.pallas_call(
        body, name=name, grid=(2, nt),
        in_specs=[pl.BlockSpec((tm, cw), lambda h, i: (h * nt + i, 0))],
        out_specs=pl.BlockSpec((None, None, tm, cw), lambda h, i: (_my_shard(), h, i, 0)),
        out_shape=jax.ShapeDtypeStruct((4, 2, rh, cw), BF16),
        compiler_params=_params(("parallel", "parallel")),
    )(w)


def _exchange_gather(bufs):
    n = len(bufs)

    def copies(out, send_sems, recv_sems, base):
        def copy(i, k, shard, half, to):
            dst = out[i].at[shard, half]
            return pltpu.make_async_remote_copy(
                src_ref=dst, dst_ref=dst, send_sem=send_sems.at[base + 6 * i + k],
                recv_sem=recv_sems.at[base + 6 * i + k], device_id=to, device_id_type=MESH_ID)
        return copy

    def first(copy):
        x, y, c, chips = _mesh_pos()
        return [copy(i, j, 2 * x + y, c, (*chip, c)) for i in range(n) for j, chip in enumerate(chips)]

    def start(cin, out, send_sems, recv_sems, base=0):
        for cp in first(copies(out, send_sems, recv_sems, base)):
            cp.start()

    def passed(copy):
        x, y, c, chips = _mesh_pos()
        return [copy(i, 3 + j, 2 * chip[0] + chip[1], c, (x, y, 1 - c)) for j, chip in enumerate(chips) for i in range(n)]

    def mid(cin, out, send_sems, recv_sems, base=0):
        copy = copies(out, send_sems, recv_sems, base)
        x, y, c, chips = _mesh_pos()
        for j, chip in enumerate(chips):
            for i in range(n):
                copy(i, j, 2 * chip[0] + chip[1], c, (x, y, c)).wait_recv()
        for cp in passed(copy):
            cp.start()

    def finish(cin, out, send_sems, recv_sems, base=0):
        copy = copies(out, send_sems, recv_sems, base)
        x, y, c, chips = _mesh_pos()
        for j, chip in enumerate(chips):
            for i in range(n):
                copy(i, 3 + j, 2 * chip[0] + chip[1], 1 - c, (x, y, c)).wait_recv()
        for cp in first(copy) + passed(copy):
            cp.wait_send()

    return dict(ins=list(bufs), outs=[jax.ShapeDtypeStruct(b.shape, b.dtype) for b in bufs],
                alias={i: i for i in range(n)}, nsem=6 * n, start=start, mid=mid, finish=finish)


def _exchange_join(a, b):
    nai, nao = len(a["ins"]), len(a["outs"])

    def start(cin, cout, send_sems, recv_sems, base=0):
        a["start"](cin[:nai], cout[:nao], send_sems, recv_sems, base)
        b["start"](cin[nai:], cout[nao:], send_sems, recv_sems, base + a["nsem"])

    def mid(cin, cout, send_sems, recv_sems, base=0):
        if "mid" in a:
            a["mid"](cin[:nai], cout[:nao], send_sems, recv_sems, base)
        if "mid" in b:
            b["mid"](cin[nai:], cout[nao:], send_sems, recv_sems, base + a["nsem"])

    def finish(cin, cout, send_sems, recv_sems, base=0):
        a["finish"](cin[:nai], cout[:nao], send_sems, recv_sems, base)
        b["finish"](cin[nai:], cout[nao:], send_sems, recv_sems, base + a["nsem"])

    alias = dict(a["alias"])
    alias.update({nai + i: nao + j for i, j in b["alias"].items()})
    return dict(ins=a["ins"] + b["ins"], outs=a["outs"] + b["outs"], alias=alias, nsem=a["nsem"] + b["nsem"],
                start=start, mid=mid, finish=finish)


def _exchange_call(ex, name):
    nci, nco = len(ex["ins"]), len(ex["outs"])

    def body(*refs):
        cin, cout, sems = refs[:nci], refs[nci:nci + nco], refs[nci + nco:]
        ex["start"](cin, cout, *sems)
        if "mid" in ex:
            ex["mid"](cin, cout, *sems)
        ex["finish"](cin, cout, *sems)

    any_spec = pl.BlockSpec(memory_space=pl.ANY)
    return list(pl.pallas_call(
        body, name=name, out_shape=list(ex["outs"]), in_specs=[any_spec] * nci, out_specs=[any_spec] * nco,
        input_output_aliases=dict(ex["alias"]),
        scratch_shapes=[pltpu.SemaphoreType.DMA((ex["nsem"],)), pltpu.SemaphoreType.DMA((ex["nsem"],))],
    )(*ex["ins"]))


def _exchange_swap_other_half(gs):
    n = len(gs)

    def copies(g, land, send_sems, recv_sems, base):
        x, y, c, _ = _mesh_pos()
        return [pltpu.make_async_remote_copy(
            src_ref=g[i].at[:, 1 - c] if len(gs[i].shape) == 4 else g[i], dst_ref=land[i],
            send_sem=send_sems.at[base + i],
            recv_sem=recv_sems.at[base + i], device_id=(x, y, 1 - c), device_id_type=MESH_ID) for i in range(n)]

    def start(g, land, send_sems, recv_sems, base=0):
        for cp in copies(g, land, send_sems, recv_sems, base):
            cp.start()

    def finish(g, land, send_sems, recv_sems, base=0):
        for cp in copies(g, land, send_sems, recv_sems, base):
            cp.wait()

    return dict(ins=list(gs), outs=[jax.ShapeDtypeStruct((4,) + g.shape[-2:], g.dtype) for g in gs],
                alias={}, nsem=n, start=start, finish=finish)


def _exchange_scatter(ps):
    n = len(ps)

    def copies(p, land, send_sems, recv_sems, base):
        x, y, c, chips = _mesh_pos()
        return [pltpu.make_async_remote_copy(
            src_ref=p[i].at[2 * chip[0] + chip[1]], dst_ref=land[i].at[j],
            send_sem=send_sems.at[base + 3 * i + j], recv_sem=recv_sems.at[base + 3 * i + j],
            device_id=(*chip, c), device_id_type=MESH_ID) for i in range(n) for j, chip in enumerate(chips)]

    def start(p, land, send_sems, recv_sems, base=0):
        for cp in copies(p, land, send_sems, recv_sems, base):
            cp.start()

    def finish(p, land, send_sems, recv_sems, base=0):
        for cp in copies(p, land, send_sems, recv_sems, base):
            cp.wait()

    return dict(ins=list(ps), outs=[jax.ShapeDtypeStruct((3,) + p.shape[1:], p.dtype) for p in ps],
                alias={}, nsem=3 * n, start=start, finish=finish)


def _exchange_swap_result(bufs):
    n = len(bufs)

    def copies(out, send_sems, recv_sems, base, half):
        x, y, c, _ = _mesh_pos()
        h = c if half == "mine" else 1 - c
        return [pltpu.make_async_remote_copy(
            src_ref=out[i].at[h], dst_ref=out[i].at[h], send_sem=send_sems.at[base + i],
            recv_sem=recv_sems.at[base + i], device_id=(x, y, 1 - c), device_id_type=MESH_ID) for i in range(n)]

    def start(cin, out, send_sems, recv_sems, base=0):
        for cp in copies(out, send_sems, recv_sems, base, "mine"):
            cp.start()

    def finish(cin, out, send_sems, recv_sems, base=0):
        for cp in copies(out, send_sems, recv_sems, base, "theirs"):
            cp.wait_recv()
        for cp in copies(out, send_sems, recv_sems, base, "mine"):
            cp.wait_send()

    return dict(ins=list(bufs), outs=[jax.ShapeDtypeStruct(b.shape, b.dtype) for b in bufs],
                alias={i: i for i in range(n)}, nsem=n, start=start, finish=finish)


def _add_own_half(g, land, name):
    rh, cw = g.shape[-2:]
    tm = _tile(rh, max(16, (1 << 20) // (4 * cw)), 16)

    def body(g_ref, l_ref, o_ref):
        o_ref[...] = (g_ref[...] + l_ref[...]).astype(o_ref.dtype)

    mine = (pl.BlockSpec((None, None, tm, cw), lambda s, i: (s, lax.axis_index("c"), i, 0)) if g.ndim == 4
            else pl.BlockSpec((None, tm, cw), lambda s, i: (s, i, 0)))
    return pl.pallas_call(
        body, name=name, grid=(4, rh // tm),
        in_specs=[mine,
                  pl.BlockSpec((None, tm, cw), lambda s, i: (s, i, 0))],
        out_specs=pl.BlockSpec((None, tm, cw), lambda s, i: (s, i, 0)),
        out_shape=jax.ShapeDtypeStruct((4, rh, cw), BF16),
        compiler_params=_params(("parallel", "parallel")),
    )(g, land)


def _sum_pieces(part, land, name):
    _, rh, cw = land.shape
    tm = _tile(rh, max(16, (1 << 20) // (4 * cw)), 16)

    def body(p_ref, l_ref, o_ref):
        v = l_ref[...].astype(F32)
        o_ref[...] = (p_ref[...].astype(F32) + v[0]) + (v[1] + v[2])

    return pl.pallas_call(
        body, name=name, grid=(rh // tm,),
        in_specs=[pl.BlockSpec((None, tm, cw), lambda i: (_my_shard(), i, 0)),
                  pl.BlockSpec((3, tm, cw), lambda i: (0, i, 0))],
        out_specs=pl.BlockSpec((None, tm, cw), lambda i: (lax.axis_index("c"), i, 0)),
        out_shape=jax.ShapeDtypeStruct((2, rh, cw), F32),
        compiler_params=_params(("parallel",)),
    )(part, land)


def _sum8(g, name):
    def body(g_ref, o_ref):
        acc = g_ref[0]
        for k in range(1, 8):
            acc = acc + g_ref[k]
        o_ref[...] = acc

    return pl.pallas_call(body, name=name, out_shape=jax.ShapeDtypeStruct(g.shape[1:], F32))(g)


def _silu(v):
    return v * _sigmoid(v)


def _dsilu(v):
    s = _sigmoid(v)
    return s * (1.0 + v * (1.0 - s))


def _ada_fwd(cin, w, b):
    d, ns = w.shape
    tn = _tile(ns, 512)

    def body(c_ref, w_ref, b_ref, o_ref):
        o_ref[...] = _dot(_silu(c_ref[...]), w_ref[...], HI) + b_ref[...]

    return pl.pallas_call(
        body, name="ada_fwd", grid=(ns // tn,),
        in_specs=[pl.BlockSpec(cin.shape, lambda n: (0, 0)), pl.BlockSpec((d, tn), lambda n: (0, n)),
                  pl.BlockSpec((1, tn), lambda n: (0, n))],
        out_specs=pl.BlockSpec((cin.shape[0], tn), lambda n: (0, n)),
        out_shape=jax.ShapeDtypeStruct((cin.shape[0], ns), F32),
        compiler_params=_params(("parallel",)),
    )(cin, w, b)


def _ada_bwd(cin, w, dm):
    d, ns = w.shape
    tn = _tile(ns, 512)

    def body(c_ref, w_ref, d_ref, dw_ref, dc_ref):
        n = pl.program_id(0)

        @pl.when(n == 0)
        def _():
            dc_ref[...] = jnp.zeros_like(dc_ref)

        dw_ref[...] = _dot_tn(_silu(c_ref[...]), d_ref[...], HI)
        dc_ref[...] += _dot_nt(d_ref[...], w_ref[...], HI)

    return pl.pallas_call(
        body, name="ada_bwd", grid=(ns // tn,),
        in_specs=[pl.BlockSpec(cin.shape, lambda n: (0, 0)), pl.BlockSpec((d, tn), lambda n: (0, n)),
                  pl.BlockSpec((cin.shape[0], tn), lambda n: (0, n))],
        out_specs=[pl.BlockSpec((d, tn), lambda n: (0, n)), pl.BlockSpec(cin.shape, lambda n: (0, 0))],
        out_shape=[jax.ShapeDtypeStruct((d, ns), F32), jax.ShapeDtypeStruct(cin.shape, F32)],
        compiler_params=_params(("arbitrary",)),
    )(cin, w, dm)


def _bias_tables(rpb, rows):
    kh = min(WIN_H, rows)
    fold, onehot, in_win = _bias_selectors(kh)
    t = jnp.einsum("hdc,dsj->hsjc", rpb, jnp.asarray(fold), precision=HI)
    t = jnp.einsum("hsjc,cqk->hsqjk", t, jnp.asarray(onehot), precision=HI)
    t = jnp.where(jnp.asarray(in_win)[None, None, :, None, :], t, MASK_VALUE)
    return t.reshape(rpb.shape[0], kh, GRID_W, kh * GRID_W).astype(F32)


def _bias_selectors(kh):
    col = np.arange(GRID_W)
    col_start = np.clip(col - WIN_W // 2, 0, GRID_W - WIN_W)
    in_win = (col[None, :] >= col_start[:, None]) & (col[None, :] < col_start[:, None] + WIN_W)
    dc_idx = np.clip(col[None, :] - col[:, None], 1 - WIN_W, WIN_W - 1) + WIN_W - 1
    onehot = np.zeros((2 * WIN_W - 1, GRID_W, GRID_W), np.float32)
    qq, kk = np.nonzero(in_win)
    onehot[dc_idx[qq, kk], qq, kk] = 1.0
    fold = np.zeros((2 * WIN_H - 1, kh, kh), np.float32)
    for sh in range(kh):
        for j in range(kh):
            fold[j - sh + WIN_H - 1, sh, j] = 1.0
    return fold, onehot, in_win


def _mm_f32(a, b, name):
    M, K = a.shape
    N = b.shape[1]
    tm = _tile(M, 256, 8)

    def body(a_ref, b_ref, o_ref):
        o_ref[...] = _dot(a_ref[...], b_ref[...], HI)

    return pl.pallas_call(
        body, name=name, grid=(M // tm,),
        in_specs=[pl.BlockSpec((tm, K), lambda i: (i, 0)), pl.BlockSpec((K, N), lambda i: (0, 0))],
        out_specs=pl.BlockSpec((tm, N), lambda i: (i, 0)),
        out_shape=jax.ShapeDtypeStruct((M, N), F32),
        compiler_params=_params(("parallel",)),
    )(a, b)


def _bias_tables_transpose(dbias, rows):
    kh = min(WIN_H, rows)
    nh = dbias.shape[0]
    fold, onehot, _ = _bias_selectors(kh)
    ndc = onehot.shape[0]
    sel = np.zeros((GRID_W * GRID_W, LANES), np.float32)
    sel[:, :ndc] = onehot.reshape(ndc, -1).T
    x = dbias.reshape(nh, kh, GRID_W, kh, GRID_W).transpose(0, 1, 3, 2, 4).reshape(nh * kh * kh, GRID_W * GRID_W)
    z = _mm_f32(x, jnp.asarray(sel), "rpb_fold")[:, :ndc].reshape(nh, kh, kh, ndc)
    return jnp.einsum("dsj,hsjc->hdc", jnp.asarray(fold), z, precision=HI)


def _na_geometry(S):
    rows = S // GRID_W
    kh = min(WIN_H, rows)

    def row_start(r):
        return jnp.clip(r - kh // 2, 0, rows - kh)

    return rows, kh, row_start


def _na_by_head(ref, rr, lane):
    t = ref[rr * GRID_W:(rr + 1) * GRID_W, :]
    zero = jnp.zeros_like(t)
    return jnp.concatenate([jnp.where(lane < NA_HEAD_DIM, t, zero), jnp.where(lane >= NA_HEAD_DIM, t, zero)], axis=0)


def _na_pick_head(t2, lane):
    return jnp.where(lane < NA_HEAD_DIM, t2[:GRID_W], t2[GRID_W:])


def _na_scores(q_ref, k_ref, b_ref, i, nrs, nb, S, L, row_start, lane):
    qh = jnp.concatenate([_na_by_head(q_ref, rr, lane) for rr in range(nrs)], axis=0)
    sc = _dot_nt(qh, k_ref[pl.ds(S, L), :])
    starts, shifts, sb = [], [], []
    for rr in range(nrs):
        r = i * nrs + rr
        rs = row_start(r)
        starts.append(pl.multiple_of(rs * GRID_W, GRID_W))
        shifts.append(r - rs)
        bias = jnp.concatenate([b_ref[0, r - rs], b_ref[1, r - rs]], axis=0)
        sb.append(_dot_nt(qh[rr * 2 * GRID_W:(rr + 1) * 2 * GRID_W], k_ref[pl.ds(starts[-1], nb), :]) + bias)
    return qh, jnp.concatenate(sb, axis=0), sc, starts, shifts


def _na_fwd(qs, kb, vb, bias, S, L):
    T, naw = qs.shape
    rows, kh, row_start = _na_geometry(S)
    nb = kh * GRID_W
    npair = naw // LANES

    nrs = min(NA_ROWS_PER_STEP, rows)
    assert rows % nrs == 0

    def body(q_ref, k_ref, v_ref, b_ref, o_ref, lse_ref):
        i = pl.program_id(1)
        lane = lax.broadcasted_iota(jnp.int32, (GRID_W, LANES), 1)
        _, sb, sc, starts, _ = _na_scores(q_ref, k_ref, b_ref, i, nrs, nb, S, L, row_start, lane)
        m = jnp.maximum(jnp.max(sb, axis=-1, keepdims=True), jnp.max(sc, axis=-1, keepdims=True))
        pb, pc = jnp.exp(sb - m), jnp.exp(sc - m)
        l = jnp.sum(pb, axis=-1, keepdims=True) + jnp.sum(pc, axis=-1, keepdims=True)
        inv = 1.0 / l
        pb16, pc16 = (pb * inv).astype(BF16), (pc * inv).astype(BF16)
        oc = _dot(pc16, v_ref[pl.ds(S, L), :])
        lse = jnp.broadcast_to(m + jnp.log(l), oc.shape)
        for rr in range(nrs):
            two = slice(rr * 2 * GRID_W, (rr + 1) * 2 * GRID_W)
            rsl = slice(rr * GRID_W, (rr + 1) * GRID_W)
            o2 = oc[two] + _dot(pb16[two], v_ref[pl.ds(starts[rr], nb), :])
            o_ref[rsl, :] = _na_pick_head(o2, lane).astype(o_ref.dtype)
            lse_ref[rsl, :] = _na_pick_head(lse[two], lane)

    blk = pl.BlockSpec((nrs * GRID_W, LANES), lambda p, i: (i, p))
    col = pl.BlockSpec((T, LANES), lambda p, i: (0, p))
    return pl.pallas_call(
        body, name="na_fwd", grid=(npair, rows // nrs),
        in_specs=[blk, col, col, pl.BlockSpec((2, kh, GRID_W, nb), lambda p, i: (p, 0, 0, 0))],
        out_specs=[blk, blk],
        out_shape=[jax.ShapeDtypeStruct((S, naw), BF16), jax.ShapeDtypeStruct((S, naw), F32)],
        compiler_params=_params(("parallel", "arbitrary")),
    )(qs, kb, vb, bias)


def _na_bwd(qs, kb, vb, bias, do, o, lse, S, L):
    T, naw = qs.shape
    rows, kh, row_start = _na_geometry(S)
    nb = kh * GRID_W
    npair = naw // LANES

    nrs = min(NA_ROWS_PER_STEP, rows)
    assert rows % nrs == 0

    def body(q_ref, k_ref, v_ref, b_ref, do_ref, o_ref, lse_ref, dq_ref, dk_ref, dv_ref, db_ref):
        i = pl.program_id(1)

        @pl.when(i == 0)
        def _():
            dk_ref[...] = jnp.zeros_like(dk_ref)
            dv_ref[...] = jnp.zeros_like(dv_ref)
            db_ref[...] = jnp.zeros_like(db_ref)

        lane = lax.broadcasted_iota(jnp.int32, (GRID_W, LANES), 1)
        qh, sb, sc, starts, shifts = _na_scores(q_ref, k_ref, b_ref, i, nrs, nb, S, L, row_start, lane)
        doh = jnp.concatenate([_na_by_head(do_ref, rr, lane) for rr in range(nrs)], axis=0)
        o2 = jnp.concatenate([o_ref[rr * GRID_W:(rr + 1) * GRID_W, :] for rr in range(nrs) for _ in range(2)], axis=0)
        lse = jnp.concatenate([lse_ref[rr * GRID_W:(rr + 1) * GRID_W, :][:, hh * NA_HEAD_DIM:hh * NA_HEAD_DIM + 1]
                               for rr in range(nrs) for hh in range(2)], axis=0)
        pb, pc = jnp.exp(sb - lse), jnp.exp(sc - lse)
        delta = jnp.sum(doh.astype(F32) * o2.astype(F32), axis=-1, keepdims=True)
        dpb = jnp.concatenate([_dot_nt(doh[rr * 2 * GRID_W:(rr + 1) * 2 * GRID_W], v_ref[pl.ds(starts[rr], nb), :])
                               for rr in range(nrs)], axis=0)
        dsb = pb * (dpb - delta)
        dsc = pc * (_dot_nt(doh, v_ref[pl.ds(S, L), :]) - delta)
        dsb16, dsc16, pb16, pc16 = dsb.astype(BF16), dsc.astype(BF16), pb.astype(BF16), pc.astype(BF16)
        dqc = _dot(dsc16, k_ref[pl.ds(S, L), :])
        dk_ref[pl.ds(S, L), :] += _dot_tn(dsc16, qh)
        dv_ref[pl.ds(S, L), :] += _dot_tn(pc16, doh)
        for rr in range(nrs):
            two = slice(rr * 2 * GRID_W, (rr + 1) * 2 * GRID_W)
            band = pl.ds(starts[rr], nb)
            dq2 = dqc[two] + _dot(dsb16[two], k_ref[band, :])
            dq_ref[rr * GRID_W:(rr + 1) * GRID_W, :] = _na_pick_head(dq2, lane)
            dk_ref[band, :] += _dot_tn(dsb16[two], qh[two])
            dv_ref[band, :] += _dot_tn(pb16[two], doh[two])
            for hh in range(2):
                db_ref[hh, shifts[rr]] += dsb[(2 * rr + hh) * GRID_W:(2 * rr + hh + 1) * GRID_W]

    blk = pl.BlockSpec((nrs * GRID_W, LANES), lambda p, r: (r, p))
    col = pl.BlockSpec((T, LANES), lambda p, r: (0, p))
    return pl.pallas_call(
        body, name="na_bwd", grid=(npair, rows // nrs),
        in_specs=[blk, col, col, pl.BlockSpec((2, kh, GRID_W, nb), lambda p, r: (p, 0, 0, 0)), blk, blk, blk],
        out_specs=[blk, col, col, pl.BlockSpec((2, kh, GRID_W, nb), lambda p, r: (p, 0, 0, 0))],
        out_shape=[jax.ShapeDtypeStruct((S, naw), F32), jax.ShapeDtypeStruct((T, naw), F32),
                   jax.ShapeDtypeStruct((T, naw), F32), jax.ShapeDtypeStruct(bias.shape, F32)],
        compiler_params=_params(("parallel", "arbitrary")),
    )(qs, kb, vb, bias, do, o, lse)


def _hg_cols(naw, hgf, rev):
    qcol = (3 * naw) // hgf
    fcol = (3 * naw + hgf * (2 if rev else 1)) // hgf
    icol = (3 * naw + 3 * hgf) // hgf
    return qcol, fcol, icol


def _hg_chunk_order(S, L, rev):
    ncl, ncc = S // HG_CHUNK, L // HG_CHUNK
    nc = ncl + ncc

    def chunk_of(i):
        if rev:
            return nc - 1 - i
        return jnp.where(i < ncc, ncl + i, i - ncc)

    return nc, ncl, chunk_of


def _hg_gates(q, z, lb, rev):
    row = lax.broadcasted_iota(jnp.int32, (HG_CHUNK, HG_CHUNK), 0)
    colm = lax.broadcasted_iota(jnp.int32, (HG_CHUNK, HG_CHUNK), 1)
    tri = (colm >= row) if rev else (row >= colm)
    trif = tri.astype(F32)
    sig = _sigmoid(z)
    f = lb + (1.0 - lb) * sig
    lf = jnp.log(f)
    k = 1.0 - f
    cum = _dot(trif, lf, HI)
    mid = cum[HG_CHUNK // 2:HG_CHUNK // 2 + 1, :]
    last = cum[0:1, :] if rev else cum[HG_CHUNK - 1:HG_CHUNK, :]
    eq = jnp.exp(jnp.clip(cum - mid, -EXP_CLAMP, EXP_CLAMP))
    ek = jnp.exp(jnp.clip(mid - cum, -EXP_CLAMP, EXP_CLAMP))
    return tri, trif, sig, f, k, cum, last, eq, ek


def _hg_fwd(u, lbr, S, L, naw, hgf, rev):
    T = S + L
    nh = hgf // HG_DIM
    nc, ncl, chunk_of = _hg_chunk_order(S, L, rev)
    qcol, fcol, icol = _hg_cols(naw, hgf, rev)

    def step(i, q_ref, z_ref, v_ref, lb_ref, o_ref, st_ref, state):
        @pl.when(i == 0)
        def _():
            state[...] = jnp.zeros_like(state)

        q, z, v = q_ref[...], z_ref[...], v_ref[...]
        tri, _, _, _, k, cum, last, eq, ek = _hg_gates(q, z, lb_ref[...], rev)
        qe, ke = (q * eq).astype(BF16), (k * ek).astype(BF16)
        qd, kd = (q * jnp.exp(cum)).astype(BF16), (k * jnp.exp(last - cum)).astype(BF16)
        v16, el = v.astype(BF16), jnp.exp(last)
        for h in range(nh):
            sl = slice(h * HG_DIM, (h + 1) * HG_DIM)
            a = jnp.where(tri, _dot_nt(qe[:, sl], ke[:, sl]), 0.0)
            s0 = state[h]
            st_ref[h] = s0
            o_ref[:, sl] = _dot(a.astype(BF16), v16[:, sl]) + _dot_nt(qd[:, sl], s0.astype(BF16))
            state[h] = s0 * el[:, sl] + _dot_tn(v16[:, sl], kd[:, sl])

    def blk(cb):
        return pl.BlockSpec((HG_CHUNK, hgf), lambda i: (chunk_of(i), cb))

    return dict(
        step=step, nc=nc, operands=[u, u, u, lbr],
        in_specs=[blk(qcol), blk(fcol), blk(icol), pl.BlockSpec((1, hgf), lambda i: (0, 0))],
        out_specs=[pl.BlockSpec((HG_CHUNK, hgf), lambda i: (chunk_of(i), 0)),
                   pl.BlockSpec((None, nh, HG_DIM, HG_DIM), lambda i: (chunk_of(i), 0, 0, 0))],
        out_shape=[jax.ShapeDtypeStruct((T, hgf), F32), jax.ShapeDtypeStruct((nc, nh, HG_DIM, HG_DIM), F32)],
        scratch=[pltpu.VMEM((nh, HG_DIM, HG_DIM), F32)])


def _hg_both(parts, name):
    nin = [len(p["in_specs"]) for p in parts]
    nout = [len(p["out_specs"]) for p in parts]
    nscr = [len(p["scratch"]) for p in parts]

    def body(*refs):
        i = pl.program_id(0)
        ins, outs, scr = refs[:sum(nin)], refs[sum(nin):sum(nin) + sum(nout)], refs[sum(nin) + sum(nout):]
        for d, p in enumerate(parts):
            p["step"](i, *ins[sum(nin[:d]):sum(nin[:d + 1])], *outs[sum(nout[:d]):sum(nout[:d + 1])],
                      *scr[sum(nscr[:d]):sum(nscr[:d + 1])])

    res = pl.pallas_call(
        body, name=name, grid=(parts[0]["nc"],),
        in_specs=[sp for p in parts for sp in p["in_specs"]],
        out_specs=[sp for p in parts for sp in p["out_specs"]],
        out_shape=[sh for p in parts for sh in p["out_shape"]],
        scratch_shapes=[sc for p in parts for sc in p["scratch"]],
        compiler_params=_params(("arbitrary",)),
    )(*[op for p in parts for op in p["operands"]])
    return [list(res[sum(nout[:d]):sum(nout[:d + 1])]) for d in range(len(parts))]


def _hg_bwd(u, lbr, st, do, S, L, naw, hgf, rev):
    T = S + L
    nh = hgf // HG_DIM
    nc, ncl, chunk_fwd = _hg_chunk_order(S, L, rev)
    qcol, fcol, icol = _hg_cols(naw, hgf, rev)

    def chunk_of(j):
        return chunk_fwd(nc - 1 - j)

    def step(j, q_ref, z_ref, v_ref, lb_ref, st_ref, do_ref, dq_ref, dz_ref, dv_ref, dlb_ref, dstate,
             dqe_s, dke_s, dqd_s, dkd_s, dl_s):
        @pl.when(j == 0)
        def _():
            dstate[...] = jnp.zeros_like(dstate)
            dlb_ref[...] = jnp.zeros_like(dlb_ref)

        q, z, v = q_ref[...], z_ref[...], v_ref[...]
        lb = lb_ref[...]
        tri, trif, sig, f, k, cum, last, eq, ek = _hg_gates(q, z, lb, rev)
        ec, el, ekd = jnp.exp(cum), jnp.exp(last), jnp.exp(last - cum)
        qe, ke, qd, kd = q * eq, k * ek, q * ec, k * ekd
        qd16, kd16 = qd.astype(BF16), kd.astype(BF16)
        dout = jnp.where(chunk_of(j) < ncl, do_ref[...], 0.0)
        qe2, ke2, v16, dout16 = _split2(qe), _split2(ke), v.astype(BF16), dout.astype(BF16)
        for h in range(nh):
            sl = slice(h * HG_DIM, (h + 1) * HG_DIM)
            qeh, keh = [(t[0][:, sl], t[1][:, sl]) for t in (qe2, ke2)]
            a = jnp.where(tri, _dot_nt(qeh[0], keh[0]), 0.0).astype(BF16)
            s0 = st_ref[h]
            ds1 = dstate[h]
            s016, ds116 = s0.astype(BF16), ds1.astype(BF16)
            dv_ref[:, sl] = (_dot_tn(a, dout16[:, sl]) + _dot_nt(kd16[:, sl], ds116)).astype(dv_ref.dtype)
            da2 = _split2(jnp.where(tri, _dot_nt(dout16[:, sl], v16[:, sl]), 0.0))
            dqe_s[:, sl] = _dot_x3(_dot, da2, keh)
            dke_s[:, sl] = _dot_x3(_dot_tn, da2, qeh)
            dqd_s[:, sl] = _dot(dout16[:, sl], s016)
            dkd_s[:, sl] = _dot(v16[:, sl], ds116)
            dl_s[:, sl] = _colsum(ds1 * s0)
            dstate[h] = _dot_tn(dout16[:, sl], qd16[:, sl]) + ds1 * el[:, sl]
        dqe, dke, dqd, dkd = dqe_s[...], dke_s[...], dqd_s[...], dkd_s[...]
        dq_ref[...] = (dqe * eq + dqd * ec).astype(dq_ref.dtype)
        dk = dke * ek + dkd * ekd
        dcum = dqe * qe - dke * ke + dqd * qd - dkd * kd
        dlast = _colsum(dkd * kd) + el * dl_s[...]
        dlf = _dot_tn(trif, dcum, HI) + dlast
        df = dlf / f - dk
        dz_ref[...] = (df * (1.0 - lb) * sig * (1.0 - sig)).astype(dz_ref.dtype)
        dlb_ref[...] += _colsum(df * (1.0 - sig))

    def blk(cb):
        return pl.BlockSpec((HG_CHUNK, hgf), lambda j: (chunk_of(j), cb))

    oblk = pl.BlockSpec((HG_CHUNK, hgf), lambda j: (chunk_of(j), 0))
    wide = pltpu.VMEM((HG_CHUNK, hgf), F32)
    return dict(
        step=step, nc=nc, operands=[u, u, u, lbr, st, do],
        in_specs=[blk(qcol), blk(fcol), blk(icol), pl.BlockSpec((1, hgf), lambda j: (0, 0)),
                  pl.BlockSpec((None, nh, HG_DIM, HG_DIM), lambda j: (chunk_of(j), 0, 0, 0)),
                  pl.BlockSpec((HG_CHUNK, hgf), lambda j: (jnp.minimum(chunk_of(j), ncl - 1), 0))],
        out_specs=[oblk, oblk, oblk, pl.BlockSpec((1, hgf), lambda j: (0, 0))],
        out_shape=[jax.ShapeDtypeStruct((T, hgf), BF16)] * 3 + [jax.ShapeDtypeStruct((1, hgf), F32)],
        scratch=[pltpu.VMEM((nh, HG_DIM, HG_DIM), F32), wide, wide, wide, wide, pltpu.VMEM((1, hgf), F32)])


def _adamw(w, g, m, v, name):
    shape = w.shape
    if w.ndim != 2 or shape[0] % 8 or shape[1] % LANES:
        w, g, m, v = [a.reshape(1, -1) for a in (w, g, m, v)]
    r, cw = w.shape
    tm = _tile(r, max(8, (1 << 19) // cw), 8) if r % 8 == 0 else r
    c1 = 1.0 / (1.0 - ADAM_B1 ** ADAM_STEP)
    c2 = 1.0 / (1.0 - ADAM_B2 ** ADAM_STEP)

    def body(w_ref, g_ref, m_ref, v_ref, d_ref, nm_ref, nv_ref):
        gg = g_ref[...]
        nm = ADAM_B1 * m_ref[...] + (1.0 - ADAM_B1) * gg
        nv = ADAM_B2 * v_ref[...] + (1.0 - ADAM_B2) * (gg * gg)
        d_ref[...] = -ADAM_LR * ((nm * c1) / (jnp.sqrt(nv * c2) + ADAM_EPS) + ADAM_WD * w_ref[...])
        nm_ref[...] = nm
        nv_ref[...] = nv

    spec = pl.BlockSpec((tm, cw), lambda i: (i, 0))
    outs = pl.pallas_call(
        body, name=name, grid=(r // tm,), in_specs=[spec] * 4, out_specs=[spec] * 3,
        out_shape=[jax.ShapeDtypeStruct((r, cw), F32)] * 3,
        compiler_params=_params(("parallel",)),
    )(w, g, m, v)
    return [o.reshape(shape) for o in outs]


def kernel(x, c, ctx, c_ctx, w_ada, b_ada, norm1_g, w_in, na_rpb, hg_lb_logits, hg_norm_g, w_pa, w_pb, w_out, norm2_g, w_ffn_in, w_ffn_out, final_g, loss_target, m_c_ctx, m_w_ada, m_b_ada, m_norm1_g, m_w_in, m_na_rpb, m_hg_lb_logits, m_hg_norm_g, m_w_pa, m_w_pb, m_w_out, m_norm2_g, m_w_ffn_in, m_w_ffn_out, m_final_g, v_c_ctx, v_w_ada, v_b_ada, v_norm1_g, v_w_in, v_na_rpb, v_hg_lb_logits, v_hg_norm_g, v_w_pa, v_w_pb, v_w_out, v_norm2_g, v_w_ffn_in, v_w_ffn_out, v_final_g):
    xi, yi, ci = lax.axis_index("x"), lax.axis_index("y"), lax.axis_index("c")
    sidx = 2 * xi + yi
    eidx = 4 * xi + 2 * yi + ci

    S, D = x.shape[1], x.shape[2]
    L = ctx.shape[1]
    T = S + L
    naw = NA_HEADS * NA_HEAD_DIM
    hgf = HG_HEADS * HG_DIM
    inw = 3 * naw + 5 * hgf + 2 * D
    fh = w_ffn_out.shape[1] * 4
    ads = w_ada.shape[2]
    fs = hg_lb_logits.shape[2]
    rows = S // GRID_W
    tr = _tile(L, 256)
    nlat, nall = S // tr, T // tr
    assert naw == hgf and D % naw == 0 and S % tr == 0 and 2 * hgf <= D

    pack0 = jnp.concatenate([c, jnp.pad(hg_lb_logits.reshape(1, -1), ((0, 0), (0, D - 4 * fs))),
                             jnp.zeros((6, D), F32)], axis=0)
    g0 = _all_gather8(pack0).reshape(8, 8, D)
    cs = g0[:, 0]
    lbl = g0[::2, 1, :4 * fs].reshape(4, 2, 2, fs).transpose(1, 2, 0, 3).reshape(2, 2, 4 * fs)
    p_lb = jax.nn.softmax(lbl, axis=0)
    lb = p_lb[0]
    lbb = [lb[d].reshape(1, hgf) for d in range(2)]

    cin = jnp.concatenate([cs, c_ctx[None], jnp.zeros((7, D), F32)], axis=0)
    b_sh = lax.dynamic_slice(b_ada, (0, sidx * ads), (1, ads))
    modp = _ada_fwd(cin, w_ada[0], b_sh)
    modfull = _all_gather8(modp).reshape(8, 16, ads)[::2].transpose(1, 0, 2).reshape(16, 4 * ads)
    mod_e = jnp.pad(lax.dynamic_index_in_dim(modfull, eidx, 0, keepdims=False).reshape(N_MOD, D), ((0, 2), (0, 0)))
    mod_c = jnp.pad(modfull[8].reshape(N_MOD, D), ((0, 2), (0, 0)))

    names = ["w_in", "w_pa", "w_pb", "w_out", "w_ffn_in", "w_ffn_out"]
    placed = [_cast_place(w[0], "cast_" + nm)
              for w, nm in zip((w_in, w_pa, w_pb, w_out, w_ffn_in, w_ffn_out), names)]
    def shards(g):
        return g.reshape(4, 2 * g.shape[2], g.shape[3])

    win3 = shards(_exchange_call(_exchange_gather(placed[:1]), "gather_w_in")[0])

    x2d, ctx2d = x[0], ctx[0]

    def f_ln1(i, rv, vv):
        xl, xc = rv
        g, me, mc = vv
        isc = i >= nlat
        xt = jnp.where(isc, xc, xl)
        sh = jnp.where(isc, mc[0:1], me[0:1])
        sc = jnp.where(isc, mc[1:2], me[1:2])
        h = xt * _rms(xt) * g * (1.0 + sc) + sh
        return [h, h], []

    hb, hbt = _rowwise(f_ln1, nall, tr, [(x2d, D, 0, lambda i: jnp.minimum(i, nlat - 1)),
                                        (ctx2d, D, 0, lambda i: jnp.maximum(i - nlat, 0))],
                       [norm1_g, mod_e, mod_c], [(D, BF16), (D, BF16, "T")], [], "ln1")
    u, gathered = _mm_nn(hb, win3, F32, "mm_in", carry=_exchange_gather(placed[1:]))
    wpa3, wpb3, wout3, wi3, wfo3 = [shards(g) for g in gathered]
    wout1 = wout3.reshape(1, D, D)
    wfo1 = wfo3.reshape(1, fh, D)

    scale = NA_HEAD_DIM ** -0.5

    def f_qkv(i, rv, vv):
        q, k, v = rv
        return [q * scale, k, v], []

    qs, kb, vb = _rowwise(f_qkv, nall, tr, [(u, naw, 0, None), (u, naw, 1, None), (u, naw, 2, None)], [],
                          [(naw, BF16)] * 3, [], "qkv_cast")
    bias = _bias_tables(na_rpb[0], rows)
    o_na, lse = _na_fwd(qs, kb, vb, bias, S, L)

    (o_f, st_f), (o_b, st_b) = _hg_both(
        [_hg_fwd(u, lbb[0], S, L, naw, hgf, False), _hg_fwd(u, lbb[1], S, L, naw, hgf, True)], "hg_fwd")

    hgn = jnp.tile(hg_norm_g, (1, HG_HEADS))
    hog_cb = (3 * naw + 4 * hgf) // hgf
    ga_cb = (3 * naw + 5 * hgf) // D
    gb_cb = ga_cb + 1

    def heads_rms(o):
        return jnp.concatenate([jnp.broadcast_to(_rms(o[:, h * HG_DIM:(h + 1) * HG_DIM]), (o.shape[0], HG_DIM))
                                for h in range(HG_HEADS)], axis=1)

    def f_readout(i, rv, vv):
        of, ob_, hog = rv
        g, = vv
        o = of + ob_
        return [o * heads_rms(o) * g * _silu(hog)], []

    ob, = _rowwise(f_readout, nlat, tr, [(o_f, hgf, 0, None), (o_b, hgf, 0, None), (u, hgf, hog_cb, None)],
                   [hgn], [(hgf, BF16)], [], "hg_readout")

    ya = _mm_nn(o_na, wpa3, BF16, "mm_pa")
    yb = _mm_nn(ob, wpb3, BF16, "mm_pb")

    def f_merge(i, rv, vv):
        ya_, yb_, ga, gb = rv
        return [_sigmoid(ga) * ya_ + _sigmoid(gb) * yb_], []

    yv, = _rowwise(f_merge, nlat, tr, [(ya, D, 0, None), (yb, D, 0, None), (u, D, ga_cb, None), (u, D, gb_cb, None)],
                   [], [(D, BF16)], [], "merge")
    z = _mm_nn(yv, wout1, F32, "mm_out")

    def f_res1(i, rv, vv):
        xt, zt = rv
        g, me = vv
        x1 = xt + me[2:3] * zt
        h = x1 * _rms(x1) * g * (1.0 + me[4:5]) + me[3:4]
        return [x1, h, h], []

    x1, h2, h2t = _rowwise(f_res1, nlat, tr, [(x2d, D, 0, None), (z, D, 0, None)], [norm2_g, mod_e],
                           [(D, F32), (D, BF16), (D, BF16, "T")], [], "res1_ln2")
    au3, sw, swt = _ffn_in_fused(h2, wi3, "mm_ffn_in")
    ff =_mm_nn(sw, wfo1, F32, "mm_ffn_out")

    fg = final_g.reshape(1, D)

    def f_final(i, rv, vv):
        x1t, ft, tg = rv
        g, me = vv
        x2 = x1t + me[5:6] * ft
        r3 = _rms(x2)
        xn = x2 * r3
        err = xn * g - tg
        dyy = err * (1.0 / D)
        dxn = dyy * g
        dx2 = r3 * (dxn - xn * jnp.mean(dxn * xn, axis=-1, keepdims=True))
        return [dx2, dx2 * me[5:6]], [_colsum(err * err), _colsum(dyy * xn), _colsum(dx2 * ft)]

    dx2, dfb, loss_cols, dfg, dg2 = _rowwise(
        f_final, nlat, tr, [(x1, D, 0, None), (ff, D, 0, None), (loss_target[0], D, 0, None)], [fg, mod_e],
        [(D, F32), (D, BF16)], [D, D, D], "final_loss")

    def dswiglu_epi(tm, tn):
        blk = pl.BlockSpec((2, tm, tn), lambda m, n, k: (0, m, n))

        def fn(d, ins):
            a, uu = ins[0][0].astype(F32), ins[0][1].astype(F32)
            return [(d * uu * _dsilu(a), d * _silu(a))]

        return dict(ins=[(au3, blk)], outs=[(jax.ShapeDtypeStruct((2, S, fh), BF16), blk)], fn=fn)

    dau3, = _mm_nt(dfb, wfo1, BF16, "mm_d_sw", epi=dswiglu_epi, tn_target=512)
    g_wfo = _mm_tn(swt, dfb, 1, "mm_dw_ffn_out", a_is_t=True).reshape(4, 2, fh // 8, D)
    dh2 = _mm_nt(dau3, wi3, F32, "mm_d_h2")
    g_wi = _mm_tn(h2t, dau3, 4, "mm_dw_ffn_in", a_is_t=True)

    def f_ln2_bwd(i, rv, vv):
        dh, x1t, dx2t, zt = rv
        g, me = vv
        r2 = _rms(x1t)
        xn = x1t * r2
        dxn = dh * g * (1.0 + me[4:5])
        dx1 = dx2t + r2 * (dxn - xn * jnp.mean(dxn * xn, axis=-1, keepdims=True))
        return ([dx1, dx1 * me[2:3]],
                [_colsum(dh), _colsum(dh * xn * g), _colsum(dh * xn * (1.0 + me[4:5])), _colsum(dx1 * zt)])

    dx1, dzb, dsh2, dsc2, dn2g, dg1 = _rowwise(
        f_ln2_bwd, nlat, tr, [(dh2, D, 0, None), (x1, D, 0, None), (dx2, D, 0, None), (z, D, 0, None)],
        [norm2_g, mod_e], [(D, F32), (D, BF16)], [D, D, D, D], "ln2_bwd")

    g_wout = _mm_tn(yv, dzb, 1, "mm_dw_out").reshape(4, 2, D // 8, D)

    def dmerge_epi(tm, tn):
        blk = pl.BlockSpec((tm, tn), lambda m, n, k: (m, n))

        def gate(cb):
            return pl.BlockSpec((tm, tn), lambda m, n, k: (m, cb * (D // tn) + n))

        def fn(d, ins):
            ya_, yb_, ga, gb = ins
            sa, sb_ = _sigmoid(ga), _sigmoid(gb)
            return [d * sa, d * sb_, d * ya_ * sa * (1.0 - sa), d * yb_ * sb_ * (1.0 - sb_)]

        return dict(ins=[(ya, blk), (yb, blk), (u, gate(ga_cb)), (u, gate(gb_cb))],
                    outs=[(jax.ShapeDtypeStruct((S, D), BF16), blk)] * 4, fn=fn)

    dya, dyb, dga, dgb = _mm_nt(dzb, wout1, BF16, "mm_d_y", epi=dmerge_epi, tn_target=512)
    d_ona = _mm_nt(dya, wpa3, BF16, "mm_d_ona")
    d_ob = _mm_nt(dyb, wpb3, F32, "mm_d_ob")
    g_wpa = _mm_tn(o_na, dya, 4, "mm_dw_pa")
    g_wpb = _mm_tn(ob, dyb, 4, "mm_dw_pb")

    def f_dreadout(i, rv, vv):
        d, of, ob_, hog = rv
        g, = vv
        o = of + ob_
        on = o * heads_rms(o)
        t = d * _silu(hog) * g
        mt = jnp.concatenate([jnp.broadcast_to(jnp.mean((t * on)[:, h * HG_DIM:(h + 1) * HG_DIM], axis=-1,
                                                        keepdims=True), (o.shape[0], HG_DIM))
                              for h in range(HG_HEADS)], axis=1)
        do_ = heads_rms(o) * (t - on * mt)
        return [do_, d * on * g * _dsilu(hog)], [_colsum(d * _silu(hog) * on)]

    do_hg, dhog, dhgn = _rowwise(
        f_dreadout, nlat, tr, [(d_ob, hgf, 0, None), (o_f, hgf, 0, None), (o_b, hgf, 0, None),
                               (u, hgf, hog_cb, None)], [hgn], [(hgf, F32), (hgf, BF16)], [hgf], "hg_readout_bwd")

    (dq_f, dz_f, dv_f, dlb_f), (dq_b, dz_b, dv_b, dlb_b) = _hg_both(
        [_hg_bwd(u, lbb[0], st_f, do_hg, S, L, naw, hgf, False), _hg_bwd(u, lbb[1], st_b, do_hg, S, L, naw, hgf, True)],
        "hg_bwd")
    dq_na, dk_na, dv_na, dbias = _na_bwd(qs, kb, vb, bias, d_ona, o_na, lse, S, L)

    ta = _tile(L, 128)
    nla, naa = S // ta, T // ta
    lat = lambda i: jnp.minimum(i, nla - 1)

    def f_assemble(i, rv, vv):
        dqn, dk, dv, dqf, dqb, dzf, dzb_, dvf, dvb, dho, dga_, dgb_ = rv
        keep = (i < nla).astype(F32)
        f32 = lambda t: t.astype(F32)
        return [jnp.concatenate([dqn * (scale * keep), dk, dv, f32(dqf) + f32(dqb), f32(dzf), f32(dzb_),
                                 f32(dvf) + f32(dvb),
                                 dho.astype(F32) * keep, dga_.astype(F32) * keep, dgb_.astype(F32) * keep],
                                axis=1)], []

    du, = _rowwise(
        f_assemble, naa, ta,
        [(dq_na, naw, 0, lat), (dk_na, naw, 0, None), (dv_na, naw, 0, None), (dq_f, hgf, 0, None),
         (dq_b, hgf, 0, None), (dz_f, hgf, 0, None), (dz_b, hgf, 0, None), (dv_f, hgf, 0, None),
         (dv_b, hgf, 0, None), (dhog, hgf, 0, lat), (dga, D, 0, lat), (dgb, D, 0, lat)],
        [], [(inw, BF16)], [], "assemble_du")

    def add_half(g, land, nm):
        return _add_own_half(g, land, "rs_add_" + nm)

    early = [g_wpa, g_wpb, g_wout, g_wi, g_wfo]
    g_win_other, lands = _mm_tn_half(hbt, du, 4, True, "mm_dw_in_other", carry=_exchange_swap_other_half(early))
    parts = [add_half(g, l, nm) for g, l, nm in zip(early, lands, names[1:])]
    g_win_mine, landed = _mm_tn_half(hbt, du, 4, False, "mm_dw_in_mine", carry=_exchange_join(
        _exchange_scatter(parts[3:4]), _exchange_swap_other_half([g_win_other])))
    piece_wi = landed[0]
    parts = [add_half(g_win_mine, landed[1], names[0])] + parts
    dh, landed = _mm_nt(du, win3, F32, "mm_d_h", carry=_exchange_scatter(parts[:4] + parts[5:]))
    pieces = landed[:4] + [piece_wi] + landed[4:]
    halves = [_sum_pieces(p, l, "rs_sum_" + nm) for p, l, nm in zip(parts, pieces, names)]
    g_win, g_wpa, g_wpb, g_wout, g_wi, g_wfo = [
        f.reshape(2 * f.shape[1], f.shape[2])
        for f in _exchange_call(_exchange_swap_result(halves), "rs_swap_result_half")]

    def f_ln1_bwd(i, rv, vv):
        dht, xt, dx1t = rv
        g, me = vv
        r1 = _rms(xt)
        xn = xt * r1
        dxn = dht * g * (1.0 + me[1:2])
        dx = dx1t + r1 * (dxn - xn * jnp.mean(dxn * xn, axis=-1, keepdims=True))
        return [dx], [_colsum(dht), _colsum(dht * xn * g), _colsum(dht * xn * (1.0 + me[1:2]))]

    grad_x, dsh1, dsc1, dn1g_l = _rowwise(
        f_ln1_bwd, nlat, tr, [(dh, D, 0, None), (x2d, D, 0, None), (dx1, D, 0, None)], [norm1_g, mod_e],
        [(D, F32)], [D, D, D], "ln1_bwd")

    def f_ln1_bwd_ctx(i, rv, vv):
        dht, xt = rv
        g, mc = vv
        xn = xt * _rms(xt)
        return [], [_colsum(dht), _colsum(dht * xn * g), _colsum(dht * xn * (1.0 + mc[1:2]))]

    ctx_rows = lambda i: i + nlat
    dsh1c, dsc1c, dn1g_c = _rowwise(
        f_ln1_bwd_ctx, nall - nlat, tr, [(dh, D, 0, ctx_rows), (ctx2d, D, 0, None)], [norm1_g, mod_c],
        [], [D, D, D], "ln1_bwd_ctx")

    drpb = _bias_tables_transpose(dbias, rows).reshape(1, -1)
    nrp = -(-drpb.shape[1] // D)
    drpb_rows = jnp.pad(drpb, ((0, 0), (0, nrp * D - drpb.shape[1]))).reshape(nrp, D)
    dlb = jnp.concatenate([dlb_f, dlb_b], axis=1)
    dhg = jnp.sum(dhgn.reshape(HG_HEADS, HG_DIM), axis=0, keepdims=True)

    def wide(v):
        return jnp.pad(v, ((0, 0), (0, D - v.shape[1])))

    pack_rows = [loss_cols, dfg, dn2g, dn1g_l + dn1g_c, dsh1, dsc1, dg1, dsh2, dsc2, dg2, dsh1c, dsc1c,
                 wide(dhg), wide(dlb), drpb_rows]
    pack = jnp.concatenate(pack_rows, axis=0)
    npk = -(-pack.shape[0] // 8) * 8
    pack = jnp.pad(pack, ((0, npk - pack.shape[0]), (0, 0)))
    gp = _all_gather8(pack).reshape(8, npk, D)
    tot = _sum8(gp, "sum_small_grads")

    loss = (0.5 / D) * jnp.sum(tot[0])
    grad_final_g = tot[1]
    grad_norm2_g = tot[2:3]
    grad_norm1_g = tot[3:4]
    grad_hg_norm_g = tot[12:13, :HG_DIM]
    dlb_tot = tot[13, :2 * hgf].reshape(2, hgf)
    grad_na_rpb = tot[14:14 + nrp].reshape(-1)[:drpb.shape[1]].reshape(na_rpb.shape)
    dlog = jnp.stack([dlb_tot * p_lb[0] * (1.0 - p_lb[0]), -dlb_tot * p_lb[0] * p_lb[1]], axis=0)
    grad_hg_lb = lax.dynamic_slice(dlog, (0, 0, sidx * fs), (2, 2, fs))

    dmod_all = gp[:, 4:10].reshape(8, N_MOD * D)
    dmod_ctx = jnp.concatenate([tot[10], tot[11], jnp.zeros((4 * D,), F32)])[None]
    dm16 = jnp.concatenate([dmod_all, dmod_ctx, jnp.zeros((7, N_MOD * D), F32)], axis=0)
    grad_b_ada = jnp.sum(dm16, axis=0, keepdims=True)
    dm_sh = lax.dynamic_slice(dm16, (0, sidx * ads), (16, ads))
    g_wada, dcin = _ada_bwd(cin, w_ada[0], dm_sh)
    gc = _all_gather8(dcin[8:16]).reshape(8, 8, D)
    grad_c_ctx = (gc[0, 0] + gc[2, 0] + gc[4, 0] + gc[6, 0]) * _dsilu(c_ctx)

    grads = {
        "c_ctx": grad_c_ctx, "w_ada": g_wada[None], "b_ada": grad_b_ada, "norm1_g": grad_norm1_g,
        "w_in": g_win[None], "na_rpb": grad_na_rpb, "hg_lb_logits": grad_hg_lb, "hg_norm_g": grad_hg_norm_g,
        "w_pa": g_wpa[None], "w_pb": g_wpb[None], "w_out": g_wout[None], "norm2_g": grad_norm2_g,
        "w_ffn_in": g_wi[None], "w_ffn_out": g_wfo[None], "final_g": grad_final_g,
    }
    weights = {
        "c_ctx": (c_ctx, m_c_ctx, v_c_ctx), "w_ada": (w_ada, m_w_ada, v_w_ada), "b_ada": (b_ada, m_b_ada, v_b_ada),
        "norm1_g": (norm1_g, m_norm1_g, v_norm1_g), "w_in": (w_in, m_w_in, v_w_in),
        "na_rpb": (na_rpb, m_na_rpb, v_na_rpb), "hg_lb_logits": (hg_lb_logits, m_hg_lb_logits, v_hg_lb_logits),
        "hg_norm_g": (hg_norm_g, m_hg_norm_g, v_hg_norm_g), "w_pa": (w_pa, m_w_pa, v_w_pa),
        "w_pb": (w_pb, m_w_pb, v_w_pb), "w_out": (w_out, m_w_out, v_w_out),
        "norm2_g": (norm2_g, m_norm2_g, v_norm2_g), "w_ffn_in": (w_ffn_in, m_w_ffn_in, v_w_ffn_in),
        "w_ffn_out": (w_ffn_out, m_w_ffn_out, v_w_ffn_out), "final_g": (final_g, m_final_g, v_final_g),
    }
    order = list(weights)
    deltas, new_ms, new_vs = [], [], []
    for nm in order:
        w, m, v = weights[nm]
        g = grads[nm].reshape(w.shape)
        grads[nm] = g
        if w.ndim == 3 and w.shape[0] == 1:
            d_, m_, v_ = _adamw(w[0], g[0], m[0], v[0], "adamw_" + nm)
            d_, m_, v_ = d_[None], m_[None], v_[None]
        else:
            d_, m_, v_ = _adamw(w, g, m, v, "adamw_" + nm)
        deltas.append(d_)
        new_ms.append(m_)
        new_vs.append(v_)

    return (loss, grad_x[None], *[grads[nm] for nm in order], *deltas, *new_ms, *new_vs)
```

```python
import numpy as np

import jax
import jax.numpy as jnp
from jax import lax
from jax.experimental import pallas as pl
from jax.experimental.pallas import tpu as pltpu

F32 = jnp.float32
BF16 = jnp.bfloat16

GRID_W = 64
WIN_H = 8
WIN_W = 16
NA_HEADS = 16
NA_HEAD_DIM = 64
HG_HEADS = 8
HG_DIM = 128
HG_CHUNK = 64
N_MOD = 6
EPS = 1e-6
ADAM_LR = 0.001
ADAM_B1 = 0.9
ADAM_B2 = 0.999
ADAM_EPS = 1e-08
ADAM_WD = 0.01
ADAM_STEP = 10

LANES = 128
NA_ROWS_PER_STEP = 16
VMEM_LIMIT = 56 * 1024 * 1024
MASK_VALUE = -1e30
EXP_CLAMP = 80.0
MESH_ID = pl.DeviceIdType.MESH
HI = lax.Precision.HIGHEST


def _tile(dim, target, mult=LANES):
    best = None
    t = mult
    while t <= min(dim, target):
        if dim % t == 0:
            best = t
        t += mult
    assert best is not None, (dim, target, mult)
    return best


def _params(sem):
    return pltpu.CompilerParams(dimension_semantics=sem, vmem_limit_bytes=VMEM_LIMIT)


def _dot(a, b, precision=None):
    return jnp.dot(a, b, preferred_element_type=F32, precision=precision)


def _dot_nt(a, b, precision=None):
    return lax.dot_general(a, b, (((1,), (1,)), ((), ())), preferred_element_type=F32, precision=precision)


def _dot_tn(a, b, precision=None):
    return lax.dot_general(a, b, (((0,), (0,)), ((), ())), preferred_element_type=F32, precision=precision)


def _split2(v):
    hi = v.astype(BF16)
    return hi, (v - hi.astype(F32)).astype(BF16)


def _dot_x3(dot, a2, b2):
    return dot(a2[0], b2[0]) + (dot(a2[0], b2[1]) + dot(a2[1], b2[0]))


def _sigmoid(v):
    return 1.0 / (1.0 + jnp.exp(-v))


def _mm_call(dot, operands, grid, in_specs, out_spec, out_shape, acc_shape, name, carry=None, epi=None):
    nk = grid[2]
    nci = 0 if carry is None else len(carry["ins"])
    nco = 0 if carry is None else len(carry["outs"])
    nei = 0 if epi is None else len(epi["ins"])
    neo = 1 if epi is None else len(epi["outs"])

    def body(*refs):
        a_ref, b_ref = refs[:2]
        ein = refs[2:2 + nei]
        cin = refs[2 + nei:2 + nei + nci]
        outs = refs[2 + nei + nci:2 + nei + nci + neo]
        cout = refs[2 + nei + nci + neo:2 + nei + nci + neo + nco]
        acc = refs[2 + nei + nci + neo + nco]
        sems = refs[3 + nei + nci + neo + nco:]
        m, n, k = pl.program_id(0), pl.program_id(1), pl.program_id(2)

        if carry is not None:
            @pl.when((m == 0) & (n == 0) & (k == 0))
            def _():
                carry["start"](cin, cout, *sems)

        @pl.when(k == 0)
        def _():
            acc[...] = jnp.zeros_like(acc)

        if carry is not None and "mid" in carry:
            @pl.when((m == grid[0] - 1) & (n == 0) & (k == 0))
            def _():
                carry["mid"](cin, cout, *sems)

        acc[...] += dot(a_ref[...], b_ref[...])

        @pl.when(k == nk - 1)
        def _():
            vals = [acc[...]] if epi is None else epi["fn"](acc[...], [r[...] for r in ein])
            for r, v in zip(outs, vals):
                if isinstance(v, tuple):
                    for i, vi in enumerate(v):
                        r[i] = vi.astype(r.dtype)
                else:
                    r[...] = v.astype(r.dtype)

        if carry is not None:
            @pl.when((m == grid[0] - 1) & (n == grid[1] - 1) & (k == nk - 1))
            def _():
                carry["finish"](cin, cout, *sems)

    any_spec = pl.BlockSpec(memory_space=pl.ANY)
    scratch = [pltpu.VMEM(acc_shape, F32)]
    extra = {}
    if carry is not None:
        scratch += [pltpu.SemaphoreType.DMA((carry["nsem"],)), pltpu.SemaphoreType.DMA((carry["nsem"],))]
        extra["input_output_aliases"] = {2 + nei + i: neo + j for i, j in carry["alias"].items()}
    sem = ("arbitrary",) * 3 if carry is not None else ("parallel", "parallel", "arbitrary")
    main_outs = [(out_shape, out_spec)] if epi is None else list(epi["outs"])
    res = pl.pallas_call(
        body, name=name, grid=grid,
        in_specs=list(in_specs) + ([] if epi is None else [sp for _, sp in epi["ins"]]) + [any_spec] * nci,
        out_specs=[sp for _, sp in main_outs] + [any_spec] * nco,
        out_shape=[sh for sh, _ in main_outs] + ([] if carry is None else list(carry["outs"])),
        scratch_shapes=scratch, compiler_params=_params(sem), **extra,
    )(*operands, *([] if epi is None else [ar for ar, _ in epi["ins"]]), *([] if carry is None else carry["ins"]))
    res = list(res)
    main = res[0] if epi is None else res[:neo]
    return main if carry is None else (main, res[neo:])


def _mm_nn(a, b3, out_dtype, name, carry=None):
    M, K = a.shape
    nsh, _, Ns = b3.shape
    tm, tn, tk = _tile(M, 1024), _tile(Ns, 1408), _tile(K, 2816)
    tps, nk = Ns // tn, K // tk
    return _mm_call(
        _dot, (a, b3), (M // tm, nsh * tps, nk),
        [pl.BlockSpec((tm, tk), lambda m, n, k: (m, k)),
         pl.BlockSpec((None, tk, tn), lambda m, n, k: (n // tps, k, n % tps))],
        pl.BlockSpec((tm, tn), lambda m, n, k: (m, n)),
        jax.ShapeDtypeStruct((M, nsh * Ns), out_dtype), (tm, tn), name, carry)


def _mm_nt(a, b3, out_dtype, name, carry=None, epi=None, tn_target=1408):
    a3 = a if a.ndim == 3 else a[None]
    na, M, Ka = a3.shape
    nsh, Kw, Ns = b3.shape
    assert na * Ka == nsh * Ns
    tm, tn, tk = _tile(M, 1024), _tile(Kw, tn_target), _tile(int(np.gcd(Ka, Ns)), 3072)
    kpa, kps = Ka // tk, Ns // tk
    return _mm_call(
        _dot_nt, (a3, b3), (M // tm, Kw // tn, nsh * kps),
        [pl.BlockSpec((None, tm, tk), lambda m, n, k: (k // kpa, m, k % kpa)),
         pl.BlockSpec((None, tn, tk), lambda m, n, k: (k // kps, n, k % kps))],
        pl.BlockSpec((tm, tn), lambda m, n, k: (m, n)),
        jax.ShapeDtypeStruct((M, Kw), out_dtype), (tm, tn), name, carry,
        None if epi is None else epi(tm, tn))


def _mm_tn(a, g, nsh, name, carry=None, a_is_t=False):
    Tk, M = a.shape[::-1] if a_is_t else a.shape
    g3 = g if g.ndim == 3 else g[None]
    ng, _, Ng = g3.shape
    Ns = ng * Ng // nsh
    tm, tn, tk = _tile(M // 2, 1408), _tile(int(np.gcd(Ng, Ns)), 1408), _tile(Tk, 2048)
    mh, tps, tpg, nk = (M // 2) // tm, Ns // tn, Ng // tn, Tk // tk
    return _mm_call(
        _dot if a_is_t else _dot_tn, (a, g3), (M // tm, nsh * tps, nk),
        [pl.BlockSpec((tm, tk), lambda m, n, k: (m, k)) if a_is_t else pl.BlockSpec((tk, tm), lambda m, n, k: (k, m)),
         pl.BlockSpec((None, tk, tn), lambda m, n, k: (n // tpg, k, n % tpg))],
        pl.BlockSpec((None, None, tm, tn), lambda m, n, k: (n // tps, m // mh, m % mh, n % tps)),
        jax.ShapeDtypeStruct((nsh, 2, M // 2, Ns), F32), (tm, tn), name, carry)


def _mm_tn_half(at, g, nsh, other, name, carry=None):
    M, Tk = at.shape
    Ns = g.shape[1] // nsh
    tm, tn, tk = _tile(M // 2, 1408), _tile(Ns, 1408), _tile(Tk, 2048)
    mh, tps, nk = (M // 2) // tm, Ns // tn, Tk // tk

    def half():
        c = lax.axis_index("c")
        return 1 - c if other else c

    return _mm_call(
        _dot, (at, g), (mh, nsh * tps, nk),
        [pl.BlockSpec((tm, tk), lambda m, n, k: (half() * mh + m, k)),
         pl.BlockSpec((tk, tn), lambda m, n, k: (k, n))],
        pl.BlockSpec((None, tm, tn), lambda m, n, k: (n // tps, m, n % tps)),
        jax.ShapeDtypeStruct((nsh, M // 2, Ns), F32), (tm, tn), name, carry)


def _ffn_in_fused(h2, wi3, name):
    M, K = h2.shape
    _, _, Ns = wi3.shape
    fh = 2 * Ns
    tm, tn = _tile(M, 512), _tile(Ns, 1408)
    tps = Ns // tn

    def body(h_ref, ba_ref, bu_ref, au_ref, sw_ref, swt_ref):
        h = h_ref[...]
        a, u = _dot(h, ba_ref[...]), _dot(h, bu_ref[...])
        au_ref[0] = a.astype(au_ref.dtype)
        au_ref[1] = u.astype(au_ref.dtype)
        sw = (_silu(a) * u).astype(sw_ref.dtype)
        sw_ref[...] = sw
        swt_ref[...] = sw.T

    return pl.pallas_call(
        body, name=name, grid=(M // tm, fh // tn),
        in_specs=[pl.BlockSpec((tm, K), lambda m, n: (m, 0)),
                  pl.BlockSpec((None, K, tn), lambda m, n: (n // tps, 0, n % tps)),
                  pl.BlockSpec((None, K, tn), lambda m, n: (2 + n // tps, 0, n % tps))],
        out_specs=[pl.BlockSpec((2, tm, tn), lambda m, n: (0, m, n)), pl.BlockSpec((tm, tn), lambda m, n: (m, n)),
                   pl.BlockSpec((tn, tm), lambda m, n: (n, m))],
        out_shape=[jax.ShapeDtypeStruct((2, M, fh), BF16), jax.ShapeDtypeStruct((M, fh), BF16),
                   jax.ShapeDtypeStruct((fh, M), BF16)],
        compiler_params=_params(("parallel", "parallel")),
    )(h2, wi3, wi3)


def _rowwise(fn, nblk, tm, rins, vins, routs, accs, name):
    nr, nv, no, na = len(rins), len(vins), len(routs), len(accs)

    def body(*refs):
        i = pl.program_id(0)
        outs, accv = fn(i, [r[...] for r in refs[:nr]], [r[...] for r in refs[nr:nr + nv]])
        for r, v, spec in zip(refs[nr + nv:nr + nv + no], outs, routs):
            v = v.astype(r.dtype)
            r[...] = v.T if len(spec) == 3 else v
        arefs = refs[nr + nv + no:]
        if na:
            @pl.when(i == 0)
            def _():
                for a in arefs:
                    a[...] = jnp.zeros_like(a)

            for a, v in zip(arefs, accv):
                a[...] += v

    def row_spec(w, cb, rm):
        if rm is None:
            return pl.BlockSpec((tm, w), lambda i: (i, cb))
        return pl.BlockSpec((tm, w), lambda i: (rm(i), cb))

    in_specs = [row_spec(w, cb, rm) for (_, w, cb, rm) in rins]
    in_specs += [pl.BlockSpec(v.shape, lambda i: (0, 0)) for v in vins]
    def out_of(spec):
        w, dt = spec[:2]
        if len(spec) == 3:
            return pl.BlockSpec((w, tm), lambda i: (0, i)), jax.ShapeDtypeStruct((w, nblk * tm), dt)
        return pl.BlockSpec((tm, w), lambda i: (i, 0)), jax.ShapeDtypeStruct((nblk * tm, w), dt)

    out_specs = [out_of(sp)[0] for sp in routs] + [pl.BlockSpec((1, w), lambda i: (0, 0)) for w in accs]
    out_shape = [out_of(sp)[1] for sp in routs] + [jax.ShapeDtypeStruct((1, w), F32) for w in accs]
    res = pl.pallas_call(
        body, name=name, grid=(nblk,), in_specs=in_specs, out_specs=out_specs, out_shape=out_shape,
        compiler_params=_params(("arbitrary",)),
    )(*[r[0] for r in rins], *vins)
    return list(res)


def _colsum(v):
    return jnp.sum(v, axis=0, keepdims=True)


def _rms(v):
    return lax.rsqrt(jnp.mean(v * v, axis=-1, keepdims=True) + EPS)


def _all_gather8(xs):
    m_per, n = xs.shape

    def body(x_ref, out_ref, send_sems, recv_sems, local_sem):
        x, y, c = lax.axis_index("x"), lax.axis_index("y"), lax.axis_index("c")
        me, sibling = (x, y, c), (x, y, 1 - c)
        chips = [(1 - x, y), (x, 1 - y), (1 - x, 1 - y)]

        def rows(px, py, pc):
            return out_ref.at[pl.ds((4 * px + 2 * py + pc) * m_per, m_per), :]

        def copy(k, block, to, src=None):
            return pltpu.make_async_remote_copy(
                src_ref=rows(*block) if src is None else src, dst_ref=rows(*block),
                send_sem=send_sems.at[k], recv_sem=recv_sems.at[k], device_id=to, device_id_type=MESH_ID)

        mine = pltpu.make_async_copy(x_ref, rows(*me), local_sem)
        mine.start()
        first = [copy(0, me, sibling, src=x_ref)]
        first += [copy(1 + j, me, (*chip, c), src=x_ref) for j, chip in enumerate(chips)]
        for cp in first:
            cp.start()
        passed = [copy(4 + j, (*chip, c), sibling) for j, chip in enumerate(chips)]
        for j, chip in enumerate(chips):
            copy(1 + j, (*chip, c), me).wait_recv()
            passed[j].start()
        copy(0, sibling, me).wait_recv()
        for j, chip in enumerate(chips):
            copy(4 + j, (*chip, 1 - c), me).wait_recv()
        for cp in first + passed:
            cp.wait_send()
        mine.wait()

    return pl.pallas_call(
        body, name="all_gather8_%dx%d" % (m_per, n),
        out_shape=jax.ShapeDtypeStruct((8 * m_per, n), xs.dtype),
        in_specs=[pl.BlockSpec(memory_space=pltpu.VMEM)],
        out_specs=pl.BlockSpec(memory_space=pltpu.VMEM),
        scratch_shapes=[pltpu.SemaphoreType.DMA((7,)), pltpu.SemaphoreType.DMA((7,)), pltpu.SemaphoreType.DMA],
    )(xs)


def _mesh_pos():
    x, y, c = lax.axis_index("x"), lax.axis_index("y"), lax.axis_index("c")
    chips = [(1 - x, y), (x, 1 - y), (1 - x, 1 - y)]
    return x, y, c, chips


def _my_shard():
    return 2 * lax.axis_index("x") + lax.axis_index("y")


def _cast_place(w, name):
    r, cw = w.shape
    rh = r // 2
    tm = _tile(rh, max(16, (1 << 20) // (4 * cw)), 16)
    nt = rh // tm

    def body(w_ref, o_ref):
        o_ref[...] = w_ref[...].astype(o_ref.dtype)

    return pl.pallas_call(
        body, name=name, grid=(2, nt),
        in_specs=[pl.BlockSpec((tm, cw), lambda h, i: (h * nt + i, 0))],
        out_specs=pl.BlockSpec((None, None, tm, cw), lambda h, i: (_my_shard(), h, i, 0)),
        out_shape=jax.ShapeDtypeStruct((4, 2, rh, cw), BF16),
        compiler_params=_params(("parallel", "parallel")),
    )(w)


def _exchange_gather(bufs):
    n = len(bufs)

    def copies(out, send_sems, recv_sems, base):
        def copy(i, k, shard, half, to):
            dst = out[i].at[shard, half]
            return pltpu.make_async_remote_copy(
                src_ref=dst, dst_ref=dst, send_sem=send_sems.at[base + 6 * i + k],
                recv_sem=recv_sems.at[base + 6 * i + k], device_id=to, device_id_type=MESH_ID)
        return copy

    def first(copy):
        x, y, c, chips = _mesh_pos()
        return [copy(i, j, 2 * x + y, c, (*chip, c)) for i in range(n) for j, chip in enumerate(chips)]

    def start(cin, out, send_sems, recv_sems, base=0):
        for cp in first(copies(out, send_sems, recv_sems, base)):
            cp.start()

    def passed(copy):
        x, y, c, chips = _mesh_pos()
        return [copy(i, 3 + j, 2 * chip[0] + chip[1], c, (x, y, 1 - c)) for j, chip in enumerate(chips) for i in range(n)]

    def mid(cin, out, send_sems, recv_sems, base=0):
        copy = copies(out, send_sems, recv_sems, base)
        x, y, c, chips = _mesh_pos()
        for j, chip in enumerate(chips):
            for i in range(n):
                copy(i, j, 2 * chip[0] + chip[1], c, (x, y, c)).wait_recv()
        for cp in passed(copy):
            cp.start()

    def finish(cin, out, send_sems, recv_sems, base=0):
        copy = copies(out, send_sems, recv_sems, base)
        x, y, c, chips = _mesh_pos()
        for j, chip in enumerate(chips):
            for i in range(n):
                copy(i, 3 + j, 2 * chip[0] + chip[1], 1 - c, (x, y, c)).wait_recv()
        for cp in first(copy) + passed(copy):
            cp.wait_send()

    return dict(ins=list(bufs), outs=[jax.ShapeDtypeStruct(b.shape, b.dtype) for b in bufs],
                alias={i: i for i in range(n)}, nsem=6 * n, start=start, mid=mid, finish=finish)


def _exchange_gather_via_neighbours(bufs):
    n = len(bufs)

    def geometry():
        x, y, c, _ = _mesh_pos()
        xn, yn = (1 - x, y), (x, 1 - y)
        shard = dict(me=2 * x + y, xn=2 * (1 - x) + y, yn=2 * x + (1 - y), dg=2 * (1 - x) + (1 - y))
        return x, y, c, xn, yn, shard

    def copy(out, sems, base, i, k, dst, to):
        return pltpu.make_async_remote_copy(
            src_ref=dst, dst_ref=dst, send_sem=sems[0].at[base + 7 * i + k], recv_sem=sems[1].at[base + 7 * i + k],
            device_id=to, device_id_type=MESH_ID)

    def quarter(out, i, shard, half, q):
        rq = bufs[i].shape[2] // 2
        return out[i].at[shard, half, pl.ds(q * rq, rq)]

    def plan(out, sems, base):
        x, y, c, xn, yn, sh = geometry()
        me, sib = (x, y, c), (x, y, 1 - c)
        p = dict(first=[], got_x=[], got_y=[], relay=[], got_dg=[], to_sib=[], from_sib=[])
        for i in range(n):
            mine = out[i].at[sh["me"], c]
            p["first"] += [copy(out, sems, base, i, 0, mine, (*xn, c)), copy(out, sems, base, i, 1, mine, (*yn, c))]
            p["got_x"].append(copy(out, sems, base, i, 0, out[i].at[sh["xn"], c], me))
            p["got_y"].append(copy(out, sems, base, i, 1, out[i].at[sh["yn"], c], me))
            p["relay"] += [copy(out, sems, base, i, 2, quarter(out, i, sh["xn"], c, 0), (*yn, c)),
                           copy(out, sems, base, i, 3, quarter(out, i, sh["yn"], c, 1), (*xn, c))]
            p["got_dg"] += [copy(out, sems, base, i, 2, quarter(out, i, sh["dg"], c, 0), me),
                            copy(out, sems, base, i, 3, quarter(out, i, sh["dg"], c, 1), me)]
            for k, who in enumerate(("xn", "yn", "dg")):
                p["to_sib"].append(copy(out, sems, base, i, 4 + k, out[i].at[sh[who], c], sib))
                p["from_sib"].append(copy(out, sems, base, i, 4 + k, out[i].at[sh[who], 1 - c], me))
        return p

    def start(cin, out, send_sems, recv_sems, base=0):
        for cp in plan(out, (send_sems, recv_sems), base)["first"]:
            cp.start()

    def mid(cin, out, send_sems, recv_sems, base=0):
        p = plan(out, (send_sems, recv_sems), base)
        for cp in p["got_x"] + p["got_y"]:
            cp.wait_recv()
        for cp in p["relay"]:
            cp.start()
        for cp in p["got_dg"]:
            cp.wait_recv()
        for cp in p["to_sib"]:
            cp.start()

    def finish(cin, out, send_sems, recv_sems, base=0):
        p = plan(out, (send_sems, recv_sems), base)
        for cp in p["from_sib"]:
            cp.wait_recv()
        for cp in p["first"] + p["relay"] + p["to_sib"]:
            cp.wait_send()

    return dict(ins=list(bufs), outs=[jax.ShapeDtypeStruct(b.shape, b.dtype) for b in bufs],
                alias={i: i for i in range(n)}, nsem=7 * n, start=start, mid=mid, finish=finish)


def _exchange_join(a, b):
    nai, nao = len(a["ins"]), len(a["outs"])

    def start(cin, cout, send_sems, recv_sems, base=0):
        a["start"](cin[:nai], cout[:nao], send_sems, recv_sems, base)
        b["start"](cin[nai:], cout[nao:], send_sems, recv_sems, base + a["nsem"])

    def mid(cin, cout, send_sems, recv_sems, base=0):
        if "mid" in a:
            a["mid"](cin[:nai], cout[:nao], send_sems, recv_sems, base)
        if "mid" in b:
            b["mid"](cin[nai:], cout[nao:], send_sems, recv_sems, base + a["nsem"])

    def finish(cin, cout, send_sems, recv_sems, base=0):
        a["finish"](cin[:nai], cout[:nao], send_sems, recv_sems, base)
        b["finish"](cin[nai:], cout[nao:], send_sems, recv_sems, base + a["nsem"])

    alias = dict(a["alias"])
    alias.update({nai + i: nao + j for i, j in b["alias"].items()})
    return dict(ins=a["ins"] + b["ins"], outs=a["outs"] + b["outs"], alias=alias, nsem=a["nsem"] + b["nsem"],
                start=start, mid=mid, finish=finish)


def _exchange_call(ex, name):
    nci, nco = len(ex["ins"]), len(ex["outs"])

    def body(*refs):
        cin, cout, sems = refs[:nci], refs[nci:nci + nco], refs[nci + nco:]
        ex["start"](cin, cout, *sems)
        if "mid" in ex:
            ex["mid"](cin, cout, *sems)
        ex["finish"](cin, cout, *sems)

    any_spec = pl.BlockSpec(memory_space=pl.ANY)
    return list(pl.pallas_call(
        body, name=name, out_shape=list(ex["outs"]), in_specs=[any_spec] * nci, out_specs=[any_spec] * nco,
        input_output_aliases=dict(ex["alias"]),
        scratch_shapes=[pltpu.SemaphoreType.DMA((ex["nsem"],)), pltpu.SemaphoreType.DMA((ex["nsem"],))],
    )(*ex["ins"]))


def _exchange_swap_other_half(gs):
    n = len(gs)

    def copies(g, land, send_sems, recv_sems, base):
        x, y, c, _ = _mesh_pos()
        return [pltpu.make_async_remote_copy(
            src_ref=g[i].at[:, 1 - c] if len(gs[i].shape) == 4 else g[i], dst_ref=land[i],
            send_sem=send_sems.at[base + i],
            recv_sem=recv_sems.at[base + i], device_id=(x, y, 1 - c), device_id_type=MESH_ID) for i in range(n)]

    def start(g, land, send_sems, recv_sems, base=0):
        for cp in copies(g, land, send_sems, recv_sems, base):
            cp.start()

    def finish(g, land, send_sems, recv_sems, base=0):
        for cp in copies(g, land, send_sems, recv_sems, base):
            cp.wait()

    return dict(ins=list(gs), outs=[jax.ShapeDtypeStruct((4,) + g.shape[-2:], g.dtype) for g in gs],
                alias={}, nsem=n, start=start, finish=finish)


def _exchange_scatter(ps):
    n = len(ps)

    def copies(p, land, send_sems, recv_sems, base):
        x, y, c, chips = _mesh_pos()
        return [pltpu.make_async_remote_copy(
            src_ref=p[i].at[2 * chip[0] + chip[1]], dst_ref=land[i].at[j],
            send_sem=send_sems.at[base + 3 * i + j], recv_sem=recv_sems.at[base + 3 * i + j],
            device_id=(*chip, c), device_id_type=MESH_ID) for i in range(n) for j, chip in enumerate(chips)]

    def start(p, land, send_sems, recv_sems, base=0):
        for cp in copies(p, land, send_sems, recv_sems, base):
            cp.start()

    def finish(p, land, send_sems, recv_sems, base=0):
        for cp in copies(p, land, send_sems, recv_sems, base):
            cp.wait()

    return dict(ins=list(ps), outs=[jax.ShapeDtypeStruct((3,) + p.shape[1:], p.dtype) for p in ps],
                alias={}, nsem=3 * n, start=start, finish=finish)


def _exchange_swap_result(bufs):
    n = len(bufs)

    def copies(out, send_sems, recv_sems, base, half):
        x, y, c, _ = _mesh_pos()
        h = c if half == "mine" else 1 - c
        return [pltpu.make_async_remote_copy(
            src_ref=out[i].at[h], dst_ref=out[i].at[h], send_sem=send_sems.at[base + i],
            recv_sem=recv_sems.at[base + i], device_id=(x, y, 1 - c), device_id_type=MESH_ID) for i in range(n)]

    def start(cin, out, send_sems, recv_sems, base=0):
        for cp in copies(out, send_sems, recv_sems, base, "mine"):
            cp.start()

    def finish(cin, out, send_sems, recv_sems, base=0):
        for cp in copies(out, send_sems, recv_sems, base, "theirs"):
            cp.wait_recv()
        for cp in copies(out, send_sems, recv_sems, base, "mine"):
            cp.wait_send()

    return dict(ins=list(bufs), outs=[jax.ShapeDtypeStruct(b.shape, b.dtype) for b in bufs],
                alias={i: i for i in range(n)}, nsem=n, start=start, finish=finish)


def _add_own_half(g, land, name):
    rh, cw = g.shape[-2:]
    tm = _tile(rh, max(16, (1 << 20) // (4 * cw)), 16)

    def body(g_ref, l_ref, o_ref):
        o_ref[...] = (g_ref[...] + l_ref[...]).astype(o_ref.dtype)

    mine = (pl.BlockSpec((None, None, tm, cw), lambda s, i: (s, lax.axis_index("c"), i, 0)) if g.ndim == 4
            else pl.BlockSpec((None, tm, cw), lambda s, i: (s, i, 0)))
    return pl.pallas_call(
        body, name=name, grid=(4, rh // tm),
        in_specs=[mine,
                  pl.BlockSpec((None, tm, cw), lambda s, i: (s, i, 0))],
        out_specs=pl.BlockSpec((None, tm, cw), lambda s, i: (s, i, 0)),
        out_shape=jax.ShapeDtypeStruct((4, rh, cw), BF16),
        compiler_params=_params(("parallel", "parallel")),
    )(g, land)


def _sum_pieces(part, land, name):
    _, rh, cw = land.shape
    tm = _tile(rh, max(16, (1 << 20) // (4 * cw)), 16)

    def body(p_ref, l_ref, o_ref):
        v = l_ref[...].astype(F32)
        o_ref[...] = (p_ref[...].astype(F32) + v[0]) + (v[1] + v[2])

    return pl.pallas_call(
        body, name=name, grid=(rh // tm,),
        in_specs=[pl.BlockSpec((None, tm, cw), lambda i: (_my_shard(), i, 0)),
                  pl.BlockSpec((3, tm, cw), lambda i: (0, i, 0))],
        out_specs=pl.BlockSpec((None, tm, cw), lambda i: (lax.axis_index("c"), i, 0)),
        out_shape=jax.ShapeDtypeStruct((2, rh, cw), F32),
        compiler_params=_params(("parallel",)),
    )(part, land)


def _sum8(g, name):
    def body(g_ref, o_ref):
        acc = g_ref[0]
        for k in range(1, 8):
            acc = acc + g_ref[k]
        o_ref[...] = acc

    return pl.pallas_call(body, name=name, out_shape=jax.ShapeDtypeStruct(g.shape[1:], F32))(g)


def _silu(v):
    return v * _sigmoid(v)


def _dsilu(v):
    s = _sigmoid(v)
    return s * (1.0 + v * (1.0 - s))


def _ada_fwd(cin, w, b):
    d, ns = w.shape
    tn = _tile(ns, 512)

    def body(c_ref, w_ref, b_ref, o_ref):
        o_ref[...] = _dot(_silu(c_ref[...]), w_ref[...], HI) + b_ref[...]

    return pl.pallas_call(
        body, name="ada_fwd", grid=(ns // tn,),
        in_specs=[pl.BlockSpec(cin.shape, lambda n: (0, 0)), pl.BlockSpec((d, tn), lambda n: (0, n)),
                  pl.BlockSpec((1, tn), lambda n: (0, n))],
        out_specs=pl.BlockSpec((cin.shape[0], tn), lambda n: (0, n)),
        out_shape=jax.ShapeDtypeStruct((cin.shape[0], ns), F32),
        compiler_params=_params(("parallel",)),
    )(cin, w, b)


def _ada_bwd(cin, w, dm):
    d, ns = w.shape
    tn = _tile(ns, 512)

    def body(c_ref, w_ref, d_ref, dw_ref, dc_ref):
        n = pl.program_id(0)

        @pl.when(n == 0)
        def _():
            dc_ref[...] = jnp.zeros_like(dc_ref)

        dw_ref[...] = _dot_tn(_silu(c_ref[...]), d_ref[...], HI)
        dc_ref[...] += _dot_nt(d_ref[...], w_ref[...], HI)

    return pl.pallas_call(
        body, name="ada_bwd", grid=(ns // tn,),
        in_specs=[pl.BlockSpec(cin.shape, lambda n: (0, 0)), pl.BlockSpec((d, tn), lambda n: (0, n)),
                  pl.BlockSpec((cin.shape[0], tn), lambda n: (0, n))],
        out_specs=[pl.BlockSpec((d, tn), lambda n: (0, n)), pl.BlockSpec(cin.shape, lambda n: (0, 0))],
        out_shape=[jax.ShapeDtypeStruct((d, ns), F32), jax.ShapeDtypeStruct(cin.shape, F32)],
        compiler_params=_params(("arbitrary",)),
    )(cin, w, dm)


def _bias_tables(rpb, rows):
    kh = min(WIN_H, rows)
    fold, onehot, in_win = _bias_selectors(kh)
    t = jnp.einsum("hdc,dsj->hsjc", rpb, jnp.asarray(fold), precision=HI)
    t = jnp.einsum("hsjc,cqk->hsqjk", t, jnp.asarray(onehot), precision=HI)
    t = jnp.where(jnp.asarray(in_win)[None, None, :, None, :], t, MASK_VALUE)
    return t.reshape(rpb.shape[0], kh, GRID_W, kh * GRID_W).astype(F32)


def _bias_selectors(kh):
    col = np.arange(GRID_W)
    col_start = np.clip(col - WIN_W // 2, 0, GRID_W - WIN_W)
    in_win = (col[None, :] >= col_start[:, None]) & (col[None, :] < col_start[:, None] + WIN_W)
    dc_idx = np.clip(col[None, :] - col[:, None], 1 - WIN_W, WIN_W - 1) + WIN_W - 1
    onehot = np.zeros((2 * WIN_W - 1, GRID_W, GRID_W), np.float32)
    qq, kk = np.nonzero(in_win)
    onehot[dc_idx[qq, kk], qq, kk] = 1.0
    fold = np.zeros((2 * WIN_H - 1, kh, kh), np.float32)
    for sh in range(kh):
        for j in range(kh):
            fold[j - sh + WIN_H - 1, sh, j] = 1.0
    return fold, onehot, in_win


def _mm_f32(a, b, name):
    M, K = a.shape
    N = b.shape[1]
    tm = _tile(M, 256, 8)

    def body(a_ref, b_ref, o_ref):
        o_ref[...] = _dot(a_ref[...], b_ref[...], HI)

    return pl.pallas_call(
        body, name=name, grid=(M // tm,),
        in_specs=[pl.BlockSpec((tm, K), lambda i: (i, 0)), pl.BlockSpec((K, N), lambda i: (0, 0))],
        out_specs=pl.BlockSpec((tm, N), lambda i: (i, 0)),
        out_shape=jax.ShapeDtypeStruct((M, N), F32),
        compiler_params=_params(("parallel",)),
    )(a, b)


def _bias_tables_transpose(dbias, rows):
    kh = min(WIN_H, rows)
    nh = dbias.shape[0]
    fold, onehot, _ = _bias_selectors(kh)
    ndc = onehot.shape[0]
    sel = np.zeros((GRID_W * GRID_W, LANES), np.float32)
    sel[:, :ndc] = onehot.reshape(ndc, -1).T
    x = dbias.reshape(nh, kh, GRID_W, kh, GRID_W).transpose(0, 1, 3, 2, 4).reshape(nh * kh * kh, GRID_W * GRID_W)
    z = _mm_f32(x, jnp.asarray(sel), "rpb_fold")[:, :ndc].reshape(nh, kh, kh, ndc)
    return jnp.einsum("dsj,hsjc->hdc", jnp.asarray(fold), z, precision=HI)


def _na_geometry(S):
    rows = S // GRID_W
    kh = min(WIN_H, rows)

    def row_start(r):
        return jnp.clip(r - kh // 2, 0, rows - kh)

    return rows, kh, row_start


def _na_by_head(ref, rr, lane):
    t = ref[rr * GRID_W:(rr + 1) * GRID_W, :]
    zero = jnp.zeros_like(t)
    return jnp.concatenate([jnp.where(lane < NA_HEAD_DIM, t, zero), jnp.where(lane >= NA_HEAD_DIM, t, zero)], axis=0)


def _na_pick_head(t2, lane):
    return jnp.where(lane < NA_HEAD_DIM, t2[:GRID_W], t2[GRID_W:])


def _na_scores(q_ref, k_ref, b_ref, i, nrs, nb, S, L, row_start, lane):
    qh = jnp.concatenate([_na_by_head(q_ref, rr, lane) for rr in range(nrs)], axis=0)
    sc = _dot_nt(qh, k_ref[pl.ds(S, L), :])
    starts, shifts, sb = [], [], []
    for rr in range(nrs):
        r = i * nrs + rr
        rs = row_start(r)
        starts.append(pl.multiple_of(rs * GRID_W, GRID_W))
        shifts.append(r - rs)
        bias = jnp.concatenate([b_ref[0, r - rs], b_ref[1, r - rs]], axis=0)
        sb.append(_dot_nt(qh[rr * 2 * GRID_W:(rr + 1) * 2 * GRID_W], k_ref[pl.ds(starts[-1], nb), :]) + bias)
    return qh, jnp.concatenate(sb, axis=0), sc, starts, shifts


def _na_fwd(qs, kb, vb, bias, S, L):
    T, naw = qs.shape
    rows, kh, row_start = _na_geometry(S)
    nb = kh * GRID_W
    npair = naw // LANES

    nrs = min(NA_ROWS_PER_STEP, rows)
    assert rows % nrs == 0

    def body(q_ref, k_ref, v_ref, b_ref, o_ref, lse_ref):
        i = pl.program_id(1)
        lane = lax.broadcasted_iota(jnp.int32, (GRID_W, LANES), 1)
        _, sb, sc, starts, _ = _na_scores(q_ref, k_ref, b_ref, i, nrs, nb, S, L, row_start, lane)
        m = jnp.maximum(jnp.max(sb, axis=-1, keepdims=True), jnp.max(sc, axis=-1, keepdims=True))
        pb, pc = jnp.exp(sb - m), jnp.exp(sc - m)
        l = jnp.sum(pb, axis=-1, keepdims=True) + jnp.sum(pc, axis=-1, keepdims=True)
        inv = 1.0 / l
        pb16, pc16 = (pb * inv).astype(BF16), (pc * inv).astype(BF16)
        oc = _dot(pc16, v_ref[pl.ds(S, L), :])
        lse = jnp.broadcast_to(m + jnp.log(l), oc.shape)
        for rr in range(nrs):
            two = slice(rr * 2 * GRID_W, (rr + 1) * 2 * GRID_W)
            rsl = slice(rr * GRID_W, (rr + 1) * GRID_W)
            o2 = oc[two] + _dot(pb16[two], v_ref[pl.ds(starts[rr], nb), :])
            o_ref[rsl, :] = _na_pick_head(o2, lane).astype(o_ref.dtype)
            lse_ref[rsl, :] = _na_pick_head(lse[two], lane)

    blk = pl.BlockSpec((nrs * GRID_W, LANES), lambda p, i: (i, p))
    col = pl.BlockSpec((T, LANES), lambda p, i: (0, p))
    return pl.pallas_call(
        body, name="na_fwd", grid=(npair, rows // nrs),
        in_specs=[blk, col, col, pl.BlockSpec((2, kh, GRID_W, nb), lambda p, i: (p, 0, 0, 0))],
        out_specs=[blk, blk],
        out_shape=[jax.ShapeDtypeStruct((S, naw), BF16), jax.ShapeDtypeStruct((S, naw), F32)],
        compiler_params=_params(("parallel", "arbitrary")),
    )(qs, kb, vb, bias)


def _na_bwd(qs, kb, vb, bias, do, o, lse, S, L):
    T, naw = qs.shape
    rows, kh, row_start = _na_geometry(S)
    nb = kh * GRID_W
    npair = naw // LANES

    nrs = min(NA_ROWS_PER_STEP, rows)
    assert rows % nrs == 0

    def body(q_ref, k_ref, v_ref, b_ref, do_ref, o_ref, lse_ref, dq_ref, dk_ref, dv_ref, db_ref):
        i = pl.program_id(1)

        @pl.when(i == 0)
        def _():
            dk_ref[...] = jnp.zeros_like(dk_ref)
            dv_ref[...] = jnp.zeros_like(dv_ref)
            db_ref[...] = jnp.zeros_like(db_ref)

        lane = lax.broadcasted_iota(jnp.int32, (GRID_W, LANES), 1)
        qh, sb, sc, starts, shifts = _na_scores(q_ref, k_ref, b_ref, i, nrs, nb, S, L, row_start, lane)
        doh = jnp.concatenate([_na_by_head(do_ref, rr, lane) for rr in range(nrs)], axis=0)
        o2 = jnp.concatenate([o_ref[rr * GRID_W:(rr + 1) * GRID_W, :] for rr in range(nrs) for _ in range(2)], axis=0)
        lse = jnp.concatenate([lse_ref[rr * GRID_W:(rr + 1) * GRID_W, :][:, hh * NA_HEAD_DIM:hh * NA_HEAD_DIM + 1]
                               for rr in range(nrs) for hh in range(2)], axis=0)
        pb, pc = jnp.exp(sb - lse), jnp.exp(sc - lse)
        delta = jnp.sum(doh.astype(F32) * o2.astype(F32), axis=-1, keepdims=True)
        dpb = jnp.concatenate([_dot_nt(doh[rr * 2 * GRID_W:(rr + 1) * 2 * GRID_W], v_ref[pl.ds(starts[rr], nb), :])
                               for rr in range(nrs)], axis=0)
        dsb = pb * (dpb - delta)
        dsc = pc * (_dot_nt(doh, v_ref[pl.ds(S, L), :]) - delta)
        dsb16, dsc16, pb16, pc16 = dsb.astype(BF16), dsc.astype(BF16), pb.astype(BF16), pc.astype(BF16)
        dqc = _dot(dsc16, k_ref[pl.ds(S, L), :])
        dk_ref[pl.ds(S, L), :] += _dot_tn(dsc16, qh)
        dv_ref[pl.ds(S, L), :] += _dot_tn(pc16, doh)
        for rr in range(nrs):
            two = slice(rr * 2 * GRID_W, (rr + 1) * 2 * GRID_W)
            band = pl.ds(starts[rr], nb)
            dq2 = dqc[two] + _dot(dsb16[two], k_ref[band, :])
            dq_ref[rr * GRID_W:(rr + 1) * GRID_W, :] = _na_pick_head(dq2, lane)
            dk_ref[band, :] += _dot_tn(dsb16[two], qh[two])
            dv_ref[band, :] += _dot_tn(pb16[two], doh[two])
            for hh in range(2):
                db_ref[hh, shifts[rr]] += dsb[(2 * rr + hh) * GRID_W:(2 * rr + hh + 1) * GRID_W]

    blk = pl.BlockSpec((nrs * GRID_W, LANES), lambda p, r: (r, p))
    col = pl.BlockSpec((T, LANES), lambda p, r: (0, p))
    return pl.pallas_call(
        body, name="na_bwd", grid=(npair, rows // nrs),
        in_specs=[blk, col, col, pl.BlockSpec((2, kh, GRID_W, nb), lambda p, r: (p, 0, 0, 0)), blk, blk, blk],
        out_specs=[blk, col, col, pl.BlockSpec((2, kh, GRID_W, nb), lambda p, r: (p, 0, 0, 0))],
        out_shape=[jax.ShapeDtypeStruct((S, naw), F32), jax.ShapeDtypeStruct((T, naw), F32),
                   jax.ShapeDtypeStruct((T, naw), F32), jax.ShapeDtypeStruct(bias.shape, F32)],
        compiler_params=_params(("parallel", "arbitrary")),
    )(qs, kb, vb, bias, do, o, lse)


def _hg_cols(naw, hgf, rev):
    qcol = (3 * naw) // hgf
    fcol = (3 * naw + hgf * (2 if rev else 1)) // hgf
    icol = (3 * naw + 3 * hgf) // hgf
    return qcol, fcol, icol


def _hg_chunk_order(S, L, rev):
    ncl, ncc = S // HG_CHUNK, L // HG_CHUNK
    nc = ncl + ncc

    def chunk_of(i):
        if rev:
            return nc - 1 - i
        return jnp.where(i < ncc, ncl + i, i - ncc)

    return nc, ncl, chunk_of


def _hg_gates(q, z, lb, rev):
    row = lax.broadcasted_iota(jnp.int32, (HG_CHUNK, HG_CHUNK), 0)
    colm = lax.broadcasted_iota(jnp.int32, (HG_CHUNK, HG_CHUNK), 1)
    tri = (colm >= row) if rev else (row >= colm)
    trif = tri.astype(F32)
    sig = _sigmoid(z)
    f = lb + (1.0 - lb) * sig
    lf = jnp.log(f)
    k = 1.0 - f
    cum = _dot(trif, lf, HI)
    mid = cum[HG_CHUNK // 2:HG_CHUNK // 2 + 1, :]
    last = cum[0:1, :] if rev else cum[HG_CHUNK - 1:HG_CHUNK, :]
    eq = jnp.exp(jnp.clip(cum - mid, -EXP_CLAMP, EXP_CLAMP))
    ek = jnp.exp(jnp.clip(mid - cum, -EXP_CLAMP, EXP_CLAMP))
    return tri, trif, sig, f, k, cum, last, eq, ek


def _hg_fwd(u, lbr, S, L, naw, hgf, rev):
    T = S + L
    nh = hgf // HG_DIM
    nc, ncl, chunk_of = _hg_chunk_order(S, L, rev)
    qcol, fcol, icol = _hg_cols(naw, hgf, rev)

    def step(i, q_ref, z_ref, v_ref, lb_ref, o_ref, st_ref, state):
        @pl.when(i == 0)
        def _():
            state[...] = jnp.zeros_like(state)

        q, z, v = q_ref[...], z_ref[...], v_ref[...]
        tri, _, _, _, k, cum, last, eq, ek = _hg_gates(q, z, lb_ref[...], rev)
        qe, ke = (q * eq).astype(BF16), (k * ek).astype(BF16)
        qd, kd = (q * jnp.exp(cum)).astype(BF16), (k * jnp.exp(last - cum)).astype(BF16)
        v16, el = v.astype(BF16), jnp.exp(last)
        for h in range(nh):
            sl = slice(h * HG_DIM, (h + 1) * HG_DIM)
            a = jnp.where(tri, _dot_nt(qe[:, sl], ke[:, sl]), 0.0)
            s0 = state[h]
            st_ref[h] = s0
            o_ref[:, sl] = _dot(a.astype(BF16), v16[:, sl]) + _dot_nt(qd[:, sl], s0.astype(BF16))
            state[h] = s0 * el[:, sl] + _dot_tn(v16[:, sl], kd[:, sl])

    def blk(cb):
        return pl.BlockSpec((HG_CHUNK, hgf), lambda i: (chunk_of(i), cb))

    return dict(
        step=step, nc=nc, operands=[u, u, u, lbr],
        in_specs=[blk(qcol), blk(fcol), blk(icol), pl.BlockSpec((1, hgf), lambda i: (0, 0))],
        out_specs=[pl.BlockSpec((HG_CHUNK, hgf), lambda i: (chunk_of(i), 0)),
                   pl.BlockSpec((None, nh, HG_DIM, HG_DIM), lambda i: (chunk_of(i), 0, 0, 0))],
        out_shape=[jax.ShapeDtypeStruct((T, hgf), F32), jax.ShapeDtypeStruct((nc, nh, HG_DIM, HG_DIM), F32)],
        scratch=[pltpu.VMEM((nh, HG_DIM, HG_DIM), F32)])


def _hg_both(parts, name):
    nin = [len(p["in_specs"]) for p in parts]
    nout = [len(p["out_specs"]) for p in parts]
    nscr = [len(p["scratch"]) for p in parts]

    def body(*refs):
        i = pl.program_id(0)
        ins, outs, scr = refs[:sum(nin)], refs[sum(nin):sum(nin) + sum(nout)], refs[sum(nin) + sum(nout):]
        for d, p in enumerate(parts):
            p["step"](i, *ins[sum(nin[:d]):sum(nin[:d + 1])], *outs[sum(nout[:d]):sum(nout[:d + 1])],
                      *scr[sum(nscr[:d]):sum(nscr[:d + 1])])

    res = pl.pallas_call(
        body, name=name, grid=(parts[0]["nc"],),
        in_specs=[sp for p in parts for sp in p["in_specs"]],
        out_specs=[sp for p in parts for sp in p["out_specs"]],
        out_shape=[sh for p in parts for sh in p["out_shape"]],
        scratch_shapes=[sc for p in parts for sc in p["scratch"]],
        compiler_params=_params(("arbitrary",)),
    )(*[op for p in parts for op in p["operands"]])
    return [list(res[sum(nout[:d]):sum(nout[:d + 1])]) for d in range(len(parts))]


def _hg_bwd(u, lbr, st, do, S, L, naw, hgf, rev):
    T = S + L
    nh = hgf // HG_DIM
    nc, ncl, chunk_fwd = _hg_chunk_order(S, L, rev)
    qcol, fcol, icol = _hg_cols(naw, hgf, rev)

    def chunk_of(j):
        return chunk_fwd(nc - 1 - j)

    def step(j, q_ref, z_ref, v_ref, lb_ref, st_ref, do_ref, dq_ref, dz_ref, dv_ref, dlb_ref, dstate,
             dqe_s, dke_s, dqd_s, dkd_s, dl_s):
        @pl.when(j == 0)
        def _():
            dstate[...] = jnp.zeros_like(dstate)
            dlb_ref[...] = jnp.zeros_like(dlb_ref)

        q, z, v = q_ref[...], z_ref[...], v_ref[...]
        lb = lb_ref[...]
        tri, trif, sig, f, k, cum, last, eq, ek = _hg_gates(q, z, lb, rev)
        ec, el, ekd = jnp.exp(cum), jnp.exp(last), jnp.exp(last - cum)
        qe, ke, qd, kd = q * eq, k * ek, q * ec, k * ekd
        qd16, kd16 = qd.astype(BF16), kd.astype(BF16)
        dout = jnp.where(chunk_of(j) < ncl, do_ref[...], 0.0)
        qe2, ke2, v16, dout16 = _split2(qe), _split2(ke), v.astype(BF16), dout.astype(BF16)
        for h in range(nh):
            sl = slice(h * HG_DIM, (h + 1) * HG_DIM)
            qeh, keh = [(t[0][:, sl], t[1][:, sl]) for t in (qe2, ke2)]
            a = jnp.where(tri, _dot_nt(qeh[0], keh[0]), 0.0).astype(BF16)
            s0 = st_ref[h]
            ds1 = dstate[h]
            s016, ds116 = s0.astype(BF16), ds1.astype(BF16)
            dv_ref[:, sl] = (_dot_tn(a, dout16[:, sl]) + _dot_nt(kd16[:, sl], ds116)).astype(dv_ref.dtype)
            da2 = _split2(jnp.where(tri, _dot_nt(dout16[:, sl], v16[:, sl]), 0.0))
            dqe_s[:, sl] = _dot_x3(_dot, da2, keh)
            dke_s[:, sl] = _dot_x3(_dot_tn, da2, qeh)
            dqd_s[:, sl] = _dot(dout16[:, sl], s016)
            dkd_s[:, sl] = _dot(v16[:, sl], ds116)
            dl_s[:, sl] = _colsum(ds1 * s0)
            dstate[h] = _dot_tn(dout16[:, sl], qd16[:, sl]) + ds1 * el[:, sl]
        dqe, dke, dqd, dkd = dqe_s[...], dke_s[...], dqd_s[...], dkd_s[...]
        dq_ref[...] = (dqe * eq + dqd * ec).astype(dq_ref.dtype)
        dk = dke * ek + dkd * ekd
        dcum = dqe * qe - dke * ke + dqd * qd - dkd * kd
        dlast = _colsum(dkd * kd) + el * dl_s[...]
        dlf = _dot_tn(trif, dcum, HI) + dlast
        df = dlf / f - dk
        dz_ref[...] = (df * (1.0 - lb) * sig * (1.0 - sig)).astype(dz_ref.dtype)
        dlb_ref[...] += _colsum(df * (1.0 - sig))

    def blk(cb):
        return pl.BlockSpec((HG_CHUNK, hgf), lambda j: (chunk_of(j), cb))

    oblk = pl.BlockSpec((HG_CHUNK, hgf), lambda j: (chunk_of(j), 0))
    wide = pltpu.VMEM((HG_CHUNK, hgf), F32)
    return dict(
        step=step, nc=nc, operands=[u, u, u, lbr, st, do],
        in_specs=[blk(qcol), blk(fcol), blk(icol), pl.BlockSpec((1, hgf), lambda j: (0, 0)),
                  pl.BlockSpec((None, nh, HG_DIM, HG_DIM), lambda j: (chunk_of(j), 0, 0, 0)),
                  pl.BlockSpec((HG_CHUNK, hgf), lambda j: (jnp.minimum(chunk_of(j), ncl - 1), 0))],
        out_specs=[oblk, oblk, oblk, pl.BlockSpec((1, hgf), lambda j: (0, 0))],
        out_shape=[jax.ShapeDtypeStruct((T, hgf), BF16)] * 3 + [jax.ShapeDtypeStruct((1, hgf), F32)],
        scratch=[pltpu.VMEM((nh, HG_DIM, HG_DIM), F32), wide, wide, wide, wide, pltpu.VMEM((1, hgf), F32)])


def _adamw(w, g, m, v, name):
    shape = w.shape
    if w.ndim != 2 or shape[0] % 8 or shape[1] % LANES:
        w, g, m, v = [a.reshape(1, -1) for a in (w, g, m, v)]
    r, cw = w.shape
    tm = _tile(r, max(8, (1 << 19) // cw), 8) if r % 8 == 0 else r
    c1 = 1.0 / (1.0 - ADAM_B1 ** ADAM_STEP)
    c2 = 1.0 / (1.0 - ADAM_B2 ** ADAM_STEP)

    def body(w_ref, g_ref, m_ref, v_ref, d_ref, nm_ref, nv_ref):
        gg = g_ref[...]
        nm = ADAM_B1 * m_ref[...] + (1.0 - ADAM_B1) * gg
        nv = ADAM_B2 * v_ref[...] + (1.0 - ADAM_B2) * (gg * gg)
        d_ref[...] = -ADAM_LR * ((nm * c1) / (jnp.sqrt(nv * c2) + ADAM_EPS) + ADAM_WD * w_ref[...])
        nm_ref[...] = nm
        nv_ref[...] = nv

    spec = pl.BlockSpec((tm, cw), lambda i: (i, 0))
    outs = pl.pallas_call(
        body, name=name, grid=(r // tm,), in_specs=[spec] * 4, out_specs=[spec] * 3,
        out_shape=[jax.ShapeDtypeStruct((r, cw), F32)] * 3,
        compiler_params=_params(("parallel",)),
    )(w, g, m, v)
    return [o.reshape(shape) for o in outs]


def kernel(x, c, ctx, c_ctx, w_ada, b_ada, norm1_g, w_in, na_rpb, hg_lb_logits, hg_norm_g, w_pa, w_pb, w_out, norm2_g, w_ffn_in, w_ffn_out, final_g, loss_target, m_c_ctx, m_w_ada, m_b_ada, m_norm1_g, m_w_in, m_na_rpb, m_hg_lb_logits, m_hg_norm_g, m_w_pa, m_w_pb, m_w_out, m_norm2_g, m_w_ffn_in, m_w_ffn_out, m_final_g, v_c_ctx, v_w_ada, v_b_ada, v_norm1_g, v_w_in, v_na_rpb, v_hg_lb_logits, v_hg_norm_g, v_w_pa, v_w_pb, v_w_out, v_norm2_g, v_w_ffn_in, v_w_ffn_out, v_final_g):
    xi, yi, ci = lax.axis_index("x"), lax.axis_index("y"), lax.axis_index("c")
    sidx = 2 * xi + yi
    eidx = 4 * xi + 2 * yi + ci

    S, D = x.shape[1], x.shape[2]
    L = ctx.shape[1]
    T = S + L
    naw = NA_HEADS * NA_HEAD_DIM
    hgf = HG_HEADS * HG_DIM
    inw = 3 * naw + 5 * hgf + 2 * D
    fh = w_ffn_out.shape[1] * 4
    ads = w_ada.shape[2]
    fs = hg_lb_logits.shape[2]
    rows = S // GRID_W
    tr = _tile(L, 256)
    nlat, nall = S // tr, T // tr
    assert naw == hgf and D % naw == 0 and S % tr == 0 and 2 * hgf <= D

    pack0 = jnp.concatenate([c, jnp.pad(hg_lb_logits.reshape(1, -1), ((0, 0), (0, D - 4 * fs))),
                             jnp.zeros((6, D), F32)], axis=0)
    g0 = _all_gather8(pack0).reshape(8, 8, D)
    cs = g0[:, 0]
    lbl = g0[::2, 1, :4 * fs].reshape(4, 2, 2, fs).transpose(1, 2, 0, 3).reshape(2, 2, 4 * fs)
    p_lb = jax.nn.softmax(lbl, axis=0)
    lb = p_lb[0]
    lbb = [lb[d].reshape(1, hgf) for d in range(2)]

    cin = jnp.concatenate([cs, c_ctx[None], jnp.zeros((7, D), F32)], axis=0)
    b_sh = lax.dynamic_slice(b_ada, (0, sidx * ads), (1, ads))
    modp = _ada_fwd(cin, w_ada[0], b_sh)
    modfull = _all_gather8(modp).reshape(8, 16, ads)[::2].transpose(1, 0, 2).reshape(16, 4 * ads)
    mod_e = jnp.pad(lax.dynamic_index_in_dim(modfull, eidx, 0, keepdims=False).reshape(N_MOD, D), ((0, 2), (0, 0)))
    mod_c = jnp.pad(modfull[8].reshape(N_MOD, D), ((0, 2), (0, 0)))

    names = ["w_in", "w_pa", "w_pb", "w_out", "w_ffn_in", "w_ffn_out"]
    placed = [_cast_place(w[0], "cast_" + nm)
              for w, nm in zip((w_in, w_pa, w_pb, w_out, w_ffn_in, w_ffn_out), names)]
    def shards(g):
        return g.reshape(4, 2 * g.shape[2], g.shape[3])

    win3 = shards(_exchange_call(_exchange_gather_via_neighbours(placed[:1]), "gather_w_in")[0])

    x2d, ctx2d = x[0], ctx[0]

    def f_ln1(i, rv, vv):
        xl, xc = rv
        g, me, mc = vv
        isc = i >= nlat
        xt = jnp.where(isc, xc, xl)
        sh = jnp.where(isc, mc[0:1], me[0:1])
        sc = jnp.where(isc, mc[1:2], me[1:2])
        h = xt * _rms(xt) * g * (1.0 + sc) + sh
        return [h, h], []

    hb, hbt = _rowwise(f_ln1, nall, tr, [(x2d, D, 0, lambda i: jnp.minimum(i, nlat - 1)),
                                        (ctx2d, D, 0, lambda i: jnp.maximum(i - nlat, 0))],
                       [norm1_g, mod_e, mod_c], [(D, BF16), (D, BF16, "T")], [], "ln1")
    u, gathered = _mm_nn(hb, win3, F32, "mm_in", carry=_exchange_gather(placed[1:]))
    wpa3, wpb3, wout3, wi3, wfo3 = [shards(g) for g in gathered]
    wout1 = wout3.reshape(1, D, D)
    wfo1 = wfo3.reshape(1, fh, D)

    scale = NA_HEAD_DIM ** -0.5

    def f_qkv(i, rv, vv):
        q, k, v = rv
        return [q * scale, k, v], []

    qs, kb, vb = _rowwise(f_qkv, nall, tr, [(u, naw, 0, None), (u, naw, 1, None), (u, naw, 2, None)], [],
                          [(naw, BF16)] * 3, [], "qkv_cast")
    bias = _bias_tables(na_rpb[0], rows)
    o_na, lse = _na_fwd(qs, kb, vb, bias, S, L)

    (o_f, st_f), (o_b, st_b) = _hg_both(
        [_hg_fwd(u, lbb[0], S, L, naw, hgf, False), _hg_fwd(u, lbb[1], S, L, naw, hgf, True)], "hg_fwd")

    hgn = jnp.tile(hg_norm_g, (1, HG_HEADS))
    hog_cb = (3 * naw + 4 * hgf) // hgf
    ga_cb = (3 * naw + 5 * hgf) // D
    gb_cb = ga_cb + 1

    def heads_rms(o):
        return jnp.concatenate([jnp.broadcast_to(_rms(o[:, h * HG_DIM:(h + 1) * HG_DIM]), (o.shape[0], HG_DIM))
                                for h in range(HG_HEADS)], axis=1)

    def f_readout(i, rv, vv):
        of, ob_, hog = rv
        g, = vv
        o = of + ob_
        return [o * heads_rms(o) * g * _silu(hog)], []

    ob, = _rowwise(f_readout, nlat, tr, [(o_f, hgf, 0, None), (o_b, hgf, 0, None), (u, hgf, hog_cb, None)],
                   [hgn], [(hgf, BF16)], [], "hg_readout")

    ya = _mm_nn(o_na, wpa3, BF16, "mm_pa")
    yb = _mm_nn(ob, wpb3, BF16, "mm_pb")

    def f_merge(i, rv, vv):
        ya_, yb_, ga, gb = rv
        return [_sigmoid(ga) * ya_ + _sigmoid(gb) * yb_], []

    yv, = _rowwise(f_merge, nlat, tr, [(ya, D, 0, None), (yb, D, 0, None), (u, D, ga_cb, None), (u, D, gb_cb, None)],
                   [], [(D, BF16)], [], "merge")
    z = _mm_nn(yv, wout1, F32, "mm_out")

    def f_res1(i, rv, vv):
        xt, zt = rv
        g, me = vv
        x1 = xt + me[2:3] * zt
        h = x1 * _rms(x1) * g * (1.0 + me[4:5]) + me[3:4]
        return [x1, h, h], []

    x1, h2, h2t = _rowwise(f_res1, nlat, tr, [(x2d, D, 0, None), (z, D, 0, None)], [norm2_g, mod_e],
                           [(D, F32), (D, BF16), (D, BF16, "T")], [], "res1_ln2")
    au3, sw, swt = _ffn_in_fused(h2, wi3, "mm_ffn_in")
    ff =_mm_nn(sw, wfo1, F32, "mm_ffn_out")

    fg = final_g.reshape(1, D)

    def f_final(i, rv, vv):
        x1t, ft, tg = rv
        g, me = vv
        x2 = x1t + me[5:6] * ft
        r3 = _rms(x2)
        xn = x2 * r3
        err = xn * g - tg
        dyy = err * (1.0 / D)
        dxn = dyy * g
        dx2 = r3 * (dxn - xn * jnp.mean(dxn * xn, axis=-1, keepdims=True))
        return [dx2, dx2 * me[5:6]], [_colsum(err * err), _colsum(dyy * xn), _colsum(dx2 * ft)]

    dx2, dfb, loss_cols, dfg, dg2 = _rowwise(
        f_final, nlat, tr, [(x1, D, 0, None), (ff, D, 0, None), (loss_target[0], D, 0, None)], [fg, mod_e],
        [(D, F32), (D, BF16)], [D, D, D], "final_loss")

    def dswiglu_epi(tm, tn):
        blk = pl.BlockSpec((2, tm, tn), lambda m, n, k: (0, m, n))

        def fn(d, ins):
            a, uu = ins[0][0].astype(F32), ins[0][1].astype(F32)
            return [(d * uu * _dsilu(a), d * _silu(a))]

        return dict(ins=[(au3, blk)], outs=[(jax.ShapeDtypeStruct((2, S, fh), BF16), blk)], fn=fn)

    dau3, = _mm_nt(dfb, wfo1, BF16, "mm_d_sw", epi=dswiglu_epi, tn_target=512)
    g_wfo = _mm_tn(swt, dfb, 1, "mm_dw_ffn_out", a_is_t=True).reshape(4, 2, fh // 8, D)
    dh2 = _mm_nt(dau3, wi3, F32, "mm_d_h2")
    g_wi = _mm_tn(h2t, dau3, 4, "mm_dw_ffn_in", a_is_t=True)

    def f_ln2_bwd(i, rv, vv):
        dh, x1t, dx2t, zt = rv
        g, me = vv
        r2 = _rms(x1t)
        xn = x1t * r2
        dxn = dh * g * (1.0 + me[4:5])
        dx1 = dx2t + r2 * (dxn - xn * jnp.mean(dxn * xn, axis=-1, keepdims=True))
        return ([dx1, dx1 * me[2:3]],
                [_colsum(dh), _colsum(dh * xn * g), _colsum(dh * xn * (1.0 + me[4:5])), _colsum(dx1 * zt)])

    dx1, dzb, dsh2, dsc2, dn2g, dg1 = _rowwise(
        f_ln2_bwd, nlat, tr, [(dh2, D, 0, None), (x1, D, 0, None), (dx2, D, 0, None), (z, D, 0, None)],
        [norm2_g, mod_e], [(D, F32), (D, BF16)], [D, D, D, D], "ln2_bwd")

    g_wout = _mm_tn(yv, dzb, 1, "mm_dw_out").reshape(4, 2, D // 8, D)

    def dmerge_epi(tm, tn):
        blk = pl.BlockSpec((tm, tn), lambda m, n, k: (m, n))

        def gate(cb):
            return pl.BlockSpec((tm, tn), lambda m, n, k: (m, cb * (D // tn) + n))

        def fn(d, ins):
            ya_, yb_, ga, gb = ins
            sa, sb_ = _sigmoid(ga), _sigmoid(gb)
            return [d * sa, d * sb_, d * ya_ * sa * (1.0 - sa), d * yb_ * sb_ * (1.0 - sb_)]

        return dict(ins=[(ya, blk), (yb, blk), (u, gate(ga_cb)), (u, gate(gb_cb))],
                    outs=[(jax.ShapeDtypeStruct((S, D), BF16), blk)] * 4, fn=fn)

    dya, dyb, dga, dgb = _mm_nt(dzb, wout1, BF16, "mm_d_y", epi=dmerge_epi, tn_target=512)
    d_ona = _mm_nt(dya, wpa3, BF16, "mm_d_ona")
    d_ob = _mm_nt(dyb, wpb3, F32, "mm_d_ob")
    g_wpa = _mm_tn(o_na, dya, 4, "mm_dw_pa")
    g_wpb = _mm_tn(ob, dyb, 4, "mm_dw_pb")

    def f_dreadout(i, rv, vv):
        d, of, ob_, hog = rv
        g, = vv
        o = of + ob_
        on = o * heads_rms(o)
        t = d * _silu(hog) * g
        mt = jnp.concatenate([jnp.broadcast_to(jnp.mean((t * on)[:, h * HG_DIM:(h + 1) * HG_DIM], axis=-1,
                                                        keepdims=True), (o.shape[0], HG_DIM))
                              for h in range(HG_HEADS)], axis=1)
        do_ = heads_rms(o) * (t - on * mt)
        return [do_, d * on * g * _dsilu(hog)], [_colsum(d * _silu(hog) * on)]

    do_hg, dhog, dhgn = _rowwise(
        f_dreadout, nlat, tr, [(d_ob, hgf, 0, None), (o_f, hgf, 0, None), (o_b, hgf, 0, None),
                               (u, hgf, hog_cb, None)], [hgn], [(hgf, F32), (hgf, BF16)], [hgf], "hg_readout_bwd")

    (dq_f, dz_f, dv_f, dlb_f), (dq_b, dz_b, dv_b, dlb_b) = _hg_both(
        [_hg_bwd(u, lbb[0], st_f, do_hg, S, L, naw, hgf, False), _hg_bwd(u, lbb[1], st_b, do_hg, S, L, naw, hgf, True)],
        "hg_bwd")
    dq_na, dk_na, dv_na, dbias = _na_bwd(qs, kb, vb, bias, d_ona, o_na, lse, S, L)

    ta = _tile(L, 128)
    nla, naa = S // ta, T // ta
    lat = lambda i: jnp.minimum(i, nla - 1)

    def f_assemble(i, rv, vv):
        dqn, dk, dv, dqf, dqb, dzf, dzb_, dvf, dvb, dho, dga_, dgb_ = rv
        keep = (i < nla).astype(F32)
        f32 = lambda t: t.astype(F32)
        return [jnp.concatenate([dqn * (scale * keep), dk, dv, f32(dqf) + f32(dqb), f32(dzf), f32(dzb_),
                                 f32(dvf) + f32(dvb),
                                 dho.astype(F32) * keep, dga_.astype(F32) * keep, dgb_.astype(F32) * keep],
                                axis=1)], []

    du, = _rowwise(
        f_assemble, naa, ta,
        [(dq_na, naw, 0, lat), (dk_na, naw, 0, None), (dv_na, naw, 0, None), (dq_f, hgf, 0, None),
         (dq_b, hgf, 0, None), (dz_f, hgf, 0, None), (dz_b, hgf, 0, None), (dv_f, hgf, 0, None),
         (dv_b, hgf, 0, None), (dhog, hgf, 0, lat), (dga, D, 0, lat), (dgb, D, 0, lat)],
        [], [(inw, BF16)], [], "assemble_du")

    def add_half(g, land, nm):
        return _add_own_half(g, land, "rs_add_" + nm)

    early = [g_wpa, g_wpb, g_wout, g_wi, g_wfo]
    g_win_other, lands = _mm_tn_half(hbt, du, 4, True, "mm_dw_in_other", carry=_exchange_swap_other_half(early))
    parts = [add_half(g, l, nm) for g, l, nm in zip(early, lands, names[1:])]
    g_win_mine, landed = _mm_tn_half(hbt, du, 4, False, "mm_dw_in_mine", carry=_exchange_join(
        _exchange_scatter(parts[3:4]), _exchange_swap_other_half([g_win_other])))
    piece_wi = landed[0]
    parts = [add_half(g_win_mine, landed[1], names[0])] + parts
    dh, landed = _mm_nt(du, win3, F32, "mm_d_h", carry=_exchange_scatter(parts[:4] + parts[5:]))
    pieces = landed[:4] + [piece_wi] + landed[4:]
    halves = [_sum_pieces(p, l, "rs_sum_" + nm) for p, l, nm in zip(parts, pieces, names)]
    g_win, g_wpa, g_wpb, g_wout, g_wi, g_wfo = [
        f.reshape(2 * f.shape[1], f.shape[2])
        for f in _exchange_call(_exchange_swap_result(halves), "rs_swap_result_half")]

    def f_ln1_bwd(i, rv, vv):
        dht, xt, dx1t = rv
        g, me = vv
        r1 = _rms(xt)
        xn = xt * r1
        dxn = dht * g * (1.0 + me[1:2])
        dx = dx1t + r1 * (dxn - xn * jnp.mean(dxn * xn, axis=-1, keepdims=True))
        return [dx], [_colsum(dht), _colsum(dht * xn * g), _colsum(dht * xn * (1.0 + me[1:2]))]

    grad_x, dsh1, dsc1, dn1g_l = _rowwise(
        f_ln1_bwd, nlat, tr, [(dh, D, 0, None), (x2d, D, 0, None), (dx1, D, 0, None)], [norm1_g, mod_e],
        [(D, F32)], [D, D, D], "ln1_bwd")

    def f_ln1_bwd_ctx(i, rv, vv):
        dht, xt = rv
        g, mc = vv
        xn = xt * _rms(xt)
        return [], [_colsum(dht), _colsum(dht * xn * g), _colsum(dht * xn * (1.0 + mc[1:2]))]

    ctx_rows = lambda i: i + nlat
    dsh1c, dsc1c, dn1g_c = _rowwise(
        f_ln1_bwd_ctx, nall - nlat, tr, [(dh, D, 0, ctx_rows), (ctx2d, D, 0, None)], [norm1_g, mod_c],
        [], [D, D, D], "ln1_bwd_ctx")

    drpb = _bias_tables_transpose(dbias, rows).reshape(1, -1)
    nrp = -(-drpb.shape[1] // D)
    drpb_rows = jnp.pad(drpb, ((0, 0), (0, nrp * D - drpb.shape[1]))).reshape(nrp, D)
    dlb = jnp.concatenate([dlb_f, dlb_b], axis=1)
    dhg = jnp.sum(dhgn.reshape(HG_HEADS, HG_DIM), axis=0, keepdims=True)

    def wide(v):
        return jnp.pad(v, ((0, 0), (0, D - v.shape[1])))

    pack_rows = [loss_cols, dfg, dn2g, dn1g_l + dn1g_c, dsh1, dsc1, dg1, dsh2, dsc2, dg2, dsh1c, dsc1c,
                 wide(dhg), wide(dlb), drpb_rows]
    pack = jnp.concatenate(pack_rows, axis=0)
    npk = -(-pack.shape[0] // 8) * 8
    pack = jnp.pad(pack, ((0, npk - pack.shape[0]), (0, 0)))
    gp = _all_gather8(pack).reshape(8, npk, D)
    tot = _sum8(gp, "sum_small_grads")

    loss = (0.5 / D) * jnp.sum(tot[0])
    grad_final_g = tot[1]
    grad_norm2_g = tot[2:3]
    grad_norm1_g = tot[3:4]
    grad_hg_norm_g = tot[12:13, :HG_DIM]
    dlb_tot = tot[13, :2 * hgf].reshape(2, hgf)
    grad_na_rpb = tot[14:14 + nrp].reshape(-1)[:drpb.shape[1]].reshape(na_rpb.shape)
    dlog = jnp.stack([dlb_tot * p_lb[0] * (1.0 - p_lb[0]), -dlb_tot * p_lb[0] * p_lb[1]], axis=0)
    grad_hg_lb = lax.dynamic_slice(dlog, (0, 0, sidx * fs), (2, 2, fs))

    dmod_all = gp[:, 4:10].reshape(8, N_MOD * D)
    dmod_ctx = jnp.concatenate([tot[10], tot[11], jnp.zeros((4 * D,), F32)])[None]
    dm16 = jnp.concatenate([dmod_all, dmod_ctx, jnp.zeros((7, N_MOD * D), F32)], axis=0)
    grad_b_ada = jnp.sum(dm16, axis=0, keepdims=True)
    dm_sh = lax.dynamic_slice(dm16, (0, sidx * ads), (16, ads))
    g_wada, dcin = _ada_bwd(cin, w_ada[0], dm_sh)
    gc = _all_gather8(dcin[8:16]).reshape(8, 8, D)
    grad_c_ctx = (gc[0, 0] + gc[2, 0] + gc[4, 0] + gc[6, 0]) * _dsilu(c_ctx)

    grads = {
        "c_ctx": grad_c_ctx, "w_ada": g_wada[None], "b_ada": grad_b_ada, "norm1_g": grad_norm1_g,
        "w_in": g_win[None], "na_rpb": grad_na_rpb, "hg_lb_logits": grad_hg_lb, "hg_norm_g": grad_hg_norm_g,
        "w_pa": g_wpa[None], "w_pb": g_wpb[None], "w_out": g_wout[None], "norm2_g": grad_norm2_g,
        "w_ffn_in": g_wi[None], "w_ffn_out": g_wfo[None], "final_g": grad_final_g,
    }
    weights = {
        "c_ctx": (c_ctx, m_c_ctx, v_c_ctx), "w_ada": (w_ada, m_w_ada, v_w_ada), "b_ada": (b_ada, m_b_ada, v_b_ada),
        "norm1_g": (norm1_g, m_norm1_g, v_norm1_g), "w_in": (w_in, m_w_in, v_w_in),
        "na_rpb": (na_rpb, m_na_rpb, v_na_rpb), "hg_lb_logits": (hg_lb_logits, m_hg_lb_logits, v_hg_lb_logits),
        "hg_norm_g": (hg_norm_g, m_hg_norm_g, v_hg_norm_g), "w_pa": (w_pa, m_w_pa, v_w_pa),
        "w_pb": (w_pb, m_w_pb, v_w_pb), "w_out": (w_out, m_w_out, v_w_out),
        "norm2_g": (norm2_g, m_norm2_g, v_norm2_g), "w_ffn_in": (w_ffn_in, m_w_ffn_in, v_w_ffn_in),
        "w_ffn_out": (w_ffn_out, m_w_ffn_out, v_w_ffn_out), "final_g": (final_g, m_final_g, v_final_g),
    }
    order = list(weights)
    deltas, new_ms, new_vs = [], [], []
    for nm in order:
        w, m, v = weights[nm]
        g = grads[nm].reshape(w.shape)
        grads[nm] = g
        if w.ndim == 3 and w.shape[0] == 1:
            d_, m_, v_ = _adamw(w[0], g[0], m[0], v[0], "adamw_" + nm)
            d_, m_, v_ = d_[None], m_[None], v_[None]
        else:
            d_, m_, v_ = _adamw(w, g, m, v, "adamw_" + nm)
        deltas.append(d_)
        new_ms.append(m_)
        new_vs.append(v_)

    return (loss, grad_x[None], *[grads[nm] for nm in order], *deltas, *new_ms, *new_vs)
```

```python
import numpy as np

import jax
import jax.numpy as jnp
from jax import lax
from jax.experimental import pallas as pl
from jax.experimental.pallas import tpu as pltpu

F32 = jnp.float32
BF16 = jnp.bfloat16

GRID_W = 64
WIN_H = 8
WIN_W = 16
NA_HEADS = 16
NA_HEAD_DIM = 64
HG_HEADS = 8
HG_DIM = 128
HG_CHUNK = 64
N_MOD = 6
EPS = 1e-6
ADAM_LR = 0.001
ADAM_B1 = 0.9
ADAM_B2 = 0.999
ADAM_EPS = 1e-08
ADAM_WD = 0.01
ADAM_STEP = 10

LANES = 128
NA_ROWS_PER_STEP = 16
VMEM_LIMIT = 56 * 1024 * 1024
MASK_VALUE = -1e30
EXP_CLAMP = 80.0
MESH_ID = pl.DeviceIdType.MESH
HI = lax.Precision.HIGHEST


def _tile(dim, target, mult=LANES):
    best = None
    t = mult
    while t <= min(dim, target):
        if dim % t == 0:
            best = t
        t += mult
    assert best is not None, (dim, target, mult)
    return best


def _params(sem):
    return pltpu.CompilerParams(dimension_semantics=sem, vmem_limit_bytes=VMEM_LIMIT)


def _dot(a, b, precision=None):
    return jnp.dot(a, b, preferred_element_type=F32, precision=precision)


def _dot_nt(a, b, precision=None):
    return lax.dot_general(a, b, (((1,), (1,)), ((), ())), preferred_element_type=F32, precision=precision)


def _dot_tn(a, b, precision=None):
    return lax.dot_general(a, b, (((0,), (0,)), ((), ())), preferred_element_type=F32, precision=precision)


def _split2(v):
    hi = v.astype(BF16)
    return hi, (v - hi.astype(F32)).astype(BF16)


def _dot_x3(dot, a2, b2):
    return dot(a2[0], b2[0]) + (dot(a2[0], b2[1]) + dot(a2[1], b2[0]))


def _sigmoid(v):
    return 1.0 / (1.0 + jnp.exp(-v))


def _mm_call(dot, operands, grid, in_specs, out_spec, out_shape, acc_shape, name, carry=None, epi=None):
    nk = grid[2]
    nci = 0 if carry is None else len(carry["ins"])
    nco = 0 if carry is None else len(carry["outs"])
    nei = 0 if epi is None else len(epi["ins"])
    neo = 1 if epi is None else len(epi["outs"])

    def body(*refs):
        a_ref, b_ref = refs[:2]
        ein = refs[2:2 + nei]
        cin = refs[2 + nei:2 + nei + nci]
        outs = refs[2 + nei + nci:2 + nei + nci + neo]
        cout = refs[2 + nei + nci + neo:2 + nei + nci + neo + nco]
        acc = refs[2 + nei + nci + neo + nco]
        sems = refs[3 + nei + nci + neo + nco:]
        m, n, k = pl.program_id(0), pl.program_id(1), pl.program_id(2)

        if carry is not None:
            @pl.when((m == 0) & (n == 0) & (k == 0))
            def _():
                carry["start"](cin, cout, *sems)

        @pl.when(k == 0)
        def _():
            acc[...] = jnp.zeros_like(acc)

        if carry is not None and "mid" in carry:
            @pl.when((m == grid[0] - 1) & (n == 0) & (k == 0))
            def _():
                carry["mid"](cin, cout, *sems)

        acc[...] += dot(a_ref[...], b_ref[...])

        @pl.when(k == nk - 1)
        def _():
            vals = [acc[...]] if epi is None else epi["fn"](acc[...], [r[...] for r in ein])
            for r, v in zip(outs, vals):
                if isinstance(v, tuple):
                    for i, vi in enumerate(v):
                        r[i] = vi.astype(r.dtype)
                else:
                    r[...] = v.astype(r.dtype)

        if carry is not None:
            @pl.when((m == grid[0] - 1) & (n == grid[1] - 1) & (k == nk - 1))
            def _():
                carry["finish"](cin, cout, *sems)

    any_spec = pl.BlockSpec(memory_space=pl.ANY)
    scratch = [pltpu.VMEM(acc_shape, F32)]
    extra = {}
    if carry is not None:
        scratch += [pltpu.SemaphoreType.DMA((carry["nsem"],)), pltpu.SemaphoreType.DMA((carry["nsem"],))]
        extra["input_output_aliases"] = {2 + nei + i: neo + j for i, j in carry["alias"].items()}
    sem = ("arbitrary",) * 3 if carry is not None else ("parallel", "parallel", "arbitrary")
    main_outs = [(out_shape, out_spec)] if epi is None else list(epi["outs"])
    res = pl.pallas_call(
        body, name=name, grid=grid,
        in_specs=list(in_specs) + ([] if epi is None else [sp for _, sp in epi["ins"]]) + [any_spec] * nci,
        out_specs=[sp for _, sp in main_outs] + [any_spec] * nco,
        out_shape=[sh for sh, _ in main_outs] + ([] if carry is None else list(carry["outs"])),
        scratch_shapes=scratch, compiler_params=_params(sem), **extra,
    )(*operands, *([] if epi is None else [ar for ar, _ in epi["ins"]]), *([] if carry is None else carry["ins"]))
    res = list(res)
    main = res[0] if epi is None else res[:neo]
    return main if carry is None else (main, res[neo:])


def _mm_nn(a, b3, out_dtype, name, carry=None):
    M, K = a.shape
    nsh, _, Ns = b3.shape
    tm, tn, tk = _tile(M, 1024), _tile(Ns, 1408), _tile(K, 2816)
    tps, nk = Ns // tn, K // tk
    return _mm_call(
        _dot, (a, b3), (M // tm, nsh * tps, nk),
        [pl.BlockSpec((tm, tk), lambda m, n, k: (m, k)),
         pl.BlockSpec((None, tk, tn), lambda m, n, k: (n // tps, k, n % tps))],
        pl.BlockSpec((tm, tn), lambda m, n, k: (m, n)),
        jax.ShapeDtypeStruct((M, nsh * Ns), out_dtype), (tm, tn), name, carry)


def _mm_nt(a, b3, out_dtype, name, carry=None, epi=None, tn_target=1408):
    a3 = a if a.ndim == 3 else a[None]
    na, M, Ka = a3.shape
    nsh, Kw, Ns = b3.shape
    assert na * Ka == nsh * Ns
    tm, tn, tk = _tile(M, 1024), _tile(Kw, tn_target), _tile(int(np.gcd(Ka, Ns)), 3072)
    kpa, kps = Ka // tk, Ns // tk
    return _mm_call(
        _dot_nt, (a3, b3), (M // tm, Kw // tn, nsh * kps),
        [pl.BlockSpec((None, tm, tk), lambda m, n, k: (k // kpa, m, k % kpa)),
         pl.BlockSpec((None, tn, tk), lambda m, n, k: (k // kps, n, k % kps))],
        pl.BlockSpec((tm, tn), lambda m, n, k: (m, n)),
        jax.ShapeDtypeStruct((M, Kw), out_dtype), (tm, tn), name, carry,
        None if epi is None else epi(tm, tn))


def _mm_tn(a, g, nsh, name, carry=None, a_is_t=False):
    Tk, M = a.shape[::-1] if a_is_t else a.shape
    g3 = g if g.ndim == 3 else g[None]
    ng, _, Ng = g3.shape
    Ns = ng * Ng // nsh
    tm, tn, tk = _tile(M // 2, 1408), _tile(int(np.gcd(Ng, Ns)), 1408), _tile(Tk, 2048)
    mh, tps, tpg, nk = (M // 2) // tm, Ns // tn, Ng // tn, Tk // tk
    return _mm_call(
        _dot if a_is_t else _dot_tn, (a, g3), (M // tm, nsh * tps, nk),
        [pl.BlockSpec((tm, tk), lambda m, n, k: (m, k)) if a_is_t else pl.BlockSpec((tk, tm), lambda m, n, k: (k, m)),
         pl.BlockSpec((None, tk, tn), lambda m, n, k: (n // tpg, k, n % tpg))],
        pl.BlockSpec((None, None, tm, tn), lambda m, n, k: (n // tps, m // mh, m % mh, n % tps)),
        jax.ShapeDtypeStruct((nsh, 2, M // 2, Ns), F32), (tm, tn), name, carry)


def _mm_tn_half(at, g, nsh, other, name, carry=None):
    M, Tk = at.shape
    Ns = g.shape[1] // nsh
    tm, tn, tk = _tile(M // 2, 1408), _tile(Ns, 1408), _tile(Tk, 2048)
    mh, tps, nk = (M // 2) // tm, Ns // tn, Tk // tk

    def half():
        c = lax.axis_index("c")
        return 1 - c if other else c

    return _mm_call(
        _dot, (at, g), (mh, nsh * tps, nk),
        [pl.BlockSpec((tm, tk), lambda m, n, k: (half() * mh + m, k)),
         pl.BlockSpec((tk, tn), lambda m, n, k: (k, n))],
        pl.BlockSpec((None, tm, tn), lambda m, n, k: (n // tps, m, n % tps)),
        jax.ShapeDtypeStruct((nsh, M // 2, Ns), F32), (tm, tn), name, carry)


def _ffn_in_fused(h2, wi3, name):
    M, K = h2.shape
    _, _, Ns = wi3.shape
    fh = 2 * Ns
    tm, tn = _tile(M, 512), _tile(Ns, 1408)
    tps = Ns // tn

    def body(h_ref, ba_ref, bu_ref, au_ref, sw_ref, swt_ref):
        h = h_ref[...]
        a, u = _dot(h, ba_ref[...]), _dot(h, bu_ref[...])
        au_ref[0] = a.astype(au_ref.dtype)
        au_ref[1] = u.astype(au_ref.dtype)
        sw = (_silu(a) * u).astype(sw_ref.dtype)
        sw_ref[...] = sw
        swt_ref[...] = sw.T

    return pl.pallas_call(
        body, name=name, grid=(M // tm, fh // tn),
        in_specs=[pl.BlockSpec((tm, K), lambda m, n: (m, 0)),
                  pl.BlockSpec((None, K, tn), lambda m, n: (n // tps, 0, n % tps)),
                  pl.BlockSpec((None, K, tn), lambda m, n: (2 + n // tps, 0, n % tps))],
        out_specs=[pl.BlockSpec((2, tm, tn), lambda m, n: (0, m, n)), pl.BlockSpec((tm, tn), lambda m, n: (m, n)),
                   pl.BlockSpec((tn, tm), lambda m, n: (n, m))],
        out_shape=[jax.ShapeDtypeStruct((2, M, fh), BF16), jax.ShapeDtypeStruct((M, fh), BF16),
                   jax.ShapeDtypeStruct((fh, M), BF16)],
        compiler_params=_params(("parallel", "parallel")),
    )(h2, wi3, wi3)


def _rowwise(fn, nblk, tm, rins, vins, routs, accs, name, carry=None):
    nr, nv, no, na = len(rins), len(vins), len(routs), len(accs)
    nci = 0 if carry is None else len(carry["ins"])
    nco = 0 if carry is None else len(carry["outs"])

    def body(*refs):
        i = pl.program_id(0)
        cin = refs[nr + nv:nr + nv + nci]
        cout = refs[nr + nv + nci + no + na:nr + nv + nci + no + na + nco]
        sems = refs[nr + nv + nci + no + na + nco:]
        if carry is not None:
            @pl.when(i == 0)
            def _():
                carry["start"](cin, cout, *sems)

        outs, accv = fn(i, [r[...] for r in refs[:nr]], [r[...] for r in refs[nr:nr + nv]])
        for r, v, spec in zip(refs[nr + nv + nci:nr + nv + nci + no], outs, routs):
            v = v.astype(r.dtype)
            r[...] = v.T if len(spec) == 3 else v
        arefs = refs[nr + nv + nci + no:nr + nv + nci + no + na]
        if carry is not None:
            @pl.when(i == nblk - 1)
            def _():
                if "mid" in carry:
                    carry["mid"](cin, cout, *sems)
                carry["finish"](cin, cout, *sems)

        if na:
            @pl.when(i == 0)
            def _():
                for a in arefs:
                    a[...] = jnp.zeros_like(a)

            for a, v in zip(arefs, accv):
                a[...] += v

    def row_spec(w, cb, rm):
        if rm is None:
            return pl.BlockSpec((tm, w), lambda i: (i, cb))
        return pl.BlockSpec((tm, w), lambda i: (rm(i), cb))

    in_specs = [row_spec(w, cb, rm) for (_, w, cb, rm) in rins]
    in_specs += [pl.BlockSpec(v.shape, lambda i: (0, 0)) for v in vins]
    def out_of(spec):
        w, dt = spec[:2]
        if len(spec) == 3:
            return pl.BlockSpec((w, tm), lambda i: (0, i)), jax.ShapeDtypeStruct((w, nblk * tm), dt)
        return pl.BlockSpec((tm, w), lambda i: (i, 0)), jax.ShapeDtypeStruct((nblk * tm, w), dt)

    out_specs = [out_of(sp)[0] for sp in routs] + [pl.BlockSpec((1, w), lambda i: (0, 0)) for w in accs]
    out_shape = [out_of(sp)[1] for sp in routs] + [jax.ShapeDtypeStruct((1, w), F32) for w in accs]
    any_spec = pl.BlockSpec(memory_space=pl.ANY)
    extra = {}
    if carry is not None:
        extra["scratch_shapes"] = [pltpu.SemaphoreType.DMA((carry["nsem"],)), pltpu.SemaphoreType.DMA((carry["nsem"],))]
        extra["input_output_aliases"] = {nr + nv + i: no + na + j for i, j in carry["alias"].items()}
    res = pl.pallas_call(
        body, name=name, grid=(nblk,), in_specs=in_specs + [any_spec] * nci, out_specs=out_specs + [any_spec] * nco,
        out_shape=out_shape + ([] if carry is None else list(carry["outs"])),
        compiler_params=_params(("arbitrary",)), **extra,
    )(*[r[0] for r in rins], *vins, *([] if carry is None else carry["ins"]))
    return list(res)


def _colsum(v):
    return jnp.sum(v, axis=0, keepdims=True)


def _rms(v):
    return lax.rsqrt(jnp.mean(v * v, axis=-1, keepdims=True) + EPS)


def _all_gather8(xs):
    m_per, n = xs.shape

    def body(x_ref, out_ref, send_sems, recv_sems, local_sem):
        x, y, c = lax.axis_index("x"), lax.axis_index("y"), lax.axis_index("c")
        me, sibling = (x, y, c), (x, y, 1 - c)
        chips = [(1 - x, y), (x, 1 - y), (1 - x, 1 - y)]

        def rows(px, py, pc):
            return out_ref.at[pl.ds((4 * px + 2 * py + pc) * m_per, m_per), :]

        def copy(k, block, to, src=None):
            return pltpu.make_async_remote_copy(
                src_ref=rows(*block) if src is None else src, dst_ref=rows(*block),
                send_sem=send_sems.at[k], recv_sem=recv_sems.at[k], device_id=to, device_id_type=MESH_ID)

        mine = pltpu.make_async_copy(x_ref, rows(*me), local_sem)
        mine.start()
        first = [copy(0, me, sibling, src=x_ref)]
        first += [copy(1 + j, me, (*chip, c), src=x_ref) for j, chip in enumerate(chips)]
        for cp in first:
            cp.start()
        passed = [copy(4 + j, (*chip, c), sibling) for j, chip in enumerate(chips)]
        for j, chip in enumerate(chips):
            copy(1 + j, (*chip, c), me).wait_recv()
            passed[j].start()
        copy(0, sibling, me).wait_recv()
        for j, chip in enumerate(chips):
            copy(4 + j, (*chip, 1 - c), me).wait_recv()
        for cp in first + passed:
            cp.wait_send()
        mine.wait()

    return pl.pallas_call(
        body, name="all_gather8_%dx%d" % (m_per, n),
        out_shape=jax.ShapeDtypeStruct((8 * m_per, n), xs.dtype),
        in_specs=[pl.BlockSpec(memory_space=pltpu.VMEM)],
        out_specs=pl.BlockSpec(memory_space=pltpu.VMEM),
        scratch_shapes=[pltpu.SemaphoreType.DMA((7,)), pltpu.SemaphoreType.DMA((7,)), pltpu.SemaphoreType.DMA],
    )(xs)


def _mesh_pos():
    x, y, c = lax.axis_index("x"), lax.axis_index("y"), lax.axis_index("c")
    chips = [(1 - x, y), (x, 1 - y), (1 - x, 1 - y)]
    return x, y, c, chips


def _my_shard():
    return 2 * lax.axis_index("x") + lax.axis_index("y")


def _cast_place(w, name):
    r, cw = w.shape
    rh = r // 2
    tm = _tile(rh, max(16, (1 << 20) // (4 * cw)), 16)
    nt = rh // tm

    def body(w_ref, o_ref):
        o_ref[...] = w_ref[...].astype(o_ref.dtype)

    return pl.pallas_call(
        body, name=name, grid=(2, nt),
        in_specs=[pl.BlockSpec((tm, cw), lambda h, i: (h * nt + i, 0))],
        out_specs=pl.BlockSpec((None, None, tm, cw), lambda h, i: (_my_shard(), h, i, 0)),
        out_shape=jax.ShapeDtypeStruct((4, 2, rh, cw), BF16),
        compiler_params=_params(("parallel", "parallel")),
    )(w)


def _exchange_gather(bufs):
    n = len(bufs)

    def copies(out, send_sems, recv_sems, base):
        def copy(i, k, shard, half, to):
            dst = out[i].at[shard, half]
            return pltpu.make_async_remote_copy(
                src_ref=dst, dst_ref=dst, send_sem=send_sems.at[base + 6 * i + k],
                recv_sem=recv_sems.at[base + 6 * i + k], device_id=to, device_id_type=MESH_ID)
        return copy

    def first(copy):
        x, y, c, chips = _mesh_pos()
        return [copy(i, j, 2 * x + y, c, (*chip, c)) for i in range(n) for j, chip in enumerate(chips)]

    def start(cin, out, send_sems, recv_sems, base=0):
        for cp in first(copies(out, send_sems, recv_sems, base)):
            cp.start()

    def passed(copy):
        x, y, c, chips = _mesh_pos()
        return [copy(i, 3 + j, 2 * chip[0] + chip[1], c, (x, y, 1 - c)) for j, chip in enumerate(chips) for i in range(n)]

    def mid(cin, out, send_sems, recv_sems, base=0):
        copy = copies(out, send_sems, recv_sems, base)
        x, y, c, chips = _mesh_pos()
        for j, chip in enumerate(chips):
            for i in range(n):
                copy(i, j, 2 * chip[0] + chip[1], c, (x, y, c)).wait_recv()
        for cp in passed(copy):
            cp.start()

    def finish(cin, out, send_sems, recv_sems, base=0):
        copy = copies(out, send_sems, recv_sems, base)
        x, y, c, chips = _mesh_pos()
        for j, chip in enumerate(chips):
            for i in range(n):
                copy(i, 3 + j, 2 * chip[0] + chip[1], 1 - c, (x, y, c)).wait_recv()
        for cp in first(copy) + passed(copy):
            cp.wait_send()

    return dict(ins=list(bufs), outs=[jax.ShapeDtypeStruct(b.shape, b.dtype) for b in bufs],
                alias={i: i for i in range(n)}, nsem=6 * n, start=start, mid=mid, finish=finish)


def _exchange_gather_via_neighbours(bufs):
    n = len(bufs)

    def geometry():
        x, y, c, _ = _mesh_pos()
        xn, yn = (1 - x, y), (x, 1 - y)
        shard = dict(me=2 * x + y, xn=2 * (1 - x) + y, yn=2 * x + (1 - y), dg=2 * (1 - x) + (1 - y))
        return x, y, c, xn, yn, shard

    def copy(out, sems, base, i, k, dst, to):
        return pltpu.make_async_remote_copy(
            src_ref=dst, dst_ref=dst, send_sem=sems[0].at[base + 7 * i + k], recv_sem=sems[1].at[base + 7 * i + k],
            device_id=to, device_id_type=MESH_ID)

    def quarter(out, i, shard, half, q):
        rq = bufs[i].shape[2] // 2
        return out[i].at[shard, half, pl.ds(q * rq, rq)]

    def plan(out, sems, base):
        x, y, c, xn, yn, sh = geometry()
        me, sib = (x, y, c), (x, y, 1 - c)
        p = dict(first=[], got_x=[], got_y=[], relay=[], got_dg=[], to_sib=[], from_sib=[])
        for i in range(n):
            mine = out[i].at[sh["me"], c]
            p["first"] += [copy(out, sems, base, i, 0, mine, (*xn, c)), copy(out, sems, base, i, 1, mine, (*yn, c))]
            p["got_x"].append(copy(out, sems, base, i, 0, out[i].at[sh["xn"], c], me))
            p["got_y"].append(copy(out, sems, base, i, 1, out[i].at[sh["yn"], c], me))
            p["relay"] += [copy(out, sems, base, i, 2, quarter(out, i, sh["xn"], c, 0), (*yn, c)),
                           copy(out, sems, base, i, 3, quarter(out, i, sh["yn"], c, 1), (*xn, c))]
            p["got_dg"] += [copy(out, sems, base, i, 2, quarter(out, i, sh["dg"], c, 0), me),
                            copy(out, sems, base, i, 3, quarter(out, i, sh["dg"], c, 1), me)]
            for k, who in enumerate(("xn", "yn", "dg")):
                p["to_sib"].append(copy(out, sems, base, i, 4 + k, out[i].at[sh[who], c], sib))
                p["from_sib"].append(copy(out, sems, base, i, 4 + k, out[i].at[sh[who], 1 - c], me))
        return p

    def start(cin, out, send_sems, recv_sems, base=0):
        for cp in plan(out, (send_sems, recv_sems), base)["first"]:
            cp.start()

    def mid(cin, out, send_sems, recv_sems, base=0):
        p = plan(out, (send_sems, recv_sems), base)
        for cp in p["got_x"] + p["got_y"]:
            cp.wait_recv()
        for cp in p["relay"]:
            cp.start()
        for cp in p["got_dg"]:
            cp.wait_recv()
        for cp in p["to_sib"]:
            cp.start()

    def finish(cin, out, send_sems, recv_sems, base=0):
        p = plan(out, (send_sems, recv_sems), base)
        for cp in p["from_sib"]:
            cp.wait_recv()
        for cp in p["first"] + p["relay"] + p["to_sib"]:
            cp.wait_send()

    return dict(ins=list(bufs), outs=[jax.ShapeDtypeStruct(b.shape, b.dtype) for b in bufs],
                alias={i: i for i in range(n)}, nsem=7 * n, start=start, mid=mid, finish=finish)


def _exchange_join(a, b):
    nai, nao = len(a["ins"]), len(a["outs"])

    def start(cin, cout, send_sems, recv_sems, base=0):
        a["start"](cin[:nai], cout[:nao], send_sems, recv_sems, base)
        b["start"](cin[nai:], cout[nao:], send_sems, recv_sems, base + a["nsem"])

    def mid(cin, cout, send_sems, recv_sems, base=0):
        if "mid" in a:
            a["mid"](cin[:nai], cout[:nao], send_sems, recv_sems, base)
        if "mid" in b:
            b["mid"](cin[nai:], cout[nao:], send_sems, recv_sems, base + a["nsem"])

    def finish(cin, cout, send_sems, recv_sems, base=0):
        a["finish"](cin[:nai], cout[:nao], send_sems, recv_sems, base)
        b["finish"](cin[nai:], cout[nao:], send_sems, recv_sems, base + a["nsem"])

    alias = dict(a["alias"])
    alias.update({nai + i: nao + j for i, j in b["alias"].items()})
    return dict(ins=a["ins"] + b["ins"], outs=a["outs"] + b["outs"], alias=alias, nsem=a["nsem"] + b["nsem"],
                start=start, mid=mid, finish=finish)


def _exchange_call(ex, name):
    nci, nco = len(ex["ins"]), len(ex["outs"])

    def body(*refs):
        cin, cout, sems = refs[:nci], refs[nci:nci + nco], refs[nci + nco:]
        ex["start"](cin, cout, *sems)
        if "mid" in ex:
            ex["mid"](cin, cout, *sems)
        ex["finish"](cin, cout, *sems)

    any_spec = pl.BlockSpec(memory_space=pl.ANY)
    return list(pl.pallas_call(
        body, name=name, out_shape=list(ex["outs"]), in_specs=[any_spec] * nci, out_specs=[any_spec] * nco,
        input_output_aliases=dict(ex["alias"]),
        scratch_shapes=[pltpu.SemaphoreType.DMA((ex["nsem"],)), pltpu.SemaphoreType.DMA((ex["nsem"],))],
    )(*ex["ins"]))


def _exchange_swap_other_half(gs):
    n = len(gs)

    def copies(g, land, send_sems, recv_sems, base):
        x, y, c, _ = _mesh_pos()
        return [pltpu.make_async_remote_copy(
            src_ref=g[i].at[:, 1 - c] if len(gs[i].shape) == 4 else g[i], dst_ref=land[i],
            send_sem=send_sems.at[base + i],
            recv_sem=recv_sems.at[base + i], device_id=(x, y, 1 - c), device_id_type=MESH_ID) for i in range(n)]

    def start(g, land, send_sems, recv_sems, base=0):
        for cp in copies(g, land, send_sems, recv_sems, base):
            cp.start()

    def finish(g, land, send_sems, recv_sems, base=0):
        for cp in copies(g, land, send_sems, recv_sems, base):
            cp.wait()

    return dict(ins=list(gs), outs=[jax.ShapeDtypeStruct((4,) + g.shape[-2:], g.dtype) for g in gs],
                alias={}, nsem=n, start=start, finish=finish)


def _exchange_scatter(ps):
    n = len(ps)

    def copies(p, land, send_sems, recv_sems, base):
        x, y, c, chips = _mesh_pos()
        return [pltpu.make_async_remote_copy(
            src_ref=p[i].at[2 * chip[0] + chip[1]], dst_ref=land[i].at[j],
            send_sem=send_sems.at[base + 3 * i + j], recv_sem=recv_sems.at[base + 3 * i + j],
            device_id=(*chip, c), device_id_type=MESH_ID) for i in range(n) for j, chip in enumerate(chips)]

    def start(p, land, send_sems, recv_sems, base=0):
        for cp in copies(p, land, send_sems, recv_sems, base):
            cp.start()

    def finish(p, land, send_sems, recv_sems, base=0):
        for cp in copies(p, land, send_sems, recv_sems, base):
            cp.wait()

    return dict(ins=list(ps), outs=[jax.ShapeDtypeStruct((3,) + p.shape[1:], p.dtype) for p in ps],
                alias={}, nsem=3 * n, start=start, finish=finish)


def _exchange_swap_result(bufs):
    n = len(bufs)

    def copies(out, send_sems, recv_sems, base, half):
        x, y, c, _ = _mesh_pos()
        h = c if half == "mine" else 1 - c
        return [pltpu.make_async_remote_copy(
            src_ref=out[i].at[h], dst_ref=out[i].at[h], send_sem=send_sems.at[base + i],
            recv_sem=recv_sems.at[base + i], device_id=(x, y, 1 - c), device_id_type=MESH_ID) for i in range(n)]

    def start(cin, out, send_sems, recv_sems, base=0):
        for cp in copies(out, send_sems, recv_sems, base, "mine"):
            cp.start()

    def finish(cin, out, send_sems, recv_sems, base=0):
        for cp in copies(out, send_sems, recv_sems, base, "theirs"):
            cp.wait_recv()
        for cp in copies(out, send_sems, recv_sems, base, "mine"):
            cp.wait_send()

    return dict(ins=list(bufs), outs=[jax.ShapeDtypeStruct(b.shape, b.dtype) for b in bufs],
                alias={i: i for i in range(n)}, nsem=n, start=start, finish=finish)


def _add_own_half(g, land, name):
    rh, cw = g.shape[-2:]
    tm = _tile(rh, max(16, (1 << 20) // (4 * cw)), 16)

    def body(g_ref, l_ref, o_ref):
        o_ref[...] = (g_ref[...] + l_ref[...]).astype(o_ref.dtype)

    mine = (pl.BlockSpec((None, None, tm, cw), lambda s, i: (s, lax.axis_index("c"), i, 0)) if g.ndim == 4
            else pl.BlockSpec((None, tm, cw), lambda s, i: (s, i, 0)))
    return pl.pallas_call(
        body, name=name, grid=(4, rh // tm),
        in_specs=[mine,
                  pl.BlockSpec((None, tm, cw), lambda s, i: (s, i, 0))],
        out_specs=pl.BlockSpec((None, tm, cw), lambda s, i: (s, i, 0)),
        out_shape=jax.ShapeDtypeStruct((4, rh, cw), BF16),
        compiler_params=_params(("parallel", "parallel")),
    )(g, land)


def _sum_pieces(part, land, name):
    _, rh, cw = land.shape
    tm = _tile(rh, max(16, (1 << 20) // (4 * cw)), 16)

    def body(p_ref, l_ref, o_ref):
        v = l_ref[...].astype(F32)
        o_ref[...] = (p_ref[...].astype(F32) + v[0]) + (v[1] + v[2])

    return pl.pallas_call(
        body, name=name, grid=(rh // tm,),
        in_specs=[pl.BlockSpec((None, tm, cw), lambda i: (_my_shard(), i, 0)),
                  pl.BlockSpec((3, tm, cw), lambda i: (0, i, 0))],
        out_specs=pl.BlockSpec((None, tm, cw), lambda i: (lax.axis_index("c"), i, 0)),
        out_shape=jax.ShapeDtypeStruct((2, rh, cw), F32),
        compiler_params=_params(("parallel",)),
    )(part, land)


def _sum8(g, name):
    def body(g_ref, o_ref):
        acc = g_ref[0]
        for k in range(1, 8):
            acc = acc + g_ref[k]
        o_ref[...] = acc

    return pl.pallas_call(body, name=name, out_shape=jax.ShapeDtypeStruct(g.shape[1:], F32))(g)


def _silu(v):
    return v * _sigmoid(v)


def _dsilu(v):
    s = _sigmoid(v)
    return s * (1.0 + v * (1.0 - s))


def _ada_fwd(cin, w, b):
    d, ns = w.shape
    tn = _tile(ns, 512)

    def body(c_ref, w_ref, b_ref, o_ref):
        o_ref[...] = _dot(_silu(c_ref[...]), w_ref[...], HI) + b_ref[...]

    return pl.pallas_call(
        body, name="ada_fwd", grid=(ns // tn,),
        in_specs=[pl.BlockSpec(cin.shape, lambda n: (0, 0)), pl.BlockSpec((d, tn), lambda n: (0, n)),
                  pl.BlockSpec((1, tn), lambda n: (0, n))],
        out_specs=pl.BlockSpec((cin.shape[0], tn), lambda n: (0, n)),
        out_shape=jax.ShapeDtypeStruct((cin.shape[0], ns), F32),
        compiler_params=_params(("parallel",)),
    )(cin, w, b)


def _ada_bwd(cin, w, dm):
    d, ns = w.shape
    tn = _tile(ns, 512)

    def body(c_ref, w_ref, d_ref, dw_ref, dc_ref):
        n = pl.program_id(0)

        @pl.when(n == 0)
        def _():
            dc_ref[...] = jnp.zeros_like(dc_ref)

        dw_ref[...] = _dot_tn(_silu(c_ref[...]), d_ref[...], HI)
        dc_ref[...] += _dot_nt(d_ref[...], w_ref[...], HI)

    return pl.pallas_call(
        body, name="ada_bwd", grid=(ns // tn,),
        in_specs=[pl.BlockSpec(cin.shape, lambda n: (0, 0)), pl.BlockSpec((d, tn), lambda n: (0, n)),
                  pl.BlockSpec((cin.shape[0], tn), lambda n: (0, n))],
        out_specs=[pl.BlockSpec((d, tn), lambda n: (0, n)), pl.BlockSpec(cin.shape, lambda n: (0, 0))],
        out_shape=[jax.ShapeDtypeStruct((d, ns), F32), jax.ShapeDtypeStruct(cin.shape, F32)],
        compiler_params=_params(("arbitrary",)),
    )(cin, w, dm)


def _bias_tables(rpb, rows):
    kh = min(WIN_H, rows)
    fold, onehot, in_win = _bias_selectors(kh)
    t = jnp.einsum("hdc,dsj->hsjc", rpb, jnp.asarray(fold), precision=HI)
    t = jnp.einsum("hsjc,cqk->hsqjk", t, jnp.asarray(onehot), precision=HI)
    t = jnp.where(jnp.asarray(in_win)[None, None, :, None, :], t, MASK_VALUE)
    return t.reshape(rpb.shape[0], kh, GRID_W, kh * GRID_W).astype(F32)


def _bias_selectors(kh):
    col = np.arange(GRID_W)
    col_start = np.clip(col - WIN_W // 2, 0, GRID_W - WIN_W)
    in_win = (col[None, :] >= col_start[:, None]) & (col[None, :] < col_start[:, None] + WIN_W)
    dc_idx = np.clip(col[None, :] - col[:, None], 1 - WIN_W, WIN_W - 1) + WIN_W - 1
    onehot = np.zeros((2 * WIN_W - 1, GRID_W, GRID_W), np.float32)
    qq, kk = np.nonzero(in_win)
    onehot[dc_idx[qq, kk], qq, kk] = 1.0
    fold = np.zeros((2 * WIN_H - 1, kh, kh), np.float32)
    for sh in range(kh):
        for j in range(kh):
            fold[j - sh + WIN_H - 1, sh, j] = 1.0
    return fold, onehot, in_win


def _mm_f32(a, b, name):
    M, K = a.shape
    N = b.shape[1]
    tm = _tile(M, 256, 8)

    def body(a_ref, b_ref, o_ref):
        o_ref[...] = _dot(a_ref[...], b_ref[...], HI)

    return pl.pallas_call(
        body, name=name, grid=(M // tm,),
        in_specs=[pl.BlockSpec((tm, K), lambda i: (i, 0)), pl.BlockSpec((K, N), lambda i: (0, 0))],
        out_specs=pl.BlockSpec((tm, N), lambda i: (i, 0)),
        out_shape=jax.ShapeDtypeStruct((M, N), F32),
        compiler_params=_params(("parallel",)),
    )(a, b)


def _bias_tables_transpose(dbias, rows):
    kh = min(WIN_H, rows)
    nh = dbias.shape[0]
    fold, onehot, _ = _bias_selectors(kh)
    ndc = onehot.shape[0]
    sel = np.zeros((GRID_W * GRID_W, LANES), np.float32)
    sel[:, :ndc] = onehot.reshape(ndc, -1).T
    x = dbias.reshape(nh, kh, GRID_W, kh, GRID_W).transpose(0, 1, 3, 2, 4).reshape(nh * kh * kh, GRID_W * GRID_W)
    z = _mm_f32(x, jnp.asarray(sel), "rpb_fold")[:, :ndc].reshape(nh, kh, kh, ndc)
    return jnp.einsum("dsj,hsjc->hdc", jnp.asarray(fold), z, precision=HI)


def _na_geometry(S):
    rows = S // GRID_W
    kh = min(WIN_H, rows)

    def row_start(r):
        return jnp.clip(r - kh // 2, 0, rows - kh)

    return rows, kh, row_start


def _na_by_head(ref, rr, lane):
    t = ref[rr * GRID_W:(rr + 1) * GRID_W, :]
    zero = jnp.zeros_like(t)
    return jnp.concatenate([jnp.where(lane < NA_HEAD_DIM, t, zero), jnp.where(lane >= NA_HEAD_DIM, t, zero)], axis=0)


def _na_pick_head(t2, lane):
    return jnp.where(lane < NA_HEAD_DIM, t2[:GRID_W], t2[GRID_W:])


def _na_scores(q_ref, k_ref, b_ref, i, nrs, nb, S, L, row_start, lane):
    qh = jnp.concatenate([_na_by_head(q_ref, rr, lane) for rr in range(nrs)], axis=0)
    sc = _dot_nt(qh, k_ref[pl.ds(S, L), :])
    starts, shifts, sb = [], [], []
    for rr in range(nrs):
        r = i * nrs + rr
        rs = row_start(r)
        starts.append(pl.multiple_of(rs * GRID_W, GRID_W))
        shifts.append(r - rs)
        bias = jnp.concatenate([b_ref[0, r - rs], b_ref[1, r - rs]], axis=0)
        sb.append(_dot_nt(qh[rr * 2 * GRID_W:(rr + 1) * 2 * GRID_W], k_ref[pl.ds(starts[-1], nb), :]) + bias)
    return qh, jnp.concatenate(sb, axis=0), sc, starts, shifts


def _na_fwd(qs, kb, vb, bias, S, L):
    T, naw = qs.shape
    rows, kh, row_start = _na_geometry(S)
    nb = kh * GRID_W
    npair = naw // LANES

    nrs = min(NA_ROWS_PER_STEP, rows)
    assert rows % nrs == 0

    def body(q_ref, k_ref, v_ref, b_ref, o_ref, lse_ref):
        i = pl.program_id(1)
        lane = lax.broadcasted_iota(jnp.int32, (GRID_W, LANES), 1)
        _, sb, sc, starts, _ = _na_scores(q_ref, k_ref, b_ref, i, nrs, nb, S, L, row_start, lane)
        m = jnp.maximum(jnp.max(sb, axis=-1, keepdims=True), jnp.max(sc, axis=-1, keepdims=True))
        pb, pc = jnp.exp(sb - m), jnp.exp(sc - m)
        l = jnp.sum(pb, axis=-1, keepdims=True) + jnp.sum(pc, axis=-1, keepdims=True)
        inv = 1.0 / l
        pb16, pc16 = (pb * inv).astype(BF16), (pc * inv).astype(BF16)
        oc = _dot(pc16, v_ref[pl.ds(S, L), :])
        lse = jnp.broadcast_to(m + jnp.log(l), oc.shape)
        for rr in range(nrs):
            two = slice(rr * 2 * GRID_W, (rr + 1) * 2 * GRID_W)
            rsl = slice(rr * GRID_W, (rr + 1) * GRID_W)
            o2 = oc[two] + _dot(pb16[two], v_ref[pl.ds(starts[rr], nb), :])
            o_ref[rsl, :] = _na_pick_head(o2, lane).astype(o_ref.dtype)
            lse_ref[rsl, :] = _na_pick_head(lse[two], lane)

    blk = pl.BlockSpec((nrs * GRID_W, LANES), lambda p, i: (i, p))
    col = pl.BlockSpec((T, LANES), lambda p, i: (0, p))
    return pl.pallas_call(
        body, name="na_fwd", grid=(npair, rows // nrs),
        in_specs=[blk, col, col, pl.BlockSpec((2, kh, GRID_W, nb), lambda p, i: (p, 0, 0, 0))],
        out_specs=[blk, blk],
        out_shape=[jax.ShapeDtypeStruct((S, naw), BF16), jax.ShapeDtypeStruct((S, naw), F32)],
        compiler_params=_params(("parallel", "arbitrary")),
    )(qs, kb, vb, bias)


def _na_bwd(qs, kb, vb, bias, do, o, lse, S, L):
    T, naw = qs.shape
    rows, kh, row_start = _na_geometry(S)
    nb = kh * GRID_W
    npair = naw // LANES

    nrs = min(NA_ROWS_PER_STEP, rows)
    assert rows % nrs == 0

    def body(q_ref, k_ref, v_ref, b_ref, do_ref, o_ref, lse_ref, dq_ref, dk_ref, dv_ref, db_ref):
        i = pl.program_id(1)

        @pl.when(i == 0)
        def _():
            dk_ref[...] = jnp.zeros_like(dk_ref)
            dv_ref[...] = jnp.zeros_like(dv_ref)
            db_ref[...] = jnp.zeros_like(db_ref)

        lane = lax.broadcasted_iota(jnp.int32, (GRID_W, LANES), 1)
        qh, sb, sc, starts, shifts = _na_scores(q_ref, k_ref, b_ref, i, nrs, nb, S, L, row_start, lane)
        doh = jnp.concatenate([_na_by_head(do_ref, rr, lane) for rr in range(nrs)], axis=0)
        o2 = jnp.concatenate([o_ref[rr * GRID_W:(rr + 1) * GRID_W, :] for rr in range(nrs) for _ in range(2)], axis=0)
        lse = jnp.concatenate([lse_ref[rr * GRID_W:(rr + 1) * GRID_W, :][:, hh * NA_HEAD_DIM:hh * NA_HEAD_DIM + 1]
                               for rr in range(nrs) for hh in range(2)], axis=0)
        pb, pc = jnp.exp(sb - lse), jnp.exp(sc - lse)
        delta = jnp.sum(doh.astype(F32) * o2.astype(F32), axis=-1, keepdims=True)
        dpb = jnp.concatenate([_dot_nt(doh[rr * 2 * GRID_W:(rr + 1) * 2 * GRID_W], v_ref[pl.ds(starts[rr], nb), :])
                               for rr in range(nrs)], axis=0)
        dsb = pb * (dpb - delta)
        dsc = pc * (_dot_nt(doh, v_ref[pl.ds(S, L), :]) - delta)
        dsb16, dsc16, pb16, pc16 = dsb.astype(BF16), dsc.astype(BF16), pb.astype(BF16), pc.astype(BF16)
        dqc = _dot(dsc16, k_ref[pl.ds(S, L), :])
        dk_ref[pl.ds(S, L), :] += _dot_tn(dsc16, qh)
        dv_ref[pl.ds(S, L), :] += _dot_tn(pc16, doh)
        for rr in range(nrs):
            two = slice(rr * 2 * GRID_W, (rr + 1) * 2 * GRID_W)
            band = pl.ds(starts[rr], nb)
            dq2 = dqc[two] + _dot(dsb16[two], k_ref[band, :])
            dq_ref[rr * GRID_W:(rr + 1) * GRID_W, :] = _na_pick_head(dq2, lane)
            dk_ref[band, :] += _dot_tn(dsb16[two], qh[two])
            dv_ref[band, :] += _dot_tn(pb16[two], doh[two])
            for hh in range(2):
                db_ref[hh, shifts[rr]] += dsb[(2 * rr + hh) * GRID_W:(2 * rr + hh + 1) * GRID_W]

    blk = pl.BlockSpec((nrs * GRID_W, LANES), lambda p, r: (r, p))
    col = pl.BlockSpec((T, LANES), lambda p, r: (0, p))
    return pl.pallas_call(
        body, name="na_bwd", grid=(npair, rows // nrs),
        in_specs=[blk, col, col, pl.BlockSpec((2, kh, GRID_W, nb), lambda p, r: (p, 0, 0, 0)), blk, blk, blk],
        out_specs=[blk, col, col, pl.BlockSpec((2, kh, GRID_W, nb), lambda p, r: (p, 0, 0, 0))],
        out_shape=[jax.ShapeDtypeStruct((S, naw), F32), jax.ShapeDtypeStruct((T, naw), F32),
                   jax.ShapeDtypeStruct((T, naw), F32), jax.ShapeDtypeStruct(bias.shape, F32)],
        compiler_params=_params(("parallel", "arbitrary")),
    )(qs, kb, vb, bias, do, o, lse)


def _hg_cols(naw, hgf, rev):
    qcol = (3 * naw) // hgf
    fcol = (3 * naw + hgf * (2 if rev else 1)) // hgf
    icol = (3 * naw + 3 * hgf) // hgf
    return qcol, fcol, icol


def _hg_chunk_order(S, L, rev):
    ncl, ncc = S // HG_CHUNK, L // HG_CHUNK
    nc = ncl + ncc

    def chunk_of(i):
        if rev:
            return nc - 1 - i
        return jnp.where(i < ncc, ncl + i, i - ncc)

    return nc, ncl, chunk_of


def _hg_gates(q, z, lb, rev):
    row = lax.broadcasted_iota(jnp.int32, (HG_CHUNK, HG_CHUNK), 0)
    colm = lax.broadcasted_iota(jnp.int32, (HG_CHUNK, HG_CHUNK), 1)
    tri = (colm >= row) if rev else (row >= colm)
    trif = tri.astype(F32)
    sig = _sigmoid(z)
    f = lb + (1.0 - lb) * sig
    lf = jnp.log(f)
    k = 1.0 - f
    cum = _dot(trif, lf, HI)
    mid = cum[HG_CHUNK // 2:HG_CHUNK // 2 + 1, :]
    last = cum[0:1, :] if rev else cum[HG_CHUNK - 1:HG_CHUNK, :]
    eq = jnp.exp(jnp.clip(cum - mid, -EXP_CLAMP, EXP_CLAMP))
    ek = jnp.exp(jnp.clip(mid - cum, -EXP_CLAMP, EXP_CLAMP))
    return tri, trif, sig, f, k, cum, last, eq, ek


def _hg_fwd(u, lbr, S, L, naw, hgf, rev):
    T = S + L
    nh = hgf // HG_DIM
    nc, ncl, chunk_of = _hg_chunk_order(S, L, rev)
    qcol, fcol, icol = _hg_cols(naw, hgf, rev)

    def step(i, q_ref, z_ref, v_ref, lb_ref, o_ref, st_ref, state):
        @pl.when(i == 0)
        def _():
            state[...] = jnp.zeros_like(state)

        q, z, v = q_ref[...], z_ref[...], v_ref[...]
        tri, _, _, _, k, cum, last, eq, ek = _hg_gates(q, z, lb_ref[...], rev)
        qe, ke = (q * eq).astype(BF16), (k * ek).astype(BF16)
        qd, kd = (q * jnp.exp(cum)).astype(BF16), (k * jnp.exp(last - cum)).astype(BF16)
        v16, el = v.astype(BF16), jnp.exp(last)
        for h in range(nh):
            sl = slice(h * HG_DIM, (h + 1) * HG_DIM)
            a = jnp.where(tri, _dot_nt(qe[:, sl], ke[:, sl]), 0.0)
            s0 = state[h]
            st_ref[h] = s0
            o_ref[:, sl] = _dot(a.astype(BF16), v16[:, sl]) + _dot_nt(qd[:, sl], s0.astype(BF16))
            state[h] = s0 * el[:, sl] + _dot_tn(v16[:, sl], kd[:, sl])

    def blk(cb):
        return pl.BlockSpec((HG_CHUNK, hgf), lambda i: (chunk_of(i), cb))

    return dict(
        step=step, nc=nc, operands=[u, u, u, lbr],
        in_specs=[blk(qcol), blk(fcol), blk(icol), pl.BlockSpec((1, hgf), lambda i: (0, 0))],
        out_specs=[pl.BlockSpec((HG_CHUNK, hgf), lambda i: (chunk_of(i), 0)),
                   pl.BlockSpec((None, nh, HG_DIM, HG_DIM), lambda i: (chunk_of(i), 0, 0, 0))],
        out_shape=[jax.ShapeDtypeStruct((T, hgf), F32), jax.ShapeDtypeStruct((nc, nh, HG_DIM, HG_DIM), F32)],
        scratch=[pltpu.VMEM((nh, HG_DIM, HG_DIM), F32)])


def _hg_both(parts, name):
    nin = [len(p["in_specs"]) for p in parts]
    nout = [len(p["out_specs"]) for p in parts]
    nscr = [len(p["scratch"]) for p in parts]

    def body(*refs):
        i = pl.program_id(0)
        ins, outs, scr = refs[:sum(nin)], refs[sum(nin):sum(nin) + sum(nout)], refs[sum(nin) + sum(nout):]
        for d, p in enumerate(parts):
            p["step"](i, *ins[sum(nin[:d]):sum(nin[:d + 1])], *outs[sum(nout[:d]):sum(nout[:d + 1])],
                      *scr[sum(nscr[:d]):sum(nscr[:d + 1])])

    res = pl.pallas_call(
        body, name=name, grid=(parts[0]["nc"],),
        in_specs=[sp for p in parts for sp in p["in_specs"]],
        out_specs=[sp for p in parts for sp in p["out_specs"]],
        out_shape=[sh for p in parts for sh in p["out_shape"]],
        scratch_shapes=[sc for p in parts for sc in p["scratch"]],
        compiler_params=_params(("arbitrary",)),
    )(*[op for p in parts for op in p["operands"]])
    return [list(res[sum(nout[:d]):sum(nout[:d + 1])]) for d in range(len(parts))]


def _hg_bwd(u, lbr, st, do, S, L, naw, hgf, rev):
    T = S + L
    nh = hgf // HG_DIM
    nc, ncl, chunk_fwd = _hg_chunk_order(S, L, rev)
    qcol, fcol, icol = _hg_cols(naw, hgf, rev)

    def chunk_of(j):
        return chunk_fwd(nc - 1 - j)

    def step(j, q_ref, z_ref, v_ref, lb_ref, st_ref, do_ref, dq_ref, dz_ref, dv_ref, dlb_ref, dstate,
             dqe_s, dke_s, dqd_s, dkd_s, dl_s):
        @pl.when(j == 0)
        def _():
            dstate[...] = jnp.zeros_like(dstate)
            dlb_ref[...] = jnp.zeros_like(dlb_ref)

        q, z, v = q_ref[...], z_ref[...], v_ref[...]
        lb = lb_ref[...]
        tri, trif, sig, f, k, cum, last, eq, ek = _hg_gates(q, z, lb, rev)
        ec, el, ekd = jnp.exp(cum), jnp.exp(last), jnp.exp(last - cum)
        qe, ke, qd, kd = q * eq, k * ek, q * ec, k * ekd
        qd16, kd16 = qd.astype(BF16), kd.astype(BF16)
        dout = jnp.where(chunk_of(j) < ncl, do_ref[...], 0.0)
        qe2, ke2, v16, dout16 = _split2(qe), _split2(ke), v.astype(BF16), dout.astype(BF16)
        for h in range(nh):
            sl = slice(h * HG_DIM, (h + 1) * HG_DIM)
            qeh, keh = [(t[0][:, sl], t[1][:, sl]) for t in (qe2, ke2)]
            a = jnp.where(tri, _dot_nt(qeh[0], keh[0]), 0.0).astype(BF16)
            s0 = st_ref[h]
            ds1 = dstate[h]
            s016, ds116 = s0.astype(BF16), ds1.astype(BF16)
            dv_ref[:, sl] = (_dot_tn(a, dout16[:, sl]) + _dot_nt(kd16[:, sl], ds116)).astype(dv_ref.dtype)
            da2 = _split2(jnp.where(tri, _dot_nt(dout16[:, sl], v16[:, sl]), 0.0))
            dqe_s[:, sl] = _dot_x3(_dot, da2, keh)
            dke_s[:, sl] = _dot_x3(_dot_tn, da2, qeh)
            dqd_s[:, sl] = _dot(dout16[:, sl], s016)
            dkd_s[:, sl] = _dot(v16[:, sl], ds116)
            dl_s[:, sl] = _colsum(ds1 * s0)
            dstate[h] = _dot_tn(dout16[:, sl], qd16[:, sl]) + ds1 * el[:, sl]
        dqe, dke, dqd, dkd = dqe_s[...], dke_s[...], dqd_s[...], dkd_s[...]
        dq_ref[...] = (dqe * eq + dqd * ec).astype(dq_ref.dtype)
        dk = dke * ek + dkd * ekd
        dcum = dqe * qe - dke * ke + dqd * qd - dkd * kd
        dlast = _colsum(dkd * kd) + el * dl_s[...]
        dlf = _dot_tn(trif, dcum, HI) + dlast
        df = dlf / f - dk
        dz_ref[...] = (df * (1.0 - lb) * sig * (1.0 - sig)).astype(dz_ref.dtype)
        dlb_ref[...] += _colsum(df * (1.0 - sig))

    def blk(cb):
        return pl.BlockSpec((HG_CHUNK, hgf), lambda j: (chunk_of(j), cb))

    oblk = pl.BlockSpec((HG_CHUNK, hgf), lambda j: (chunk_of(j), 0))
    wide = pltpu.VMEM((HG_CHUNK, hgf), F32)
    return dict(
        step=step, nc=nc, operands=[u, u, u, lbr, st, do],
        in_specs=[blk(qcol), blk(fcol), blk(icol), pl.BlockSpec((1, hgf), lambda j: (0, 0)),
                  pl.BlockSpec((None, nh, HG_DIM, HG_DIM), lambda j: (chunk_of(j), 0, 0, 0)),
                  pl.BlockSpec((HG_CHUNK, hgf), lambda j: (jnp.minimum(chunk_of(j), ncl - 1), 0))],
        out_specs=[oblk, oblk, oblk, pl.BlockSpec((1, hgf), lambda j: (0, 0))],
        out_shape=[jax.ShapeDtypeStruct((T, hgf), BF16)] * 3 + [jax.ShapeDtypeStruct((1, hgf), F32)],
        scratch=[pltpu.VMEM((nh, HG_DIM, HG_DIM), F32), wide, wide, wide, wide, pltpu.VMEM((1, hgf), F32)])


def _adamw(w, g, m, v, name):
    shape = w.shape
    if w.ndim != 2 or shape[0] % 8 or shape[1] % LANES:
        w, g, m, v = [a.reshape(1, -1) for a in (w, g, m, v)]
    r, cw = w.shape
    tm = _tile(r, max(8, (1 << 19) // cw), 8) if r % 8 == 0 else r
    c1 = 1.0 / (1.0 - ADAM_B1 ** ADAM_STEP)
    c2 = 1.0 / (1.0 - ADAM_B2 ** ADAM_STEP)

    def body(w_ref, g_ref, m_ref, v_ref, d_ref, nm_ref, nv_ref):
        gg = g_ref[...]
        nm = ADAM_B1 * m_ref[...] + (1.0 - ADAM_B1) * gg
        nv = ADAM_B2 * v_ref[...] + (1.0 - ADAM_B2) * (gg * gg)
        d_ref[...] = -ADAM_LR * ((nm * c1) / (jnp.sqrt(nv * c2) + ADAM_EPS) + ADAM_WD * w_ref[...])
        nm_ref[...] = nm
        nv_ref[...] = nv

    spec = pl.BlockSpec((tm, cw), lambda i: (i, 0))
    outs = pl.pallas_call(
        body, name=name, grid=(r // tm,), in_specs=[spec] * 4, out_specs=[spec] * 3,
        out_shape=[jax.ShapeDtypeStruct((r, cw), F32)] * 3,
        compiler_params=_params(("parallel",)),
    )(w, g, m, v)
    return [o.reshape(shape) for o in outs]


def kernel(x, c, ctx, c_ctx, w_ada, b_ada, norm1_g, w_in, na_rpb, hg_lb_logits, hg_norm_g, w_pa, w_pb, w_out, norm2_g, w_ffn_in, w_ffn_out, final_g, loss_target, m_c_ctx, m_w_ada, m_b_ada, m_norm1_g, m_w_in, m_na_rpb, m_hg_lb_logits, m_hg_norm_g, m_w_pa, m_w_pb, m_w_out, m_norm2_g, m_w_ffn_in, m_w_ffn_out, m_final_g, v_c_ctx, v_w_ada, v_b_ada, v_norm1_g, v_w_in, v_na_rpb, v_hg_lb_logits, v_hg_norm_g, v_w_pa, v_w_pb, v_w_out, v_norm2_g, v_w_ffn_in, v_w_ffn_out, v_final_g):
    xi, yi, ci = lax.axis_index("x"), lax.axis_index("y"), lax.axis_index("c")
    sidx = 2 * xi + yi
    eidx = 4 * xi + 2 * yi + ci

    S, D = x.shape[1], x.shape[2]
    L = ctx.shape[1]
    T = S + L
    naw = NA_HEADS * NA_HEAD_DIM
    hgf = HG_HEADS * HG_DIM
    inw = 3 * naw + 5 * hgf + 2 * D
    fh = w_ffn_out.shape[1] * 4
    ads = w_ada.shape[2]
    fs = hg_lb_logits.shape[2]
    rows = S // GRID_W
    tr = _tile(L, 256)
    nlat, nall = S // tr, T // tr
    assert naw == hgf and D % naw == 0 and S % tr == 0 and 2 * hgf <= D

    pack0 = jnp.concatenate([c, jnp.pad(hg_lb_logits.reshape(1, -1), ((0, 0), (0, D - 4 * fs))),
                             jnp.zeros((6, D), F32)], axis=0)
    g0 = _all_gather8(pack0).reshape(8, 8, D)
    cs = g0[:, 0]
    lbl = g0[::2, 1, :4 * fs].reshape(4, 2, 2, fs).transpose(1, 2, 0, 3).reshape(2, 2, 4 * fs)
    p_lb = jax.nn.softmax(lbl, axis=0)
    lb = p_lb[0]
    lbb = [lb[d].reshape(1, hgf) for d in range(2)]

    cin = jnp.concatenate([cs, c_ctx[None], jnp.zeros((7, D), F32)], axis=0)
    b_sh = lax.dynamic_slice(b_ada, (0, sidx * ads), (1, ads))
    modp = _ada_fwd(cin, w_ada[0], b_sh)
    modfull = _all_gather8(modp).reshape(8, 16, ads)[::2].transpose(1, 0, 2).reshape(16, 4 * ads)
    mod_e = jnp.pad(lax.dynamic_index_in_dim(modfull, eidx, 0, keepdims=False).reshape(N_MOD, D), ((0, 2), (0, 0)))
    mod_c = jnp.pad(modfull[8].reshape(N_MOD, D), ((0, 2), (0, 0)))

    names = ["w_in", "w_pa", "w_pb", "w_out", "w_ffn_in", "w_ffn_out"]
    placed = [_cast_place(w[0], "cast_" + nm)
              for w, nm in zip((w_in, w_pa, w_pb, w_out, w_ffn_in, w_ffn_out), names)]
    def shards(g):
        return g.reshape(4, 2 * g.shape[2], g.shape[3])

    x2d, ctx2d = x[0], ctx[0]

    def f_ln1(i, rv, vv):
        xl, xc = rv
        g, me, mc = vv
        isc = i >= nlat
        xt = jnp.where(isc, xc, xl)
        sh = jnp.where(isc, mc[0:1], me[0:1])
        sc = jnp.where(isc, mc[1:2], me[1:2])
        h = xt * _rms(xt) * g * (1.0 + sc) + sh
        return [h, h], []

    hb, hbt, win_all = _rowwise(f_ln1, nall, tr, [(x2d, D, 0, lambda i: jnp.minimum(i, nlat - 1)),
                                                 (ctx2d, D, 0, lambda i: jnp.maximum(i - nlat, 0))],
                                [norm1_g, mod_e, mod_c], [(D, BF16), (D, BF16, "T")], [], "ln1",
                                carry=_exchange_gather_via_neighbours(placed[:1]))
    win3 = shards(win_all)
    u, gathered = _mm_nn(hb, win3, F32, "mm_in", carry=_exchange_gather(placed[1:]))
    wpa3, wpb3, wout3, wi3, wfo3 = [shards(g) for g in gathered]
    wout1 = wout3.reshape(1, D, D)
    wfo1 = wfo3.reshape(1, fh, D)

    scale = NA_HEAD_DIM ** -0.5

    def f_qkv(i, rv, vv):
        q, k, v = rv
        return [q * scale, k, v], []

    qs, kb, vb = _rowwise(f_qkv, nall, tr, [(u, naw, 0, None), (u, naw, 1, None), (u, naw, 2, None)], [],
                          [(naw, BF16)] * 3, [], "qkv_cast")
    bias = _bias_tables(na_rpb[0], rows)
    o_na, lse = _na_fwd(qs, kb, vb, bias, S, L)

    (o_f, st_f), (o_b, st_b) = _hg_both(
        [_hg_fwd(u, lbb[0], S, L, naw, hgf, False), _hg_fwd(u, lbb[1], S, L, naw, hgf, True)], "hg_fwd")

    hgn = jnp.tile(hg_norm_g, (1, HG_HEADS))
    hog_cb = (3 * naw + 4 * hgf) // hgf
    ga_cb = (3 * naw + 5 * hgf) // D
    gb_cb = ga_cb + 1

    def heads_rms(o):
        return jnp.concatenate([jnp.broadcast_to(_rms(o[:, h * HG_DIM:(h + 1) * HG_DIM]), (o.shape[0], HG_DIM))
                                for h in range(HG_HEADS)], axis=1)

    def f_readout(i, rv, vv):
        of, ob_, hog = rv
        g, = vv
        o = of + ob_
        return [o * heads_rms(o) * g * _silu(hog)], []

    ob, = _rowwise(f_readout, nlat, tr, [(o_f, hgf, 0, None), (o_b, hgf, 0, None), (u, hgf, hog_cb, None)],
                   [hgn], [(hgf, BF16)], [], "hg_readout")

    ya = _mm_nn(o_na, wpa3, BF16, "mm_pa")
    yb = _mm_nn(ob, wpb3, BF16, "mm_pb")

    def f_merge(i, rv, vv):
        ya_, yb_, ga, gb = rv
        return [_sigmoid(ga) * ya_ + _sigmoid(gb) * yb_], []

    yv, = _rowwise(f_merge, nlat, tr, [(ya, D, 0, None), (yb, D, 0, None), (u, D, ga_cb, None), (u, D, gb_cb, None)],
                   [], [(D, BF16)], [], "merge")
    z = _mm_nn(yv, wout1, F32, "mm_out")

    def f_res1(i, rv, vv):
        xt, zt = rv
        g, me = vv
        x1 = xt + me[2:3] * zt
        h = x1 * _rms(x1) * g * (1.0 + me[4:5]) + me[3:4]
        return [x1, h, h], []

    x1, h2, h2t = _rowwise(f_res1, nlat, tr, [(x2d, D, 0, None), (z, D, 0, None)], [norm2_g, mod_e],
                           [(D, F32), (D, BF16), (D, BF16, "T")], [], "res1_ln2")
    au3, sw, swt = _ffn_in_fused(h2, wi3, "mm_ffn_in")
    ff =_mm_nn(sw, wfo1, F32, "mm_ffn_out")

    fg = final_g.reshape(1, D)

    def f_final(i, rv, vv):
        x1t, ft, tg = rv
        g, me = vv
        x2 = x1t + me[5:6] * ft
        r3 = _rms(x2)
        xn = x2 * r3
        err = xn * g - tg
        dyy = err * (1.0 / D)
        dxn = dyy * g
        dx2 = r3 * (dxn - xn * jnp.mean(dxn * xn, axis=-1, keepdims=True))
        return [dx2, dx2 * me[5:6]], [_colsum(err * err), _colsum(dyy * xn), _colsum(dx2 * ft)]

    dx2, dfb, loss_cols, dfg, dg2 = _rowwise(
        f_final, nlat, tr, [(x1, D, 0, None), (ff, D, 0, None), (loss_target[0], D, 0, None)], [fg, mod_e],
        [(D, F32), (D, BF16)], [D, D, D], "final_loss")

    def dswiglu_epi(tm, tn):
        blk = pl.BlockSpec((2, tm, tn), lambda m, n, k: (0, m, n))

        def fn(d, ins):
            a, uu = ins[0][0].astype(F32), ins[0][1].astype(F32)
            return [(d * uu * _dsilu(a), d * _silu(a))]

        return dict(ins=[(au3, blk)], outs=[(jax.ShapeDtypeStruct((2, S, fh), BF16), blk)], fn=fn)

    dau3, = _mm_nt(dfb, wfo1, BF16, "mm_d_sw", epi=dswiglu_epi, tn_target=512)
    g_wfo = _mm_tn(swt, dfb, 1, "mm_dw_ffn_out", a_is_t=True).reshape(4, 2, fh // 8, D)
    dh2 = _mm_nt(dau3, wi3, F32, "mm_d_h2")
    g_wi = _mm_tn(h2t, dau3, 4, "mm_dw_ffn_in", a_is_t=True)

    def f_ln2_bwd(i, rv, vv):
        dh, x1t, dx2t, zt = rv
        g, me = vv
        r2 = _rms(x1t)
        xn = x1t * r2
        dxn = dh * g * (1.0 + me[4:5])
        dx1 = dx2t + r2 * (dxn - xn * jnp.mean(dxn * xn, axis=-1, keepdims=True))
        return ([dx1, dx1 * me[2:3]],
                [_colsum(dh), _colsum(dh * xn * g), _colsum(dh * xn * (1.0 + me[4:5])), _colsum(dx1 * zt)])

    dx1, dzb, dsh2, dsc2, dn2g, dg1 = _rowwise(
        f_ln2_bwd, nlat, tr, [(dh2, D, 0, None), (x1, D, 0, None), (dx2, D, 0, None), (z, D, 0, None)],
        [norm2_g, mod_e], [(D, F32), (D, BF16)], [D, D, D, D], "ln2_bwd")

    g_wout = _mm_tn(yv, dzb, 1, "mm_dw_out").reshape(4, 2, D // 8, D)

    def dmerge_epi(tm, tn):
        blk = pl.BlockSpec((tm, tn), lambda m, n, k: (m, n))

        def gate(cb):
            return pl.BlockSpec((tm, tn), lambda m, n, k: (m, cb * (D // tn) + n))

        def fn(d, ins):
            ya_, yb_, ga, gb = ins
            sa, sb_ = _sigmoid(ga), _sigmoid(gb)
            return [d * sa, d * sb_, d * ya_ * sa * (1.0 - sa), d * yb_ * sb_ * (1.0 - sb_)]

        return dict(ins=[(ya, blk), (yb, blk), (u, gate(ga_cb)), (u, gate(gb_cb))],
                    outs=[(jax.ShapeDtypeStruct((S, D), BF16), blk)] * 4, fn=fn)

    dya, dyb, dga, dgb = _mm_nt(dzb, wout1, BF16, "mm_d_y", epi=dmerge_epi, tn_target=512)
    d_ona = _mm_nt(dya, wpa3, BF16, "mm_d_ona")
    d_ob = _mm_nt(dyb, wpb3, F32, "mm_d_ob")
    g_wpa = _mm_tn(o_na, dya, 4, "mm_dw_pa")
    g_wpb = _mm_tn(ob, dyb, 4, "mm_dw_pb")

    def f_dreadout(i, rv, vv):
        d, of, ob_, hog = rv
        g, = vv
        o = of + ob_
        on = o * heads_rms(o)
        t = d * _silu(hog) * g
        mt = jnp.concatenate([jnp.broadcast_to(jnp.mean((t * on)[:, h * HG_DIM:(h + 1) * HG_DIM], axis=-1,
                                                        keepdims=True), (o.shape[0], HG_DIM))
                              for h in range(HG_HEADS)], axis=1)
        do_ = heads_rms(o) * (t - on * mt)
        return [do_, d * on * g * _dsilu(hog)], [_colsum(d * _silu(hog) * on)]

    do_hg, dhog, dhgn = _rowwise(
        f_dreadout, nlat, tr, [(d_ob, hgf, 0, None), (o_f, hgf, 0, None), (o_b, hgf, 0, None),
                               (u, hgf, hog_cb, None)], [hgn], [(hgf, F32), (hgf, BF16)], [hgf], "hg_readout_bwd")

    (dq_f, dz_f, dv_f, dlb_f), (dq_b, dz_b, dv_b, dlb_b) = _hg_both(
        [_hg_bwd(u, lbb[0], st_f, do_hg, S, L, naw, hgf, False), _hg_bwd(u, lbb[1], st_b, do_hg, S, L, naw, hgf, True)],
        "hg_bwd")
    dq_na, dk_na, dv_na, dbias = _na_bwd(qs, kb, vb, bias, d_ona, o_na, lse, S, L)

    ta = _tile(L, 128)
    nla, naa = S // ta, T // ta
    lat = lambda i: jnp.minimum(i, nla - 1)

    def f_assemble(i, rv, vv):
        dqn, dk, dv, dqf, dqb, dzf, dzb_, dvf, dvb, dho, dga_, dgb_ = rv
        keep = (i < nla).astype(F32)
        f32 = lambda t: t.astype(F32)
        return [jnp.concatenate([dqn * (scale * keep), dk, dv, f32(dqf) + f32(dqb), f32(dzf), f32(dzb_),
                                 f32(dvf) + f32(dvb),
                                 dho.astype(F32) * keep, dga_.astype(F32) * keep, dgb_.astype(F32) * keep],
                                axis=1)], []

    du, = _rowwise(
        f_assemble, naa, ta,
        [(dq_na, naw, 0, lat), (dk_na, naw, 0, None), (dv_na, naw, 0, None), (dq_f, hgf, 0, None),
         (dq_b, hgf, 0, None), (dz_f, hgf, 0, None), (dz_b, hgf, 0, None), (dv_f, hgf, 0, None),
         (dv_b, hgf, 0, None), (dhog, hgf, 0, lat), (dga, D, 0, lat), (dgb, D, 0, lat)],
        [], [(inw, BF16)], [], "assemble_du")

    def add_half(g, land, nm):
        return _add_own_half(g, land, "rs_add_" + nm)

    early = [g_wpa, g_wpb, g_wout, g_wi, g_wfo]
    g_win_other, lands = _mm_tn_half(hbt, du, 4, True, "mm_dw_in_other", carry=_exchange_swap_other_half(early))
    parts = [add_half(g, l, nm) for g, l, nm in zip(early, lands, names[1:])]
    g_win_mine, landed = _mm_tn_half(hbt, du, 4, False, "mm_dw_in_mine", carry=_exchange_join(
        _exchange_scatter(parts[3:4]), _exchange_swap_other_half([g_win_other])))
    piece_wi = landed[0]
    parts = [add_half(g_win_mine, landed[1], names[0])] + parts
    dh, landed = _mm_nt(du, win3, F32, "mm_d_h", carry=_exchange_scatter(parts[:4] + parts[5:]))
    pieces = landed[:4] + [piece_wi] + landed[4:]
    halves = [_sum_pieces(p, l, "rs_sum_" + nm) for p, l, nm in zip(parts, pieces, names)]
    g_win, g_wpa, g_wpb, g_wout, g_wi, g_wfo = [
        f.reshape(2 * f.shape[1], f.shape[2])
        for f in _exchange_call(_exchange_swap_result(halves), "rs_swap_result_half")]

    def f_ln1_bwd(i, rv, vv):
        dht, xt, dx1t = rv
        g, me = vv
        r1 = _rms(xt)
        xn = xt * r1
        dxn = dht * g * (1.0 + me[1:2])
        dx = dx1t + r1 * (dxn - xn * jnp.mean(dxn * xn, axis=-1, keepdims=True))
        return [dx], [_colsum(dht), _colsum(dht * xn * g), _colsum(dht * xn * (1.0 + me[1:2]))]

    grad_x, dsh1, dsc1, dn1g_l = _rowwise(
        f_ln1_bwd, nlat, tr, [(dh, D, 0, None), (x2d, D, 0, None), (dx1, D, 0, None)], [norm1_g, mod_e],
        [(D, F32)], [D, D, D], "ln1_bwd")

    def f_ln1_bwd_ctx(i, rv, vv):
        dht, xt = rv
        g, mc = vv
        xn = xt * _rms(xt)
        return [], [_colsum(dht), _colsum(dht * xn * g), _colsum(dht * xn * (1.0 + mc[1:2]))]

    ctx_rows = lambda i: i + nlat
    dsh1c, dsc1c, dn1g_c = _rowwise(
        f_ln1_bwd_ctx, nall - nlat, tr, [(dh, D, 0, ctx_rows), (ctx2d, D, 0, None)], [norm1_g, mod_c],
        [], [D, D, D], "ln1_bwd_ctx")

    drpb = _bias_tables_transpose(dbias, rows).reshape(1, -1)
    nrp = -(-drpb.shape[1] // D)
    drpb_rows = jnp.pad(drpb, ((0, 0), (0, nrp * D - drpb.shape[1]))).reshape(nrp, D)
    dlb = jnp.concatenate([dlb_f, dlb_b], axis=1)
    dhg = jnp.sum(dhgn.reshape(HG_HEADS, HG_DIM), axis=0, keepdims=True)

    def wide(v):
        return jnp.pad(v, ((0, 0), (0, D - v.shape[1])))

    pack_rows = [loss_cols, dfg, dn2g, dn1g_l + dn1g_c, dsh1, dsc1, dg1, dsh2, dsc2, dg2, dsh1c, dsc1c,
                 wide(dhg), wide(dlb), drpb_rows]
    pack = jnp.concatenate(pack_rows, axis=0)
    npk = -(-pack.shape[0] // 8) * 8
    pack = jnp.pad(pack, ((0, npk - pack.shape[0]), (0, 0)))
    gp = _all_gather8(pack).reshape(8, npk, D)
    tot = _sum8(gp, "sum_small_grads")

    loss = (0.5 / D) * jnp.sum(tot[0])
    grad_final_g = tot[1]
    grad_norm2_g = tot[2:3]
    grad_norm1_g = tot[3:4]
    grad_hg_norm_g = tot[12:13, :HG_DIM]
    dlb_tot = tot[13, :2 * hgf].reshape(2, hgf)
    grad_na_rpb = tot[14:14 + nrp].reshape(-1)[:drpb.shape[1]].reshape(na_rpb.shape)
    dlog = jnp.stack([dlb_tot * p_lb[0] * (1.0 - p_lb[0]), -dlb_tot * p_lb[0] * p_lb[1]], axis=0)
    grad_hg_lb = lax.dynamic_slice(dlog, (0, 0, sidx * fs), (2, 2, fs))

    dmod_all = gp[:, 4:10].reshape(8, N_MOD * D)
    dmod_ctx = jnp.concatenate([tot[10], tot[11], jnp.zeros((4 * D,), F32)])[None]
    dm16 = jnp.concatenate([dmod_all, dmod_ctx, jnp.zeros((7, N_MOD * D), F32)], axis=0)
    grad_b_ada = jnp.sum(dm16, axis=0, keepdims=True)
    dm_sh = lax.dynamic_slice(dm16, (0, sidx * ads), (16, ads))
    g_wada, dcin = _ada_bwd(cin, w_ada[0], dm_sh)
    gc = _all_gather8(dcin[8:16]).reshape(8, 8, D)
    grad_c_ctx = (gc[0, 0] + gc[2, 0] + gc[4, 0] + gc[6, 0]) * _dsilu(c_ctx)

    grads = {
        "c_ctx": grad_c_ctx, "w_ada": g_wada[None], "b_ada": grad_b_ada, "norm1_g": grad_norm1_g,
        "w_in": g_win[None], "na_rpb": grad_na_rpb, "hg_lb_logits": grad_hg_lb, "hg_norm_g": grad_hg_norm_g,
        "w_pa": g_wpa[None], "w_pb": g_wpb[None], "w_out": g_wout[None], "norm2_g": grad_norm2_g,
        "w_ffn_in": g_wi[None], "w_ffn_out": g_wfo[None], "final_g": grad_final_g,
    }
    weights = {
        "c_ctx": (c_ctx, m_c_ctx, v_c_ctx), "w_ada": (w_ada, m_w_ada, v_w_ada), "b_ada": (b_ada, m_b_ada, v_b_ada),
        "norm1_g": (norm1_g, m_norm1_g, v_norm1_g), "w_in": (w_in, m_w_in, v_w_in),
        "na_rpb": (na_rpb, m_na_rpb, v_na_rpb), "hg_lb_logits": (hg_lb_logits, m_hg_lb_logits, v_hg_lb_logits),
        "hg_norm_g": (hg_norm_g, m_hg_norm_g, v_hg_norm_g), "w_pa": (w_pa, m_w_pa, v_w_pa),
        "w_pb": (w_pb, m_w_pb, v_w_pb), "w_out": (w_out, m_w_out, v_w_out),
        "norm2_g": (norm2_g, m_norm2_g, v_norm2_g), "w_ffn_in": (w_ffn_in, m_w_ffn_in, v_w_ffn_in),
        "w_ffn_out": (w_ffn_out, m_w_ffn_out, v_w_ffn_out), "final_g": (final_g, m_final_g, v_final_g),
    }
    order = list(weights)
    deltas, new_ms, new_vs = [], [], []
    for nm in order:
        w, m, v = weights[nm]
        g = grads[nm].reshape(w.shape)
        grads[nm] = g
        if w.ndim == 3 and w.shape[0] == 1:
            d_, m_, v_ = _adamw(w[0], g[0], m[0], v[0], "adamw_" + nm)
            d_, m_, v_ = d_[None], m_[None], v_[None]
        else:
            d_, m_, v_ = _adamw(w, g, m, v, "adamw_" + nm)
        deltas.append(d_)
        new_ms.append(m_)
        new_vs.append(v_)

    return (loss, grad_x[None], *[grads[nm] for nm in order], *deltas, *new_ms, *new_vs)
```

```python
import numpy as np

import jax
import jax.numpy as jnp
from jax import lax
from jax.experimental import pallas as pl
from jax.experimental.pallas import tpu as pltpu

F32 = jnp.float32
BF16 = jnp.bfloat16

GRID_W = 64
WIN_H = 8
WIN_W = 16
NA_HEADS = 16
NA_HEAD_DIM = 64
HG_HEADS = 8
HG_DIM = 128
HG_CHUNK = 64
N_MOD = 6
EPS = 1e-6
ADAM_LR = 0.001
ADAM_B1 = 0.9
ADAM_B2 = 0.999
ADAM_EPS = 1e-08
ADAM_WD = 0.01
ADAM_STEP = 10

LANES = 128
NA_ROWS_PER_STEP = 16
VMEM_LIMIT = 56 * 1024 * 1024
MASK_VALUE = -1e30
EXP_CLAMP = 80.0
MESH_ID = pl.DeviceIdType.MESH
HI = lax.Precision.HIGHEST


def _tile(dim, target, mult=LANES):
    best = None
    t = mult
    while t <= min(dim, target):
        if dim % t == 0:
            best = t
        t += mult
    assert best is not None, (dim, target, mult)
    return best


def _params(sem):
    return pltpu.CompilerParams(dimension_semantics=sem, vmem_limit_bytes=VMEM_LIMIT)


def _dot(a, b, precision=None):
    return jnp.dot(a, b, preferred_element_type=F32, precision=precision)


def _dot_nt(a, b, precision=None):
    return lax.dot_general(a, b, (((1,), (1,)), ((), ())), preferred_element_type=F32, precision=precision)


def _dot_tn(a, b, precision=None):
    return lax.dot_general(a, b, (((0,), (0,)), ((), ())), preferred_element_type=F32, precision=precision)


def _split2(v):
    hi = v.astype(BF16)
    return hi, (v - hi.astype(F32)).astype(BF16)


def _dot_x3(dot, a2, b2):
    return dot(a2[0], b2[0]) + (dot(a2[0], b2[1]) + dot(a2[1], b2[0]))


def _sigmoid(v):
    return 1.0 / (1.0 + jnp.exp(-v))


def _mm_call(dot, operands, grid, in_specs, out_spec, out_shape, acc_shape, name, carry=None, epi=None):
    nk = grid[2]
    nci = 0 if carry is None else len(carry["ins"])
    nco = 0 if carry is None else len(carry["outs"])
    nei = 0 if epi is None else len(epi["ins"])
    neo = 1 if epi is None else len(epi["outs"])

    def body(*refs):
        a_ref, b_ref = refs[:2]
        ein = refs[2:2 + nei]
        cin = refs[2 + nei:2 + nei + nci]
        outs = refs[2 + nei + nci:2 + nei + nci + neo]
        cout = refs[2 + nei + nci + neo:2 + nei + nci + neo + nco]
        acc = refs[2 + nei + nci + neo + nco]
        sems = refs[3 + nei + nci + neo + nco:]
        m, n, k = pl.program_id(0), pl.program_id(1), pl.program_id(2)

        if carry is not None:
            @pl.when((m == 0) & (n == 0) & (k == 0))
            def _():
                carry["start"](cin, cout, *sems)

        @pl.when(k == 0)
        def _():
            acc[...] = jnp.zeros_like(acc)

        if carry is not None and "mid" in carry:
            @pl.when((m == grid[0] - 1) & (n == 0) & (k == 0))
            def _():
                carry["mid"](cin, cout, *sems)

        acc[...] += dot(a_ref[...], b_ref[...])

        @pl.when(k == nk - 1)
        def _():
            vals = [acc[...]] if epi is None else epi["fn"](acc[...], [r[...] for r in ein])
            for r, v in zip(outs, vals):
                if isinstance(v, tuple):
                    for i, vi in enumerate(v):
                        r[i] = vi.astype(r.dtype)
                else:
                    r[...] = v.astype(r.dtype)

        if carry is not None:
            @pl.when((m == grid[0] - 1) & (n == grid[1] - 1) & (k == nk - 1))
            def _():
                carry["finish"](cin, cout, *sems)

    any_spec = pl.BlockSpec(memory_space=pl.ANY)
    scratch = [pltpu.VMEM(acc_shape, F32)]
    extra = {}
    if carry is not None:
        scratch += [pltpu.SemaphoreType.DMA((carry["nsem"],)), pltpu.SemaphoreType.DMA((carry["nsem"],))]
        extra["input_output_aliases"] = {2 + nei + i: neo + j for i, j in carry["alias"].items()}
    sem = ("arbitrary",) * 3 if carry is not None else ("parallel", "parallel", "arbitrary")
    main_outs = [(out_shape, out_spec)] if epi is None else list(epi["outs"])
    res = pl.pallas_call(
        body, name=name, grid=grid,
        in_specs=list(in_specs) + ([] if epi is None else [sp for _, sp in epi["ins"]]) + [any_spec] * nci,
        out_specs=[sp for _, sp in main_outs] + [any_spec] * nco,
        out_shape=[sh for sh, _ in main_outs] + ([] if carry is None else list(carry["outs"])),
        scratch_shapes=scratch, compiler_params=_params(sem), **extra,
    )(*operands, *([] if epi is None else [ar for ar, _ in epi["ins"]]), *([] if carry is None else carry["ins"]))
    res = list(res)
    main = res[0] if epi is None else res[:neo]
    return main if carry is None else (main, res[neo:])


def _mm_nn(a, b3, out_dtype, name, carry=None):
    M, K = a.shape
    nsh, _, Ns = b3.shape
    tm, tn, tk = _tile(M, 1024), _tile(Ns, 1536), _tile(K, 2816)
    tps, nk = Ns // tn, K // tk
    return _mm_call(
        _dot, (a, b3), (M // tm, nsh * tps, nk),
        [pl.BlockSpec((tm, tk), lambda m, n, k: (m, k)),
         pl.BlockSpec((None, tk, tn), lambda m, n, k: (n // tps, k, n % tps))],
        pl.BlockSpec((tm, tn), lambda m, n, k: (m, n)),
        jax.ShapeDtypeStruct((M, nsh * Ns), out_dtype), (tm, tn), name, carry)


def _mm_nt(a, b3, out_dtype, name, carry=None, epi=None, tn_target=1408):
    a3 = a if a.ndim == 3 else a[None]
    na, M, Ka = a3.shape
    nsh, Kw, Ns = b3.shape
    assert na * Ka == nsh * Ns
    tm, tn, tk = _tile(M, 1024), _tile(Kw, tn_target), _tile(int(np.gcd(Ka, Ns)), 3072)
    kpa, kps = Ka // tk, Ns // tk
    return _mm_call(
        _dot_nt, (a3, b3), (M // tm, Kw // tn, nsh * kps),
        [pl.BlockSpec((None, tm, tk), lambda m, n, k: (k // kpa, m, k % kpa)),
         pl.BlockSpec((None, tn, tk), lambda m, n, k: (k // kps, n, k % kps))],
        pl.BlockSpec((tm, tn), lambda m, n, k: (m, n)),
        jax.ShapeDtypeStruct((M, Kw), out_dtype), (tm, tn), name, carry,
        None if epi is None else epi(tm, tn))


def _mm_tn(a, g, nsh, name, carry=None, a_is_t=False):
    Tk, M = a.shape[::-1] if a_is_t else a.shape
    g3 = g if g.ndim == 3 else g[None]
    ng, _, Ng = g3.shape
    Ns = ng * Ng // nsh
    tm, tn, tk = _tile(M // 2, 1408), _tile(int(np.gcd(Ng, Ns)), 1408), _tile(Tk, 2048)
    mh, tps, tpg, nk = (M // 2) // tm, Ns // tn, Ng // tn, Tk // tk
    return _mm_call(
        _dot if a_is_t else _dot_tn, (a, g3), (M // tm, nsh * tps, nk),
        [pl.BlockSpec((tm, tk), lambda m, n, k: (m, k)) if a_is_t else pl.BlockSpec((tk, tm), lambda m, n, k: (k, m)),
         pl.BlockSpec((None, tk, tn), lambda m, n, k: (n // tpg, k, n % tpg))],
        pl.BlockSpec((None, None, tm, tn), lambda m, n, k: (n // tps, m // mh, m % mh, n % tps)),
        jax.ShapeDtypeStruct((nsh, 2, M // 2, Ns), F32), (tm, tn), name, carry)


def _mm_tn_half(at, g, nsh, other, name, carry=None):
    M, Tk = at.shape
    Ns = g.shape[1] // nsh
    tm, tn, tk = _tile(M // 2, 1408), _tile(Ns, 1408), _tile(Tk, 2048)
    mh, tps, nk = (M // 2) // tm, Ns // tn, Tk // tk

    def half():
        c = lax.axis_index("c")
        return 1 - c if other else c

    return _mm_call(
        _dot, (at, g), (mh, nsh * tps, nk),
        [pl.BlockSpec((tm, tk), lambda m, n, k: (half() * mh + m, k)),
         pl.BlockSpec((tk, tn), lambda m, n, k: (k, n))],
        pl.BlockSpec((None, tm, tn), lambda m, n, k: (n // tps, m, n % tps)),
        jax.ShapeDtypeStruct((nsh, M // 2, Ns), F32), (tm, tn), name, carry)


def _ffn_in_fused(h2, wi3, name):
    M, K = h2.shape
    _, _, Ns = wi3.shape
    fh = 2 * Ns
    tm, tn = _tile(M, 512), _tile(Ns, 1408)
    tps = Ns // tn

    def body(h_ref, ba_ref, bu_ref, au_ref, sw_ref, swt_ref):
        h = h_ref[...]
        a, u = _dot(h, ba_ref[...]), _dot(h, bu_ref[...])
        au_ref[0] = a.astype(au_ref.dtype)
        au_ref[1] = u.astype(au_ref.dtype)
        sw = (_silu(a) * u).astype(sw_ref.dtype)
        sw_ref[...] = sw
        swt_ref[...] = sw.T

    return pl.pallas_call(
        body, name=name, grid=(M // tm, fh // tn),
        in_specs=[pl.BlockSpec((tm, K), lambda m, n: (m, 0)),
                  pl.BlockSpec((None, K, tn), lambda m, n: (n // tps, 0, n % tps)),
                  pl.BlockSpec((None, K, tn), lambda m, n: (2 + n // tps, 0, n % tps))],
        out_specs=[pl.BlockSpec((2, tm, tn), lambda m, n: (0, m, n)), pl.BlockSpec((tm, tn), lambda m, n: (m, n)),
                   pl.BlockSpec((tn, tm), lambda m, n: (n, m))],
        out_shape=[jax.ShapeDtypeStruct((2, M, fh), BF16), jax.ShapeDtypeStruct((M, fh), BF16),
                   jax.ShapeDtypeStruct((fh, M), BF16)],
        compiler_params=_params(("parallel", "parallel")),
    )(h2, wi3, wi3)


def _rowwise(fn, nblk, tm, rins, vins, routs, accs, name, carry=None):
    nr, nv, no, na = len(rins), len(vins), len(routs), len(accs)
    nci = 0 if carry is None else len(carry["ins"])
    nco = 0 if carry is None else len(carry["outs"])

    def body(*refs):
        i = pl.program_id(0)
        cin = refs[nr + nv:nr + nv + nci]
        cout = refs[nr + nv + nci + no + na:nr + nv + nci + no + na + nco]
        sems = refs[nr + nv + nci + no + na + nco:]
        if carry is not None:
            @pl.when(i == 0)
            def _():
                carry["start"](cin, cout, *sems)

        outs, accv = fn(i, [r[...] for r in refs[:nr]], [r[...] for r in refs[nr:nr + nv]])
        for r, v, spec in zip(refs[nr + nv + nci:nr + nv + nci + no], outs, routs):
            v = v.astype(r.dtype)
            r[...] = v.T if len(spec) == 3 else v
        arefs = refs[nr + nv + nci + no:nr + nv + nci + no + na]
        if carry is not None:
            @pl.when(i == nblk - 1)
            def _():
                if "mid" in carry:
                    carry["mid"](cin, cout, *sems)
                carry["finish"](cin, cout, *sems)

        if na:
            @pl.when(i == 0)
            def _():
                for a in arefs:
                    a[...] = jnp.zeros_like(a)

            for a, v in zip(arefs, accv):
                a[...] += v

    def row_spec(w, cb, rm):
        if rm is None:
            return pl.BlockSpec((tm, w), lambda i: (i, cb))
        return pl.BlockSpec((tm, w), lambda i: (rm(i), cb))

    in_specs = [row_spec(w, cb, rm) for (_, w, cb, rm) in rins]
    in_specs += [pl.BlockSpec(v.shape, lambda i: (0, 0)) for v in vins]
    def out_of(spec):
        w, dt = spec[:2]
        if len(spec) == 3:
            return pl.BlockSpec((w, tm), lambda i: (0, i)), jax.ShapeDtypeStruct((w, nblk * tm), dt)
        return pl.BlockSpec((tm, w), lambda i: (i, 0)), jax.ShapeDtypeStruct((nblk * tm, w), dt)

    out_specs = [out_of(sp)[0] for sp in routs] + [pl.BlockSpec((1, w), lambda i: (0, 0)) for w in accs]
    out_shape = [out_of(sp)[1] for sp in routs] + [jax.ShapeDtypeStruct((1, w), F32) for w in accs]
    any_spec = pl.BlockSpec(memory_space=pl.ANY)
    extra = {}
    if carry is not None:
        extra["scratch_shapes"] = [pltpu.SemaphoreType.DMA((carry["nsem"],)), pltpu.SemaphoreType.DMA((carry["nsem"],))]
        extra["input_output_aliases"] = {nr + nv + i: no + na + j for i, j in carry["alias"].items()}
    res = pl.pallas_call(
        body, name=name, grid=(nblk,), in_specs=in_specs + [any_spec] * nci, out_specs=out_specs + [any_spec] * nco,
        out_shape=out_shape + ([] if carry is None else list(carry["outs"])),
        compiler_params=_params(("arbitrary",)), **extra,
    )(*[r[0] for r in rins], *vins, *([] if carry is None else carry["ins"]))
    return list(res)


def _colsum(v):
    return jnp.sum(v, axis=0, keepdims=True)


def _rms(v):
    return lax.rsqrt(jnp.mean(v * v, axis=-1, keepdims=True) + EPS)


def _all_gather8(xs):
    m_per, n = xs.shape

    def body(x_ref, out_ref, send_sems, recv_sems, local_sem):
        x, y, c = lax.axis_index("x"), lax.axis_index("y"), lax.axis_index("c")
        me, sibling = (x, y, c), (x, y, 1 - c)
        chips = [(1 - x, y), (x, 1 - y), (1 - x, 1 - y)]

        def rows(px, py, pc):
            return out_ref.at[pl.ds((4 * px + 2 * py + pc) * m_per, m_per), :]

        def copy(k, block, to, src=None):
            return pltpu.make_async_remote_copy(
                src_ref=rows(*block) if src is None else src, dst_ref=rows(*block),
                send_sem=send_sems.at[k], recv_sem=recv_sems.at[k], device_id=to, device_id_type=MESH_ID)

        mine = pltpu.make_async_copy(x_ref, rows(*me), local_sem)
        mine.start()
        first = [copy(0, me, sibling, src=x_ref)]
        first += [copy(1 + j, me, (*chip, c), src=x_ref) for j, chip in enumerate(chips)]
        for cp in first:
            cp.start()
        passed = [copy(4 + j, (*chip, c), sibling) for j, chip in enumerate(chips)]
        for j, chip in enumerate(chips):
            copy(1 + j, (*chip, c), me).wait_recv()
            passed[j].start()
        copy(0, sibling, me).wait_recv()
        for j, chip in enumerate(chips):
            copy(4 + j, (*chip, 1 - c), me).wait_recv()
        for cp in first + passed:
            cp.wait_send()
        mine.wait()

    return pl.pallas_call(
        body, name="all_gather8_%dx%d" % (m_per, n),
        out_shape=jax.ShapeDtypeStruct((8 * m_per, n), xs.dtype),
        in_specs=[pl.BlockSpec(memory_space=pltpu.VMEM)],
        out_specs=pl.BlockSpec(memory_space=pltpu.VMEM),
        scratch_shapes=[pltpu.SemaphoreType.DMA((7,)), pltpu.SemaphoreType.DMA((7,)), pltpu.SemaphoreType.DMA],
    )(xs)


def _mesh_pos():
    x, y, c = lax.axis_index("x"), lax.axis_index("y"), lax.axis_index("c")
    chips = [(1 - x, y), (x, 1 - y), (1 - x, 1 - y)]
    return x, y, c, chips


def _my_shard():
    return 2 * lax.axis_index("x") + lax.axis_index("y")


def _cast_place(w, name):
    r, cw = w.shape
    rh = r // 2
    tm = _tile(rh, max(16, (1 << 20) // (4 * cw)), 16)
    nt = rh // tm

    def body(w_ref, o_ref):
        o_ref[...] = w_ref[...].astype(o_ref.dtype)

    return pl.pallas_call(
        body, name=name, grid=(2, nt),
        in_specs=[pl.BlockSpec((tm, cw), lambda h, i: (h * nt + i, 0))],
        out_specs=pl.BlockSpec((None, None, tm, cw), lambda h, i: (_my_shard(), h, i, 0)),
        out_shape=jax.ShapeDtypeStruct((4, 2, rh, cw), BF16),
        compiler_params=_params(("parallel", "parallel")),
    )(w)


def _exchange_gather(bufs):
    n = len(bufs)

    def copies(out, send_sems, recv_sems, base):
        def copy(i, k, shard, half, to):
            dst = out[i].at[shard, half]
            return pltpu.make_async_remote_copy(
                src_ref=dst, dst_ref=dst, send_sem=send_sems.at[base + 6 * i + k],
                recv_sem=recv_sems.at[base + 6 * i + k], device_id=to, device_id_type=MESH_ID)
        return copy

    def first(copy):
        x, y, c, chips = _mesh_pos()
        return [copy(i, j, 2 * x + y, c, (*chip, c)) for i in range(n) for j, chip in enumerate(chips)]

    def start(cin, out, send_sems, recv_sems, base=0):
        for cp in first(copies(out, send_sems, recv_sems, base)):
            cp.start()

    def passed(copy):
        x, y, c, chips = _mesh_pos()
        return [copy(i, 3 + j, 2 * chip[0] + chip[1], c, (x, y, 1 - c)) for j, chip in enumerate(chips) for i in range(n)]

    def mid(cin, out, send_sems, recv_sems, base=0):
        copy = copies(out, send_sems, recv_sems, base)
        x, y, c, chips = _mesh_pos()
        for j, chip in enumerate(chips):
            for i in range(n):
                copy(i, j, 2 * chip[0] + chip[1], c, (x, y, c)).wait_recv()
        for cp in passed(copy):
            cp.start()

    def finish(cin, out, send_sems, recv_sems, base=0):
        copy = copies(out, send_sems, recv_sems, base)
        x, y, c, chips = _mesh_pos()
        for j, chip in enumerate(chips):
            for i in range(n):
                copy(i, 3 + j, 2 * chip[0] + chip[1], 1 - c, (x, y, c)).wait_recv()
        for cp in first(copy) + passed(copy):
            cp.wait_send()

    return dict(ins=list(bufs), outs=[jax.ShapeDtypeStruct(b.shape, b.dtype) for b in bufs],
                alias={i: i for i in range(n)}, nsem=6 * n, start=start, mid=mid, finish=finish)


def _exchange_gather_via_neighbours(bufs):
    n = len(bufs)

    def geometry():
        x, y, c, _ = _mesh_pos()
        xn, yn = (1 - x, y), (x, 1 - y)
        shard = dict(me=2 * x + y, xn=2 * (1 - x) + y, yn=2 * x + (1 - y), dg=2 * (1 - x) + (1 - y))
        return x, y, c, xn, yn, shard

    def copy(out, sems, base, i, k, dst, to):
        return pltpu.make_async_remote_copy(
            src_ref=dst, dst_ref=dst, send_sem=sems[0].at[base + 7 * i + k], recv_sem=sems[1].at[base + 7 * i + k],
            device_id=to, device_id_type=MESH_ID)

    def quarter(out, i, shard, half, q):
        rq = bufs[i].shape[2] // 2
        return out[i].at[shard, half, pl.ds(q * rq, rq)]

    def plan(out, sems, base):
        x, y, c, xn, yn, sh = geometry()
        me, sib = (x, y, c), (x, y, 1 - c)
        p = dict(first=[], got_x=[], got_y=[], relay=[], got_dg=[], to_sib=[], from_sib=[])
        for i in range(n):
            mine = out[i].at[sh["me"], c]
            p["first"] += [copy(out, sems, base, i, 0, mine, (*xn, c)), copy(out, sems, base, i, 1, mine, (*yn, c))]
            p["got_x"].append(copy(out, sems, base, i, 0, out[i].at[sh["xn"], c], me))
            p["got_y"].append(copy(out, sems, base, i, 1, out[i].at[sh["yn"], c], me))
            p["relay"] += [copy(out, sems, base, i, 2, quarter(out, i, sh["xn"], c, 0), (*yn, c)),
                           copy(out, sems, base, i, 3, quarter(out, i, sh["yn"], c, 1), (*xn, c))]
            p["got_dg"] += [copy(out, sems, base, i, 2, quarter(out, i, sh["dg"], c, 0), me),
                            copy(out, sems, base, i, 3, quarter(out, i, sh["dg"], c, 1), me)]
            for k, who in enumerate(("xn", "yn", "dg")):
                p["to_sib"].append(copy(out, sems, base, i, 4 + k, out[i].at[sh[who], c], sib))
                p["from_sib"].append(copy(out, sems, base, i, 4 + k, out[i].at[sh[who], 1 - c], me))
        return p

    def start(cin, out, send_sems, recv_sems, base=0):
        for cp in plan(out, (send_sems, recv_sems), base)["first"]:
            cp.start()

    def mid(cin, out, send_sems, recv_sems, base=0):
        p = plan(out, (send_sems, recv_sems), base)
        for cp in p["got_x"] + p["got_y"]:
            cp.wait_recv()
        for cp in p["relay"]:
            cp.start()
        for cp in p["got_dg"]:
            cp.wait_recv()
        for cp in p["to_sib"]:
            cp.start()

    def finish(cin, out, send_sems, recv_sems, base=0):
        p = plan(out, (send_sems, recv_sems), base)
        for cp in p["from_sib"]:
            cp.wait_recv()
        for cp in p["first"] + p["relay"] + p["to_sib"]:
            cp.wait_send()

    return dict(ins=list(bufs), outs=[jax.ShapeDtypeStruct(b.shape, b.dtype) for b in bufs],
                alias={i: i for i in range(n)}, nsem=7 * n, start=start, mid=mid, finish=finish)


def _exchange_join(a, b):
    nai, nao = len(a["ins"]), len(a["outs"])

    def start(cin, cout, send_sems, recv_sems, base=0):
        a["start"](cin[:nai], cout[:nao], send_sems, recv_sems, base)
        b["start"](cin[nai:], cout[nao:], send_sems, recv_sems, base + a["nsem"])

    def mid(cin, cout, send_sems, recv_sems, base=0):
        if "mid" in a:
            a["mid"](cin[:nai], cout[:nao], send_sems, recv_sems, base)
        if "mid" in b:
            b["mid"](cin[nai:], cout[nao:], send_sems, recv_sems, base + a["nsem"])

    def finish(cin, cout, send_sems, recv_sems, base=0):
        a["finish"](cin[:nai], cout[:nao], send_sems, recv_sems, base)
        b["finish"](cin[nai:], cout[nao:], send_sems, recv_sems, base + a["nsem"])

    alias = dict(a["alias"])
    alias.update({nai + i: nao + j for i, j in b["alias"].items()})
    return dict(ins=a["ins"] + b["ins"], outs=a["outs"] + b["outs"], alias=alias, nsem=a["nsem"] + b["nsem"],
                start=start, mid=mid, finish=finish)


def _exchange_call(ex, name):
    nci, nco = len(ex["ins"]), len(ex["outs"])

    def body(*refs):
        cin, cout, sems = refs[:nci], refs[nci:nci + nco], refs[nci + nco:]
        ex["start"](cin, cout, *sems)
        if "mid" in ex:
            ex["mid"](cin, cout, *sems)
        ex["finish"](cin, cout, *sems)

    any_spec = pl.BlockSpec(memory_space=pl.ANY)
    return list(pl.pallas_call(
        body, name=name, out_shape=list(ex["outs"]), in_specs=[any_spec] * nci, out_specs=[any_spec] * nco,
        input_output_aliases=dict(ex["alias"]),
        scratch_shapes=[pltpu.SemaphoreType.DMA((ex["nsem"],)), pltpu.SemaphoreType.DMA((ex["nsem"],))],
    )(*ex["ins"]))


def _exchange_swap_other_half(gs):
    n = len(gs)

    def copies(g, land, send_sems, recv_sems, base):
        x, y, c, _ = _mesh_pos()
        return [pltpu.make_async_remote_copy(
            src_ref=g[i].at[:, 1 - c] if len(gs[i].shape) == 4 else g[i], dst_ref=land[i],
            send_sem=send_sems.at[base + i],
            recv_sem=recv_sems.at[base + i], device_id=(x, y, 1 - c), device_id_type=MESH_ID) for i in range(n)]

    def start(g, land, send_sems, recv_sems, base=0):
        for cp in copies(g, land, send_sems, recv_sems, base):
            cp.start()

    def finish(g, land, send_sems, recv_sems, base=0):
        for cp in copies(g, land, send_sems, recv_sems, base):
            cp.wait()

    return dict(ins=list(gs), outs=[jax.ShapeDtypeStruct((4,) + g.shape[-2:], g.dtype) for g in gs],
                alias={}, nsem=n, start=start, finish=finish)


def _exchange_scatter(ps):
    n = len(ps)

    def copies(p, land, send_sems, recv_sems, base):
        x, y, c, chips = _mesh_pos()
        return [pltpu.make_async_remote_copy(
            src_ref=p[i].at[2 * chip[0] + chip[1]], dst_ref=land[i].at[j],
            send_sem=send_sems.at[base + 3 * i + j], recv_sem=recv_sems.at[base + 3 * i + j],
            device_id=(*chip, c), device_id_type=MESH_ID) for i in range(n) for j, chip in enumerate(chips)]

    def start(p, land, send_sems, recv_sems, base=0):
        for cp in copies(p, land, send_sems, recv_sems, base):
            cp.start()

    def finish(p, land, send_sems, recv_sems, base=0):
        for cp in copies(p, land, send_sems, recv_sems, base):
            cp.wait()

    return dict(ins=list(ps), outs=[jax.ShapeDtypeStruct((3,) + p.shape[1:], p.dtype) for p in ps],
                alias={}, nsem=3 * n, start=start, finish=finish)


def _exchange_swap_result(bufs):
    n = len(bufs)

    def copies(out, send_sems, recv_sems, base, half):
        x, y, c, _ = _mesh_pos()
        h = c if half == "mine" else 1 - c
        return [pltpu.make_async_remote_copy(
            src_ref=out[i].at[h], dst_ref=out[i].at[h], send_sem=send_sems.at[base + i],
            recv_sem=recv_sems.at[base + i], device_id=(x, y, 1 - c), device_id_type=MESH_ID) for i in range(n)]

    def start(cin, out, send_sems, recv_sems, base=0):
        for cp in copies(out, send_sems, recv_sems, base, "mine"):
            cp.start()

    def finish(cin, out, send_sems, recv_sems, base=0):
        for cp in copies(out, send_sems, recv_sems, base, "theirs"):
            cp.wait_recv()
        for cp in copies(out, send_sems, recv_sems, base, "mine"):
            cp.wait_send()

    return dict(ins=list(bufs), outs=[jax.ShapeDtypeStruct(b.shape, b.dtype) for b in bufs],
                alias={i: i for i in range(n)}, nsem=n, start=start, finish=finish)


def _add_own_half(g, land, name):
    rh, cw = g.shape[-2:]
    tm = _tile(rh, max(16, (1 << 20) // (4 * cw)), 16)

    def body(g_ref, l_ref, o_ref):
        o_ref[...] = (g_ref[...] + l_ref[...]).astype(o_ref.dtype)

    mine = (pl.BlockSpec((None, None, tm, cw), lambda s, i: (s, lax.axis_index("c"), i, 0)) if g.ndim == 4
            else pl.BlockSpec((None, tm, cw), lambda s, i: (s, i, 0)))
    return pl.pallas_call(
        body, name=name, grid=(4, rh // tm),
        in_specs=[mine,
                  pl.BlockSpec((None, tm, cw), lambda s, i: (s, i, 0))],
        out_specs=pl.BlockSpec((None, tm, cw), lambda s, i: (s, i, 0)),
        out_shape=jax.ShapeDtypeStruct((4, rh, cw), BF16),
        compiler_params=_params(("parallel", "parallel")),
    )(g, land)


def _sum_pieces(part, land, name):
    _, rh, cw = land.shape
    tm = _tile(rh, max(16, (1 << 20) // (4 * cw)), 16)

    def body(p_ref, l_ref, o_ref):
        v = l_ref[...].astype(F32)
        o_ref[...] = (p_ref[...].astype(F32) + v[0]) + (v[1] + v[2])

    return pl.pallas_call(
        body, name=name, grid=(rh // tm,),
        in_specs=[pl.BlockSpec((None, tm, cw), lambda i: (_my_shard(), i, 0)),
                  pl.BlockSpec((3, tm, cw), lambda i: (0, i, 0))],
        out_specs=pl.BlockSpec((None, tm, cw), lambda i: (lax.axis_index("c"), i, 0)),
        out_shape=jax.ShapeDtypeStruct((2, rh, cw), F32),
        compiler_params=_params(("parallel",)),
    )(part, land)


def _sum8(g, name):
    def body(g_ref, o_ref):
        acc = g_ref[0]
        for k in range(1, 8):
            acc = acc + g_ref[k]
        o_ref[...] = acc

    return pl.pallas_call(body, name=name, out_shape=jax.ShapeDtypeStruct(g.shape[1:], F32))(g)


def _silu(v):
    return v * _sigmoid(v)


def _dsilu(v):
    s = _sigmoid(v)
    return s * (1.0 + v * (1.0 - s))


def _ada_fwd(cin, w, b):
    d, ns = w.shape
    tn = _tile(ns, 512)

    def body(c_ref, w_ref, b_ref, o_ref):
        o_ref[...] = _dot(_silu(c_ref[...]), w_ref[...], HI) + b_ref[...]

    return pl.pallas_call(
        body, name="ada_fwd", grid=(ns // tn,),
        in_specs=[pl.BlockSpec(cin.shape, lambda n: (0, 0)), pl.BlockSpec((d, tn), lambda n: (0, n)),
                  pl.BlockSpec((1, tn), lambda n: (0, n))],
        out_specs=pl.BlockSpec((cin.shape[0], tn), lambda n: (0, n)),
        out_shape=jax.ShapeDtypeStruct((cin.shape[0], ns), F32),
        compiler_params=_params(("parallel",)),
    )(cin, w, b)


def _ada_bwd(cin, w, dm):
    d, ns = w.shape
    tn = _tile(ns, 512)

    def body(c_ref, w_ref, d_ref, dw_ref, dc_ref):
        n = pl.program_id(0)

        @pl.when(n == 0)
        def _():
            dc_ref[...] = jnp.zeros_like(dc_ref)

        dw_ref[...] = _dot_tn(_silu(c_ref[...]), d_ref[...], HI)
        dc_ref[...] += _dot_nt(d_ref[...], w_ref[...], HI)

    return pl.pallas_call(
        body, name="ada_bwd", grid=(ns // tn,),
        in_specs=[pl.BlockSpec(cin.shape, lambda n: (0, 0)), pl.BlockSpec((d, tn), lambda n: (0, n)),
                  pl.BlockSpec((cin.shape[0], tn), lambda n: (0, n))],
        out_specs=[pl.BlockSpec((d, tn), lambda n: (0, n)), pl.BlockSpec(cin.shape, lambda n: (0, 0))],
        out_shape=[jax.ShapeDtypeStruct((d, ns), F32), jax.ShapeDtypeStruct(cin.shape, F32)],
        compiler_params=_params(("arbitrary",)),
    )(cin, w, dm)


def _bias_tables(rpb, rows):
    kh = min(WIN_H, rows)
    fold, onehot, in_win = _bias_selectors(kh)
    t = jnp.einsum("hdc,dsj->hsjc", rpb, jnp.asarray(fold), precision=HI)
    t = jnp.einsum("hsjc,cqk->hsqjk", t, jnp.asarray(onehot), precision=HI)
    t = jnp.where(jnp.asarray(in_win)[None, None, :, None, :], t, MASK_VALUE)
    return t.reshape(rpb.shape[0], kh, GRID_W, kh * GRID_W).astype(F32)


def _bias_selectors(kh):
    col = np.arange(GRID_W)
    col_start = np.clip(col - WIN_W // 2, 0, GRID_W - WIN_W)
    in_win = (col[None, :] >= col_start[:, None]) & (col[None, :] < col_start[:, None] + WIN_W)
    dc_idx = np.clip(col[None, :] - col[:, None], 1 - WIN_W, WIN_W - 1) + WIN_W - 1
    onehot = np.zeros((2 * WIN_W - 1, GRID_W, GRID_W), np.float32)
    qq, kk = np.nonzero(in_win)
    onehot[dc_idx[qq, kk], qq, kk] = 1.0
    fold = np.zeros((2 * WIN_H - 1, kh, kh), np.float32)
    for sh in range(kh):
        for j in range(kh):
            fold[j - sh + WIN_H - 1, sh, j] = 1.0
    return fold, onehot, in_win


def _mm_f32(a, b, name):
    M, K = a.shape
    N = b.shape[1]
    tm = _tile(M, 256, 8)

    def body(a_ref, b_ref, o_ref):
        o_ref[...] = _dot(a_ref[...], b_ref[...], HI)

    return pl.pallas_call(
        body, name=name, grid=(M // tm,),
        in_specs=[pl.BlockSpec((tm, K), lambda i: (i, 0)), pl.BlockSpec((K, N), lambda i: (0, 0))],
        out_specs=pl.BlockSpec((tm, N), lambda i: (i, 0)),
        out_shape=jax.ShapeDtypeStruct((M, N), F32),
        compiler_params=_params(("parallel",)),
    )(a, b)


def _bias_tables_transpose(dbias, rows):
    kh = min(WIN_H, rows)
    nh = dbias.shape[0]
    fold, onehot, _ = _bias_selectors(kh)
    ndc = onehot.shape[0]
    sel = np.zeros((GRID_W * GRID_W, LANES), np.float32)
    sel[:, :ndc] = onehot.reshape(ndc, -1).T
    x = dbias.reshape(nh, kh, GRID_W, kh, GRID_W).transpose(0, 1, 3, 2, 4).reshape(nh * kh * kh, GRID_W * GRID_W)
    z = _mm_f32(x, jnp.asarray(sel), "rpb_fold")[:, :ndc].reshape(nh, kh, kh, ndc)
    return jnp.einsum("dsj,hsjc->hdc", jnp.asarray(fold), z, precision=HI)


def _na_geometry(S):
    rows = S // GRID_W
    kh = min(WIN_H, rows)

    def row_start(r):
        return jnp.clip(r - kh // 2, 0, rows - kh)

    return rows, kh, row_start


def _na_by_head(ref, rr, lane):
    t = ref[rr * GRID_W:(rr + 1) * GRID_W, :]
    zero = jnp.zeros_like(t)
    return jnp.concatenate([jnp.where(lane < NA_HEAD_DIM, t, zero), jnp.where(lane >= NA_HEAD_DIM, t, zero)], axis=0)


def _na_pick_head(t2, lane):
    return jnp.where(lane < NA_HEAD_DIM, t2[:GRID_W], t2[GRID_W:])


def _na_scores(q_ref, k_ref, b_ref, i, nrs, nb, S, L, row_start, lane):
    qh = jnp.concatenate([_na_by_head(q_ref, rr, lane) for rr in range(nrs)], axis=0)
    sc = _dot_nt(qh, k_ref[pl.ds(S, L), :])
    starts, shifts, sb = [], [], []
    for rr in range(nrs):
        r = i * nrs + rr
        rs = row_start(r)
        starts.append(pl.multiple_of(rs * GRID_W, GRID_W))
        shifts.append(r - rs)
        bias = jnp.concatenate([b_ref[0, r - rs], b_ref[1, r - rs]], axis=0)
        sb.append(_dot_nt(qh[rr * 2 * GRID_W:(rr + 1) * 2 * GRID_W], k_ref[pl.ds(starts[-1], nb), :]) + bias)
    return qh, jnp.concatenate(sb, axis=0), sc, starts, shifts


def _na_fwd(qs, kb, vb, bias, S, L):
    T, naw = qs.shape
    rows, kh, row_start = _na_geometry(S)
    nb = kh * GRID_W
    npair = naw // LANES

    nrs = min(NA_ROWS_PER_STEP, rows)
    assert rows % nrs == 0

    def body(q_ref, k_ref, v_ref, b_ref, o_ref, lse_ref):
        i = pl.program_id(1)
        lane = lax.broadcasted_iota(jnp.int32, (GRID_W, LANES), 1)
        _, sb, sc, starts, _ = _na_scores(q_ref, k_ref, b_ref, i, nrs, nb, S, L, row_start, lane)
        m = jnp.maximum(jnp.max(sb, axis=-1, keepdims=True), jnp.max(sc, axis=-1, keepdims=True))
        pb, pc = jnp.exp(sb - m), jnp.exp(sc - m)
        l = jnp.sum(pb, axis=-1, keepdims=True) + jnp.sum(pc, axis=-1, keepdims=True)
        inv = 1.0 / l
        pb16, pc16 = (pb * inv).astype(BF16), (pc * inv).astype(BF16)
        oc = _dot(pc16, v_ref[pl.ds(S, L), :])
        lse = jnp.broadcast_to(m + jnp.log(l), oc.shape)
        for rr in range(nrs):
            two = slice(rr * 2 * GRID_W, (rr + 1) * 2 * GRID_W)
            rsl = slice(rr * GRID_W, (rr + 1) * GRID_W)
            o2 = oc[two] + _dot(pb16[two], v_ref[pl.ds(starts[rr], nb), :])
            o_ref[rsl, :] = _na_pick_head(o2, lane).astype(o_ref.dtype)
            lse_ref[rsl, :] = _na_pick_head(lse[two], lane)

    blk = pl.BlockSpec((nrs * GRID_W, LANES), lambda p, i: (i, p))
    col = pl.BlockSpec((T, LANES), lambda p, i: (0, p))
    return pl.pallas_call(
        body, name="na_fwd", grid=(npair, rows // nrs),
        in_specs=[blk, col, col, pl.BlockSpec((2, kh, GRID_W, nb), lambda p, i: (p, 0, 0, 0))],
        out_specs=[blk, blk],
        out_shape=[jax.ShapeDtypeStruct((S, naw), BF16), jax.ShapeDtypeStruct((S, naw), F32)],
        compiler_params=_params(("parallel", "arbitrary")),
    )(qs, kb, vb, bias)


def _na_bwd(qs, kb, vb, bias, do, o, lse, S, L):
    T, naw = qs.shape
    rows, kh, row_start = _na_geometry(S)
    nb = kh * GRID_W
    npair = naw // LANES

    nrs = min(NA_ROWS_PER_STEP, rows)
    assert rows % nrs == 0

    def body(q_ref, k_ref, v_ref, b_ref, do_ref, o_ref, lse_ref, dq_ref, dk_ref, dv_ref, db_ref):
        i = pl.program_id(1)

        @pl.when(i == 0)
        def _():
            dk_ref[...] = jnp.zeros_like(dk_ref)
            dv_ref[...] = jnp.zeros_like(dv_ref)
            db_ref[...] = jnp.zeros_like(db_ref)

        lane = lax.broadcasted_iota(jnp.int32, (GRID_W, LANES), 1)
        qh, sb, sc, starts, shifts = _na_scores(q_ref, k_ref, b_ref, i, nrs, nb, S, L, row_start, lane)
        doh = jnp.concatenate([_na_by_head(do_ref, rr, lane) for rr in range(nrs)], axis=0)
        o2 = jnp.concatenate([o_ref[rr * GRID_W:(rr + 1) * GRID_W, :] for rr in range(nrs) for _ in range(2)], axis=0)
        lse = jnp.concatenate([lse_ref[rr * GRID_W:(rr + 1) * GRID_W, :][:, hh * NA_HEAD_DIM:hh * NA_HEAD_DIM + 1]
                               for rr in range(nrs) for hh in range(2)], axis=0)
        pb, pc = jnp.exp(sb - lse), jnp.exp(sc - lse)
        delta = jnp.sum(doh.astype(F32) * o2.astype(F32), axis=-1, keepdims=True)
        dpb = jnp.concatenate([_dot_nt(doh[rr * 2 * GRID_W:(rr + 1) * 2 * GRID_W], v_ref[pl.ds(starts[rr], nb), :])
                               for rr in range(nrs)], axis=0)
        dsb = pb * (dpb - delta)
        dsc = pc * (_dot_nt(doh, v_ref[pl.ds(S, L), :]) - delta)
        dsb16, dsc16, pb16, pc16 = dsb.astype(BF16), dsc.astype(BF16), pb.astype(BF16), pc.astype(BF16)
        dqc = _dot(dsc16, k_ref[pl.ds(S, L), :])
        dk_ref[pl.ds(S, L), :] += _dot_tn(dsc16, qh)
        dv_ref[pl.ds(S, L), :] += _dot_tn(pc16, doh)
        for rr in range(nrs):
            two = slice(rr * 2 * GRID_W, (rr + 1) * 2 * GRID_W)
            band = pl.ds(starts[rr], nb)
            dq2 = dqc[two] + _dot(dsb16[two], k_ref[band, :])
            dq_ref[rr * GRID_W:(rr + 1) * GRID_W, :] = _na_pick_head(dq2, lane)
            dk_ref[band, :] += _dot_tn(dsb16[two], qh[two])
            dv_ref[band, :] += _dot_tn(pb16[two], doh[two])
            for hh in range(2):
                db_ref[hh, shifts[rr]] += dsb[(2 * rr + hh) * GRID_W:(2 * rr + hh + 1) * GRID_W]

    blk = pl.BlockSpec((nrs * GRID_W, LANES), lambda p, r: (r, p))
    col = pl.BlockSpec((T, LANES), lambda p, r: (0, p))
    return pl.pallas_call(
        body, name="na_bwd", grid=(npair, rows // nrs),
        in_specs=[blk, col, col, pl.BlockSpec((2, kh, GRID_W, nb), lambda p, r: (p, 0, 0, 0)), blk, blk, blk],
        out_specs=[blk, col, col, pl.BlockSpec((2, kh, GRID_W, nb), lambda p, r: (p, 0, 0, 0))],
        out_shape=[jax.ShapeDtypeStruct((S, naw), F32), jax.ShapeDtypeStruct((T, naw), F32),
                   jax.ShapeDtypeStruct((T, naw), F32), jax.ShapeDtypeStruct(bias.shape, F32)],
        compiler_params=_params(("parallel", "arbitrary")),
    )(qs, kb, vb, bias, do, o, lse)


def _hg_cols(naw, hgf, rev):
    qcol = (3 * naw) // hgf
    fcol = (3 * naw + hgf * (2 if rev else 1)) // hgf
    icol = (3 * naw + 3 * hgf) // hgf
    return qcol, fcol, icol


def _hg_chunk_order(S, L, rev):
    ncl, ncc = S // HG_CHUNK, L // HG_CHUNK
    nc = ncl + ncc

    def chunk_of(i):
        if rev:
            return nc - 1 - i
        return jnp.where(i < ncc, ncl + i, i - ncc)

    return nc, ncl, chunk_of


def _hg_gates(q, z, lb, rev):
    row = lax.broadcasted_iota(jnp.int32, (HG_CHUNK, HG_CHUNK), 0)
    colm = lax.broadcasted_iota(jnp.int32, (HG_CHUNK, HG_CHUNK), 1)
    tri = (colm >= row) if rev else (row >= colm)
    trif = tri.astype(F32)
    sig = _sigmoid(z)
    f = lb + (1.0 - lb) * sig
    lf = jnp.log(f)
    k = 1.0 - f
    cum = _dot(trif, lf, HI)
    mid = cum[HG_CHUNK // 2:HG_CHUNK // 2 + 1, :]
    last = cum[0:1, :] if rev else cum[HG_CHUNK - 1:HG_CHUNK, :]
    eq = jnp.exp(jnp.clip(cum - mid, -EXP_CLAMP, EXP_CLAMP))
    ek = jnp.exp(jnp.clip(mid - cum, -EXP_CLAMP, EXP_CLAMP))
    return tri, trif, sig, f, k, cum, last, eq, ek


def _hg_fwd(u, lbr, S, L, naw, hgf, rev):
    T = S + L
    nh = hgf // HG_DIM
    nc, ncl, chunk_of = _hg_chunk_order(S, L, rev)
    qcol, fcol, icol = _hg_cols(naw, hgf, rev)

    def step(i, q_ref, z_ref, v_ref, lb_ref, o_ref, st_ref, state):
        @pl.when(i == 0)
        def _():
            state[...] = jnp.zeros_like(state)

        q, z, v = q_ref[...], z_ref[...], v_ref[...]
        tri, _, _, _, k, cum, last, eq, ek = _hg_gates(q, z, lb_ref[...], rev)
        qe, ke = (q * eq).astype(BF16), (k * ek).astype(BF16)
        qd, kd = (q * jnp.exp(cum)).astype(BF16), (k * jnp.exp(last - cum)).astype(BF16)
        v16, el = v.astype(BF16), jnp.exp(last)
        for h in range(nh):
            sl = slice(h * HG_DIM, (h + 1) * HG_DIM)
            a = jnp.where(tri, _dot_nt(qe[:, sl], ke[:, sl]), 0.0)
            s0 = state[h]
            st_ref[h] = s0
            o_ref[:, sl] = _dot(a.astype(BF16), v16[:, sl]) + _dot_nt(qd[:, sl], s0.astype(BF16))
            state[h] = s0 * el[:, sl] + _dot_tn(v16[:, sl], kd[:, sl])

    def blk(cb):
        return pl.BlockSpec((HG_CHUNK, hgf), lambda i: (chunk_of(i), cb))

    return dict(
        step=step, nc=nc, operands=[u, u, u, lbr],
        in_specs=[blk(qcol), blk(fcol), blk(icol), pl.BlockSpec((1, hgf), lambda i: (0, 0))],
        out_specs=[pl.BlockSpec((HG_CHUNK, hgf), lambda i: (chunk_of(i), 0)),
                   pl.BlockSpec((None, nh, HG_DIM, HG_DIM), lambda i: (chunk_of(i), 0, 0, 0))],
        out_shape=[jax.ShapeDtypeStruct((T, hgf), F32), jax.ShapeDtypeStruct((nc, nh, HG_DIM, HG_DIM), F32)],
        scratch=[pltpu.VMEM((nh, HG_DIM, HG_DIM), F32)])


def _hg_both(parts, name):
    nin = [len(p["in_specs"]) for p in parts]
    nout = [len(p["out_specs"]) for p in parts]
    nscr = [len(p["scratch"]) for p in parts]

    def body(*refs):
        i = pl.program_id(0)
        ins, outs, scr = refs[:sum(nin)], refs[sum(nin):sum(nin) + sum(nout)], refs[sum(nin) + sum(nout):]
        for d, p in enumerate(parts):
            p["step"](i, *ins[sum(nin[:d]):sum(nin[:d + 1])], *outs[sum(nout[:d]):sum(nout[:d + 1])],
                      *scr[sum(nscr[:d]):sum(nscr[:d + 1])])

    res = pl.pallas_call(
        body, name=name, grid=(parts[0]["nc"],),
        in_specs=[sp for p in parts for sp in p["in_specs"]],
        out_specs=[sp for p in parts for sp in p["out_specs"]],
        out_shape=[sh for p in parts for sh in p["out_shape"]],
        scratch_shapes=[sc for p in parts for sc in p["scratch"]],
        compiler_params=_params(("arbitrary",)),
    )(*[op for p in parts for op in p["operands"]])
    return [list(res[sum(nout[:d]):sum(nout[:d + 1])]) for d in range(len(parts))]


def _hg_bwd(u, lbr, st, do, S, L, naw, hgf, rev):
    T = S + L
    nh = hgf // HG_DIM
    nc, ncl, chunk_fwd = _hg_chunk_order(S, L, rev)
    qcol, fcol, icol = _hg_cols(naw, hgf, rev)

    def chunk_of(j):
        return chunk_fwd(nc - 1 - j)

    def step(j, q_ref, z_ref, v_ref, lb_ref, st_ref, do_ref, dq_ref, dz_ref, dv_ref, dlb_ref, dstate,
             dqe_s, dke_s, dqd_s, dkd_s, dl_s):
        @pl.when(j == 0)
        def _():
            dstate[...] = jnp.zeros_like(dstate)
            dlb_ref[...] = jnp.zeros_like(dlb_ref)

        q, z, v = q_ref[...], z_ref[...], v_ref[...]
        lb = lb_ref[...]
        tri, trif, sig, f, k, cum, last, eq, ek = _hg_gates(q, z, lb, rev)
        ec, el, ekd = jnp.exp(cum), jnp.exp(last), jnp.exp(last - cum)
        qe, ke, qd, kd = q * eq, k * ek, q * ec, k * ekd
        qd16, kd16 = qd.astype(BF16), kd.astype(BF16)
        dout = jnp.where(chunk_of(j) < ncl, do_ref[...], 0.0)
        qe2, ke2, v16, dout16 = _split2(qe), _split2(ke), v.astype(BF16), dout.astype(BF16)
        for h in range(nh):
            sl = slice(h * HG_DIM, (h + 1) * HG_DIM)
            qeh, keh = [(t[0][:, sl], t[1][:, sl]) for t in (qe2, ke2)]
            a = jnp.where(tri, _dot_nt(qeh[0], keh[0]), 0.0).astype(BF16)
            s0 = st_ref[h]
            ds1 = dstate[h]
            s016, ds116 = s0.astype(BF16), ds1.astype(BF16)
            dv_ref[:, sl] = (_dot_tn(a, dout16[:, sl]) + _dot_nt(kd16[:, sl], ds116)).astype(dv_ref.dtype)
            da2 = _split2(jnp.where(tri, _dot_nt(dout16[:, sl], v16[:, sl]), 0.0))
            dqe_s[:, sl] = _dot_x3(_dot, da2, keh)
            dke_s[:, sl] = _dot_x3(_dot_tn, da2, qeh)
            dqd_s[:, sl] = _dot(dout16[:, sl], s016)
            dkd_s[:, sl] = _dot(v16[:, sl], ds116)
            dl_s[:, sl] = _colsum(ds1 * s0)
            dstate[h] = _dot_tn(dout16[:, sl], qd16[:, sl]) + ds1 * el[:, sl]
        dqe, dke, dqd, dkd = dqe_s[...], dke_s[...], dqd_s[...], dkd_s[...]
        dq_ref[...] = (dqe * eq + dqd * ec).astype(dq_ref.dtype)
        dk = dke * ek + dkd * ekd
        dcum = dqe * qe - dke * ke + dqd * qd - dkd * kd
        dlast = _colsum(dkd * kd) + el * dl_s[...]
        dlf = _dot_tn(trif, dcum, HI) + dlast
        df = dlf / f - dk
        dz_ref[...] = (df * (1.0 - lb) * sig * (1.0 - sig)).astype(dz_ref.dtype)
        dlb_ref[...] += _colsum(df * (1.0 - sig))

    def blk(cb):
        return pl.BlockSpec((HG_CHUNK, hgf), lambda j: (chunk_of(j), cb))

    oblk = pl.BlockSpec((HG_CHUNK, hgf), lambda j: (chunk_of(j), 0))
    wide = pltpu.VMEM((HG_CHUNK, hgf), F32)
    return dict(
        step=step, nc=nc, operands=[u, u, u, lbr, st, do],
        in_specs=[blk(qcol), blk(fcol), blk(icol), pl.BlockSpec((1, hgf), lambda j: (0, 0)),
                  pl.BlockSpec((None, nh, HG_DIM, HG_DIM), lambda j: (chunk_of(j), 0, 0, 0)),
                  pl.BlockSpec((HG_CHUNK, hgf), lambda j: (jnp.minimum(chunk_of(j), ncl - 1), 0))],
        out_specs=[oblk, oblk, oblk, pl.BlockSpec((1, hgf), lambda j: (0, 0))],
        out_shape=[jax.ShapeDtypeStruct((T, hgf), BF16)] * 3 + [jax.ShapeDtypeStruct((1, hgf), F32)],
        scratch=[pltpu.VMEM((nh, HG_DIM, HG_DIM), F32), wide, wide, wide, wide, pltpu.VMEM((1, hgf), F32)])


def _adamw(w, g, m, v, name):
    shape = w.shape
    if w.ndim != 2 or shape[0] % 8 or shape[1] % LANES:
        w, g, m, v = [a.reshape(1, -1) for a in (w, g, m, v)]
    r, cw = w.shape
    tm = _tile(r, max(8, (1 << 19) // cw), 8) if r % 8 == 0 else r
    c1 = 1.0 / (1.0 - ADAM_B1 ** ADAM_STEP)
    c2 = 1.0 / (1.0 - ADAM_B2 ** ADAM_STEP)

    def body(w_ref, g_ref, m_ref, v_ref, d_ref, nm_ref, nv_ref):
        gg = g_ref[...]
        nm = ADAM_B1 * m_ref[...] + (1.0 - ADAM_B1) * gg
        nv = ADAM_B2 * v_ref[...] + (1.0 - ADAM_B2) * (gg * gg)
        d_ref[...] = -ADAM_LR * ((nm * c1) / (jnp.sqrt(nv * c2) + ADAM_EPS) + ADAM_WD * w_ref[...])
        nm_ref[...] = nm
        nv_ref[...] = nv

    spec = pl.BlockSpec((tm, cw), lambda i: (i, 0))
    outs = pl.pallas_call(
        body, name=name, grid=(r // tm,), in_specs=[spec] * 4, out_specs=[spec] * 3,
        out_shape=[jax.ShapeDtypeStruct((r, cw), F32)] * 3,
        compiler_params=_params(("parallel",)),
    )(w, g, m, v)
    return [o.reshape(shape) for o in outs]


def kernel(x, c, ctx, c_ctx, w_ada, b_ada, norm1_g, w_in, na_rpb, hg_lb_logits, hg_norm_g, w_pa, w_pb, w_out, norm2_g, w_ffn_in, w_ffn_out, final_g, loss_target, m_c_ctx, m_w_ada, m_b_ada, m_norm1_g, m_w_in, m_na_rpb, m_hg_lb_logits, m_hg_norm_g, m_w_pa, m_w_pb, m_w_out, m_norm2_g, m_w_ffn_in, m_w_ffn_out, m_final_g, v_c_ctx, v_w_ada, v_b_ada, v_norm1_g, v_w_in, v_na_rpb, v_hg_lb_logits, v_hg_norm_g, v_w_pa, v_w_pb, v_w_out, v_norm2_g, v_w_ffn_in, v_w_ffn_out, v_final_g):
    xi, yi, ci = lax.axis_index("x"), lax.axis_index("y"), lax.axis_index("c")
    sidx = 2 * xi + yi
    eidx = 4 * xi + 2 * yi + ci

    S, D = x.shape[1], x.shape[2]
    L = ctx.shape[1]
    T = S + L
    naw = NA_HEADS * NA_HEAD_DIM
    hgf = HG_HEADS * HG_DIM
    inw = 3 * naw + 5 * hgf + 2 * D
    fh = w_ffn_out.shape[1] * 4
    ads = w_ada.shape[2]
    fs = hg_lb_logits.shape[2]
    rows = S // GRID_W
    tr = _tile(L, 256)
    nlat, nall = S // tr, T // tr
    assert naw == hgf and D % naw == 0 and S % tr == 0 and 2 * hgf <= D

    pack0 = jnp.concatenate([c, jnp.pad(hg_lb_logits.reshape(1, -1), ((0, 0), (0, D - 4 * fs))),
                             jnp.zeros((6, D), F32)], axis=0)
    g0 = _all_gather8(pack0).reshape(8, 8, D)
    cs = g0[:, 0]
    lbl = g0[::2, 1, :4 * fs].reshape(4, 2, 2, fs).transpose(1, 2, 0, 3).reshape(2, 2, 4 * fs)
    p_lb = jax.nn.softmax(lbl, axis=0)
    lb = p_lb[0]
    lbb = [lb[d].reshape(1, hgf) for d in range(2)]

    cin = jnp.concatenate([cs, c_ctx[None], jnp.zeros((7, D), F32)], axis=0)
    b_sh = lax.dynamic_slice(b_ada, (0, sidx * ads), (1, ads))
    modp = _ada_fwd(cin, w_ada[0], b_sh)
    modfull = _all_gather8(modp).reshape(8, 16, ads)[::2].transpose(1, 0, 2).reshape(16, 4 * ads)
    mod_e = jnp.pad(lax.dynamic_index_in_dim(modfull, eidx, 0, keepdims=False).reshape(N_MOD, D), ((0, 2), (0, 0)))
    mod_c = jnp.pad(modfull[8].reshape(N_MOD, D), ((0, 2), (0, 0)))

    names = ["w_in", "w_pa", "w_pb", "w_out", "w_ffn_in", "w_ffn_out"]
    placed = [_cast_place(w[0], "cast_" + nm)
              for w, nm in zip((w_in, w_pa, w_pb, w_out, w_ffn_in, w_ffn_out), names)]
    def shards(g):
        return g.reshape(4, 2 * g.shape[2], g.shape[3])

    x2d, ctx2d = x[0], ctx[0]

    def f_ln1(i, rv, vv):
        xl, xc = rv
        g, me, mc = vv
        isc = i >= nlat
        xt = jnp.where(isc, xc, xl)
        sh = jnp.where(isc, mc[0:1], me[0:1])
        sc = jnp.where(isc, mc[1:2], me[1:2])
        h = xt * _rms(xt) * g * (1.0 + sc) + sh
        return [h, h], []

    hb, hbt, win_all = _rowwise(f_ln1, nall, tr, [(x2d, D, 0, lambda i: jnp.minimum(i, nlat - 1)),
                                                 (ctx2d, D, 0, lambda i: jnp.maximum(i - nlat, 0))],
                                [norm1_g, mod_e, mod_c], [(D, BF16), (D, BF16, "T")], [], "ln1",
                                carry=_exchange_gather_via_neighbours(placed[:1]))
    win3 = shards(win_all)
    u, gathered = _mm_nn(hb, win3, F32, "mm_in", carry=_exchange_gather(placed[1:]))
    wpa3, wpb3, wout3, wi3, wfo3 = [shards(g) for g in gathered]
    wout1 = wout3.reshape(1, D, D)
    wfo1 = wfo3.reshape(1, fh, D)

    scale = NA_HEAD_DIM ** -0.5

    def f_qkv(i, rv, vv):
        q, k, v = rv
        return [q * scale, k, v], []

    qs, kb, vb = _rowwise(f_qkv, nall, tr, [(u, naw, 0, None), (u, naw, 1, None), (u, naw, 2, None)], [],
                          [(naw, BF16)] * 3, [], "qkv_cast")
    bias = _bias_tables(na_rpb[0], rows)
    o_na, lse = _na_fwd(qs, kb, vb, bias, S, L)

    (o_f, st_f), (o_b, st_b) = _hg_both(
        [_hg_fwd(u, lbb[0], S, L, naw, hgf, False), _hg_fwd(u, lbb[1], S, L, naw, hgf, True)], "hg_fwd")

    hgn = jnp.tile(hg_norm_g, (1, HG_HEADS))
    hog_cb = (3 * naw + 4 * hgf) // hgf
    ga_cb = (3 * naw + 5 * hgf) // D
    gb_cb = ga_cb + 1

    def heads_rms(o):
        return jnp.concatenate([jnp.broadcast_to(_rms(o[:, h * HG_DIM:(h + 1) * HG_DIM]), (o.shape[0], HG_DIM))
                                for h in range(HG_HEADS)], axis=1)

    def f_readout(i, rv, vv):
        of, ob_, hog = rv
        g, = vv
        o = of + ob_
        return [o * heads_rms(o) * g * _silu(hog)], []

    ob, = _rowwise(f_readout, nlat, tr, [(o_f, hgf, 0, None), (o_b, hgf, 0, None), (u, hgf, hog_cb, None)],
                   [hgn], [(hgf, BF16)], [], "hg_readout")

    ya = _mm_nn(o_na, wpa3, BF16, "mm_pa")
    yb = _mm_nn(ob, wpb3, BF16, "mm_pb")

    def f_merge(i, rv, vv):
        ya_, yb_, ga, gb = rv
        return [_sigmoid(ga) * ya_ + _sigmoid(gb) * yb_], []

    yv, = _rowwise(f_merge, nlat, tr, [(ya, D, 0, None), (yb, D, 0, None), (u, D, ga_cb, None), (u, D, gb_cb, None)],
                   [], [(D, BF16)], [], "merge")
    z = _mm_nn(yv, wout1, F32, "mm_out")

    def f_res1(i, rv, vv):
        xt, zt = rv
        g, me = vv
        x1 = xt + me[2:3] * zt
        h = x1 * _rms(x1) * g * (1.0 + me[4:5]) + me[3:4]
        return [x1, h, h], []

    x1, h2, h2t = _rowwise(f_res1, nlat, tr, [(x2d, D, 0, None), (z, D, 0, None)], [norm2_g, mod_e],
                           [(D, F32), (D, BF16), (D, BF16, "T")], [], "res1_ln2")
    au3, sw, swt = _ffn_in_fused(h2, wi3, "mm_ffn_in")
    ff =_mm_nn(sw, wfo1, F32, "mm_ffn_out")

    fg = final_g.reshape(1, D)

    def f_final(i, rv, vv):
        x1t, ft, tg = rv
        g, me = vv
        x2 = x1t + me[5:6] * ft
        r3 = _rms(x2)
        xn = x2 * r3
        err = xn * g - tg
        dyy = err * (1.0 / D)
        dxn = dyy * g
        dx2 = r3 * (dxn - xn * jnp.mean(dxn * xn, axis=-1, keepdims=True))
        return [dx2, dx2 * me[5:6]], [_colsum(err * err), _colsum(dyy * xn), _colsum(dx2 * ft)]

    dx2, dfb, loss_cols, dfg, dg2 = _rowwise(
        f_final, nlat, tr, [(x1, D, 0, None), (ff, D, 0, None), (loss_target[0], D, 0, None)], [fg, mod_e],
        [(D, F32), (D, BF16)], [D, D, D], "final_loss")

    def dswiglu_epi(tm, tn):
        blk = pl.BlockSpec((2, tm, tn), lambda m, n, k: (0, m, n))

        def fn(d, ins):
            a, uu = ins[0][0].astype(F32), ins[0][1].astype(F32)
            return [(d * uu * _dsilu(a), d * _silu(a))]

        return dict(ins=[(au3, blk)], outs=[(jax.ShapeDtypeStruct((2, S, fh), BF16), blk)], fn=fn)

    dau3, = _mm_nt(dfb, wfo1, BF16, "mm_d_sw", epi=dswiglu_epi, tn_target=512)
    g_wfo = _mm_tn(swt, dfb, 1, "mm_dw_ffn_out", a_is_t=True).reshape(4, 2, fh // 8, D)
    dh2 = _mm_nt(dau3, wi3, F32, "mm_d_h2")
    g_wi = _mm_tn(h2t, dau3, 4, "mm_dw_ffn_in", a_is_t=True)

    def f_ln2_bwd(i, rv, vv):
        dh, x1t, dx2t, zt = rv
        g, me = vv
        r2 = _rms(x1t)
        xn = x1t * r2
        dxn = dh * g * (1.0 + me[4:5])
        dx1 = dx2t + r2 * (dxn - xn * jnp.mean(dxn * xn, axis=-1, keepdims=True))
        return ([dx1, dx1 * me[2:3]],
                [_colsum(dh), _colsum(dh * xn * g), _colsum(dh * xn * (1.0 + me[4:5])), _colsum(dx1 * zt)])

    dx1, dzb, dsh2, dsc2, dn2g, dg1 = _rowwise(
        f_ln2_bwd, nlat, tr, [(dh2, D, 0, None), (x1, D, 0, None), (dx2, D, 0, None), (z, D, 0, None)],
        [norm2_g, mod_e], [(D, F32), (D, BF16)], [D, D, D, D], "ln2_bwd")

    g_wout = _mm_tn(yv, dzb, 1, "mm_dw_out").reshape(4, 2, D // 8, D)

    def dmerge_epi(tm, tn):
        blk = pl.BlockSpec((tm, tn), lambda m, n, k: (m, n))

        def gate(cb):
            return pl.BlockSpec((tm, tn), lambda m, n, k: (m, cb * (D // tn) + n))

        def fn(d, ins):
            ya_, yb_, ga, gb = ins
            sa, sb_ = _sigmoid(ga), _sigmoid(gb)
            return [d * sa, d * sb_, d * ya_ * sa * (1.0 - sa), d * yb_ * sb_ * (1.0 - sb_)]

        return dict(ins=[(ya, blk), (yb, blk), (u, gate(ga_cb)), (u, gate(gb_cb))],
                    outs=[(jax.ShapeDtypeStruct((S, D), BF16), blk)] * 4, fn=fn)

    dya, dyb, dga, dgb = _mm_nt(dzb, wout1, BF16, "mm_d_y", epi=dmerge_epi, tn_target=512)
    d_ona = _mm_nt(dya, wpa3, BF16, "mm_d_ona")
    d_ob = _mm_nt(dyb, wpb3, F32, "mm_d_ob")
    g_wpa = _mm_tn(o_na, dya, 4, "mm_dw_pa")
    g_wpb = _mm_tn(ob, dyb, 4, "mm_dw_pb")

    def f_dreadout(i, rv, vv):
        d, of, ob_, hog = rv
        g, = vv
        o = of + ob_
        on = o * heads_rms(o)
        t = d * _silu(hog) * g
        mt = jnp.concatenate([jnp.broadcast_to(jnp.mean((t * on)[:, h * HG_DIM:(h + 1) * HG_DIM], axis=-1,
                                                        keepdims=True), (o.shape[0], HG_DIM))
                              for h in range(HG_HEADS)], axis=1)
        do_ = heads_rms(o) * (t - on * mt)
        return [do_, d * on * g * _dsilu(hog)], [_colsum(d * _silu(hog) * on)]

    do_hg, dhog, dhgn = _rowwise(
        f_dreadout, nlat, tr, [(d_ob, hgf, 0, None), (o_f, hgf, 0, None), (o_b, hgf, 0, None),
                               (u, hgf, hog_cb, None)], [hgn], [(hgf, F32), (hgf, BF16)], [hgf], "hg_readout_bwd")

    (dq_f, dz_f, dv_f, dlb_f), (dq_b, dz_b, dv_b, dlb_b) = _hg_both(
        [_hg_bwd(u, lbb[0], st_f, do_hg, S, L, naw, hgf, False), _hg_bwd(u, lbb[1], st_b, do_hg, S, L, naw, hgf, True)],
        "hg_bwd")
    dq_na, dk_na, dv_na, dbias = _na_bwd(qs, kb, vb, bias, d_ona, o_na, lse, S, L)

    ta = _tile(L, 128)
    nla, naa = S // ta, T // ta
    lat = lambda i: jnp.minimum(i, nla - 1)

    def f_assemble(i, rv, vv):
        dqn, dk, dv, dqf, dqb, dzf, dzb_, dvf, dvb, dho, dga_, dgb_ = rv
        keep = (i < nla).astype(F32)
        f32 = lambda t: t.astype(F32)
        return [jnp.concatenate([dqn * (scale * keep), dk, dv, f32(dqf) + f32(dqb), f32(dzf), f32(dzb_),
                                 f32(dvf) + f32(dvb),
                                 dho.astype(F32) * keep, dga_.astype(F32) * keep, dgb_.astype(F32) * keep],
                                axis=1)], []

    du, = _rowwise(
        f_assemble, naa, ta,
        [(dq_na, naw, 0, lat), (dk_na, naw, 0, None), (dv_na, naw, 0, None), (dq_f, hgf, 0, None),
         (dq_b, hgf, 0, None), (dz_f, hgf, 0, None), (dz_b, hgf, 0, None), (dv_f, hgf, 0, None),
         (dv_b, hgf, 0, None), (dhog, hgf, 0, lat), (dga, D, 0, lat), (dgb, D, 0, lat)],
        [], [(inw, BF16)], [], "assemble_du")

    def add_half(g, land, nm):
        return _add_own_half(g, land, "rs_add_" + nm)

    early = [g_wpa, g_wpb, g_wout, g_wi, g_wfo]
    g_win_other, lands = _mm_tn_half(hbt, du, 4, True, "mm_dw_in_other", carry=_exchange_swap_other_half(early))
    parts = [add_half(g, l, nm) for g, l, nm in zip(early, lands, names[1:])]
    g_win_mine, landed = _mm_tn_half(hbt, du, 4, False, "mm_dw_in_mine", carry=_exchange_join(
        _exchange_scatter(parts[3:4]), _exchange_swap_other_half([g_win_other])))
    piece_wi = landed[0]
    parts = [add_half(g_win_mine, landed[1], names[0])] + parts
    dh, landed = _mm_nt(du, win3, F32, "mm_d_h", carry=_exchange_scatter(parts[:4] + parts[5:]))
    pieces = landed[:4] + [piece_wi] + landed[4:]
    halves = [_sum_pieces(p, l, "rs_sum_" + nm) for p, l, nm in zip(parts, pieces, names)]
    g_win, g_wpa, g_wpb, g_wout, g_wi, g_wfo = [
        f.reshape(2 * f.shape[1], f.shape[2])
        for f in _exchange_call(_exchange_swap_result(halves), "rs_swap_result_half")]

    def f_ln1_bwd(i, rv, vv):
        dht, xt, dx1t = rv
        g, me = vv
        r1 = _rms(xt)
        xn = xt * r1
        dxn = dht * g * (1.0 + me[1:2])
        dx = dx1t + r1 * (dxn - xn * jnp.mean(dxn * xn, axis=-1, keepdims=True))
        return [dx], [_colsum(dht), _colsum(dht * xn * g), _colsum(dht * xn * (1.0 + me[1:2]))]

    grad_x, dsh1, dsc1, dn1g_l = _rowwise(
        f_ln1_bwd, nlat, tr, [(dh, D, 0, None), (x2d, D, 0, None), (dx1, D, 0, None)], [norm1_g, mod_e],
        [(D, F32)], [D, D, D], "ln1_bwd")

    def f_ln1_bwd_ctx(i, rv, vv):
        dht, xt = rv
        g, mc = vv
        xn = xt * _rms(xt)
        return [], [_colsum(dht), _colsum(dht * xn * g), _colsum(dht * xn * (1.0 + mc[1:2]))]

    ctx_rows = lambda i: i + nlat
    dsh1c, dsc1c, dn1g_c = _rowwise(
        f_ln1_bwd_ctx, nall - nlat, tr, [(dh, D, 0, ctx_rows), (ctx2d, D, 0, None)], [norm1_g, mod_c],
        [], [D, D, D], "ln1_bwd_ctx")

    drpb = _bias_tables_transpose(dbias, rows).reshape(1, -1)
    nrp = -(-drpb.shape[1] // D)
    drpb_rows = jnp.pad(drpb, ((0, 0), (0, nrp * D - drpb.shape[1]))).reshape(nrp, D)
    dlb = jnp.concatenate([dlb_f, dlb_b], axis=1)
    dhg = jnp.sum(dhgn.reshape(HG_HEADS, HG_DIM), axis=0, keepdims=True)

    def wide(v):
        return jnp.pad(v, ((0, 0), (0, D - v.shape[1])))

    pack_rows = [loss_cols, dfg, dn2g, dn1g_l + dn1g_c, dsh1, dsc1, dg1, dsh2, dsc2, dg2, dsh1c, dsc1c,
                 wide(dhg), wide(dlb), drpb_rows]
    pack = jnp.concatenate(pack_rows, axis=0)
    npk = -(-pack.shape[0] // 8) * 8
    pack = jnp.pad(pack, ((0, npk - pack.shape[0]), (0, 0)))
    gp = _all_gather8(pack).reshape(8, npk, D)
    tot = _sum8(gp, "sum_small_grads")

    loss = (0.5 / D) * jnp.sum(tot[0])
    grad_final_g = tot[1]
    grad_norm2_g = tot[2:3]
    grad_norm1_g = tot[3:4]
    grad_hg_norm_g = tot[12:13, :HG_DIM]
    dlb_tot = tot[13, :2 * hgf].reshape(2, hgf)
    grad_na_rpb = tot[14:14 + nrp].reshape(-1)[:drpb.shape[1]].reshape(na_rpb.shape)
    dlog = jnp.stack([dlb_tot * p_lb[0] * (1.0 - p_lb[0]), -dlb_tot * p_lb[0] * p_lb[1]], axis=0)
    grad_hg_lb = lax.dynamic_slice(dlog, (0, 0, sidx * fs), (2, 2, fs))

    dmod_all = gp[:, 4:10].reshape(8, N_MOD * D)
    dmod_ctx = jnp.concatenate([tot[10], tot[11], jnp.zeros((4 * D,), F32)])[None]
    dm16 = jnp.concatenate([dmod_all, dmod_ctx, jnp.zeros((7, N_MOD * D), F32)], axis=0)
    grad_b_ada = jnp.sum(dm16, axis=0, keepdims=True)
    dm_sh = lax.dynamic_slice(dm16, (0, sidx * ads), (16, ads))
    g_wada, dcin = _ada_bwd(cin, w_ada[0], dm_sh)
    gc = _all_gather8(dcin[8:16]).reshape(8, 8, D)
    grad_c_ctx = (gc[0, 0] + gc[2, 0] + gc[4, 0] + gc[6, 0]) * _dsilu(c_ctx)

    grads = {
        "c_ctx": grad_c_ctx, "w_ada": g_wada[None], "b_ada": grad_b_ada, "norm1_g": grad_norm1_g,
        "w_in": g_win[None], "na_rpb": grad_na_rpb, "hg_lb_logits": grad_hg_lb, "hg_norm_g": grad_hg_norm_g,
        "w_pa": g_wpa[None], "w_pb": g_wpb[None], "w_out": g_wout[None], "norm2_g": grad_norm2_g,
        "w_ffn_in": g_wi[None], "w_ffn_out": g_wfo[None], "final_g": grad_final_g,
    }
    weights = {
        "c_ctx": (c_ctx, m_c_ctx, v_c_ctx), "w_ada": (w_ada, m_w_ada, v_w_ada), "b_ada": (b_ada, m_b_ada, v_b_ada),
        "norm1_g": (norm1_g, m_norm1_g, v_norm1_g), "w_in": (w_in, m_w_in, v_w_in),
        "na_rpb": (na_rpb, m_na_rpb, v_na_rpb), "hg_lb_logits": (hg_lb_logits, m_hg_lb_logits, v_hg_lb_logits),
        "hg_norm_g": (hg_norm_g, m_hg_norm_g, v_hg_norm_g), "w_pa": (w_pa, m_w_pa, v_w_pa),
        "w_pb": (w_pb, m_w_pb, v_w_pb), "w_out": (w_out, m_w_out, v_w_out),
        "norm2_g": (norm2_g, m_norm2_g, v_norm2_g), "w_ffn_in": (w_ffn_in, m_w_ffn_in, v_w_ffn_in),
        "w_ffn_out": (w_ffn_out, m_w_ffn_out, v_w_ffn_out), "final_g": (final_g, m_final_g, v_final_g),
    }
    order = list(weights)
    deltas, new_ms, new_vs = [], [], []
    for nm in order:
        w, m, v = weights[nm]
        g = grads[nm].reshape(w.shape)
        grads[nm] = g
        if w.ndim == 3 and w.shape[0] == 1:
            d_, m_, v_ = _adamw(w[0], g[0], m[0], v[0], "adamw_" + nm)
            d_, m_, v_ = d_[None], m_[None], v_[None]
        else:
            d_, m_, v_ = _adamw(w, g, m, v, "adamw_" + nm)
        deltas.append(d_)
        new_ms.append(m_)
        new_vs.append(v_)

    return (loss, grad_x[None], *[grads[nm] for nm in order], *deltas, *new_ms, *new_vs)
```

```python
import numpy as np

import jax
import jax.numpy as jnp
from jax import lax
from jax.experimental import pallas as pl
from jax.experimental.pallas import tpu as pltpu

F32 = jnp.float32
BF16 = jnp.bfloat16

GRID_W = 64
WIN_H = 8
WIN_W = 16
NA_HEADS = 16
NA_HEAD_DIM = 64
HG_HEADS = 8
HG_DIM = 128
HG_CHUNK = 64
N_MOD = 6
EPS = 1e-6
ADAM_LR = 0.001
ADAM_B1 = 0.9
ADAM_B2 = 0.999
ADAM_EPS = 1e-08
ADAM_WD = 0.01
ADAM_STEP = 10

LANES = 128
NA_ROWS_PER_STEP = 16
VMEM_LIMIT = 56 * 1024 * 1024
MASK_VALUE = -1e30
EXP_CLAMP = 80.0
MESH_ID = pl.DeviceIdType.MESH
HI = lax.Precision.HIGHEST


def _tile(dim, target, mult=LANES):
    best = None
    t = mult
    while t <= min(dim, target):
        if dim % t == 0:
            best = t
        t += mult
    assert best is not None, (dim, target, mult)
    return best


def _params(sem):
    return pltpu.CompilerParams(dimension_semantics=sem, vmem_limit_bytes=VMEM_LIMIT)


def _dot(a, b, precision=None):
    return jnp.dot(a, b, preferred_element_type=F32, precision=precision)


def _dot_nt(a, b, precision=None):
    return lax.dot_general(a, b, (((1,), (1,)), ((), ())), preferred_element_type=F32, precision=precision)


def _dot_tn(a, b, precision=None):
    return lax.dot_general(a, b, (((0,), (0,)), ((), ())), preferred_element_type=F32, precision=precision)


def _split2(v):
    hi = v.astype(BF16)
    return hi, (v - hi.astype(F32)).astype(BF16)


def _dot_x3(dot, a2, b2):
    return dot(a2[0], b2[0]) + (dot(a2[0], b2[1]) + dot(a2[1], b2[0]))


def _sigmoid(v):
    return 1.0 / (1.0 + jnp.exp(-v))


def _mm_call(dot, operands, grid, in_specs, out_spec, out_shape, acc_shape, name, carry=None, epi=None):
    nk = grid[2]
    nci = 0 if carry is None else len(carry["ins"])
    nco = 0 if carry is None else len(carry["outs"])
    nei = 0 if epi is None else len(epi["ins"])
    neo = 1 if epi is None else len(epi["outs"])

    def body(*refs):
        a_ref, b_ref = refs[:2]
        ein = refs[2:2 + nei]
        cin = refs[2 + nei:2 + nei + nci]
        outs = refs[2 + nei + nci:2 + nei + nci + neo]
        cout = refs[2 + nei + nci + neo:2 + nei + nci + neo + nco]
        acc = refs[2 + nei + nci + neo + nco]
        sems = refs[3 + nei + nci + neo + nco:]
        m, n, k = pl.program_id(0), pl.program_id(1), pl.program_id(2)

        if carry is not None:
            @pl.when((m == 0) & (n == 0) & (k == 0))
            def _():
                carry["start"](cin, cout, *sems)

        @pl.when(k == 0)
        def _():
            acc[...] = jnp.zeros_like(acc)

        if carry is not None and "mid" in carry:
            @pl.when((m == grid[0] - 1) & (n == 0) & (k == 0))
            def _():
                carry["mid"](cin, cout, *sems)

        acc[...] += dot(a_ref[...], b_ref[...])

        @pl.when(k == nk - 1)
        def _():
            vals = [acc[...]] if epi is None else epi["fn"](acc[...], [r[...] for r in ein])
            whens = [None] * len(outs) if epi is None else epi.get("when", [None] * len(outs))
            for r, v, w in zip(outs, vals, whens):
                if w is not None:
                    @pl.when(w(m, n))
                    def _(r=r, v=v):
                        r[...] = v.astype(r.dtype)
                elif isinstance(v, tuple):
                    for i, vi in enumerate(v):
                        r[i] = vi.astype(r.dtype)
                else:
                    r[...] = v.astype(r.dtype)

        if carry is not None:
            @pl.when((m == grid[0] - 1) & (n == grid[1] - 1) & (k == nk - 1))
            def _():
                carry["finish"](cin, cout, *sems)

    any_spec = pl.BlockSpec(memory_space=pl.ANY)
    scratch = [pltpu.VMEM(acc_shape, F32)]
    extra = {}
    if carry is not None:
        scratch += [pltpu.SemaphoreType.DMA((carry["nsem"],)), pltpu.SemaphoreType.DMA((carry["nsem"],))]
        extra["input_output_aliases"] = {2 + nei + i: neo + j for i, j in carry["alias"].items()}
    sem = ("arbitrary",) * 3 if carry is not None else ("parallel", "parallel", "arbitrary")
    main_outs = [(out_shape, out_spec)] if epi is None else list(epi["outs"])
    res = pl.pallas_call(
        body, name=name, grid=grid,
        in_specs=list(in_specs) + ([] if epi is None else [sp for _, sp in epi["ins"]]) + [any_spec] * nci,
        out_specs=[sp for _, sp in main_outs] + [any_spec] * nco,
        out_shape=[sh for sh, _ in main_outs] + ([] if carry is None else list(carry["outs"])),
        scratch_shapes=scratch, compiler_params=_params(sem), **extra,
    )(*operands, *([] if epi is None else [ar for ar, _ in epi["ins"]]), *([] if carry is None else carry["ins"]))
    res = list(res)
    main = res[0] if epi is None else res[:neo]
    return main if carry is None else (main, res[neo:])


def _mm_nn(a, b3, out_dtype, name, carry=None, epi=None, tn_target=1536):
    M, K = a.shape
    nsh, _, Ns = b3.shape
    tm, tn, tk = _tile(M, 1024), _tile(Ns, tn_target), _tile(K, 2816)
    tps, nk = Ns // tn, K // tk
    return _mm_call(
        _dot, (a, b3), (M // tm, nsh * tps, nk),
        [pl.BlockSpec((tm, tk), lambda m, n, k: (m, k)),
         pl.BlockSpec((None, tk, tn), lambda m, n, k: (n // tps, k, n % tps))],
        pl.BlockSpec((tm, tn), lambda m, n, k: (m, n)),
        jax.ShapeDtypeStruct((M, nsh * Ns), out_dtype), (tm, tn), name, carry,
        None if epi is None else epi(tm, tn))


def _mm_nt(a, b3, out_dtype, name, carry=None, epi=None, tn_target=1408):
    a3 = a if a.ndim == 3 else a[None]
    na, M, Ka = a3.shape
    nsh, Kw, Ns = b3.shape
    assert na * Ka == nsh * Ns
    tm, tn, tk = _tile(M, 1024), _tile(Kw, tn_target), _tile(int(np.gcd(Ka, Ns)), 3072)
    kpa, kps = Ka // tk, Ns // tk
    return _mm_call(
        _dot_nt, (a3, b3), (M // tm, Kw // tn, nsh * kps),
        [pl.BlockSpec((None, tm, tk), lambda m, n, k: (k // kpa, m, k % kpa)),
         pl.BlockSpec((None, tn, tk), lambda m, n, k: (k // kps, n, k % kps))],
        pl.BlockSpec((tm, tn), lambda m, n, k: (m, n)),
        jax.ShapeDtypeStruct((M, Kw), out_dtype), (tm, tn), name, carry,
        None if epi is None else epi(tm, tn))


def _mm_tn(a, g, nsh, name, carry=None, a_is_t=False):
    Tk, M = a.shape[::-1] if a_is_t else a.shape
    g3 = g if g.ndim == 3 else g[None]
    ng, _, Ng = g3.shape
    Ns = ng * Ng // nsh
    tm, tn, tk = _tile(M // 2, 1408), _tile(int(np.gcd(Ng, Ns)), 1408), _tile(Tk, 2048)
    mh, tps, tpg, nk = (M // 2) // tm, Ns // tn, Ng // tn, Tk // tk
    return _mm_call(
        _dot if a_is_t else _dot_tn, (a, g3), (M // tm, nsh * tps, nk),
        [pl.BlockSpec((tm, tk), lambda m, n, k: (m, k)) if a_is_t else pl.BlockSpec((tk, tm), lambda m, n, k: (k, m)),
         pl.BlockSpec((None, tk, tn), lambda m, n, k: (n // tpg, k, n % tpg))],
        pl.BlockSpec((None, None, tm, tn), lambda m, n, k: (n // tps, m // mh, m % mh, n % tps)),
        jax.ShapeDtypeStruct((nsh, 2, M // 2, Ns), F32), (tm, tn), name, carry)


def _mm_tn_half(at, g, nsh, other, name, carry=None):
    M, Tk = at.shape
    Ns = g.shape[1] // nsh
    tm, tn, tk = _tile(M // 2, 1408), _tile(Ns, 1408), _tile(Tk, 2048)
    mh, tps, nk = (M // 2) // tm, Ns // tn, Tk // tk

    def half():
        c = lax.axis_index("c")
        return 1 - c if other else c

    return _mm_call(
        _dot, (at, g), (mh, nsh * tps, nk),
        [pl.BlockSpec((tm, tk), lambda m, n, k: (half() * mh + m, k)),
         pl.BlockSpec((tk, tn), lambda m, n, k: (k, n))],
        pl.BlockSpec((None, tm, tn), lambda m, n, k: (n // tps, m, n % tps)),
        jax.ShapeDtypeStruct((nsh, M // 2, Ns), F32), (tm, tn), name, carry)


def _ffn_in_fused(h2, wi3, name):
    M, K = h2.shape
    _, _, Ns = wi3.shape
    fh = 2 * Ns
    tm, tn = _tile(M, 512), _tile(Ns, 1408)
    tps = Ns // tn

    def body(h_ref, ba_ref, bu_ref, au_ref, sw_ref, swt_ref):
        h = h_ref[...]
        a, u = _dot(h, ba_ref[...]), _dot(h, bu_ref[...])
        au_ref[0] = a.astype(au_ref.dtype)
        au_ref[1] = u.astype(au_ref.dtype)
        sw = (_silu(a) * u).astype(sw_ref.dtype)
        sw_ref[...] = sw
        swt_ref[...] = sw.T

    return pl.pallas_call(
        body, name=name, grid=(M // tm, fh // tn),
        in_specs=[pl.BlockSpec((tm, K), lambda m, n: (m, 0)),
                  pl.BlockSpec((None, K, tn), lambda m, n: (n // tps, 0, n % tps)),
                  pl.BlockSpec((None, K, tn), lambda m, n: (2 + n // tps, 0, n % tps))],
        out_specs=[pl.BlockSpec((2, tm, tn), lambda m, n: (0, m, n)), pl.BlockSpec((tm, tn), lambda m, n: (m, n)),
                   pl.BlockSpec((tn, tm), lambda m, n: (n, m))],
        out_shape=[jax.ShapeDtypeStruct((2, M, fh), BF16), jax.ShapeDtypeStruct((M, fh), BF16),
                   jax.ShapeDtypeStruct((fh, M), BF16)],
        compiler_params=_params(("parallel", "parallel")),
    )(h2, wi3, wi3)


def _rowwise(fn, nblk, tm, rins, vins, routs, accs, name, carry=None):
    nr, nv, no, na = len(rins), len(vins), len(routs), len(accs)
    nci = 0 if carry is None else len(carry["ins"])
    nco = 0 if carry is None else len(carry["outs"])

    def body(*refs):
        i = pl.program_id(0)
        cin = refs[nr + nv:nr + nv + nci]
        cout = refs[nr + nv + nci + no + na:nr + nv + nci + no + na + nco]
        sems = refs[nr + nv + nci + no + na + nco:]
        if carry is not None:
            @pl.when(i == 0)
            def _():
                carry["start"](cin, cout, *sems)

        outs, accv = fn(i, [r[...] for r in refs[:nr]], [r[...] for r in refs[nr:nr + nv]])
        for r, v, spec in zip(refs[nr + nv + nci:nr + nv + nci + no], outs, routs):
            v = v.astype(r.dtype)
            r[...] = v.T if len(spec) == 3 else v
        arefs = refs[nr + nv + nci + no:nr + nv + nci + no + na]
        if carry is not None:
            @pl.when(i == nblk - 1)
            def _():
                if "mid" in carry:
                    carry["mid"](cin, cout, *sems)
                carry["finish"](cin, cout, *sems)

        if na:
            @pl.when(i == 0)
            def _():
                for a in arefs:
                    a[...] = jnp.zeros_like(a)

            for a, v in zip(arefs, accv):
                a[...] += v

    def row_spec(w, cb, rm):
        if rm is None:
            return pl.BlockSpec((tm, w), lambda i: (i, cb))
        return pl.BlockSpec((tm, w), lambda i: (rm(i), cb))

    in_specs = [row_spec(w, cb, rm) for (_, w, cb, rm) in rins]
    in_specs += [pl.BlockSpec(v.shape, lambda i: (0, 0)) for v in vins]
    def out_of(spec):
        w, dt = spec[:2]
        if len(spec) == 3:
            return pl.BlockSpec((w, tm), lambda i: (0, i)), jax.ShapeDtypeStruct((w, nblk * tm), dt)
        return pl.BlockSpec((tm, w), lambda i: (i, 0)), jax.ShapeDtypeStruct((nblk * tm, w), dt)

    out_specs = [out_of(sp)[0] for sp in routs] + [pl.BlockSpec((1, w), lambda i: (0, 0)) for w in accs]
    out_shape = [out_of(sp)[1] for sp in routs] + [jax.ShapeDtypeStruct((1, w), F32) for w in accs]
    any_spec = pl.BlockSpec(memory_space=pl.ANY)
    extra = {}
    if carry is not None:
        extra["scratch_shapes"] = [pltpu.SemaphoreType.DMA((carry["nsem"],)), pltpu.SemaphoreType.DMA((carry["nsem"],))]
        extra["input_output_aliases"] = {nr + nv + i: no + na + j for i, j in carry["alias"].items()}
    res = pl.pallas_call(
        body, name=name, grid=(nblk,), in_specs=in_specs + [any_spec] * nci, out_specs=out_specs + [any_spec] * nco,
        out_shape=out_shape + ([] if carry is None else list(carry["outs"])),
        compiler_params=_params(("arbitrary",)), **extra,
    )(*[r[0] for r in rins], *vins, *([] if carry is None else carry["ins"]))
    return list(res)


def _colsum(v):
    return jnp.sum(v, axis=0, keepdims=True)


def _rms(v):
    return lax.rsqrt(jnp.mean(v * v, axis=-1, keepdims=True) + EPS)


def _all_gather8(xs):
    m_per, n = xs.shape

    def body(x_ref, out_ref, send_sems, recv_sems, local_sem):
        x, y, c = lax.axis_index("x"), lax.axis_index("y"), lax.axis_index("c")
        me, sibling = (x, y, c), (x, y, 1 - c)
        chips = [(1 - x, y), (x, 1 - y), (1 - x, 1 - y)]

        def rows(px, py, pc):
            return out_ref.at[pl.ds((4 * px + 2 * py + pc) * m_per, m_per), :]

        def copy(k, block, to, src=None):
            return pltpu.make_async_remote_copy(
                src_ref=rows(*block) if src is None else src, dst_ref=rows(*block),
                send_sem=send_sems.at[k], recv_sem=recv_sems.at[k], device_id=to, device_id_type=MESH_ID)

        mine = pltpu.make_async_copy(x_ref, rows(*me), local_sem)
        mine.start()
        first = [copy(0, me, sibling, src=x_ref)]
        first += [copy(1 + j, me, (*chip, c), src=x_ref) for j, chip in enumerate(chips)]
        for cp in first:
            cp.start()
        passed = [copy(4 + j, (*chip, c), sibling) for j, chip in enumerate(chips)]
        for j, chip in enumerate(chips):
            copy(1 + j, (*chip, c), me).wait_recv()
            passed[j].start()
        copy(0, sibling, me).wait_recv()
        for j, chip in enumerate(chips):
            copy(4 + j, (*chip, 1 - c), me).wait_recv()
        for cp in first + passed:
            cp.wait_send()
        mine.wait()

    return pl.pallas_call(
        body, name="all_gather8_%dx%d" % (m_per, n),
        out_shape=jax.ShapeDtypeStruct((8 * m_per, n), xs.dtype),
        in_specs=[pl.BlockSpec(memory_space=pltpu.VMEM)],
        out_specs=pl.BlockSpec(memory_space=pltpu.VMEM),
        scratch_shapes=[pltpu.SemaphoreType.DMA((7,)), pltpu.SemaphoreType.DMA((7,)), pltpu.SemaphoreType.DMA],
    )(xs)


def _mesh_pos():
    x, y, c = lax.axis_index("x"), lax.axis_index("y"), lax.axis_index("c")
    chips = [(1 - x, y), (x, 1 - y), (1 - x, 1 - y)]
    return x, y, c, chips


def _my_shard():
    return 2 * lax.axis_index("x") + lax.axis_index("y")


def _cast_place(w, name):
    r, cw = w.shape
    rh = r // 2
    tm = _tile(rh, max(16, (1 << 20) // (4 * cw)), 16)
    nt = rh // tm

    def body(w_ref, o_ref):
        o_ref[...] = w_ref[...].astype(o_ref.dtype)

    return pl.pallas_call(
        body, name=name, grid=(2, nt),
        in_specs=[pl.BlockSpec((tm, cw), lambda h, i: (h * nt + i, 0))],
        out_specs=pl.BlockSpec((None, None, tm, cw), lambda h, i: (_my_shard(), h, i, 0)),
        out_shape=jax.ShapeDtypeStruct((4, 2, rh, cw), BF16),
        compiler_params=_params(("parallel", "parallel")),
    )(w)


def _exchange_gather(bufs):
    n = len(bufs)

    def copies(out, send_sems, recv_sems, base):
        def copy(i, k, shard, half, to):
            dst = out[i].at[shard, half]
            return pltpu.make_async_remote_copy(
                src_ref=dst, dst_ref=dst, send_sem=send_sems.at[base + 6 * i + k],
                recv_sem=recv_sems.at[base + 6 * i + k], device_id=to, device_id_type=MESH_ID)
        return copy

    def first(copy):
        x, y, c, chips = _mesh_pos()
        return [copy(i, j, 2 * x + y, c, (*chip, c)) for i in range(n) for j, chip in enumerate(chips)]

    def start(cin, out, send_sems, recv_sems, base=0):
        for cp in first(copies(out, send_sems, recv_sems, base)):
            cp.start()

    def passed(copy):
        x, y, c, chips = _mesh_pos()
        return [copy(i, 3 + j, 2 * chip[0] + chip[1], c, (x, y, 1 - c)) for j, chip in enumerate(chips) for i in range(n)]

    def mid(cin, out, send_sems, recv_sems, base=0):
        copy = copies(out, send_sems, recv_sems, base)
        x, y, c, chips = _mesh_pos()
        for j, chip in enumerate(chips):
            for i in range(n):
                copy(i, j, 2 * chip[0] + chip[1], c, (x, y, c)).wait_recv()
        for cp in passed(copy):
            cp.start()

    def finish(cin, out, send_sems, recv_sems, base=0):
        copy = copies(out, send_sems, recv_sems, base)
        x, y, c, chips = _mesh_pos()
        for j, chip in enumerate(chips):
            for i in range(n):
                copy(i, 3 + j, 2 * chip[0] + chip[1], 1 - c, (x, y, c)).wait_recv()
        for cp in first(copy) + passed(copy):
            cp.wait_send()

    return dict(ins=list(bufs), outs=[jax.ShapeDtypeStruct(b.shape, b.dtype) for b in bufs],
                alias={i: i for i in range(n)}, nsem=6 * n, start=start, mid=mid, finish=finish)


def _exchange_gather_via_neighbours(bufs):
    n = len(bufs)

    def geometry():
        x, y, c, _ = _mesh_pos()
        xn, yn = (1 - x, y), (x, 1 - y)
        shard = dict(me=2 * x + y, xn=2 * (1 - x) + y, yn=2 * x + (1 - y), dg=2 * (1 - x) + (1 - y))
        return x, y, c, xn, yn, shard

    def copy(out, sems, base, i, k, dst, to):
        return pltpu.make_async_remote_copy(
            src_ref=dst, dst_ref=dst, send_sem=sems[0].at[base + 7 * i + k], recv_sem=sems[1].at[base + 7 * i + k],
            device_id=to, device_id_type=MESH_ID)

    def quarter(out, i, shard, half, q):
        rq = bufs[i].shape[2] // 2
        return out[i].at[shard, half, pl.ds(q * rq, rq)]

    def plan(out, sems, base):
        x, y, c, xn, yn, sh = geometry()
        me, sib = (x, y, c), (x, y, 1 - c)
        p = dict(first=[], got_x=[], got_y=[], relay=[], got_dg=[], to_sib=[], from_sib=[])
        for i in range(n):
            mine = out[i].at[sh["me"], c]
            p["first"] += [copy(out, sems, base, i, 0, mine, (*xn, c)), copy(out, sems, base, i, 1, mine, (*yn, c))]
            p["got_x"].append(copy(out, sems, base, i, 0, out[i].at[sh["xn"], c], me))
            p["got_y"].append(copy(out, sems, base, i, 1, out[i].at[sh["yn"], c], me))
            p["relay"] += [copy(out, sems, base, i, 2, quarter(out, i, sh["xn"], c, 0), (*yn, c)),
                           copy(out, sems, base, i, 3, quarter(out, i, sh["yn"], c, 1), (*xn, c))]
            p["got_dg"] += [copy(out, sems, base, i, 2, quarter(out, i, sh["dg"], c, 0), me),
                            copy(out, sems, base, i, 3, quarter(out, i, sh["dg"], c, 1), me)]
            for k, who in enumerate(("xn", "yn", "dg")):
                p["to_sib"].append(copy(out, sems, base, i, 4 + k, out[i].at[sh[who], c], sib))
                p["from_sib"].append(copy(out, sems, base, i, 4 + k, out[i].at[sh[who], 1 - c], me))
        return p

    def start(cin, out, send_sems, recv_sems, base=0):
        for cp in plan(out, (send_sems, recv_sems), base)["first"]:
            cp.start()

    def mid(cin, out, send_sems, recv_sems, base=0):
        p = plan(out, (send_sems, recv_sems), base)
        for cp in p["got_x"] + p["got_y"]:
            cp.wait_recv()
        for cp in p["relay"]:
            cp.start()
        for cp in p["got_dg"]:
            cp.wait_recv()
        for cp in p["to_sib"]:
            cp.start()

    def finish(cin, out, send_sems, recv_sems, base=0):
        p = plan(out, (send_sems, recv_sems), base)
        for cp in p["from_sib"]:
            cp.wait_recv()
        for cp in p["first"] + p["relay"] + p["to_sib"]:
            cp.wait_send()

    return dict(ins=list(bufs), outs=[jax.ShapeDtypeStruct(b.shape, b.dtype) for b in bufs],
                alias={i: i for i in range(n)}, nsem=7 * n, start=start, mid=mid, finish=finish)


def _exchange_join(a, b):
    nai, nao = len(a["ins"]), len(a["outs"])

    def start(cin, cout, send_sems, recv_sems, base=0):
        a["start"](cin[:nai], cout[:nao], send_sems, recv_sems, base)
        b["start"](cin[nai:], cout[nao:], send_sems, recv_sems, base + a["nsem"])

    def mid(cin, cout, send_sems, recv_sems, base=0):
        if "mid" in a:
            a["mid"](cin[:nai], cout[:nao], send_sems, recv_sems, base)
        if "mid" in b:
            b["mid"](cin[nai:], cout[nao:], send_sems, recv_sems, base + a["nsem"])

    def finish(cin, cout, send_sems, recv_sems, base=0):
        a["finish"](cin[:nai], cout[:nao], send_sems, recv_sems, base)
        b["finish"](cin[nai:], cout[nao:], send_sems, recv_sems, base + a["nsem"])

    alias = dict(a["alias"])
    alias.update({nai + i: nao + j for i, j in b["alias"].items()})
    return dict(ins=a["ins"] + b["ins"], outs=a["outs"] + b["outs"], alias=alias, nsem=a["nsem"] + b["nsem"],
                start=start, mid=mid, finish=finish)


def _exchange_call(ex, name):
    nci, nco = len(ex["ins"]), len(ex["outs"])

    def body(*refs):
        cin, cout, sems = refs[:nci], refs[nci:nci + nco], refs[nci + nco:]
        ex["start"](cin, cout, *sems)
        if "mid" in ex:
            ex["mid"](cin, cout, *sems)
        ex["finish"](cin, cout, *sems)

    any_spec = pl.BlockSpec(memory_space=pl.ANY)
    return list(pl.pallas_call(
        body, name=name, out_shape=list(ex["outs"]), in_specs=[any_spec] * nci, out_specs=[any_spec] * nco,
        input_output_aliases=dict(ex["alias"]),
        scratch_shapes=[pltpu.SemaphoreType.DMA((ex["nsem"],)), pltpu.SemaphoreType.DMA((ex["nsem"],))],
    )(*ex["ins"]))


def _exchange_swap_other_half(gs):
    n = len(gs)

    def copies(g, land, send_sems, recv_sems, base):
        x, y, c, _ = _mesh_pos()
        return [pltpu.make_async_remote_copy(
            src_ref=g[i].at[:, 1 - c] if len(gs[i].shape) == 4 else g[i], dst_ref=land[i],
            send_sem=send_sems.at[base + i],
            recv_sem=recv_sems.at[base + i], device_id=(x, y, 1 - c), device_id_type=MESH_ID) for i in range(n)]

    def start(g, land, send_sems, recv_sems, base=0):
        for cp in copies(g, land, send_sems, recv_sems, base):
            cp.start()

    def finish(g, land, send_sems, recv_sems, base=0):
        for cp in copies(g, land, send_sems, recv_sems, base):
            cp.wait()

    return dict(ins=list(gs), outs=[jax.ShapeDtypeStruct((4,) + g.shape[-2:], g.dtype) for g in gs],
                alias={}, nsem=n, start=start, finish=finish)


def _exchange_scatter(ps):
    n = len(ps)

    def copies(p, land, send_sems, recv_sems, base):
        x, y, c, chips = _mesh_pos()
        return [pltpu.make_async_remote_copy(
            src_ref=p[i].at[2 * chip[0] + chip[1]], dst_ref=land[i].at[j],
            send_sem=send_sems.at[base + 3 * i + j], recv_sem=recv_sems.at[base + 3 * i + j],
            device_id=(*chip, c), device_id_type=MESH_ID) for i in range(n) for j, chip in enumerate(chips)]

    def start(p, land, send_sems, recv_sems, base=0):
        for cp in copies(p, land, send_sems, recv_sems, base):
            cp.start()

    def finish(p, land, send_sems, recv_sems, base=0):
        for cp in copies(p, land, send_sems, recv_sems, base):
            cp.wait()

    return dict(ins=list(ps), outs=[jax.ShapeDtypeStruct((3,) + p.shape[1:], p.dtype) for p in ps],
                alias={}, nsem=3 * n, start=start, finish=finish)


def _exchange_swap_result(bufs):
    n = len(bufs)

    def copies(out, send_sems, recv_sems, base, half):
        x, y, c, _ = _mesh_pos()
        h = c if half == "mine" else 1 - c
        return [pltpu.make_async_remote_copy(
            src_ref=out[i].at[h], dst_ref=out[i].at[h], send_sem=send_sems.at[base + i],
            recv_sem=recv_sems.at[base + i], device_id=(x, y, 1 - c), device_id_type=MESH_ID) for i in range(n)]

    def start(cin, out, send_sems, recv_sems, base=0):
        for cp in copies(out, send_sems, recv_sems, base, "mine"):
            cp.start()

    def finish(cin, out, send_sems, recv_sems, base=0):
        for cp in copies(out, send_sems, recv_sems, base, "theirs"):
            cp.wait_recv()
        for cp in copies(out, send_sems, recv_sems, base, "mine"):
            cp.wait_send()

    return dict(ins=list(bufs), outs=[jax.ShapeDtypeStruct(b.shape, b.dtype) for b in bufs],
                alias={i: i for i in range(n)}, nsem=n, start=start, finish=finish)


def _add_own_half(g, land, name):
    rh, cw = g.shape[-2:]
    tm = _tile(rh, max(16, (1 << 20) // (4 * cw)), 16)

    def body(g_ref, l_ref, o_ref):
        o_ref[...] = (g_ref[...] + l_ref[...]).astype(o_ref.dtype)

    mine = (pl.BlockSpec((None, None, tm, cw), lambda s, i: (s, lax.axis_index("c"), i, 0)) if g.ndim == 4
            else pl.BlockSpec((None, tm, cw), lambda s, i: (s, i, 0)))
    return pl.pallas_call(
        body, name=name, grid=(4, rh // tm),
        in_specs=[mine,
                  pl.BlockSpec((None, tm, cw), lambda s, i: (s, i, 0))],
        out_specs=pl.BlockSpec((None, tm, cw), lambda s, i: (s, i, 0)),
        out_shape=jax.ShapeDtypeStruct((4, rh, cw), BF16),
        compiler_params=_params(("parallel", "parallel")),
    )(g, land)


def _sum_pieces(part, land, name):
    _, rh, cw = land.shape
    tm = _tile(rh, max(16, (1 << 20) // (4 * cw)), 16)

    def body(p_ref, l_ref, o_ref):
        v = l_ref[...].astype(F32)
        o_ref[...] = (p_ref[...].astype(F32) + v[0]) + (v[1] + v[2])

    return pl.pallas_call(
        body, name=name, grid=(rh // tm,),
        in_specs=[pl.BlockSpec((None, tm, cw), lambda i: (_my_shard(), i, 0)),
                  pl.BlockSpec((3, tm, cw), lambda i: (0, i, 0))],
        out_specs=pl.BlockSpec((None, tm, cw), lambda i: (lax.axis_index("c"), i, 0)),
        out_shape=jax.ShapeDtypeStruct((2, rh, cw), F32),
        compiler_params=_params(("parallel",)),
    )(part, land)


def _sum8(g, name):
    def body(g_ref, o_ref):
        acc = g_ref[0]
        for k in range(1, 8):
            acc = acc + g_ref[k]
        o_ref[...] = acc

    return pl.pallas_call(body, name=name, out_shape=jax.ShapeDtypeStruct(g.shape[1:], F32))(g)


def _silu(v):
    return v * _sigmoid(v)


def _dsilu(v):
    s = _sigmoid(v)
    return s * (1.0 + v * (1.0 - s))


def _ada_fwd(cin, w, b):
    d, ns = w.shape
    tn = _tile(ns, 512)

    def body(c_ref, w_ref, b_ref, o_ref):
        o_ref[...] = _dot(_silu(c_ref[...]), w_ref[...], HI) + b_ref[...]

    return pl.pallas_call(
        body, name="ada_fwd", grid=(ns // tn,),
        in_specs=[pl.BlockSpec(cin.shape, lambda n: (0, 0)), pl.BlockSpec((d, tn), lambda n: (0, n)),
                  pl.BlockSpec((1, tn), lambda n: (0, n))],
        out_specs=pl.BlockSpec((cin.shape[0], tn), lambda n: (0, n)),
        out_shape=jax.ShapeDtypeStruct((cin.shape[0], ns), F32),
        compiler_params=_params(("parallel",)),
    )(cin, w, b)


def _ada_bwd(cin, w, dm):
    d, ns = w.shape
    tn = _tile(ns, 512)

    def body(c_ref, w_ref, d_ref, dw_ref, dc_ref):
        n = pl.program_id(0)

        @pl.when(n == 0)
        def _():
            dc_ref[...] = jnp.zeros_like(dc_ref)

        dw_ref[...] = _dot_tn(_silu(c_ref[...]), d_ref[...], HI)
        dc_ref[...] += _dot_nt(d_ref[...], w_ref[...], HI)

    return pl.pallas_call(
        body, name="ada_bwd", grid=(ns // tn,),
        in_specs=[pl.BlockSpec(cin.shape, lambda n: (0, 0)), pl.BlockSpec((d, tn), lambda n: (0, n)),
                  pl.BlockSpec((cin.shape[0], tn), lambda n: (0, n))],
        out_specs=[pl.BlockSpec((d, tn), lambda n: (0, n)), pl.BlockSpec(cin.shape, lambda n: (0, 0))],
        out_shape=[jax.ShapeDtypeStruct((d, ns), F32), jax.ShapeDtypeStruct(cin.shape, F32)],
        compiler_params=_params(("arbitrary",)),
    )(cin, w, dm)


def _bias_tables(rpb, rows):
    kh = min(WIN_H, rows)
    fold, onehot, in_win = _bias_selectors(kh)
    t = jnp.einsum("hdc,dsj->hsjc", rpb, jnp.asarray(fold), precision=HI)
    t = jnp.einsum("hsjc,cqk->hsqjk", t, jnp.asarray(onehot), precision=HI)
    t = jnp.where(jnp.asarray(in_win)[None, None, :, None, :], t, MASK_VALUE)
    return t.reshape(rpb.shape[0], kh, GRID_W, kh * GRID_W).astype(F32)


def _bias_selectors(kh):
    col = np.arange(GRID_W)
    col_start = np.clip(col - WIN_W // 2, 0, GRID_W - WIN_W)
    in_win = (col[None, :] >= col_start[:, None]) & (col[None, :] < col_start[:, None] + WIN_W)
    dc_idx = np.clip(col[None, :] - col[:, None], 1 - WIN_W, WIN_W - 1) + WIN_W - 1
    onehot = np.zeros((2 * WIN_W - 1, GRID_W, GRID_W), np.float32)
    qq, kk = np.nonzero(in_win)
    onehot[dc_idx[qq, kk], qq, kk] = 1.0
    fold = np.zeros((2 * WIN_H - 1, kh, kh), np.float32)
    for sh in range(kh):
        for j in range(kh):
            fold[j - sh + WIN_H - 1, sh, j] = 1.0
    return fold, onehot, in_win


def _mm_f32(a, b, name):
    M, K = a.shape
    N = b.shape[1]
    tm = _tile(M, 256, 8)

    def body(a_ref, b_ref, o_ref):
        o_ref[...] = _dot(a_ref[...], b_ref[...], HI)

    return pl.pallas_call(
        body, name=name, grid=(M // tm,),
        in_specs=[pl.BlockSpec((tm, K), lambda i: (i, 0)), pl.BlockSpec((K, N), lambda i: (0, 0))],
        out_specs=pl.BlockSpec((tm, N), lambda i: (i, 0)),
        out_shape=jax.ShapeDtypeStruct((M, N), F32),
        compiler_params=_params(("parallel",)),
    )(a, b)


def _bias_tables_transpose(dbias, rows):
    kh = min(WIN_H, rows)
    nh = dbias.shape[0]
    fold, onehot, _ = _bias_selectors(kh)
    ndc = onehot.shape[0]
    sel = np.zeros((GRID_W * GRID_W, LANES), np.float32)
    sel[:, :ndc] = onehot.reshape(ndc, -1).T
    x = dbias.reshape(nh, kh, GRID_W, kh, GRID_W).transpose(0, 1, 3, 2, 4).reshape(nh * kh * kh, GRID_W * GRID_W)
    z = _mm_f32(x, jnp.asarray(sel), "rpb_fold")[:, :ndc].reshape(nh, kh, kh, ndc)
    return jnp.einsum("dsj,hsjc->hdc", jnp.asarray(fold), z, precision=HI)


def _na_geometry(S):
    rows = S // GRID_W
    kh = min(WIN_H, rows)

    def row_start(r):
        return jnp.clip(r - kh // 2, 0, rows - kh)

    return rows, kh, row_start


def _na_by_head(ref, rr, lane):
    t = ref[rr * GRID_W:(rr + 1) * GRID_W, :]
    zero = jnp.zeros_like(t)
    return jnp.concatenate([jnp.where(lane < NA_HEAD_DIM, t, zero), jnp.where(lane >= NA_HEAD_DIM, t, zero)], axis=0)


def _na_pick_head(t2, lane):
    return jnp.where(lane < NA_HEAD_DIM, t2[:GRID_W], t2[GRID_W:])


def _na_scores(q_ref, k_ref, b_ref, i, nrs, nb, S, L, row_start, lane):
    qh = jnp.concatenate([_na_by_head(q_ref, rr, lane) for rr in range(nrs)], axis=0)
    sc = _dot_nt(qh, k_ref[pl.ds(S, L), :])
    starts, shifts, sb = [], [], []
    for rr in range(nrs):
        r = i * nrs + rr
        rs = row_start(r)
        starts.append(pl.multiple_of(rs * GRID_W, GRID_W))
        shifts.append(r - rs)
        bias = jnp.concatenate([b_ref[0, r - rs], b_ref[1, r - rs]], axis=0)
        sb.append(_dot_nt(qh[rr * 2 * GRID_W:(rr + 1) * 2 * GRID_W], k_ref[pl.ds(starts[-1], nb), :]) + bias)
    return qh, jnp.concatenate(sb, axis=0), sc, starts, shifts


def _na_fwd(qs, kb, vb, bias, S, L):
    T, naw = qs.shape
    rows, kh, row_start = _na_geometry(S)
    nb = kh * GRID_W
    npair = naw // LANES

    nrs = min(NA_ROWS_PER_STEP, rows)
    assert rows % nrs == 0

    def body(q_ref, k_ref, v_ref, b_ref, o_ref, lse_ref):
        i = pl.program_id(1)
        lane = lax.broadcasted_iota(jnp.int32, (GRID_W, LANES), 1)
        _, sb, sc, starts, _ = _na_scores(q_ref, k_ref, b_ref, i, nrs, nb, S, L, row_start, lane)
        m = jnp.maximum(jnp.max(sb, axis=-1, keepdims=True), jnp.max(sc, axis=-1, keepdims=True))
        pb, pc = jnp.exp(sb - m), jnp.exp(sc - m)
        l = jnp.sum(pb, axis=-1, keepdims=True) + jnp.sum(pc, axis=-1, keepdims=True)
        inv = 1.0 / l
        pb16, pc16 = (pb * inv).astype(BF16), (pc * inv).astype(BF16)
        oc = _dot(pc16, v_ref[pl.ds(S, L), :])
        lse = jnp.broadcast_to(m + jnp.log(l), oc.shape)
        for rr in range(nrs):
            two = slice(rr * 2 * GRID_W, (rr + 1) * 2 * GRID_W)
            rsl = slice(rr * GRID_W, (rr + 1) * GRID_W)
            o2 = oc[two] + _dot(pb16[two], v_ref[pl.ds(starts[rr], nb), :])
            o_ref[rsl, :] = _na_pick_head(o2, lane).astype(o_ref.dtype)
            lse_ref[rsl, :] = _na_pick_head(lse[two], lane)

    blk = pl.BlockSpec((nrs * GRID_W, LANES), lambda p, i: (i, p))
    col = pl.BlockSpec((T, LANES), lambda p, i: (0, p))
    return pl.pallas_call(
        body, name="na_fwd", grid=(npair, rows // nrs),
        in_specs=[blk, col, col, pl.BlockSpec((2, kh, GRID_W, nb), lambda p, i: (p, 0, 0, 0))],
        out_specs=[blk, blk],
        out_shape=[jax.ShapeDtypeStruct((S, naw), BF16), jax.ShapeDtypeStruct((S, naw), F32)],
        compiler_params=_params(("parallel", "arbitrary")),
    )(qs, kb, vb, bias)


def _na_bwd(qs, kb, vb, bias, do, o, lse, S, L):
    T, naw = qs.shape
    rows, kh, row_start = _na_geometry(S)
    nb = kh * GRID_W
    npair = naw // LANES

    nrs = min(NA_ROWS_PER_STEP, rows)
    assert rows % nrs == 0

    def body(q_ref, k_ref, v_ref, b_ref, do_ref, o_ref, lse_ref, dq_ref, dk_ref, dv_ref, db_ref):
        i = pl.program_id(1)

        @pl.when(i == 0)
        def _():
            dk_ref[...] = jnp.zeros_like(dk_ref)
            dv_ref[...] = jnp.zeros_like(dv_ref)
            db_ref[...] = jnp.zeros_like(db_ref)

        lane = lax.broadcasted_iota(jnp.int32, (GRID_W, LANES), 1)
        qh, sb, sc, starts, shifts = _na_scores(q_ref, k_ref, b_ref, i, nrs, nb, S, L, row_start, lane)
        doh = jnp.concatenate([_na_by_head(do_ref, rr, lane) for rr in range(nrs)], axis=0)
        o2 = jnp.concatenate([o_ref[rr * GRID_W:(rr + 1) * GRID_W, :] for rr in range(nrs) for _ in range(2)], axis=0)
        lse = jnp.concatenate([lse_ref[rr * GRID_W:(rr + 1) * GRID_W, :][:, hh * NA_HEAD_DIM:hh * NA_HEAD_DIM + 1]
                               for rr in range(nrs) for hh in range(2)], axis=0)
        pb, pc = jnp.exp(sb - lse), jnp.exp(sc - lse)
        delta = jnp.sum(doh.astype(F32) * o2.astype(F32), axis=-1, keepdims=True)
        dpb = jnp.concatenate([_dot_nt(doh[rr * 2 * GRID_W:(rr + 1) * 2 * GRID_W], v_ref[pl.ds(starts[rr], nb), :])
                               for rr in range(nrs)], axis=0)
        dsb = pb * (dpb - delta)
        dsc = pc * (_dot_nt(doh, v_ref[pl.ds(S, L), :]) - delta)
        dsb16, dsc16, pb16, pc16 = dsb.astype(BF16), dsc.astype(BF16), pb.astype(BF16), pc.astype(BF16)
        dqc = _dot(dsc16, k_ref[pl.ds(S, L), :])
        dk_ref[pl.ds(S, L), :] += _dot_tn(dsc16, qh)
        dv_ref[pl.ds(S, L), :] += _dot_tn(pc16, doh)
        for rr in range(nrs):
            two = slice(rr * 2 * GRID_W, (rr + 1) * 2 * GRID_W)
            band = pl.ds(starts[rr], nb)
            dq2 = dqc[two] + _dot(dsb16[two], k_ref[band, :])
            dq_ref[rr * GRID_W:(rr + 1) * GRID_W, :] = _na_pick_head(dq2, lane)
            dk_ref[band, :] += _dot_tn(dsb16[two], qh[two])
            dv_ref[band, :] += _dot_tn(pb16[two], doh[two])
            for hh in range(2):
                db_ref[hh, shifts[rr]] += dsb[(2 * rr + hh) * GRID_W:(2 * rr + hh + 1) * GRID_W]

    blk = pl.BlockSpec((nrs * GRID_W, LANES), lambda p, r: (r, p))
    col = pl.BlockSpec((T, LANES), lambda p, r: (0, p))
    return pl.pallas_call(
        body, name="na_bwd", grid=(npair, rows // nrs),
        in_specs=[blk, col, col, pl.BlockSpec((2, kh, GRID_W, nb), lambda p, r: (p, 0, 0, 0)), blk, blk, blk],
        out_specs=[blk, col, col, pl.BlockSpec((2, kh, GRID_W, nb), lambda p, r: (p, 0, 0, 0))],
        out_shape=[jax.ShapeDtypeStruct((S, naw), F32), jax.ShapeDtypeStruct((T, naw), F32),
                   jax.ShapeDtypeStruct((T, naw), F32), jax.ShapeDtypeStruct(bias.shape, F32)],
        compiler_params=_params(("parallel", "arbitrary")),
    )(qs, kb, vb, bias, do, o, lse)


def _hg_cols(naw, hgf, rev):
    qcol = (3 * naw) // hgf
    fcol = (3 * naw + hgf * (2 if rev else 1)) // hgf
    icol = (3 * naw + 3 * hgf) // hgf
    return qcol, fcol, icol


def _hg_chunk_order(S, L, rev):
    ncl, ncc = S // HG_CHUNK, L // HG_CHUNK
    nc = ncl + ncc

    def chunk_of(i):
        if rev:
            return nc - 1 - i
        return jnp.where(i < ncc, ncl + i, i - ncc)

    return nc, ncl, chunk_of


def _hg_gates(q, z, lb, rev):
    row = lax.broadcasted_iota(jnp.int32, (HG_CHUNK, HG_CHUNK), 0)
    colm = lax.broadcasted_iota(jnp.int32, (HG_CHUNK, HG_CHUNK), 1)
    tri = (colm >= row) if rev else (row >= colm)
    trif = tri.astype(F32)
    sig = _sigmoid(z)
    f = lb + (1.0 - lb) * sig
    lf = jnp.log(f)
    k = 1.0 - f
    cum = _dot(trif, lf, HI)
    mid = cum[HG_CHUNK // 2:HG_CHUNK // 2 + 1, :]
    last = cum[0:1, :] if rev else cum[HG_CHUNK - 1:HG_CHUNK, :]
    eq = jnp.exp(jnp.clip(cum - mid, -EXP_CLAMP, EXP_CLAMP))
    ek = jnp.exp(jnp.clip(mid - cum, -EXP_CLAMP, EXP_CLAMP))
    return tri, trif, sig, f, k, cum, last, eq, ek


def _hg_fwd(u, lbr, S, L, naw, hgf, rev):
    T = S + L
    nh = hgf // HG_DIM
    nc, ncl, chunk_of = _hg_chunk_order(S, L, rev)
    qcol, fcol, icol = _hg_cols(naw, hgf, rev)

    def step(i, q_ref, z_ref, v_ref, lb_ref, o_ref, st_ref, state):
        @pl.when(i == 0)
        def _():
            state[...] = jnp.zeros_like(state)

        q, z, v = q_ref[...], z_ref[...], v_ref[...]
        tri, _, _, _, k, cum, last, eq, ek = _hg_gates(q, z, lb_ref[...], rev)
        qe, ke = (q * eq).astype(BF16), (k * ek).astype(BF16)
        qd, kd = (q * jnp.exp(cum)).astype(BF16), (k * jnp.exp(last - cum)).astype(BF16)
        v16, el = v.astype(BF16), jnp.exp(last)
        for h in range(nh):
            sl = slice(h * HG_DIM, (h + 1) * HG_DIM)
            a = jnp.where(tri, _dot_nt(qe[:, sl], ke[:, sl]), 0.0)
            s0 = state[h]
            st_ref[h] = s0
            o_ref[:, sl] = _dot(a.astype(BF16), v16[:, sl]) + _dot_nt(qd[:, sl], s0.astype(BF16))
            state[h] = s0 * el[:, sl] + _dot_tn(v16[:, sl], kd[:, sl])

    def blk(cb):
        return pl.BlockSpec((HG_CHUNK, hgf), lambda i: (chunk_of(i), cb))

    return dict(
        step=step, nc=nc, operands=[u, u, u, lbr],
        in_specs=[blk(qcol), blk(fcol), blk(icol), pl.BlockSpec((1, hgf), lambda i: (0, 0))],
        out_specs=[pl.BlockSpec((HG_CHUNK, hgf), lambda i: (chunk_of(i), 0)),
                   pl.BlockSpec((None, nh, HG_DIM, HG_DIM), lambda i: (chunk_of(i), 0, 0, 0))],
        out_shape=[jax.ShapeDtypeStruct((T, hgf), F32), jax.ShapeDtypeStruct((nc, nh, HG_DIM, HG_DIM), F32)],
        scratch=[pltpu.VMEM((nh, HG_DIM, HG_DIM), F32)])


def _hg_both(parts, name):
    nin = [len(p["in_specs"]) for p in parts]
    nout = [len(p["out_specs"]) for p in parts]
    nscr = [len(p["scratch"]) for p in parts]

    def body(*refs):
        i = pl.program_id(0)
        ins, outs, scr = refs[:sum(nin)], refs[sum(nin):sum(nin) + sum(nout)], refs[sum(nin) + sum(nout):]
        for d, p in enumerate(parts):
            p["step"](i, *ins[sum(nin[:d]):sum(nin[:d + 1])], *outs[sum(nout[:d]):sum(nout[:d + 1])],
                      *scr[sum(nscr[:d]):sum(nscr[:d + 1])])

    res = pl.pallas_call(
        body, name=name, grid=(parts[0]["nc"],),
        in_specs=[sp for p in parts for sp in p["in_specs"]],
        out_specs=[sp for p in parts for sp in p["out_specs"]],
        out_shape=[sh for p in parts for sh in p["out_shape"]],
        scratch_shapes=[sc for p in parts for sc in p["scratch"]],
        compiler_params=_params(("arbitrary",)),
    )(*[op for p in parts for op in p["operands"]])
    return [list(res[sum(nout[:d]):sum(nout[:d + 1])]) for d in range(len(parts))]


def _hg_bwd(u, lbr, st, do, S, L, naw, hgf, rev):
    T = S + L
    nh = hgf // HG_DIM
    nc, ncl, chunk_fwd = _hg_chunk_order(S, L, rev)
    qcol, fcol, icol = _hg_cols(naw, hgf, rev)

    def chunk_of(j):
        return chunk_fwd(nc - 1 - j)

    def step(j, q_ref, z_ref, v_ref, lb_ref, st_ref, do_ref, dq_ref, dz_ref, dv_ref, dlb_ref, dstate,
             dqe_s, dke_s, dqd_s, dkd_s, dl_s):
        @pl.when(j == 0)
        def _():
            dstate[...] = jnp.zeros_like(dstate)
            dlb_ref[...] = jnp.zeros_like(dlb_ref)

        q, z, v = q_ref[...], z_ref[...], v_ref[...]
        lb = lb_ref[...]
        tri, trif, sig, f, k, cum, last, eq, ek = _hg_gates(q, z, lb, rev)
        ec, el, ekd = jnp.exp(cum), jnp.exp(last), jnp.exp(last - cum)
        qe, ke, qd, kd = q * eq, k * ek, q * ec, k * ekd
        qd16, kd16 = qd.astype(BF16), kd.astype(BF16)
        dout = jnp.where(chunk_of(j) < ncl, do_ref[...], 0.0)
        qe2, ke2, v16, dout16 = _split2(qe), _split2(ke), v.astype(BF16), dout.astype(BF16)
        for h in range(nh):
            sl = slice(h * HG_DIM, (h + 1) * HG_DIM)
            qeh, keh = [(t[0][:, sl], t[1][:, sl]) for t in (qe2, ke2)]
            a = jnp.where(tri, _dot_nt(qeh[0], keh[0]), 0.0).astype(BF16)
            s0 = st_ref[h]
            ds1 = dstate[h]
            s016, ds116 = s0.astype(BF16), ds1.astype(BF16)
            dv_ref[:, sl] = (_dot_tn(a, dout16[:, sl]) + _dot_nt(kd16[:, sl], ds116)).astype(dv_ref.dtype)
            da2 = _split2(jnp.where(tri, _dot_nt(dout16[:, sl], v16[:, sl]), 0.0))
            dqe_s[:, sl] = _dot_x3(_dot, da2, keh)
            dke_s[:, sl] = _dot_x3(_dot_tn, da2, qeh)
            dqd_s[:, sl] = _dot(dout16[:, sl], s016)
            dkd_s[:, sl] = _dot(v16[:, sl], ds116)
            dl_s[:, sl] = _colsum(ds1 * s0)
            dstate[h] = _dot_tn(dout16[:, sl], qd16[:, sl]) + ds1 * el[:, sl]
        dqe, dke, dqd, dkd = dqe_s[...], dke_s[...], dqd_s[...], dkd_s[...]
        dq_ref[...] = (dqe * eq + dqd * ec).astype(dq_ref.dtype)
        dk = dke * ek + dkd * ekd
        dcum = dqe * qe - dke * ke + dqd * qd - dkd * kd
        dlast = _colsum(dkd * kd) + el * dl_s[...]
        dlf = _dot_tn(trif, dcum, HI) + dlast
        df = dlf / f - dk
        dz_ref[...] = (df * (1.0 - lb) * sig * (1.0 - sig)).astype(dz_ref.dtype)
        dlb_ref[...] += _colsum(df * (1.0 - sig))

    def blk(cb):
        return pl.BlockSpec((HG_CHUNK, hgf), lambda j: (chunk_of(j), cb))

    oblk = pl.BlockSpec((HG_CHUNK, hgf), lambda j: (chunk_of(j), 0))
    wide = pltpu.VMEM((HG_CHUNK, hgf), F32)
    return dict(
        step=step, nc=nc, operands=[u, u, u, lbr, st, do],
        in_specs=[blk(qcol), blk(fcol), blk(icol), pl.BlockSpec((1, hgf), lambda j: (0, 0)),
                  pl.BlockSpec((None, nh, HG_DIM, HG_DIM), lambda j: (chunk_of(j), 0, 0, 0)),
                  pl.BlockSpec((HG_CHUNK, hgf), lambda j: (jnp.minimum(chunk_of(j), ncl - 1), 0))],
        out_specs=[oblk, oblk, oblk, pl.BlockSpec((1, hgf), lambda j: (0, 0))],
        out_shape=[jax.ShapeDtypeStruct((T, hgf), BF16)] * 3 + [jax.ShapeDtypeStruct((1, hgf), F32)],
        scratch=[pltpu.VMEM((nh, HG_DIM, HG_DIM), F32), wide, wide, wide, wide, pltpu.VMEM((1, hgf), F32)])


def _adamw(w, g, m, v, name):
    shape = w.shape
    if w.ndim != 2 or shape[0] % 8 or shape[1] % LANES:
        w, g, m, v = [a.reshape(1, -1) for a in (w, g, m, v)]
    r, cw = w.shape
    tm = _tile(r, max(8, (1 << 19) // cw), 8) if r % 8 == 0 else r
    c1 = 1.0 / (1.0 - ADAM_B1 ** ADAM_STEP)
    c2 = 1.0 / (1.0 - ADAM_B2 ** ADAM_STEP)

    def body(w_ref, g_ref, m_ref, v_ref, d_ref, nm_ref, nv_ref):
        gg = g_ref[...]
        nm = ADAM_B1 * m_ref[...] + (1.0 - ADAM_B1) * gg
        nv = ADAM_B2 * v_ref[...] + (1.0 - ADAM_B2) * (gg * gg)
        d_ref[...] = -ADAM_LR * ((nm * c1) / (jnp.sqrt(nv * c2) + ADAM_EPS) + ADAM_WD * w_ref[...])
        nm_ref[...] = nm
        nv_ref[...] = nv

    spec = pl.BlockSpec((tm, cw), lambda i: (i, 0))
    outs = pl.pallas_call(
        body, name=name, grid=(r // tm,), in_specs=[spec] * 4, out_specs=[spec] * 3,
        out_shape=[jax.ShapeDtypeStruct((r, cw), F32)] * 3,
        compiler_params=_params(("parallel",)),
    )(w, g, m, v)
    return [o.reshape(shape) for o in outs]


def kernel(x, c, ctx, c_ctx, w_ada, b_ada, norm1_g, w_in, na_rpb, hg_lb_logits, hg_norm_g, w_pa, w_pb, w_out, norm2_g, w_ffn_in, w_ffn_out, final_g, loss_target, m_c_ctx, m_w_ada, m_b_ada, m_norm1_g, m_w_in, m_na_rpb, m_hg_lb_logits, m_hg_norm_g, m_w_pa, m_w_pb, m_w_out, m_norm2_g, m_w_ffn_in, m_w_ffn_out, m_final_g, v_c_ctx, v_w_ada, v_b_ada, v_norm1_g, v_w_in, v_na_rpb, v_hg_lb_logits, v_hg_norm_g, v_w_pa, v_w_pb, v_w_out, v_norm2_g, v_w_ffn_in, v_w_ffn_out, v_final_g):
    xi, yi, ci = lax.axis_index("x"), lax.axis_index("y"), lax.axis_index("c")
    sidx = 2 * xi + yi
    eidx = 4 * xi + 2 * yi + ci

    S, D = x.shape[1], x.shape[2]
    L = ctx.shape[1]
    T = S + L
    naw = NA_HEADS * NA_HEAD_DIM
    hgf = HG_HEADS * HG_DIM
    inw = 3 * naw + 5 * hgf + 2 * D
    fh = w_ffn_out.shape[1] * 4
    ads = w_ada.shape[2]
    fs = hg_lb_logits.shape[2]
    rows = S // GRID_W
    tr = _tile(L, 256)
    nlat, nall = S // tr, T // tr
    assert naw == hgf and D % naw == 0 and S % tr == 0 and 2 * hgf <= D

    pack0 = jnp.concatenate([c, jnp.pad(hg_lb_logits.reshape(1, -1), ((0, 0), (0, D - 4 * fs))),
                             jnp.zeros((6, D), F32)], axis=0)
    g0 = _all_gather8(pack0).reshape(8, 8, D)
    cs = g0[:, 0]
    lbl = g0[::2, 1, :4 * fs].reshape(4, 2, 2, fs).transpose(1, 2, 0, 3).reshape(2, 2, 4 * fs)
    p_lb = jax.nn.softmax(lbl, axis=0)
    lb = p_lb[0]
    lbb = [lb[d].reshape(1, hgf) for d in range(2)]

    cin = jnp.concatenate([cs, c_ctx[None], jnp.zeros((7, D), F32)], axis=0)
    b_sh = lax.dynamic_slice(b_ada, (0, sidx * ads), (1, ads))
    modp = _ada_fwd(cin, w_ada[0], b_sh)
    modfull = _all_gather8(modp).reshape(8, 16, ads)[::2].transpose(1, 0, 2).reshape(16, 4 * ads)
    mod_e = jnp.pad(lax.dynamic_index_in_dim(modfull, eidx, 0, keepdims=False).reshape(N_MOD, D), ((0, 2), (0, 0)))
    mod_c = jnp.pad(modfull[8].reshape(N_MOD, D), ((0, 2), (0, 0)))

    names = ["w_in", "w_pa", "w_pb", "w_out", "w_ffn_in", "w_ffn_out"]
    placed = [_cast_place(w[0], "cast_" + nm)
              for w, nm in zip((w_in, w_pa, w_pb, w_out, w_ffn_in, w_ffn_out), names)]
    def shards(g):
        return g.reshape(4, 2 * g.shape[2], g.shape[3])

    x2d, ctx2d = x[0], ctx[0]

    def f_ln1(i, rv, vv):
        xl, xc = rv
        g, me, mc = vv
        isc = i >= nlat
        xt = jnp.where(isc, xc, xl)
        sh = jnp.where(isc, mc[0:1], me[0:1])
        sc = jnp.where(isc, mc[1:2], me[1:2])
        h = xt * _rms(xt) * g * (1.0 + sc) + sh
        return [h, h], []

    hb, hbt, win_all = _rowwise(f_ln1, nall, tr, [(x2d, D, 0, lambda i: jnp.minimum(i, nlat - 1)),
                                                 (ctx2d, D, 0, lambda i: jnp.maximum(i - nlat, 0))],
                                [norm1_g, mod_e, mod_c], [(D, BF16), (D, BF16, "T")], [], "ln1",
                                carry=_exchange_gather_via_neighbours(placed[:1]))
    win3 = shards(win_all)
    scale = NA_HEAD_DIM ** -0.5

    def qkv_epi(tm, tn):
        assert tn == naw
        full = pl.BlockSpec((tm, tn), lambda m, n, k: (m, n))
        one = pl.BlockSpec((tm, tn), lambda m, n, k: (m, 0))

        def fn(acc, ins):
            return [acc, acc * scale, acc, acc]

        sh16 = jax.ShapeDtypeStruct((T, naw), BF16)
        return dict(ins=[], outs=[(jax.ShapeDtypeStruct((T, inw), F32), full), (sh16, one), (sh16, one), (sh16, one)],
                    fn=fn, when=[None, lambda m, n: n == 0, lambda m, n: n == 1, lambda m, n: n == 2])

    (u, qs, kb, vb), gathered = _mm_nn(hb, win3, F32, "mm_in", carry=_exchange_gather(placed[1:]),
                                       epi=qkv_epi, tn_target=naw)
    wpa3, wpb3, wout3, wi3, wfo3 = [shards(g) for g in gathered]
    wout1 = wout3.reshape(1, D, D)
    wfo1 = wfo3.reshape(1, fh, D)

    bias =_bias_tables(na_rpb[0], rows)
    o_na, lse = _na_fwd(qs, kb, vb, bias, S, L)

    (o_f, st_f), (o_b, st_b) = _hg_both(
        [_hg_fwd(u, lbb[0], S, L, naw, hgf, False), _hg_fwd(u, lbb[1], S, L, naw, hgf, True)], "hg_fwd")

    hgn = jnp.tile(hg_norm_g, (1, HG_HEADS))
    hog_cb = (3 * naw + 4 * hgf) // hgf
    ga_cb = (3 * naw + 5 * hgf) // D
    gb_cb = ga_cb + 1

    def heads_rms(o):
        return jnp.concatenate([jnp.broadcast_to(_rms(o[:, h * HG_DIM:(h + 1) * HG_DIM]), (o.shape[0], HG_DIM))
                                for h in range(HG_HEADS)], axis=1)

    def f_readout(i, rv, vv):
        of, ob_, hog = rv
        g, = vv
        o = of + ob_
        return [o * heads_rms(o) * g * _silu(hog)], []

    ob, = _rowwise(f_readout, nlat, tr, [(o_f, hgf, 0, None), (o_b, hgf, 0, None), (u, hgf, hog_cb, None)],
                   [hgn], [(hgf, BF16)], [], "hg_readout")

    ya = _mm_nn(o_na, wpa3, BF16, "mm_pa")
    yb = _mm_nn(ob, wpb3, BF16, "mm_pb")

    def f_merge(i, rv, vv):
        ya_, yb_, ga, gb = rv
        return [_sigmoid(ga) * ya_ + _sigmoid(gb) * yb_], []

    yv, = _rowwise(f_merge, nlat, tr, [(ya, D, 0, None), (yb, D, 0, None), (u, D, ga_cb, None), (u, D, gb_cb, None)],
                   [], [(D, BF16)], [], "merge")
    z = _mm_nn(yv, wout1, F32, "mm_out")

    def f_res1(i, rv, vv):
        xt, zt = rv
        g, me = vv
        x1 = xt + me[2:3] * zt
        h = x1 * _rms(x1) * g * (1.0 + me[4:5]) + me[3:4]
        return [x1, h, h], []

    x1, h2, h2t = _rowwise(f_res1, nlat, tr, [(x2d, D, 0, None), (z, D, 0, None)], [norm2_g, mod_e],
                           [(D, F32), (D, BF16), (D, BF16, "T")], [], "res1_ln2")
    au3, sw, swt = _ffn_in_fused(h2, wi3, "mm_ffn_in")
    ff =_mm_nn(sw, wfo1, F32, "mm_ffn_out")

    fg = final_g.reshape(1, D)

    def f_final(i, rv, vv):
        x1t, ft, tg = rv
        g, me = vv
        x2 = x1t + me[5:6] * ft
        r3 = _rms(x2)
        xn = x2 * r3
        err = xn * g - tg
        dyy = err * (1.0 / D)
        dxn = dyy * g
        dx2 = r3 * (dxn - xn * jnp.mean(dxn * xn, axis=-1, keepdims=True))
        return [dx2, dx2 * me[5:6]], [_colsum(err * err), _colsum(dyy * xn), _colsum(dx2 * ft)]

    dx2, dfb, loss_cols, dfg, dg2 = _rowwise(
        f_final, nlat, tr, [(x1, D, 0, None), (ff, D, 0, None), (loss_target[0], D, 0, None)], [fg, mod_e],
        [(D, F32), (D, BF16)], [D, D, D], "final_loss")

    def dswiglu_epi(tm, tn):
        blk = pl.BlockSpec((2, tm, tn), lambda m, n, k: (0, m, n))

        def fn(d, ins):
            a, uu = ins[0][0].astype(F32), ins[0][1].astype(F32)
            return [(d * uu * _dsilu(a), d * _silu(a))]

        return dict(ins=[(au3, blk)], outs=[(jax.ShapeDtypeStruct((2, S, fh), BF16), blk)], fn=fn)

    dau3, = _mm_nt(dfb, wfo1, BF16, "mm_d_sw", epi=dswiglu_epi, tn_target=512)
    g_wfo = _mm_tn(swt, dfb, 1, "mm_dw_ffn_out", a_is_t=True).reshape(4, 2, fh // 8, D)
    dh2 = _mm_nt(dau3, wi3, F32, "mm_d_h2")
    g_wi = _mm_tn(h2t, dau3, 4, "mm_dw_ffn_in", a_is_t=True)

    def f_ln2_bwd(i, rv, vv):
        dh, x1t, dx2t, zt = rv
        g, me = vv
        r2 = _rms(x1t)
        xn = x1t * r2
        dxn = dh * g * (1.0 + me[4:5])
        dx1 = dx2t + r2 * (dxn - xn * jnp.mean(dxn * xn, axis=-1, keepdims=True))
        return ([dx1, dx1 * me[2:3]],
                [_colsum(dh), _colsum(dh * xn * g), _colsum(dh * xn * (1.0 + me[4:5])), _colsum(dx1 * zt)])

    dx1, dzb, dsh2, dsc2, dn2g, dg1 = _rowwise(
        f_ln2_bwd, nlat, tr, [(dh2, D, 0, None), (x1, D, 0, None), (dx2, D, 0, None), (z, D, 0, None)],
        [norm2_g, mod_e], [(D, F32), (D, BF16)], [D, D, D, D], "ln2_bwd")

    g_wout = _mm_tn(yv, dzb, 1, "mm_dw_out").reshape(4, 2, D // 8, D)

    def dmerge_epi(tm, tn):
        blk = pl.BlockSpec((tm, tn), lambda m, n, k: (m, n))

        def gate(cb):
            return pl.BlockSpec((tm, tn), lambda m, n, k: (m, cb * (D // tn) + n))

        def fn(d, ins):
            ya_, yb_, ga, gb = ins
            sa, sb_ = _sigmoid(ga), _sigmoid(gb)
            return [d * sa, d * sb_, d * ya_ * sa * (1.0 - sa), d * yb_ * sb_ * (1.0 - sb_)]

        return dict(ins=[(ya, blk), (yb, blk), (u, gate(ga_cb)), (u, gate(gb_cb))],
                    outs=[(jax.ShapeDtypeStruct((S, D), BF16), blk)] * 4, fn=fn)

    dya, dyb, dga, dgb = _mm_nt(dzb, wout1, BF16, "mm_d_y", epi=dmerge_epi, tn_target=512)
    d_ona = _mm_nt(dya, wpa3, BF16, "mm_d_ona")
    d_ob = _mm_nt(dyb, wpb3, F32, "mm_d_ob")
    g_wpa = _mm_tn(o_na, dya, 4, "mm_dw_pa")
    g_wpb = _mm_tn(ob, dyb, 4, "mm_dw_pb")

    def f_dreadout(i, rv, vv):
        d, of, ob_, hog = rv
        g, = vv
        o = of + ob_
        on = o * heads_rms(o)
        t = d * _silu(hog) * g
        mt = jnp.concatenate([jnp.broadcast_to(jnp.mean((t * on)[:, h * HG_DIM:(h + 1) * HG_DIM], axis=-1,
                                                        keepdims=True), (o.shape[0], HG_DIM))
                              for h in range(HG_HEADS)], axis=1)
        do_ = heads_rms(o) * (t - on * mt)
        return [do_, d * on * g * _dsilu(hog)], [_colsum(d * _silu(hog) * on)]

    do_hg, dhog, dhgn = _rowwise(
        f_dreadout, nlat, tr, [(d_ob, hgf, 0, None), (o_f, hgf, 0, None), (o_b, hgf, 0, None),
                               (u, hgf, hog_cb, None)], [hgn], [(hgf, F32), (hgf, BF16)], [hgf], "hg_readout_bwd")

    (dq_f, dz_f, dv_f, dlb_f), (dq_b, dz_b, dv_b, dlb_b) = _hg_both(
        [_hg_bwd(u, lbb[0], st_f, do_hg, S, L, naw, hgf, False), _hg_bwd(u, lbb[1], st_b, do_hg, S, L, naw, hgf, True)],
        "hg_bwd")
    dq_na, dk_na, dv_na, dbias = _na_bwd(qs, kb, vb, bias, d_ona, o_na, lse, S, L)

    ta = _tile(L, 128)
    nla, naa = S // ta, T // ta
    lat = lambda i: jnp.minimum(i, nla - 1)

    def f_assemble(i, rv, vv):
        dqn, dk, dv, dqf, dqb, dzf, dzb_, dvf, dvb, dho, dga_, dgb_ = rv
        keep = (i < nla).astype(F32)
        f32 = lambda t: t.astype(F32)
        return [jnp.concatenate([dqn * (scale * keep), dk, dv, f32(dqf) + f32(dqb), f32(dzf), f32(dzb_),
                                 f32(dvf) + f32(dvb),
                                 dho.astype(F32) * keep, dga_.astype(F32) * keep, dgb_.astype(F32) * keep],
                                axis=1)], []

    du, = _rowwise(
        f_assemble, naa, ta,
        [(dq_na, naw, 0, lat), (dk_na, naw, 0, None), (dv_na, naw, 0, None), (dq_f, hgf, 0, None),
         (dq_b, hgf, 0, None), (dz_f, hgf, 0, None), (dz_b, hgf, 0, None), (dv_f, hgf, 0, None),
         (dv_b, hgf, 0, None), (dhog, hgf, 0, lat), (dga, D, 0, lat), (dgb, D, 0, lat)],
        [], [(inw, BF16)], [], "assemble_du")

    def add_half(g, land, nm):
        return _add_own_half(g, land, "rs_add_" + nm)

    early = [g_wpa, g_wpb, g_wout, g_wi, g_wfo]
    g_win_other, lands = _mm_tn_half(hbt, du, 4, True, "mm_dw_in_other", carry=_exchange_swap_other_half(early))
    parts = [add_half(g, l, nm) for g, l, nm in zip(early, lands, names[1:])]
    g_win_mine, landed = _mm_tn_half(hbt, du, 4, False, "mm_dw_in_mine", carry=_exchange_join(
        _exchange_scatter(parts[3:4]), _exchange_swap_other_half([g_win_other])))
    piece_wi = landed[0]
    parts = [add_half(g_win_mine, landed[1], names[0])] + parts
    dh, landed = _mm_nt(du, win3, F32, "mm_d_h", carry=_exchange_scatter(parts[:4] + parts[5:]))
    pieces = landed[:4] + [piece_wi] + landed[4:]
    halves = [_sum_pieces(p, l, "rs_sum_" + nm) for p, l, nm in zip(parts, pieces, names)]
    g_win, g_wpa, g_wpb, g_wout, g_wi, g_wfo = [
        f.reshape(2 * f.shape[1], f.shape[2])
        for f in _exchange_call(_exchange_swap_result(halves), "rs_swap_result_half")]

    def f_ln1_bwd(i, rv, vv):
        dht, xt, dx1t = rv
        g, me = vv
        r1 = _rms(xt)
        xn = xt * r1
        dxn = dht * g * (1.0 + me[1:2])
        dx = dx1t + r1 * (dxn - xn * jnp.mean(dxn * xn, axis=-1, keepdims=True))
        return [dx], [_colsum(dht), _colsum(dht * xn * g), _colsum(dht * xn * (1.0 + me[1:2]))]

    grad_x, dsh1, dsc1, dn1g_l = _rowwise(
        f_ln1_bwd, nlat, tr, [(dh, D, 0, None), (x2d, D, 0, None), (dx1, D, 0, None)], [norm1_g, mod_e],
        [(D, F32)], [D, D, D], "ln1_bwd")

    def f_ln1_bwd_ctx(i, rv, vv):
        dht, xt = rv
        g, mc = vv
        xn = xt * _rms(xt)
        return [], [_colsum(dht), _colsum(dht * xn * g), _colsum(dht * xn * (1.0 + mc[1:2]))]

    ctx_rows = lambda i: i + nlat
    dsh1c, dsc1c, dn1g_c = _rowwise(
        f_ln1_bwd_ctx, nall - nlat, tr, [(dh, D, 0, ctx_rows), (ctx2d, D, 0, None)], [norm1_g, mod_c],
        [], [D, D, D], "ln1_bwd_ctx")

    drpb = _bias_tables_transpose(dbias, rows).reshape(1, -1)
    nrp = -(-drpb.shape[1] // D)
    drpb_rows = jnp.pad(drpb, ((0, 0), (0, nrp * D - drpb.shape[1]))).reshape(nrp, D)
    dlb = jnp.concatenate([dlb_f, dlb_b], axis=1)
    dhg = jnp.sum(dhgn.reshape(HG_HEADS, HG_DIM), axis=0, keepdims=True)

    def wide(v):
        return jnp.pad(v, ((0, 0), (0, D - v.shape[1])))

    pack_rows = [loss_cols, dfg, dn2g, dn1g_l + dn1g_c, dsh1, dsc1, dg1, dsh2, dsc2, dg2, dsh1c, dsc1c,
                 wide(dhg), wide(dlb), drpb_rows]
    pack = jnp.concatenate(pack_rows, axis=0)
    npk = -(-pack.shape[0] // 8) * 8
    pack = jnp.pad(pack, ((0, npk - pack.shape[0]), (0, 0)))
    gp = _all_gather8(pack).reshape(8, npk, D)
    tot = _sum8(gp, "sum_small_grads")

    loss = (0.5 / D) * jnp.sum(tot[0])
    grad_final_g = tot[1]
    grad_norm2_g = tot[2:3]
    grad_norm1_g = tot[3:4]
    grad_hg_norm_g = tot[12:13, :HG_DIM]
    dlb_tot = tot[13, :2 * hgf].reshape(2, hgf)
    grad_na_rpb = tot[14:14 + nrp].reshape(-1)[:drpb.shape[1]].reshape(na_rpb.shape)
    dlog = jnp.stack([dlb_tot * p_lb[0] * (1.0 - p_lb[0]), -dlb_tot * p_lb[0] * p_lb[1]], axis=0)
    grad_hg_lb = lax.dynamic_slice(dlog, (0, 0, sidx * fs), (2, 2, fs))

    dmod_all = gp[:, 4:10].reshape(8, N_MOD * D)
    dmod_ctx = jnp.concatenate([tot[10], tot[11], jnp.zeros((4 * D,), F32)])[None]
    dm16 = jnp.concatenate([dmod_all, dmod_ctx, jnp.zeros((7, N_MOD * D), F32)], axis=0)
    grad_b_ada = jnp.sum(dm16, axis=0, keepdims=True)
    dm_sh = lax.dynamic_slice(dm16, (0, sidx * ads), (16, ads))
    g_wada, dcin = _ada_bwd(cin, w_ada[0], dm_sh)
    gc = _all_gather8(dcin[8:16]).reshape(8, 8, D)
    grad_c_ctx = (gc[0, 0] + gc[2, 0] + gc[4, 0] + gc[6, 0]) * _dsilu(c_ctx)

    grads = {
        "c_ctx": grad_c_ctx, "w_ada": g_wada[None], "b_ada": grad_b_ada, "norm1_g": grad_norm1_g,
        "w_in": g_win[None], "na_rpb": grad_na_rpb, "hg_lb_logits": grad_hg_lb, "hg_norm_g": grad_hg_norm_g,
        "w_pa": g_wpa[None], "w_pb": g_wpb[None], "w_out": g_wout[None], "norm2_g": grad_norm2_g,
        "w_ffn_in": g_wi[None], "w_ffn_out": g_wfo[None], "final_g": grad_final_g,
    }
    weights = {
        "c_ctx": (c_ctx, m_c_ctx, v_c_ctx), "w_ada": (w_ada, m_w_ada, v_w_ada), "b_ada": (b_ada, m_b_ada, v_b_ada),
        "norm1_g": (norm1_g, m_norm1_g, v_norm1_g), "w_in": (w_in, m_w_in, v_w_in),
        "na_rpb": (na_rpb, m_na_rpb, v_na_rpb), "hg_lb_logits": (hg_lb_logits, m_hg_lb_logits, v_hg_lb_logits),
        "hg_norm_g": (hg_norm_g, m_hg_norm_g, v_hg_norm_g), "w_pa": (w_pa, m_w_pa, v_w_pa),
        "w_pb": (w_pb, m_w_pb, v_w_pb), "w_out": (w_out, m_w_out, v_w_out),
        "norm2_g": (norm2_g, m_norm2_g, v_norm2_g), "w_ffn_in": (w_ffn_in, m_w_ffn_in, v_w_ffn_in),
        "w_ffn_out": (w_ffn_out, m_w_ffn_out, v_w_ffn_out), "final_g": (final_g, m_final_g, v_final_g),
    }
    order = list(weights)
    deltas, new_ms, new_vs = [], [], []
    for nm in order:
        w, m, v = weights[nm]
        g = grads[nm].reshape(w.shape)
        grads[nm] = g
        if w.ndim == 3 and w.shape[0] == 1:
            d_, m_, v_ = _adamw(w[0], g[0], m[0], v[0], "adamw_" + nm)
            d_, m_, v_ = d_[None], m_[None], v_[None]
        else:
            d_, m_, v_ = _adamw(w, g, m, v, "adamw_" + nm)
        deltas.append(d_)
        new_ms.append(m_)
        new_vs.append(v_)

    return (loss, grad_x[None], *[grads[nm] for nm in order], *deltas, *new_ms, *new_vs)
```
